```python
import math
import jax, jax.numpy as jnp
from jax import lax
import numpy as np

D_MODEL = 1024
BATCH = 8
SEQ = 2048
DEPTH = 1

CHUNK = 64
D_MIX = D_MODEL
RET_WIDTH = D_MIX // 2
CONV_WIDTH = D_MIX - RET_WIDTH
RET_HEADS = 8
RET_HEAD_DIM = RET_WIDTH // RET_HEADS
CONV_GROUPS = 8
CONV_GROUP_DIM = CONV_WIDTH // CONV_GROUPS
CONV_K = 3
ROPE_BASE = 10000.0
N_GROUPS = 4
EXPERTS_PER_GROUP = 8
TOP_K = 2
D_EXPERT = D_MODEL // 2
EPS = 1e-6
GN_EPS = 1e-5
IN_COLS = 4 * RET_WIDTH + 3 * CONV_WIDTH

kernel_name = "hybrid_retention_shortconv_hmoe_adaln"


def rms_norm(x, g):
    xf = x.astype(jnp.float32)
    y = xf * lax.rsqrt(jnp.mean(xf * xf, axis=-1, keepdims=True) + EPS)
    return (y * g.astype(jnp.float32)).astype(x.dtype)


def rotary(positions, dim, dtype):
    inv_freq = ROPE_BASE ** (-jnp.arange(0, dim, 2, dtype=jnp.float32) / dim)
    ang = positions.astype(jnp.float32)[..., None] * inv_freq
    return jnp.cos(ang)[:, :, None, :].astype(dtype), jnp.sin(ang)[:, :, None, :].astype(dtype)


def apply_rotary(t, cos, sin):
    t1, t2 = jnp.split(t, 2, axis=-1)
    return jnp.concatenate([t1 * cos - t2 * sin, t2 * cos + t1 * sin], axis=-1)


def retention(q, k, v, positions):
    B, S, H, dh = q.shape
    nc = S // CHUNK
    dt = q.dtype
    cos, sin = rotary(positions, dh, dt)
    q = apply_rotary(q, cos, sin)
    k = apply_rotary(k, cos, sin) * (dh ** -0.5)

    def to_chunks(t):
        return t.reshape(B, nc, CHUNK, H, dh).transpose(0, 3, 1, 2, 4)

    qc, kc, vc = to_chunks(q), to_chunks(k), to_chunks(v)
    log_gamma = jnp.log1p(-jnp.exp2(-5.0 - jnp.arange(H, dtype=jnp.float32)))
    idx = jnp.arange(CHUNK, dtype=jnp.float32)
    dist = jnp.abs(idx[:, None] - idx[None, :])
    d_intra = jnp.exp(log_gamma[:, None, None] * dist).astype(dt)
    k_decay = jnp.exp(log_gamma[:, None] * (CHUNK - 1 - idx)).astype(dt)
    q_decay = jnp.exp(log_gamma[:, None] * (idx + 1.0)).astype(dt)
    chunk_decay = jnp.exp(log_gamma * CHUNK).astype(dt)

    scores = jnp.einsum('bhncd,bhnmd->bhncm', qc, kc) * d_intra[None, :, None]
    y_intra = jnp.einsum('bhncm,bhnmd->bhncd', scores, vc)

    kv = jnp.einsum('bhnmd,bhnme->bhnde', kc * k_decay[None, :, None, :, None], vc)

    def step(state, kv_i):
        return state * chunk_decay[None, :, None, None] + kv_i, state

    _, s_prev = lax.scan(step, jnp.zeros((B, H, dh, dh), kv.dtype), jnp.moveaxis(kv, 2, 0))
    s_prev = jnp.moveaxis(s_prev, 0, 2)
    y_inter = jnp.einsum('bhncd,bhnde->bhnce', qc * q_decay[None, :, None, :, None], s_prev)

    y = y_intra + y_inter
    return y.transpose(0, 2, 3, 1, 4).reshape(B, S, H, dh)


def group_norm_heads(y):
    yf = y.astype(jnp.float32)
    mu = jnp.mean(yf, axis=-1, keepdims=True)
    var = jnp.mean(jnp.square(yf - mu), axis=-1, keepdims=True)
    return ((yf - mu) * lax.rsqrt(var + GN_EPS)).astype(y.dtype)


def mixer(h, positions, w_in, conv_w, conv_b, beta_ret, beta_conv, w_out):
    B, S, _ = h.shape
    proj = h @ w_in
    q, k, v, g, b_gate, c_gate, h_conv = jnp.split(
        proj, np.cumsum([RET_WIDTH] * 4 + [CONV_WIDTH] * 2).tolist(), axis=-1)

    shp = (B, S, RET_HEADS, RET_HEAD_DIM)
    y_ret = retention(q.reshape(shp), k.reshape(shp), v.reshape(shp), positions)
    y_ret = group_norm_heads(y_ret).reshape(B, S, RET_WIDTH)
    y_ret = jax.nn.silu(g) * y_ret * beta_ret

    u = c_gate * h_conv
    conv = lax.conv_general_dilated(
        u, conv_w[:, None, :], window_strides=(1,), padding=[(CONV_K - 1, 0)],
        dimension_numbers=('NWC', 'WIO', 'NWC'), feature_group_count=CONV_WIDTH) + conv_b
    y_conv = (b_gate * conv).reshape(B, S, CONV_GROUPS, CONV_GROUP_DIM)
    y_conv = rms_norm(y_conv, jnp.ones((CONV_GROUP_DIM,), y_conv.dtype)).reshape(B, S, CONV_WIDTH)
    y_conv = y_conv * beta_conv

    return jnp.concatenate([y_ret, y_conv], axis=-1) @ w_out


def hier_moe(h, wg, bg, we, be, w_gate, w_up, w_down):
    B, S, D = h.shape
    t = h.reshape(-1, D)
    n = t.shape[0]
    g_prob = jax.nn.softmax((t @ wg + bg).astype(jnp.float32), axis=-1)
    g_top, g_idx = lax.top_k(g_prob, 1)
    e_logits = (t @ we + be).astype(jnp.float32).reshape(n, N_GROUPS, EXPERTS_PER_GROUP)
    e_sel = jnp.take_along_axis(e_logits, g_idx[:, :, None], axis=1)[:, 0]
    e_prob = jax.nn.softmax(e_sel, axis=-1)
    e_top, e_idx = lax.top_k(e_prob, TOP_K)
    e_w = e_top / jnp.sum(e_top, axis=-1, keepdims=True) * g_top
    within = jnp.sum(jax.nn.one_hot(e_idx, EXPERTS_PER_GROUP, dtype=jnp.float32) * e_w[..., None], axis=1)
    combine = (jax.nn.one_hot(g_idx[:, 0], N_GROUPS, dtype=jnp.float32)[:, :, None]
               * within[:, None, :]).astype(t.dtype)
    out = jnp.zeros_like(t)
    for gi in range(N_GROUPS):
        a = jnp.einsum('nd,edf->nef', t, w_gate[gi])
        u = jnp.einsum('nd,edf->nef', t, w_up[gi])
        hid = jax.nn.silu(a) * u * combine[:, gi, :, None]
        out = out + jnp.einsum('nef,efd->nd', hid, w_down[gi])
    return out.reshape(B, S, D)


def setup_inputs(seed: int = 0) -> dict:
    key = jax.random.key(seed)
    ks = jax.random.split(key, 24)
    f32 = jnp.float32
    nrm = lambda k, shape, s: jax.random.normal(k, shape, f32) * s
    x = jax.random.normal(ks[0], (BATCH, SEQ, D_MODEL), f32)
    c = jax.random.normal(ks[1], (BATCH, D_MODEL), f32)
    offset = jax.random.randint(ks[2], (BATCH, 1), 0, 4096, dtype=jnp.int32)
    positions = (offset + jnp.arange(SEQ, dtype=jnp.int32)[None, :]).astype(jnp.int32)
    ge = N_GROUPS * EXPERTS_PER_GROUP
    return {
        "x": x,
        "c": c,
        "positions": positions,
        "ada_w": nrm(ks[3], (DEPTH, D_MODEL, 6 * D_MODEL), 0.5 * D_MODEL ** -0.5),
        "ada_b": nrm(ks[4], (DEPTH, 6 * D_MODEL), 0.02),
        "norm_mix_g": 1.0 + nrm(ks[5], (DEPTH, D_MODEL), 0.02),
        "norm_ffn_g": 1.0 + nrm(ks[6], (DEPTH, D_MODEL), 0.02),
        "w_in": nrm(ks[7], (DEPTH, D_MODEL, IN_COLS), D_MODEL ** -0.5),
        "conv_w": nrm(ks[8], (DEPTH, CONV_K, CONV_WIDTH), CONV_K ** -0.5),
        "conv_b": nrm(ks[9], (DEPTH, CONV_WIDTH), 0.02),
        "beta_ret": 1.0 + nrm(ks[10], (DEPTH, RET_WIDTH), 0.02),
        "beta_conv": 1.0 + nrm(ks[11], (DEPTH, CONV_WIDTH), 0.02),
        "w_out": nrm(ks[12], (DEPTH, D_MIX, D_MODEL), D_MIX ** -0.5),
        "router_group_w": nrm(ks[13], (DEPTH, D_MODEL, N_GROUPS), D_MODEL ** -0.5),
        "router_group_b": nrm(ks[14], (DEPTH, N_GROUPS), 0.01),
        "router_expert_w": nrm(ks[15], (DEPTH, D_MODEL, ge), D_MODEL ** -0.5),
        "router_expert_b": nrm(ks[16], (DEPTH, ge), 0.01),
        "expert_w_gate": nrm(ks[17], (DEPTH, N_GROUPS, EXPERTS_PER_GROUP, D_MODEL, D_EXPERT), D_MODEL ** -0.5),
        "expert_w_up": nrm(ks[18], (DEPTH, N_GROUPS, EXPERTS_PER_GROUP, D_MODEL, D_EXPERT), D_MODEL ** -0.5),
        "expert_w_down": nrm(ks[19], (DEPTH, N_GROUPS, EXPERTS_PER_GROUP, D_EXPERT, D_MODEL), D_EXPERT ** -0.5),
        "norm_final_g": 1.0 + nrm(ks[20], (D_MODEL,), 0.02),
    }


def reference(x, c, positions, ada_w, ada_b, norm_mix_g, norm_ffn_g, w_in, conv_w, conv_b,
              beta_ret, beta_conv, w_out, router_group_w, router_group_b, router_expert_w,
              router_expert_b, expert_w_gate, expert_w_up, expert_w_down, norm_final_g):
    for l in range(DEPTH):
        mod = jax.nn.silu(c) @ ada_w[l] + ada_b[l]
        shift_m, scale_m, gate_m, shift_f, scale_f, gate_f = jnp.split(mod[:, None, :], 6, axis=-1)

        h = rms_norm(x, norm_mix_g[l]) * (1.0 + scale_m) + shift_m
        x = x + gate_m * mixer(h, positions, w_in[l], conv_w[l], conv_b[l],
                               beta_ret[l], beta_conv[l], w_out[l])

        h = rms_norm(x, norm_ffn_g[l]) * (1.0 + scale_f) + shift_f
        x = x + gate_f * hier_moe(h, router_group_w[l], router_group_b[l], router_expert_w[l],
                                  router_expert_b[l], expert_w_gate[l], expert_w_up[l],
                                  expert_w_down[l])
    return rms_norm(x, norm_final_g)
```

```python
import functools

import jax
import jax.numpy as jnp
import numpy as np
from jax import lax
from jax.experimental import pallas as pl
from jax.experimental.pallas import tpu as pltpu

F32 = jnp.float32
BF16 = jnp.bfloat16

CHUNK = 64
RET_HEADS = 8
HEAD_DIM = 64
CONV_GROUP_DIM = 64
ROPE_BASE = 10000.0
N_GROUPS = 4
EXPERTS_PER_GROUP = 8
N_EXPERTS = N_GROUPS * EXPERTS_PER_GROUP
EPS = 1e-6
GN_EPS = 1e-5

LANES = 128
SUBLANES = 8
VMEM_LIMIT_BYTES = 56 * 1024 * 1024

SEQ_TILE = 256
EXPERT_TILE = 256
COMBINE_TILE = 256
ROUTER_LANES = LANES


def _silu(v):
    return v * (1.0 / (1.0 + jnp.exp(-v)))


def _adaln_kernel(c_ref, w_ref, b_ref, o_ref):
    s = _silu(c_ref[...])
    o_ref[...] = jnp.dot(s, w_ref[...], precision=lax.Precision.HIGHEST,
                         preferred_element_type=F32) + b_ref[...]


def _adaln(c, w, b):
    bsz, d = c.shape
    n = w.shape[1]
    tn = 1024
    return pl.pallas_call(
        _adaln_kernel,
        out_shape=jax.ShapeDtypeStruct((bsz, n), F32),
        grid=(n // tn,),
        in_specs=[pl.BlockSpec((bsz, d), lambda j: (0, 0)),
                  pl.BlockSpec((d, tn), lambda j: (0, j)),
                  pl.BlockSpec((1, tn), lambda j: (0, j))],
        out_specs=pl.BlockSpec((bsz, tn), lambda j: (0, j)),
        name="adaln",
    )(c, w, b.reshape(1, n))


def _rope_kernel(pos_ref, invf_ref, cos_ref, sin_ref):
    ang = pos_ref[...].astype(F32) * invf_ref[...]
    cos_ref[...] = jnp.cos(ang)
    sin_ref[...] = jnp.sin(ang)


def _rope_tables(positions):
    n = positions.size
    half = HEAD_DIM // 2
    inv_freq = ROPE_BASE ** (-jnp.arange(0, HEAD_DIM, 2, dtype=F32) / HEAD_DIM)
    per_row = LANES // half
    rows = n // per_row
    pos_rep = jnp.repeat(positions.reshape(-1), half).reshape(rows, LANES)
    invf = jnp.tile(inv_freq, per_row).reshape(1, LANES)
    tr = 1024
    cos, sin = pl.pallas_call(
        _rope_kernel,
        out_shape=(jax.ShapeDtypeStruct((rows, LANES), F32),) * 2,
        grid=(rows // tr,),
        in_specs=[pl.BlockSpec((tr, LANES), lambda i: (i, 0)),
                  pl.BlockSpec((1, LANES), lambda i: (0, 0))],
        out_specs=(pl.BlockSpec((tr, LANES), lambda i: (i, 0)),) * 2,
        name="rope_table",
    )(pos_rep, invf)
    cos = cos.reshape(n, half)
    sin = sin.reshape(n, half)
    cos128 = jnp.concatenate([cos, cos, cos, cos], axis=1)
    sin128 = jnp.concatenate([-sin, sin, -sin, sin], axis=1)
    return cos128, sin128


def _mixer_kernel(lg_ref, x_ref, mod_ref, cos_ref, sin_ref, gmix_ref, win_ref, convw_ref,
                  convb_ref, bret_ref, bconv_ref, wout_ref, gffn_ref, wr_ref, br_ref, lgl_ref,
                  blk_ref, x1_ref, h2_ref, slab_ref, state_ref, ubuf_ref):
    L = SEQ_TILE
    W = RET_HEADS * HEAD_DIM
    s = pl.program_id(1)

    @pl.when(s == 0)
    def _():
        state_ref[...] = jnp.zeros_like(state_ref)
        ubuf_ref[0:SUBLANES, :] = jnp.zeros((SUBLANES, W), F32)

    x = x_ref[0]
    mod = mod_ref[0]
    shift_m, scale_m, gate_m = mod[0:1], mod[1:2], mod[2:3]
    shift_f, scale_f = mod[3:4], mod[4:5]

    ms = jnp.mean(x * x, axis=-1, keepdims=True)
    h = x * lax.rsqrt(ms + EPS) * gmix_ref[...]
    h = h * (1.0 + scale_m) + shift_m
    hb = h.astype(BF16)

    def proj(i):
        return jnp.dot(hb, win_ref[:, i * W:(i + 1) * W], preferred_element_type=F32)

    cos = jnp.concatenate([cos_ref[...]] * 4, axis=1)
    sin = jnp.concatenate([sin_ref[...]] * 4, axis=1)
    lane_w = lax.broadcasted_iota(jnp.int32, (L, W), 1)
    first_half = (lane_w & (HEAD_DIM - 1)) < (HEAD_DIM // 2)

    def rot(t):
        partner = jnp.where(first_half, pltpu.roll(t, W - HEAD_DIM // 2, 1),
                            pltpu.roll(t, HEAD_DIM // 2, 1))
        return t * cos + partner * sin

    q = rot(proj(0))
    k = rot(proj(1)) * (HEAD_DIM ** -0.5)
    v = proj(2)
    vb = v.astype(BF16)
    kb = k.astype(BF16)

    lgl = lgl_ref[...]
    rowf = lax.broadcasted_iota(jnp.int32, (L, W), 0).astype(F32)
    qd = q * jnp.exp(lgl * (rowf + 1.0))
    kd = k * jnp.exp(lgl * (float(L - 1) - rowf))
    blk = blk_ref[...]
    st = state_ref[...]
    y_inter = jnp.dot(qd.astype(BF16), st.astype(BF16), preferred_element_type=F32)
    kv = lax.dot_general(kd.astype(BF16), vb, (((0,), (0,)), ((), ())),
                         preferred_element_type=F32)
    state_ref[...] = st * jnp.exp(lgl * float(L)) + kv * blk.astype(F32)

    ii = lax.broadcasted_iota(jnp.int32, (L, L), 0)
    jj = lax.broadcasted_iota(jnp.int32, (L, L), 1)
    dist = jnp.abs(ii - jj).astype(F32)
    allowed = (jj // CHUNK) <= (ii // CHUNK)
    lane_p = lax.broadcasted_iota(jnp.int32, (L, LANES), 1)
    lo_head = lane_p < HEAD_DIM
    pairs = []
    for p in range(RET_HEADS // 2):
        sl = slice(p * LANES, (p + 1) * LANES)
        qp, kp, vp = q[:, sl], kb[:, sl], vb[:, sl]
        ys = []
        for hh in range(2):
            head = 2 * p + hh
            keep = lo_head if hh == 0 else jnp.logical_not(lo_head)
            qh = jnp.where(keep, qp, 0.0).astype(BF16)
            sc = lax.dot_general(qh, kp, (((1,), (1,)), ((), ())),
                                 preferred_element_type=F32)
            decay = jnp.where(allowed, jnp.exp(lg_ref[head] * dist), 0.0)
            ys.append(jnp.dot((sc * decay).astype(BF16), vp, preferred_element_type=F32))
        pairs.append(jnp.where(lo_head, ys[0], ys[1]))
    y = jnp.concatenate(pairs, axis=1) + y_inter

    inv_hd = 1.0 / HEAD_DIM
    mu = jnp.dot(y.astype(BF16), blk, preferred_element_type=F32) * inv_hd
    d = y - mu
    var = jnp.dot((d * d).astype(BF16), blk, preferred_element_type=F32) * inv_hd
    g = proj(3)
    y_ret = _silu(g) * (d * lax.rsqrt(var + GN_EPS)) * bret_ref[...]

    b_gate = proj(4)
    u = proj(5) * proj(6)
    ubuf_ref[SUBLANES:SUBLANES + L, :] = u
    u1 = ubuf_ref[SUBLANES - 1:SUBLANES - 1 + L, :]
    u2 = ubuf_ref[SUBLANES - 2:SUBLANES - 2 + L, :]
    ubuf_ref[0:SUBLANES, :] = ubuf_ref[L:L + SUBLANES, :]
    cw = convw_ref[...]
    conv = u2 * cw[0:1] + u1 * cw[1:2] + u * cw[2:3] + convb_ref[...]
    yc = b_gate * conv
    msc = jnp.dot((yc * yc).astype(BF16), blk, preferred_element_type=F32) * (1.0 / CONV_GROUP_DIM)
    y_conv = yc * lax.rsqrt(msc + EPS) * bconv_ref[...]

    mix = (jnp.dot(y_ret.astype(BF16), wout_ref[0:W, :], preferred_element_type=F32)
           + jnp.dot(y_conv.astype(BF16), wout_ref[W:2 * W, :], preferred_element_type=F32))
    x1 = x + gate_m * mix
    x1_ref[0] = x1

    ms2 = jnp.mean(x1 * x1, axis=-1, keepdims=True)
    h2 = x1 * lax.rsqrt(ms2 + EPS) * gffn_ref[...]
    h2 = h2 * (1.0 + scale_f) + shift_f
    h2_ref[0] = h2

    hi = h2.astype(BF16)
    lo = (h2 - hi.astype(F32)).astype(BF16)
    w2 = wr_ref[...]
    parts = (jnp.dot(hi, w2, preferred_element_type=F32)
             + jnp.dot(lo, w2, preferred_element_type=F32))
    logits = parts[:, :ROUTER_LANES] + parts[:, ROUTER_LANES:] + br_ref[...]

    lane = lax.broadcasted_iota(jnp.int32, (L, ROUTER_LANES), 1).astype(F32)
    neg_inf = F32(-jnp.inf)
    big = F32(1e9)
    gmask = lane < float(N_GROUPS)
    lgm = jnp.where(gmask, logits, neg_inf)
    gexp = jnp.exp(lgm - jnp.max(lgm, axis=-1, keepdims=True))
    gp = gexp / jnp.sum(gexp, axis=-1, keepdims=True)
    g_top = jnp.max(gp, axis=-1, keepdims=True)
    g_idx = jnp.min(jnp.where(gmask & (gp == g_top), lane, big), axis=-1, keepdims=True)

    e_lo = float(N_GROUPS) + float(EXPERTS_PER_GROUP) * g_idx
    emask = (lane >= e_lo) & (lane < e_lo + float(EXPERTS_PER_GROUP))
    lem = jnp.where(emask, logits, neg_inf)
    eexp = jnp.exp(lem - jnp.max(lem, axis=-1, keepdims=True))
    ep = eexp / jnp.sum(eexp, axis=-1, keepdims=True)
    p1 = jnp.max(jnp.where(emask, ep, -1.0), axis=-1, keepdims=True)
    i1 = jnp.min(jnp.where(emask & (ep == p1), lane, big), axis=-1, keepdims=True)
    m2 = emask & (lane != i1)
    p2 = jnp.max(jnp.where(m2, ep, -1.0), axis=-1, keepdims=True)
    i2 = jnp.min(jnp.where(m2 & (ep == p2), lane, big), axis=-1, keepdims=True)
    den = p1 + p2
    w1 = p1 / den * g_top
    w2_ = p2 / den * g_top
    slab = jnp.where(lane == 0.0, i1 - float(N_GROUPS),
                     jnp.where(lane == 1.0, i2 - float(N_GROUPS),
                               jnp.where(lane == 2.0, w1,
                                         jnp.where(lane == 3.0, w2_, 0.0))))
    slab_ref[...] = slab


def _mixer(x, mod, cos128, sin128, gmix, win_b, convw, convb, bret, bconv, wout_b, gffn,
           wr2, br, lg, lgl, blk):
    bsz, seq, d = x.shape
    L = SEQ_TILE
    ns = seq // L
    W = RET_HEADS * HEAD_DIM
    n = bsz * seq
    const2 = lambda b, s: (0, 0)
    in_specs = [
        pl.BlockSpec(memory_space=pltpu.SMEM),
        pl.BlockSpec((1, L, d), lambda b, s: (b, s, 0)),
        pl.BlockSpec((1, 6, d), lambda b, s: (b, 0, 0)),
        pl.BlockSpec((L, LANES), lambda b, s: (b * ns + s, 0)),
        pl.BlockSpec((L, LANES), lambda b, s: (b * ns + s, 0)),
        pl.BlockSpec((1, d), const2),
        pl.BlockSpec(win_b.shape, const2),
        pl.BlockSpec(convw.shape, const2),
        pl.BlockSpec((1, W), const2),
        pl.BlockSpec((1, W), const2),
        pl.BlockSpec((1, W), const2),
        pl.BlockSpec(wout_b.shape, const2),
        pl.BlockSpec((1, d), const2),
        pl.BlockSpec(wr2.shape, const2),
        pl.BlockSpec((1, ROUTER_LANES), const2),
        pl.BlockSpec((1, W), const2),
        pl.BlockSpec((W, W), const2),
    ]
    out_shape = (jax.ShapeDtypeStruct((bsz, seq, d), F32),
                 jax.ShapeDtypeStruct((bsz, seq, d), F32),
                 jax.ShapeDtypeStruct((n, ROUTER_LANES), F32))
    out_specs = (pl.BlockSpec((1, L, d), lambda b, s: (b, s, 0)),
                 pl.BlockSpec((1, L, d), lambda b, s: (b, s, 0)),
                 pl.BlockSpec((L, ROUTER_LANES), lambda b, s: (b * ns + s, 0)))
    return pl.pallas_call(
        _mixer_kernel,
        out_shape=out_shape,
        grid=(bsz, ns),
        in_specs=in_specs,
        out_specs=out_specs,
        scratch_shapes=[pltpu.VMEM((W, W), F32),
                        pltpu.VMEM((L + 2 * SUBLANES, W), F32)],
        compiler_params=pltpu.CompilerParams(
            dimension_semantics=("arbitrary", "arbitrary"),
            vmem_limit_bytes=VMEM_LIMIT_BYTES),
        name="mixer",
    )(lg, x, mod, cos128, sin128, gmix, win_b, convw, convb, bret, bconv, wout_b, gffn,
      wr2, br, lgl, blk)


def _row_gather_copy(src_hbm, row, dst, dst_row, sem):
    return pltpu.make_async_copy(src_hbm.at[pl.ds(row, 1), :], dst.at[pl.ds(dst_row, 1), :], sem)


def _expert_kernel(te_ref, nused_ref, rows_ref, h2_hbm, wg_ref, wu_ref, wd_ref, y_ref,
                   xbuf, sem, wgb, wub, wdb):
    tm = EXPERT_TILE
    i = pl.program_id(0)
    n_used = nused_ref[0]

    def issue(tile, slot):
        base = tile * tm

        def body(r, carry):
            _row_gather_copy(h2_hbm, rows_ref[base + r], xbuf.at[slot], r, sem.at[slot]).start()
            return carry

        lax.fori_loop(0, tm, body, 0)

    @pl.when(i == 0)
    def _():
        issue(0, 0)

    @pl.when(i + 1 < n_used)
    def _():
        issue(i + 1, (i + 1) % 2)

    @pl.when(i < n_used)
    def _():
        slot = i % 2
        pltpu.make_async_copy(h2_hbm.at[pl.ds(0, tm), :], xbuf.at[slot], sem.at[slot]).wait()
        e = te_ref[i]
        e_prev = te_ref[jnp.maximum(i - 1, 0)]

        @pl.when((i == 0) | (e != e_prev))
        def _():
            wgb[...] = wg_ref[0].astype(BF16)
            wub[...] = wu_ref[0].astype(BF16)
            wdb[...] = wd_ref[0].astype(BF16)

        xb = xbuf[slot].astype(BF16)
        a = jnp.dot(xb, wgb[...], preferred_element_type=F32)
        u = jnp.dot(xb, wub[...], preferred_element_type=F32)
        hid = (_silu(a) * u).astype(BF16)
        y_ref[...] = jnp.dot(hid, wdb[...], preferred_element_type=F32)

    @pl.when(i >= n_used)
    def _():
        y_ref[...] = jnp.zeros_like(y_ref)


def _experts(tile_expert, n_used, row_ids, h2, wg, wu, wd):
    n, d = h2.shape
    tm = EXPERT_TILE
    p_rows = row_ids.shape[0]
    n_tiles = p_rows // tm
    de = wg.shape[-1]
    grid_spec = pltpu.PrefetchScalarGridSpec(
        num_scalar_prefetch=3,
        grid=(n_tiles,),
        in_specs=[
            pl.BlockSpec(memory_space=pl.ANY),
            pl.BlockSpec((1, d, de), lambda i, te, nu, rows: (te[i], 0, 0)),
            pl.BlockSpec((1, d, de), lambda i, te, nu, rows: (te[i], 0, 0)),
            pl.BlockSpec((1, de, d), lambda i, te, nu, rows: (te[i], 0, 0)),
        ],
        out_specs=pl.BlockSpec((tm, d), lambda i, te, nu, rows: (i, 0)),
        scratch_shapes=[pltpu.VMEM((2, tm, d), F32),
                        pltpu.SemaphoreType.DMA((2,)),
                        pltpu.VMEM((d, de), BF16),
                        pltpu.VMEM((d, de), BF16),
                        pltpu.VMEM((de, d), BF16)],
    )
    return pl.pallas_call(
        _expert_kernel,
        out_shape=jax.ShapeDtypeStruct((p_rows, d), F32),
        grid_spec=grid_spec,
        compiler_params=pltpu.CompilerParams(
            dimension_semantics=("arbitrary",),
            vmem_limit_bytes=VMEM_LIMIT_BYTES),
        name="experts",
    )(tile_expert, n_used, row_ids, h2, wg, wu, wd)


def _combine_kernel(p0_ref, p1_ref, y_hbm, x1_ref, slab_ref, mod_ref, gfin_ref, o_ref,
                    ybuf, sem):
    tm = COMBINE_TILE
    i = pl.program_id(0)
    nt = pl.num_programs(0)

    def issue(tile, slot):
        base = tile * tm

        def body(r, carry):
            _row_gather_copy(y_hbm, p0_ref[base + r], ybuf.at[slot, 0], r, sem.at[slot]).start()
            _row_gather_copy(y_hbm, p1_ref[base + r], ybuf.at[slot, 1], r, sem.at[slot]).start()
            return carry

        lax.fori_loop(0, tm, body, 0)

    @pl.when(i == 0)
    def _():
        issue(0, 0)

    @pl.when(i + 1 < nt)
    def _():
        issue(i + 1, (i + 1) % 2)

    slot = i % 2
    for j in range(2):
        pltpu.make_async_copy(y_hbm.at[pl.ds(0, tm), :], ybuf.at[slot, j], sem.at[slot]).wait()

    slab = slab_ref[...]
    w0 = slab[:, 2:3]
    w1 = slab[:, 3:4]
    gate_f = mod_ref[0][5:6]
    moe = w0 * ybuf[slot, 0] + w1 * ybuf[slot, 1]
    xo = x1_ref[...] + gate_f * moe
    ms = jnp.mean(xo * xo, axis=-1, keepdims=True)
    o_ref[...] = xo * lax.rsqrt(ms + EPS) * gfin_ref[...]


def _combine(p0, p1, y, x1, slab, mod, gfin, seq):
    n, d = x1.shape
    tm = COMBINE_TILE
    tiles_per_seq = seq // tm
    grid_spec = pltpu.PrefetchScalarGridSpec(
        num_scalar_prefetch=2,
        grid=(n // tm,),
        in_specs=[
            pl.BlockSpec(memory_space=pl.ANY),
            pl.BlockSpec((tm, d), lambda i, a, b: (i, 0)),
            pl.BlockSpec((tm, ROUTER_LANES), lambda i, a, b: (i, 0)),
            pl.BlockSpec((1, 6, d), lambda i, a, b: (i // tiles_per_seq, 0, 0)),
            pl.BlockSpec((1, d), lambda i, a, b: (0, 0)),
        ],
        out_specs=pl.BlockSpec((tm, d), lambda i, a, b: (i, 0)),
        scratch_shapes=[pltpu.VMEM((2, 2, tm, d), F32),
                        pltpu.SemaphoreType.DMA((2,))],
    )
    return pl.pallas_call(
        _combine_kernel,
        out_shape=jax.ShapeDtypeStruct((n, d), F32),
        grid_spec=grid_spec,
        compiler_params=pltpu.CompilerParams(
            dimension_semantics=("arbitrary",),
            vmem_limit_bytes=VMEM_LIMIT_BYTES),
        name="combine",
    )(p0, p1, y, x1, slab, mod, gfin)


def _routing_plan(slab, n):
    tm = EXPERT_TILE
    eid = slab[:, 0:2].astype(jnp.int32).reshape(-1)
    onehot = (eid[:, None] == jnp.arange(N_EXPERTS, dtype=jnp.int32)[None, :]).astype(jnp.int32)
    csum = jnp.cumsum(onehot, axis=0)
    rank = jnp.sum((csum - 1) * onehot, axis=1)
    counts = csum[-1]
    padded = ((counts + tm - 1) // tm) * tm
    ends = jnp.cumsum(padded)
    starts = ends - padded
    dest = starts[eid] + rank
    p_rows = 2 * n + N_EXPERTS * tm
    n_tiles = p_rows // tm
    tok = jnp.arange(2 * n, dtype=jnp.int32) // 2
    row_ids = jnp.zeros((p_rows,), jnp.int32).at[dest].set(tok)
    n_used = (ends[-1] // tm).astype(jnp.int32)
    tile_start = jnp.arange(n_tiles, dtype=jnp.int32) * tm
    te = jnp.sum((ends[None, :] <= tile_start[:, None]).astype(jnp.int32), axis=1)
    te = jnp.minimum(te, N_EXPERTS - 1)
    last = te[jnp.maximum(n_used - 1, 0)]
    te = jnp.where(jnp.arange(n_tiles) < n_used, te, last).astype(jnp.int32)
    p = dest.reshape(n, 2).astype(jnp.int32)
    return te, n_used.reshape(1), row_ids, p[:, 0], p[:, 1]


def kernel(x, c, positions, ada_w, ada_b, norm_mix_g, norm_ffn_g, w_in, conv_w, conv_b,
           beta_ret, beta_conv, w_out, router_group_w, router_group_b, router_expert_w,
           router_expert_b, expert_w_gate, expert_w_up, expert_w_down, norm_final_g):
    bsz, seq, d = x.shape
    n = bsz * seq
    depth = ada_w.shape[0]
    assert depth == 1, "the combine kernel fuses the trunk's final RMSNorm (single layer)"
    W = RET_HEADS * HEAD_DIM

    cos128, sin128 = _rope_tables(positions)
    heads = jnp.arange(RET_HEADS, dtype=F32)
    lg = jnp.log1p(-jnp.exp2(-5.0 - heads))
    lgl = jnp.repeat(lg, HEAD_DIM).reshape(1, W)
    blk_np = np.kron(np.eye(RET_HEADS, dtype=np.float32), np.ones((HEAD_DIM, HEAD_DIM), np.float32))
    blk = jnp.asarray(blk_np, dtype=BF16)

    for l in range(depth):
        mod = _adaln(c, ada_w[l], ada_b[l]).reshape(bsz, 6, d)

        wr = jnp.concatenate([router_group_w[l], router_expert_w[l]], axis=1)
        wr = jnp.pad(wr, ((0, 0), (0, ROUTER_LANES - wr.shape[1])))
        wr_hi = wr.astype(BF16)
        wr_lo = (wr - wr_hi.astype(F32)).astype(BF16)
        wr2 = jnp.concatenate([wr_hi, wr_lo], axis=1)
        br = jnp.concatenate([router_group_b[l], router_expert_b[l]])
        br = jnp.pad(br, (0, ROUTER_LANES - br.shape[0])).reshape(1, ROUTER_LANES)

        x1, h2, slab = _mixer(
            x, mod, cos128, sin128, norm_mix_g[l].reshape(1, d), w_in[l].astype(BF16),
            conv_w[l], conv_b[l].reshape(1, W), beta_ret[l].reshape(1, W),
            beta_conv[l].reshape(1, W), w_out[l].astype(BF16), norm_ffn_g[l].reshape(1, d),
            wr2, br, lg, lgl, blk)

        te, n_used, row_ids, p0, p1 = _routing_plan(slab, n)
        de = expert_w_gate.shape[-1]
        y = _experts(te, n_used, row_ids, h2.reshape(n, d),
                     expert_w_gate[l].reshape(N_EXPERTS, d, de),
                     expert_w_up[l].reshape(N_EXPERTS, d, de),
                     expert_w_down[l].reshape(N_EXPERTS, de, d))
        out = _combine(p0, p1, y, x1.reshape(n, d), slab, mod, norm_final_g.reshape(1, d), seq)
        x = out.reshape(bsz, seq, d)
    return x
```

```python
import functools

import jax
import jax.numpy as jnp
import numpy as np
from jax import lax
from jax.experimental import pallas as pl
from jax.experimental.pallas import tpu as pltpu

F32 = jnp.float32
BF16 = jnp.bfloat16

CHUNK = 64
RET_HEADS = 8
HEAD_DIM = 64
CONV_GROUP_DIM = 64
ROPE_BASE = 10000.0
N_GROUPS = 4
EXPERTS_PER_GROUP = 8
N_EXPERTS = N_GROUPS * EXPERTS_PER_GROUP
EPS = 1e-6
GN_EPS = 1e-5

LANES = 128
SUBLANES = 8
VMEM_LIMIT_BYTES = 56 * 1024 * 1024

SEQ_TILE = 256
EXPERT_TILE = 256
COMBINE_TILE = 256
ROUTER_LANES = LANES


def _silu(v):
    return v * (1.0 / (1.0 + jnp.exp(-v)))


def _adaln_kernel(c_ref, w_ref, b_ref, o_ref):
    s = _silu(c_ref[...])
    o_ref[...] = jnp.dot(s, w_ref[...], precision=lax.Precision.HIGHEST,
                         preferred_element_type=F32) + b_ref[...]


def _adaln(c, w, b):
    bsz, d = c.shape
    n = w.shape[1]
    tn = 1024
    return pl.pallas_call(
        _adaln_kernel,
        out_shape=jax.ShapeDtypeStruct((bsz, n), F32),
        grid=(n // tn,),
        in_specs=[pl.BlockSpec((bsz, d), lambda j: (0, 0)),
                  pl.BlockSpec((d, tn), lambda j: (0, j)),
                  pl.BlockSpec((1, tn), lambda j: (0, j))],
        out_specs=pl.BlockSpec((bsz, tn), lambda j: (0, j)),
        name="adaln",
    )(c, w, b.reshape(1, n))


def _rope_kernel(pos_ref, invf_ref, cos_ref, sin_ref):
    ang = pos_ref[...].astype(F32) * invf_ref[...]
    cos_ref[...] = jnp.cos(ang)
    sin_ref[...] = jnp.sin(ang)


def _rope_tables(positions):
    n = positions.size
    half = HEAD_DIM // 2
    inv_freq = ROPE_BASE ** (-jnp.arange(0, HEAD_DIM, 2, dtype=F32) / HEAD_DIM)
    per_row = LANES // half
    rows = n // per_row
    pos_rep = jnp.repeat(positions.reshape(-1), half).reshape(rows, LANES)
    invf = jnp.tile(inv_freq, per_row).reshape(1, LANES)
    tr = 1024
    cos, sin = pl.pallas_call(
        _rope_kernel,
        out_shape=(jax.ShapeDtypeStruct((rows, LANES), F32),) * 2,
        grid=(rows // tr,),
        in_specs=[pl.BlockSpec((tr, LANES), lambda i: (i, 0)),
                  pl.BlockSpec((1, LANES), lambda i: (0, 0))],
        out_specs=(pl.BlockSpec((tr, LANES), lambda i: (i, 0)),) * 2,
        name="rope_table",
    )(pos_rep, invf)
    cos = cos.reshape(n, half)
    sin = sin.reshape(n, half)
    cos128 = jnp.concatenate([cos, cos, cos, cos], axis=1)
    sin128 = jnp.concatenate([-sin, sin, -sin, sin], axis=1)
    return cos128, sin128


def _mixer_kernel(lg_ref, x_ref, mod_ref, cos_ref, sin_ref, gmix_ref, win_ref, convw_ref,
                  convb_ref, bret_ref, bconv_ref, wout_ref, gffn_ref, wr_ref, br_ref, lgl_ref,
                  blk_ref, x1_ref, h2_ref, slab_ref, cnt_out_ref, state_ref, ubuf_ref, cnt_ref):
    L = SEQ_TILE
    W = RET_HEADS * HEAD_DIM
    s = pl.program_id(1)

    @pl.when(s == 0)
    def _():
        state_ref[...] = jnp.zeros_like(state_ref)
        ubuf_ref[0:SUBLANES, :] = jnp.zeros((SUBLANES, W), F32)

    @pl.when((s == 0) & (pl.program_id(0) == 0))
    def _():
        cnt_ref[...] = jnp.zeros_like(cnt_ref)

    x = x_ref[0]
    d_model = x.shape[-1]
    mod = mod_ref[0]
    shift_m, scale_m, gate_m = mod[0:1], mod[1:2], mod[2:3]
    shift_f, scale_f = mod[3:4], mod[4:5]

    ms = jnp.mean(x * x, axis=-1, keepdims=True)
    h = x * lax.rsqrt(ms + EPS) * gmix_ref[...]
    h = h * (1.0 + scale_m) + shift_m
    hb = h.astype(BF16)

    def proj(i):
        return jnp.dot(hb, win_ref[:, i * W:(i + 1) * W], preferred_element_type=F32)

    cos = jnp.concatenate([cos_ref[...]] * 4, axis=1)
    sin = jnp.concatenate([sin_ref[...]] * 4, axis=1)
    lane_w = lax.broadcasted_iota(jnp.int32, (L, W), 1)
    first_half = (lane_w & (HEAD_DIM - 1)) < (HEAD_DIM // 2)

    def rot(t):
        partner = jnp.where(first_half, pltpu.roll(t, W - HEAD_DIM // 2, 1),
                            pltpu.roll(t, HEAD_DIM // 2, 1))
        return t * cos + partner * sin

    q = rot(proj(0))
    k = rot(proj(1)) * (HEAD_DIM ** -0.5)
    v = proj(2)
    vb = v.astype(BF16)
    kb = k.astype(BF16)

    lgl = lgl_ref[...]
    rowf = lax.broadcasted_iota(jnp.int32, (L, W), 0).astype(F32)
    qd = q * jnp.exp(lgl * (rowf + 1.0))
    kd = k * jnp.exp(lgl * (float(L - 1) - rowf))
    blk = blk_ref[...]
    st = state_ref[...]
    y_inter = jnp.dot(qd.astype(BF16), st.astype(BF16), preferred_element_type=F32)
    kv = lax.dot_general(kd.astype(BF16), vb, (((0,), (0,)), ((), ())),
                         preferred_element_type=F32)
    state_ref[...] = st * jnp.exp(lgl * float(L)) + kv * blk.astype(F32)

    ii = lax.broadcasted_iota(jnp.int32, (L, L), 0)
    jj = lax.broadcasted_iota(jnp.int32, (L, L), 1)
    dist = jnp.abs(ii - jj).astype(F32)
    allowed = (jj // CHUNK) <= (ii // CHUNK)
    lane_p = lax.broadcasted_iota(jnp.int32, (L, LANES), 1)
    lo_head = lane_p < HEAD_DIM
    pairs = []
    for p in range(RET_HEADS // 2):
        sl = slice(p * LANES, (p + 1) * LANES)
        qp, kp, vp = q[:, sl], kb[:, sl], vb[:, sl]
        ys = []
        for hh in range(2):
            head = 2 * p + hh
            keep = lo_head if hh == 0 else jnp.logical_not(lo_head)
            qh = jnp.where(keep, qp, 0.0).astype(BF16)
            sc = lax.dot_general(qh, kp, (((1,), (1,)), ((), ())),
                                 preferred_element_type=F32)
            decay = jnp.where(allowed, jnp.exp(lg_ref[head] * dist), 0.0)
            ys.append(jnp.dot((sc * decay).astype(BF16), vp, preferred_element_type=F32))
        pairs.append(jnp.where(lo_head, ys[0], ys[1]))
    y = jnp.concatenate(pairs, axis=1) + y_inter

    inv_hd = 1.0 / HEAD_DIM
    mu = jnp.dot(y.astype(BF16), blk, preferred_element_type=F32) * inv_hd
    d = y - mu
    var = jnp.dot((d * d).astype(BF16), blk, preferred_element_type=F32) * inv_hd
    g = proj(3)
    y_ret = _silu(g) * (d * lax.rsqrt(var + GN_EPS)) * bret_ref[...]

    b_gate = proj(4)
    u = proj(5) * proj(6)
    ubuf_ref[SUBLANES:SUBLANES + L, :] = u
    u1 = ubuf_ref[SUBLANES - 1:SUBLANES - 1 + L, :]
    u2 = ubuf_ref[SUBLANES - 2:SUBLANES - 2 + L, :]
    ubuf_ref[0:SUBLANES, :] = ubuf_ref[L:L + SUBLANES, :]
    cw = convw_ref[...]
    conv = u2 * cw[0:1] + u1 * cw[1:2] + u * cw[2:3] + convb_ref[...]
    yc = b_gate * conv
    msc = jnp.dot((yc * yc).astype(BF16), blk, preferred_element_type=F32) * (1.0 / CONV_GROUP_DIM)
    y_conv = yc * lax.rsqrt(msc + EPS) * bconv_ref[...]

    mix = (jnp.dot(y_ret.astype(BF16), wout_ref[0:W, :], preferred_element_type=F32)
           + jnp.dot(y_conv.astype(BF16), wout_ref[W:2 * W, :], preferred_element_type=F32))
    x1 = x + gate_m * mix
    x1_ref[0] = x1

    ms2 = jnp.mean(x1 * x1, axis=-1, keepdims=True)
    h2 = x1 * lax.rsqrt(ms2 + EPS) * gffn_ref[...]
    h2 = h2 * (1.0 + scale_f) + shift_f
    for cch in range(d_model // LANES):
        h2_ref[pl.ds(cch, L, stride=SUBLANES), :] = h2[:, cch * LANES:(cch + 1) * LANES]

    hi = h2.astype(BF16)
    lo = (h2 - hi.astype(F32)).astype(BF16)
    w2 = wr_ref[...]
    parts = (jnp.dot(hi, w2, preferred_element_type=F32)
             + jnp.dot(lo, w2, preferred_element_type=F32))
    logits = parts[:, :ROUTER_LANES] + parts[:, ROUTER_LANES:] + br_ref[...]

    lane = lax.broadcasted_iota(jnp.int32, (L, ROUTER_LANES), 1).astype(F32)
    neg_inf = F32(-jnp.inf)
    big = F32(1e9)
    gmask = lane < float(N_GROUPS)
    lgm = jnp.where(gmask, logits, neg_inf)
    gexp = jnp.exp(lgm - jnp.max(lgm, axis=-1, keepdims=True))
    gp = gexp / jnp.sum(gexp, axis=-1, keepdims=True)
    g_top = jnp.max(gp, axis=-1, keepdims=True)
    g_idx = jnp.min(jnp.where(gmask & (gp == g_top), lane, big), axis=-1, keepdims=True)

    e_lo = float(N_GROUPS) + float(EXPERTS_PER_GROUP) * g_idx
    emask = (lane >= e_lo) & (lane < e_lo + float(EXPERTS_PER_GROUP))
    lem = jnp.where(emask, logits, neg_inf)
    eexp = jnp.exp(lem - jnp.max(lem, axis=-1, keepdims=True))
    ep = eexp / jnp.sum(eexp, axis=-1, keepdims=True)
    p1 = jnp.max(jnp.where(emask, ep, -1.0), axis=-1, keepdims=True)
    i1 = jnp.min(jnp.where(emask & (ep == p1), lane, big), axis=-1, keepdims=True)
    m2 = emask & (lane != i1)
    p2 = jnp.max(jnp.where(m2, ep, -1.0), axis=-1, keepdims=True)
    i2 = jnp.min(jnp.where(m2 & (ep == p2), lane, big), axis=-1, keepdims=True)
    den = p1 + p2
    w1 = p1 / den * g_top
    w2_ = p2 / den * g_top
    e1 = i1 - float(N_GROUPS)
    e2 = i2 - float(N_GROUPS)

    oh1 = lane == e1
    oh2 = lane == e2
    ltri = jnp.where(jj < ii, 1.0, 0.0).astype(BF16)
    r1 = jnp.dot(ltri, jnp.where(oh1, 1.0, 0.0).astype(BF16), preferred_element_type=F32)
    r2 = jnp.dot(ltri, jnp.where(oh2, 1.0, 0.0).astype(BF16), preferred_element_type=F32)
    c1 = jnp.sum(jnp.where(oh1, 1.0, 0.0), axis=0, keepdims=True)
    c2 = jnp.sum(jnp.where(oh2, 1.0, 0.0), axis=0, keepdims=True)
    base = cnt_ref[...]
    rank1 = jnp.sum(jnp.where(oh1, base + r1, 0.0), axis=-1, keepdims=True)
    rank2 = jnp.sum(jnp.where(oh2, base + c1 + r2, 0.0), axis=-1, keepdims=True)
    total = base + c1 + c2
    cnt_ref[...] = total
    cnt_out_ref[...] = total

    slab = jnp.where(lane == 0.0, e1,
                     jnp.where(lane == 1.0, e2,
                               jnp.where(lane == 2.0, w1,
                                         jnp.where(lane == 3.0, w2_,
                                                   jnp.where(lane == 4.0, rank1,
                                                             jnp.where(lane == 5.0, rank2, 0.0))))))
    slab_ref[...] = slab


def _mixer(x, mod, cos128, sin128, gmix, win_b, convw, convb, bret, bconv, wout_b, gffn,
           wr2, br, lg, lgl, blk):
    bsz, seq, d = x.shape
    L = SEQ_TILE
    ns = seq // L
    W = RET_HEADS * HEAD_DIM
    n = bsz * seq
    const2 = lambda b, s: (0, 0)
    in_specs = [
        pl.BlockSpec(memory_space=pltpu.SMEM),
        pl.BlockSpec((1, L, d), lambda b, s: (b, s, 0)),
        pl.BlockSpec((1, 6, d), lambda b, s: (b, 0, 0)),
        pl.BlockSpec((L, LANES), lambda b, s: (b * ns + s, 0)),
        pl.BlockSpec((L, LANES), lambda b, s: (b * ns + s, 0)),
        pl.BlockSpec((1, d), const2),
        pl.BlockSpec(win_b.shape, const2),
        pl.BlockSpec(convw.shape, const2),
        pl.BlockSpec((1, W), const2),
        pl.BlockSpec((1, W), const2),
        pl.BlockSpec((1, W), const2),
        pl.BlockSpec(wout_b.shape, const2),
        pl.BlockSpec((1, d), const2),
        pl.BlockSpec(wr2.shape, const2),
        pl.BlockSpec((1, ROUTER_LANES), const2),
        pl.BlockSpec((1, W), const2),
        pl.BlockSpec((W, W), const2),
    ]
    assert d == SUBLANES * LANES, "one token must fill exactly one (8, 128) f32 tile"
    out_shape = (jax.ShapeDtypeStruct((bsz, seq, d), F32),
                 jax.ShapeDtypeStruct((n * SUBLANES, LANES), F32),
                 jax.ShapeDtypeStruct((n, ROUTER_LANES), F32),
                 jax.ShapeDtypeStruct((1, ROUTER_LANES), F32))
    out_specs = (pl.BlockSpec((1, L, d), lambda b, s: (b, s, 0)),
                 pl.BlockSpec((L * SUBLANES, LANES), lambda b, s: (b * ns + s, 0)),
                 pl.BlockSpec((L, ROUTER_LANES), lambda b, s: (b * ns + s, 0)),
                 pl.BlockSpec((1, ROUTER_LANES), const2))
    return pl.pallas_call(
        _mixer_kernel,
        out_shape=out_shape,
        grid=(bsz, ns),
        in_specs=in_specs,
        out_specs=out_specs,
        scratch_shapes=[pltpu.VMEM((W, W), F32),
                        pltpu.VMEM((L + 2 * SUBLANES, W), F32),
                        pltpu.VMEM((1, ROUTER_LANES), F32)],
        compiler_params=pltpu.CompilerParams(
            dimension_semantics=("arbitrary", "arbitrary"),
            vmem_limit_bytes=VMEM_LIMIT_BYTES),
        name="mixer",
    )(lg, x, mod, cos128, sin128, gmix, win_b, convw, convb, bret, bconv, wout_b, gffn,
      wr2, br, lgl, blk)


def _load_token_rows(ref, rows):
    return jnp.concatenate(
        [ref[pl.ds(c, rows, stride=SUBLANES), :] for c in range(SUBLANES)], axis=1)


def _store_token_rows(ref, val):
    rows = val.shape[0]
    for c in range(SUBLANES):
        ref[pl.ds(c, rows, stride=SUBLANES), :] = val[:, c * LANES:(c + 1) * LANES]


DISPATCH_BATCH = 256
DISPATCH_UNROLL = 8
PAD_UNITS = tuple(1 << b for b in reversed(range(EXPERT_TILE.bit_length() - 1)))


def _dispatch_kernel(d0_ref, d1_ref, ps_ref, pn_ref, nused_ref, h2_hbm, z_hbm, xs_hbm, sem, zsem):
    n = h2_hbm.shape[0]
    nb = n // DISPATCH_BATCH

    def wait_batch(slot):
        cnt = 2 * DISPATCH_BATCH
        pltpu.make_async_copy(h2_hbm.at[pl.ds(0, cnt)], xs_hbm.at[pl.ds(0, cnt)],
                              sem.at[slot]).wait()

    def batch(bi, carry):
        slot = bi % 2

        @pl.when(bi >= 2)
        def _():
            wait_batch(slot)

        def body(r, c2):
            t = bi * DISPATCH_BATCH + r
            pltpu.make_async_copy(h2_hbm.at[t], xs_hbm.at[d0_ref[t]], sem.at[slot]).start()
            pltpu.make_async_copy(h2_hbm.at[t], xs_hbm.at[d1_ref[t]], sem.at[slot]).start()
            return c2

        lax.fori_loop(0, DISPATCH_BATCH, body, 0, unroll=DISPATCH_UNROLL)
        return carry

    lax.fori_loop(0, nb, batch, 0)

    def pad_copy(start, unit):
        return pltpu.make_async_copy(z_hbm.at[pl.ds(0, unit)], xs_hbm.at[pl.ds(start, unit)], zsem)

    def pad_pass(do):
        def per_expert(e, carry):
            start = ps_ref[e]
            npad = pn_ref[e]
            for unit in PAD_UNITS:
                @pl.when((npad & unit) != 0)
                def _():
                    do(pad_copy(start + (npad & ~(2 * unit - 1)), unit))
            return carry
        lax.fori_loop(0, N_EXPERTS, per_expert, 0)

    def tail_pass(do):
        zrows = PAD_UNITS[0]

        def per_unit(k, carry):
            do(pad_copy(k * zrows, zrows))
            return carry
        per_tile = EXPERT_TILE // zrows
        lax.fori_loop(nused_ref[0] * per_tile, (xs_hbm.shape[0] // zrows), per_unit, 0)

    pad_pass(lambda cp: cp.start())
    tail_pass(lambda cp: cp.start())
    wait_batch(0)
    wait_batch(1)
    pad_pass(lambda cp: cp.wait())
    tail_pass(lambda cp: cp.wait())


def _dispatch(dest0, dest1, pad_start, pad_n, n_used, h2t, p_rows):
    n = dest0.shape[0]
    h2r = h2t.reshape(n, SUBLANES, LANES)
    zeros = jnp.zeros((PAD_UNITS[0], SUBLANES, LANES), F32)
    grid_spec = pltpu.PrefetchScalarGridSpec(
        num_scalar_prefetch=5,
        grid=(1,),
        in_specs=[pl.BlockSpec(memory_space=pl.ANY), pl.BlockSpec(memory_space=pl.ANY)],
        out_specs=pl.BlockSpec(memory_space=pl.ANY),
        scratch_shapes=[pltpu.SemaphoreType.DMA((2,)), pltpu.SemaphoreType.DMA(())],
    )
    xs = pl.pallas_call(
        _dispatch_kernel,
        out_shape=jax.ShapeDtypeStruct((p_rows, SUBLANES, LANES), F32),
        grid_spec=grid_spec,
        compiler_params=pltpu.CompilerParams(dimension_semantics=("arbitrary",)),
        name="dispatch",
    )(dest0, dest1, pad_start, pad_n, n_used, h2r, zeros)
    return xs.reshape(p_rows * SUBLANES, LANES)


def _expert_kernel(te_ref, nused_ref, xs_ref, wg_ref, wu_ref, wd_ref, y_ref, wgb, wub, wdb):
    tm = EXPERT_TILE
    i = pl.program_id(0)
    n_used = nused_ref[0]

    @pl.when(i < n_used)
    def _():
        e = te_ref[i]
        e_prev = te_ref[jnp.maximum(i - 1, 0)]

        @pl.when((i == 0) | (e != e_prev))
        def _():
            wgb[...] = wg_ref[0].astype(BF16)
            wub[...] = wu_ref[0].astype(BF16)
            wdb[...] = wd_ref[0].astype(BF16)

        xb = _load_token_rows(xs_ref, tm).astype(BF16)
        a = jnp.dot(xb, wgb[...], preferred_element_type=F32)
        u = jnp.dot(xb, wub[...], preferred_element_type=F32)
        hid = (_silu(a) * u).astype(BF16)
        _store_token_rows(y_ref, jnp.dot(hid, wdb[...], preferred_element_type=F32))

    @pl.when(i >= n_used)
    def _():
        y_ref[...] = jnp.zeros_like(y_ref)


def _experts(tile_expert, n_used, xs, wg, wu, wd):
    tm = EXPERT_TILE
    p_rows = xs.shape[0] // SUBLANES
    n_tiles = p_rows // tm
    d, de = wg.shape[1], wg.shape[2]
    used_tile = lambda i, te, nu: (jnp.minimum(i, nu[0] - 1), 0)
    grid_spec = pltpu.PrefetchScalarGridSpec(
        num_scalar_prefetch=2,
        grid=(n_tiles,),
        in_specs=[
            pl.BlockSpec((tm * SUBLANES, LANES), used_tile),
            pl.BlockSpec((1, d, de), lambda i, te, nu: (te[i], 0, 0)),
            pl.BlockSpec((1, d, de), lambda i, te, nu: (te[i], 0, 0)),
            pl.BlockSpec((1, de, d), lambda i, te, nu: (te[i], 0, 0)),
        ],
        out_specs=pl.BlockSpec((tm * SUBLANES, LANES), lambda i, te, nu: (i, 0)),
        scratch_shapes=[pltpu.VMEM((d, de), BF16),
                        pltpu.VMEM((d, de), BF16),
                        pltpu.VMEM((de, d), BF16)],
    )
    return pl.pallas_call(
        _expert_kernel,
        out_shape=jax.ShapeDtypeStruct((p_rows * SUBLANES, LANES), F32),
        grid_spec=grid_spec,
        compiler_params=pltpu.CompilerParams(
            dimension_semantics=("arbitrary",),
            vmem_limit_bytes=VMEM_LIMIT_BYTES),
        name="experts",
    )(tile_expert, n_used, xs, wg, wu, wd)


def _combine_kernel(p0_ref, p1_ref, y_hbm, x1_ref, slab_ref, mod_ref, gfin_ref, o_ref,
                    ybuf, sem):
    tm = COMBINE_TILE
    i = pl.program_id(0)
    nt = pl.num_programs(0)

    def issue(tile, slot):
        base = tile * tm

        def body(r, carry):
            dst_row = pl.multiple_of(r * SUBLANES, SUBLANES)
            for j, p_ref in enumerate((p0_ref, p1_ref)):
                src_row = pl.multiple_of(p_ref[base + r] * SUBLANES, SUBLANES)
                pltpu.make_async_copy(y_hbm.at[pl.ds(src_row, SUBLANES), :],
                                      ybuf.at[slot, j, pl.ds(dst_row, SUBLANES), :],
                                      sem.at[slot]).start()
            return carry

        lax.fori_loop(0, tm, body, 0, unroll=DISPATCH_UNROLL)

    @pl.when(i == 0)
    def _():
        issue(0, 0)

    @pl.when(i + 1 < nt)
    def _():
        issue(i + 1, (i + 1) % 2)

    slot = i % 2
    for j in range(2):
        pltpu.make_async_copy(y_hbm.at[pl.ds(0, tm * SUBLANES), :], ybuf.at[slot, j],
                              sem.at[slot]).wait()

    slab = slab_ref[...]
    w0 = slab[:, 2:3]
    w1 = slab[:, 3:4]
    gate_f = mod_ref[0][5:6]
    moe = (w0 * _load_token_rows(ybuf.at[slot, 0], tm)
           + w1 * _load_token_rows(ybuf.at[slot, 1], tm))
    xo = x1_ref[...] + gate_f * moe
    ms = jnp.mean(xo * xo, axis=-1, keepdims=True)
    o_ref[...] = xo * lax.rsqrt(ms + EPS) * gfin_ref[...]


def _combine(p0, p1, y, x1, slab, mod, gfin, seq):
    n, d = x1.shape
    tm = COMBINE_TILE
    tiles_per_seq = seq // tm
    grid_spec = pltpu.PrefetchScalarGridSpec(
        num_scalar_prefetch=2,
        grid=(n // tm,),
        in_specs=[
            pl.BlockSpec(memory_space=pl.ANY),
            pl.BlockSpec((tm, d), lambda i, a, b: (i, 0)),
            pl.BlockSpec((tm, ROUTER_LANES), lambda i, a, b: (i, 0)),
            pl.BlockSpec((1, 6, d), lambda i, a, b: (i // tiles_per_seq, 0, 0)),
            pl.BlockSpec((1, d), lambda i, a, b: (0, 0)),
        ],
        out_specs=pl.BlockSpec((tm, d), lambda i, a, b: (i, 0)),
        scratch_shapes=[pltpu.VMEM((2, 2, tm * SUBLANES, LANES), F32),
                        pltpu.SemaphoreType.DMA((2,))],
    )
    return pl.pallas_call(
        _combine_kernel,
        out_shape=jax.ShapeDtypeStruct((n, d), F32),
        grid_spec=grid_spec,
        compiler_params=pltpu.CompilerParams(
            dimension_semantics=("arbitrary",),
            vmem_limit_bytes=VMEM_LIMIT_BYTES),
        name="combine",
    )(p0, p1, y, x1, slab, mod, gfin)


def _routing_plan(slab, counts_f, n):
    tm = EXPERT_TILE
    counts = counts_f[0, :N_EXPERTS].astype(jnp.int32)
    padded = ((counts + tm - 1) // tm) * tm
    ends = jnp.cumsum(padded)
    starts = ends - padded
    eid = slab[:, 0:2].astype(jnp.int32)
    rank = slab[:, 4:6].astype(jnp.int32)
    onehot = eid[:, :, None] == jnp.arange(N_EXPERTS, dtype=jnp.int32)[None, None, :]
    dest = jnp.sum(jnp.where(onehot, starts[None, None, :], 0), axis=-1) + rank
    p_rows = 2 * n + N_EXPERTS * tm
    n_tiles = p_rows // tm
    n_used = (ends[-1] // tm).astype(jnp.int32)
    tile_start = jnp.arange(n_tiles, dtype=jnp.int32) * tm
    te = jnp.sum((ends[None, :] <= tile_start[:, None]).astype(jnp.int32), axis=1)
    te = jnp.minimum(te, N_EXPERTS - 1)
    last = jnp.sum(jnp.where(jnp.arange(n_tiles) == n_used - 1, te, 0))
    te = jnp.where(jnp.arange(n_tiles) < n_used, te, last).astype(jnp.int32)
    return (te, n_used.reshape(1), dest[:, 0], dest[:, 1], starts + counts, padded - counts,
            p_rows)


def kernel(x, c, positions, ada_w, ada_b, norm_mix_g, norm_ffn_g, w_in, conv_w, conv_b,
           beta_ret, beta_conv, w_out, router_group_w, router_group_b, router_expert_w,
           router_expert_b, expert_w_gate, expert_w_up, expert_w_down, norm_final_g):
    bsz, seq, d = x.shape
    n = bsz * seq
    depth = ada_w.shape[0]
    assert depth == 1, "the combine kernel fuses the trunk's final RMSNorm (single layer)"
    W = RET_HEADS * HEAD_DIM

    cos128, sin128 = _rope_tables(positions)
    heads = jnp.arange(RET_HEADS, dtype=F32)
    lg = jnp.log1p(-jnp.exp2(-5.0 - heads))
    lgl = jnp.repeat(lg, HEAD_DIM).reshape(1, W)
    blk_np = np.kron(np.eye(RET_HEADS, dtype=np.float32), np.ones((HEAD_DIM, HEAD_DIM), np.float32))
    blk = jnp.asarray(blk_np, dtype=BF16)

    for l in range(depth):
        mod = _adaln(c, ada_w[l], ada_b[l]).reshape(bsz, 6, d)

        wr = jnp.concatenate([router_group_w[l], router_expert_w[l]], axis=1)
        wr = jnp.pad(wr, ((0, 0), (0, ROUTER_LANES - wr.shape[1])))
        wr_hi = wr.astype(BF16)
        wr_lo = (wr - wr_hi.astype(F32)).astype(BF16)
        wr2 = jnp.concatenate([wr_hi, wr_lo], axis=1)
        br = jnp.concatenate([router_group_b[l], router_expert_b[l]])
        br = jnp.pad(br, (0, ROUTER_LANES - br.shape[0])).reshape(1, ROUTER_LANES)

        x1, h2t, slab, counts = _mixer(
            x, mod, cos128, sin128, norm_mix_g[l].reshape(1, d), w_in[l].astype(BF16),
            conv_w[l], conv_b[l].reshape(1, W), beta_ret[l].reshape(1, W),
            beta_conv[l].reshape(1, W), w_out[l].astype(BF16), norm_ffn_g[l].reshape(1, d),
            wr2, br, lg, lgl, blk)

        te, n_used, p0, p1, pad_start, pad_n, p_rows = _routing_plan(slab, counts, n)
        xs = _dispatch(p0, p1, pad_start, pad_n, n_used, h2t, p_rows)
        de = expert_w_gate.shape[-1]
        y = _experts(te, n_used, xs,
                     expert_w_gate[l].reshape(N_EXPERTS, d, de),
                     expert_w_up[l].reshape(N_EXPERTS, d, de),
                     expert_w_down[l].reshape(N_EXPERTS, de, d))
        out = _combine(p0, p1, y, x1.reshape(n, d), slab, mod, norm_final_g.reshape(1, d), seq)
        x = out.reshape(bsz, seq, d)
    return x
```

```python
import functools

import jax
import jax.numpy as jnp
import numpy as np
from jax import lax
from jax.experimental import pallas as pl
from jax.experimental.pallas import tpu as pltpu

F32 = jnp.float32
BF16 = jnp.bfloat16

CHUNK = 64
RET_HEADS = 8
HEAD_DIM = 64
CONV_GROUP_DIM = 64
ROPE_BASE = 10000.0
N_GROUPS = 4
EXPERTS_PER_GROUP = 8
N_EXPERTS = N_GROUPS * EXPERTS_PER_GROUP
EPS = 1e-6
GN_EPS = 1e-5

LANES = 128
SUBLANES = 8
VMEM_LIMIT_BYTES = 56 * 1024 * 1024

SEQ_TILE = 256
EXPERT_TILE = 256
COMBINE_TILE = 256
ROUTER_LANES = LANES


def _silu(v):
    return v * (1.0 / (1.0 + jnp.exp(-v)))


def _adaln_kernel(c_ref, w_ref, b_ref, o_ref):
    s = _silu(c_ref[...])
    o_ref[...] = jnp.dot(s, w_ref[...], precision=lax.Precision.HIGHEST,
                         preferred_element_type=F32) + b_ref[...]


def _adaln(c, w, b):
    bsz, d = c.shape
    n = w.shape[1]
    tn = 1024
    return pl.pallas_call(
        _adaln_kernel,
        out_shape=jax.ShapeDtypeStruct((bsz, n), F32),
        grid=(n // tn,),
        in_specs=[pl.BlockSpec((bsz, d), lambda j: (0, 0)),
                  pl.BlockSpec((d, tn), lambda j: (0, j)),
                  pl.BlockSpec((1, tn), lambda j: (0, j))],
        out_specs=pl.BlockSpec((bsz, tn), lambda j: (0, j)),
        name="adaln",
    )(c, w, b.reshape(1, n))


def _rope_kernel(pos_ref, invf_ref, cos_ref, sin_ref):
    ang = pos_ref[...].astype(F32) * invf_ref[...]
    cos_ref[...] = jnp.cos(ang)
    sin_ref[...] = jnp.sin(ang)


def _rope_tables(positions):
    n = positions.size
    half = HEAD_DIM // 2
    inv_freq = ROPE_BASE ** (-jnp.arange(0, HEAD_DIM, 2, dtype=F32) / HEAD_DIM)
    per_row = LANES // half
    rows = n // per_row
    pos_rep = jnp.repeat(positions.reshape(-1), half).reshape(rows, LANES)
    invf = jnp.tile(inv_freq, per_row).reshape(1, LANES)
    tr = 1024
    cos, sin = pl.pallas_call(
        _rope_kernel,
        out_shape=(jax.ShapeDtypeStruct((rows, LANES), F32),) * 2,
        grid=(rows // tr,),
        in_specs=[pl.BlockSpec((tr, LANES), lambda i: (i, 0)),
                  pl.BlockSpec((1, LANES), lambda i: (0, 0))],
        out_specs=(pl.BlockSpec((tr, LANES), lambda i: (i, 0)),) * 2,
        name="rope_table",
    )(pos_rep, invf)
    cos = cos.reshape(n, half)
    sin = sin.reshape(n, half)
    cos128 = jnp.concatenate([cos, cos, cos, cos], axis=1)
    sin128 = jnp.concatenate([-sin, sin, -sin, sin], axis=1)
    return cos128, sin128


def _mixer_kernel(lg_ref, x_ref, mod_ref, cos_ref, sin_ref, gmix_ref, win_ref, convw_ref,
                  convb_ref, bret_ref, bconv_ref, wout_ref, gffn_ref, wr_ref, br_ref, lgl_ref,
                  blk_ref, x1_ref, h2_ref, slab_ref, cnt_out_ref, state_ref, ubuf_ref, cnt_ref):
    L = SEQ_TILE
    W = RET_HEADS * HEAD_DIM
    s = pl.program_id(1)

    @pl.when(s == 0)
    def _():
        state_ref[...] = jnp.zeros_like(state_ref)
        ubuf_ref[0:SUBLANES, :] = jnp.zeros((SUBLANES, W), F32)

    @pl.when((s == 0) & (pl.program_id(0) == 0))
    def _():
        cnt_ref[...] = jnp.zeros_like(cnt_ref)

    x = x_ref[0]
    d_model = x.shape[-1]
    mod = mod_ref[0]
    shift_m, scale_m, gate_m = mod[0:1], mod[1:2], mod[2:3]
    shift_f, scale_f = mod[3:4], mod[4:5]

    ms = jnp.mean(x * x, axis=-1, keepdims=True)
    h = x * lax.rsqrt(ms + EPS) * gmix_ref[...]
    h = h * (1.0 + scale_m) + shift_m
    hb = h.astype(BF16)

    def proj(i):
        return jnp.dot(hb, win_ref[:, i * W:(i + 1) * W], preferred_element_type=F32)

    cos = jnp.concatenate([cos_ref[...]] * 4, axis=1)
    sin = jnp.concatenate([sin_ref[...]] * 4, axis=1)
    lane_w = lax.broadcasted_iota(jnp.int32, (L, W), 1)
    first_half = (lane_w & (HEAD_DIM - 1)) < (HEAD_DIM // 2)

    def rot(t):
        partner = jnp.where(first_half, pltpu.roll(t, W - HEAD_DIM // 2, 1),
                            pltpu.roll(t, HEAD_DIM // 2, 1))
        return t * cos + partner * sin

    q = rot(proj(0))
    k = rot(proj(1)) * (HEAD_DIM ** -0.5)
    v = proj(2)
    vb = v.astype(BF16)
    kb = k.astype(BF16)

    lgl = lgl_ref[...]
    rowf = lax.broadcasted_iota(jnp.int32, (L, W), 0).astype(F32)
    qd = q * jnp.exp(lgl * (rowf + 1.0))
    kd = k * jnp.exp(lgl * (float(L - 1) - rowf))
    blk = blk_ref[...]
    st = state_ref[...]
    y_inter = jnp.dot(qd.astype(BF16), st.astype(BF16), preferred_element_type=F32)
    kv = lax.dot_general(kd.astype(BF16), vb, (((0,), (0,)), ((), ())),
                         preferred_element_type=F32)
    state_ref[...] = st * jnp.exp(lgl * float(L)) + kv * blk.astype(F32)

    ii = lax.broadcasted_iota(jnp.int32, (L, L), 0)
    jj = lax.broadcasted_iota(jnp.int32, (L, L), 1)
    dist = jnp.abs(ii - jj).astype(F32)
    allowed = (jj // CHUNK) <= (ii // CHUNK)
    lane_p = lax.broadcasted_iota(jnp.int32, (L, LANES), 1)
    lo_head = lane_p < HEAD_DIM
    pairs = []
    for p in range(RET_HEADS // 2):
        sl = slice(p * LANES, (p + 1) * LANES)
        qp, kp, vp = q[:, sl], kb[:, sl], vb[:, sl]
        ys = []
        for hh in range(2):
            head = 2 * p + hh
            keep = lo_head if hh == 0 else jnp.logical_not(lo_head)
            qh = jnp.where(keep, qp, 0.0).astype(BF16)
            sc = lax.dot_general(qh, kp, (((1,), (1,)), ((), ())),
                                 preferred_element_type=F32)
            decay = jnp.where(allowed, jnp.exp(lg_ref[head] * dist), 0.0)
            ys.append(jnp.dot((sc * decay).astype(BF16), vp, preferred_element_type=F32))
        pairs.append(jnp.where(lo_head, ys[0], ys[1]))
    y = jnp.concatenate(pairs, axis=1) + y_inter

    inv_hd = 1.0 / HEAD_DIM
    mu = jnp.dot(y.astype(BF16), blk, preferred_element_type=F32) * inv_hd
    d = y - mu
    var = jnp.dot((d * d).astype(BF16), blk, preferred_element_type=F32) * inv_hd
    g = proj(3)
    y_ret = _silu(g) * (d * lax.rsqrt(var + GN_EPS)) * bret_ref[...]

    b_gate = proj(4)
    u = proj(5) * proj(6)
    ubuf_ref[SUBLANES:SUBLANES + L, :] = u
    u1 = ubuf_ref[SUBLANES - 1:SUBLANES - 1 + L, :]
    u2 = ubuf_ref[SUBLANES - 2:SUBLANES - 2 + L, :]
    ubuf_ref[0:SUBLANES, :] = ubuf_ref[L:L + SUBLANES, :]
    cw = convw_ref[...]
    conv = u2 * cw[0:1] + u1 * cw[1:2] + u * cw[2:3] + convb_ref[...]
    yc = b_gate * conv
    msc = jnp.dot((yc * yc).astype(BF16), blk, preferred_element_type=F32) * (1.0 / CONV_GROUP_DIM)
    y_conv = yc * lax.rsqrt(msc + EPS) * bconv_ref[...]

    mix = (jnp.dot(y_ret.astype(BF16), wout_ref[0:W, :], preferred_element_type=F32)
           + jnp.dot(y_conv.astype(BF16), wout_ref[W:2 * W, :], preferred_element_type=F32))
    x1 = x + gate_m * mix
    x1_ref[0] = x1

    ms2 = jnp.mean(x1 * x1, axis=-1, keepdims=True)
    h2 = x1 * lax.rsqrt(ms2 + EPS) * gffn_ref[...]
    h2 = h2 * (1.0 + scale_f) + shift_f
    for cch in range(d_model // LANES):
        h2_ref[pl.ds(cch, L, stride=SUBLANES), :] = h2[:, cch * LANES:(cch + 1) * LANES]

    hi = h2.astype(BF16)
    lo = (h2 - hi.astype(F32)).astype(BF16)
    w2 = wr_ref[...]
    parts = (jnp.dot(hi, w2, preferred_element_type=F32)
             + jnp.dot(lo, w2, preferred_element_type=F32))
    logits = parts[:, :ROUTER_LANES] + parts[:, ROUTER_LANES:] + br_ref[...]

    lane = lax.broadcasted_iota(jnp.int32, (L, ROUTER_LANES), 1).astype(F32)
    neg_inf = F32(-jnp.inf)
    big = F32(1e9)
    gmask = lane < float(N_GROUPS)
    lgm = jnp.where(gmask, logits, neg_inf)
    gexp = jnp.exp(lgm - jnp.max(lgm, axis=-1, keepdims=True))
    gp = gexp / jnp.sum(gexp, axis=-1, keepdims=True)
    g_top = jnp.max(gp, axis=-1, keepdims=True)
    g_idx = jnp.min(jnp.where(gmask & (gp == g_top), lane, big), axis=-1, keepdims=True)

    e_lo = float(N_GROUPS) + float(EXPERTS_PER_GROUP) * g_idx
    emask = (lane >= e_lo) & (lane < e_lo + float(EXPERTS_PER_GROUP))
    lem = jnp.where(emask, logits, neg_inf)
    eexp = jnp.exp(lem - jnp.max(lem, axis=-1, keepdims=True))
    ep = eexp / jnp.sum(eexp, axis=-1, keepdims=True)
    p1 = jnp.max(jnp.where(emask, ep, -1.0), axis=-1, keepdims=True)
    i1 = jnp.min(jnp.where(emask & (ep == p1), lane, big), axis=-1, keepdims=True)
    m2 = emask & (lane != i1)
    p2 = jnp.max(jnp.where(m2, ep, -1.0), axis=-1, keepdims=True)
    i2 = jnp.min(jnp.where(m2 & (ep == p2), lane, big), axis=-1, keepdims=True)
    den = p1 + p2
    w1 = p1 / den * g_top
    w2_ = p2 / den * g_top
    e1 = i1 - float(N_GROUPS)
    e2 = i2 - float(N_GROUPS)

    oh1 = lane == e1
    oh2 = lane == e2
    ltri = jnp.where(jj < ii, 1.0, 0.0).astype(BF16)
    r1 = jnp.dot(ltri, jnp.where(oh1, 1.0, 0.0).astype(BF16), preferred_element_type=F32)
    r2 = jnp.dot(ltri, jnp.where(oh2, 1.0, 0.0).astype(BF16), preferred_element_type=F32)
    c1 = jnp.sum(jnp.where(oh1, 1.0, 0.0), axis=0, keepdims=True)
    c2 = jnp.sum(jnp.where(oh2, 1.0, 0.0), axis=0, keepdims=True)
    base = cnt_ref[...]
    rank1 = jnp.sum(jnp.where(oh1, base + r1, 0.0), axis=-1, keepdims=True)
    rank2 = jnp.sum(jnp.where(oh2, base + c1 + r2, 0.0), axis=-1, keepdims=True)
    total = base + c1 + c2
    cnt_ref[...] = total
    cnt_out_ref[...] = total

    slab = jnp.where(lane == 0.0, e1,
                     jnp.where(lane == 1.0, e2,
                               jnp.where(lane == 2.0, w1,
                                         jnp.where(lane == 3.0, w2_,
                                                   jnp.where(lane == 4.0, rank1,
                                                             jnp.where(lane == 5.0, rank2, 0.0))))))
    slab_ref[...] = slab


def _mixer(x, mod, cos128, sin128, gmix, win_b, convw, convb, bret, bconv, wout_b, gffn,
           wr2, br, lg, lgl, blk):
    bsz, seq, d = x.shape
    L = SEQ_TILE
    ns = seq // L
    W = RET_HEADS * HEAD_DIM
    n = bsz * seq
    const2 = lambda b, s: (0, 0)
    in_specs = [
        pl.BlockSpec(memory_space=pltpu.SMEM),
        pl.BlockSpec((1, L, d), lambda b, s: (b, s, 0)),
        pl.BlockSpec((1, 6, d), lambda b, s: (b, 0, 0)),
        pl.BlockSpec((L, LANES), lambda b, s: (b * ns + s, 0)),
        pl.BlockSpec((L, LANES), lambda b, s: (b * ns + s, 0)),
        pl.BlockSpec((1, d), const2),
        pl.BlockSpec(win_b.shape, const2),
        pl.BlockSpec(convw.shape, const2),
        pl.BlockSpec((1, W), const2),
        pl.BlockSpec((1, W), const2),
        pl.BlockSpec((1, W), const2),
        pl.BlockSpec(wout_b.shape, const2),
        pl.BlockSpec((1, d), const2),
        pl.BlockSpec(wr2.shape, const2),
        pl.BlockSpec((1, ROUTER_LANES), const2),
        pl.BlockSpec((1, W), const2),
        pl.BlockSpec((W, W), const2),
    ]
    assert d == SUBLANES * LANES, "one token must fill exactly one (8, 128) f32 tile"
    out_shape = (jax.ShapeDtypeStruct((bsz, seq, d), F32),
                 jax.ShapeDtypeStruct((n * SUBLANES, LANES), F32),
                 jax.ShapeDtypeStruct((n, ROUTER_LANES), F32),
                 jax.ShapeDtypeStruct((1, ROUTER_LANES), F32))
    out_specs = (pl.BlockSpec((1, L, d), lambda b, s: (b, s, 0)),
                 pl.BlockSpec((L * SUBLANES, LANES), lambda b, s: (b * ns + s, 0)),
                 pl.BlockSpec((L, ROUTER_LANES), lambda b, s: (b * ns + s, 0)),
                 pl.BlockSpec((1, ROUTER_LANES), const2))
    return pl.pallas_call(
        _mixer_kernel,
        out_shape=out_shape,
        grid=(bsz, ns),
        in_specs=in_specs,
        out_specs=out_specs,
        scratch_shapes=[pltpu.VMEM((W, W), F32),
                        pltpu.VMEM((L + 2 * SUBLANES, W), F32),
                        pltpu.VMEM((1, ROUTER_LANES), F32)],
        compiler_params=pltpu.CompilerParams(
            dimension_semantics=("arbitrary", "arbitrary"),
            vmem_limit_bytes=VMEM_LIMIT_BYTES),
        name="mixer",
    )(lg, x, mod, cos128, sin128, gmix, win_b, convw, convb, bret, bconv, wout_b, gffn,
      wr2, br, lgl, blk)


def _load_token_rows(ref, rows):
    return jnp.concatenate(
        [ref[pl.ds(c, rows, stride=SUBLANES), :] for c in range(SUBLANES)], axis=1)


def _store_token_rows(ref, val):
    rows = val.shape[0]
    for c in range(SUBLANES):
        ref[pl.ds(c, rows, stride=SUBLANES), :] = val[:, c * LANES:(c + 1) * LANES]


DISPATCH_TILE = 512
DISPATCH_UNROLL = 8
PAD_UNITS = tuple(1 << b for b in reversed(range(EXPERT_TILE.bit_length() - 1)))


def _token_rows(ref, row):
    return ref.at[pl.ds(pl.multiple_of(row * SUBLANES, SUBLANES), SUBLANES), :]


def _dispatch_kernel(d0_ref, d1_ref, ps_ref, pn_ref, nused_ref, h2_ref, xs_hbm, zbuf, sem, zsem):
    dt = DISPATCH_TILE
    i = pl.program_id(0)
    zrows = PAD_UNITS[0]

    def pad_copy(start, unit):
        return pltpu.make_async_copy(zbuf.at[pl.ds(0, unit * SUBLANES), :],
                                     xs_hbm.at[pl.ds(pl.multiple_of(start * SUBLANES, SUBLANES),
                                                     unit * SUBLANES), :], zsem)

    def pad_pass(do):
        def per_expert(e, carry):
            start = ps_ref[e]
            npad = pn_ref[e]
            for unit in PAD_UNITS:
                @pl.when((npad & unit) != 0)
                def _():
                    do(pad_copy(start + (npad & ~(2 * unit - 1)), unit))
            return carry
        lax.fori_loop(0, N_EXPERTS, per_expert, 0)

    def tail_pass(do):
        def per_unit(k, carry):
            do(pad_copy(k * zrows, zrows))
            return carry
        per_tile = EXPERT_TILE // zrows
        n_units = xs_hbm.shape[0] // (zrows * SUBLANES)
        lax.fori_loop(nused_ref[0] * per_tile, n_units, per_unit, 0)

    @pl.when(i == 0)
    def _():
        zbuf[...] = jnp.zeros_like(zbuf)
        pad_pass(lambda cp: cp.start())
        tail_pass(lambda cp: cp.start())

    base = i * dt

    def body(r, carry):
        src = _token_rows(h2_ref, r)
        for d_ref in (d0_ref, d1_ref):
            pltpu.make_async_copy(src, _token_rows(xs_hbm, d_ref[base + r]), sem).start()
        return carry

    lax.fori_loop(0, dt, body, 0, unroll=DISPATCH_UNROLL)
    for _ in range(2):
        pltpu.make_async_copy(h2_ref, xs_hbm.at[pl.ds(0, dt * SUBLANES), :], sem).wait()

    @pl.when(i == pl.num_programs(0) - 1)
    def _():
        pad_pass(lambda cp: cp.wait())
        tail_pass(lambda cp: cp.wait())


def _dispatch(dest0, dest1, pad_start, pad_n, n_used, h2t, p_rows):
    n = dest0.shape[0]
    dt = DISPATCH_TILE
    grid_spec = pltpu.PrefetchScalarGridSpec(
        num_scalar_prefetch=5,
        grid=(n // dt,),
        in_specs=[pl.BlockSpec((dt * SUBLANES, LANES), lambda i, *_: (i, 0))],
        out_specs=pl.BlockSpec(memory_space=pl.ANY),
        scratch_shapes=[pltpu.VMEM((PAD_UNITS[0] * SUBLANES, LANES), F32),
                        pltpu.SemaphoreType.DMA(()), pltpu.SemaphoreType.DMA(())],
    )
    return pl.pallas_call(
        _dispatch_kernel,
        out_shape=jax.ShapeDtypeStruct((p_rows * SUBLANES, LANES), F32),
        grid_spec=grid_spec,
        compiler_params=pltpu.CompilerParams(dimension_semantics=("arbitrary",)),
        name="dispatch",
    )(dest0, dest1, pad_start, pad_n, n_used, h2t)


def _expert_kernel(te_ref, nused_ref, xs_ref, wg_ref, wu_ref, wd_ref, y_ref, wgb, wub, wdb):
    tm = EXPERT_TILE
    i = pl.program_id(0)
    n_used = nused_ref[0]

    @pl.when(i < n_used)
    def _():
        e = te_ref[i]
        e_prev = te_ref[jnp.maximum(i - 1, 0)]

        @pl.when((i == 0) | (e != e_prev))
        def _():
            wgb[...] = wg_ref[0].astype(BF16)
            wub[...] = wu_ref[0].astype(BF16)
            wdb[...] = wd_ref[0].astype(BF16)

        xb = _load_token_rows(xs_ref, tm).astype(BF16)
        a = jnp.dot(xb, wgb[...], preferred_element_type=F32)
        u = jnp.dot(xb, wub[...], preferred_element_type=F32)
        hid = (_silu(a) * u).astype(BF16)
        _store_token_rows(y_ref, jnp.dot(hid, wdb[...], preferred_element_type=F32))

    @pl.when(i >= n_used)
    def _():
        y_ref[...] = jnp.zeros_like(y_ref)


def _experts(tile_expert, n_used, xs, wg, wu, wd):
    tm = EXPERT_TILE
    p_rows = xs.shape[0] // SUBLANES
    n_tiles = p_rows // tm
    d, de = wg.shape[1], wg.shape[2]
    used_tile = lambda i, te, nu: (jnp.minimum(i, nu[0] - 1), 0)
    grid_spec = pltpu.PrefetchScalarGridSpec(
        num_scalar_prefetch=2,
        grid=(n_tiles,),
        in_specs=[
            pl.BlockSpec((tm * SUBLANES, LANES), used_tile),
            pl.BlockSpec((1, d, de), lambda i, te, nu: (te[i], 0, 0)),
            pl.BlockSpec((1, d, de), lambda i, te, nu: (te[i], 0, 0)),
            pl.BlockSpec((1, de, d), lambda i, te, nu: (te[i], 0, 0)),
        ],
        out_specs=pl.BlockSpec((tm * SUBLANES, LANES), lambda i, te, nu: (i, 0)),
        scratch_shapes=[pltpu.VMEM((d, de), BF16),
                        pltpu.VMEM((d, de), BF16),
                        pltpu.VMEM((de, d), BF16)],
    )
    return pl.pallas_call(
        _expert_kernel,
        out_shape=jax.ShapeDtypeStruct((p_rows * SUBLANES, LANES), F32),
        grid_spec=grid_spec,
        compiler_params=pltpu.CompilerParams(
            dimension_semantics=("arbitrary",),
            vmem_limit_bytes=VMEM_LIMIT_BYTES),
        name="experts",
    )(tile_expert, n_used, xs, wg, wu, wd)


def _combine_kernel(p0_ref, p1_ref, y_hbm, x1_ref, slab_ref, mod_ref, gfin_ref, o_ref,
                    ybuf, sem):
    tm = COMBINE_TILE
    i = pl.program_id(0)
    nt = pl.num_programs(0)

    def issue(tile, slot):
        base = tile * tm

        def body(r, carry):
            dst_row = pl.multiple_of(r * SUBLANES, SUBLANES)
            for j, p_ref in enumerate((p0_ref, p1_ref)):
                src_row = pl.multiple_of(p_ref[base + r] * SUBLANES, SUBLANES)
                pltpu.make_async_copy(y_hbm.at[pl.ds(src_row, SUBLANES), :],
                                      ybuf.at[slot, j, pl.ds(dst_row, SUBLANES), :],
                                      sem.at[slot]).start()
            return carry

        lax.fori_loop(0, tm, body, 0, unroll=DISPATCH_UNROLL)

    @pl.when(i == 0)
    def _():
        issue(0, 0)

    @pl.when(i + 1 < nt)
    def _():
        issue(i + 1, (i + 1) % 2)

    slot = i % 2
    for j in range(2):
        pltpu.make_async_copy(y_hbm.at[pl.ds(0, tm * SUBLANES), :], ybuf.at[slot, j],
                              sem.at[slot]).wait()

    slab = slab_ref[...]
    w0 = slab[:, 2:3]
    w1 = slab[:, 3:4]
    gate_f = mod_ref[0][5:6]
    moe = (w0 * _load_token_rows(ybuf.at[slot, 0], tm)
           + w1 * _load_token_rows(ybuf.at[slot, 1], tm))
    xo = x1_ref[...] + gate_f * moe
    ms = jnp.mean(xo * xo, axis=-1, keepdims=True)
    o_ref[...] = xo * lax.rsqrt(ms + EPS) * gfin_ref[...]


def _combine(p0, p1, y, x1, slab, mod, gfin, seq):
    n, d = x1.shape
    tm = COMBINE_TILE
    tiles_per_seq = seq // tm
    grid_spec = pltpu.PrefetchScalarGridSpec(
        num_scalar_prefetch=2,
        grid=(n // tm,),
        in_specs=[
            pl.BlockSpec(memory_space=pl.ANY),
            pl.BlockSpec((tm, d), lambda i, a, b: (i, 0)),
            pl.BlockSpec((tm, ROUTER_LANES), lambda i, a, b: (i, 0)),
            pl.BlockSpec((1, 6, d), lambda i, a, b: (i // tiles_per_seq, 0, 0)),
            pl.BlockSpec((1, d), lambda i, a, b: (0, 0)),
        ],
        out_specs=pl.BlockSpec((tm, d), lambda i, a, b: (i, 0)),
        scratch_shapes=[pltpu.VMEM((2, 2, tm * SUBLANES, LANES), F32),
                        pltpu.SemaphoreType.DMA((2,))],
    )
    return pl.pallas_call(
        _combine_kernel,
        out_shape=jax.ShapeDtypeStruct((n, d), F32),
        grid_spec=grid_spec,
        compiler_params=pltpu.CompilerParams(
            dimension_semantics=("arbitrary",),
            vmem_limit_bytes=VMEM_LIMIT_BYTES),
        name="combine",
    )(p0, p1, y, x1, slab, mod, gfin)


def _routing_plan(slab, counts_f, n):
    tm = EXPERT_TILE
    counts = counts_f[0, :N_EXPERTS].astype(jnp.int32)
    padded = ((counts + tm - 1) // tm) * tm
    ends = jnp.cumsum(padded)
    starts = ends - padded
    eid = slab[:, 0:2].astype(jnp.int32)
    rank = slab[:, 4:6].astype(jnp.int32)
    onehot = eid[:, :, None] == jnp.arange(N_EXPERTS, dtype=jnp.int32)[None, None, :]
    dest = jnp.sum(jnp.where(onehot, starts[None, None, :], 0), axis=-1) + rank
    p_rows = 2 * n + N_EXPERTS * tm
    n_tiles = p_rows // tm
    n_used = (ends[-1] // tm).astype(jnp.int32)
    tile_start = jnp.arange(n_tiles, dtype=jnp.int32) * tm
    te = jnp.sum((ends[None, :] <= tile_start[:, None]).astype(jnp.int32), axis=1)
    te = jnp.minimum(te, N_EXPERTS - 1)
    last = jnp.sum(jnp.where(jnp.arange(n_tiles) == n_used - 1, te, 0))
    te = jnp.where(jnp.arange(n_tiles) < n_used, te, last).astype(jnp.int32)
    return (te, n_used.reshape(1), dest[:, 0], dest[:, 1], starts + counts, padded - counts,
            p_rows)


def kernel(x, c, positions, ada_w, ada_b, norm_mix_g, norm_ffn_g, w_in, conv_w, conv_b,
           beta_ret, beta_conv, w_out, router_group_w, router_group_b, router_expert_w,
           router_expert_b, expert_w_gate, expert_w_up, expert_w_down, norm_final_g):
    bsz, seq, d = x.shape
    n = bsz * seq
    depth = ada_w.shape[0]
    assert depth == 1, "the combine kernel fuses the trunk's final RMSNorm (single layer)"
    W = RET_HEADS * HEAD_DIM

    cos128, sin128 = _rope_tables(positions)
    heads = jnp.arange(RET_HEADS, dtype=F32)
    lg = jnp.log1p(-jnp.exp2(-5.0 - heads))
    lgl = jnp.repeat(lg, HEAD_DIM).reshape(1, W)
    blk_np = np.kron(np.eye(RET_HEADS, dtype=np.float32), np.ones((HEAD_DIM, HEAD_DIM), np.float32))
    blk = jnp.asarray(blk_np, dtype=BF16)

    for l in range(depth):
        mod = _adaln(c, ada_w[l], ada_b[l]).reshape(bsz, 6, d)

        wr = jnp.concatenate([router_group_w[l], router_expert_w[l]], axis=1)
        wr = jnp.pad(wr, ((0, 0), (0, ROUTER_LANES - wr.shape[1])))
        wr_hi = wr.astype(BF16)
        wr_lo = (wr - wr_hi.astype(F32)).astype(BF16)
        wr2 = jnp.concatenate([wr_hi, wr_lo], axis=1)
        br = jnp.concatenate([router_group_b[l], router_expert_b[l]])
        br = jnp.pad(br, (0, ROUTER_LANES - br.shape[0])).reshape(1, ROUTER_LANES)

        x1, h2t, slab, counts = _mixer(
            x, mod, cos128, sin128, norm_mix_g[l].reshape(1, d), w_in[l].astype(BF16),
            conv_w[l], conv_b[l].reshape(1, W), beta_ret[l].reshape(1, W),
            beta_conv[l].reshape(1, W), w_out[l].astype(BF16), norm_ffn_g[l].reshape(1, d),
            wr2, br, lg, lgl, blk)

        te, n_used, p0, p1, pad_start, pad_n, p_rows = _routing_plan(slab, counts, n)
        xs = _dispatch(p0, p1, pad_start, pad_n, n_used, h2t, p_rows)
        de = expert_w_gate.shape[-1]
        y = _experts(te, n_used, xs,
                     expert_w_gate[l].reshape(N_EXPERTS, d, de),
                     expert_w_up[l].reshape(N_EXPERTS, d, de),
                     expert_w_down[l].reshape(N_EXPERTS, de, d))
        out = _combine(p0, p1, y, x1.reshape(n, d), slab, mod, norm_final_g.reshape(1, d), seq)
        x = out.reshape(bsz, seq, d)
    return x
```

```python
import functools

import jax
import jax.numpy as jnp
import numpy as np
from jax import lax
from jax.experimental import pallas as pl
from jax.experimental.pallas import tpu as pltpu

F32 = jnp.float32
BF16 = jnp.bfloat16

CHUNK = 64
RET_HEADS = 8
HEAD_DIM = 64
CONV_GROUP_DIM = 64
ROPE_BASE = 10000.0
N_GROUPS = 4
EXPERTS_PER_GROUP = 8
N_EXPERTS = N_GROUPS * EXPERTS_PER_GROUP
EPS = 1e-6
GN_EPS = 1e-5

LANES = 128
SUBLANES = 8
VMEM_LIMIT_BYTES = 56 * 1024 * 1024

SEQ_TILE = 256
EXPERT_TILE = 256
COMBINE_TILE = 256
ROUTER_LANES = LANES


def _silu(v):
    return v * (1.0 / (1.0 + jnp.exp(-v)))


def _adaln_kernel(c_ref, w_ref, b_ref, o_ref):
    s = _silu(c_ref[...])
    o_ref[...] = jnp.dot(s, w_ref[...], precision=lax.Precision.HIGHEST,
                         preferred_element_type=F32) + b_ref[...]


def _adaln(c, w, b):
    bsz, d = c.shape
    n = w.shape[1]
    tn = 1024
    return pl.pallas_call(
        _adaln_kernel,
        out_shape=jax.ShapeDtypeStruct((bsz, n), F32),
        grid=(n // tn,),
        in_specs=[pl.BlockSpec((bsz, d), lambda j: (0, 0)),
                  pl.BlockSpec((d, tn), lambda j: (0, j)),
                  pl.BlockSpec((1, tn), lambda j: (0, j))],
        out_specs=pl.BlockSpec((bsz, tn), lambda j: (0, j)),
        name="adaln",
    )(c, w, b.reshape(1, n))


def _rope_kernel(pos_ref, invf_ref, cos_ref, sin_ref):
    ang = pos_ref[...].astype(F32) * invf_ref[...]
    cos_ref[...] = jnp.cos(ang)
    sin_ref[...] = jnp.sin(ang)


def _rope_tables(positions):
    n = positions.size
    half = HEAD_DIM // 2
    inv_freq = ROPE_BASE ** (-jnp.arange(0, HEAD_DIM, 2, dtype=F32) / HEAD_DIM)
    per_row = LANES // half
    rows = n // per_row
    pos_rep = jnp.broadcast_to(positions.reshape(n, 1), (n, half)).reshape(rows, LANES)
    invf = jnp.tile(inv_freq, per_row).reshape(1, LANES)
    tr = 1024
    cos, sin = pl.pallas_call(
        _rope_kernel,
        out_shape=(jax.ShapeDtypeStruct((rows, LANES), F32),) * 2,
        grid=(rows // tr,),
        in_specs=[pl.BlockSpec((tr, LANES), lambda i: (i, 0)),
                  pl.BlockSpec((1, LANES), lambda i: (0, 0))],
        out_specs=(pl.BlockSpec((tr, LANES), lambda i: (i, 0)),) * 2,
        name="rope_table",
    )(pos_rep, invf)
    cos = cos.reshape(n, half)
    sin = sin.reshape(n, half)
    cos128 = jnp.concatenate([cos, cos, cos, cos], axis=1)
    sin128 = jnp.concatenate([-sin, sin, -sin, sin], axis=1)
    return cos128, sin128


def _mixer_kernel(lg_ref, x_ref, mod_ref, cos_ref, sin_ref, gmix_ref, win_ref, convw_ref,
                  convb_ref, bret_ref, bconv_ref, wout_ref, gffn_ref, wr_ref, br_ref, lgl_ref,
                  blk_ref, x1_ref, h2_ref, slab_ref, cnt_out_ref, state_ref, ubuf_ref, cnt_ref):
    L = SEQ_TILE
    W = RET_HEADS * HEAD_DIM
    s = pl.program_id(1)

    @pl.when(s == 0)
    def _():
        state_ref[...] = jnp.zeros_like(state_ref)
        ubuf_ref[0:SUBLANES, :] = jnp.zeros((SUBLANES, W), F32)

    @pl.when((s == 0) & (pl.program_id(0) == 0))
    def _():
        cnt_ref[...] = jnp.zeros_like(cnt_ref)

    x = x_ref[0]
    d_model = x.shape[-1]
    mod = mod_ref[0]
    shift_m, scale_m, gate_m = mod[0:1], mod[1:2], mod[2:3]
    shift_f, scale_f = mod[3:4], mod[4:5]

    ms = jnp.mean(x * x, axis=-1, keepdims=True)
    h = x * lax.rsqrt(ms + EPS) * gmix_ref[...]
    h = h * (1.0 + scale_m) + shift_m
    hb = h.astype(BF16)

    def proj(i):
        return jnp.dot(hb, win_ref[:, i * W:(i + 1) * W], preferred_element_type=F32)

    cos = jnp.concatenate([cos_ref[...]] * 4, axis=1)
    sin = jnp.concatenate([sin_ref[...]] * 4, axis=1)
    lane_w = lax.broadcasted_iota(jnp.int32, (L, W), 1)
    first_half = (lane_w & (HEAD_DIM - 1)) < (HEAD_DIM // 2)

    def rot(t):
        partner = jnp.where(first_half, pltpu.roll(t, W - HEAD_DIM // 2, 1),
                            pltpu.roll(t, HEAD_DIM // 2, 1))
        return t * cos + partner * sin

    q = rot(proj(0))
    k = rot(proj(1)) * (HEAD_DIM ** -0.5)
    v = proj(2)
    vb = v.astype(BF16)
    kb = k.astype(BF16)

    lgl = lgl_ref[...]
    rowf = lax.broadcasted_iota(jnp.int32, (L, W), 0).astype(F32)
    qd = q * jnp.exp(lgl * (rowf + 1.0))
    kd = k * jnp.exp(lgl * (float(L - 1) - rowf))
    blk = blk_ref[...]
    HW = W // 2
    blk_f = blk.astype(F32)
    qdb = qd.astype(BF16)
    kdb = kd.astype(BF16)
    state_decay = jnp.exp(lgl * float(L))
    inter = []
    for hf in range(2):
        sl = slice(hf * HW, (hf + 1) * HW)
        st = state_ref[hf]
        inter.append(jnp.dot(qdb[:, sl], st.astype(BF16), preferred_element_type=F32))
        kv = lax.dot_general(kdb[:, sl], vb[:, sl], (((0,), (0,)), ((), ())),
                             preferred_element_type=F32)
        state_ref[hf] = st * state_decay[:, sl] + kv * blk_f
    y_inter = jnp.concatenate(inter, axis=1)

    def head_sums(t):
        tb = t.astype(BF16)
        return jnp.concatenate(
            [jnp.dot(tb[:, hf * HW:(hf + 1) * HW], blk, preferred_element_type=F32)
             for hf in range(2)], axis=1)

    ii = lax.broadcasted_iota(jnp.int32, (L, L), 0)
    jj = lax.broadcasted_iota(jnp.int32, (L, L), 1)
    dist = jnp.abs(ii - jj).astype(F32)
    allowed = (jj // CHUNK) <= (ii // CHUNK)
    lane_p = lax.broadcasted_iota(jnp.int32, (L, LANES), 1)
    lo_head = lane_p < HEAD_DIM
    pairs = []
    for p in range(RET_HEADS // 2):
        sl = slice(p * LANES, (p + 1) * LANES)
        qp, kp, vp = q[:, sl], kb[:, sl], vb[:, sl]
        ys = []
        for hh in range(2):
            head = 2 * p + hh
            keep = lo_head if hh == 0 else jnp.logical_not(lo_head)
            qh = jnp.where(keep, qp, 0.0).astype(BF16)
            sc = lax.dot_general(qh, kp, (((1,), (1,)), ((), ())),
                                 preferred_element_type=F32)
            decay = jnp.where(allowed, jnp.exp(lg_ref[head] * dist), 0.0)
            ys.append(jnp.dot((sc * decay).astype(BF16), vp, preferred_element_type=F32))
        pairs.append(jnp.where(lo_head, ys[0], ys[1]))
    y = jnp.concatenate(pairs, axis=1) + y_inter

    inv_hd = 1.0 / HEAD_DIM
    mu = head_sums(y) * inv_hd
    d = y - mu
    var = head_sums(d * d) * inv_hd
    g = proj(3)
    y_ret = _silu(g) * (d * lax.rsqrt(var + GN_EPS)) * bret_ref[...]

    b_gate = proj(4)
    u = proj(5) * proj(6)
    ubuf_ref[SUBLANES:SUBLANES + L, :] = u
    u1 = ubuf_ref[SUBLANES - 1:SUBLANES - 1 + L, :]
    u2 = ubuf_ref[SUBLANES - 2:SUBLANES - 2 + L, :]
    ubuf_ref[0:SUBLANES, :] = ubuf_ref[L:L + SUBLANES, :]
    cw = convw_ref[...]
    conv = u2 * cw[0:1] + u1 * cw[1:2] + u * cw[2:3] + convb_ref[...]
    yc = b_gate * conv
    msc = head_sums(yc * yc) * (1.0 / CONV_GROUP_DIM)
    y_conv = yc * lax.rsqrt(msc + EPS) * bconv_ref[...]

    mix = (jnp.dot(y_ret.astype(BF16), wout_ref[0:W, :], preferred_element_type=F32)
           + jnp.dot(y_conv.astype(BF16), wout_ref[W:2 * W, :], preferred_element_type=F32))
    x1 = x + gate_m * mix
    x1_ref[0] = x1

    ms2 = jnp.mean(x1 * x1, axis=-1, keepdims=True)
    h2 = x1 * lax.rsqrt(ms2 + EPS) * gffn_ref[...]
    h2 = h2 * (1.0 + scale_f) + shift_f
    for cch in range(d_model // LANES):
        h2_ref[pl.ds(cch, L, stride=SUBLANES), :] = h2[:, cch * LANES:(cch + 1) * LANES]

    hi = h2.astype(BF16)
    lo = (h2 - hi.astype(F32)).astype(BF16)
    w2 = wr_ref[...]
    parts = (jnp.dot(hi, w2, preferred_element_type=F32)
             + jnp.dot(lo, w2, preferred_element_type=F32))
    logits = parts[:, :ROUTER_LANES] + parts[:, ROUTER_LANES:] + br_ref[...]

    lane = lax.broadcasted_iota(jnp.int32, (L, ROUTER_LANES), 1).astype(F32)
    neg_inf = F32(-jnp.inf)
    big = F32(1e9)
    gmask = lane < float(N_GROUPS)
    lgm = jnp.where(gmask, logits, neg_inf)
    gexp = jnp.exp(lgm - jnp.max(lgm, axis=-1, keepdims=True))
    gp = gexp / jnp.sum(gexp, axis=-1, keepdims=True)
    g_top = jnp.max(gp, axis=-1, keepdims=True)
    g_idx = jnp.min(jnp.where(gmask & (gp == g_top), lane, big), axis=-1, keepdims=True)

    e_lo = float(N_GROUPS) + float(EXPERTS_PER_GROUP) * g_idx
    emask = (lane >= e_lo) & (lane < e_lo + float(EXPERTS_PER_GROUP))
    lem = jnp.where(emask, logits, neg_inf)
    eexp = jnp.exp(lem - jnp.max(lem, axis=-1, keepdims=True))
    ep = eexp / jnp.sum(eexp, axis=-1, keepdims=True)
    p1 = jnp.max(jnp.where(emask, ep, -1.0), axis=-1, keepdims=True)
    i1 = jnp.min(jnp.where(emask & (ep == p1), lane, big), axis=-1, keepdims=True)
    m2 = emask & (lane != i1)
    p2 = jnp.max(jnp.where(m2, ep, -1.0), axis=-1, keepdims=True)
    i2 = jnp.min(jnp.where(m2 & (ep == p2), lane, big), axis=-1, keepdims=True)
    den = p1 + p2
    w1 = p1 / den * g_top
    w2_ = p2 / den * g_top
    e1 = i1 - float(N_GROUPS)
    e2 = i2 - float(N_GROUPS)

    oh1 = lane == e1
    oh2 = lane == e2
    ltri = jnp.where(jj < ii, 1.0, 0.0).astype(BF16)
    r1 = jnp.dot(ltri, jnp.where(oh1, 1.0, 0.0).astype(BF16), preferred_element_type=F32)
    r2 = jnp.dot(ltri, jnp.where(oh2, 1.0, 0.0).astype(BF16), preferred_element_type=F32)
    c1 = jnp.sum(jnp.where(oh1, 1.0, 0.0), axis=0, keepdims=True)
    c2 = jnp.sum(jnp.where(oh2, 1.0, 0.0), axis=0, keepdims=True)
    base = cnt_ref[...]
    rank1 = jnp.sum(jnp.where(oh1, base + r1, 0.0), axis=-1, keepdims=True)
    rank2 = jnp.sum(jnp.where(oh2, base + c1 + r2, 0.0), axis=-1, keepdims=True)
    total = base + c1 + c2
    cnt_ref[...] = total
    cnt_out_ref[...] = total

    slab = jnp.where(lane == 0.0, e1,
                     jnp.where(lane == 1.0, e2,
                               jnp.where(lane == 2.0, w1,
                                         jnp.where(lane == 3.0, w2_,
                                                   jnp.where(lane == 4.0, rank1,
                                                             jnp.where(lane == 5.0, rank2, 0.0))))))
    slab_ref[...] = slab


def _mixer(x, mod, cos128, sin128, gmix, win_b, convw, convb, bret, bconv, wout_b, gffn,
           wr2, br, lg, lgl, blk):
    bsz, seq, d = x.shape
    L = SEQ_TILE
    ns = seq // L
    W = RET_HEADS * HEAD_DIM
    n = bsz * seq
    const2 = lambda b, s: (0, 0)
    in_specs = [
        pl.BlockSpec(memory_space=pltpu.SMEM),
        pl.BlockSpec((1, L, d), lambda b, s: (b, s, 0)),
        pl.BlockSpec((1, 6, d), lambda b, s: (b, 0, 0)),
        pl.BlockSpec((L, LANES), lambda b, s: (b * ns + s, 0)),
        pl.BlockSpec((L, LANES), lambda b, s: (b * ns + s, 0)),
        pl.BlockSpec((1, d), const2),
        pl.BlockSpec(win_b.shape, const2),
        pl.BlockSpec(convw.shape, const2),
        pl.BlockSpec((1, W), const2),
        pl.BlockSpec((1, W), const2),
        pl.BlockSpec((1, W), const2),
        pl.BlockSpec(wout_b.shape, const2),
        pl.BlockSpec((1, d), const2),
        pl.BlockSpec(wr2.shape, const2),
        pl.BlockSpec((1, ROUTER_LANES), const2),
        pl.BlockSpec((1, W), const2),
        pl.BlockSpec((W // 2, W // 2), const2),
    ]
    assert d == SUBLANES * LANES, "one token must fill exactly one (8, 128) f32 tile"
    out_shape = (jax.ShapeDtypeStruct((bsz, seq, d), F32),
                 jax.ShapeDtypeStruct((n * SUBLANES, LANES), F32),
                 jax.ShapeDtypeStruct((n, ROUTER_LANES), F32),
                 jax.ShapeDtypeStruct((1, ROUTER_LANES), F32))
    out_specs = (pl.BlockSpec((1, L, d), lambda b, s: (b, s, 0)),
                 pl.BlockSpec((L * SUBLANES, LANES), lambda b, s: (b * ns + s, 0)),
                 pl.BlockSpec((L, ROUTER_LANES), lambda b, s: (b * ns + s, 0)),
                 pl.BlockSpec((1, ROUTER_LANES), const2))
    return pl.pallas_call(
        _mixer_kernel,
        out_shape=out_shape,
        grid=(bsz, ns),
        in_specs=in_specs,
        out_specs=out_specs,
        scratch_shapes=[pltpu.VMEM((2, W // 2, W // 2), F32),
                        pltpu.VMEM((L + 2 * SUBLANES, W), F32),
                        pltpu.VMEM((1, ROUTER_LANES), F32)],
        compiler_params=pltpu.CompilerParams(
            dimension_semantics=("arbitrary", "arbitrary"),
            vmem_limit_bytes=VMEM_LIMIT_BYTES),
        name="mixer",
    )(lg, x, mod, cos128, sin128, gmix, win_b, convw, convb, bret, bconv, wout_b, gffn,
      wr2, br, lgl, blk)


def _load_token_rows(ref, rows):
    return jnp.concatenate(
        [ref[pl.ds(c, rows, stride=SUBLANES), :] for c in range(SUBLANES)], axis=1)


def _store_token_rows(ref, val):
    rows = val.shape[0]
    for c in range(SUBLANES):
        ref[pl.ds(c, rows, stride=SUBLANES), :] = val[:, c * LANES:(c + 1) * LANES]


DISPATCH_TILE = 512
DISPATCH_UNROLL = 8
PAD_UNITS = tuple(1 << b for b in reversed(range(EXPERT_TILE.bit_length() - 1)))


def _token_rows(ref, row):
    return ref.at[pl.ds(pl.multiple_of(row * SUBLANES, SUBLANES), SUBLANES), :]


def _dispatch_kernel(d0_ref, d1_ref, ps_ref, pn_ref, nused_ref, h2_ref, xs_hbm, zbuf, sem, zsem):
    dt = DISPATCH_TILE
    i = pl.program_id(0)
    zrows = PAD_UNITS[0]

    def pad_copy(start, unit):
        return pltpu.make_async_copy(zbuf.at[pl.ds(0, unit * SUBLANES), :],
                                     xs_hbm.at[pl.ds(pl.multiple_of(start * SUBLANES, SUBLANES),
                                                     unit * SUBLANES), :], zsem)

    def pad_pass(do):
        def per_expert(e, carry):
            start = ps_ref[e]
            npad = pn_ref[e]
            for unit in PAD_UNITS:
                @pl.when((npad & unit) != 0)
                def _():
                    do(pad_copy(start + (npad & ~(2 * unit - 1)), unit))
            return carry
        lax.fori_loop(0, N_EXPERTS, per_expert, 0)

    def tail_pass(do):
        def per_unit(k, carry):
            do(pad_copy(k * zrows, zrows))
            return carry
        per_tile = EXPERT_TILE // zrows
        n_units = xs_hbm.shape[0] // (zrows * SUBLANES)
        lax.fori_loop(nused_ref[0] * per_tile, n_units, per_unit, 0)

    @pl.when(i == 0)
    def _():
        zbuf[...] = jnp.zeros_like(zbuf)
        pad_pass(lambda cp: cp.start())
        tail_pass(lambda cp: cp.start())

    base = i * dt

    def body(r, carry):
        src = _token_rows(h2_ref, r)
        for prio, d_ref in enumerate((d0_ref, d1_ref)):
            pltpu.make_async_copy(src, _token_rows(xs_hbm, d_ref[base + r]), sem).start(
                priority=prio)
        return carry

    lax.fori_loop(0, dt, body, 0, unroll=DISPATCH_UNROLL)
    for _ in range(2):
        pltpu.make_async_copy(h2_ref, xs_hbm.at[pl.ds(0, dt * SUBLANES), :], sem).wait()

    @pl.when(i == pl.num_programs(0) - 1)
    def _():
        pad_pass(lambda cp: cp.wait())
        tail_pass(lambda cp: cp.wait())


def _dispatch(dest0, dest1, pad_start, pad_n, n_used, h2t, p_rows):
    n = dest0.shape[0]
    dt = DISPATCH_TILE
    grid_spec = pltpu.PrefetchScalarGridSpec(
        num_scalar_prefetch=5,
        grid=(n // dt,),
        in_specs=[pl.BlockSpec((dt * SUBLANES, LANES), lambda i, *_: (i, 0))],
        out_specs=pl.BlockSpec(memory_space=pl.ANY),
        scratch_shapes=[pltpu.VMEM((PAD_UNITS[0] * SUBLANES, LANES), F32),
                        pltpu.SemaphoreType.DMA(()), pltpu.SemaphoreType.DMA(())],
    )
    return pl.pallas_call(
        _dispatch_kernel,
        out_shape=jax.ShapeDtypeStruct((p_rows * SUBLANES, LANES), F32),
        grid_spec=grid_spec,
        compiler_params=pltpu.CompilerParams(dimension_semantics=("arbitrary",)),
        name="dispatch",
    )(dest0, dest1, pad_start, pad_n, n_used, h2t)


def _expert_kernel(te_ref, nused_ref, xs_ref, wg_ref, wu_ref, wd_ref, y_ref, wgb, wub, wdb):
    tm = EXPERT_TILE
    i = pl.program_id(0)
    n_used = nused_ref[0]

    @pl.when(i < n_used)
    def _():
        e = te_ref[i]
        e_prev = te_ref[jnp.maximum(i - 1, 0)]

        @pl.when((i == 0) | (e != e_prev))
        def _():
            wgb[...] = wg_ref[0].astype(BF16)
            wub[...] = wu_ref[0].astype(BF16)
            wdb[...] = wd_ref[0].astype(BF16)

        xb = _load_token_rows(xs_ref, tm).astype(BF16)
        a = jnp.dot(xb, wgb[...], preferred_element_type=F32)
        u = jnp.dot(xb, wub[...], preferred_element_type=F32)
        hid = (_silu(a) * u).astype(BF16)
        _store_token_rows(y_ref, jnp.dot(hid, wdb[...], preferred_element_type=F32))

    @pl.when(i >= n_used)
    def _():
        y_ref[...] = jnp.zeros_like(y_ref)


def _experts(tile_expert, n_used, xs, wg, wu, wd):
    tm = EXPERT_TILE
    p_rows = xs.shape[0] // SUBLANES
    n_tiles = p_rows // tm
    d, de = wg.shape[1], wg.shape[2]
    used_tile = lambda i, te, nu: (jnp.minimum(i, nu[0] - 1), 0)
    grid_spec = pltpu.PrefetchScalarGridSpec(
        num_scalar_prefetch=2,
        grid=(n_tiles,),
        in_specs=[
            pl.BlockSpec((tm * SUBLANES, LANES), used_tile),
            pl.BlockSpec((1, d, de), lambda i, te, nu: (te[i], 0, 0)),
            pl.BlockSpec((1, d, de), lambda i, te, nu: (te[i], 0, 0)),
            pl.BlockSpec((1, de, d), lambda i, te, nu: (te[i], 0, 0)),
        ],
        out_specs=pl.BlockSpec((tm * SUBLANES, LANES), lambda i, te, nu: (i, 0)),
        scratch_shapes=[pltpu.VMEM((d, de), BF16),
                        pltpu.VMEM((d, de), BF16),
                        pltpu.VMEM((de, d), BF16)],
    )
    return pl.pallas_call(
        _expert_kernel,
        out_shape=jax.ShapeDtypeStruct((p_rows * SUBLANES, LANES), F32),
        grid_spec=grid_spec,
        compiler_params=pltpu.CompilerParams(
            dimension_semantics=("arbitrary",),
            vmem_limit_bytes=VMEM_LIMIT_BYTES),
        name="experts",
    )(tile_expert, n_used, xs, wg, wu, wd)


def _combine_kernel(p0_ref, p1_ref, y_hbm, x1_ref, slab_ref, mod_ref, gfin_ref, o_ref,
                    ybuf, sem):
    tm = COMBINE_TILE
    i = pl.program_id(0)
    nt = pl.num_programs(0)

    def issue(tile, slot):
        base = tile * tm

        def body(r, carry):
            dst_row = pl.multiple_of(r * SUBLANES, SUBLANES)
            for j, p_ref in enumerate((p0_ref, p1_ref)):
                src_row = pl.multiple_of(p_ref[base + r] * SUBLANES, SUBLANES)
                pltpu.make_async_copy(y_hbm.at[pl.ds(src_row, SUBLANES), :],
                                      ybuf.at[slot, j, pl.ds(dst_row, SUBLANES), :],
                                      sem.at[slot]).start(priority=j)
            return carry

        lax.fori_loop(0, tm, body, 0, unroll=DISPATCH_UNROLL)

    @pl.when(i == 0)
    def _():
        issue(0, 0)

    @pl.when(i + 1 < nt)
    def _():
        issue(i + 1, (i + 1) % 2)

    slot = i % 2
    for j in range(2):
        pltpu.make_async_copy(y_hbm.at[pl.ds(0, tm * SUBLANES), :], ybuf.at[slot, j],
                              sem.at[slot]).wait()

    slab = slab_ref[...]
    w0 = slab[:, 2:3]
    w1 = slab[:, 3:4]
    gate_f = mod_ref[0][5:6]
    moe = (w0 * _load_token_rows(ybuf.at[slot, 0], tm)
           + w1 * _load_token_rows(ybuf.at[slot, 1], tm))
    xo = x1_ref[...] + gate_f * moe
    ms = jnp.mean(xo * xo, axis=-1, keepdims=True)
    o_ref[...] = xo * lax.rsqrt(ms + EPS) * gfin_ref[...]


def _combine(p0, p1, y, x1, slab, mod, gfin, seq):
    n, d = x1.shape
    tm = COMBINE_TILE
    tiles_per_seq = seq // tm
    grid_spec = pltpu.PrefetchScalarGridSpec(
        num_scalar_prefetch=2,
        grid=(n // tm,),
        in_specs=[
            pl.BlockSpec(memory_space=pl.ANY),
            pl.BlockSpec((tm, d), lambda i, a, b: (i, 0)),
            pl.BlockSpec((tm, ROUTER_LANES), lambda i, a, b: (i, 0)),
            pl.BlockSpec((1, 6, d), lambda i, a, b: (i // tiles_per_seq, 0, 0)),
            pl.BlockSpec((1, d), lambda i, a, b: (0, 0)),
        ],
        out_specs=pl.BlockSpec((tm, d), lambda i, a, b: (i, 0)),
        scratch_shapes=[pltpu.VMEM((2, 2, tm * SUBLANES, LANES), F32),
                        pltpu.SemaphoreType.DMA((2,))],
    )
    return pl.pallas_call(
        _combine_kernel,
        out_shape=jax.ShapeDtypeStruct((n, d), F32),
        grid_spec=grid_spec,
        compiler_params=pltpu.CompilerParams(
            dimension_semantics=("arbitrary",),
            vmem_limit_bytes=VMEM_LIMIT_BYTES),
        name="combine",
    )(p0, p1, y, x1, slab, mod, gfin)


def _routing_plan(slab, counts_f, n):
    tm = EXPERT_TILE
    counts = counts_f[0, :N_EXPERTS].astype(jnp.int32)
    padded = ((counts + tm - 1) // tm) * tm
    ends = jnp.cumsum(padded)
    starts = ends - padded
    eid = slab[:, 0:2].astype(jnp.int32)
    rank = slab[:, 4:6].astype(jnp.int32)
    onehot = eid[:, :, None] == jnp.arange(N_EXPERTS, dtype=jnp.int32)[None, None, :]
    dest = jnp.sum(jnp.where(onehot, starts[None, None, :], 0), axis=-1) + rank
    p_rows = 2 * n + N_EXPERTS * tm
    n_tiles = p_rows // tm
    n_used = (ends[-1] // tm).astype(jnp.int32)
    tile_start = jnp.arange(n_tiles, dtype=jnp.int32) * tm
    te = jnp.sum((ends[None, :] <= tile_start[:, None]).astype(jnp.int32), axis=1)
    te = jnp.minimum(te, N_EXPERTS - 1)
    last = jnp.sum(jnp.where(jnp.arange(n_tiles) == n_used - 1, te, 0))
    te = jnp.where(jnp.arange(n_tiles) < n_used, te, last).astype(jnp.int32)
    return (te, n_used.reshape(1), dest[:, 0], dest[:, 1], starts + counts, padded - counts,
            p_rows)


def kernel(x, c, positions, ada_w, ada_b, norm_mix_g, norm_ffn_g, w_in, conv_w, conv_b,
           beta_ret, beta_conv, w_out, router_group_w, router_group_b, router_expert_w,
           router_expert_b, expert_w_gate, expert_w_up, expert_w_down, norm_final_g):
    bsz, seq, d = x.shape
    n = bsz * seq
    depth = ada_w.shape[0]
    assert depth == 1, "the combine kernel fuses the trunk's final RMSNorm (single layer)"
    W = RET_HEADS * HEAD_DIM

    cos128, sin128 = _rope_tables(positions)
    heads = jnp.arange(RET_HEADS, dtype=F32)
    lg = jnp.log1p(-jnp.exp2(-5.0 - heads))
    lgl = jnp.repeat(lg, HEAD_DIM).reshape(1, W)
    assert CONV_GROUP_DIM == HEAD_DIM, "conv groups and retention heads share the 64-lane block sums"
    blk_np = np.kron(np.eye(RET_HEADS // 2, dtype=np.float32),
                     np.ones((HEAD_DIM, HEAD_DIM), np.float32))
    blk = jnp.asarray(blk_np, dtype=BF16)

    for l in range(depth):
        mod = _adaln(c, ada_w[l], ada_b[l]).reshape(bsz, 6, d)

        wr = jnp.concatenate([router_group_w[l], router_expert_w[l]], axis=1)
        wr = jnp.pad(wr, ((0, 0), (0, ROUTER_LANES - wr.shape[1])))
        wr_hi = wr.astype(BF16)
        wr_lo = (wr - wr_hi.astype(F32)).astype(BF16)
        wr2 = jnp.concatenate([wr_hi, wr_lo], axis=1)
        br = jnp.concatenate([router_group_b[l], router_expert_b[l]])
        br = jnp.pad(br, (0, ROUTER_LANES - br.shape[0])).reshape(1, ROUTER_LANES)

        x1, h2t, slab, counts = _mixer(
            x, mod, cos128, sin128, norm_mix_g[l].reshape(1, d), w_in[l].astype(BF16),
            conv_w[l], conv_b[l].reshape(1, W), beta_ret[l].reshape(1, W),
            beta_conv[l].reshape(1, W), w_out[l].astype(BF16), norm_ffn_g[l].reshape(1, d),
            wr2, br, lg, lgl, blk)

        te, n_used, p0, p1, pad_start, pad_n, p_rows = _routing_plan(slab, counts, n)
        xs = _dispatch(p0, p1, pad_start, pad_n, n_used, h2t, p_rows)
        de = expert_w_gate.shape[-1]
        y = _experts(te, n_used, xs,
                     expert_w_gate[l].reshape(N_EXPERTS, d, de),
                     expert_w_up[l].reshape(N_EXPERTS, d, de),
                     expert_w_down[l].reshape(N_EXPERTS, de, d))
        out = _combine(p0, p1, y, x1.reshape(n, d), slab, mod, norm_final_g.reshape(1, d), seq)
        x = out.reshape(bsz, seq, d)
    return x
```

```python
import functools

import jax
import jax.numpy as jnp
import numpy as np
from jax import lax
from jax.experimental import pallas as pl
from jax.experimental.pallas import tpu as pltpu

F32 = jnp.float32
BF16 = jnp.bfloat16

CHUNK = 64
RET_HEADS = 8
HEAD_DIM = 64
CONV_GROUP_DIM = 64
ROPE_BASE = 10000.0
N_GROUPS = 4
EXPERTS_PER_GROUP = 8
N_EXPERTS = N_GROUPS * EXPERTS_PER_GROUP
EPS = 1e-6
GN_EPS = 1e-5

LANES = 128
SUBLANES = 8
VMEM_LIMIT_BYTES = 56 * 1024 * 1024

SEQ_TILE = 256
EXPERT_TILE = 256
COMBINE_TILE = 256
ROUTER_LANES = LANES


def _silu(v):
    return v * (1.0 / (1.0 + jnp.exp(-v)))


def _adaln_kernel(c_ref, w_ref, b_ref, o_ref):
    s = _silu(c_ref[...])
    o_ref[...] = jnp.dot(s, w_ref[...], precision=lax.Precision.HIGHEST,
                         preferred_element_type=F32) + b_ref[...]


def _adaln(c, w, b):
    bsz, d = c.shape
    n = w.shape[1]
    tn = 1024
    return pl.pallas_call(
        _adaln_kernel,
        out_shape=jax.ShapeDtypeStruct((bsz, n), F32),
        grid=(n // tn,),
        in_specs=[pl.BlockSpec((bsz, d), lambda j: (0, 0)),
                  pl.BlockSpec((d, tn), lambda j: (0, j)),
                  pl.BlockSpec((1, tn), lambda j: (0, j))],
        out_specs=pl.BlockSpec((bsz, tn), lambda j: (0, j)),
        name="adaln",
    )(c, w, b.reshape(1, n))


def _rope_kernel(pos_ref, invf_ref, cos_ref, sin_ref):
    ang = pos_ref[...].astype(F32) * invf_ref[...]
    cos_ref[...] = jnp.cos(ang)
    sin_ref[...] = jnp.sin(ang)


def _rope_tables(positions):
    n = positions.size
    half = HEAD_DIM // 2
    inv_freq = ROPE_BASE ** (-jnp.arange(0, HEAD_DIM, 2, dtype=F32) / HEAD_DIM)
    per_row = LANES // half
    rows = n // per_row
    pos_rep = jnp.broadcast_to(positions.reshape(n, 1), (n, half)).reshape(rows, LANES)
    invf = jnp.tile(inv_freq, per_row).reshape(1, LANES)
    tr = 1024
    cos, sin = pl.pallas_call(
        _rope_kernel,
        out_shape=(jax.ShapeDtypeStruct((rows, LANES), F32),) * 2,
        grid=(rows // tr,),
        in_specs=[pl.BlockSpec((tr, LANES), lambda i: (i, 0)),
                  pl.BlockSpec((1, LANES), lambda i: (0, 0))],
        out_specs=(pl.BlockSpec((tr, LANES), lambda i: (i, 0)),) * 2,
        name="rope_table",
    )(pos_rep, invf)
    cos = cos.reshape(n, half)
    sin = sin.reshape(n, half)
    cos128 = jnp.concatenate([cos, cos, cos, cos], axis=1)
    sin128 = jnp.concatenate([-sin, sin, -sin, sin], axis=1)
    return cos128, sin128


def _mixer_kernel(lg_ref, x_ref, mod_ref, cos_ref, sin_ref, gmix_ref, win_ref, convw_ref,
                  convb_ref, bret_ref, bconv_ref, wout_ref, gffn_ref, wr_ref, br_ref, lgl_ref,
                  blk_ref, x1_ref, h2_ref, slab_ref, cnt_out_ref, state_ref, ubuf_ref, cnt_ref):
    L = SEQ_TILE
    W = RET_HEADS * HEAD_DIM
    s = pl.program_id(1)

    @pl.when(s == 0)
    def _():
        state_ref[...] = jnp.zeros_like(state_ref)
        ubuf_ref[0:SUBLANES, :] = jnp.zeros((SUBLANES, W), F32)

    @pl.when((s == 0) & (pl.program_id(0) == 0))
    def _():
        cnt_ref[...] = jnp.zeros_like(cnt_ref)

    x = x_ref[0]
    d_model = x.shape[-1]
    mod = mod_ref[0]
    shift_m, scale_m, gate_m = mod[0:1], mod[1:2], mod[2:3]
    shift_f, scale_f = mod[3:4], mod[4:5]

    ms = jnp.mean(x * x, axis=-1, keepdims=True)
    h = x * lax.rsqrt(ms + EPS) * gmix_ref[...]
    h = h * (1.0 + scale_m) + shift_m
    hb = h.astype(BF16)

    def proj(i):
        return jnp.dot(hb, win_ref[:, i * W:(i + 1) * W], preferred_element_type=F32)

    cos = jnp.concatenate([cos_ref[...]] * 4, axis=1)
    sin = jnp.concatenate([sin_ref[...]] * 4, axis=1)
    lane_w = lax.broadcasted_iota(jnp.int32, (L, W), 1)
    first_half = (lane_w & (HEAD_DIM - 1)) < (HEAD_DIM // 2)

    def rot(t):
        partner = jnp.where(first_half, pltpu.roll(t, W - HEAD_DIM // 2, 1),
                            pltpu.roll(t, HEAD_DIM // 2, 1))
        return t * cos + partner * sin

    q = rot(proj(0))
    k = rot(proj(1)) * (HEAD_DIM ** -0.5)
    v = proj(2)
    vb = v.astype(BF16)
    kb = k.astype(BF16)

    lgl = lgl_ref[...]
    rowf = lax.broadcasted_iota(jnp.int32, (L, W), 0).astype(F32)
    qd = q * jnp.exp(lgl * (rowf + 1.0))
    kd = k * jnp.exp(lgl * (float(L - 1) - rowf))
    blk = blk_ref[...]
    HW = W // 2
    blk_f = blk.astype(F32)
    qdb = qd.astype(BF16)
    kdb = kd.astype(BF16)
    state_decay = jnp.exp(lgl * float(L))
    inter = []
    for hf in range(2):
        sl = slice(hf * HW, (hf + 1) * HW)
        st = state_ref[hf]
        inter.append(jnp.dot(qdb[:, sl], st.astype(BF16), preferred_element_type=F32))
        kv = lax.dot_general(kdb[:, sl], vb[:, sl], (((0,), (0,)), ((), ())),
                             preferred_element_type=F32)
        state_ref[hf] = st * state_decay[:, sl] + kv * blk_f
    y_inter = jnp.concatenate(inter, axis=1)

    def head_sums(t):
        tb = t.astype(BF16)
        return jnp.concatenate(
            [jnp.dot(tb[:, hf * HW:(hf + 1) * HW], blk, preferred_element_type=F32)
             for hf in range(2)], axis=1)

    ii = lax.broadcasted_iota(jnp.int32, (L, L), 0)
    jj = lax.broadcasted_iota(jnp.int32, (L, L), 1)
    dist = jnp.abs(ii - jj).astype(F32)
    allowed = (jj // CHUNK) <= (ii // CHUNK)
    lane_p = lax.broadcasted_iota(jnp.int32, (L, LANES), 1)
    lo_head = lane_p < HEAD_DIM
    pairs = []
    for p in range(RET_HEADS // 2):
        sl = slice(p * LANES, (p + 1) * LANES)
        qp, kp, vp = q[:, sl], kb[:, sl], vb[:, sl]
        ys = []
        for hh in range(2):
            head = 2 * p + hh
            keep = lo_head if hh == 0 else jnp.logical_not(lo_head)
            qh = jnp.where(keep, qp, 0.0).astype(BF16)
            sc = lax.dot_general(qh, kp, (((1,), (1,)), ((), ())),
                                 preferred_element_type=F32)
            decay = jnp.where(allowed, jnp.exp(lg_ref[head] * dist), 0.0)
            ys.append(jnp.dot((sc * decay).astype(BF16), vp, preferred_element_type=F32))
        pairs.append(jnp.where(lo_head, ys[0], ys[1]))
    y = jnp.concatenate(pairs, axis=1) + y_inter

    inv_hd = 1.0 / HEAD_DIM
    mu = head_sums(y) * inv_hd
    d = y - mu
    var = head_sums(d * d) * inv_hd
    g = proj(3)
    y_ret = _silu(g) * (d * lax.rsqrt(var + GN_EPS)) * bret_ref[...]

    b_gate = proj(4)
    u = proj(5) * proj(6)
    ubuf_ref[SUBLANES:SUBLANES + L, :] = u
    u1 = ubuf_ref[SUBLANES - 1:SUBLANES - 1 + L, :]
    u2 = ubuf_ref[SUBLANES - 2:SUBLANES - 2 + L, :]
    ubuf_ref[0:SUBLANES, :] = ubuf_ref[L:L + SUBLANES, :]
    cw = convw_ref[...]
    conv = u2 * cw[0:1] + u1 * cw[1:2] + u * cw[2:3] + convb_ref[...]
    yc = b_gate * conv
    msc = head_sums(yc * yc) * (1.0 / CONV_GROUP_DIM)
    y_conv = yc * lax.rsqrt(msc + EPS) * bconv_ref[...]

    mix = (jnp.dot(y_ret.astype(BF16), wout_ref[0:W, :], preferred_element_type=F32)
           + jnp.dot(y_conv.astype(BF16), wout_ref[W:2 * W, :], preferred_element_type=F32))
    x1 = x + gate_m * mix
    x1_ref[0] = x1

    ms2 = jnp.mean(x1 * x1, axis=-1, keepdims=True)
    h2 = x1 * lax.rsqrt(ms2 + EPS) * gffn_ref[...]
    h2 = h2 * (1.0 + scale_f) + shift_f
    for cch in range(d_model // LANES):
        h2_ref[pl.ds(cch, L, stride=SUBLANES), :] = h2[:, cch * LANES:(cch + 1) * LANES]

    hi = h2.astype(BF16)
    lo = (h2 - hi.astype(F32)).astype(BF16)
    w2 = wr_ref[...]
    parts = (jnp.dot(hi, w2, preferred_element_type=F32)
             + jnp.dot(lo, w2, preferred_element_type=F32))
    logits = parts[:, :ROUTER_LANES] + parts[:, ROUTER_LANES:] + br_ref[...]

    lane = lax.broadcasted_iota(jnp.int32, (L, ROUTER_LANES), 1).astype(F32)
    neg_inf = F32(-jnp.inf)
    big = F32(1e9)
    gmask = lane < float(N_GROUPS)
    lgm = jnp.where(gmask, logits, neg_inf)
    gexp = jnp.exp(lgm - jnp.max(lgm, axis=-1, keepdims=True))
    gp = gexp / jnp.sum(gexp, axis=-1, keepdims=True)
    g_top = jnp.max(gp, axis=-1, keepdims=True)
    g_idx = jnp.min(jnp.where(gmask & (gp == g_top), lane, big), axis=-1, keepdims=True)

    e_lo = float(N_GROUPS) + float(EXPERTS_PER_GROUP) * g_idx
    emask = (lane >= e_lo) & (lane < e_lo + float(EXPERTS_PER_GROUP))
    lem = jnp.where(emask, logits, neg_inf)
    eexp = jnp.exp(lem - jnp.max(lem, axis=-1, keepdims=True))
    ep = eexp / jnp.sum(eexp, axis=-1, keepdims=True)
    p1 = jnp.max(jnp.where(emask, ep, -1.0), axis=-1, keepdims=True)
    i1 = jnp.min(jnp.where(emask & (ep == p1), lane, big), axis=-1, keepdims=True)
    m2 = emask & (lane != i1)
    p2 = jnp.max(jnp.where(m2, ep, -1.0), axis=-1, keepdims=True)
    i2 = jnp.min(jnp.where(m2 & (ep == p2), lane, big), axis=-1, keepdims=True)
    den = p1 + p2
    w1 = p1 / den * g_top
    w2_ = p2 / den * g_top
    e1 = i1 - float(N_GROUPS)
    e2 = i2 - float(N_GROUPS)

    oh1 = lane == e1
    oh2 = lane == e2
    ltri = jnp.where(jj < ii, 1.0, 0.0).astype(BF16)
    r1 = jnp.dot(ltri, jnp.where(oh1, 1.0, 0.0).astype(BF16), preferred_element_type=F32)
    r2 = jnp.dot(ltri, jnp.where(oh2, 1.0, 0.0).astype(BF16), preferred_element_type=F32)
    c1 = jnp.sum(jnp.where(oh1, 1.0, 0.0), axis=0, keepdims=True)
    c2 = jnp.sum(jnp.where(oh2, 1.0, 0.0), axis=0, keepdims=True)
    base = cnt_ref[...]
    rank1 = jnp.sum(jnp.where(oh1, base + r1, 0.0), axis=-1, keepdims=True)
    rank2 = jnp.sum(jnp.where(oh2, base + c1 + r2, 0.0), axis=-1, keepdims=True)
    total = base + c1 + c2
    cnt_ref[...] = total
    cnt_out_ref[...] = total

    slab = jnp.where(lane == 0.0, e1,
                     jnp.where(lane == 1.0, e2,
                               jnp.where(lane == 2.0, w1,
                                         jnp.where(lane == 3.0, w2_,
                                                   jnp.where(lane == 4.0, rank1,
                                                             jnp.where(lane == 5.0, rank2, 0.0))))))
    slab_ref[...] = slab


def _mixer(x, mod, cos128, sin128, gmix, win_b, convw, convb, bret, bconv, wout_b, gffn,
           wr2, br, lg, lgl, blk):
    bsz, seq, d = x.shape
    L = SEQ_TILE
    ns = seq // L
    W = RET_HEADS * HEAD_DIM
    n = bsz * seq
    const2 = lambda b, s: (0, 0)
    in_specs = [
        pl.BlockSpec(memory_space=pltpu.SMEM),
        pl.BlockSpec((1, L, d), lambda b, s: (b, s, 0)),
        pl.BlockSpec((1, 6, d), lambda b, s: (b, 0, 0)),
        pl.BlockSpec((L, LANES), lambda b, s: (b * ns + s, 0)),
        pl.BlockSpec((L, LANES), lambda b, s: (b * ns + s, 0)),
        pl.BlockSpec((1, d), const2),
        pl.BlockSpec(win_b.shape, const2),
        pl.BlockSpec(convw.shape, const2),
        pl.BlockSpec((1, W), const2),
        pl.BlockSpec((1, W), const2),
        pl.BlockSpec((1, W), const2),
        pl.BlockSpec(wout_b.shape, const2),
        pl.BlockSpec((1, d), const2),
        pl.BlockSpec(wr2.shape, const2),
        pl.BlockSpec((1, ROUTER_LANES), const2),
        pl.BlockSpec((1, W), const2),
        pl.BlockSpec((W // 2, W // 2), const2),
    ]
    assert d == SUBLANES * LANES, "one token must fill exactly one (8, 128) f32 tile"
    out_shape = (jax.ShapeDtypeStruct((bsz, seq, d), F32),
                 jax.ShapeDtypeStruct((n * SUBLANES, LANES), F32),
                 jax.ShapeDtypeStruct((n, ROUTER_LANES), F32),
                 jax.ShapeDtypeStruct((1, ROUTER_LANES), F32))
    out_specs = (pl.BlockSpec((1, L, d), lambda b, s: (b, s, 0)),
                 pl.BlockSpec((L * SUBLANES, LANES), lambda b, s: (b * ns + s, 0)),
                 pl.BlockSpec((L, ROUTER_LANES), lambda b, s: (b * ns + s, 0)),
                 pl.BlockSpec((1, ROUTER_LANES), const2))
    return pl.pallas_call(
        _mixer_kernel,
        out_shape=out_shape,
        grid=(bsz, ns),
        in_specs=in_specs,
        out_specs=out_specs,
        scratch_shapes=[pltpu.VMEM((2, W // 2, W // 2), F32),
                        pltpu.VMEM((L + 2 * SUBLANES, W), F32),
                        pltpu.VMEM((1, ROUTER_LANES), F32)],
        compiler_params=pltpu.CompilerParams(
            dimension_semantics=("arbitrary", "arbitrary"),
            vmem_limit_bytes=VMEM_LIMIT_BYTES),
        name="mixer",
    )(lg, x, mod, cos128, sin128, gmix, win_b, convw, convb, bret, bconv, wout_b, gffn,
      wr2, br, lgl, blk)


def _load_token_rows(ref, rows):
    return jnp.concatenate(
        [ref[pl.ds(c, rows, stride=SUBLANES), :] for c in range(SUBLANES)], axis=1)


def _store_token_rows(ref, val):
    rows = val.shape[0]
    for c in range(SUBLANES):
        ref[pl.ds(c, rows, stride=SUBLANES), :] = val[:, c * LANES:(c + 1) * LANES]


DISPATCH_TILE = 512
DISPATCH_UNROLL = 8
PAD_UNITS = tuple(1 << b for b in reversed(range(EXPERT_TILE.bit_length() - 1)))


def _token_rows(ref, row):
    return ref.at[pl.ds(pl.multiple_of(row * SUBLANES, SUBLANES), SUBLANES), :]


def _dispatch_kernel(d0_ref, d1_ref, ps_ref, pn_ref, nused_ref, h2_ref, xs_hbm, zbuf, sem, zsem):
    dt = DISPATCH_TILE
    i = pl.program_id(0)
    zrows = PAD_UNITS[0]

    def pad_copy(start, unit):
        return pltpu.make_async_copy(zbuf.at[pl.ds(0, unit * SUBLANES), :],
                                     xs_hbm.at[pl.ds(pl.multiple_of(start * SUBLANES, SUBLANES),
                                                     unit * SUBLANES), :], zsem)

    def pad_pass(do):
        def per_expert(e, carry):
            start = ps_ref[e]
            npad = pn_ref[e]
            for unit in PAD_UNITS:
                @pl.when((npad & unit) != 0)
                def _():
                    do(pad_copy(start + (npad & ~(2 * unit - 1)), unit))
            return carry
        lax.fori_loop(0, N_EXPERTS, per_expert, 0)

    def tail_pass(do):
        def per_unit(k, carry):
            do(pad_copy(k * zrows, zrows))
            return carry
        per_tile = EXPERT_TILE // zrows
        n_units = xs_hbm.shape[0] // (zrows * SUBLANES)
        lax.fori_loop(nused_ref[0] * per_tile, n_units, per_unit, 0)

    @pl.when(i == 0)
    def _():
        zbuf[...] = jnp.zeros_like(zbuf)
        pad_pass(lambda cp: cp.start())
        tail_pass(lambda cp: cp.start())

    base = i * dt

    def body(r, carry):
        src = _token_rows(h2_ref, r)
        for prio, d_ref in enumerate((d0_ref, d1_ref)):
            pltpu.make_async_copy(src, _token_rows(xs_hbm, d_ref[base + r]), sem).start(
                priority=prio)
        return carry

    lax.fori_loop(0, dt, body, 0, unroll=DISPATCH_UNROLL)
    for _ in range(2):
        pltpu.make_async_copy(h2_ref, xs_hbm.at[pl.ds(0, dt * SUBLANES), :], sem).wait()

    @pl.when(i == pl.num_programs(0) - 1)
    def _():
        pad_pass(lambda cp: cp.wait())
        tail_pass(lambda cp: cp.wait())


def _dispatch(dest0, dest1, pad_start, pad_n, n_used, h2t, p_rows):
    n = dest0.shape[0]
    dt = DISPATCH_TILE
    grid_spec = pltpu.PrefetchScalarGridSpec(
        num_scalar_prefetch=5,
        grid=(n // dt,),
        in_specs=[pl.BlockSpec((dt * SUBLANES, LANES), lambda i, *_: (i, 0))],
        out_specs=pl.BlockSpec(memory_space=pl.ANY),
        scratch_shapes=[pltpu.VMEM((PAD_UNITS[0] * SUBLANES, LANES), F32),
                        pltpu.SemaphoreType.DMA(()), pltpu.SemaphoreType.DMA(())],
    )
    return pl.pallas_call(
        _dispatch_kernel,
        out_shape=jax.ShapeDtypeStruct((p_rows * SUBLANES, LANES), F32),
        grid_spec=grid_spec,
        compiler_params=pltpu.CompilerParams(dimension_semantics=("arbitrary",)),
        name="dispatch",
    )(dest0, dest1, pad_start, pad_n, n_used, h2t)


def _expert_kernel(first_ref, ntile_ref, nused_ref, xs_hbm, wg_ref, wu_ref, wd_ref, y_hbm,
                   xbuf, ybuf, wgb, wub, wdb, isem, osem):
    tm = EXPERT_TILE
    rows = tm * SUBLANES
    e = pl.program_id(0)
    n_used = nused_ref[0]
    n_tiles = y_hbm.shape[0] // rows

    def tile_rows(ref, g):
        return ref.at[pl.ds(pl.multiple_of(g * rows, rows), rows), :]

    def in_copy(g, slot):
        return pltpu.make_async_copy(tile_rows(xs_hbm, g), xbuf.at[slot], isem.at[slot])

    def out_copy(g, slot):
        return pltpu.make_async_copy(ybuf.at[slot], tile_rows(y_hbm, g), osem.at[slot])

    @pl.when(e == 0)
    def _():
        in_copy(0, 0).start()

    @pl.when(ntile_ref[e] > 0)
    def _():
        wgb[...] = wg_ref[0].astype(BF16)
        wub[...] = wu_ref[0].astype(BF16)
        wdb[...] = wd_ref[0].astype(BF16)

    def tile(g, carry):
        slot = g % 2
        in_copy(g, slot).wait()

        @pl.when(g + 1 < n_used)
        def _():
            in_copy(g + 1, 1 - slot).start()

        xb = _load_token_rows(xbuf.at[slot], tm).astype(BF16)
        a = jnp.dot(xb, wgb[...], preferred_element_type=F32)
        u = jnp.dot(xb, wub[...], preferred_element_type=F32)
        hid = (_silu(a) * u).astype(BF16)
        y = jnp.dot(hid, wdb[...], preferred_element_type=F32)

        @pl.when(g >= 2)
        def _():
            out_copy(g - 2, slot).wait()

        _store_token_rows(ybuf.at[slot], y)
        out_copy(g, slot).start()
        return carry

    first = first_ref[e]
    lax.fori_loop(first, first + ntile_ref[e], tile, 0)

    @pl.when(e == pl.num_programs(0) - 1)
    def _():
        @pl.when(n_used >= 2)
        def _():
            out_copy(n_used - 2, n_used % 2).wait()
        out_copy(n_used - 1, (n_used - 1) % 2).wait()

        ybuf[0] = jnp.zeros((rows, LANES), F32)

        def fill(g, carry):
            out_copy(g, 0).start()
            return carry

        def drain(g, carry):
            out_copy(g, 0).wait()
            return carry

        lax.fori_loop(n_used, n_tiles, fill, 0)
        lax.fori_loop(n_used, n_tiles, drain, 0)


def _experts(first_tile, n_tile, n_used, xs, wg, wu, wd):
    tm = EXPERT_TILE
    n_exp, d, de = wg.shape
    grid_spec = pltpu.PrefetchScalarGridSpec(
        num_scalar_prefetch=3,
        grid=(n_exp,),
        in_specs=[
            pl.BlockSpec(memory_space=pl.ANY),
            pl.BlockSpec((1, d, de), lambda e, *_: (e, 0, 0)),
            pl.BlockSpec((1, d, de), lambda e, *_: (e, 0, 0)),
            pl.BlockSpec((1, de, d), lambda e, *_: (e, 0, 0)),
        ],
        out_specs=pl.BlockSpec(memory_space=pl.ANY),
        scratch_shapes=[pltpu.VMEM((2, tm * SUBLANES, LANES), F32),
                        pltpu.VMEM((2, tm * SUBLANES, LANES), F32),
                        pltpu.VMEM((d, de), BF16),
                        pltpu.VMEM((d, de), BF16),
                        pltpu.VMEM((de, d), BF16),
                        pltpu.SemaphoreType.DMA((2,)),
                        pltpu.SemaphoreType.DMA((2,))],
    )
    return pl.pallas_call(
        _expert_kernel,
        out_shape=jax.ShapeDtypeStruct(xs.shape, F32),
        grid_spec=grid_spec,
        compiler_params=pltpu.CompilerParams(
            dimension_semantics=("arbitrary",),
            vmem_limit_bytes=VMEM_LIMIT_BYTES),
        name="experts",
    )(first_tile, n_tile, n_used, xs, wg, wu, wd)


def _combine_kernel(p0_ref, p1_ref, y_hbm, x1_ref, slab_ref, mod_ref, gfin_ref, o_ref,
                    ybuf, sem):
    tm = COMBINE_TILE
    i = pl.program_id(0)
    nt = pl.num_programs(0)

    def issue(tile, slot):
        base = tile * tm

        def body(r, carry):
            dst_row = pl.multiple_of(r * SUBLANES, SUBLANES)
            for j, p_ref in enumerate((p0_ref, p1_ref)):
                src_row = pl.multiple_of(p_ref[base + r] * SUBLANES, SUBLANES)
                pltpu.make_async_copy(y_hbm.at[pl.ds(src_row, SUBLANES), :],
                                      ybuf.at[slot, j, pl.ds(dst_row, SUBLANES), :],
                                      sem.at[slot]).start(priority=j)
            return carry

        lax.fori_loop(0, tm, body, 0, unroll=DISPATCH_UNROLL)

    @pl.when(i == 0)
    def _():
        issue(0, 0)

    @pl.when(i + 1 < nt)
    def _():
        issue(i + 1, (i + 1) % 2)

    slot = i % 2
    for j in range(2):
        pltpu.make_async_copy(y_hbm.at[pl.ds(0, tm * SUBLANES), :], ybuf.at[slot, j],
                              sem.at[slot]).wait()

    slab = slab_ref[...]
    w0 = slab[:, 2:3]
    w1 = slab[:, 3:4]
    gate_f = mod_ref[0][5:6]
    moe = (w0 * _load_token_rows(ybuf.at[slot, 0], tm)
           + w1 * _load_token_rows(ybuf.at[slot, 1], tm))
    xo = x1_ref[...] + gate_f * moe
    ms = jnp.mean(xo * xo, axis=-1, keepdims=True)
    o_ref[...] = xo * lax.rsqrt(ms + EPS) * gfin_ref[...]


def _combine(p0, p1, y, x1, slab, mod, gfin, seq):
    n, d = x1.shape
    tm = COMBINE_TILE
    tiles_per_seq = seq // tm
    grid_spec = pltpu.PrefetchScalarGridSpec(
        num_scalar_prefetch=2,
        grid=(n // tm,),
        in_specs=[
            pl.BlockSpec(memory_space=pl.ANY),
            pl.BlockSpec((tm, d), lambda i, a, b: (i, 0)),
            pl.BlockSpec((tm, ROUTER_LANES), lambda i, a, b: (i, 0)),
            pl.BlockSpec((1, 6, d), lambda i, a, b: (i // tiles_per_seq, 0, 0)),
            pl.BlockSpec((1, d), lambda i, a, b: (0, 0)),
        ],
        out_specs=pl.BlockSpec((tm, d), lambda i, a, b: (i, 0)),
        scratch_shapes=[pltpu.VMEM((2, 2, tm * SUBLANES, LANES), F32),
                        pltpu.SemaphoreType.DMA((2,))],
    )
    return pl.pallas_call(
        _combine_kernel,
        out_shape=jax.ShapeDtypeStruct((n, d), F32),
        grid_spec=grid_spec,
        compiler_params=pltpu.CompilerParams(
            dimension_semantics=("arbitrary",),
            vmem_limit_bytes=VMEM_LIMIT_BYTES),
        name="combine",
    )(p0, p1, y, x1, slab, mod, gfin)


def _routing_plan(slab, counts_f, n):
    tm = EXPERT_TILE
    counts = counts_f[0, :N_EXPERTS].astype(jnp.int32)
    padded = ((counts + tm - 1) // tm) * tm
    ends = jnp.cumsum(padded)
    starts = ends - padded
    eid = slab[:, 0:2].astype(jnp.int32)
    rank = slab[:, 4:6].astype(jnp.int32)
    onehot = eid[:, :, None] == jnp.arange(N_EXPERTS, dtype=jnp.int32)[None, None, :]
    dest = jnp.sum(jnp.where(onehot, starts[None, None, :], 0), axis=-1) + rank
    p_rows = 2 * n + N_EXPERTS * tm
    n_used = (ends[-1] // tm).astype(jnp.int32)
    return (starts // tm, padded // tm, n_used.reshape(1), dest[:, 0], dest[:, 1],
            starts + counts, padded - counts, p_rows)


def kernel(x, c, positions, ada_w, ada_b, norm_mix_g, norm_ffn_g, w_in, conv_w, conv_b,
           beta_ret, beta_conv, w_out, router_group_w, router_group_b, router_expert_w,
           router_expert_b, expert_w_gate, expert_w_up, expert_w_down, norm_final_g):
    bsz, seq, d = x.shape
    n = bsz * seq
    depth = ada_w.shape[0]
    assert depth == 1, "the combine kernel fuses the trunk's final RMSNorm (single layer)"
    W = RET_HEADS * HEAD_DIM

    cos128, sin128 = _rope_tables(positions)
    heads = jnp.arange(RET_HEADS, dtype=F32)
    lg = jnp.log1p(-jnp.exp2(-5.0 - heads))
    lgl = jnp.repeat(lg, HEAD_DIM).reshape(1, W)
    assert CONV_GROUP_DIM == HEAD_DIM, "conv groups and retention heads share the 64-lane block sums"
    blk_np = np.kron(np.eye(RET_HEADS // 2, dtype=np.float32),
                     np.ones((HEAD_DIM, HEAD_DIM), np.float32))
    blk = jnp.asarray(blk_np, dtype=BF16)

    for l in range(depth):
        mod = _adaln(c, ada_w[l], ada_b[l]).reshape(bsz, 6, d)

        wr = jnp.concatenate([router_group_w[l], router_expert_w[l]], axis=1)
        wr = jnp.pad(wr, ((0, 0), (0, ROUTER_LANES - wr.shape[1])))
        wr_hi = wr.astype(BF16)
        wr_lo = (wr - wr_hi.astype(F32)).astype(BF16)
        wr2 = jnp.concatenate([wr_hi, wr_lo], axis=1)
        br = jnp.concatenate([router_group_b[l], router_expert_b[l]])
        br = jnp.pad(br, (0, ROUTER_LANES - br.shape[0])).reshape(1, ROUTER_LANES)

        x1, h2t, slab, counts = _mixer(
            x, mod, cos128, sin128, norm_mix_g[l].reshape(1, d), w_in[l].astype(BF16),
            conv_w[l], conv_b[l].reshape(1, W), beta_ret[l].reshape(1, W),
            beta_conv[l].reshape(1, W), w_out[l].astype(BF16), norm_ffn_g[l].reshape(1, d),
            wr2, br, lg, lgl, blk)

        first_tile, n_tile, n_used, p0, p1, pad_start, pad_n, p_rows = _routing_plan(
            slab, counts, n)
        xs = _dispatch(p0, p1, pad_start, pad_n, n_used, h2t, p_rows)
        de = expert_w_gate.shape[-1]
        y = _experts(first_tile, n_tile, n_used, xs,
                     expert_w_gate[l].reshape(N_EXPERTS, d, de),
                     expert_w_up[l].reshape(N_EXPERTS, d, de),
                     expert_w_down[l].reshape(N_EXPERTS, de, d))
        out = _combine(p0, p1, y, x1.reshape(n, d), slab, mod, norm_final_g.reshape(1, d), seq)
        x = out.reshape(bsz, seq, d)
    return x
```

```python
import functools

import jax
import jax.numpy as jnp
import numpy as np
from jax import lax
from jax.experimental import pallas as pl
from jax.experimental.pallas import tpu as pltpu

F32 = jnp.float32
BF16 = jnp.bfloat16

CHUNK = 64
RET_HEADS = 8
HEAD_DIM = 64
CONV_GROUP_DIM = 64
ROPE_BASE = 10000.0
N_GROUPS = 4
EXPERTS_PER_GROUP = 8
N_EXPERTS = N_GROUPS * EXPERTS_PER_GROUP
EPS = 1e-6
GN_EPS = 1e-5

LANES = 128
SUBLANES = 8
VMEM_LIMIT_BYTES = 56 * 1024 * 1024

SEQ_TILE = 256
EXPERT_TILE = 256
COMBINE_TILE = 256
ROUTER_LANES = LANES


def _silu(v):
    return v * (1.0 / (1.0 + jnp.exp(-v)))


def _adaln_kernel(c_ref, w_ref, b_ref, o_ref):
    s = _silu(c_ref[...])
    o_ref[...] = jnp.dot(s, w_ref[...], precision=lax.Precision.HIGHEST,
                         preferred_element_type=F32) + b_ref[...]


def _adaln(c, w, b):
    bsz, d = c.shape
    n = w.shape[1]
    tn = 1024
    return pl.pallas_call(
        _adaln_kernel,
        out_shape=jax.ShapeDtypeStruct((bsz, n), F32),
        grid=(n // tn,),
        in_specs=[pl.BlockSpec((bsz, d), lambda j: (0, 0)),
                  pl.BlockSpec((d, tn), lambda j: (0, j)),
                  pl.BlockSpec((1, tn), lambda j: (0, j))],
        out_specs=pl.BlockSpec((bsz, tn), lambda j: (0, j)),
        name="adaln",
    )(c, w, b.reshape(1, n))


def _rope_kernel(pos_ref, invf_ref, cos_ref, sin_ref):
    ang = pos_ref[...].astype(F32) * invf_ref[...]
    cos_ref[...] = jnp.cos(ang)
    sin_ref[...] = jnp.sin(ang)


def _rope_tables(positions):
    n = positions.size
    half = HEAD_DIM // 2
    inv_freq = ROPE_BASE ** (-jnp.arange(0, HEAD_DIM, 2, dtype=F32) / HEAD_DIM)
    per_row = LANES // half
    rows = n // per_row
    pos_rep = jnp.broadcast_to(positions.reshape(n, 1), (n, half)).reshape(rows, LANES)
    invf = jnp.tile(inv_freq, per_row).reshape(1, LANES)
    tr = 1024
    cos, sin = pl.pallas_call(
        _rope_kernel,
        out_shape=(jax.ShapeDtypeStruct((rows, LANES), F32),) * 2,
        grid=(rows // tr,),
        in_specs=[pl.BlockSpec((tr, LANES), lambda i: (i, 0)),
                  pl.BlockSpec((1, LANES), lambda i: (0, 0))],
        out_specs=(pl.BlockSpec((tr, LANES), lambda i: (i, 0)),) * 2,
        name="rope_table",
    )(pos_rep, invf)
    cos = cos.reshape(n, half)
    sin = sin.reshape(n, half)
    cos128 = jnp.concatenate([cos, cos, cos, cos], axis=1)
    sin128 = jnp.concatenate([-sin, sin, -sin, sin], axis=1)
    return cos128, sin128


def _mixer_kernel(lg_ref, x_ref, mod_ref, cos_ref, sin_ref, gmix_ref, win_ref, convw_ref,
                  convb_ref, bret_ref, bconv_ref, wout_ref, gffn_ref, wr_ref, br_ref, lgl_ref,
                  blk_ref, x1_ref, h2_ref, logit_ref, state_ref, ubuf_ref):
    L = SEQ_TILE
    W = RET_HEADS * HEAD_DIM
    s = pl.program_id(1)

    @pl.when(s == 0)
    def _():
        state_ref[...] = jnp.zeros_like(state_ref)
        ubuf_ref[0:SUBLANES, :] = jnp.zeros((SUBLANES, W), F32)

    x = x_ref[0]
    d_model = x.shape[-1]
    mod = mod_ref[0]
    shift_m, scale_m, gate_m = mod[0:1], mod[1:2], mod[2:3]
    shift_f, scale_f = mod[3:4], mod[4:5]

    ms = jnp.mean(x * x, axis=-1, keepdims=True)
    h = x * lax.rsqrt(ms + EPS) * gmix_ref[...]
    h = h * (1.0 + scale_m) + shift_m
    hb = h.astype(BF16)

    def proj(i):
        return jnp.dot(hb, win_ref[:, i * W:(i + 1) * W], preferred_element_type=F32)

    cos = jnp.concatenate([cos_ref[...]] * 4, axis=1)
    sin = jnp.concatenate([sin_ref[...]] * 4, axis=1)
    lane_w = lax.broadcasted_iota(jnp.int32, (L, W), 1)
    first_half = (lane_w & (HEAD_DIM - 1)) < (HEAD_DIM // 2)

    def rot(t):
        partner = jnp.where(first_half, pltpu.roll(t, W - HEAD_DIM // 2, 1),
                            pltpu.roll(t, HEAD_DIM // 2, 1))
        return t * cos + partner * sin

    q = rot(proj(0))
    k = rot(proj(1)) * (HEAD_DIM ** -0.5)
    v = proj(2)
    vb = v.astype(BF16)
    kb = k.astype(BF16)

    lgl = lgl_ref[...]
    rowf = lax.broadcasted_iota(jnp.int32, (L, W), 0).astype(F32)
    qd = q * jnp.exp(lgl * (rowf + 1.0))
    kd = k * jnp.exp(lgl * (float(L - 1) - rowf))
    blk = blk_ref[...]
    HW = W // 2
    blk_f = blk.astype(F32)
    qdb = qd.astype(BF16)
    kdb = kd.astype(BF16)
    state_decay = jnp.exp(lgl * float(L))
    inter = []
    for hf in range(2):
        sl = slice(hf * HW, (hf + 1) * HW)
        st = state_ref[hf]
        inter.append(jnp.dot(qdb[:, sl], st.astype(BF16), preferred_element_type=F32))
        kv = lax.dot_general(kdb[:, sl], vb[:, sl], (((0,), (0,)), ((), ())),
                             preferred_element_type=F32)
        state_ref[hf] = st * state_decay[:, sl] + kv * blk_f
    y_inter = jnp.concatenate(inter, axis=1)

    def head_sums(t):
        tb = t.astype(BF16)
        return jnp.concatenate(
            [jnp.dot(tb[:, hf * HW:(hf + 1) * HW], blk, preferred_element_type=F32)
             for hf in range(2)], axis=1)

    ii = lax.broadcasted_iota(jnp.int32, (L, L), 0)
    jj = lax.broadcasted_iota(jnp.int32, (L, L), 1)
    dist = jnp.abs(ii - jj).astype(F32)
    allowed = (jj // CHUNK) <= (ii // CHUNK)
    lane_p = lax.broadcasted_iota(jnp.int32, (L, LANES), 1)
    lo_head = lane_p < HEAD_DIM
    pairs = []
    for p in range(RET_HEADS // 2):
        sl = slice(p * LANES, (p + 1) * LANES)
        qp, kp, vp = q[:, sl], kb[:, sl], vb[:, sl]
        ys = []
        for hh in range(2):
            head = 2 * p + hh
            keep = lo_head if hh == 0 else jnp.logical_not(lo_head)
            qh = jnp.where(keep, qp, 0.0).astype(BF16)
            sc = lax.dot_general(qh, kp, (((1,), (1,)), ((), ())),
                                 preferred_element_type=F32)
            decay = jnp.where(allowed, jnp.exp(lg_ref[head] * dist), 0.0)
            ys.append(jnp.dot((sc * decay).astype(BF16), vp, preferred_element_type=F32))
        pairs.append(jnp.where(lo_head, ys[0], ys[1]))
    y = jnp.concatenate(pairs, axis=1) + y_inter

    inv_hd = 1.0 / HEAD_DIM
    mu = head_sums(y) * inv_hd
    d = y - mu
    var = head_sums(d * d) * inv_hd
    g = proj(3)
    y_ret = _silu(g) * (d * lax.rsqrt(var + GN_EPS)) * bret_ref[...]

    b_gate = proj(4)
    u = proj(5) * proj(6)
    ubuf_ref[SUBLANES:SUBLANES + L, :] = u
    u1 = ubuf_ref[SUBLANES - 1:SUBLANES - 1 + L, :]
    u2 = ubuf_ref[SUBLANES - 2:SUBLANES - 2 + L, :]
    ubuf_ref[0:SUBLANES, :] = ubuf_ref[L:L + SUBLANES, :]
    cw = convw_ref[...]
    conv = u2 * cw[0:1] + u1 * cw[1:2] + u * cw[2:3] + convb_ref[...]
    yc = b_gate * conv
    msc = head_sums(yc * yc) * (1.0 / CONV_GROUP_DIM)
    y_conv = yc * lax.rsqrt(msc + EPS) * bconv_ref[...]

    mix = (jnp.dot(y_ret.astype(BF16), wout_ref[0:W, :], preferred_element_type=F32)
           + jnp.dot(y_conv.astype(BF16), wout_ref[W:2 * W, :], preferred_element_type=F32))
    x1 = x + gate_m * mix
    x1_ref[0] = x1

    ms2 = jnp.mean(x1 * x1, axis=-1, keepdims=True)
    h2 = x1 * lax.rsqrt(ms2 + EPS) * gffn_ref[...]
    h2 = h2 * (1.0 + scale_f) + shift_f
    for cch in range(d_model // LANES):
        h2_ref[pl.ds(cch, L, stride=SUBLANES), :] = h2[:, cch * LANES:(cch + 1) * LANES]

    hi = h2.astype(BF16)
    lo = (h2 - hi.astype(F32)).astype(BF16)
    w2 = wr_ref[...]
    nt_dims = (((1,), (1,)), ((), ()))
    parts = (lax.dot_general(hi, w2, nt_dims, preferred_element_type=F32)
             + lax.dot_general(lo, w2, nt_dims, preferred_element_type=F32))
    logit_ref[...] = parts[:, :ROUTER_LANES] + parts[:, ROUTER_LANES:] + br_ref[...]


ROUTER_TILE = 2048
RANK_BLOCK = 256


def _router_kernel(logit_ref, slab_ref, cnt_out_ref, cnt_ref):
    T = ROUTER_TILE

    @pl.when(pl.program_id(0) == 0)
    def _():
        cnt_ref[...] = jnp.zeros_like(cnt_ref)

    logits = logit_ref[...]
    lane = lax.broadcasted_iota(jnp.int32, (T, ROUTER_LANES), 1).astype(F32)
    neg_inf = F32(-jnp.inf)
    big = F32(1e9)
    gmask = lane < float(N_GROUPS)
    lgm = jnp.where(gmask, logits, neg_inf)
    gexp = jnp.exp(lgm - jnp.max(lgm, axis=-1, keepdims=True))
    gp = gexp / jnp.sum(gexp, axis=-1, keepdims=True)
    g_top = jnp.max(gp, axis=-1, keepdims=True)
    g_idx = jnp.min(jnp.where(gmask & (gp == g_top), lane, big), axis=-1, keepdims=True)

    e_lo = float(N_GROUPS) + float(EXPERTS_PER_GROUP) * g_idx
    emask = (lane >= e_lo) & (lane < e_lo + float(EXPERTS_PER_GROUP))
    lem = jnp.where(emask, logits, neg_inf)
    eexp = jnp.exp(lem - jnp.max(lem, axis=-1, keepdims=True))
    ep = eexp / jnp.sum(eexp, axis=-1, keepdims=True)
    p1 = jnp.max(jnp.where(emask, ep, -1.0), axis=-1, keepdims=True)
    i1 = jnp.min(jnp.where(emask & (ep == p1), lane, big), axis=-1, keepdims=True)
    m2 = emask & (lane != i1)
    p2 = jnp.max(jnp.where(m2, ep, -1.0), axis=-1, keepdims=True)
    i2 = jnp.min(jnp.where(m2 & (ep == p2), lane, big), axis=-1, keepdims=True)
    den = p1 + p2
    w1 = p1 / den * g_top
    w2 = p2 / den * g_top
    e1 = i1 - float(N_GROUPS)
    e2 = i2 - float(N_GROUPS)

    oh1 = jnp.where(lane == e1, 1.0, 0.0)
    oh2 = jnp.where(lane == e2, 1.0, 0.0)
    rb = RANK_BLOCK
    ii = lax.broadcasted_iota(jnp.int32, (rb, rb), 0)
    jj = lax.broadcasted_iota(jnp.int32, (rb, rb), 1)
    ltri = jnp.where(jj < ii, 1.0, 0.0).astype(BF16)
    base = cnt_ref[...]
    ranks1, ranks2 = [], []
    for k in range(T // rb):
        o1 = oh1[k * rb:(k + 1) * rb]
        o2 = oh2[k * rb:(k + 1) * rb]
        r1 = jnp.dot(ltri, o1.astype(BF16), preferred_element_type=F32)
        r2 = jnp.dot(ltri, o2.astype(BF16), preferred_element_type=F32)
        c1 = jnp.sum(o1, axis=0, keepdims=True)
        c2 = jnp.sum(o2, axis=0, keepdims=True)
        ranks1.append(jnp.sum(o1 * (base + r1), axis=-1, keepdims=True))
        ranks2.append(jnp.sum(o2 * (base + c1 + r2), axis=-1, keepdims=True))
        base = base + c1 + c2
    rank1 = jnp.concatenate(ranks1, axis=0)
    rank2 = jnp.concatenate(ranks2, axis=0)
    cnt_ref[...] = base
    cnt_out_ref[...] = base

    slab = jnp.where(lane == 0.0, e1,
                     jnp.where(lane == 1.0, e2,
                               jnp.where(lane == 2.0, w1,
                                         jnp.where(lane == 3.0, w2,
                                                   jnp.where(lane == 4.0, rank1,
                                                             jnp.where(lane == 5.0, rank2, 0.0))))))
    slab_ref[...] = slab


def _router(logits):
    n = logits.shape[0]
    T = ROUTER_TILE
    return pl.pallas_call(
        _router_kernel,
        out_shape=(jax.ShapeDtypeStruct((n, ROUTER_LANES), F32),
                   jax.ShapeDtypeStruct((1, ROUTER_LANES), F32)),
        grid=(n // T,),
        in_specs=[pl.BlockSpec((T, ROUTER_LANES), lambda i: (i, 0))],
        out_specs=(pl.BlockSpec((T, ROUTER_LANES), lambda i: (i, 0)),
                   pl.BlockSpec((1, ROUTER_LANES), lambda i: (0, 0))),
        scratch_shapes=[pltpu.VMEM((1, ROUTER_LANES), F32)],
        compiler_params=pltpu.CompilerParams(
            dimension_semantics=("arbitrary",), vmem_limit_bytes=VMEM_LIMIT_BYTES),
        name="router",
    )(logits)


def _mixer(x, mod, cos128, sin128, gmix, win_b, convw, convb, bret, bconv, wout_b, gffn,
           wr2, br, lg, lgl, blk):
    bsz, seq, d = x.shape
    L = SEQ_TILE
    ns = seq // L
    W = RET_HEADS * HEAD_DIM
    n = bsz * seq
    const2 = lambda b, s: (0, 0)
    in_specs = [
        pl.BlockSpec(memory_space=pltpu.SMEM),
        pl.BlockSpec((1, L, d), lambda b, s: (b, s, 0)),
        pl.BlockSpec((1, 6, d), lambda b, s: (b, 0, 0)),
        pl.BlockSpec((L, LANES), lambda b, s: (b * ns + s, 0)),
        pl.BlockSpec((L, LANES), lambda b, s: (b * ns + s, 0)),
        pl.BlockSpec((1, d), const2),
        pl.BlockSpec(win_b.shape, const2),
        pl.BlockSpec(convw.shape, const2),
        pl.BlockSpec((1, W), const2),
        pl.BlockSpec((1, W), const2),
        pl.BlockSpec((1, W), const2),
        pl.BlockSpec(wout_b.shape, const2),
        pl.BlockSpec((1, d), const2),
        pl.BlockSpec(wr2.shape, const2),
        pl.BlockSpec((1, ROUTER_LANES), const2),
        pl.BlockSpec((1, W), const2),
        pl.BlockSpec((W // 2, W // 2), const2),
    ]
    assert d == SUBLANES * LANES, "one token must fill exactly one (8, 128) f32 tile"
    out_shape = (jax.ShapeDtypeStruct((bsz, seq, d), F32),
                 jax.ShapeDtypeStruct((n * SUBLANES, LANES), F32),
                 jax.ShapeDtypeStruct((n, ROUTER_LANES), F32))
    out_specs = (pl.BlockSpec((1, L, d), lambda b, s: (b, s, 0)),
                 pl.BlockSpec((L * SUBLANES, LANES), lambda b, s: (b * ns + s, 0)),
                 pl.BlockSpec((L, ROUTER_LANES), lambda b, s: (b * ns + s, 0)))
    return pl.pallas_call(
        _mixer_kernel,
        out_shape=out_shape,
        grid=(bsz, ns),
        in_specs=in_specs,
        out_specs=out_specs,
        scratch_shapes=[pltpu.VMEM((2, W // 2, W // 2), F32),
                        pltpu.VMEM((L + 2 * SUBLANES, W), F32)],
        compiler_params=pltpu.CompilerParams(
            dimension_semantics=("arbitrary", "arbitrary"),
            vmem_limit_bytes=VMEM_LIMIT_BYTES),
        name="mixer",
    )(lg, x, mod, cos128, sin128, gmix, win_b, convw, convb, bret, bconv, wout_b, gffn,
      wr2, br, lgl, blk)


def _load_token_rows(ref, rows):
    return jnp.concatenate(
        [ref[pl.ds(c, rows, stride=SUBLANES), :] for c in range(SUBLANES)], axis=1)


def _store_token_rows(ref, val):
    rows = val.shape[0]
    for c in range(SUBLANES):
        ref[pl.ds(c, rows, stride=SUBLANES), :] = val[:, c * LANES:(c + 1) * LANES]


DISPATCH_TILE = 512
DISPATCH_UNROLL = 8
PAD_UNITS = tuple(1 << b for b in reversed(range(EXPERT_TILE.bit_length() - 1)))


def _token_rows(ref, row):
    return ref.at[pl.ds(pl.multiple_of(row * SUBLANES, SUBLANES), SUBLANES), :]


def _dispatch_kernel(d0_ref, d1_ref, ps_ref, pn_ref, nused_ref, h2_ref, xs_hbm, zbuf, sem, zsem):
    dt = DISPATCH_TILE
    i = pl.program_id(0)
    zrows = PAD_UNITS[0]

    def pad_copy(start, unit):
        return pltpu.make_async_copy(zbuf.at[pl.ds(0, unit * SUBLANES), :],
                                     xs_hbm.at[pl.ds(pl.multiple_of(start * SUBLANES, SUBLANES),
                                                     unit * SUBLANES), :], zsem)

    def pad_pass(do):
        def per_expert(e, carry):
            start = ps_ref[e]
            npad = pn_ref[e]
            for unit in PAD_UNITS:
                @pl.when((npad & unit) != 0)
                def _():
                    do(pad_copy(start + (npad & ~(2 * unit - 1)), unit))
            return carry
        lax.fori_loop(0, N_EXPERTS, per_expert, 0)

    def tail_pass(do):
        def per_unit(k, carry):
            do(pad_copy(k * zrows, zrows))
            return carry
        per_tile = EXPERT_TILE // zrows
        n_units = xs_hbm.shape[0] // (zrows * SUBLANES)
        lax.fori_loop(nused_ref[0] * per_tile, n_units, per_unit, 0)

    @pl.when(i == 0)
    def _():
        zbuf[...] = jnp.zeros_like(zbuf)
        pad_pass(lambda cp: cp.start())
        tail_pass(lambda cp: cp.start())

    base = i * dt

    def body(r, carry):
        src = _token_rows(h2_ref, r)
        for prio, d_ref in enumerate((d0_ref, d1_ref)):
            pltpu.make_async_copy(src, _token_rows(xs_hbm, d_ref[base + r]), sem).start(
                priority=prio)
        return carry

    lax.fori_loop(0, dt, body, 0, unroll=DISPATCH_UNROLL)
    for _ in range(2):
        pltpu.make_async_copy(h2_ref, xs_hbm.at[pl.ds(0, dt * SUBLANES), :], sem).wait()

    @pl.when(i == pl.num_programs(0) - 1)
    def _():
        pad_pass(lambda cp: cp.wait())
        tail_pass(lambda cp: cp.wait())


def _dispatch(dest0, dest1, pad_start, pad_n, n_used, h2t, p_rows):
    n = dest0.shape[0]
    dt = DISPATCH_TILE
    grid_spec = pltpu.PrefetchScalarGridSpec(
        num_scalar_prefetch=5,
        grid=(n // dt,),
        in_specs=[pl.BlockSpec((dt * SUBLANES, LANES), lambda i, *_: (i, 0))],
        out_specs=pl.BlockSpec(memory_space=pl.ANY),
        scratch_shapes=[pltpu.VMEM((PAD_UNITS[0] * SUBLANES, LANES), F32),
                        pltpu.SemaphoreType.DMA(()), pltpu.SemaphoreType.DMA(())],
    )
    return pl.pallas_call(
        _dispatch_kernel,
        out_shape=jax.ShapeDtypeStruct((p_rows * SUBLANES, LANES), F32),
        grid_spec=grid_spec,
        compiler_params=pltpu.CompilerParams(dimension_semantics=("arbitrary",)),
        name="dispatch",
    )(dest0, dest1, pad_start, pad_n, n_used, h2t)


def _expert_kernel(first_ref, ntile_ref, nused_ref, xs_hbm, wg_ref, wu_ref, wd_ref, y_hbm,
                   xbuf, ybuf, wgb, wub, wdb, isem, osem):
    tm = EXPERT_TILE
    rows = tm * SUBLANES
    e = pl.program_id(0)
    n_used = nused_ref[0]
    n_tiles = y_hbm.shape[0] // rows

    def tile_rows(ref, g):
        return ref.at[pl.ds(pl.multiple_of(g * rows, rows), rows), :]

    def in_copy(g, slot):
        return pltpu.make_async_copy(tile_rows(xs_hbm, g), xbuf.at[slot], isem.at[slot])

    def out_copy(g, slot):
        return pltpu.make_async_copy(ybuf.at[slot], tile_rows(y_hbm, g), osem.at[slot])

    @pl.when(e == 0)
    def _():
        in_copy(0, 0).start()

    @pl.when(ntile_ref[e] > 0)
    def _():
        wgb[...] = wg_ref[0].astype(BF16)
        wub[...] = wu_ref[0].astype(BF16)
        wdb[...] = wd_ref[0].astype(BF16)

    def tile(g, carry):
        slot = g % 2
        in_copy(g, slot).wait()

        @pl.when(g + 1 < n_used)
        def _():
            in_copy(g + 1, 1 - slot).start()

        xb = _load_token_rows(xbuf.at[slot], tm).astype(BF16)
        a = jnp.dot(xb, wgb[...], preferred_element_type=F32)
        u = jnp.dot(xb, wub[...], preferred_element_type=F32)
        hid = (_silu(a) * u).astype(BF16)
        y = jnp.dot(hid, wdb[...], preferred_element_type=F32)

        @pl.when(g >= 2)
        def _():
            out_copy(g - 2, slot).wait()

        _store_token_rows(ybuf.at[slot], y)
        out_copy(g, slot).start()
        return carry

    first = first_ref[e]
    lax.fori_loop(first, first + ntile_ref[e], tile, 0)

    @pl.when(e == pl.num_programs(0) - 1)
    def _():
        @pl.when(n_used >= 2)
        def _():
            out_copy(n_used - 2, n_used % 2).wait()
        out_copy(n_used - 1, (n_used - 1) % 2).wait()

        ybuf[0] = jnp.zeros((rows, LANES), F32)

        def fill(g, carry):
            out_copy(g, 0).start()
            return carry

        def drain(g, carry):
            out_copy(g, 0).wait()
            return carry

        lax.fori_loop(n_used, n_tiles, fill, 0)
        lax.fori_loop(n_used, n_tiles, drain, 0)


def _experts(first_tile, n_tile, n_used, xs, wg, wu, wd):
    tm = EXPERT_TILE
    n_exp, d, de = wg.shape
    grid_spec = pltpu.PrefetchScalarGridSpec(
        num_scalar_prefetch=3,
        grid=(n_exp,),
        in_specs=[
            pl.BlockSpec(memory_space=pl.ANY),
            pl.BlockSpec((1, d, de), lambda e, *_: (e, 0, 0)),
            pl.BlockSpec((1, d, de), lambda e, *_: (e, 0, 0)),
            pl.BlockSpec((1, de, d), lambda e, *_: (e, 0, 0)),
        ],
        out_specs=pl.BlockSpec(memory_space=pl.ANY),
        scratch_shapes=[pltpu.VMEM((2, tm * SUBLANES, LANES), F32),
                        pltpu.VMEM((2, tm * SUBLANES, LANES), F32),
                        pltpu.VMEM((d, de), BF16),
                        pltpu.VMEM((d, de), BF16),
                        pltpu.VMEM((de, d), BF16),
                        pltpu.SemaphoreType.DMA((2,)),
                        pltpu.SemaphoreType.DMA((2,))],
    )
    return pl.pallas_call(
        _expert_kernel,
        out_shape=jax.ShapeDtypeStruct(xs.shape, F32),
        grid_spec=grid_spec,
        compiler_params=pltpu.CompilerParams(
            dimension_semantics=("arbitrary",),
            vmem_limit_bytes=VMEM_LIMIT_BYTES),
        name="experts",
    )(first_tile, n_tile, n_used, xs, wg, wu, wd)


def _combine_kernel(p0_ref, p1_ref, y_hbm, x1_ref, slab_ref, mod_ref, gfin_ref, o_ref,
                    ybuf, sem):
    tm = COMBINE_TILE
    i = pl.program_id(0)
    nt = pl.num_programs(0)

    def issue(tile, slot):
        base = tile * tm

        def body(r, carry):
            dst_row = pl.multiple_of(r * SUBLANES, SUBLANES)
            for j, p_ref in enumerate((p0_ref, p1_ref)):
                src_row = pl.multiple_of(p_ref[base + r] * SUBLANES, SUBLANES)
                pltpu.make_async_copy(y_hbm.at[pl.ds(src_row, SUBLANES), :],
                                      ybuf.at[slot, j, pl.ds(dst_row, SUBLANES), :],
                                      sem.at[slot]).start(priority=j)
            return carry

        lax.fori_loop(0, tm, body, 0, unroll=DISPATCH_UNROLL)

    @pl.when(i == 0)
    def _():
        issue(0, 0)

    @pl.when(i + 1 < nt)
    def _():
        issue(i + 1, (i + 1) % 2)

    slot = i % 2
    for j in range(2):
        pltpu.make_async_copy(y_hbm.at[pl.ds(0, tm * SUBLANES), :], ybuf.at[slot, j],
                              sem.at[slot]).wait()

    slab = slab_ref[...]
    w0 = slab[:, 2:3]
    w1 = slab[:, 3:4]
    gate_f = mod_ref[0][5:6]
    moe = (w0 * _load_token_rows(ybuf.at[slot, 0], tm)
           + w1 * _load_token_rows(ybuf.at[slot, 1], tm))
    xo = x1_ref[...] + gate_f * moe
    ms = jnp.mean(xo * xo, axis=-1, keepdims=True)
    o_ref[...] = xo * lax.rsqrt(ms + EPS) * gfin_ref[...]


def _combine(p0, p1, y, x1, slab, mod, gfin, seq):
    n, d = x1.shape
    tm = COMBINE_TILE
    tiles_per_seq = seq // tm
    grid_spec = pltpu.PrefetchScalarGridSpec(
        num_scalar_prefetch=2,
        grid=(n // tm,),
        in_specs=[
            pl.BlockSpec(memory_space=pl.ANY),
            pl.BlockSpec((tm, d), lambda i, a, b: (i, 0)),
            pl.BlockSpec((tm, ROUTER_LANES), lambda i, a, b: (i, 0)),
            pl.BlockSpec((1, 6, d), lambda i, a, b: (i // tiles_per_seq, 0, 0)),
            pl.BlockSpec((1, d), lambda i, a, b: (0, 0)),
        ],
        out_specs=pl.BlockSpec((tm, d), lambda i, a, b: (i, 0)),
        scratch_shapes=[pltpu.VMEM((2, 2, tm * SUBLANES, LANES), F32),
                        pltpu.SemaphoreType.DMA((2,))],
    )
    return pl.pallas_call(
        _combine_kernel,
        out_shape=jax.ShapeDtypeStruct((n, d), F32),
        grid_spec=grid_spec,
        compiler_params=pltpu.CompilerParams(
            dimension_semantics=("arbitrary",),
            vmem_limit_bytes=VMEM_LIMIT_BYTES),
        name="combine",
    )(p0, p1, y, x1, slab, mod, gfin)


def _routing_plan(slab, counts_f, n):
    tm = EXPERT_TILE
    counts = counts_f[0, :N_EXPERTS].astype(jnp.int32)
    padded = ((counts + tm - 1) // tm) * tm
    ends = jnp.cumsum(padded)
    starts = ends - padded
    eid = slab[:, 0:2].astype(jnp.int32)
    rank = slab[:, 4:6].astype(jnp.int32)
    onehot = eid[:, :, None] == jnp.arange(N_EXPERTS, dtype=jnp.int32)[None, None, :]
    dest = jnp.sum(jnp.where(onehot, starts[None, None, :], 0), axis=-1) + rank
    p_rows = 2 * n + N_EXPERTS * tm
    n_used = (ends[-1] // tm).astype(jnp.int32)
    return (starts // tm, padded // tm, n_used.reshape(1), dest[:, 0], dest[:, 1],
            starts + counts, padded - counts, p_rows)


def kernel(x, c, positions, ada_w, ada_b, norm_mix_g, norm_ffn_g, w_in, conv_w, conv_b,
           beta_ret, beta_conv, w_out, router_group_w, router_group_b, router_expert_w,
           router_expert_b, expert_w_gate, expert_w_up, expert_w_down, norm_final_g):
    bsz, seq, d = x.shape
    n = bsz * seq
    depth = ada_w.shape[0]
    assert depth == 1, "the combine kernel fuses the trunk's final RMSNorm (single layer)"
    W = RET_HEADS * HEAD_DIM

    cos128, sin128 = _rope_tables(positions)
    heads = jnp.arange(RET_HEADS, dtype=F32)
    lg = jnp.log1p(-jnp.exp2(-5.0 - heads))
    lgl = jnp.repeat(lg, HEAD_DIM).reshape(1, W)
    assert CONV_GROUP_DIM == HEAD_DIM, "conv groups and retention heads share the 64-lane block sums"
    blk_np = np.kron(np.eye(RET_HEADS // 2, dtype=np.float32),
                     np.ones((HEAD_DIM, HEAD_DIM), np.float32))
    blk = jnp.asarray(blk_np, dtype=BF16)

    for l in range(depth):
        mod = _adaln(c, ada_w[l], ada_b[l]).reshape(bsz, 6, d)

        n_route = N_GROUPS + N_EXPERTS
        wr = jnp.concatenate([router_group_w[l].T, router_expert_w[l].T,
                              jnp.zeros((ROUTER_LANES - n_route, d), F32)], axis=0)
        wr_hi = wr.astype(BF16)
        wr_lo = (wr - wr_hi.astype(F32)).astype(BF16)
        wr2 = jnp.concatenate([wr_hi, wr_lo], axis=0)
        br = jnp.concatenate([router_group_b[l], router_expert_b[l]])
        br = jnp.pad(br, (0, ROUTER_LANES - br.shape[0])).reshape(1, ROUTER_LANES)

        x1, h2t, logits = _mixer(
            x, mod, cos128, sin128, norm_mix_g[l].reshape(1, d), w_in[l].astype(BF16),
            conv_w[l], conv_b[l].reshape(1, W), beta_ret[l].reshape(1, W),
            beta_conv[l].reshape(1, W), w_out[l].astype(BF16), norm_ffn_g[l].reshape(1, d),
            wr2, br, lg, lgl, blk)
        slab, counts = _router(logits)

        first_tile, n_tile, n_used, p0, p1, pad_start, pad_n, p_rows = _routing_plan(
            slab, counts, n)
        xs = _dispatch(p0, p1, pad_start, pad_n, n_used, h2t, p_rows)
        de = expert_w_gate.shape[-1]
        y = _experts(first_tile, n_tile, n_used, xs,
                     expert_w_gate[l].reshape(N_EXPERTS, d, de),
                     expert_w_up[l].reshape(N_EXPERTS, d, de),
                     expert_w_down[l].reshape(N_EXPERTS, de, d))
        out = _combine(p0, p1, y, x1.reshape(n, d), slab, mod, norm_final_g.reshape(1, d), seq)
        x = out.reshape(bsz, seq, d)
    return x
```

```python
import functools

import jax
import jax.numpy as jnp
import numpy as np
from jax import lax
from jax.experimental import pallas as pl
from jax.experimental.pallas import tpu as pltpu

F32 = jnp.float32
BF16 = jnp.bfloat16

CHUNK = 64
RET_HEADS = 8
HEAD_DIM = 64
CONV_GROUP_DIM = 64
ROPE_BASE = 10000.0
N_GROUPS = 4
EXPERTS_PER_GROUP = 8
N_EXPERTS = N_GROUPS * EXPERTS_PER_GROUP
EPS = 1e-6
GN_EPS = 1e-5

LANES = 128
SUBLANES = 8
VMEM_LIMIT_BYTES = 56 * 1024 * 1024

SEQ_TILE = 256
EXPERT_TILE = 256
COMBINE_TILE = 256
ROUTER_LANES = LANES


def _silu(v):
    return v * (1.0 / (1.0 + jnp.exp(-v)))


def _adaln_kernel(c_ref, w_ref, b_ref, o_ref):
    s = _silu(c_ref[...])
    o_ref[...] = jnp.dot(s, w_ref[...], precision=lax.Precision.HIGHEST,
                         preferred_element_type=F32) + b_ref[...]


def _adaln(c, w, b):
    bsz, d = c.shape
    n = w.shape[1]
    tn = 1024
    return pl.pallas_call(
        _adaln_kernel,
        out_shape=jax.ShapeDtypeStruct((bsz, n), F32),
        grid=(n // tn,),
        in_specs=[pl.BlockSpec((bsz, d), lambda j: (0, 0)),
                  pl.BlockSpec((d, tn), lambda j: (0, j)),
                  pl.BlockSpec((1, tn), lambda j: (0, j))],
        out_specs=pl.BlockSpec((bsz, tn), lambda j: (0, j)),
        name="adaln",
    )(c, w, b.reshape(1, n))


ROPE_ROWS = 1024


def _rope_kernel(pos_ref, invf_ref, cos_ref, sin_ref):
    half = HEAD_DIM // 2
    per_row = LANES // half
    r = ROPE_ROWS
    ang = pos_ref[...].astype(F32) * invf_ref[...]
    lane = lax.broadcasted_iota(jnp.int32, (r, LANES), 1)
    quarter = lane // half
    sign = jnp.where(quarter % 2 == 0, -1.0, 1.0)
    for table, out_ref, scale in ((jnp.cos(ang), cos_ref, None), (jnp.sin(ang), sin_ref, sign)):
        rolled = [table] + [pltpu.roll(table, half * k, 1) for k in range(1, per_row)]
        for q in range(per_row):
            val = rolled[(0 - q) % per_row]
            for k in range(1, per_row):
                val = jnp.where(quarter == k, rolled[(k - q) % per_row], val)
            if scale is not None:
                val = val * scale
            out_ref[pl.ds(q, r, stride=per_row), :] = val


def _rope_tables(positions):
    n = positions.size
    half = HEAD_DIM // 2
    inv_freq = ROPE_BASE ** (-jnp.arange(0, HEAD_DIM, 2, dtype=F32) / HEAD_DIM)
    per_row = LANES // half
    rows = n // per_row
    pos_rep = jnp.broadcast_to(positions.reshape(n, 1), (n, half)).reshape(rows, LANES)
    invf = jnp.tile(inv_freq, per_row).reshape(1, LANES)
    tr = ROPE_ROWS
    return pl.pallas_call(
        _rope_kernel,
        out_shape=(jax.ShapeDtypeStruct((n, LANES), F32),) * 2,
        grid=(rows // tr,),
        in_specs=[pl.BlockSpec((tr, LANES), lambda i: (i, 0)),
                  pl.BlockSpec((1, LANES), lambda i: (0, 0))],
        out_specs=(pl.BlockSpec((tr * per_row, LANES), lambda i: (i, 0)),) * 2,
        name="rope_table",
    )(pos_rep, invf)


def _mixer_kernel(lg_ref, x_ref, mod_ref, cos_ref, sin_ref, gmix_ref, win_ref, convw_ref,
                  convb_ref, bret_ref, bconv_ref, wout_ref, gffn_ref, wr_ref, br_ref, lgl_ref,
                  blk_ref, x1_ref, h2_ref, logit_ref, state_ref, ubuf_ref):
    L = SEQ_TILE
    W = RET_HEADS * HEAD_DIM
    s = pl.program_id(1)

    @pl.when(s == 0)
    def _():
        state_ref[...] = jnp.zeros_like(state_ref)
        ubuf_ref[0:SUBLANES, :] = jnp.zeros((SUBLANES, W), F32)

    x = x_ref[0]
    d_model = x.shape[-1]
    mod = mod_ref[0]
    shift_m, scale_m, gate_m = mod[0:1], mod[1:2], mod[2:3]
    shift_f, scale_f = mod[3:4], mod[4:5]

    ms = jnp.mean(x * x, axis=-1, keepdims=True)
    h = x * lax.rsqrt(ms + EPS) * gmix_ref[...]
    h = h * (1.0 + scale_m) + shift_m
    hb = h.astype(BF16)

    def proj(i):
        return jnp.dot(hb, win_ref[:, i * W:(i + 1) * W], preferred_element_type=F32)

    cos = jnp.concatenate([cos_ref[...]] * 4, axis=1)
    sin = jnp.concatenate([sin_ref[...]] * 4, axis=1)
    lane_w = lax.broadcasted_iota(jnp.int32, (L, W), 1)
    first_half = (lane_w & (HEAD_DIM - 1)) < (HEAD_DIM // 2)

    def rot(t):
        partner = jnp.where(first_half, pltpu.roll(t, W - HEAD_DIM // 2, 1),
                            pltpu.roll(t, HEAD_DIM // 2, 1))
        return t * cos + partner * sin

    q = rot(proj(0))
    k = rot(proj(1)) * (HEAD_DIM ** -0.5)
    v = proj(2)
    vb = v.astype(BF16)
    kb = k.astype(BF16)

    lgl = lgl_ref[...]
    rowf = lax.broadcasted_iota(jnp.int32, (L, W), 0).astype(F32)
    qd = q * jnp.exp(lgl * (rowf + 1.0))
    kd = k * jnp.exp(lgl * (float(L - 1) - rowf))
    blk = blk_ref[...]
    HW = W // 2
    blk_f = blk.astype(F32)
    qdb = qd.astype(BF16)
    kdb = kd.astype(BF16)
    state_decay = jnp.exp(lgl * float(L))
    inter = []
    for hf in range(2):
        sl = slice(hf * HW, (hf + 1) * HW)
        st = state_ref[hf]
        inter.append(jnp.dot(qdb[:, sl], st.astype(BF16), preferred_element_type=F32))
        kv = lax.dot_general(kdb[:, sl], vb[:, sl], (((0,), (0,)), ((), ())),
                             preferred_element_type=F32)
        state_ref[hf] = st * state_decay[:, sl] + kv * blk_f
    y_inter = jnp.concatenate(inter, axis=1)

    def head_sums(t):
        tb = t.astype(BF16)
        return jnp.concatenate(
            [jnp.dot(tb[:, hf * HW:(hf + 1) * HW], blk, preferred_element_type=F32)
             for hf in range(2)], axis=1)

    ii = lax.broadcasted_iota(jnp.int32, (L, L), 0)
    jj = lax.broadcasted_iota(jnp.int32, (L, L), 1)
    dist = jnp.abs(ii - jj).astype(F32)
    allowed = (jj // CHUNK) <= (ii // CHUNK)
    lane_p = lax.broadcasted_iota(jnp.int32, (L, LANES), 1)
    lo_head = lane_p < HEAD_DIM
    pairs = []
    for p in range(RET_HEADS // 2):
        sl = slice(p * LANES, (p + 1) * LANES)
        qp, kp, vp = q[:, sl], kb[:, sl], vb[:, sl]
        ys = []
        for hh in range(2):
            head = 2 * p + hh
            keep = lo_head if hh == 0 else jnp.logical_not(lo_head)
            qh = jnp.where(keep, qp, 0.0).astype(BF16)
            sc = lax.dot_general(qh, kp, (((1,), (1,)), ((), ())),
                                 preferred_element_type=F32)
            decay = jnp.where(allowed, jnp.exp(lg_ref[head] * dist), 0.0)
            ys.append(jnp.dot((sc * decay).astype(BF16), vp, preferred_element_type=F32))
        pairs.append(jnp.where(lo_head, ys[0], ys[1]))
    y = jnp.concatenate(pairs, axis=1) + y_inter

    inv_hd = 1.0 / HEAD_DIM
    mu = head_sums(y) * inv_hd
    d = y - mu
    var = head_sums(d * d) * inv_hd
    g = proj(3)
    y_ret = _silu(g) * (d * lax.rsqrt(var + GN_EPS)) * bret_ref[...]

    b_gate = proj(4)
    u = proj(5) * proj(6)
    ubuf_ref[SUBLANES:SUBLANES + L, :] = u
    u1 = ubuf_ref[SUBLANES - 1:SUBLANES - 1 + L, :]
    u2 = ubuf_ref[SUBLANES - 2:SUBLANES - 2 + L, :]
    ubuf_ref[0:SUBLANES, :] = ubuf_ref[L:L + SUBLANES, :]
    cw = convw_ref[...]
    conv = u2 * cw[0:1] + u1 * cw[1:2] + u * cw[2:3] + convb_ref[...]
    yc = b_gate * conv
    msc = head_sums(yc * yc) * (1.0 / CONV_GROUP_DIM)
    y_conv = yc * lax.rsqrt(msc + EPS) * bconv_ref[...]

    mix = (jnp.dot(y_ret.astype(BF16), wout_ref[0:W, :], preferred_element_type=F32)
           + jnp.dot(y_conv.astype(BF16), wout_ref[W:2 * W, :], preferred_element_type=F32))
    x1 = x + gate_m * mix
    x1_ref[0] = x1

    ms2 = jnp.mean(x1 * x1, axis=-1, keepdims=True)
    h2 = x1 * lax.rsqrt(ms2 + EPS) * gffn_ref[...]
    h2 = h2 * (1.0 + scale_f) + shift_f
    for cch in range(d_model // LANES):
        h2_ref[pl.ds(cch, L, stride=SUBLANES), :] = h2[:, cch * LANES:(cch + 1) * LANES]

    hi = h2.astype(BF16)
    lo = (h2 - hi.astype(F32)).astype(BF16)
    w2 = wr_ref[...]
    nt_dims = (((1,), (1,)), ((), ()))
    parts = (lax.dot_general(hi, w2, nt_dims, preferred_element_type=F32)
             + lax.dot_general(lo, w2, nt_dims, preferred_element_type=F32))
    logit_ref[...] = parts[:, :ROUTER_LANES] + parts[:, ROUTER_LANES:] + br_ref[...]


ROUTER_TILE = 2048
RANK_BLOCK = 256


def _router_kernel(logit_ref, slab_ref, cnt_out_ref, cnt_ref):
    T = ROUTER_TILE

    @pl.when(pl.program_id(0) == 0)
    def _():
        cnt_ref[...] = jnp.zeros_like(cnt_ref)

    logits = logit_ref[...]
    lane = lax.broadcasted_iota(jnp.int32, (T, ROUTER_LANES), 1).astype(F32)
    neg_inf = F32(-jnp.inf)
    big = F32(1e9)
    gmask = lane < float(N_GROUPS)
    lgm = jnp.where(gmask, logits, neg_inf)
    gexp = jnp.exp(lgm - jnp.max(lgm, axis=-1, keepdims=True))
    gp = gexp / jnp.sum(gexp, axis=-1, keepdims=True)
    g_top = jnp.max(gp, axis=-1, keepdims=True)
    g_idx = jnp.min(jnp.where(gmask & (gp == g_top), lane, big), axis=-1, keepdims=True)

    e_lo = float(N_GROUPS) + float(EXPERTS_PER_GROUP) * g_idx
    emask = (lane >= e_lo) & (lane < e_lo + float(EXPERTS_PER_GROUP))
    lem = jnp.where(emask, logits, neg_inf)
    eexp = jnp.exp(lem - jnp.max(lem, axis=-1, keepdims=True))
    ep = eexp / jnp.sum(eexp, axis=-1, keepdims=True)
    p1 = jnp.max(jnp.where(emask, ep, -1.0), axis=-1, keepdims=True)
    i1 = jnp.min(jnp.where(emask & (ep == p1), lane, big), axis=-1, keepdims=True)
    m2 = emask & (lane != i1)
    p2 = jnp.max(jnp.where(m2, ep, -1.0), axis=-1, keepdims=True)
    i2 = jnp.min(jnp.where(m2 & (ep == p2), lane, big), axis=-1, keepdims=True)
    den = p1 + p2
    w1 = p1 / den * g_top
    w2 = p2 / den * g_top
    e1 = i1 - float(N_GROUPS)
    e2 = i2 - float(N_GROUPS)

    oh1 = jnp.where(lane == e1, 1.0, 0.0)
    oh2 = jnp.where(lane == e2, 1.0, 0.0)
    rb = RANK_BLOCK
    ii = lax.broadcasted_iota(jnp.int32, (rb, rb), 0)
    jj = lax.broadcasted_iota(jnp.int32, (rb, rb), 1)
    ltri = jnp.where(jj < ii, 1.0, 0.0).astype(BF16)
    base = cnt_ref[...]
    ranks1, ranks2 = [], []
    for k in range(T // rb):
        o1 = oh1[k * rb:(k + 1) * rb]
        o2 = oh2[k * rb:(k + 1) * rb]
        r1 = jnp.dot(ltri, o1.astype(BF16), preferred_element_type=F32)
        r2 = jnp.dot(ltri, o2.astype(BF16), preferred_element_type=F32)
        c1 = jnp.sum(o1, axis=0, keepdims=True)
        c2 = jnp.sum(o2, axis=0, keepdims=True)
        ranks1.append(jnp.sum(o1 * (base + r1), axis=-1, keepdims=True))
        ranks2.append(jnp.sum(o2 * (base + c1 + r2), axis=-1, keepdims=True))
        base = base + c1 + c2
    rank1 = jnp.concatenate(ranks1, axis=0)
    rank2 = jnp.concatenate(ranks2, axis=0)
    cnt_ref[...] = base
    cnt_out_ref[...] = base

    slab = jnp.where(lane == 0.0, e1,
                     jnp.where(lane == 1.0, e2,
                               jnp.where(lane == 2.0, w1,
                                         jnp.where(lane == 3.0, w2,
                                                   jnp.where(lane == 4.0, rank1,
                                                             jnp.where(lane == 5.0, rank2, 0.0))))))
    slab_ref[...] = slab


def _router(logits):
    n = logits.shape[0]
    T = ROUTER_TILE
    return pl.pallas_call(
        _router_kernel,
        out_shape=(jax.ShapeDtypeStruct((n, ROUTER_LANES), F32),
                   jax.ShapeDtypeStruct((1, ROUTER_LANES), F32)),
        grid=(n // T,),
        in_specs=[pl.BlockSpec((T, ROUTER_LANES), lambda i: (i, 0))],
        out_specs=(pl.BlockSpec((T, ROUTER_LANES), lambda i: (i, 0)),
                   pl.BlockSpec((1, ROUTER_LANES), lambda i: (0, 0))),
        scratch_shapes=[pltpu.VMEM((1, ROUTER_LANES), F32)],
        compiler_params=pltpu.CompilerParams(
            dimension_semantics=("arbitrary",), vmem_limit_bytes=VMEM_LIMIT_BYTES),
        name="router",
    )(logits)


def _mixer(x, mod, cos128, sin128, gmix, win_b, convw, convb, bret, bconv, wout_b, gffn,
           wr2, br, lg, lgl, blk):
    bsz, seq, d = x.shape
    L = SEQ_TILE
    ns = seq // L
    W = RET_HEADS * HEAD_DIM
    n = bsz * seq
    const2 = lambda b, s: (0, 0)
    in_specs = [
        pl.BlockSpec(memory_space=pltpu.SMEM),
        pl.BlockSpec((1, L, d), lambda b, s: (b, s, 0)),
        pl.BlockSpec((1, 6, d), lambda b, s: (b, 0, 0)),
        pl.BlockSpec((L, LANES), lambda b, s: (b * ns + s, 0)),
        pl.BlockSpec((L, LANES), lambda b, s: (b * ns + s, 0)),
        pl.BlockSpec((1, d), const2),
        pl.BlockSpec(win_b.shape, const2),
        pl.BlockSpec(convw.shape, const2),
        pl.BlockSpec((1, W), const2),
        pl.BlockSpec((1, W), const2),
        pl.BlockSpec((1, W), const2),
        pl.BlockSpec(wout_b.shape, const2),
        pl.BlockSpec((1, d), const2),
        pl.BlockSpec(wr2.shape, const2),
        pl.BlockSpec((1, ROUTER_LANES), const2),
        pl.BlockSpec((1, W), const2),
        pl.BlockSpec((W // 2, W // 2), const2),
    ]
    assert d == SUBLANES * LANES, "one token must fill exactly one (8, 128) f32 tile"
    out_shape = (jax.ShapeDtypeStruct((bsz, seq, d), F32),
                 jax.ShapeDtypeStruct((n * SUBLANES, LANES), F32),
                 jax.ShapeDtypeStruct((n, ROUTER_LANES), F32))
    out_specs = (pl.BlockSpec((1, L, d), lambda b, s: (b, s, 0)),
                 pl.BlockSpec((L * SUBLANES, LANES), lambda b, s: (b * ns + s, 0)),
                 pl.BlockSpec((L, ROUTER_LANES), lambda b, s: (b * ns + s, 0)))
    return pl.pallas_call(
        _mixer_kernel,
        out_shape=out_shape,
        grid=(bsz, ns),
        in_specs=in_specs,
        out_specs=out_specs,
        scratch_shapes=[pltpu.VMEM((2, W // 2, W // 2), F32),
                        pltpu.VMEM((L + 2 * SUBLANES, W), F32)],
        compiler_params=pltpu.CompilerParams(
            dimension_semantics=("arbitrary", "arbitrary"),
            vmem_limit_bytes=VMEM_LIMIT_BYTES),
        name="mixer",
    )(lg, x, mod, cos128, sin128, gmix, win_b, convw, convb, bret, bconv, wout_b, gffn,
      wr2, br, lgl, blk)


def _load_token_rows(ref, rows):
    return jnp.concatenate(
        [ref[pl.ds(c, rows, stride=SUBLANES), :] for c in range(SUBLANES)], axis=1)


def _store_token_rows(ref, val):
    rows = val.shape[0]
    for c in range(SUBLANES):
        ref[pl.ds(c, rows, stride=SUBLANES), :] = val[:, c * LANES:(c + 1) * LANES]


DISPATCH_TILE = 512
DISPATCH_UNROLL = 8
PAD_UNITS = tuple(1 << b for b in reversed(range(EXPERT_TILE.bit_length() - 1)))


def _token_rows(ref, row):
    return ref.at[pl.ds(pl.multiple_of(row * SUBLANES, SUBLANES), SUBLANES), :]


def _dispatch_kernel(d0_ref, d1_ref, ps_ref, pn_ref, nused_ref, h2_ref, xs_hbm, zbuf, sem, zsem):
    dt = DISPATCH_TILE
    i = pl.program_id(0)
    zrows = PAD_UNITS[0]

    def pad_copy(start, unit):
        return pltpu.make_async_copy(zbuf.at[pl.ds(0, unit * SUBLANES), :],
                                     xs_hbm.at[pl.ds(pl.multiple_of(start * SUBLANES, SUBLANES),
                                                     unit * SUBLANES), :], zsem)

    def pad_pass(do):
        def per_expert(e, carry):
            start = ps_ref[e]
            npad = pn_ref[e]
            for unit in PAD_UNITS:
                @pl.when((npad & unit) != 0)
                def _():
                    do(pad_copy(start + (npad & ~(2 * unit - 1)), unit))
            return carry
        lax.fori_loop(0, N_EXPERTS, per_expert, 0)

    def tail_pass(do):
        def per_unit(k, carry):
            do(pad_copy(k * zrows, zrows))
            return carry
        per_tile = EXPERT_TILE // zrows
        n_units = xs_hbm.shape[0] // (zrows * SUBLANES)
        lax.fori_loop(nused_ref[0] * per_tile, n_units, per_unit, 0)

    @pl.when(i == 0)
    def _():
        zbuf[...] = jnp.zeros_like(zbuf)
        pad_pass(lambda cp: cp.start())
        tail_pass(lambda cp: cp.start())

    base = i * dt

    def body(r, carry):
        src = _token_rows(h2_ref, r)
        for prio, d_ref in enumerate((d0_ref, d1_ref)):
            pltpu.make_async_copy(src, _token_rows(xs_hbm, d_ref[base + r]), sem).start(
                priority=prio)
        return carry

    lax.fori_loop(0, dt, body, 0, unroll=DISPATCH_UNROLL)
    for _ in range(2):
        pltpu.make_async_copy(h2_ref, xs_hbm.at[pl.ds(0, dt * SUBLANES), :], sem).wait()

    @pl.when(i == pl.num_programs(0) - 1)
    def _():
        pad_pass(lambda cp: cp.wait())
        tail_pass(lambda cp: cp.wait())


def _dispatch(dest0, dest1, pad_start, pad_n, n_used, h2t, p_rows):
    n = dest0.shape[0]
    dt = DISPATCH_TILE
    grid_spec = pltpu.PrefetchScalarGridSpec(
        num_scalar_prefetch=5,
        grid=(n // dt,),
        in_specs=[pl.BlockSpec((dt * SUBLANES, LANES), lambda i, *_: (i, 0))],
        out_specs=pl.BlockSpec(memory_space=pl.ANY),
        scratch_shapes=[pltpu.VMEM((PAD_UNITS[0] * SUBLANES, LANES), F32),
                        pltpu.SemaphoreType.DMA(()), pltpu.SemaphoreType.DMA(())],
    )
    return pl.pallas_call(
        _dispatch_kernel,
        out_shape=jax.ShapeDtypeStruct((p_rows * SUBLANES, LANES), F32),
        grid_spec=grid_spec,
        compiler_params=pltpu.CompilerParams(dimension_semantics=("arbitrary",)),
        name="dispatch",
    )(dest0, dest1, pad_start, pad_n, n_used, h2t)


def _expert_kernel(first_ref, ntile_ref, nused_ref, xs_hbm, wg_ref, wu_ref, wd_ref, y_hbm,
                   xbuf, ybuf, wgb, wub, wdb, isem, osem):
    tm = EXPERT_TILE
    rows = tm * SUBLANES
    e = pl.program_id(0)
    n_used = nused_ref[0]
    n_tiles = y_hbm.shape[0] // rows

    def tile_rows(ref, g):
        return ref.at[pl.ds(pl.multiple_of(g * rows, rows), rows), :]

    def in_copy(g, slot):
        return pltpu.make_async_copy(tile_rows(xs_hbm, g), xbuf.at[slot], isem.at[slot])

    def out_copy(g, slot):
        return pltpu.make_async_copy(ybuf.at[slot], tile_rows(y_hbm, g), osem.at[slot])

    @pl.when(e == 0)
    def _():
        in_copy(0, 0).start()

    @pl.when(ntile_ref[e] > 0)
    def _():
        wgb[...] = wg_ref[0].astype(BF16)
        wub[...] = wu_ref[0].astype(BF16)
        wdb[...] = wd_ref[0].astype(BF16)

    def tile(g, carry):
        slot = g % 2
        in_copy(g, slot).wait()

        @pl.when(g + 1 < n_used)
        def _():
            in_copy(g + 1, 1 - slot).start()

        xb = _load_token_rows(xbuf.at[slot], tm).astype(BF16)
        a = jnp.dot(xb, wgb[...], preferred_element_type=F32)
        u = jnp.dot(xb, wub[...], preferred_element_type=F32)
        hid = (_silu(a) * u).astype(BF16)
        y = jnp.dot(hid, wdb[...], preferred_element_type=F32)

        @pl.when(g >= 2)
        def _():
            out_copy(g - 2, slot).wait()

        _store_token_rows(ybuf.at[slot], y)
        out_copy(g, slot).start()
        return carry

    first = first_ref[e]
    lax.fori_loop(first, first + ntile_ref[e], tile, 0)

    @pl.when(e == pl.num_programs(0) - 1)
    def _():
        @pl.when(n_used >= 2)
        def _():
            out_copy(n_used - 2, n_used % 2).wait()
        out_copy(n_used - 1, (n_used - 1) % 2).wait()

        ybuf[0] = jnp.zeros((rows, LANES), F32)

        def fill(g, carry):
            out_copy(g, 0).start()
            return carry

        def drain(g, carry):
            out_copy(g, 0).wait()
            return carry

        lax.fori_loop(n_used, n_tiles, fill, 0)
        lax.fori_loop(n_used, n_tiles, drain, 0)


def _experts(first_tile, n_tile, n_used, xs, wg, wu, wd):
    tm = EXPERT_TILE
    n_exp, d, de = wg.shape
    grid_spec = pltpu.PrefetchScalarGridSpec(
        num_scalar_prefetch=3,
        grid=(n_exp,),
        in_specs=[
            pl.BlockSpec(memory_space=pl.ANY),
            pl.BlockSpec((1, d, de), lambda e, *_: (e, 0, 0)),
            pl.BlockSpec((1, d, de), lambda e, *_: (e, 0, 0)),
            pl.BlockSpec((1, de, d), lambda e, *_: (e, 0, 0)),
        ],
        out_specs=pl.BlockSpec(memory_space=pl.ANY),
        scratch_shapes=[pltpu.VMEM((2, tm * SUBLANES, LANES), F32),
                        pltpu.VMEM((2, tm * SUBLANES, LANES), F32),
                        pltpu.VMEM((d, de), BF16),
                        pltpu.VMEM((d, de), BF16),
                        pltpu.VMEM((de, d), BF16),
                        pltpu.SemaphoreType.DMA((2,)),
                        pltpu.SemaphoreType.DMA((2,))],
    )
    return pl.pallas_call(
        _expert_kernel,
        out_shape=jax.ShapeDtypeStruct(xs.shape, F32),
        grid_spec=grid_spec,
        compiler_params=pltpu.CompilerParams(
            dimension_semantics=("arbitrary",),
            vmem_limit_bytes=VMEM_LIMIT_BYTES),
        name="experts",
    )(first_tile, n_tile, n_used, xs, wg, wu, wd)


def _combine_kernel(p0_ref, p1_ref, y_hbm, x1_ref, slab_ref, mod_ref, gfin_ref, o_ref,
                    ybuf, sem):
    tm = COMBINE_TILE
    i = pl.program_id(0)
    nt = pl.num_programs(0)

    def issue(tile, slot):
        base = tile * tm

        def body(r, carry):
            dst_row = pl.multiple_of(r * SUBLANES, SUBLANES)
            for j, p_ref in enumerate((p0_ref, p1_ref)):
                src_row = pl.multiple_of(p_ref[base + r] * SUBLANES, SUBLANES)
                pltpu.make_async_copy(y_hbm.at[pl.ds(src_row, SUBLANES), :],
                                      ybuf.at[slot, j, pl.ds(dst_row, SUBLANES), :],
                                      sem.at[slot]).start(priority=j)
            return carry

        lax.fori_loop(0, tm, body, 0, unroll=DISPATCH_UNROLL)

    @pl.when(i == 0)
    def _():
        issue(0, 0)

    @pl.when(i + 1 < nt)
    def _():
        issue(i + 1, (i + 1) % 2)

    slot = i % 2
    for j in range(2):
        pltpu.make_async_copy(y_hbm.at[pl.ds(0, tm * SUBLANES), :], ybuf.at[slot, j],
                              sem.at[slot]).wait()

    slab = slab_ref[...]
    w0 = slab[:, 2:3]
    w1 = slab[:, 3:4]
    gate_f = mod_ref[0][5:6]
    moe = (w0 * _load_token_rows(ybuf.at[slot, 0], tm)
           + w1 * _load_token_rows(ybuf.at[slot, 1], tm))
    xo = x1_ref[...] + gate_f * moe
    ms = jnp.mean(xo * xo, axis=-1, keepdims=True)
    o_ref[...] = xo * lax.rsqrt(ms + EPS) * gfin_ref[...]


def _combine(p0, p1, y, x1, slab, mod, gfin, seq):
    n, d = x1.shape
    tm = COMBINE_TILE
    tiles_per_seq = seq // tm
    grid_spec = pltpu.PrefetchScalarGridSpec(
        num_scalar_prefetch=2,
        grid=(n // tm,),
        in_specs=[
            pl.BlockSpec(memory_space=pl.ANY),
            pl.BlockSpec((tm, d), lambda i, a, b: (i, 0)),
            pl.BlockSpec((tm, ROUTER_LANES), lambda i, a, b: (i, 0)),
            pl.BlockSpec((1, 6, d), lambda i, a, b: (i // tiles_per_seq, 0, 0)),
            pl.BlockSpec((1, d), lambda i, a, b: (0, 0)),
        ],
        out_specs=pl.BlockSpec((tm, d), lambda i, a, b: (i, 0)),
        scratch_shapes=[pltpu.VMEM((2, 2, tm * SUBLANES, LANES), F32),
                        pltpu.SemaphoreType.DMA((2,))],
    )
    return pl.pallas_call(
        _combine_kernel,
        out_shape=jax.ShapeDtypeStruct((n, d), F32),
        grid_spec=grid_spec,
        compiler_params=pltpu.CompilerParams(
            dimension_semantics=("arbitrary",),
            vmem_limit_bytes=VMEM_LIMIT_BYTES),
        name="combine",
    )(p0, p1, y, x1, slab, mod, gfin)


def _routing_plan(slab, counts_f, n):
    tm = EXPERT_TILE
    counts = counts_f[0, :N_EXPERTS].astype(jnp.int32)
    padded = ((counts + tm - 1) // tm) * tm
    ends = jnp.cumsum(padded)
    starts = ends - padded
    eid = slab[:, 0:2].astype(jnp.int32)
    rank = slab[:, 4:6].astype(jnp.int32)
    onehot = eid[:, :, None] == jnp.arange(N_EXPERTS, dtype=jnp.int32)[None, None, :]
    dest = jnp.sum(jnp.where(onehot, starts[None, None, :], 0), axis=-1) + rank
    p_rows = 2 * n + N_EXPERTS * tm
    n_used = (ends[-1] // tm).astype(jnp.int32)
    return (starts // tm, padded // tm, n_used.reshape(1), dest[:, 0], dest[:, 1],
            starts + counts, padded - counts, p_rows)


def kernel(x, c, positions, ada_w, ada_b, norm_mix_g, norm_ffn_g, w_in, conv_w, conv_b,
           beta_ret, beta_conv, w_out, router_group_w, router_group_b, router_expert_w,
           router_expert_b, expert_w_gate, expert_w_up, expert_w_down, norm_final_g):
    bsz, seq, d = x.shape
    n = bsz * seq
    depth = ada_w.shape[0]
    assert depth == 1, "the combine kernel fuses the trunk's final RMSNorm (single layer)"
    W = RET_HEADS * HEAD_DIM

    cos128, sin128 = _rope_tables(positions)
    heads = jnp.arange(RET_HEADS, dtype=F32)
    lg = jnp.log1p(-jnp.exp2(-5.0 - heads))
    lgl = jnp.repeat(lg, HEAD_DIM).reshape(1, W)
    assert CONV_GROUP_DIM == HEAD_DIM, "conv groups and retention heads share the 64-lane block sums"
    blk_np = np.kron(np.eye(RET_HEADS // 2, dtype=np.float32),
                     np.ones((HEAD_DIM, HEAD_DIM), np.float32))
    blk = jnp.asarray(blk_np, dtype=BF16)

    for l in range(depth):
        mod = _adaln(c, ada_w[l], ada_b[l]).reshape(bsz, 6, d)

        n_route = N_GROUPS + N_EXPERTS
        wr = jnp.concatenate([router_group_w[l].T, router_expert_w[l].T,
                              jnp.zeros((ROUTER_LANES - n_route, d), F32)], axis=0)
        wr_hi = wr.astype(BF16)
        wr_lo = (wr - wr_hi.astype(F32)).astype(BF16)
        wr2 = jnp.concatenate([wr_hi, wr_lo], axis=0)
        br = jnp.concatenate([router_group_b[l], router_expert_b[l]])
        br = jnp.pad(br, (0, ROUTER_LANES - br.shape[0])).reshape(1, ROUTER_LANES)

        x1, h2t, logits = _mixer(
            x, mod, cos128, sin128, norm_mix_g[l].reshape(1, d), w_in[l].astype(BF16),
            conv_w[l], conv_b[l].reshape(1, W), beta_ret[l].reshape(1, W),
            beta_conv[l].reshape(1, W), w_out[l].astype(BF16), norm_ffn_g[l].reshape(1, d),
            wr2, br, lg, lgl, blk)
        slab, counts = _router(logits)

        first_tile, n_tile, n_used, p0, p1, pad_start, pad_n, p_rows = _routing_plan(
            slab, counts, n)
        xs = _dispatch(p0, p1, pad_start, pad_n, n_used, h2t, p_rows)
        de = expert_w_gate.shape[-1]
        y = _experts(first_tile, n_tile, n_used, xs,
                     expert_w_gate[l].reshape(N_EXPERTS, d, de),
                     expert_w_up[l].reshape(N_EXPERTS, d, de),
                     expert_w_down[l].reshape(N_EXPERTS, de, d))
        out = _combine(p0, p1, y, x1.reshape(n, d), slab, mod, norm_final_g.reshape(1, d), seq)
        x = out.reshape(bsz, seq, d)
    return x
```

```python
import functools

import jax
import jax.numpy as jnp
import numpy as np
from jax import lax
from jax.experimental import pallas as pl
from jax.experimental.pallas import tpu as pltpu

F32 = jnp.float32
BF16 = jnp.bfloat16

CHUNK = 64
RET_HEADS = 8
HEAD_DIM = 64
CONV_GROUP_DIM = 64
ROPE_BASE = 10000.0
N_GROUPS = 4
EXPERTS_PER_GROUP = 8
N_EXPERTS = N_GROUPS * EXPERTS_PER_GROUP
EPS = 1e-6
GN_EPS = 1e-5

LANES = 128
SUBLANES = 8
VMEM_LIMIT_BYTES = 56 * 1024 * 1024

SEQ_TILE = 256
EXPERT_TILE = 256
COMBINE_TILE = 256
ROUTER_LANES = LANES


def _silu(v):
    return v * (1.0 / (1.0 + jnp.exp(-v)))


def _adaln_kernel(c_ref, w_ref, b_ref, o_ref):
    s = _silu(c_ref[...])
    o_ref[...] = jnp.dot(s, w_ref[...], precision=lax.Precision.HIGHEST,
                         preferred_element_type=F32) + b_ref[...]


def _adaln(c, w, b):
    bsz, d = c.shape
    n = w.shape[1]
    tn = 1024
    return pl.pallas_call(
        _adaln_kernel,
        out_shape=jax.ShapeDtypeStruct((bsz, n), F32),
        grid=(n // tn,),
        in_specs=[pl.BlockSpec((bsz, d), lambda j: (0, 0)),
                  pl.BlockSpec((d, tn), lambda j: (0, j)),
                  pl.BlockSpec((1, tn), lambda j: (0, j))],
        out_specs=pl.BlockSpec((bsz, tn), lambda j: (0, j)),
        name="adaln",
    )(c, w, b.reshape(1, n))


ROPE_ROWS = 1024


def _rope_kernel(pos_ref, invf_ref, cos_ref, sin_ref):
    half = HEAD_DIM // 2
    per_row = LANES // half
    r = ROPE_ROWS
    ang = pos_ref[...].astype(F32) * invf_ref[...]
    lane = lax.broadcasted_iota(jnp.int32, (r, LANES), 1)
    quarter = lane // half
    sign = jnp.where(quarter % 2 == 0, -1.0, 1.0)
    for table, out_ref, scale in ((jnp.cos(ang), cos_ref, None), (jnp.sin(ang), sin_ref, sign)):
        rolled = [table] + [pltpu.roll(table, half * k, 1) for k in range(1, per_row)]
        for q in range(per_row):
            val = rolled[(0 - q) % per_row]
            for k in range(1, per_row):
                val = jnp.where(quarter == k, rolled[(k - q) % per_row], val)
            if scale is not None:
                val = val * scale
            out_ref[pl.ds(q, r, stride=per_row), :] = val


def _rope_tables(positions):
    n = positions.size
    half = HEAD_DIM // 2
    inv_freq = ROPE_BASE ** (-jnp.arange(0, HEAD_DIM, 2, dtype=F32) / HEAD_DIM)
    per_row = LANES // half
    rows = n // per_row
    pos_rep = jnp.broadcast_to(positions.reshape(n, 1), (n, half)).reshape(rows, LANES)
    invf = jnp.tile(inv_freq, per_row).reshape(1, LANES)
    tr = ROPE_ROWS
    return pl.pallas_call(
        _rope_kernel,
        out_shape=(jax.ShapeDtypeStruct((n, LANES), F32),) * 2,
        grid=(rows // tr,),
        in_specs=[pl.BlockSpec((tr, LANES), lambda i: (i, 0)),
                  pl.BlockSpec((1, LANES), lambda i: (0, 0))],
        out_specs=(pl.BlockSpec((tr * per_row, LANES), lambda i: (i, 0)),) * 2,
        name="rope_table",
    )(pos_rep, invf)


def _mixer_kernel(lg_ref, x_ref, mod_ref, cos_ref, sin_ref, gmix_ref, win_ref, convw_ref,
                  convb_ref, bret_ref, bconv_ref, wout_ref, gffn_ref, wr_ref, br_ref, lgl_ref,
                  blk_ref, x1_ref, h2_ref, logit_ref, state_ref, ubuf_ref):
    L = SEQ_TILE
    W = RET_HEADS * HEAD_DIM
    s = pl.program_id(1)

    @pl.when(s == 0)
    def _():
        state_ref[...] = jnp.zeros_like(state_ref)
        ubuf_ref[0:SUBLANES, :] = jnp.zeros((SUBLANES, W), F32)

    x = x_ref[0]
    d_model = x.shape[-1]
    mod = mod_ref[0]
    shift_m, scale_m, gate_m = mod[0:1], mod[1:2], mod[2:3]
    shift_f, scale_f = mod[3:4], mod[4:5]

    ms = jnp.mean(x * x, axis=-1, keepdims=True)
    h = x * lax.rsqrt(ms + EPS) * gmix_ref[...]
    h = h * (1.0 + scale_m) + shift_m
    hb = h.astype(BF16)

    def proj(i):
        return jnp.dot(hb, win_ref[:, i * W:(i + 1) * W], preferred_element_type=F32)

    cos = jnp.concatenate([cos_ref[...]] * 4, axis=1)
    sin = jnp.concatenate([sin_ref[...]] * 4, axis=1)
    lane_w = lax.broadcasted_iota(jnp.int32, (L, W), 1)
    first_half = (lane_w & (HEAD_DIM - 1)) < (HEAD_DIM // 2)

    def rot(t):
        partner = jnp.where(first_half, pltpu.roll(t, W - HEAD_DIM // 2, 1),
                            pltpu.roll(t, HEAD_DIM // 2, 1))
        return t * cos + partner * sin

    q = rot(proj(0))
    k = rot(proj(1)) * (HEAD_DIM ** -0.5)
    v = proj(2)
    vb = v.astype(BF16)
    kb = k.astype(BF16)

    lgl = lgl_ref[...]
    rowf = lax.broadcasted_iota(jnp.int32, (L, W), 0).astype(F32)
    qd = q * jnp.exp(lgl * (rowf + 1.0))
    kd = k * jnp.exp(lgl * (float(L - 1) - rowf))
    blk = blk_ref[...]
    HW = W // 2
    blk_f = blk.astype(F32)
    qdb = qd.astype(BF16)
    kdb = kd.astype(BF16)
    state_decay = jnp.exp(lgl * float(L))
    inter = []
    for hf in range(2):
        sl = slice(hf * HW, (hf + 1) * HW)
        st = state_ref[hf]
        inter.append(jnp.dot(qdb[:, sl], st.astype(BF16), preferred_element_type=F32))
        kv = lax.dot_general(kdb[:, sl], vb[:, sl], (((0,), (0,)), ((), ())),
                             preferred_element_type=F32)
        state_ref[hf] = st * state_decay[:, sl] + kv * blk_f
    y_inter = jnp.concatenate(inter, axis=1)

    def head_sums(t):
        tb = t.astype(BF16)
        return jnp.concatenate(
            [jnp.dot(tb[:, hf * HW:(hf + 1) * HW], blk, preferred_element_type=F32)
             for hf in range(2)], axis=1)

    ii = lax.broadcasted_iota(jnp.int32, (L, L), 0)
    jj = lax.broadcasted_iota(jnp.int32, (L, L), 1)
    dist = jnp.abs(ii - jj).astype(F32)
    allowed = (jj // CHUNK) <= (ii // CHUNK)
    lane_p = lax.broadcasted_iota(jnp.int32, (L, LANES), 1)
    lo_head = lane_p < HEAD_DIM
    pairs = []
    for p in range(RET_HEADS // 2):
        sl = slice(p * LANES, (p + 1) * LANES)
        qp, kp, vp = q[:, sl], kb[:, sl], vb[:, sl]
        ys = []
        for hh in range(2):
            head = 2 * p + hh
            keep = lo_head if hh == 0 else jnp.logical_not(lo_head)
            qh = jnp.where(keep, qp, 0.0).astype(BF16)
            sc = lax.dot_general(qh, kp, (((1,), (1,)), ((), ())),
                                 preferred_element_type=F32)
            decay = jnp.where(allowed, jnp.exp(lg_ref[head] * dist), 0.0)
            ys.append(jnp.dot((sc * decay).astype(BF16), vp, preferred_element_type=F32))
        pairs.append(jnp.where(lo_head, ys[0], ys[1]))
    y = jnp.concatenate(pairs, axis=1) + y_inter

    inv_hd = 1.0 / HEAD_DIM
    mu = head_sums(y) * inv_hd
    d = y - mu
    var = head_sums(d * d) * inv_hd
    g = proj(3)
    y_ret = _silu(g) * (d * lax.rsqrt(var + GN_EPS)) * bret_ref[...]

    b_gate = proj(4)
    u = proj(5) * proj(6)
    ubuf_ref[SUBLANES:SUBLANES + L, :] = u
    u1 = ubuf_ref[SUBLANES - 1:SUBLANES - 1 + L, :]
    u2 = ubuf_ref[SUBLANES - 2:SUBLANES - 2 + L, :]
    ubuf_ref[0:SUBLANES, :] = ubuf_ref[L:L + SUBLANES, :]
    cw = convw_ref[...]
    conv = u2 * cw[0:1] + u1 * cw[1:2] + u * cw[2:3] + convb_ref[...]
    yc = b_gate * conv
    msc = head_sums(yc * yc) * (1.0 / CONV_GROUP_DIM)
    y_conv = yc * lax.rsqrt(msc + EPS) * bconv_ref[...]

    mix = (jnp.dot(y_ret.astype(BF16), wout_ref[0:W, :], preferred_element_type=F32)
           + jnp.dot(y_conv.astype(BF16), wout_ref[W:2 * W, :], preferred_element_type=F32))
    x1 = x + gate_m * mix
    x1_ref[0] = x1

    ms2 = jnp.mean(x1 * x1, axis=-1, keepdims=True)
    h2 = x1 * lax.rsqrt(ms2 + EPS) * gffn_ref[...]
    h2 = h2 * (1.0 + scale_f) + shift_f
    for cch in range(d_model // LANES):
        h2_ref[pl.ds(cch, L, stride=SUBLANES), :] = h2[:, cch * LANES:(cch + 1) * LANES]

    hi = h2.astype(BF16)
    lo = (h2 - hi.astype(F32)).astype(BF16)
    w2 = wr_ref[...]
    nt_dims = (((1,), (1,)), ((), ()))
    parts = (lax.dot_general(hi, w2, nt_dims, preferred_element_type=F32)
             + lax.dot_general(lo, w2, nt_dims, preferred_element_type=F32))
    logit_ref[...] = parts[:, :ROUTER_LANES] + parts[:, ROUTER_LANES:] + br_ref[...]


ROUTER_TILE = 2048
RANK_BLOCK = 256


def _router_kernel(logit_ref, slab_ref, cnt_out_ref, cnt_ref):
    T = ROUTER_TILE

    @pl.when(pl.program_id(0) == 0)
    def _():
        cnt_ref[...] = jnp.zeros_like(cnt_ref)

    logits = logit_ref[...]
    lane = lax.broadcasted_iota(jnp.int32, (T, ROUTER_LANES), 1).astype(F32)
    neg_inf = F32(-jnp.inf)
    big = F32(1e9)
    gmask = lane < float(N_GROUPS)
    lgm = jnp.where(gmask, logits, neg_inf)
    gexp = jnp.exp(lgm - jnp.max(lgm, axis=-1, keepdims=True))
    gp = gexp / jnp.sum(gexp, axis=-1, keepdims=True)
    g_top = jnp.max(gp, axis=-1, keepdims=True)
    g_idx = jnp.min(jnp.where(gmask & (gp == g_top), lane, big), axis=-1, keepdims=True)

    e_lo = float(N_GROUPS) + float(EXPERTS_PER_GROUP) * g_idx
    emask = (lane >= e_lo) & (lane < e_lo + float(EXPERTS_PER_GROUP))
    lem = jnp.where(emask, logits, neg_inf)
    eexp = jnp.exp(lem - jnp.max(lem, axis=-1, keepdims=True))
    ep = eexp / jnp.sum(eexp, axis=-1, keepdims=True)
    p1 = jnp.max(jnp.where(emask, ep, -1.0), axis=-1, keepdims=True)
    i1 = jnp.min(jnp.where(emask & (ep == p1), lane, big), axis=-1, keepdims=True)
    m2 = emask & (lane != i1)
    p2 = jnp.max(jnp.where(m2, ep, -1.0), axis=-1, keepdims=True)
    i2 = jnp.min(jnp.where(m2 & (ep == p2), lane, big), axis=-1, keepdims=True)
    den = p1 + p2
    w1 = p1 / den * g_top
    w2 = p2 / den * g_top
    e1 = i1 - float(N_GROUPS)
    e2 = i2 - float(N_GROUPS)

    oh1 = jnp.where(lane == e1, 1.0, 0.0)
    oh2 = jnp.where(lane == e2, 1.0, 0.0)
    rb = RANK_BLOCK
    ii = lax.broadcasted_iota(jnp.int32, (rb, rb), 0)
    jj = lax.broadcasted_iota(jnp.int32, (rb, rb), 1)
    ltri = jnp.where(jj < ii, 1.0, 0.0).astype(BF16)
    base = cnt_ref[...]
    ranks1, ranks2 = [], []
    for k in range(T // rb):
        o1 = oh1[k * rb:(k + 1) * rb]
        o2 = oh2[k * rb:(k + 1) * rb]
        r1 = jnp.dot(ltri, o1.astype(BF16), preferred_element_type=F32)
        r2 = jnp.dot(ltri, o2.astype(BF16), preferred_element_type=F32)
        c1 = jnp.sum(o1, axis=0, keepdims=True)
        c2 = jnp.sum(o2, axis=0, keepdims=True)
        ranks1.append(jnp.sum(o1 * (base + r1), axis=-1, keepdims=True))
        ranks2.append(jnp.sum(o2 * (base + c1 + r2), axis=-1, keepdims=True))
        base = base + c1 + c2
    rank1 = jnp.concatenate(ranks1, axis=0)
    rank2 = jnp.concatenate(ranks2, axis=0)
    cnt_ref[...] = base
    cnt_out_ref[...] = base

    slab = jnp.where(lane == 0.0, e1,
                     jnp.where(lane == 1.0, e2,
                               jnp.where(lane == 2.0, w1,
                                         jnp.where(lane == 3.0, w2,
                                                   jnp.where(lane == 4.0, rank1,
                                                             jnp.where(lane == 5.0, rank2, 0.0))))))
    slab_ref[...] = slab


def _router(logits):
    n = logits.shape[0]
    T = ROUTER_TILE
    return pl.pallas_call(
        _router_kernel,
        out_shape=(jax.ShapeDtypeStruct((n, ROUTER_LANES), F32),
                   jax.ShapeDtypeStruct((1, ROUTER_LANES), F32)),
        grid=(n // T,),
        in_specs=[pl.BlockSpec((T, ROUTER_LANES), lambda i: (i, 0))],
        out_specs=(pl.BlockSpec((T, ROUTER_LANES), lambda i: (i, 0)),
                   pl.BlockSpec((1, ROUTER_LANES), lambda i: (0, 0))),
        scratch_shapes=[pltpu.VMEM((1, ROUTER_LANES), F32)],
        compiler_params=pltpu.CompilerParams(
            dimension_semantics=("arbitrary",), vmem_limit_bytes=VMEM_LIMIT_BYTES),
        name="router",
    )(logits)


def _mixer(x, mod, cos128, sin128, gmix, win_b, convw, convb, bret, bconv, wout_b, gffn,
           wr2, br, lg, lgl, blk):
    bsz, seq, d = x.shape
    L = SEQ_TILE
    ns = seq // L
    W = RET_HEADS * HEAD_DIM
    n = bsz * seq
    const2 = lambda b, s: (0, 0)
    in_specs = [
        pl.BlockSpec(memory_space=pltpu.SMEM),
        pl.BlockSpec((1, L, d), lambda b, s: (b, s, 0)),
        pl.BlockSpec((1, 6, d), lambda b, s: (b, 0, 0)),
        pl.BlockSpec((L, LANES), lambda b, s: (b * ns + s, 0)),
        pl.BlockSpec((L, LANES), lambda b, s: (b * ns + s, 0)),
        pl.BlockSpec((1, d), const2),
        pl.BlockSpec(win_b.shape, const2),
        pl.BlockSpec(convw.shape, const2),
        pl.BlockSpec((1, W), const2),
        pl.BlockSpec((1, W), const2),
        pl.BlockSpec((1, W), const2),
        pl.BlockSpec(wout_b.shape, const2),
        pl.BlockSpec((1, d), const2),
        pl.BlockSpec(wr2.shape, const2),
        pl.BlockSpec((1, ROUTER_LANES), const2),
        pl.BlockSpec((1, W), const2),
        pl.BlockSpec((W // 2, W // 2), const2),
    ]
    assert d == SUBLANES * LANES, "one token must fill exactly one (8, 128) f32 tile"
    out_shape = (jax.ShapeDtypeStruct((bsz, seq, d), F32),
                 jax.ShapeDtypeStruct((n * SUBLANES, LANES), F32),
                 jax.ShapeDtypeStruct((n, ROUTER_LANES), F32))
    out_specs = (pl.BlockSpec((1, L, d), lambda b, s: (b, s, 0)),
                 pl.BlockSpec((L * SUBLANES, LANES), lambda b, s: (b * ns + s, 0)),
                 pl.BlockSpec((L, ROUTER_LANES), lambda b, s: (b * ns + s, 0)))
    return pl.pallas_call(
        _mixer_kernel,
        out_shape=out_shape,
        grid=(bsz, ns),
        in_specs=in_specs,
        out_specs=out_specs,
        scratch_shapes=[pltpu.VMEM((2, W // 2, W // 2), F32),
                        pltpu.VMEM((L + 2 * SUBLANES, W), F32)],
        compiler_params=pltpu.CompilerParams(
            dimension_semantics=("arbitrary", "arbitrary"),
            vmem_limit_bytes=VMEM_LIMIT_BYTES),
        name="mixer",
    )(lg, x, mod, cos128, sin128, gmix, win_b, convw, convb, bret, bconv, wout_b, gffn,
      wr2, br, lgl, blk)


def _load_token_rows(ref, rows):
    return jnp.concatenate(
        [ref[pl.ds(c, rows, stride=SUBLANES), :] for c in range(SUBLANES)], axis=1)


def _store_token_rows(ref, val):
    rows = val.shape[0]
    for c in range(SUBLANES):
        ref[pl.ds(c, rows, stride=SUBLANES), :] = val[:, c * LANES:(c + 1) * LANES]


ROW_DMA_UNROLL = 8


def _token_rows(ref, row):
    return ref.at[pl.ds(pl.multiple_of(row * SUBLANES, SUBLANES), SUBLANES), :]


def _expert_kernel(d0_ref, d1_ref, ps_ref, pn_ref, first_ref, ntile_ref, nused_ref,
                   h2_hbm, wg_ref, wu_ref, wd_ref, y_hbm,
                   xbuf, ybuf, wgb, wub, wdb, rid, isem, osem):
    tm = EXPERT_TILE
    rows = tm * SUBLANES
    e = pl.program_id(0)
    n_used = nused_ref[0]
    n_tiles = y_hbm.shape[0] // rows

    def tile_rows(ref, g):
        return ref.at[pl.ds(pl.multiple_of(g * rows, rows), rows), :]

    def out_copy(g, slot):
        return pltpu.make_async_copy(ybuf.at[slot], tile_rows(y_hbm, g), osem.at[slot])

    def start_gather(g, slot):
        base = g * tm
        for r in range(tm):
            pltpu.make_async_copy(_token_rows(h2_hbm, rid[base + r]),
                                  xbuf.at[slot, r * SUBLANES:(r + 1) * SUBLANES, :],
                                  isem.at[slot]).start(priority=r % 2)

    def wait_gather(slot):
        pltpu.make_async_copy(h2_hbm.at[pl.ds(0, rows), :], xbuf.at[slot], isem.at[slot]).wait()

    @pl.when(e == 0)
    def _():
        def pad_expert(ex, carry):
            def pad_row(r, c2):
                rid[ps_ref[ex] + r] = 0
                return c2
            lax.fori_loop(0, pn_ref[ex], pad_row, 0)
            return carry

        lax.fori_loop(0, pl.num_programs(0), pad_expert, 0)

        def place(t, carry):
            rid[d0_ref[t]] = t
            rid[d1_ref[t]] = t
            return carry

        lax.fori_loop(0, d0_ref.shape[0], place, 0, unroll=ROW_DMA_UNROLL)
        start_gather(0, 0)

    @pl.when(ntile_ref[e] > 0)
    def _():
        wgb[...] = wg_ref[0].astype(BF16)
        wub[...] = wu_ref[0].astype(BF16)
        wdb[...] = wd_ref[0].astype(BF16)

    def tile(g, carry):
        slot = g % 2
        wait_gather(slot)
        start_gather(jnp.minimum(g + 1, n_used - 1), 1 - slot)

        xb = _load_token_rows(xbuf.at[slot], tm).astype(BF16)
        a = jnp.dot(xb, wgb[...], preferred_element_type=F32)
        u = jnp.dot(xb, wub[...], preferred_element_type=F32)
        hid = (_silu(a) * u).astype(BF16)
        y = jnp.dot(hid, wdb[...], preferred_element_type=F32)

        @pl.when(g >= 2)
        def _():
            out_copy(g - 2, slot).wait()

        _store_token_rows(ybuf.at[slot], y)
        out_copy(g, slot).start()
        return carry

    first = first_ref[e]
    lax.fori_loop(first, first + ntile_ref[e], tile, 0)

    @pl.when(e == pl.num_programs(0) - 1)
    def _():
        wait_gather(n_used % 2)

        @pl.when(n_used >= 2)
        def _():
            out_copy(n_used - 2, n_used % 2).wait()
        out_copy(n_used - 1, (n_used - 1) % 2).wait()

        ybuf[0] = jnp.zeros((rows, LANES), F32)

        def fill(g, carry):
            out_copy(g, 0).start()
            return carry

        def drain(g, carry):
            out_copy(g, 0).wait()
            return carry

        lax.fori_loop(n_used, n_tiles, fill, 0)
        lax.fori_loop(n_used, n_tiles, drain, 0)


def _experts(dest0, dest1, pad_start, pad_n, first_tile, n_tile, n_used, h2t, wg, wu, wd,
             p_rows):
    tm = EXPERT_TILE
    n_exp, d, de = wg.shape
    grid_spec = pltpu.PrefetchScalarGridSpec(
        num_scalar_prefetch=7,
        grid=(n_exp,),
        in_specs=[
            pl.BlockSpec(memory_space=pl.ANY),
            pl.BlockSpec((1, d, de), lambda e, *_: (e, 0, 0)),
            pl.BlockSpec((1, d, de), lambda e, *_: (e, 0, 0)),
            pl.BlockSpec((1, de, d), lambda e, *_: (e, 0, 0)),
        ],
        out_specs=pl.BlockSpec(memory_space=pl.ANY),
        scratch_shapes=[pltpu.VMEM((2, tm * SUBLANES, LANES), F32),
                        pltpu.VMEM((2, tm * SUBLANES, LANES), F32),
                        pltpu.VMEM((d, de), BF16),
                        pltpu.VMEM((d, de), BF16),
                        pltpu.VMEM((de, d), BF16),
                        pltpu.SMEM((p_rows,), jnp.int32),
                        pltpu.SemaphoreType.DMA((2,)),
                        pltpu.SemaphoreType.DMA((2,))],
    )
    return pl.pallas_call(
        _expert_kernel,
        out_shape=jax.ShapeDtypeStruct((p_rows * SUBLANES, LANES), F32),
        grid_spec=grid_spec,
        compiler_params=pltpu.CompilerParams(
            dimension_semantics=("arbitrary",),
            vmem_limit_bytes=VMEM_LIMIT_BYTES),
        name="experts",
    )(dest0, dest1, pad_start, pad_n, first_tile, n_tile, n_used, h2t, wg, wu, wd)


def _combine_kernel(p0_ref, p1_ref, y_hbm, x1_ref, slab_ref, mod_ref, gfin_ref, o_ref,
                    ybuf, sem):
    tm = COMBINE_TILE
    i = pl.program_id(0)
    nt = pl.num_programs(0)

    def issue(tile, slot):
        base = tile * tm

        def body(r, carry):
            dst_row = pl.multiple_of(r * SUBLANES, SUBLANES)
            for j, p_ref in enumerate((p0_ref, p1_ref)):
                src_row = pl.multiple_of(p_ref[base + r] * SUBLANES, SUBLANES)
                pltpu.make_async_copy(y_hbm.at[pl.ds(src_row, SUBLANES), :],
                                      ybuf.at[slot, j, pl.ds(dst_row, SUBLANES), :],
                                      sem.at[slot]).start(priority=j)
            return carry

        lax.fori_loop(0, tm, body, 0, unroll=ROW_DMA_UNROLL)

    @pl.when(i == 0)
    def _():
        issue(0, 0)

    @pl.when(i + 1 < nt)
    def _():
        issue(i + 1, (i + 1) % 2)

    slot = i % 2
    for j in range(2):
        pltpu.make_async_copy(y_hbm.at[pl.ds(0, tm * SUBLANES), :], ybuf.at[slot, j],
                              sem.at[slot]).wait()

    slab = slab_ref[...]
    w0 = slab[:, 2:3]
    w1 = slab[:, 3:4]
    gate_f = mod_ref[0][5:6]
    moe = (w0 * _load_token_rows(ybuf.at[slot, 0], tm)
           + w1 * _load_token_rows(ybuf.at[slot, 1], tm))
    xo = x1_ref[...] + gate_f * moe
    ms = jnp.mean(xo * xo, axis=-1, keepdims=True)
    o_ref[...] = xo * lax.rsqrt(ms + EPS) * gfin_ref[...]


def _combine(p0, p1, y, x1, slab, mod, gfin, seq):
    n, d = x1.shape
    tm = COMBINE_TILE
    tiles_per_seq = seq // tm
    grid_spec = pltpu.PrefetchScalarGridSpec(
        num_scalar_prefetch=2,
        grid=(n // tm,),
        in_specs=[
            pl.BlockSpec(memory_space=pl.ANY),
            pl.BlockSpec((tm, d), lambda i, a, b: (i, 0)),
            pl.BlockSpec((tm, ROUTER_LANES), lambda i, a, b: (i, 0)),
            pl.BlockSpec((1, 6, d), lambda i, a, b: (i // tiles_per_seq, 0, 0)),
            pl.BlockSpec((1, d), lambda i, a, b: (0, 0)),
        ],
        out_specs=pl.BlockSpec((tm, d), lambda i, a, b: (i, 0)),
        scratch_shapes=[pltpu.VMEM((2, 2, tm * SUBLANES, LANES), F32),
                        pltpu.SemaphoreType.DMA((2,))],
    )
    return pl.pallas_call(
        _combine_kernel,
        out_shape=jax.ShapeDtypeStruct((n, d), F32),
        grid_spec=grid_spec,
        compiler_params=pltpu.CompilerParams(
            dimension_semantics=("arbitrary",),
            vmem_limit_bytes=VMEM_LIMIT_BYTES),
        name="combine",
    )(p0, p1, y, x1, slab, mod, gfin)


def _routing_plan(slab, counts_f, n):
    tm = EXPERT_TILE
    counts = counts_f[0, :N_EXPERTS].astype(jnp.int32)
    padded = ((counts + tm - 1) // tm) * tm
    ends = jnp.cumsum(padded)
    starts = ends - padded
    eid = slab[:, 0:2].astype(jnp.int32)
    rank = slab[:, 4:6].astype(jnp.int32)
    onehot = eid[:, :, None] == jnp.arange(N_EXPERTS, dtype=jnp.int32)[None, None, :]
    dest = jnp.sum(jnp.where(onehot, starts[None, None, :], 0), axis=-1) + rank
    p_rows = 2 * n + N_EXPERTS * tm
    n_used = (ends[-1] // tm).astype(jnp.int32)
    return (starts // tm, padded // tm, n_used.reshape(1), dest[:, 0], dest[:, 1],
            starts + counts, padded - counts, p_rows)


def kernel(x, c, positions, ada_w, ada_b, norm_mix_g, norm_ffn_g, w_in, conv_w, conv_b,
           beta_ret, beta_conv, w_out, router_group_w, router_group_b, router_expert_w,
           router_expert_b, expert_w_gate, expert_w_up, expert_w_down, norm_final_g):
    bsz, seq, d = x.shape
    n = bsz * seq
    depth = ada_w.shape[0]
    assert depth == 1, "the combine kernel fuses the trunk's final RMSNorm (single layer)"
    W = RET_HEADS * HEAD_DIM

    cos128, sin128 = _rope_tables(positions)
    heads = jnp.arange(RET_HEADS, dtype=F32)
    lg = jnp.log1p(-jnp.exp2(-5.0 - heads))
    lgl = jnp.repeat(lg, HEAD_DIM).reshape(1, W)
    assert CONV_GROUP_DIM == HEAD_DIM, "conv groups and retention heads share the 64-lane block sums"
    blk_np = np.kron(np.eye(RET_HEADS // 2, dtype=np.float32),
                     np.ones((HEAD_DIM, HEAD_DIM), np.float32))
    blk = jnp.asarray(blk_np, dtype=BF16)

    for l in range(depth):
        mod = _adaln(c, ada_w[l], ada_b[l]).reshape(bsz, 6, d)

        n_route = N_GROUPS + N_EXPERTS
        wr = jnp.concatenate([router_group_w[l].T, router_expert_w[l].T,
                              jnp.zeros((ROUTER_LANES - n_route, d), F32)], axis=0)
        wr_hi = wr.astype(BF16)
        wr_lo = (wr - wr_hi.astype(F32)).astype(BF16)
        wr2 = jnp.concatenate([wr_hi, wr_lo], axis=0)
        br = jnp.concatenate([router_group_b[l], router_expert_b[l]])
        br = jnp.pad(br, (0, ROUTER_LANES - br.shape[0])).reshape(1, ROUTER_LANES)

        x1, h2t, logits = _mixer(
            x, mod, cos128, sin128, norm_mix_g[l].reshape(1, d), w_in[l].astype(BF16),
            conv_w[l], conv_b[l].reshape(1, W), beta_ret[l].reshape(1, W),
            beta_conv[l].reshape(1, W), w_out[l].astype(BF16), norm_ffn_g[l].reshape(1, d),
            wr2, br, lg, lgl, blk)
        slab, counts = _router(logits)

        first_tile, n_tile, n_used, p0, p1, pad_start, pad_n, p_rows = _routing_plan(
            slab, counts, n)
        de = expert_w_gate.shape[-1]
        y = _experts(p0, p1, pad_start, pad_n, first_tile, n_tile, n_used, h2t,
                     expert_w_gate[l].reshape(N_EXPERTS, d, de),
                     expert_w_up[l].reshape(N_EXPERTS, d, de),
                     expert_w_down[l].reshape(N_EXPERTS, de, d), p_rows)
        out = _combine(p0, p1, y, x1.reshape(n, d), slab, mod, norm_final_g.reshape(1, d), seq)
        x = out.reshape(bsz, seq, d)
    return x
```

```python
import functools

import jax
import jax.numpy as jnp
import numpy as np
from jax import lax
from jax.experimental import pallas as pl
from jax.experimental.pallas import tpu as pltpu

F32 = jnp.float32
BF16 = jnp.bfloat16

CHUNK = 64
RET_HEADS = 8
HEAD_DIM = 64
CONV_GROUP_DIM = 64
ROPE_BASE = 10000.0
N_GROUPS = 4
EXPERTS_PER_GROUP = 8
N_EXPERTS = N_GROUPS * EXPERTS_PER_GROUP
EPS = 1e-6
GN_EPS = 1e-5

LANES = 128
SUBLANES = 8
VMEM_LIMIT_BYTES = 56 * 1024 * 1024

SEQ_TILE = 256
EXPERT_TILE = 256
COMBINE_TILE = 256
ROUTER_LANES = LANES


def _silu(v):
    return v * (1.0 / (1.0 + jnp.exp(-v)))


def _adaln_kernel(c_ref, w_ref, b_ref, o_ref):
    s = _silu(c_ref[...])
    o_ref[...] = jnp.dot(s, w_ref[...], precision=lax.Precision.HIGHEST,
                         preferred_element_type=F32) + b_ref[...]


def _adaln(c, w, b):
    bsz, d = c.shape
    n = w.shape[1]
    tn = 1024
    return pl.pallas_call(
        _adaln_kernel,
        out_shape=jax.ShapeDtypeStruct((bsz, n), F32),
        grid=(n // tn,),
        in_specs=[pl.BlockSpec((bsz, d), lambda j: (0, 0)),
                  pl.BlockSpec((d, tn), lambda j: (0, j)),
                  pl.BlockSpec((1, tn), lambda j: (0, j))],
        out_specs=pl.BlockSpec((bsz, tn), lambda j: (0, j)),
        name="adaln",
    )(c, w, b.reshape(1, n))


ROPE_ROWS = 1024


def _rope_kernel(pos_ref, invf_ref, cos_ref, sin_ref):
    half = HEAD_DIM // 2
    per_row = LANES // half
    r = ROPE_ROWS
    ang = pos_ref[...].astype(F32) * invf_ref[...]
    lane = lax.broadcasted_iota(jnp.int32, (r, LANES), 1)
    quarter = lane // half
    sign = jnp.where(quarter % 2 == 0, -1.0, 1.0)
    for table, out_ref, scale in ((jnp.cos(ang), cos_ref, None), (jnp.sin(ang), sin_ref, sign)):
        rolled = [table] + [pltpu.roll(table, half * k, 1) for k in range(1, per_row)]
        for q in range(per_row):
            val = rolled[(0 - q) % per_row]
            for k in range(1, per_row):
                val = jnp.where(quarter == k, rolled[(k - q) % per_row], val)
            if scale is not None:
                val = val * scale
            out_ref[pl.ds(q, r, stride=per_row), :] = val


def _rope_tables(positions):
    n = positions.size
    half = HEAD_DIM // 2
    inv_freq = ROPE_BASE ** (-jnp.arange(0, HEAD_DIM, 2, dtype=F32) / HEAD_DIM)
    per_row = LANES // half
    rows = n // per_row
    pos_rep = jnp.broadcast_to(positions.reshape(n, 1), (n, half)).reshape(rows, LANES)
    invf = jnp.tile(inv_freq, per_row).reshape(1, LANES)
    tr = ROPE_ROWS
    return pl.pallas_call(
        _rope_kernel,
        out_shape=(jax.ShapeDtypeStruct((n, LANES), F32),) * 2,
        grid=(rows // tr,),
        in_specs=[pl.BlockSpec((tr, LANES), lambda i: (i, 0)),
                  pl.BlockSpec((1, LANES), lambda i: (0, 0))],
        out_specs=(pl.BlockSpec((tr * per_row, LANES), lambda i: (i, 0)),) * 2,
        name="rope_table",
    )(pos_rep, invf)


def _load_token_rows(ref, rows):
    return jnp.concatenate(
        [ref[pl.ds(c, rows, stride=SUBLANES), :] for c in range(SUBLANES)], axis=1)


def _store_token_rows(ref, val):
    rows = val.shape[0]
    for c in range(SUBLANES):
        ref[pl.ds(c, rows, stride=SUBLANES), :] = val[:, c * LANES:(c + 1) * LANES]


def _token_rows(ref, row):
    return ref.at[pl.ds(pl.multiple_of(row * SUBLANES, SUBLANES), SUBLANES), :]


def _mixer_kernel(lg_ref, x_ref, mod_ref, cos_ref, sin_ref, gmix_ref, win_ref, convw_ref,
                  convb_ref, bret_ref, bconv_ref, wout_ref, gffn_ref, wr_ref, br_ref, lgl_ref,
                  blk_ref, x1_ref, h2_ref, logit_ref, state_ref, ubuf_ref):
    L = SEQ_TILE
    W = RET_HEADS * HEAD_DIM
    s = pl.program_id(1)

    @pl.when(s == 0)
    def _():
        state_ref[...] = jnp.zeros_like(state_ref)
        ubuf_ref[0:SUBLANES, :] = jnp.zeros((SUBLANES, W), F32)

    x = x_ref[0]
    mod = mod_ref[0]
    shift_m, scale_m, gate_m = mod[0:1], mod[1:2], mod[2:3]
    shift_f, scale_f = mod[3:4], mod[4:5]

    ms = jnp.mean(x * x, axis=-1, keepdims=True)
    h = x * lax.rsqrt(ms + EPS) * gmix_ref[...]
    h = h * (1.0 + scale_m) + shift_m
    hb = h.astype(BF16)

    def proj(i):
        return jnp.dot(hb, win_ref[:, i * W:(i + 1) * W], preferred_element_type=F32)

    cos = jnp.concatenate([cos_ref[...]] * 4, axis=1)
    sin = jnp.concatenate([sin_ref[...]] * 4, axis=1)
    lane_w = lax.broadcasted_iota(jnp.int32, (L, W), 1)
    first_half = (lane_w & (HEAD_DIM - 1)) < (HEAD_DIM // 2)

    def rot(t):
        partner = jnp.where(first_half, pltpu.roll(t, W - HEAD_DIM // 2, 1),
                            pltpu.roll(t, HEAD_DIM // 2, 1))
        return t * cos + partner * sin

    q = rot(proj(0))
    k = rot(proj(1)) * (HEAD_DIM ** -0.5)
    v = proj(2)
    vb = v.astype(BF16)
    kb = k.astype(BF16)

    lgl = lgl_ref[...]
    rowf = lax.broadcasted_iota(jnp.int32, (L, W), 0).astype(F32)
    qd = q * jnp.exp(lgl * (rowf + 1.0))
    kd = k * jnp.exp(lgl * (float(L - 1) - rowf))
    blk = blk_ref[...]
    HW = W // 2
    blk_f = blk.astype(F32)
    qdb = qd.astype(BF16)
    kdb = kd.astype(BF16)
    state_decay = jnp.exp(lgl * float(L))
    inter = []
    for hf in range(2):
        sl = slice(hf * HW, (hf + 1) * HW)
        st = state_ref[hf]
        inter.append(jnp.dot(qdb[:, sl], st.astype(BF16), preferred_element_type=F32))
        kv = lax.dot_general(kdb[:, sl], vb[:, sl], (((0,), (0,)), ((), ())),
                             preferred_element_type=F32)
        state_ref[hf] = st * state_decay[:, sl] + kv * blk_f
    y_inter = jnp.concatenate(inter, axis=1)

    def head_sums(t):
        tb = t.astype(BF16)
        return jnp.concatenate(
            [jnp.dot(tb[:, hf * HW:(hf + 1) * HW], blk, preferred_element_type=F32)
             for hf in range(2)], axis=1)

    ii = lax.broadcasted_iota(jnp.int32, (L, L), 0)
    jj = lax.broadcasted_iota(jnp.int32, (L, L), 1)
    dist = jnp.abs(ii - jj).astype(F32)
    allowed = (jj // CHUNK) <= (ii // CHUNK)
    lane_p = lax.broadcasted_iota(jnp.int32, (L, LANES), 1)
    lo_head = lane_p < HEAD_DIM
    pairs = []
    for p in range(RET_HEADS // 2):
        sl = slice(p * LANES, (p + 1) * LANES)
        qp, kp, vp = q[:, sl], kb[:, sl], vb[:, sl]
        ys = []
        for hh in range(2):
            head = 2 * p + hh
            keep = lo_head if hh == 0 else jnp.logical_not(lo_head)
            qh = jnp.where(keep, qp, 0.0).astype(BF16)
            sc = lax.dot_general(qh, kp, (((1,), (1,)), ((), ())),
                                 preferred_element_type=F32)
            decay = jnp.where(allowed, jnp.exp(lg_ref[head] * dist), 0.0)
            ys.append(jnp.dot((sc * decay).astype(BF16), vp, preferred_element_type=F32))
        pairs.append(jnp.where(lo_head, ys[0], ys[1]))
    y = jnp.concatenate(pairs, axis=1) + y_inter

    inv_hd = 1.0 / HEAD_DIM
    mu = head_sums(y) * inv_hd
    d = y - mu
    var = head_sums(d * d) * inv_hd
    g = proj(3)
    y_ret = _silu(g) * (d * lax.rsqrt(var + GN_EPS)) * bret_ref[...]

    b_gate = proj(4)
    u = proj(5) * proj(6)
    ubuf_ref[SUBLANES:SUBLANES + L, :] = u
    u1 = ubuf_ref[SUBLANES - 1:SUBLANES - 1 + L, :]
    u2 = ubuf_ref[SUBLANES - 2:SUBLANES - 2 + L, :]
    ubuf_ref[0:SUBLANES, :] = ubuf_ref[L:L + SUBLANES, :]
    cw = convw_ref[...]
    conv = u2 * cw[0:1] + u1 * cw[1:2] + u * cw[2:3] + convb_ref[...]
    yc = b_gate * conv
    msc = head_sums(yc * yc) * (1.0 / CONV_GROUP_DIM)
    y_conv = yc * lax.rsqrt(msc + EPS) * bconv_ref[...]

    mix = (jnp.dot(y_ret.astype(BF16), wout_ref[0:W, :], preferred_element_type=F32)
           + jnp.dot(y_conv.astype(BF16), wout_ref[W:2 * W, :], preferred_element_type=F32))
    x1 = x + gate_m * mix
    x1_ref[0] = x1

    ms2 = jnp.mean(x1 * x1, axis=-1, keepdims=True)
    h2 = x1 * lax.rsqrt(ms2 + EPS) * gffn_ref[...]
    h2 = h2 * (1.0 + scale_f) + shift_f
    _store_token_rows(h2_ref, h2)

    hi = h2.astype(BF16)
    lo = (h2 - hi.astype(F32)).astype(BF16)
    w2 = wr_ref[...]
    nt_dims = (((1,), (1,)), ((), ()))
    parts = (lax.dot_general(hi, w2, nt_dims, preferred_element_type=F32)
             + lax.dot_general(lo, w2, nt_dims, preferred_element_type=F32))
    logit_ref[...] = parts[:, :ROUTER_LANES] + parts[:, ROUTER_LANES:] + br_ref[...]


def _mixer(x, mod, cos128, sin128, gmix, win_b, convw, convb, bret, bconv, wout_b, gffn,
           wr2, br, lg, lgl, blk):
    bsz, seq, d = x.shape
    L = SEQ_TILE
    ns = seq // L
    W = RET_HEADS * HEAD_DIM
    n = bsz * seq
    const2 = lambda b, s: (0, 0)
    in_specs = [
        pl.BlockSpec(memory_space=pltpu.SMEM),
        pl.BlockSpec((1, L, d), lambda b, s: (b, s, 0)),
        pl.BlockSpec((1, 6, d), lambda b, s: (b, 0, 0)),
        pl.BlockSpec((L, LANES), lambda b, s: (b * ns + s, 0)),
        pl.BlockSpec((L, LANES), lambda b, s: (b * ns + s, 0)),
        pl.BlockSpec((1, d), const2),
        pl.BlockSpec(win_b.shape, const2),
        pl.BlockSpec(convw.shape, const2),
        pl.BlockSpec((1, W), const2),
        pl.BlockSpec((1, W), const2),
        pl.BlockSpec((1, W), const2),
        pl.BlockSpec(wout_b.shape, const2),
        pl.BlockSpec((1, d), const2),
        pl.BlockSpec(wr2.shape, const2),
        pl.BlockSpec((1, ROUTER_LANES), const2),
        pl.BlockSpec((1, W), const2),
        pl.BlockSpec((W // 2, W // 2), const2),
    ]
    assert d == SUBLANES * LANES, "one token must fill exactly one (8, 128) f32 tile"
    out_shape = (jax.ShapeDtypeStruct((bsz, seq, d), F32),
                 jax.ShapeDtypeStruct((n * SUBLANES, LANES), F32),
                 jax.ShapeDtypeStruct((n, ROUTER_LANES), F32))
    out_specs = (pl.BlockSpec((1, L, d), lambda b, s: (b, s, 0)),
                 pl.BlockSpec((L * SUBLANES, LANES), lambda b, s: (b * ns + s, 0)),
                 pl.BlockSpec((L, ROUTER_LANES), lambda b, s: (b * ns + s, 0)))
    return pl.pallas_call(
        _mixer_kernel,
        out_shape=out_shape,
        grid=(bsz, ns),
        in_specs=in_specs,
        out_specs=out_specs,
        scratch_shapes=[pltpu.VMEM((2, W // 2, W // 2), F32),
                        pltpu.VMEM((L + 2 * SUBLANES, W), F32)],
        compiler_params=pltpu.CompilerParams(
            dimension_semantics=("arbitrary", "arbitrary"),
            vmem_limit_bytes=VMEM_LIMIT_BYTES),
        name="mixer",
    )(lg, x, mod, cos128, sin128, gmix, win_b, convw, convb, bret, bconv, wout_b, gffn,
      wr2, br, lgl, blk)


ROUTER_TILE = 2048
RANK_BLOCK = 256


def _router_kernel(logit_ref, slab_ref, cnt_out_ref, cnt_ref):
    T = ROUTER_TILE

    @pl.when(pl.program_id(0) == 0)
    def _():
        cnt_ref[...] = jnp.zeros_like(cnt_ref)

    logits = logit_ref[...]
    lane = lax.broadcasted_iota(jnp.int32, (T, ROUTER_LANES), 1).astype(F32)
    neg_inf = F32(-jnp.inf)
    big = F32(1e9)
    gmask = lane < float(N_GROUPS)
    lgm = jnp.where(gmask, logits, neg_inf)
    gexp = jnp.exp(lgm - jnp.max(lgm, axis=-1, keepdims=True))
    gp = gexp / jnp.sum(gexp, axis=-1, keepdims=True)
    g_top = jnp.max(gp, axis=-1, keepdims=True)
    g_idx = jnp.min(jnp.where(gmask & (gp == g_top), lane, big), axis=-1, keepdims=True)

    e_lo = float(N_GROUPS) + float(EXPERTS_PER_GROUP) * g_idx
    emask = (lane >= e_lo) & (lane < e_lo + float(EXPERTS_PER_GROUP))
    lem = jnp.where(emask, logits, neg_inf)
    eexp = jnp.exp(lem - jnp.max(lem, axis=-1, keepdims=True))
    ep = eexp / jnp.sum(eexp, axis=-1, keepdims=True)
    p1 = jnp.max(jnp.where(emask, ep, -1.0), axis=-1, keepdims=True)
    i1 = jnp.min(jnp.where(emask & (ep == p1), lane, big), axis=-1, keepdims=True)
    m2 = emask & (lane != i1)
    p2 = jnp.max(jnp.where(m2, ep, -1.0), axis=-1, keepdims=True)
    i2 = jnp.min(jnp.where(m2 & (ep == p2), lane, big), axis=-1, keepdims=True)
    den = p1 + p2
    w1 = p1 / den * g_top
    w2 = p2 / den * g_top
    e1 = i1 - float(N_GROUPS)
    e2 = i2 - float(N_GROUPS)

    oh1 = jnp.where(lane == e1, 1.0, 0.0)
    oh2 = jnp.where(lane == e2, 1.0, 0.0)
    rb = RANK_BLOCK
    ii = lax.broadcasted_iota(jnp.int32, (rb, rb), 0)
    jj = lax.broadcasted_iota(jnp.int32, (rb, rb), 1)
    ltri = jnp.where(jj < ii, 1.0, 0.0).astype(BF16)
    base = cnt_ref[...]
    ranks1, ranks2 = [], []
    for k in range(T // rb):
        o1 = oh1[k * rb:(k + 1) * rb]
        o2 = oh2[k * rb:(k + 1) * rb]
        r1 = jnp.dot(ltri, o1.astype(BF16), preferred_element_type=F32)
        r2 = jnp.dot(ltri, o2.astype(BF16), preferred_element_type=F32)
        c1 = jnp.sum(o1, axis=0, keepdims=True)
        c2 = jnp.sum(o2, axis=0, keepdims=True)
        ranks1.append(jnp.sum(o1 * (base + r1), axis=-1, keepdims=True))
        ranks2.append(jnp.sum(o2 * (base + c1 + r2), axis=-1, keepdims=True))
        base = base + c1 + c2
    rank1 = jnp.concatenate(ranks1, axis=0)
    rank2 = jnp.concatenate(ranks2, axis=0)
    cnt_ref[...] = base
    cnt_out_ref[...] = base

    slab = jnp.where(lane == 0.0, e1,
                     jnp.where(lane == 1.0, e2,
                               jnp.where(lane == 2.0, w1,
                                         jnp.where(lane == 3.0, w2,
                                                   jnp.where(lane == 4.0, rank1,
                                                             jnp.where(lane == 5.0, rank2, 0.0))))))
    slab_ref[...] = slab


def _router(logits):
    n = logits.shape[0]
    T = ROUTER_TILE
    return pl.pallas_call(
        _router_kernel,
        out_shape=(jax.ShapeDtypeStruct((n, ROUTER_LANES), F32),
                   jax.ShapeDtypeStruct((1, ROUTER_LANES), F32)),
        grid=(n // T,),
        in_specs=[pl.BlockSpec((T, ROUTER_LANES), lambda i: (i, 0))],
        out_specs=(pl.BlockSpec((T, ROUTER_LANES), lambda i: (i, 0)),
                   pl.BlockSpec((1, ROUTER_LANES), lambda i: (0, 0))),
        scratch_shapes=[pltpu.VMEM((1, ROUTER_LANES), F32)],
        compiler_params=pltpu.CompilerParams(
            dimension_semantics=("arbitrary",), vmem_limit_bytes=VMEM_LIMIT_BYTES),
        name="router",
    )(logits)


DISPATCH_TILE = 512
ROW_DMA_UNROLL = 8
PAD_UNITS = tuple(1 << b for b in reversed(range(EXPERT_TILE.bit_length() - 1)))


DISPATCH_SLOTS = 3


def _dispatch_kernel(d0_ref, d1_ref, ps_ref, pn_ref, nused_ref, h2_hbm, xs_hbm,
                     stage, zbuf, isem, ssem, zsem):
    dt = DISPATCH_TILE
    ps = SUBLANES
    i = pl.program_id(0)
    nsteps = pl.num_programs(0)
    zrows = PAD_UNITS[0]

    def in_copy(blk, slot):
        src = h2_hbm.at[pl.ds(pl.multiple_of(blk * (dt * ps), dt * ps), dt * ps), :]
        return pltpu.make_async_copy(src, stage.at[slot], isem.at[slot])

    def wait_rows(slot):
        for _ in range(2):
            pltpu.make_async_copy(stage.at[slot], xs_hbm.at[pl.ds(0, dt * ps), :],
                                  ssem.at[slot]).wait()

    def pad_copy(start, unit):
        return pltpu.make_async_copy(zbuf.at[pl.ds(0, unit * ps), :],
                                     xs_hbm.at[pl.ds(pl.multiple_of(start * ps, ps), unit * ps), :],
                                     zsem)

    def pad_pass(do):
        def per_expert(e, carry):
            start = ps_ref[e]
            npad = pn_ref[e]
            for unit in PAD_UNITS:
                @pl.when((npad & unit) != 0)
                def _():
                    do(pad_copy(start + (npad & ~(2 * unit - 1)), unit))
            return carry
        lax.fori_loop(0, N_EXPERTS, per_expert, 0)

    def tail_pass(do):
        def per_unit(k, carry):
            do(pad_copy(k * zrows, zrows))
            return carry
        per_tile = EXPERT_TILE // zrows
        n_units = xs_hbm.shape[0] // (zrows * ps)
        lax.fori_loop(nused_ref[0] * per_tile, n_units, per_unit, 0)

    slot = i % DISPATCH_SLOTS

    @pl.when(i == 0)
    def _():
        in_copy(0, 0).start()

        @pl.when(nsteps > 1)
        def _():
            in_copy(1, 1).start()

        zbuf[...] = jnp.zeros_like(zbuf)
        pad_pass(lambda cp: cp.start())
        tail_pass(lambda cp: cp.start())

    @pl.when(i >= 1)
    def _():
        wait_rows((i - 1) % DISPATCH_SLOTS)

    @pl.when(i + 2 < nsteps)
    def _():
        in_copy(i + 2, (i + 2) % DISPATCH_SLOTS).start()

    in_copy(i, slot).wait()
    base = i * dt
    src_ref = stage.at[slot]

    def body(r, carry):
        src = _token_rows(src_ref, r)
        for prio, d_ref in enumerate((d0_ref, d1_ref)):
            pltpu.make_async_copy(src, _token_rows(xs_hbm, d_ref[base + r]),
                                  ssem.at[slot]).start(priority=prio)
        return carry

    lax.fori_loop(0, dt, body, 0, unroll=ROW_DMA_UNROLL)

    @pl.when(i == nsteps - 1)
    def _():
        wait_rows(slot)
        pad_pass(lambda cp: cp.wait())
        tail_pass(lambda cp: cp.wait())


def _dispatch(dest0, dest1, pad_start, pad_n, n_used, h2t, p_rows):
    n = dest0.shape[0]
    dt = DISPATCH_TILE
    grid_spec = pltpu.PrefetchScalarGridSpec(
        num_scalar_prefetch=5,
        grid=(n // dt,),
        in_specs=[pl.BlockSpec(memory_space=pl.ANY)],
        out_specs=pl.BlockSpec(memory_space=pl.ANY),
        scratch_shapes=[pltpu.VMEM((DISPATCH_SLOTS, dt * SUBLANES, LANES), F32),
                        pltpu.VMEM((PAD_UNITS[0] * SUBLANES, LANES), F32),
                        pltpu.SemaphoreType.DMA((DISPATCH_SLOTS,)),
                        pltpu.SemaphoreType.DMA((DISPATCH_SLOTS,)),
                        pltpu.SemaphoreType.DMA(())],
    )
    return pl.pallas_call(
        _dispatch_kernel,
        out_shape=jax.ShapeDtypeStruct((p_rows * SUBLANES, LANES), F32),
        grid_spec=grid_spec,
        compiler_params=pltpu.CompilerParams(dimension_semantics=("arbitrary",)),
        name="dispatch",
    )(dest0, dest1, pad_start, pad_n, n_used, h2t)


EXPERT_IN_SLOTS = 4
EXPERT_OUT_SLOTS = 3


def _expert_kernel(first_ref, ntile_ref, nused_ref, xs_hbm, wg_ref, wu_ref, wd_ref, y_hbm,
                   xbuf, ybuf, wgb, wub, wdb, isem, osem):
    tm = EXPERT_TILE
    rows = tm * SUBLANES
    ni, no = EXPERT_IN_SLOTS, EXPERT_OUT_SLOTS
    e = pl.program_id(0)
    n_used = nused_ref[0]
    n_tiles = y_hbm.shape[0] // rows

    def tile_rows(ref, g):
        return ref.at[pl.ds(pl.multiple_of(g * rows, rows), rows), :]

    def in_copy(g):
        return pltpu.make_async_copy(tile_rows(xs_hbm, g), xbuf.at[g % ni], isem.at[g % ni])

    def out_copy(g):
        return pltpu.make_async_copy(ybuf.at[g % no], tile_rows(y_hbm, g), osem.at[g % no])

    @pl.when(e == 0)
    def _():
        for g0 in range(ni - 1):
            @pl.when(g0 < n_used)
            def _():
                in_copy(g0).start()

    @pl.when(ntile_ref[e] > 0)
    def _():
        wgb[...] = wg_ref[0].astype(BF16)
        wub[...] = wu_ref[0].astype(BF16)
        wdb[...] = wd_ref[0].astype(BF16)

    def tile(g, carry):
        in_copy(g).wait()

        @pl.when(g + ni - 1 < n_used)
        def _():
            in_copy(g + ni - 1).start()

        xb = _load_token_rows(xbuf.at[g % ni], tm).astype(BF16)
        a = jnp.dot(xb, wgb[...], preferred_element_type=F32)
        u = jnp.dot(xb, wub[...], preferred_element_type=F32)
        hid = (_silu(a) * u).astype(BF16)
        y = jnp.dot(hid, wdb[...], preferred_element_type=F32)

        @pl.when(g >= no)
        def _():
            out_copy(g - no).wait()

        _store_token_rows(ybuf.at[g % no], y)
        out_copy(g).start()
        return carry

    first = first_ref[e]
    lax.fori_loop(first, first + ntile_ref[e], tile, 0)

    @pl.when(e == pl.num_programs(0) - 1)
    def _():
        for back in range(no, 0, -1):
            @pl.when(n_used >= back)
            def _():
                out_copy(n_used - back).wait()

        ybuf[0] = jnp.zeros((rows, LANES), F32)

        def zero_copy(g):
            return pltpu.make_async_copy(ybuf.at[0], tile_rows(y_hbm, g), osem.at[0])

        def fill(g, carry):
            zero_copy(g).start()
            return carry

        def drain(g, carry):
            zero_copy(g).wait()
            return carry

        lax.fori_loop(n_used, n_tiles, fill, 0)
        lax.fori_loop(n_used, n_tiles, drain, 0)


def _experts(first_tile, n_tile, n_used, xs, wg, wu, wd):
    tm = EXPERT_TILE
    n_exp, d, de = wg.shape
    grid_spec = pltpu.PrefetchScalarGridSpec(
        num_scalar_prefetch=3,
        grid=(n_exp,),
        in_specs=[
            pl.BlockSpec(memory_space=pl.ANY),
            pl.BlockSpec((1, d, de), lambda e, *_: (e, 0, 0)),
            pl.BlockSpec((1, d, de), lambda e, *_: (e, 0, 0)),
            pl.BlockSpec((1, de, d), lambda e, *_: (e, 0, 0)),
        ],
        out_specs=pl.BlockSpec(memory_space=pl.ANY),
        scratch_shapes=[pltpu.VMEM((EXPERT_IN_SLOTS, tm * SUBLANES, LANES), F32),
                        pltpu.VMEM((EXPERT_OUT_SLOTS, tm * SUBLANES, LANES), F32),
                        pltpu.VMEM((d, de), BF16),
                        pltpu.VMEM((d, de), BF16),
                        pltpu.VMEM((de, d), BF16),
                        pltpu.SemaphoreType.DMA((EXPERT_IN_SLOTS,)),
                        pltpu.SemaphoreType.DMA((EXPERT_OUT_SLOTS,))],
    )
    return pl.pallas_call(
        _expert_kernel,
        out_shape=jax.ShapeDtypeStruct(xs.shape, F32),
        grid_spec=grid_spec,
        compiler_params=pltpu.CompilerParams(
            dimension_semantics=("arbitrary",),
            vmem_limit_bytes=VMEM_LIMIT_BYTES),
        name="experts",
    )(first_tile, n_tile, n_used, xs, wg, wu, wd)


COMBINE_SLOTS = 3


def _combine_kernel(p0_ref, p1_ref, y_hbm, x1_ref, slab_ref, mod_ref, gfin_ref, o_ref,
                    ybuf, sem):
    tm = COMBINE_TILE
    ahead = COMBINE_SLOTS - 1
    i = pl.program_id(0)
    nt = pl.num_programs(0)

    def issue(tile):
        base = tile * tm
        slot = tile % COMBINE_SLOTS

        def body(r, carry):
            for j, p_ref in enumerate((p0_ref, p1_ref)):
                pltpu.make_async_copy(_token_rows(y_hbm, p_ref[base + r]),
                                      _token_rows(ybuf.at[slot, j], r),
                                      sem.at[slot]).start(priority=j)
            return carry

        lax.fori_loop(0, tm, body, 0, unroll=ROW_DMA_UNROLL)

    @pl.when(i == 0)
    def _():
        for t0 in range(ahead):
            @pl.when(t0 < nt)
            def _():
                issue(t0)

    @pl.when(i + ahead < nt)
    def _():
        issue(i + ahead)

    slot = i % COMBINE_SLOTS
    for j in range(2):
        pltpu.make_async_copy(y_hbm.at[pl.ds(0, tm * SUBLANES), :], ybuf.at[slot, j],
                              sem.at[slot]).wait()

    slab = slab_ref[...]
    w0 = slab[:, 2:3]
    w1 = slab[:, 3:4]
    gate_f = mod_ref[0][5:6]
    moe = (w0 * _load_token_rows(ybuf.at[slot, 0], tm)
           + w1 * _load_token_rows(ybuf.at[slot, 1], tm))
    xo = x1_ref[...] + gate_f * moe
    ms = jnp.mean(xo * xo, axis=-1, keepdims=True)
    o_ref[...] = xo * lax.rsqrt(ms + EPS) * gfin_ref[...]


def _combine(p0, p1, y, x1, slab, mod, gfin, seq):
    n, d = x1.shape
    tm = COMBINE_TILE
    tiles_per_seq = seq // tm
    grid_spec = pltpu.PrefetchScalarGridSpec(
        num_scalar_prefetch=2,
        grid=(n // tm,),
        in_specs=[
            pl.BlockSpec(memory_space=pl.ANY),
            pl.BlockSpec((tm, d), lambda i, a, b: (i, 0)),
            pl.BlockSpec((tm, ROUTER_LANES), lambda i, a, b: (i, 0)),
            pl.BlockSpec((1, 6, d), lambda i, a, b: (i // tiles_per_seq, 0, 0)),
            pl.BlockSpec((1, d), lambda i, a, b: (0, 0)),
        ],
        out_specs=pl.BlockSpec((tm, d), lambda i, a, b: (i, 0)),
        scratch_shapes=[pltpu.VMEM((COMBINE_SLOTS, 2, tm * SUBLANES, LANES), F32),
                        pltpu.SemaphoreType.DMA((COMBINE_SLOTS,))],
    )
    return pl.pallas_call(
        _combine_kernel,
        out_shape=jax.ShapeDtypeStruct((n, d), F32),
        grid_spec=grid_spec,
        compiler_params=pltpu.CompilerParams(
            dimension_semantics=("arbitrary",),
            vmem_limit_bytes=VMEM_LIMIT_BYTES),
        name="combine",
    )(p0, p1, y, x1, slab, mod, gfin)


def _routing_plan(slab, counts_f, n):
    tm = EXPERT_TILE
    counts = counts_f[0, :N_EXPERTS].astype(jnp.int32)
    padded = ((counts + tm - 1) // tm) * tm
    ends = jnp.cumsum(padded)
    starts = ends - padded
    eid = slab[:, 0:2].astype(jnp.int32)
    rank = slab[:, 4:6].astype(jnp.int32)
    onehot = eid[:, :, None] == jnp.arange(N_EXPERTS, dtype=jnp.int32)[None, None, :]
    dest = jnp.sum(jnp.where(onehot, starts[None, None, :], 0), axis=-1) + rank
    p_rows = 2 * n + N_EXPERTS * tm
    n_used = (ends[-1] // tm).astype(jnp.int32)
    return (starts // tm, padded // tm, n_used.reshape(1), dest[:, 0], dest[:, 1],
            starts + counts, padded - counts, p_rows)


def kernel(x, c, positions, ada_w, ada_b, norm_mix_g, norm_ffn_g, w_in, conv_w, conv_b,
           beta_ret, beta_conv, w_out, router_group_w, router_group_b, router_expert_w,
           router_expert_b, expert_w_gate, expert_w_up, expert_w_down, norm_final_g):
    bsz, seq, d = x.shape
    n = bsz * seq
    depth = ada_w.shape[0]
    assert depth == 1, "the combine kernel fuses the trunk's final RMSNorm (single layer)"
    W = RET_HEADS * HEAD_DIM

    cos128, sin128 = _rope_tables(positions)
    heads = jnp.arange(RET_HEADS, dtype=F32)
    lg = jnp.log1p(-jnp.exp2(-5.0 - heads))
    lgl = jnp.repeat(lg, HEAD_DIM).reshape(1, W)
    assert CONV_GROUP_DIM == HEAD_DIM, "conv groups and retention heads share the 64-lane block sums"
    blk_np = np.kron(np.eye(RET_HEADS // 2, dtype=np.float32),
                     np.ones((HEAD_DIM, HEAD_DIM), np.float32))
    blk = jnp.asarray(blk_np, dtype=BF16)

    for l in range(depth):
        mod = _adaln(c, ada_w[l], ada_b[l]).reshape(bsz, 6, d)

        n_route = N_GROUPS + N_EXPERTS
        wr = jnp.concatenate([router_group_w[l].T, router_expert_w[l].T,
                              jnp.zeros((ROUTER_LANES - n_route, d), F32)], axis=0)
        wr_hi = wr.astype(BF16)
        wr_lo = (wr - wr_hi.astype(F32)).astype(BF16)
        wr2 = jnp.concatenate([wr_hi, wr_lo], axis=0)
        br = jnp.concatenate([router_group_b[l], router_expert_b[l]])
        br = jnp.pad(br, (0, ROUTER_LANES - br.shape[0])).reshape(1, ROUTER_LANES)

        x1, h2p, logits = _mixer(
            x, mod, cos128, sin128, norm_mix_g[l].reshape(1, d), w_in[l].astype(BF16),
            conv_w[l], conv_b[l].reshape(1, W), beta_ret[l].reshape(1, W),
            beta_conv[l].reshape(1, W), w_out[l].astype(BF16), norm_ffn_g[l].reshape(1, d),
            wr2, br, lg, lgl, blk)
        slab, counts = _router(logits)

        first_tile, n_tile, n_used, p0, p1, pad_start, pad_n, p_rows = _routing_plan(
            slab, counts, n)
        xs = _dispatch(p0, p1, pad_start, pad_n, n_used, h2p, p_rows)
        de = expert_w_gate.shape[-1]
        y = _experts(first_tile, n_tile, n_used, xs,
                     expert_w_gate[l].reshape(N_EXPERTS, d, de),
                     expert_w_up[l].reshape(N_EXPERTS, d, de),
                     expert_w_down[l].reshape(N_EXPERTS, de, d))
        out = _combine(p0, p1, y, x1.reshape(n, d), slab, mod, norm_final_g.reshape(1, d), seq)
        x = out.reshape(bsz, seq, d)
    return x
```

```python
import functools

import jax
import jax.numpy as jnp
import numpy as np
from jax import lax
from jax.experimental import pallas as pl
from jax.experimental.pallas import tpu as pltpu

F32 = jnp.float32
BF16 = jnp.bfloat16

CHUNK = 64
RET_HEADS = 8
HEAD_DIM = 64
CONV_GROUP_DIM = 64
ROPE_BASE = 10000.0
N_GROUPS = 4
EXPERTS_PER_GROUP = 8
N_EXPERTS = N_GROUPS * EXPERTS_PER_GROUP
EPS = 1e-6
GN_EPS = 1e-5

LANES = 128
SUBLANES = 8
VMEM_LIMIT_BYTES = 56 * 1024 * 1024

SEQ_TILE = 256
EXPERT_TILE = 256
COMBINE_TILE = 256
ROUTER_LANES = LANES


def _silu(v):
    return v * (1.0 / (1.0 + jnp.exp(-v)))


def _adaln_kernel(c_ref, w_ref, b_ref, o_ref):
    s = _silu(c_ref[...])
    o_ref[...] = jnp.dot(s, w_ref[...], precision=lax.Precision.HIGHEST,
                         preferred_element_type=F32) + b_ref[...]


def _adaln(c, w, b):
    bsz, d = c.shape
    n = w.shape[1]
    tn = 1024
    return pl.pallas_call(
        _adaln_kernel,
        out_shape=jax.ShapeDtypeStruct((bsz, n), F32),
        grid=(n // tn,),
        in_specs=[pl.BlockSpec((bsz, d), lambda j: (0, 0)),
                  pl.BlockSpec((d, tn), lambda j: (0, j)),
                  pl.BlockSpec((1, tn), lambda j: (0, j))],
        out_specs=pl.BlockSpec((bsz, tn), lambda j: (0, j)),
        name="adaln",
    )(c, w, b.reshape(1, n))


ROPE_ROWS = 1024


def _rope_kernel(pos_ref, invf_ref, cos_ref, sin_ref):
    half = HEAD_DIM // 2
    per_row = LANES // half
    r = ROPE_ROWS
    ang = pos_ref[...].astype(F32) * invf_ref[...]
    lane = lax.broadcasted_iota(jnp.int32, (r, LANES), 1)
    quarter = lane // half
    sign = jnp.where(quarter % 2 == 0, -1.0, 1.0)
    for table, out_ref, scale in ((jnp.cos(ang), cos_ref, None), (jnp.sin(ang), sin_ref, sign)):
        rolled = [table] + [pltpu.roll(table, half * k, 1) for k in range(1, per_row)]
        for q in range(per_row):
            val = rolled[(0 - q) % per_row]
            for k in range(1, per_row):
                val = jnp.where(quarter == k, rolled[(k - q) % per_row], val)
            if scale is not None:
                val = val * scale
            out_ref[pl.ds(q, r, stride=per_row), :] = val


def _rope_tables(positions):
    n = positions.size
    half = HEAD_DIM // 2
    inv_freq = ROPE_BASE ** (-jnp.arange(0, HEAD_DIM, 2, dtype=F32) / HEAD_DIM)
    per_row = LANES // half
    rows = n // per_row
    pos_rep = jnp.broadcast_to(positions.reshape(n, 1), (n, half)).reshape(rows, LANES)
    invf = jnp.tile(inv_freq, per_row).reshape(1, LANES)
    tr = ROPE_ROWS
    return pl.pallas_call(
        _rope_kernel,
        out_shape=(jax.ShapeDtypeStruct((n, LANES), F32),) * 2,
        grid=(rows // tr,),
        in_specs=[pl.BlockSpec((tr, LANES), lambda i: (i, 0)),
                  pl.BlockSpec((1, LANES), lambda i: (0, 0))],
        out_specs=(pl.BlockSpec((tr * per_row, LANES), lambda i: (i, 0)),) * 2,
        name="rope_table",
    )(pos_rep, invf)


def _load_token_rows(ref, rows):
    return jnp.concatenate(
        [ref[pl.ds(c, rows, stride=SUBLANES), :] for c in range(SUBLANES)], axis=1)


def _store_token_rows(ref, val, row0=0):
    rows = val.shape[0]
    for c in range(SUBLANES):
        ref[pl.ds(row0 * SUBLANES + c, rows, stride=SUBLANES), :] = val[:, c * LANES:(c + 1) * LANES]


def _token_rows(ref, row):
    return ref.at[pl.ds(pl.multiple_of(row * SUBLANES, SUBLANES), SUBLANES), :]


MIXER_SUBTILES = 2


def _mixer_kernel(*refs):
    state_ref, ubuf_ref = refs[-2:]

    @pl.when(pl.program_id(1) == 0)
    def _():
        state_ref[...] = jnp.zeros_like(state_ref)
        ubuf_ref[0:SUBLANES, :] = jnp.zeros((SUBLANES, ubuf_ref.shape[1]), F32)

    for sub in range(MIXER_SUBTILES):
        _mixer_tile(sub, *refs)


def _mixer_tile(sub, lg_ref, x_ref, mod_ref, cos_ref, sin_ref, gmix_ref, win_ref, convw_ref,
                convb_ref, bret_ref, bconv_ref, wout_ref, gffn_ref, wr_ref, br_ref, lgl_ref,
                blk_ref, x1_ref, h2_ref, logit_ref, state_ref, ubuf_ref):
    L = SEQ_TILE
    W = RET_HEADS * HEAD_DIM
    tile_rows = slice(sub * L, (sub + 1) * L)

    x = x_ref[0, tile_rows, :]
    mod = mod_ref[0]
    shift_m, scale_m, gate_m = mod[0:1], mod[1:2], mod[2:3]
    shift_f, scale_f = mod[3:4], mod[4:5]

    ms = jnp.mean(x * x, axis=-1, keepdims=True)
    h = x * lax.rsqrt(ms + EPS) * gmix_ref[...]
    h = h * (1.0 + scale_m) + shift_m
    hb = h.astype(BF16)

    def proj(i):
        return jnp.dot(hb, win_ref[:, i * W:(i + 1) * W], preferred_element_type=F32)

    cos = jnp.concatenate([cos_ref[tile_rows, :]] * 4, axis=1)
    sin = jnp.concatenate([sin_ref[tile_rows, :]] * 4, axis=1)
    lane_w = lax.broadcasted_iota(jnp.int32, (L, W), 1)
    first_half = (lane_w & (HEAD_DIM - 1)) < (HEAD_DIM // 2)

    def rot(t):
        partner = jnp.where(first_half, pltpu.roll(t, W - HEAD_DIM // 2, 1),
                            pltpu.roll(t, HEAD_DIM // 2, 1))
        return t * cos + partner * sin

    q = rot(proj(0))
    k = rot(proj(1)) * (HEAD_DIM ** -0.5)
    v = proj(2)
    vb = v.astype(BF16)
    kb = k.astype(BF16)

    lgl = lgl_ref[...]
    rowf = lax.broadcasted_iota(jnp.int32, (L, W), 0).astype(F32)
    qd = q * jnp.exp(lgl * (rowf + 1.0))
    kd = k * jnp.exp(lgl * (float(L - 1) - rowf))
    blk = blk_ref[...]
    HW = W // 2
    blk_f = blk.astype(F32)
    qdb = qd.astype(BF16)
    kdb = kd.astype(BF16)
    state_decay = jnp.exp(lgl * float(L))
    inter = []
    for hf in range(2):
        sl = slice(hf * HW, (hf + 1) * HW)
        st = state_ref[hf]
        inter.append(jnp.dot(qdb[:, sl], st.astype(BF16), preferred_element_type=F32))
        kv = lax.dot_general(kdb[:, sl], vb[:, sl], (((0,), (0,)), ((), ())),
                             preferred_element_type=F32)
        state_ref[hf] = st * state_decay[:, sl] + kv * blk_f
    y_inter = jnp.concatenate(inter, axis=1)

    def head_sums(t):
        tb = t.astype(BF16)
        return jnp.concatenate(
            [jnp.dot(tb[:, hf * HW:(hf + 1) * HW], blk, preferred_element_type=F32)
             for hf in range(2)], axis=1)

    ii = lax.broadcasted_iota(jnp.int32, (L, L), 0)
    jj = lax.broadcasted_iota(jnp.int32, (L, L), 1)
    dist = jnp.abs(ii - jj).astype(F32)
    allowed = (jj // CHUNK) <= (ii // CHUNK)
    lane_p = lax.broadcasted_iota(jnp.int32, (L, LANES), 1)
    lo_head = lane_p < HEAD_DIM
    pairs = []
    for p in range(RET_HEADS // 2):
        sl = slice(p * LANES, (p + 1) * LANES)
        qp, kp, vp = q[:, sl], kb[:, sl], vb[:, sl]
        ys = []
        for hh in range(2):
            head = 2 * p + hh
            keep = lo_head if hh == 0 else jnp.logical_not(lo_head)
            qh = jnp.where(keep, qp, 0.0).astype(BF16)
            sc = lax.dot_general(qh, kp, (((1,), (1,)), ((), ())),
                                 preferred_element_type=F32)
            decay = jnp.where(allowed, jnp.exp(lg_ref[head] * dist), 0.0)
            ys.append(jnp.dot((sc * decay).astype(BF16), vp, preferred_element_type=F32))
        pairs.append(jnp.where(lo_head, ys[0], ys[1]))
    y = jnp.concatenate(pairs, axis=1) + y_inter

    inv_hd = 1.0 / HEAD_DIM
    mu = head_sums(y) * inv_hd
    d = y - mu
    var = head_sums(d * d) * inv_hd
    g = proj(3)
    y_ret = _silu(g) * (d * lax.rsqrt(var + GN_EPS)) * bret_ref[...]

    b_gate = proj(4)
    u = proj(5) * proj(6)
    ubuf_ref[SUBLANES:SUBLANES + L, :] = u
    u1 = ubuf_ref[SUBLANES - 1:SUBLANES - 1 + L, :]
    u2 = ubuf_ref[SUBLANES - 2:SUBLANES - 2 + L, :]
    ubuf_ref[0:SUBLANES, :] = ubuf_ref[L:L + SUBLANES, :]
    cw = convw_ref[...]
    conv = u2 * cw[0:1] + u1 * cw[1:2] + u * cw[2:3] + convb_ref[...]
    yc = b_gate * conv
    msc = head_sums(yc * yc) * (1.0 / CONV_GROUP_DIM)
    y_conv = yc * lax.rsqrt(msc + EPS) * bconv_ref[...]

    mix = (jnp.dot(y_ret.astype(BF16), wout_ref[0:W, :], preferred_element_type=F32)
           + jnp.dot(y_conv.astype(BF16), wout_ref[W:2 * W, :], preferred_element_type=F32))
    x1 = x + gate_m * mix
    x1_ref[0, tile_rows, :] = x1

    ms2 = jnp.mean(x1 * x1, axis=-1, keepdims=True)
    h2 = x1 * lax.rsqrt(ms2 + EPS) * gffn_ref[...]
    h2 = h2 * (1.0 + scale_f) + shift_f
    _store_token_rows(h2_ref, h2, row0=sub * L)

    hi = h2.astype(BF16)
    lo = (h2 - hi.astype(F32)).astype(BF16)
    w2 = wr_ref[...]
    nt_dims = (((1,), (1,)), ((), ()))
    parts = (lax.dot_general(hi, w2, nt_dims, preferred_element_type=F32)
             + lax.dot_general(lo, w2, nt_dims, preferred_element_type=F32))
    logit_ref[tile_rows, :] = parts[:, :ROUTER_LANES] + parts[:, ROUTER_LANES:] + br_ref[...]


def _mixer(x, mod, cos128, sin128, gmix, win_b, convw, convb, bret, bconv, wout_b, gffn,
           wr2, br, lg, lgl, blk):
    bsz, seq, d = x.shape
    L = SEQ_TILE * MIXER_SUBTILES
    ns = seq // L
    W = RET_HEADS * HEAD_DIM
    n = bsz * seq
    const2 = lambda b, s: (0, 0)
    in_specs = [
        pl.BlockSpec(memory_space=pltpu.SMEM),
        pl.BlockSpec((1, L, d), lambda b, s: (b, s, 0)),
        pl.BlockSpec((1, 6, d), lambda b, s: (b, 0, 0)),
        pl.BlockSpec((L, LANES), lambda b, s: (b * ns + s, 0)),
        pl.BlockSpec((L, LANES), lambda b, s: (b * ns + s, 0)),
        pl.BlockSpec((1, d), const2),
        pl.BlockSpec(win_b.shape, const2),
        pl.BlockSpec(convw.shape, const2),
        pl.BlockSpec((1, W), const2),
        pl.BlockSpec((1, W), const2),
        pl.BlockSpec((1, W), const2),
        pl.BlockSpec(wout_b.shape, const2),
        pl.BlockSpec((1, d), const2),
        pl.BlockSpec(wr2.shape, const2),
        pl.BlockSpec((1, ROUTER_LANES), const2),
        pl.BlockSpec((1, W), const2),
        pl.BlockSpec((W // 2, W // 2), const2),
    ]
    assert d == SUBLANES * LANES, "one token must fill exactly one (8, 128) f32 tile"
    out_shape = (jax.ShapeDtypeStruct((bsz, seq, d), F32),
                 jax.ShapeDtypeStruct((n * SUBLANES, LANES), F32),
                 jax.ShapeDtypeStruct((n, ROUTER_LANES), F32))
    out_specs = (pl.BlockSpec((1, L, d), lambda b, s: (b, s, 0)),
                 pl.BlockSpec((L * SUBLANES, LANES), lambda b, s: (b * ns + s, 0)),
                 pl.BlockSpec((L, ROUTER_LANES), lambda b, s: (b * ns + s, 0)))
    return pl.pallas_call(
        _mixer_kernel,
        out_shape=out_shape,
        grid=(bsz, ns),
        in_specs=in_specs,
        out_specs=out_specs,
        scratch_shapes=[pltpu.VMEM((2, W // 2, W // 2), F32),
                        pltpu.VMEM((SEQ_TILE + 2 * SUBLANES, W), F32)],
        compiler_params=pltpu.CompilerParams(
            dimension_semantics=("arbitrary", "arbitrary"),
            vmem_limit_bytes=VMEM_LIMIT_BYTES),
        name="mixer",
    )(lg, x, mod, cos128, sin128, gmix, win_b, convw, convb, bret, bconv, wout_b, gffn,
      wr2, br, lgl, blk)


ROUTER_TILE = 2048
RANK_BLOCK = 256


def _router_kernel(logit_ref, slab_ref, cnt_out_ref, cnt_ref):
    T = ROUTER_TILE

    @pl.when(pl.program_id(0) == 0)
    def _():
        cnt_ref[...] = jnp.zeros_like(cnt_ref)

    logits = logit_ref[...]
    lane = lax.broadcasted_iota(jnp.int32, (T, ROUTER_LANES), 1).astype(F32)
    neg_inf = F32(-jnp.inf)
    big = F32(1e9)
    gmask = lane < float(N_GROUPS)
    lgm = jnp.where(gmask, logits, neg_inf)
    gexp = jnp.exp(lgm - jnp.max(lgm, axis=-1, keepdims=True))
    gp = gexp / jnp.sum(gexp, axis=-1, keepdims=True)
    g_top = jnp.max(gp, axis=-1, keepdims=True)
    g_idx = jnp.min(jnp.where(gmask & (gp == g_top), lane, big), axis=-1, keepdims=True)

    e_lo = float(N_GROUPS) + float(EXPERTS_PER_GROUP) * g_idx
    emask = (lane >= e_lo) & (lane < e_lo + float(EXPERTS_PER_GROUP))
    lem = jnp.where(emask, logits, neg_inf)
    eexp = jnp.exp(lem - jnp.max(lem, axis=-1, keepdims=True))
    ep = eexp / jnp.sum(eexp, axis=-1, keepdims=True)
    p1 = jnp.max(jnp.where(emask, ep, -1.0), axis=-1, keepdims=True)
    i1 = jnp.min(jnp.where(emask & (ep == p1), lane, big), axis=-1, keepdims=True)
    m2 = emask & (lane != i1)
    p2 = jnp.max(jnp.where(m2, ep, -1.0), axis=-1, keepdims=True)
    i2 = jnp.min(jnp.where(m2 & (ep == p2), lane, big), axis=-1, keepdims=True)
    den = p1 + p2
    w1 = p1 / den * g_top
    w2 = p2 / den * g_top
    e1 = i1 - float(N_GROUPS)
    e2 = i2 - float(N_GROUPS)

    oh1 = jnp.where(lane == e1, 1.0, 0.0)
    oh2 = jnp.where(lane == e2, 1.0, 0.0)
    rb = RANK_BLOCK
    ii = lax.broadcasted_iota(jnp.int32, (rb, rb), 0)
    jj = lax.broadcasted_iota(jnp.int32, (rb, rb), 1)
    ltri = jnp.where(jj < ii, 1.0, 0.0).astype(BF16)
    base = cnt_ref[...]
    ranks1, ranks2 = [], []
    for k in range(T // rb):
        o1 = oh1[k * rb:(k + 1) * rb]
        o2 = oh2[k * rb:(k + 1) * rb]
        r1 = jnp.dot(ltri, o1.astype(BF16), preferred_element_type=F32)
        r2 = jnp.dot(ltri, o2.astype(BF16), preferred_element_type=F32)
        c1 = jnp.sum(o1, axis=0, keepdims=True)
        c2 = jnp.sum(o2, axis=0, keepdims=True)
        ranks1.append(jnp.sum(o1 * (base + r1), axis=-1, keepdims=True))
        ranks2.append(jnp.sum(o2 * (base + c1 + r2), axis=-1, keepdims=True))
        base = base + c1 + c2
    rank1 = jnp.concatenate(ranks1, axis=0)
    rank2 = jnp.concatenate(ranks2, axis=0)
    cnt_ref[...] = base
    cnt_out_ref[...] = base

    slab = jnp.where(lane == 0.0, e1,
                     jnp.where(lane == 1.0, e2,
                               jnp.where(lane == 2.0, w1,
                                         jnp.where(lane == 3.0, w2,
                                                   jnp.where(lane == 4.0, rank1,
                                                             jnp.where(lane == 5.0, rank2, 0.0))))))
    slab_ref[...] = slab


def _router(logits):
    n = logits.shape[0]
    T = ROUTER_TILE
    return pl.pallas_call(
        _router_kernel,
        out_shape=(jax.ShapeDtypeStruct((n, ROUTER_LANES), F32),
                   jax.ShapeDtypeStruct((1, ROUTER_LANES), F32)),
        grid=(n // T,),
        in_specs=[pl.BlockSpec((T, ROUTER_LANES), lambda i: (i, 0))],
        out_specs=(pl.BlockSpec((T, ROUTER_LANES), lambda i: (i, 0)),
                   pl.BlockSpec((1, ROUTER_LANES), lambda i: (0, 0))),
        scratch_shapes=[pltpu.VMEM((1, ROUTER_LANES), F32)],
        compiler_params=pltpu.CompilerParams(
            dimension_semantics=("arbitrary",), vmem_limit_bytes=VMEM_LIMIT_BYTES),
        name="router",
    )(logits)


DISPATCH_TILE = 512
ROW_DMA_UNROLL = 8
PAD_UNITS = tuple(1 << b for b in reversed(range(EXPERT_TILE.bit_length() - 1)))


DISPATCH_SLOTS = 3


def _dispatch_kernel(d0_ref, d1_ref, ps_ref, pn_ref, nused_ref, h2_hbm, xs_hbm,
                     stage, zbuf, isem, ssem, zsem):
    dt = DISPATCH_TILE
    ps = SUBLANES
    i = pl.program_id(0)
    nsteps = pl.num_programs(0)
    zrows = PAD_UNITS[0]

    def in_copy(blk, slot):
        src = h2_hbm.at[pl.ds(pl.multiple_of(blk * (dt * ps), dt * ps), dt * ps), :]
        return pltpu.make_async_copy(src, stage.at[slot], isem.at[slot])

    def wait_rows(slot):
        for _ in range(2):
            pltpu.make_async_copy(stage.at[slot], xs_hbm.at[pl.ds(0, dt * ps), :],
                                  ssem.at[slot]).wait()

    def pad_copy(start, unit):
        return pltpu.make_async_copy(zbuf.at[pl.ds(0, unit * ps), :],
                                     xs_hbm.at[pl.ds(pl.multiple_of(start * ps, ps), unit * ps), :],
                                     zsem)

    def pad_pass(do):
        def per_expert(e, carry):
            start = ps_ref[e]
            npad = pn_ref[e]
            for unit in PAD_UNITS:
                @pl.when((npad & unit) != 0)
                def _():
                    do(pad_copy(start + (npad & ~(2 * unit - 1)), unit))
            return carry
        lax.fori_loop(0, N_EXPERTS, per_expert, 0)

    def tail_pass(do):
        def per_unit(k, carry):
            do(pad_copy(k * zrows, zrows))
            return carry
        per_tile = EXPERT_TILE // zrows
        n_units = xs_hbm.shape[0] // (zrows * ps)
        lax.fori_loop(nused_ref[0] * per_tile, n_units, per_unit, 0)

    slot = i % DISPATCH_SLOTS

    @pl.when(i == 0)
    def _():
        in_copy(0, 0).start()

        @pl.when(nsteps > 1)
        def _():
            in_copy(1, 1).start()

        zbuf[...] = jnp.zeros_like(zbuf)
        pad_pass(lambda cp: cp.start())
        tail_pass(lambda cp: cp.start())

    @pl.when(i >= 1)
    def _():
        wait_rows((i - 1) % DISPATCH_SLOTS)

    @pl.when(i + 2 < nsteps)
    def _():
        in_copy(i + 2, (i + 2) % DISPATCH_SLOTS).start()

    in_copy(i, slot).wait()
    base = i * dt
    src_ref = stage.at[slot]

    def body(r, carry):
        src = _token_rows(src_ref, r)
        for prio, d_ref in enumerate((d0_ref, d1_ref)):
            pltpu.make_async_copy(src, _token_rows(xs_hbm, d_ref[base + r]),
                                  ssem.at[slot]).start(priority=prio)
        return carry

    lax.fori_loop(0, dt, body, 0, unroll=ROW_DMA_UNROLL)

    @pl.when(i == nsteps - 1)
    def _():
        wait_rows(slot)
        pad_pass(lambda cp: cp.wait())
        tail_pass(lambda cp: cp.wait())


def _dispatch(dest0, dest1, pad_start, pad_n, n_used, h2t, p_rows):
    n = dest0.shape[0]
    dt = DISPATCH_TILE
    grid_spec = pltpu.PrefetchScalarGridSpec(
        num_scalar_prefetch=5,
        grid=(n // dt,),
        in_specs=[pl.BlockSpec(memory_space=pl.ANY)],
        out_specs=pl.BlockSpec(memory_space=pl.ANY),
        scratch_shapes=[pltpu.VMEM((DISPATCH_SLOTS, dt * SUBLANES, LANES), F32),
                        pltpu.VMEM((PAD_UNITS[0] * SUBLANES, LANES), F32),
                        pltpu.SemaphoreType.DMA((DISPATCH_SLOTS,)),
                        pltpu.SemaphoreType.DMA((DISPATCH_SLOTS,)),
                        pltpu.SemaphoreType.DMA(())],
    )
    return pl.pallas_call(
        _dispatch_kernel,
        out_shape=jax.ShapeDtypeStruct((p_rows * SUBLANES, LANES), F32),
        grid_spec=grid_spec,
        compiler_params=pltpu.CompilerParams(dimension_semantics=("arbitrary",)),
        name="dispatch",
    )(dest0, dest1, pad_start, pad_n, n_used, h2t)


EXPERT_IN_SLOTS = 4
EXPERT_OUT_SLOTS = 3


def _expert_kernel(first_ref, ntile_ref, nused_ref, xs_hbm, wg_ref, wu_ref, wd_ref, y_hbm,
                   xbuf, ybuf, wgb, wub, wdb, isem, osem):
    tm = EXPERT_TILE
    rows = tm * SUBLANES
    ni, no = EXPERT_IN_SLOTS, EXPERT_OUT_SLOTS
    e = pl.program_id(0)
    n_used = nused_ref[0]
    n_tiles = y_hbm.shape[0] // rows

    def tile_rows(ref, g):
        return ref.at[pl.ds(pl.multiple_of(g * rows, rows), rows), :]

    def in_copy(g):
        return pltpu.make_async_copy(tile_rows(xs_hbm, g), xbuf.at[g % ni], isem.at[g % ni])

    def out_copy(g):
        return pltpu.make_async_copy(ybuf.at[g % no], tile_rows(y_hbm, g), osem.at[g % no])

    @pl.when(e == 0)
    def _():
        for g0 in range(ni - 1):
            @pl.when(g0 < n_used)
            def _():
                in_copy(g0).start()

    @pl.when(ntile_ref[e] > 0)
    def _():
        wgb[...] = wg_ref[0].astype(BF16)
        wub[...] = wu_ref[0].astype(BF16)
        wdb[...] = wd_ref[0].astype(BF16)

    def tile(g, carry):
        in_copy(g).wait()

        @pl.when(g + ni - 1 < n_used)
        def _():
            in_copy(g + ni - 1).start()

        xb = _load_token_rows(xbuf.at[g % ni], tm).astype(BF16)
        a = jnp.dot(xb, wgb[...], preferred_element_type=F32)
        u = jnp.dot(xb, wub[...], preferred_element_type=F32)
        hid = (_silu(a) * u).astype(BF16)
        y = jnp.dot(hid, wdb[...], preferred_element_type=F32)

        @pl.when(g >= no)
        def _():
            out_copy(g - no).wait()

        _store_token_rows(ybuf.at[g % no], y)
        out_copy(g).start()
        return carry

    first = first_ref[e]
    lax.fori_loop(first, first + ntile_ref[e], tile, 0)

    @pl.when(e == pl.num_programs(0) - 1)
    def _():
        for back in range(no, 0, -1):
            @pl.when(n_used >= back)
            def _():
                out_copy(n_used - back).wait()

        ybuf[0] = jnp.zeros((rows, LANES), F32)

        def zero_copy(g):
            return pltpu.make_async_copy(ybuf.at[0], tile_rows(y_hbm, g), osem.at[0])

        def fill(g, carry):
            zero_copy(g).start()
            return carry

        def drain(g, carry):
            zero_copy(g).wait()
            return carry

        lax.fori_loop(n_used, n_tiles, fill, 0)
        lax.fori_loop(n_used, n_tiles, drain, 0)


def _experts(first_tile, n_tile, n_used, xs, wg, wu, wd):
    tm = EXPERT_TILE
    n_exp, d, de = wg.shape
    grid_spec = pltpu.PrefetchScalarGridSpec(
        num_scalar_prefetch=3,
        grid=(n_exp,),
        in_specs=[
            pl.BlockSpec(memory_space=pl.ANY),
            pl.BlockSpec((1, d, de), lambda e, *_: (e, 0, 0)),
            pl.BlockSpec((1, d, de), lambda e, *_: (e, 0, 0)),
            pl.BlockSpec((1, de, d), lambda e, *_: (e, 0, 0)),
        ],
        out_specs=pl.BlockSpec(memory_space=pl.ANY),
        scratch_shapes=[pltpu.VMEM((EXPERT_IN_SLOTS, tm * SUBLANES, LANES), F32),
                        pltpu.VMEM((EXPERT_OUT_SLOTS, tm * SUBLANES, LANES), F32),
                        pltpu.VMEM((d, de), BF16),
                        pltpu.VMEM((d, de), BF16),
                        pltpu.VMEM((de, d), BF16),
                        pltpu.SemaphoreType.DMA((EXPERT_IN_SLOTS,)),
                        pltpu.SemaphoreType.DMA((EXPERT_OUT_SLOTS,))],
    )
    return pl.pallas_call(
        _expert_kernel,
        out_shape=jax.ShapeDtypeStruct(xs.shape, F32),
        grid_spec=grid_spec,
        compiler_params=pltpu.CompilerParams(
            dimension_semantics=("arbitrary",),
            vmem_limit_bytes=VMEM_LIMIT_BYTES),
        name="experts",
    )(first_tile, n_tile, n_used, xs, wg, wu, wd)


COMBINE_SLOTS = 3


def _combine_kernel(p0_ref, p1_ref, y_hbm, x1_ref, slab_ref, mod_ref, gfin_ref, o_ref,
                    ybuf, sem):
    tm = COMBINE_TILE
    ahead = COMBINE_SLOTS - 1
    i = pl.program_id(0)
    nt = pl.num_programs(0)

    def issue(tile):
        base = tile * tm
        slot = tile % COMBINE_SLOTS

        def body(r, carry):
            for j, p_ref in enumerate((p0_ref, p1_ref)):
                pltpu.make_async_copy(_token_rows(y_hbm, p_ref[base + r]),
                                      _token_rows(ybuf.at[slot, j], r),
                                      sem.at[slot]).start(priority=j)
            return carry

        lax.fori_loop(0, tm, body, 0, unroll=ROW_DMA_UNROLL)

    @pl.when(i == 0)
    def _():
        for t0 in range(ahead):
            @pl.when(t0 < nt)
            def _():
                issue(t0)

    @pl.when(i + ahead < nt)
    def _():
        issue(i + ahead)

    slot = i % COMBINE_SLOTS
    for j in range(2):
        pltpu.make_async_copy(y_hbm.at[pl.ds(0, tm * SUBLANES), :], ybuf.at[slot, j],
                              sem.at[slot]).wait()

    slab = slab_ref[...]
    w0 = slab[:, 2:3]
    w1 = slab[:, 3:4]
    gate_f = mod_ref[0][5:6]
    moe = (w0 * _load_token_rows(ybuf.at[slot, 0], tm)
           + w1 * _load_token_rows(ybuf.at[slot, 1], tm))
    xo = x1_ref[...] + gate_f * moe
    ms = jnp.mean(xo * xo, axis=-1, keepdims=True)
    o_ref[...] = xo * lax.rsqrt(ms + EPS) * gfin_ref[...]


def _combine(p0, p1, y, x1, slab, mod, gfin, seq):
    n, d = x1.shape
    tm = COMBINE_TILE
    tiles_per_seq = seq // tm
    grid_spec = pltpu.PrefetchScalarGridSpec(
        num_scalar_prefetch=2,
        grid=(n // tm,),
        in_specs=[
            pl.BlockSpec(memory_space=pl.ANY),
            pl.BlockSpec((tm, d), lambda i, a, b: (i, 0)),
            pl.BlockSpec((tm, ROUTER_LANES), lambda i, a, b: (i, 0)),
            pl.BlockSpec((1, 6, d), lambda i, a, b: (i // tiles_per_seq, 0, 0)),
            pl.BlockSpec((1, d), lambda i, a, b: (0, 0)),
        ],
        out_specs=pl.BlockSpec((tm, d), lambda i, a, b: (i, 0)),
        scratch_shapes=[pltpu.VMEM((COMBINE_SLOTS, 2, tm * SUBLANES, LANES), F32),
                        pltpu.SemaphoreType.DMA((COMBINE_SLOTS,))],
    )
    return pl.pallas_call(
        _combine_kernel,
        out_shape=jax.ShapeDtypeStruct((n, d), F32),
        grid_spec=grid_spec,
        compiler_params=pltpu.CompilerParams(
            dimension_semantics=("arbitrary",),
            vmem_limit_bytes=VMEM_LIMIT_BYTES),
        name="combine",
    )(p0, p1, y, x1, slab, mod, gfin)


def _routing_plan(slab, counts_f, n):
    tm = EXPERT_TILE
    counts = counts_f[0, :N_EXPERTS].astype(jnp.int32)
    padded = ((counts + tm - 1) // tm) * tm
    ends = jnp.cumsum(padded)
    starts = ends - padded
    eid = slab[:, 0:2].astype(jnp.int32)
    rank = slab[:, 4:6].astype(jnp.int32)
    onehot = eid[:, :, None] == jnp.arange(N_EXPERTS, dtype=jnp.int32)[None, None, :]
    dest = jnp.sum(jnp.where(onehot, starts[None, None, :], 0), axis=-1) + rank
    p_rows = 2 * n + N_EXPERTS * tm
    n_used = (ends[-1] // tm).astype(jnp.int32)
    return (starts // tm, padded // tm, n_used.reshape(1), dest[:, 0], dest[:, 1],
            starts + counts, padded - counts, p_rows)


def kernel(x, c, positions, ada_w, ada_b, norm_mix_g, norm_ffn_g, w_in, conv_w, conv_b,
           beta_ret, beta_conv, w_out, router_group_w, router_group_b, router_expert_w,
           router_expert_b, expert_w_gate, expert_w_up, expert_w_down, norm_final_g):
    bsz, seq, d = x.shape
    n = bsz * seq
    depth = ada_w.shape[0]
    assert depth == 1, "the combine kernel fuses the trunk's final RMSNorm (single layer)"
    W = RET_HEADS * HEAD_DIM

    cos128, sin128 = _rope_tables(positions)
    heads = jnp.arange(RET_HEADS, dtype=F32)
    lg = jnp.log1p(-jnp.exp2(-5.0 - heads))
    lgl = jnp.repeat(lg, HEAD_DIM).reshape(1, W)
    assert CONV_GROUP_DIM == HEAD_DIM, "conv groups and retention heads share the 64-lane block sums"
    blk_np = np.kron(np.eye(RET_HEADS // 2, dtype=np.float32),
                     np.ones((HEAD_DIM, HEAD_DIM), np.float32))
    blk = jnp.asarray(blk_np, dtype=BF16)

    for l in range(depth):
        mod = _adaln(c, ada_w[l], ada_b[l]).reshape(bsz, 6, d)

        n_route = N_GROUPS + N_EXPERTS
        wr = jnp.concatenate([router_group_w[l].T, router_expert_w[l].T,
                              jnp.zeros((ROUTER_LANES - n_route, d), F32)], axis=0)
        wr_hi = wr.astype(BF16)
        wr_lo = (wr - wr_hi.astype(F32)).astype(BF16)
        wr2 = jnp.concatenate([wr_hi, wr_lo], axis=0)
        br = jnp.concatenate([router_group_b[l], router_expert_b[l]])
        br = jnp.pad(br, (0, ROUTER_LANES - br.shape[0])).reshape(1, ROUTER_LANES)

        x1, h2p, logits = _mixer(
            x, mod, cos128, sin128, norm_mix_g[l].reshape(1, d), w_in[l].astype(BF16),
            conv_w[l], conv_b[l].reshape(1, W), beta_ret[l].reshape(1, W),
            beta_conv[l].reshape(1, W), w_out[l].astype(BF16), norm_ffn_g[l].reshape(1, d),
            wr2, br, lg, lgl, blk)
        slab, counts = _router(logits)

        first_tile, n_tile, n_used, p0, p1, pad_start, pad_n, p_rows = _routing_plan(
            slab, counts, n)
        xs = _dispatch(p0, p1, pad_start, pad_n, n_used, h2p, p_rows)
        de = expert_w_gate.shape[-1]
        y = _experts(first_tile, n_tile, n_used, xs,
                     expert_w_gate[l].reshape(N_EXPERTS, d, de),
                     expert_w_up[l].reshape(N_EXPERTS, d, de),
                     expert_w_down[l].reshape(N_EXPERTS, de, d))
        out = _combine(p0, p1, y, x1.reshape(n, d), slab, mod, norm_final_g.reshape(1, d), seq)
        x = out.reshape(bsz, seq, d)
    return x
```

```python
import functools

import jax
import jax.numpy as jnp
import numpy as np
from jax import lax
from jax.experimental import pallas as pl
from jax.experimental.pallas import tpu as pltpu

F32 = jnp.float32
BF16 = jnp.bfloat16

CHUNK = 64
RET_HEADS = 8
HEAD_DIM = 64
CONV_GROUP_DIM = 64
ROPE_BASE = 10000.0
N_GROUPS = 4
EXPERTS_PER_GROUP = 8
N_EXPERTS = N_GROUPS * EXPERTS_PER_GROUP
EPS = 1e-6
GN_EPS = 1e-5

LANES = 128
SUBLANES = 8
VMEM_LIMIT_BYTES = 56 * 1024 * 1024

SEQ_TILE = 256
EXPERT_TILE = 256
COMBINE_TILE = 256
ROUTER_LANES = LANES


def _silu(v):
    return v * (1.0 / (1.0 + jnp.exp(-v)))


def _adaln_kernel(c_ref, w_ref, b_ref, o_ref):
    s = _silu(c_ref[...])
    o_ref[...] = jnp.dot(s, w_ref[...], precision=lax.Precision.HIGHEST,
                         preferred_element_type=F32) + b_ref[...]


def _adaln(c, w, b):
    bsz, d = c.shape
    n = w.shape[1]
    tn = 1024
    return pl.pallas_call(
        _adaln_kernel,
        out_shape=jax.ShapeDtypeStruct((bsz, n), F32),
        grid=(n // tn,),
        in_specs=[pl.BlockSpec((bsz, d), lambda j: (0, 0)),
                  pl.BlockSpec((d, tn), lambda j: (0, j)),
                  pl.BlockSpec((1, tn), lambda j: (0, j))],
        out_specs=pl.BlockSpec((bsz, tn), lambda j: (0, j)),
        name="adaln",
    )(c, w, b.reshape(1, n))


ROPE_ROWS = 1024


def _rope_kernel(pos_ref, invf_ref, cos_ref, sin_ref):
    half = HEAD_DIM // 2
    per_row = LANES // half
    r = ROPE_ROWS
    ang = pos_ref[...].astype(F32) * invf_ref[...]
    lane = lax.broadcasted_iota(jnp.int32, (r, LANES), 1)
    quarter = lane // half
    sign = jnp.where(quarter % 2 == 0, -1.0, 1.0)
    for table, out_ref, scale in ((jnp.cos(ang), cos_ref, None), (jnp.sin(ang), sin_ref, sign)):
        rolled = [table] + [pltpu.roll(table, half * k, 1) for k in range(1, per_row)]
        for q in range(per_row):
            val = rolled[(0 - q) % per_row]
            for k in range(1, per_row):
                val = jnp.where(quarter == k, rolled[(k - q) % per_row], val)
            if scale is not None:
                val = val * scale
            out_ref[pl.ds(q, r, stride=per_row), :] = val


def _rope_tables(positions):
    n = positions.size
    half = HEAD_DIM // 2
    inv_freq = ROPE_BASE ** (-jnp.arange(0, HEAD_DIM, 2, dtype=F32) / HEAD_DIM)
    per_row = LANES // half
    rows = n // per_row
    pos_rep = jnp.broadcast_to(positions.reshape(n, 1), (n, half)).reshape(rows, LANES)
    invf = jnp.tile(inv_freq, per_row).reshape(1, LANES)
    tr = ROPE_ROWS
    return pl.pallas_call(
        _rope_kernel,
        out_shape=(jax.ShapeDtypeStruct((n, LANES), F32),) * 2,
        grid=(rows // tr,),
        in_specs=[pl.BlockSpec((tr, LANES), lambda i: (i, 0)),
                  pl.BlockSpec((1, LANES), lambda i: (0, 0))],
        out_specs=(pl.BlockSpec((tr * per_row, LANES), lambda i: (i, 0)),) * 2,
        name="rope_table",
    )(pos_rep, invf)


def _load_token_rows(ref, rows):
    return jnp.concatenate(
        [ref[pl.ds(c, rows, stride=SUBLANES), :] for c in range(SUBLANES)], axis=1)


def _store_token_rows(ref, val, row0=0):
    rows = val.shape[0]
    for c in range(SUBLANES):
        ref[pl.ds(row0 * SUBLANES + c, rows, stride=SUBLANES), :] = val[:, c * LANES:(c + 1) * LANES]


def _pack_token_rows(val, scr):
    _store_token_rows(scr, val)
    return scr[...].astype(BF16)


def _unpack_token_rows(packed, scr):
    scr[...] = packed.astype(F32)
    return _load_token_rows(scr, packed.shape[0] // SUBLANES)


def _token_rows(ref, row):
    return ref.at[pl.ds(pl.multiple_of(row * SUBLANES, SUBLANES), SUBLANES), :]


MIXER_SUBTILES = 2


def _mixer_kernel(*refs):
    state_ref, ubuf_ref = refs[-3:-1]

    @pl.when(pl.program_id(1) == 0)
    def _():
        state_ref[...] = jnp.zeros_like(state_ref)
        ubuf_ref[0:SUBLANES, :] = jnp.zeros((SUBLANES, ubuf_ref.shape[1]), F32)

    for sub in range(MIXER_SUBTILES):
        _mixer_tile(sub, *refs)


def _mixer_tile(sub, lg_ref, x_ref, mod_ref, cos_ref, sin_ref, gmix_ref, win_ref, convw_ref,
                convb_ref, bret_ref, bconv_ref, wout_ref, gffn_ref, wr_ref, br_ref, lgl_ref,
                blk_ref, x1_ref, h2_ref, logit_ref, state_ref, ubuf_ref, pack_ref):
    L = SEQ_TILE
    W = RET_HEADS * HEAD_DIM
    tile_rows = slice(sub * L, (sub + 1) * L)

    x = x_ref[0, tile_rows, :]
    mod = mod_ref[0]
    shift_m, scale_m, gate_m = mod[0:1], mod[1:2], mod[2:3]
    shift_f, scale_f = mod[3:4], mod[4:5]

    ms = jnp.mean(x * x, axis=-1, keepdims=True)
    h = x * lax.rsqrt(ms + EPS) * gmix_ref[...]
    h = h * (1.0 + scale_m) + shift_m
    hb = h.astype(BF16)

    def proj(i):
        return jnp.dot(hb, win_ref[:, i * W:(i + 1) * W], preferred_element_type=F32)

    cos = jnp.concatenate([cos_ref[tile_rows, :]] * 4, axis=1)
    sin = jnp.concatenate([sin_ref[tile_rows, :]] * 4, axis=1)
    lane_w = lax.broadcasted_iota(jnp.int32, (L, W), 1)
    first_half = (lane_w & (HEAD_DIM - 1)) < (HEAD_DIM // 2)

    def rot(t):
        partner = jnp.where(first_half, pltpu.roll(t, W - HEAD_DIM // 2, 1),
                            pltpu.roll(t, HEAD_DIM // 2, 1))
        return t * cos + partner * sin

    q = rot(proj(0))
    k = rot(proj(1)) * (HEAD_DIM ** -0.5)
    v = proj(2)
    vb = v.astype(BF16)
    kb = k.astype(BF16)

    lgl = lgl_ref[...]
    rowf = lax.broadcasted_iota(jnp.int32, (L, W), 0).astype(F32)
    qd = q * jnp.exp(lgl * (rowf + 1.0))
    kd = k * jnp.exp(lgl * (float(L - 1) - rowf))
    blk = blk_ref[...]
    HW = W // 2
    blk_f = blk.astype(F32)
    qdb = qd.astype(BF16)
    kdb = kd.astype(BF16)
    state_decay = jnp.exp(lgl * float(L))
    inter = []
    for hf in range(2):
        sl = slice(hf * HW, (hf + 1) * HW)
        st = state_ref[hf]
        inter.append(jnp.dot(qdb[:, sl], st.astype(BF16), preferred_element_type=F32))
        kv = lax.dot_general(kdb[:, sl], vb[:, sl], (((0,), (0,)), ((), ())),
                             preferred_element_type=F32)
        state_ref[hf] = st * state_decay[:, sl] + kv * blk_f
    y_inter = jnp.concatenate(inter, axis=1)

    def head_sums(t):
        tb = t.astype(BF16)
        return jnp.concatenate(
            [jnp.dot(tb[:, hf * HW:(hf + 1) * HW], blk, preferred_element_type=F32)
             for hf in range(2)], axis=1)

    ii = lax.broadcasted_iota(jnp.int32, (L, L), 0)
    jj = lax.broadcasted_iota(jnp.int32, (L, L), 1)
    dist = jnp.abs(ii - jj).astype(F32)
    allowed = (jj // CHUNK) <= (ii // CHUNK)
    lane_p = lax.broadcasted_iota(jnp.int32, (L, LANES), 1)
    lo_head = lane_p < HEAD_DIM
    pairs = []
    for p in range(RET_HEADS // 2):
        sl = slice(p * LANES, (p + 1) * LANES)
        qp, kp, vp = q[:, sl], kb[:, sl], vb[:, sl]
        ys = []
        for hh in range(2):
            head = 2 * p + hh
            keep = lo_head if hh == 0 else jnp.logical_not(lo_head)
            qh = jnp.where(keep, qp, 0.0).astype(BF16)
            sc = lax.dot_general(qh, kp, (((1,), (1,)), ((), ())),
                                 preferred_element_type=F32)
            decay = jnp.where(allowed, jnp.exp(lg_ref[head] * dist), 0.0)
            ys.append(jnp.dot((sc * decay).astype(BF16), vp, preferred_element_type=F32))
        pairs.append(jnp.where(lo_head, ys[0], ys[1]))
    y = jnp.concatenate(pairs, axis=1) + y_inter

    inv_hd = 1.0 / HEAD_DIM
    mu = head_sums(y) * inv_hd
    d = y - mu
    var = head_sums(d * d) * inv_hd
    g = proj(3)
    y_ret = _silu(g) * (d * lax.rsqrt(var + GN_EPS)) * bret_ref[...]

    b_gate = proj(4)
    u = proj(5) * proj(6)
    ubuf_ref[SUBLANES:SUBLANES + L, :] = u
    u1 = ubuf_ref[SUBLANES - 1:SUBLANES - 1 + L, :]
    u2 = ubuf_ref[SUBLANES - 2:SUBLANES - 2 + L, :]
    ubuf_ref[0:SUBLANES, :] = ubuf_ref[L:L + SUBLANES, :]
    cw = convw_ref[...]
    conv = u2 * cw[0:1] + u1 * cw[1:2] + u * cw[2:3] + convb_ref[...]
    yc = b_gate * conv
    msc = head_sums(yc * yc) * (1.0 / CONV_GROUP_DIM)
    y_conv = yc * lax.rsqrt(msc + EPS) * bconv_ref[...]

    mix = (jnp.dot(y_ret.astype(BF16), wout_ref[0:W, :], preferred_element_type=F32)
           + jnp.dot(y_conv.astype(BF16), wout_ref[W:2 * W, :], preferred_element_type=F32))
    x1 = x + gate_m * mix
    x1_ref[0, tile_rows, :] = x1

    ms2 = jnp.mean(x1 * x1, axis=-1, keepdims=True)
    h2 = x1 * lax.rsqrt(ms2 + EPS) * gffn_ref[...]
    h2 = h2 * (1.0 + scale_f) + shift_f
    h2_ref[sub * L * SUBLANES:(sub + 1) * L * SUBLANES, :] = _pack_token_rows(h2, pack_ref)

    hi = h2.astype(BF16)
    lo = (h2 - hi.astype(F32)).astype(BF16)
    w2 = wr_ref[...]
    nt_dims = (((1,), (1,)), ((), ()))
    parts = (lax.dot_general(hi, w2, nt_dims, preferred_element_type=F32)
             + lax.dot_general(lo, w2, nt_dims, preferred_element_type=F32))
    logit_ref[tile_rows, :] = parts[:, :ROUTER_LANES] + parts[:, ROUTER_LANES:] + br_ref[...]


def _mixer(x, mod, cos128, sin128, gmix, win_b, convw, convb, bret, bconv, wout_b, gffn,
           wr2, br, lg, lgl, blk):
    bsz, seq, d = x.shape
    L = SEQ_TILE * MIXER_SUBTILES
    ns = seq // L
    W = RET_HEADS * HEAD_DIM
    n = bsz * seq
    const2 = lambda b, s: (0, 0)
    in_specs = [
        pl.BlockSpec(memory_space=pltpu.SMEM),
        pl.BlockSpec((1, L, d), lambda b, s: (b, s, 0)),
        pl.BlockSpec((1, 6, d), lambda b, s: (b, 0, 0)),
        pl.BlockSpec((L, LANES), lambda b, s: (b * ns + s, 0)),
        pl.BlockSpec((L, LANES), lambda b, s: (b * ns + s, 0)),
        pl.BlockSpec((1, d), const2),
        pl.BlockSpec(win_b.shape, const2),
        pl.BlockSpec(convw.shape, const2),
        pl.BlockSpec((1, W), const2),
        pl.BlockSpec((1, W), const2),
        pl.BlockSpec((1, W), const2),
        pl.BlockSpec(wout_b.shape, const2),
        pl.BlockSpec((1, d), const2),
        pl.BlockSpec(wr2.shape, const2),
        pl.BlockSpec((1, ROUTER_LANES), const2),
        pl.BlockSpec((1, W), const2),
        pl.BlockSpec((W // 2, W // 2), const2),
    ]
    assert d == SUBLANES * LANES, "one token must fill exactly one (8, 128) f32 tile"
    out_shape = (jax.ShapeDtypeStruct((bsz, seq, d), F32),
                 jax.ShapeDtypeStruct((n * SUBLANES, LANES), BF16),
                 jax.ShapeDtypeStruct((n, ROUTER_LANES), F32))
    out_specs = (pl.BlockSpec((1, L, d), lambda b, s: (b, s, 0)),
                 pl.BlockSpec((L * SUBLANES, LANES), lambda b, s: (b * ns + s, 0)),
                 pl.BlockSpec((L, ROUTER_LANES), lambda b, s: (b * ns + s, 0)))
    return pl.pallas_call(
        _mixer_kernel,
        out_shape=out_shape,
        grid=(bsz, ns),
        in_specs=in_specs,
        out_specs=out_specs,
        scratch_shapes=[pltpu.VMEM((2, W // 2, W // 2), F32),
                        pltpu.VMEM((SEQ_TILE + 2 * SUBLANES, W), F32),
                        pltpu.VMEM((SEQ_TILE * SUBLANES, LANES), F32)],
        compiler_params=pltpu.CompilerParams(
            dimension_semantics=("arbitrary", "arbitrary"),
            vmem_limit_bytes=VMEM_LIMIT_BYTES),
        name="mixer",
    )(lg, x, mod, cos128, sin128, gmix, win_b, convw, convb, bret, bconv, wout_b, gffn,
      wr2, br, lgl, blk)


ROUTER_TILE = 2048
RANK_BLOCK = 256


def _router_kernel(logit_ref, slab_ref, cnt_out_ref, cnt_ref):
    T = ROUTER_TILE

    @pl.when(pl.program_id(0) == 0)
    def _():
        cnt_ref[...] = jnp.zeros_like(cnt_ref)

    logits = logit_ref[...]
    lane = lax.broadcasted_iota(jnp.int32, (T, ROUTER_LANES), 1).astype(F32)
    neg_inf = F32(-jnp.inf)
    big = F32(1e9)
    gmask = lane < float(N_GROUPS)
    lgm = jnp.where(gmask, logits, neg_inf)
    gexp = jnp.exp(lgm - jnp.max(lgm, axis=-1, keepdims=True))
    gp = gexp / jnp.sum(gexp, axis=-1, keepdims=True)
    g_top = jnp.max(gp, axis=-1, keepdims=True)
    g_idx = jnp.min(jnp.where(gmask & (gp == g_top), lane, big), axis=-1, keepdims=True)

    e_lo = float(N_GROUPS) + float(EXPERTS_PER_GROUP) * g_idx
    emask = (lane >= e_lo) & (lane < e_lo + float(EXPERTS_PER_GROUP))
    lem = jnp.where(emask, logits, neg_inf)
    eexp = jnp.exp(lem - jnp.max(lem, axis=-1, keepdims=True))
    ep = eexp / jnp.sum(eexp, axis=-1, keepdims=True)
    p1 = jnp.max(jnp.where(emask, ep, -1.0), axis=-1, keepdims=True)
    i1 = jnp.min(jnp.where(emask & (ep == p1), lane, big), axis=-1, keepdims=True)
    m2 = emask & (lane != i1)
    p2 = jnp.max(jnp.where(m2, ep, -1.0), axis=-1, keepdims=True)
    i2 = jnp.min(jnp.where(m2 & (ep == p2), lane, big), axis=-1, keepdims=True)
    den = p1 + p2
    w1 = p1 / den * g_top
    w2 = p2 / den * g_top
    e1 = i1 - float(N_GROUPS)
    e2 = i2 - float(N_GROUPS)

    oh1 = jnp.where(lane == e1, 1.0, 0.0)
    oh2 = jnp.where(lane == e2, 1.0, 0.0)
    rb = RANK_BLOCK
    ii = lax.broadcasted_iota(jnp.int32, (rb, rb), 0)
    jj = lax.broadcasted_iota(jnp.int32, (rb, rb), 1)
    ltri = jnp.where(jj < ii, 1.0, 0.0).astype(BF16)
    base = cnt_ref[...]
    ranks1, ranks2 = [], []
    for k in range(T // rb):
        o1 = oh1[k * rb:(k + 1) * rb]
        o2 = oh2[k * rb:(k + 1) * rb]
        r1 = jnp.dot(ltri, o1.astype(BF16), preferred_element_type=F32)
        r2 = jnp.dot(ltri, o2.astype(BF16), preferred_element_type=F32)
        c1 = jnp.sum(o1, axis=0, keepdims=True)
        c2 = jnp.sum(o2, axis=0, keepdims=True)
        ranks1.append(jnp.sum(o1 * (base + r1), axis=-1, keepdims=True))
        ranks2.append(jnp.sum(o2 * (base + c1 + r2), axis=-1, keepdims=True))
        base = base + c1 + c2
    rank1 = jnp.concatenate(ranks1, axis=0)
    rank2 = jnp.concatenate(ranks2, axis=0)
    cnt_ref[...] = base
    cnt_out_ref[...] = base

    slab = jnp.where(lane == 0.0, e1,
                     jnp.where(lane == 1.0, e2,
                               jnp.where(lane == 2.0, w1,
                                         jnp.where(lane == 3.0, w2,
                                                   jnp.where(lane == 4.0, rank1,
                                                             jnp.where(lane == 5.0, rank2, 0.0))))))
    slab_ref[...] = slab


def _router(logits):
    n = logits.shape[0]
    T = ROUTER_TILE
    return pl.pallas_call(
        _router_kernel,
        out_shape=(jax.ShapeDtypeStruct((n, ROUTER_LANES), F32),
                   jax.ShapeDtypeStruct((1, ROUTER_LANES), F32)),
        grid=(n // T,),
        in_specs=[pl.BlockSpec((T, ROUTER_LANES), lambda i: (i, 0))],
        out_specs=(pl.BlockSpec((T, ROUTER_LANES), lambda i: (i, 0)),
                   pl.BlockSpec((1, ROUTER_LANES), lambda i: (0, 0))),
        scratch_shapes=[pltpu.VMEM((1, ROUTER_LANES), F32)],
        compiler_params=pltpu.CompilerParams(
            dimension_semantics=("arbitrary",), vmem_limit_bytes=VMEM_LIMIT_BYTES),
        name="router",
    )(logits)


DISPATCH_TILE = 512
ROW_DMA_UNROLL = 8
PAD_UNITS = tuple(1 << b for b in reversed(range(EXPERT_TILE.bit_length() - 1)))


DISPATCH_SLOTS = 3


def _dispatch_kernel(d0_ref, d1_ref, ps_ref, pn_ref, nused_ref, h2_hbm, xs_hbm,
                     stage, zbuf, isem, ssem, zsem):
    dt = DISPATCH_TILE
    ps = SUBLANES
    i = pl.program_id(0)
    nsteps = pl.num_programs(0)
    zrows = PAD_UNITS[0]

    def in_copy(blk, slot):
        src = h2_hbm.at[pl.ds(pl.multiple_of(blk * (dt * ps), dt * ps), dt * ps), :]
        return pltpu.make_async_copy(src, stage.at[slot], isem.at[slot])

    def wait_rows(slot):
        for _ in range(2):
            pltpu.make_async_copy(stage.at[slot], xs_hbm.at[pl.ds(0, dt * ps), :],
                                  ssem.at[slot]).wait()

    def pad_copy(start, unit):
        return pltpu.make_async_copy(zbuf.at[pl.ds(0, unit * ps), :],
                                     xs_hbm.at[pl.ds(pl.multiple_of(start * ps, ps), unit * ps), :],
                                     zsem)

    def pad_pass(do):
        def per_expert(e, carry):
            start = ps_ref[e]
            npad = pn_ref[e]
            for unit in PAD_UNITS:
                @pl.when((npad & unit) != 0)
                def _():
                    do(pad_copy(start + (npad & ~(2 * unit - 1)), unit))
            return carry
        lax.fori_loop(0, N_EXPERTS, per_expert, 0)

    def tail_pass(do):
        def per_unit(k, carry):
            do(pad_copy(k * zrows, zrows))
            return carry
        per_tile = EXPERT_TILE // zrows
        n_units = xs_hbm.shape[0] // (zrows * ps)
        lax.fori_loop(nused_ref[0] * per_tile, n_units, per_unit, 0)

    slot = i % DISPATCH_SLOTS

    @pl.when(i == 0)
    def _():
        in_copy(0, 0).start()

        @pl.when(nsteps > 1)
        def _():
            in_copy(1, 1).start()

        zbuf[...] = jnp.zeros_like(zbuf)
        pad_pass(lambda cp: cp.start())
        tail_pass(lambda cp: cp.start())

    @pl.when(i >= 1)
    def _():
        wait_rows((i - 1) % DISPATCH_SLOTS)

    @pl.when(i + 2 < nsteps)
    def _():
        in_copy(i + 2, (i + 2) % DISPATCH_SLOTS).start()

    in_copy(i, slot).wait()
    base = i * dt
    src_ref = stage.at[slot]

    def body(r, carry):
        src = _token_rows(src_ref, r)
        for prio, d_ref in enumerate((d0_ref, d1_ref)):
            pltpu.make_async_copy(src, _token_rows(xs_hbm, d_ref[base + r]),
                                  ssem.at[slot]).start(priority=prio)
        return carry

    lax.fori_loop(0, dt, body, 0, unroll=ROW_DMA_UNROLL)

    @pl.when(i == nsteps - 1)
    def _():
        wait_rows(slot)
        pad_pass(lambda cp: cp.wait())
        tail_pass(lambda cp: cp.wait())


def _dispatch(dest0, dest1, pad_start, pad_n, n_used, h2t, p_rows):
    n = dest0.shape[0]
    dt = DISPATCH_TILE
    grid_spec = pltpu.PrefetchScalarGridSpec(
        num_scalar_prefetch=5,
        grid=(n // dt,),
        in_specs=[pl.BlockSpec(memory_space=pl.ANY)],
        out_specs=pl.BlockSpec(memory_space=pl.ANY),
        scratch_shapes=[pltpu.VMEM((DISPATCH_SLOTS, dt * SUBLANES, LANES), BF16),
                        pltpu.VMEM((PAD_UNITS[0] * SUBLANES, LANES), BF16),
                        pltpu.SemaphoreType.DMA((DISPATCH_SLOTS,)),
                        pltpu.SemaphoreType.DMA((DISPATCH_SLOTS,)),
                        pltpu.SemaphoreType.DMA(())],
    )
    return pl.pallas_call(
        _dispatch_kernel,
        out_shape=jax.ShapeDtypeStruct((p_rows * SUBLANES, LANES), BF16),
        grid_spec=grid_spec,
        compiler_params=pltpu.CompilerParams(dimension_semantics=("arbitrary",)),
        name="dispatch",
    )(dest0, dest1, pad_start, pad_n, n_used, h2t)


EXPERT_IN_SLOTS = 4
EXPERT_OUT_SLOTS = 3


def _expert_kernel(first_ref, ntile_ref, nused_ref, xs_hbm, wg_ref, wu_ref, wd_ref, y_hbm,
                   xbuf, ybuf, scr, wgb, wub, wdb, isem, osem):
    tm = EXPERT_TILE
    rows = tm * SUBLANES
    ni, no = EXPERT_IN_SLOTS, EXPERT_OUT_SLOTS
    e = pl.program_id(0)
    n_used = nused_ref[0]
    n_tiles = y_hbm.shape[0] // rows

    def tile_rows(ref, g):
        return ref.at[pl.ds(pl.multiple_of(g * rows, rows), rows), :]

    def in_copy(g):
        return pltpu.make_async_copy(tile_rows(xs_hbm, g), xbuf.at[g % ni], isem.at[g % ni])

    def out_copy(g):
        return pltpu.make_async_copy(ybuf.at[g % no], tile_rows(y_hbm, g), osem.at[g % no])

    @pl.when(e == 0)
    def _():
        for g0 in range(ni - 1):
            @pl.when(g0 < n_used)
            def _():
                in_copy(g0).start()

    @pl.when(ntile_ref[e] > 0)
    def _():
        wgb[...] = wg_ref[0].astype(BF16)
        wub[...] = wu_ref[0].astype(BF16)
        wdb[...] = wd_ref[0].astype(BF16)

    def tile(g, carry):
        in_copy(g).wait()

        @pl.when(g + ni - 1 < n_used)
        def _():
            in_copy(g + ni - 1).start()

        xb = _unpack_token_rows(xbuf[g % ni], scr).astype(BF16)
        a = jnp.dot(xb, wgb[...], preferred_element_type=F32)
        u = jnp.dot(xb, wub[...], preferred_element_type=F32)
        hid = (_silu(a) * u).astype(BF16)
        y = jnp.dot(hid, wdb[...], preferred_element_type=F32)

        @pl.when(g >= no)
        def _():
            out_copy(g - no).wait()

        ybuf[g % no] = _pack_token_rows(y, scr)
        out_copy(g).start()
        return carry

    first = first_ref[e]
    lax.fori_loop(first, first + ntile_ref[e], tile, 0)

    @pl.when(e == pl.num_programs(0) - 1)
    def _():
        for back in range(no, 0, -1):
            @pl.when(n_used >= back)
            def _():
                out_copy(n_used - back).wait()

        ybuf[0] = jnp.zeros((rows, LANES), BF16)

        def zero_copy(g):
            return pltpu.make_async_copy(ybuf.at[0], tile_rows(y_hbm, g), osem.at[0])

        def fill(g, carry):
            zero_copy(g).start()
            return carry

        def drain(g, carry):
            zero_copy(g).wait()
            return carry

        lax.fori_loop(n_used, n_tiles, fill, 0)
        lax.fori_loop(n_used, n_tiles, drain, 0)


def _experts(first_tile, n_tile, n_used, xs, wg, wu, wd):
    tm = EXPERT_TILE
    n_exp, d, de = wg.shape
    grid_spec = pltpu.PrefetchScalarGridSpec(
        num_scalar_prefetch=3,
        grid=(n_exp,),
        in_specs=[
            pl.BlockSpec(memory_space=pl.ANY),
            pl.BlockSpec((1, d, de), lambda e, *_: (e, 0, 0)),
            pl.BlockSpec((1, d, de), lambda e, *_: (e, 0, 0)),
            pl.BlockSpec((1, de, d), lambda e, *_: (e, 0, 0)),
        ],
        out_specs=pl.BlockSpec(memory_space=pl.ANY),
        scratch_shapes=[pltpu.VMEM((EXPERT_IN_SLOTS, tm * SUBLANES, LANES), BF16),
                        pltpu.VMEM((EXPERT_OUT_SLOTS, tm * SUBLANES, LANES), BF16),
                        pltpu.VMEM((tm * SUBLANES, LANES), F32),
                        pltpu.VMEM((d, de), BF16),
                        pltpu.VMEM((d, de), BF16),
                        pltpu.VMEM((de, d), BF16),
                        pltpu.SemaphoreType.DMA((EXPERT_IN_SLOTS,)),
                        pltpu.SemaphoreType.DMA((EXPERT_OUT_SLOTS,))],
    )
    return pl.pallas_call(
        _expert_kernel,
        out_shape=jax.ShapeDtypeStruct(xs.shape, BF16),
        grid_spec=grid_spec,
        compiler_params=pltpu.CompilerParams(
            dimension_semantics=("arbitrary",),
            vmem_limit_bytes=VMEM_LIMIT_BYTES),
        name="experts",
    )(first_tile, n_tile, n_used, xs, wg, wu, wd)


COMBINE_SLOTS = 3


def _combine_kernel(p0_ref, p1_ref, y_hbm, x1_ref, slab_ref, mod_ref, gfin_ref, o_ref,
                    ybuf, scr, sem):
    tm = COMBINE_TILE
    ahead = COMBINE_SLOTS - 1
    i = pl.program_id(0)
    nt = pl.num_programs(0)

    def issue(tile):
        base = tile * tm
        slot = tile % COMBINE_SLOTS

        def body(r, carry):
            for j, p_ref in enumerate((p0_ref, p1_ref)):
                pltpu.make_async_copy(_token_rows(y_hbm, p_ref[base + r]),
                                      _token_rows(ybuf.at[slot, j], r),
                                      sem.at[slot]).start(priority=j)
            return carry

        lax.fori_loop(0, tm, body, 0, unroll=ROW_DMA_UNROLL)

    @pl.when(i == 0)
    def _():
        for t0 in range(ahead):
            @pl.when(t0 < nt)
            def _():
                issue(t0)

    @pl.when(i + ahead < nt)
    def _():
        issue(i + ahead)

    slot = i % COMBINE_SLOTS
    for j in range(2):
        pltpu.make_async_copy(y_hbm.at[pl.ds(0, tm * SUBLANES), :], ybuf.at[slot, j],
                              sem.at[slot]).wait()

    slab = slab_ref[...]
    w0 = slab[:, 2:3]
    w1 = slab[:, 3:4]
    gate_f = mod_ref[0][5:6]
    moe = w0 * _unpack_token_rows(ybuf[slot, 0], scr)
    moe = moe + w1 * _unpack_token_rows(ybuf[slot, 1], scr)
    xo = x1_ref[...] + gate_f * moe
    ms = jnp.mean(xo * xo, axis=-1, keepdims=True)
    o_ref[...] = xo * lax.rsqrt(ms + EPS) * gfin_ref[...]


def _combine(p0, p1, y, x1, slab, mod, gfin, seq):
    n, d = x1.shape
    tm = COMBINE_TILE
    tiles_per_seq = seq // tm
    grid_spec = pltpu.PrefetchScalarGridSpec(
        num_scalar_prefetch=2,
        grid=(n // tm,),
        in_specs=[
            pl.BlockSpec(memory_space=pl.ANY),
            pl.BlockSpec((tm, d), lambda i, a, b: (i, 0)),
            pl.BlockSpec((tm, ROUTER_LANES), lambda i, a, b: (i, 0)),
            pl.BlockSpec((1, 6, d), lambda i, a, b: (i // tiles_per_seq, 0, 0)),
            pl.BlockSpec((1, d), lambda i, a, b: (0, 0)),
        ],
        out_specs=pl.BlockSpec((tm, d), lambda i, a, b: (i, 0)),
        scratch_shapes=[pltpu.VMEM((COMBINE_SLOTS, 2, tm * SUBLANES, LANES), BF16),
                        pltpu.VMEM((tm * SUBLANES, LANES), F32),
                        pltpu.SemaphoreType.DMA((COMBINE_SLOTS,))],
    )
    return pl.pallas_call(
        _combine_kernel,
        out_shape=jax.ShapeDtypeStruct((n, d), F32),
        grid_spec=grid_spec,
        compiler_params=pltpu.CompilerParams(
            dimension_semantics=("arbitrary",),
            vmem_limit_bytes=VMEM_LIMIT_BYTES),
        name="combine",
    )(p0, p1, y, x1, slab, mod, gfin)


def _routing_plan(slab, counts_f, n):
    tm = EXPERT_TILE
    counts = counts_f[0, :N_EXPERTS].astype(jnp.int32)
    padded = ((counts + tm - 1) // tm) * tm
    ends = jnp.cumsum(padded)
    starts = ends - padded
    eid = slab[:, 0:2].astype(jnp.int32)
    rank = slab[:, 4:6].astype(jnp.int32)
    onehot = eid[:, :, None] == jnp.arange(N_EXPERTS, dtype=jnp.int32)[None, None, :]
    dest = jnp.sum(jnp.where(onehot, starts[None, None, :], 0), axis=-1) + rank
    p_rows = 2 * n + N_EXPERTS * tm
    n_used = (ends[-1] // tm).astype(jnp.int32)
    return (starts // tm, padded // tm, n_used.reshape(1), dest[:, 0], dest[:, 1],
            starts + counts, padded - counts, p_rows)


def kernel(x, c, positions, ada_w, ada_b, norm_mix_g, norm_ffn_g, w_in, conv_w, conv_b,
           beta_ret, beta_conv, w_out, router_group_w, router_group_b, router_expert_w,
           router_expert_b, expert_w_gate, expert_w_up, expert_w_down, norm_final_g):
    bsz, seq, d = x.shape
    n = bsz * seq
    depth = ada_w.shape[0]
    assert depth == 1, "the combine kernel fuses the trunk's final RMSNorm (single layer)"
    W = RET_HEADS * HEAD_DIM

    cos128, sin128 = _rope_tables(positions)
    heads = jnp.arange(RET_HEADS, dtype=F32)
    lg = jnp.log1p(-jnp.exp2(-5.0 - heads))
    lgl = jnp.repeat(lg, HEAD_DIM).reshape(1, W)
    assert CONV_GROUP_DIM == HEAD_DIM, "conv groups and retention heads share the 64-lane block sums"
    blk_np = np.kron(np.eye(RET_HEADS // 2, dtype=np.float32),
                     np.ones((HEAD_DIM, HEAD_DIM), np.float32))
    blk = jnp.asarray(blk_np, dtype=BF16)

    for l in range(depth):
        mod = _adaln(c, ada_w[l], ada_b[l]).reshape(bsz, 6, d)

        n_route = N_GROUPS + N_EXPERTS
        wr = jnp.concatenate([router_group_w[l].T, router_expert_w[l].T,
                              jnp.zeros((ROUTER_LANES - n_route, d), F32)], axis=0)
        wr_hi = wr.astype(BF16)
        wr_lo = (wr - wr_hi.astype(F32)).astype(BF16)
        wr2 = jnp.concatenate([wr_hi, wr_lo], axis=0)
        br = jnp.concatenate([router_group_b[l], router_expert_b[l]])
        br = jnp.pad(br, (0, ROUTER_LANES - br.shape[0])).reshape(1, ROUTER_LANES)

        x1, h2p, logits = _mixer(
            x, mod, cos128, sin128, norm_mix_g[l].reshape(1, d), w_in[l].astype(BF16),
            conv_w[l], conv_b[l].reshape(1, W), beta_ret[l].reshape(1, W),
            beta_conv[l].reshape(1, W), w_out[l].astype(BF16), norm_ffn_g[l].reshape(1, d),
            wr2, br, lg, lgl, blk)
        slab, counts = _router(logits)

        first_tile, n_tile, n_used, p0, p1, pad_start, pad_n, p_rows = _routing_plan(
            slab, counts, n)
        xs = _dispatch(p0, p1, pad_start, pad_n, n_used, h2p, p_rows)
        de = expert_w_gate.shape[-1]
        y = _experts(first_tile, n_tile, n_used, xs,
                     expert_w_gate[l].reshape(N_EXPERTS, d, de),
                     expert_w_up[l].reshape(N_EXPERTS, d, de),
                     expert_w_down[l].reshape(N_EXPERTS, de, d))
        out = _combine(p0, p1, y, x1.reshape(n, d), slab, mod, norm_final_g.reshape(1, d), seq)
        x = out.reshape(bsz, seq, d)
    return x
```

```python
import functools

import jax
import jax.numpy as jnp
import numpy as np
from jax import lax
from jax.experimental import pallas as pl
from jax.experimental.pallas import tpu as pltpu

F32 = jnp.float32
BF16 = jnp.bfloat16

CHUNK = 64
RET_HEADS = 8
HEAD_DIM = 64
CONV_GROUP_DIM = 64
ROPE_BASE = 10000.0
N_GROUPS = 4
EXPERTS_PER_GROUP = 8
N_EXPERTS = N_GROUPS * EXPERTS_PER_GROUP
EPS = 1e-6
GN_EPS = 1e-5

LANES = 128
SUBLANES = 8
VMEM_LIMIT_BYTES = 56 * 1024 * 1024

SEQ_TILE = 256
EXPERT_TILE = 256
COMBINE_TILE = 256
ROUTER_LANES = LANES


def _silu(v):
    return v * (1.0 / (1.0 + jnp.exp(-v)))


def _adaln_kernel(c_ref, w_ref, b_ref, o_ref):
    s = _silu(c_ref[...])
    o_ref[...] = jnp.dot(s, w_ref[...], precision=lax.Precision.HIGHEST,
                         preferred_element_type=F32) + b_ref[...]


def _adaln(c, w, b):
    bsz, d = c.shape
    n = w.shape[1]
    tn = 1024
    return pl.pallas_call(
        _adaln_kernel,
        out_shape=jax.ShapeDtypeStruct((bsz, n), F32),
        grid=(n // tn,),
        in_specs=[pl.BlockSpec((bsz, d), lambda j: (0, 0)),
                  pl.BlockSpec((d, tn), lambda j: (0, j)),
                  pl.BlockSpec((1, tn), lambda j: (0, j))],
        out_specs=pl.BlockSpec((bsz, tn), lambda j: (0, j)),
        name="adaln",
    )(c, w, b.reshape(1, n))


ROPE_ROWS = 1024


def _rope_kernel(pos_ref, invf_ref, cos_ref, sin_ref):
    half = HEAD_DIM // 2
    per_row = LANES // half
    r = ROPE_ROWS
    ang = pos_ref[...].astype(F32) * invf_ref[...]
    lane = lax.broadcasted_iota(jnp.int32, (r, LANES), 1)
    quarter = lane // half
    sign = jnp.where(quarter % 2 == 0, -1.0, 1.0)
    for table, out_ref, scale in ((jnp.cos(ang), cos_ref, None), (jnp.sin(ang), sin_ref, sign)):
        rolled = [table] + [pltpu.roll(table, half * k, 1) for k in range(1, per_row)]
        for q in range(per_row):
            val = rolled[(0 - q) % per_row]
            for k in range(1, per_row):
                val = jnp.where(quarter == k, rolled[(k - q) % per_row], val)
            if scale is not None:
                val = val * scale
            out_ref[pl.ds(q, r, stride=per_row), :] = val


def _rope_tables(positions):
    n = positions.size
    half = HEAD_DIM // 2
    inv_freq = ROPE_BASE ** (-jnp.arange(0, HEAD_DIM, 2, dtype=F32) / HEAD_DIM)
    per_row = LANES // half
    rows = n // per_row
    pos_rep = jnp.broadcast_to(positions.reshape(n, 1), (n, half)).reshape(rows, LANES)
    invf = jnp.tile(inv_freq, per_row).reshape(1, LANES)
    tr = ROPE_ROWS
    return pl.pallas_call(
        _rope_kernel,
        out_shape=(jax.ShapeDtypeStruct((n, LANES), F32),) * 2,
        grid=(rows // tr,),
        in_specs=[pl.BlockSpec((tr, LANES), lambda i: (i, 0)),
                  pl.BlockSpec((1, LANES), lambda i: (0, 0))],
        out_specs=(pl.BlockSpec((tr * per_row, LANES), lambda i: (i, 0)),) * 2,
        name="rope_table",
    )(pos_rep, invf)


def _load_token_rows(ref, rows):
    return jnp.concatenate(
        [ref[pl.ds(c, rows, stride=SUBLANES), :] for c in range(SUBLANES)], axis=1)


def _store_token_rows(ref, val, row0=0):
    rows = val.shape[0]
    for c in range(SUBLANES):
        ref[pl.ds(row0 * SUBLANES + c, rows, stride=SUBLANES), :] = val[:, c * LANES:(c + 1) * LANES]


def _token_rows(ref, row):
    return ref.at[pl.ds(pl.multiple_of(row * SUBLANES, SUBLANES), SUBLANES), :]


MIXER_SUBTILES = 2


def _mixer_kernel(*refs):
    state_ref, ubuf_ref = refs[-2:]

    @pl.when(pl.program_id(1) == 0)
    def _():
        state_ref[...] = jnp.zeros_like(state_ref)
        ubuf_ref[0:SUBLANES, :] = jnp.zeros((SUBLANES, ubuf_ref.shape[1]), F32)

    for sub in range(MIXER_SUBTILES):
        _mixer_tile(sub, *refs)


def _mixer_tile(sub, lg_ref, x_ref, mod_ref, cos_ref, sin_ref, gmix_ref, win_ref, convw_ref,
                convb_ref, bret_ref, bconv_ref, wout_ref, gffn_ref, wr_ref, br_ref, lgl_ref,
                blk_ref, x1_ref, h2_ref, logit_ref, state_ref, ubuf_ref):
    L = SEQ_TILE
    W = RET_HEADS * HEAD_DIM
    tile_rows = slice(sub * L, (sub + 1) * L)

    x = x_ref[0, tile_rows, :]
    mod = mod_ref[0]
    shift_m, scale_m, gate_m = mod[0:1], mod[1:2], mod[2:3]
    shift_f, scale_f = mod[3:4], mod[4:5]

    ms = jnp.mean(x * x, axis=-1, keepdims=True)
    h = x * lax.rsqrt(ms + EPS) * gmix_ref[...]
    h = h * (1.0 + scale_m) + shift_m
    hb = h.astype(BF16)

    def proj(i):
        return jnp.dot(hb, win_ref[:, i * W:(i + 1) * W], preferred_element_type=F32)

    cos = jnp.concatenate([cos_ref[tile_rows, :]] * 4, axis=1)
    sin = jnp.concatenate([sin_ref[tile_rows, :]] * 4, axis=1)
    lane_w = lax.broadcasted_iota(jnp.int32, (L, W), 1)
    first_half = (lane_w & (HEAD_DIM - 1)) < (HEAD_DIM // 2)

    def rot(t):
        partner = jnp.where(first_half, pltpu.roll(t, W - HEAD_DIM // 2, 1),
                            pltpu.roll(t, HEAD_DIM // 2, 1))
        return t * cos + partner * sin

    q = rot(proj(0))
    k = rot(proj(1)) * (HEAD_DIM ** -0.5)
    v = proj(2)
    vb = v.astype(BF16)
    kb = k.astype(BF16)

    lgl = lgl_ref[...]
    rowf = lax.broadcasted_iota(jnp.int32, (L, W), 0).astype(F32)
    qd = q * jnp.exp(lgl * (rowf + 1.0))
    kd = k * jnp.exp(lgl * (float(L - 1) - rowf))
    blk = blk_ref[...]
    HW = W // 2
    blk_f = blk.astype(F32)
    qdb = qd.astype(BF16)
    kdb = kd.astype(BF16)
    state_decay = jnp.exp(lgl * float(L))
    inter = []
    for hf in range(2):
        sl = slice(hf * HW, (hf + 1) * HW)
        st = state_ref[hf]
        inter.append(jnp.dot(qdb[:, sl], st.astype(BF16), preferred_element_type=F32))
        kv = lax.dot_general(kdb[:, sl], vb[:, sl], (((0,), (0,)), ((), ())),
                             preferred_element_type=F32)
        state_ref[hf] = st * state_decay[:, sl] + kv * blk_f
    y_inter = jnp.concatenate(inter, axis=1)

    def head_sums(t):
        tb = t.astype(BF16)
        return jnp.concatenate(
            [jnp.dot(tb[:, hf * HW:(hf + 1) * HW], blk, preferred_element_type=F32)
             for hf in range(2)], axis=1)

    ii = lax.broadcasted_iota(jnp.int32, (L, L), 0)
    jj = lax.broadcasted_iota(jnp.int32, (L, L), 1)
    dist = jnp.abs(ii - jj).astype(F32)
    allowed = (jj // CHUNK) <= (ii // CHUNK)
    lane_p = lax.broadcasted_iota(jnp.int32, (L, LANES), 1)
    lo_head = lane_p < HEAD_DIM
    pairs = []
    for p in range(RET_HEADS // 2):
        sl = slice(p * LANES, (p + 1) * LANES)
        qp, kp, vp = q[:, sl], kb[:, sl], vb[:, sl]
        ys = []
        for hh in range(2):
            head = 2 * p + hh
            keep = lo_head if hh == 0 else jnp.logical_not(lo_head)
            qh = jnp.where(keep, qp, 0.0).astype(BF16)
            sc = lax.dot_general(qh, kp, (((1,), (1,)), ((), ())),
                                 preferred_element_type=F32)
            decay = jnp.where(allowed, jnp.exp(lg_ref[head] * dist), 0.0)
            ys.append(jnp.dot((sc * decay).astype(BF16), vp, preferred_element_type=F32))
        pairs.append(jnp.where(lo_head, ys[0], ys[1]))
    y = jnp.concatenate(pairs, axis=1) + y_inter

    inv_hd = 1.0 / HEAD_DIM
    mu = head_sums(y) * inv_hd
    d = y - mu
    var = head_sums(d * d) * inv_hd
    g = proj(3)
    y_ret = _silu(g) * (d * lax.rsqrt(var + GN_EPS)) * bret_ref[...]

    b_gate = proj(4)
    u = proj(5) * proj(6)
    ubuf_ref[SUBLANES:SUBLANES + L, :] = u
    u1 = ubuf_ref[SUBLANES - 1:SUBLANES - 1 + L, :]
    u2 = ubuf_ref[SUBLANES - 2:SUBLANES - 2 + L, :]
    ubuf_ref[0:SUBLANES, :] = ubuf_ref[L:L + SUBLANES, :]
    cw = convw_ref[...]
    conv = u2 * cw[0:1] + u1 * cw[1:2] + u * cw[2:3] + convb_ref[...]
    yc = b_gate * conv
    msc = head_sums(yc * yc) * (1.0 / CONV_GROUP_DIM)
    y_conv = yc * lax.rsqrt(msc + EPS) * bconv_ref[...]

    mix = (jnp.dot(y_ret.astype(BF16), wout_ref[0:W, :], preferred_element_type=F32)
           + jnp.dot(y_conv.astype(BF16), wout_ref[W:2 * W, :], preferred_element_type=F32))
    x1 = x + gate_m * mix
    x1_ref[0, tile_rows, :] = x1

    ms2 = jnp.mean(x1 * x1, axis=-1, keepdims=True)
    h2 = x1 * lax.rsqrt(ms2 + EPS) * gffn_ref[...]
    h2 = h2 * (1.0 + scale_f) + shift_f
    _store_token_rows(h2_ref, h2, row0=sub * L)

    hi = h2.astype(BF16)
    lo = (h2 - hi.astype(F32)).astype(BF16)
    w2 = wr_ref[...]
    nt_dims = (((1,), (1,)), ((), ()))
    parts = (lax.dot_general(hi, w2, nt_dims, preferred_element_type=F32)
             + lax.dot_general(lo, w2, nt_dims, preferred_element_type=F32))
    logit_ref[tile_rows, :] = parts[:, :ROUTER_LANES] + parts[:, ROUTER_LANES:] + br_ref[...]


def _mixer(x, mod, cos128, sin128, gmix, win_b, convw, convb, bret, bconv, wout_b, gffn,
           wr2, br, lg, lgl, blk):
    bsz, seq, d = x.shape
    L = SEQ_TILE * MIXER_SUBTILES
    ns = seq // L
    W = RET_HEADS * HEAD_DIM
    n = bsz * seq
    const2 = lambda b, s: (0, 0)
    in_specs = [
        pl.BlockSpec(memory_space=pltpu.SMEM),
        pl.BlockSpec((1, L, d), lambda b, s: (b, s, 0)),
        pl.BlockSpec((1, 6, d), lambda b, s: (b, 0, 0)),
        pl.BlockSpec((L, LANES), lambda b, s: (b * ns + s, 0)),
        pl.BlockSpec((L, LANES), lambda b, s: (b * ns + s, 0)),
        pl.BlockSpec((1, d), const2),
        pl.BlockSpec(win_b.shape, const2),
        pl.BlockSpec(convw.shape, const2),
        pl.BlockSpec((1, W), const2),
        pl.BlockSpec((1, W), const2),
        pl.BlockSpec((1, W), const2),
        pl.BlockSpec(wout_b.shape, const2),
        pl.BlockSpec((1, d), const2),
        pl.BlockSpec(wr2.shape, const2),
        pl.BlockSpec((1, ROUTER_LANES), const2),
        pl.BlockSpec((1, W), const2),
        pl.BlockSpec((W // 2, W // 2), const2),
    ]
    assert d == SUBLANES * LANES, "one token must fill exactly one (8, 128) f32 tile"
    out_shape = (jax.ShapeDtypeStruct((bsz, seq, d), F32),
                 jax.ShapeDtypeStruct((n * SUBLANES, LANES), F32),
                 jax.ShapeDtypeStruct((n, ROUTER_LANES), F32))
    out_specs = (pl.BlockSpec((1, L, d), lambda b, s: (b, s, 0)),
                 pl.BlockSpec((L * SUBLANES, LANES), lambda b, s: (b * ns + s, 0)),
                 pl.BlockSpec((L, ROUTER_LANES), lambda b, s: (b * ns + s, 0)))
    return pl.pallas_call(
        _mixer_kernel,
        out_shape=out_shape,
        grid=(bsz, ns),
        in_specs=in_specs,
        out_specs=out_specs,
        scratch_shapes=[pltpu.VMEM((2, W // 2, W // 2), F32),
                        pltpu.VMEM((SEQ_TILE + 2 * SUBLANES, W), F32)],
        compiler_params=pltpu.CompilerParams(
            dimension_semantics=("arbitrary", "arbitrary"),
            vmem_limit_bytes=VMEM_LIMIT_BYTES),
        name="mixer",
    )(lg, x, mod, cos128, sin128, gmix, win_b, convw, convb, bret, bconv, wout_b, gffn,
      wr2, br, lgl, blk)


ROUTER_TILE = 2048
RANK_BLOCK = 256


def _router_kernel(logit_ref, slab_ref, cnt_out_ref, cnt_ref):
    T = ROUTER_TILE

    @pl.when(pl.program_id(0) == 0)
    def _():
        cnt_ref[...] = jnp.zeros_like(cnt_ref)

    logits = logit_ref[...]
    lane = lax.broadcasted_iota(jnp.int32, (T, ROUTER_LANES), 1).astype(F32)
    neg_inf = F32(-jnp.inf)
    big = F32(1e9)
    gmask = lane < float(N_GROUPS)
    lgm = jnp.where(gmask, logits, neg_inf)
    gexp = jnp.exp(lgm - jnp.max(lgm, axis=-1, keepdims=True))
    gp = gexp / jnp.sum(gexp, axis=-1, keepdims=True)
    g_top = jnp.max(gp, axis=-1, keepdims=True)
    g_idx = jnp.min(jnp.where(gmask & (gp == g_top), lane, big), axis=-1, keepdims=True)

    e_lo = float(N_GROUPS) + float(EXPERTS_PER_GROUP) * g_idx
    emask = (lane >= e_lo) & (lane < e_lo + float(EXPERTS_PER_GROUP))
    lem = jnp.where(emask, logits, neg_inf)
    eexp = jnp.exp(lem - jnp.max(lem, axis=-1, keepdims=True))
    ep = eexp / jnp.sum(eexp, axis=-1, keepdims=True)
    p1 = jnp.max(jnp.where(emask, ep, -1.0), axis=-1, keepdims=True)
    i1 = jnp.min(jnp.where(emask & (ep == p1), lane, big), axis=-1, keepdims=True)
    m2 = emask & (lane != i1)
    p2 = jnp.max(jnp.where(m2, ep, -1.0), axis=-1, keepdims=True)
    i2 = jnp.min(jnp.where(m2 & (ep == p2), lane, big), axis=-1, keepdims=True)
    den = p1 + p2
    w1 = p1 / den * g_top
    w2 = p2 / den * g_top
    e1 = i1 - float(N_GROUPS)
    e2 = i2 - float(N_GROUPS)

    oh1 = jnp.where(lane == e1, 1.0, 0.0)
    oh2 = jnp.where(lane == e2, 1.0, 0.0)
    rb = RANK_BLOCK
    ii = lax.broadcasted_iota(jnp.int32, (rb, rb), 0)
    jj = lax.broadcasted_iota(jnp.int32, (rb, rb), 1)
    ltri = jnp.where(jj < ii, 1.0, 0.0).astype(BF16)
    base = cnt_ref[...]
    ranks1, ranks2 = [], []
    for k in range(T // rb):
        o1 = oh1[k * rb:(k + 1) * rb]
        o2 = oh2[k * rb:(k + 1) * rb]
        r1 = jnp.dot(ltri, o1.astype(BF16), preferred_element_type=F32)
        r2 = jnp.dot(ltri, o2.astype(BF16), preferred_element_type=F32)
        c1 = jnp.sum(o1, axis=0, keepdims=True)
        c2 = jnp.sum(o2, axis=0, keepdims=True)
        ranks1.append(jnp.sum(o1 * (base + r1), axis=-1, keepdims=True))
        ranks2.append(jnp.sum(o2 * (base + c1 + r2), axis=-1, keepdims=True))
        base = base + c1 + c2
    rank1 = jnp.concatenate(ranks1, axis=0)
    rank2 = jnp.concatenate(ranks2, axis=0)
    cnt_ref[...] = base
    cnt_out_ref[...] = base

    slab = jnp.where(lane == 0.0, e1,
                     jnp.where(lane == 1.0, e2,
                               jnp.where(lane == 2.0, w1,
                                         jnp.where(lane == 3.0, w2,
                                                   jnp.where(lane == 4.0, rank1,
                                                             jnp.where(lane == 5.0, rank2, 0.0))))))
    slab_ref[...] = slab


def _router(logits):
    n = logits.shape[0]
    T = ROUTER_TILE
    return pl.pallas_call(
        _router_kernel,
        out_shape=(jax.ShapeDtypeStruct((n, ROUTER_LANES), F32),
                   jax.ShapeDtypeStruct((1, ROUTER_LANES), F32)),
        grid=(n // T,),
        in_specs=[pl.BlockSpec((T, ROUTER_LANES), lambda i: (i, 0))],
        out_specs=(pl.BlockSpec((T, ROUTER_LANES), lambda i: (i, 0)),
                   pl.BlockSpec((1, ROUTER_LANES), lambda i: (0, 0))),
        scratch_shapes=[pltpu.VMEM((1, ROUTER_LANES), F32)],
        compiler_params=pltpu.CompilerParams(
            dimension_semantics=("arbitrary",), vmem_limit_bytes=VMEM_LIMIT_BYTES),
        name="router",
    )(logits)


DISPATCH_TILE = 512
ROW_DMA_UNROLL = 8
PAD_UNITS = tuple(1 << b for b in reversed(range(EXPERT_TILE.bit_length() - 1)))


DISPATCH_SLOTS = 3


def _dispatch_kernel(d0_ref, d1_ref, ps_ref, pn_ref, nused_ref, h2_hbm, xs_hbm,
                     stage, zbuf, isem, ssem, zsem):
    dt = DISPATCH_TILE
    ps = SUBLANES
    i = pl.program_id(0)
    nsteps = pl.num_programs(0)
    zrows = PAD_UNITS[0]

    def in_copy(blk, slot):
        src = h2_hbm.at[pl.ds(pl.multiple_of(blk * (dt * ps), dt * ps), dt * ps), :]
        return pltpu.make_async_copy(src, stage.at[slot], isem.at[slot])

    def wait_rows(slot):
        for _ in range(2):
            pltpu.make_async_copy(stage.at[slot], xs_hbm.at[pl.ds(0, dt * ps), :],
                                  ssem.at[slot]).wait()

    def pad_copy(start, unit):
        return pltpu.make_async_copy(zbuf.at[pl.ds(0, unit * ps), :],
                                     xs_hbm.at[pl.ds(pl.multiple_of(start * ps, ps), unit * ps), :],
                                     zsem)

    def pad_pass(do):
        def per_expert(e, carry):
            start = ps_ref[e]
            npad = pn_ref[e]
            for unit in PAD_UNITS:
                @pl.when((npad & unit) != 0)
                def _():
                    do(pad_copy(start + (npad & ~(2 * unit - 1)), unit))
            return carry
        lax.fori_loop(0, N_EXPERTS, per_expert, 0)

    def tail_pass(do):
        def per_unit(k, carry):
            do(pad_copy(k * zrows, zrows))
            return carry
        per_tile = EXPERT_TILE // zrows
        n_units = xs_hbm.shape[0] // (zrows * ps)
        lax.fori_loop(nused_ref[0] * per_tile, n_units, per_unit, 0)

    slot = i % DISPATCH_SLOTS

    @pl.when(i == 0)
    def _():
        in_copy(0, 0).start()

        @pl.when(nsteps > 1)
        def _():
            in_copy(1, 1).start()

        zbuf[...] = jnp.zeros_like(zbuf)
        pad_pass(lambda cp: cp.start())
        tail_pass(lambda cp: cp.start())

    @pl.when(i >= 1)
    def _():
        wait_rows((i - 1) % DISPATCH_SLOTS)

    @pl.when(i + 2 < nsteps)
    def _():
        in_copy(i + 2, (i + 2) % DISPATCH_SLOTS).start()

    in_copy(i, slot).wait()
    base = i * dt
    src_ref = stage.at[slot]

    def body(r, carry):
        src = _token_rows(src_ref, r)
        for prio, d_ref in enumerate((d0_ref, d1_ref)):
            pltpu.make_async_copy(src, _token_rows(xs_hbm, d_ref[base + r]),
                                  ssem.at[slot]).start(priority=prio)
        return carry

    lax.fori_loop(0, dt, body, 0, unroll=ROW_DMA_UNROLL)

    @pl.when(i == nsteps - 1)
    def _():
        wait_rows(slot)
        pad_pass(lambda cp: cp.wait())
        tail_pass(lambda cp: cp.wait())


def _dispatch(dest0, dest1, pad_start, pad_n, n_used, h2t, p_rows):
    n = dest0.shape[0]
    dt = DISPATCH_TILE
    grid_spec = pltpu.PrefetchScalarGridSpec(
        num_scalar_prefetch=5,
        grid=(n // dt,),
        in_specs=[pl.BlockSpec(memory_space=pl.ANY)],
        out_specs=pl.BlockSpec(memory_space=pl.ANY),
        scratch_shapes=[pltpu.VMEM((DISPATCH_SLOTS, dt * SUBLANES, LANES), F32),
                        pltpu.VMEM((PAD_UNITS[0] * SUBLANES, LANES), F32),
                        pltpu.SemaphoreType.DMA((DISPATCH_SLOTS,)),
                        pltpu.SemaphoreType.DMA((DISPATCH_SLOTS,)),
                        pltpu.SemaphoreType.DMA(())],
    )
    return pl.pallas_call(
        _dispatch_kernel,
        out_shape=jax.ShapeDtypeStruct((p_rows * SUBLANES, LANES), F32),
        grid_spec=grid_spec,
        compiler_params=pltpu.CompilerParams(dimension_semantics=("arbitrary",)),
        name="dispatch",
    )(dest0, dest1, pad_start, pad_n, n_used, h2t)


EXPERT_IN_SLOTS = 4
EXPERT_OUT_SLOTS = 3


def _expert_kernel(first_ref, ntile_ref, nused_ref, xs_hbm, wg_ref, wu_ref, wd_ref, y_hbm,
                   xbuf, ybuf, wgb, wub, wdb, isem, osem):
    tm = EXPERT_TILE
    rows = tm * SUBLANES
    ni, no = EXPERT_IN_SLOTS, EXPERT_OUT_SLOTS
    e = pl.program_id(0)
    n_used = nused_ref[0]
    n_tiles = y_hbm.shape[0] // rows

    def tile_rows(ref, g):
        return ref.at[pl.ds(pl.multiple_of(g * rows, rows), rows), :]

    def in_copy(g):
        return pltpu.make_async_copy(tile_rows(xs_hbm, g), xbuf.at[g % ni], isem.at[g % ni])

    def out_copy(g):
        return pltpu.make_async_copy(ybuf.at[g % no], tile_rows(y_hbm, g), osem.at[g % no])

    @pl.when(e == 0)
    def _():
        for g0 in range(ni - 1):
            @pl.when(g0 < n_used)
            def _():
                in_copy(g0).start()

    @pl.when(ntile_ref[e] > 0)
    def _():
        wgb[...] = wg_ref[0].astype(BF16)
        wub[...] = wu_ref[0].astype(BF16)
        wdb[...] = wd_ref[0].astype(BF16)

    def tile(g, carry):
        in_copy(g).wait()

        @pl.when(g + ni - 1 < n_used)
        def _():
            in_copy(g + ni - 1).start()

        xb = _load_token_rows(xbuf.at[g % ni], tm).astype(BF16)
        a = jnp.dot(xb, wgb[...], preferred_element_type=F32)
        u = jnp.dot(xb, wub[...], preferred_element_type=F32)
        hid = (_silu(a) * u).astype(BF16)
        y = jnp.dot(hid, wdb[...], preferred_element_type=F32)

        @pl.when(g >= no)
        def _():
            out_copy(g - no).wait()

        _store_token_rows(ybuf.at[g % no], y)
        out_copy(g).start()
        return carry

    first = first_ref[e]
    lax.fori_loop(first, first + ntile_ref[e], tile, 0)

    @pl.when(e == pl.num_programs(0) - 1)
    def _():
        for back in range(no, 0, -1):
            @pl.when(n_used >= back)
            def _():
                out_copy(n_used - back).wait()

        ybuf[0] = jnp.zeros((rows, LANES), F32)

        def zero_copy(g):
            return pltpu.make_async_copy(ybuf.at[0], tile_rows(y_hbm, g), osem.at[0])

        def fill(g, carry):
            zero_copy(g).start()
            return carry

        def drain(g, carry):
            zero_copy(g).wait()
            return carry

        lax.fori_loop(n_used, n_tiles, fill, 0)
        lax.fori_loop(n_used, n_tiles, drain, 0)


def _experts(first_tile, n_tile, n_used, xs, wg, wu, wd):
    tm = EXPERT_TILE
    n_exp, d, de = wg.shape
    grid_spec = pltpu.PrefetchScalarGridSpec(
        num_scalar_prefetch=3,
        grid=(n_exp,),
        in_specs=[
            pl.BlockSpec(memory_space=pl.ANY),
            pl.BlockSpec((1, d, de), lambda e, *_: (e, 0, 0)),
            pl.BlockSpec((1, d, de), lambda e, *_: (e, 0, 0)),
            pl.BlockSpec((1, de, d), lambda e, *_: (e, 0, 0)),
        ],
        out_specs=pl.BlockSpec(memory_space=pl.ANY),
        scratch_shapes=[pltpu.VMEM((EXPERT_IN_SLOTS, tm * SUBLANES, LANES), F32),
                        pltpu.VMEM((EXPERT_OUT_SLOTS, tm * SUBLANES, LANES), F32),
                        pltpu.VMEM((d, de), BF16),
                        pltpu.VMEM((d, de), BF16),
                        pltpu.VMEM((de, d), BF16),
                        pltpu.SemaphoreType.DMA((EXPERT_IN_SLOTS,)),
                        pltpu.SemaphoreType.DMA((EXPERT_OUT_SLOTS,))],
    )
    return pl.pallas_call(
        _expert_kernel,
        out_shape=jax.ShapeDtypeStruct(xs.shape, F32),
        grid_spec=grid_spec,
        compiler_params=pltpu.CompilerParams(
            dimension_semantics=("arbitrary",),
            vmem_limit_bytes=VMEM_LIMIT_BYTES),
        name="experts",
    )(first_tile, n_tile, n_used, xs, wg, wu, wd)


COMBINE_SLOTS = 3


def _combine_kernel(p0_ref, p1_ref, y_hbm, x1_ref, slab_ref, mod_ref, gfin_ref, o_ref,
                    *scratch):
    tm = COMBINE_TILE
    ns = COMBINE_SLOTS
    ahead = ns - 1
    ybufs, sem = scratch[:ns], scratch[ns]
    i = pl.program_id(0)
    nt = pl.num_programs(0)

    def row_copies(tile, slot, r):
        for j, p_ref in enumerate((p0_ref, p1_ref)):
            pltpu.make_async_copy(_token_rows(y_hbm, p_ref[tile * tm + r]),
                                  _token_rows(ybufs[slot].at[j], r),
                                  sem.at[slot]).start(priority=j)

    def wait_tile(slot):
        for j in range(2):
            pltpu.make_async_copy(y_hbm.at[pl.ds(0, tm * SUBLANES), :], ybufs[slot].at[j],
                                  sem.at[slot]).wait()

    @pl.when(i == 0)
    def _():
        for t0 in range(ahead):
            @pl.when(t0 < nt)
            def _():
                def body(r, carry):
                    row_copies(t0, t0, r)
                    return carry
                lax.fori_loop(0, tm, body, 0, unroll=ROW_DMA_UNROLL)

    n_chunks = SUBLANES
    batch = tm // n_chunks

    def step(slot):
        wait_tile(slot)
        nxt = jnp.minimum(i + ahead, nt - 1)
        nslot = (slot + ahead) % ns
        slab = slab_ref[...]
        w0 = slab[:, 2:3]
        w1 = slab[:, 3:4]
        gate_f = mod_ref[0][5:6]
        sq = jnp.zeros((tm, LANES), F32)
        for c in range(n_chunks):
            lanes = slice(c * LANES, (c + 1) * LANES)
            y0 = ybufs[slot].at[0][pl.ds(c, tm, stride=SUBLANES), :]
            y1 = ybufs[slot].at[1][pl.ds(c, tm, stride=SUBLANES), :]
            xo = x1_ref[:, lanes] + gate_f[:, lanes] * (w0 * y0 + w1 * y1)
            sq = sq + xo * xo
            o_ref[:, lanes] = xo
            for r in range(c * batch, (c + 1) * batch):
                row_copies(nxt, nslot, r)
        ms = jnp.sum(sq, axis=-1, keepdims=True) * (1.0 / (n_chunks * LANES))
        scale = lax.rsqrt(ms + EPS)
        for c in range(n_chunks):
            lanes = slice(c * LANES, (c + 1) * LANES)
            o_ref[:, lanes] = o_ref[:, lanes] * scale * gfin_ref[:, lanes]

    for slot in range(ns):
        @pl.when(i % ns == slot)
        def _():
            step(slot)

    @pl.when(i == nt - 1)
    def _():
        for k in range(ahead):
            for slot in range(ns):
                @pl.when((nt - 1 - k >= 0) & ((nt - 1 - k + ahead) % ns == slot))
                def _():
                    wait_tile(slot)


def _combine(p0, p1, y, x1, slab, mod, gfin, seq):
    n, d = x1.shape
    tm = COMBINE_TILE
    tiles_per_seq = seq // tm
    grid_spec = pltpu.PrefetchScalarGridSpec(
        num_scalar_prefetch=2,
        grid=(n // tm,),
        in_specs=[
            pl.BlockSpec(memory_space=pl.ANY),
            pl.BlockSpec((tm, d), lambda i, a, b: (i, 0)),
            pl.BlockSpec((tm, ROUTER_LANES), lambda i, a, b: (i, 0)),
            pl.BlockSpec((1, 6, d), lambda i, a, b: (i // tiles_per_seq, 0, 0)),
            pl.BlockSpec((1, d), lambda i, a, b: (0, 0)),
        ],
        out_specs=pl.BlockSpec((tm, d), lambda i, a, b: (i, 0)),
        scratch_shapes=([pltpu.VMEM((2, tm * SUBLANES, LANES), F32)] * COMBINE_SLOTS
                        + [pltpu.SemaphoreType.DMA((COMBINE_SLOTS,))]),
    )
    return pl.pallas_call(
        _combine_kernel,
        out_shape=jax.ShapeDtypeStruct((n, d), F32),
        grid_spec=grid_spec,
        compiler_params=pltpu.CompilerParams(
            dimension_semantics=("arbitrary",),
            vmem_limit_bytes=VMEM_LIMIT_BYTES),
        name="combine",
    )(p0, p1, y, x1, slab, mod, gfin)


def _routing_plan(slab, counts_f, n):
    tm = EXPERT_TILE
    counts = counts_f[0, :N_EXPERTS].astype(jnp.int32)
    padded = ((counts + tm - 1) // tm) * tm
    ends = jnp.cumsum(padded)
    starts = ends - padded
    eid = slab[:, 0:2].astype(jnp.int32)
    rank = slab[:, 4:6].astype(jnp.int32)
    onehot = eid[:, :, None] == jnp.arange(N_EXPERTS, dtype=jnp.int32)[None, None, :]
    dest = jnp.sum(jnp.where(onehot, starts[None, None, :], 0), axis=-1) + rank
    p_rows = 2 * n + N_EXPERTS * tm
    n_used = (ends[-1] // tm).astype(jnp.int32)
    return (starts // tm, padded // tm, n_used.reshape(1), dest[:, 0], dest[:, 1],
            starts + counts, padded - counts, p_rows)


def kernel(x, c, positions, ada_w, ada_b, norm_mix_g, norm_ffn_g, w_in, conv_w, conv_b,
           beta_ret, beta_conv, w_out, router_group_w, router_group_b, router_expert_w,
           router_expert_b, expert_w_gate, expert_w_up, expert_w_down, norm_final_g):
    bsz, seq, d = x.shape
    n = bsz * seq
    depth = ada_w.shape[0]
    assert depth == 1, "the combine kernel fuses the trunk's final RMSNorm (single layer)"
    W = RET_HEADS * HEAD_DIM

    cos128, sin128 = _rope_tables(positions)
    heads = jnp.arange(RET_HEADS, dtype=F32)
    lg = jnp.log1p(-jnp.exp2(-5.0 - heads))
    lgl = jnp.repeat(lg, HEAD_DIM).reshape(1, W)
    assert CONV_GROUP_DIM == HEAD_DIM, "conv groups and retention heads share the 64-lane block sums"
    blk_np = np.kron(np.eye(RET_HEADS // 2, dtype=np.float32),
                     np.ones((HEAD_DIM, HEAD_DIM), np.float32))
    blk = jnp.asarray(blk_np, dtype=BF16)

    for l in range(depth):
        mod = _adaln(c, ada_w[l], ada_b[l]).reshape(bsz, 6, d)

        n_route = N_GROUPS + N_EXPERTS
        wr = jnp.concatenate([router_group_w[l].T, router_expert_w[l].T,
                              jnp.zeros((ROUTER_LANES - n_route, d), F32)], axis=0)
        wr_hi = wr.astype(BF16)
        wr_lo = (wr - wr_hi.astype(F32)).astype(BF16)
        wr2 = jnp.concatenate([wr_hi, wr_lo], axis=0)
        br = jnp.concatenate([router_group_b[l], router_expert_b[l]])
        br = jnp.pad(br, (0, ROUTER_LANES - br.shape[0])).reshape(1, ROUTER_LANES)

        x1, h2p, logits = _mixer(
            x, mod, cos128, sin128, norm_mix_g[l].reshape(1, d), w_in[l].astype(BF16),
            conv_w[l], conv_b[l].reshape(1, W), beta_ret[l].reshape(1, W),
            beta_conv[l].reshape(1, W), w_out[l].astype(BF16), norm_ffn_g[l].reshape(1, d),
            wr2, br, lg, lgl, blk)
        slab, counts = _router(logits)

        first_tile, n_tile, n_used, p0, p1, pad_start, pad_n, p_rows = _routing_plan(
            slab, counts, n)
        xs = _dispatch(p0, p1, pad_start, pad_n, n_used, h2p, p_rows)
        de = expert_w_gate.shape[-1]
        y = _experts(first_tile, n_tile, n_used, xs,
                     expert_w_gate[l].reshape(N_EXPERTS, d, de),
                     expert_w_up[l].reshape(N_EXPERTS, d, de),
                     expert_w_down[l].reshape(N_EXPERTS, de, d))
        out = _combine(p0, p1, y, x1.reshape(n, d), slab, mod, norm_final_g.reshape(1, d), seq)
        x = out.reshape(bsz, seq, d)
    return x
```

```python
import functools

import jax
import jax.numpy as jnp
import numpy as np
from jax import lax
from jax.experimental import pallas as pl
from jax.experimental.pallas import tpu as pltpu

F32 = jnp.float32
BF16 = jnp.bfloat16

CHUNK = 64
RET_HEADS = 8
HEAD_DIM = 64
CONV_GROUP_DIM = 64
ROPE_BASE = 10000.0
N_GROUPS = 4
EXPERTS_PER_GROUP = 8
N_EXPERTS = N_GROUPS * EXPERTS_PER_GROUP
EPS = 1e-6
GN_EPS = 1e-5

LANES = 128
SUBLANES = 8
VMEM_LIMIT_BYTES = 56 * 1024 * 1024

SEQ_TILE = 256
EXPERT_TILE = 256
COMBINE_TILE = 256
ROUTER_LANES = LANES
ROUTER_GROUP_ROW0 = 0
ROUTER_EXPERT_ROW0 = SUBLANES
ROUTER_ROWS = ROUTER_EXPERT_ROW0 + N_EXPERTS


def _silu(v):
    return v * (1.0 / (1.0 + jnp.exp(-v)))


def _adaln_kernel(c_ref, w_ref, b_ref, o_ref):
    s = _silu(c_ref[...])
    o_ref[...] = jnp.dot(s, w_ref[...], precision=lax.Precision.HIGHEST,
                         preferred_element_type=F32) + b_ref[...]


def _adaln(c, w, b):
    bsz, d = c.shape
    n = w.shape[1]
    tn = 1024
    return pl.pallas_call(
        _adaln_kernel,
        out_shape=jax.ShapeDtypeStruct((bsz, n), F32),
        grid=(n // tn,),
        in_specs=[pl.BlockSpec((bsz, d), lambda j: (0, 0)),
                  pl.BlockSpec((d, tn), lambda j: (0, j)),
                  pl.BlockSpec((1, tn), lambda j: (0, j))],
        out_specs=pl.BlockSpec((bsz, tn), lambda j: (0, j)),
        name="adaln",
    )(c, w, b.reshape(1, n))


ROPE_ROWS = 1024


def _rope_kernel(pos_ref, invf_ref, cos_ref, sin_ref):
    half = HEAD_DIM // 2
    per_row = LANES // half
    r = ROPE_ROWS
    ang = pos_ref[...].astype(F32) * invf_ref[...]
    lane = lax.broadcasted_iota(jnp.int32, (r, LANES), 1)
    quarter = lane // half
    sign = jnp.where(quarter % 2 == 0, -1.0, 1.0)
    for table, out_ref, scale in ((jnp.cos(ang), cos_ref, None), (jnp.sin(ang), sin_ref, sign)):
        rolled = [table] + [pltpu.roll(table, half * k, 1) for k in range(1, per_row)]
        for q in range(per_row):
            val = rolled[(0 - q) % per_row]
            for k in range(1, per_row):
                val = jnp.where(quarter == k, rolled[(k - q) % per_row], val)
            if scale is not None:
                val = val * scale
            out_ref[pl.ds(q, r, stride=per_row), :] = val


def _rope_tables(positions):
    n = positions.size
    half = HEAD_DIM // 2
    inv_freq = ROPE_BASE ** (-jnp.arange(0, HEAD_DIM, 2, dtype=F32) / HEAD_DIM)
    per_row = LANES // half
    rows = n // per_row
    pos_rep = jnp.broadcast_to(positions.reshape(n, 1), (n, half)).reshape(rows, LANES)
    invf = jnp.tile(inv_freq, per_row).reshape(1, LANES)
    tr = ROPE_ROWS
    return pl.pallas_call(
        _rope_kernel,
        out_shape=(jax.ShapeDtypeStruct((n, LANES), F32),) * 2,
        grid=(rows // tr,),
        in_specs=[pl.BlockSpec((tr, LANES), lambda i: (i, 0)),
                  pl.BlockSpec((1, LANES), lambda i: (0, 0))],
        out_specs=(pl.BlockSpec((tr * per_row, LANES), lambda i: (i, 0)),) * 2,
        name="rope_table",
    )(pos_rep, invf)


def _load_token_rows(ref, rows):
    return jnp.concatenate(
        [ref[pl.ds(c, rows, stride=SUBLANES), :] for c in range(SUBLANES)], axis=1)


def _store_token_rows(ref, val, row0=0):
    rows = val.shape[0]
    for c in range(SUBLANES):
        ref[pl.ds(row0 * SUBLANES + c, rows, stride=SUBLANES), :] = val[:, c * LANES:(c + 1) * LANES]


def _token_rows(ref, row):
    return ref.at[pl.ds(pl.multiple_of(row * SUBLANES, SUBLANES), SUBLANES), :]


MIXER_SUBTILES = 2


def _mixer_kernel(*refs):
    state_ref, ubuf_ref = refs[-2:]

    @pl.when(pl.program_id(1) == 0)
    def _():
        state_ref[...] = jnp.zeros_like(state_ref)
        ubuf_ref[0:SUBLANES, :] = jnp.zeros((SUBLANES, ubuf_ref.shape[1]), F32)

    for sub in range(MIXER_SUBTILES):
        _mixer_tile(sub, *refs)


def _mixer_tile(sub, lg_ref, x_ref, mod_ref, cos_ref, sin_ref, gmix_ref, win_ref, convw_ref,
                convb_ref, bret_ref, bconv_ref, wout_ref, gffn_ref, wr_ref, br_ref, lgl_ref,
                blk_ref, x1_ref, h2_ref, logit_ref, state_ref, ubuf_ref):
    L = SEQ_TILE
    W = RET_HEADS * HEAD_DIM
    tile_rows = slice(sub * L, (sub + 1) * L)

    x = x_ref[0, tile_rows, :]
    mod = mod_ref[0]
    shift_m, scale_m, gate_m = mod[0:1], mod[1:2], mod[2:3]
    shift_f, scale_f = mod[3:4], mod[4:5]

    ms = jnp.mean(x * x, axis=-1, keepdims=True)
    h = x * lax.rsqrt(ms + EPS) * gmix_ref[...]
    h = h * (1.0 + scale_m) + shift_m
    hb = h.astype(BF16)

    def proj(i):
        return jnp.dot(hb, win_ref[:, i * W:(i + 1) * W], preferred_element_type=F32)

    cos = jnp.concatenate([cos_ref[tile_rows, :]] * 4, axis=1)
    sin = jnp.concatenate([sin_ref[tile_rows, :]] * 4, axis=1)
    lane_w = lax.broadcasted_iota(jnp.int32, (L, W), 1)
    first_half = (lane_w & (HEAD_DIM - 1)) < (HEAD_DIM // 2)

    def rot(t):
        partner = jnp.where(first_half, pltpu.roll(t, W - HEAD_DIM // 2, 1),
                            pltpu.roll(t, HEAD_DIM // 2, 1))
        return t * cos + partner * sin

    q = rot(proj(0))
    k = rot(proj(1)) * (HEAD_DIM ** -0.5)
    v = proj(2)
    vb = v.astype(BF16)
    kb = k.astype(BF16)

    lgl = lgl_ref[...]
    rowf = lax.broadcasted_iota(jnp.int32, (L, W), 0).astype(F32)
    qd = q * jnp.exp(lgl * (rowf + 1.0))
    kd = k * jnp.exp(lgl * (float(L - 1) - rowf))
    blk = blk_ref[...]
    HW = W // 2
    blk_f = blk.astype(F32)
    qdb = qd.astype(BF16)
    kdb = kd.astype(BF16)
    state_decay = jnp.exp(lgl * float(L))
    inter = []
    for hf in range(2):
        sl = slice(hf * HW, (hf + 1) * HW)
        st = state_ref[hf]
        inter.append(jnp.dot(qdb[:, sl], st.astype(BF16), preferred_element_type=F32))
        kv = lax.dot_general(kdb[:, sl], vb[:, sl], (((0,), (0,)), ((), ())),
                             preferred_element_type=F32)
        state_ref[hf] = st * state_decay[:, sl] + kv * blk_f
    y_inter = jnp.concatenate(inter, axis=1)

    def head_sums(t):
        tb = t.astype(BF16)
        return jnp.concatenate(
            [jnp.dot(tb[:, hf * HW:(hf + 1) * HW], blk, preferred_element_type=F32)
             for hf in range(2)], axis=1)

    ii = lax.broadcasted_iota(jnp.int32, (L, L), 0)
    jj = lax.broadcasted_iota(jnp.int32, (L, L), 1)
    dist = jnp.abs(ii - jj).astype(F32)
    allowed = (jj // CHUNK) <= (ii // CHUNK)
    lane_p = lax.broadcasted_iota(jnp.int32, (L, LANES), 1)
    lo_head = lane_p < HEAD_DIM
    pairs = []
    for p in range(RET_HEADS // 2):
        sl = slice(p * LANES, (p + 1) * LANES)
        qp, kp, vp = q[:, sl], kb[:, sl], vb[:, sl]
        ys = []
        for hh in range(2):
            head = 2 * p + hh
            keep = lo_head if hh == 0 else jnp.logical_not(lo_head)
            qh = jnp.where(keep, qp, 0.0).astype(BF16)
            sc = lax.dot_general(qh, kp, (((1,), (1,)), ((), ())),
                                 preferred_element_type=F32)
            decay = jnp.where(allowed, jnp.exp(lg_ref[head] * dist), 0.0)
            ys.append(jnp.dot((sc * decay).astype(BF16), vp, preferred_element_type=F32))
        pairs.append(jnp.where(lo_head, ys[0], ys[1]))
    y = jnp.concatenate(pairs, axis=1) + y_inter

    inv_hd = 1.0 / HEAD_DIM
    mu = head_sums(y) * inv_hd
    d = y - mu
    var = head_sums(d * d) * inv_hd
    g = proj(3)
    y_ret = _silu(g) * (d * lax.rsqrt(var + GN_EPS)) * bret_ref[...]

    b_gate = proj(4)
    u = proj(5) * proj(6)
    ubuf_ref[SUBLANES:SUBLANES + L, :] = u
    u1 = ubuf_ref[SUBLANES - 1:SUBLANES - 1 + L, :]
    u2 = ubuf_ref[SUBLANES - 2:SUBLANES - 2 + L, :]
    ubuf_ref[0:SUBLANES, :] = ubuf_ref[L:L + SUBLANES, :]
    cw = convw_ref[...]
    conv = u2 * cw[0:1] + u1 * cw[1:2] + u * cw[2:3] + convb_ref[...]
    yc = b_gate * conv
    msc = head_sums(yc * yc) * (1.0 / CONV_GROUP_DIM)
    y_conv = yc * lax.rsqrt(msc + EPS) * bconv_ref[...]

    mix = (jnp.dot(y_ret.astype(BF16), wout_ref[0:W, :], preferred_element_type=F32)
           + jnp.dot(y_conv.astype(BF16), wout_ref[W:2 * W, :], preferred_element_type=F32))
    x1 = x + gate_m * mix
    x1_ref[0, tile_rows, :] = x1

    ms2 = jnp.mean(x1 * x1, axis=-1, keepdims=True)
    h2 = x1 * lax.rsqrt(ms2 + EPS) * gffn_ref[...]
    h2 = h2 * (1.0 + scale_f) + shift_f
    _store_token_rows(h2_ref, h2, row0=sub * L)

    hi = h2.astype(BF16)
    lo = (h2 - hi.astype(F32)).astype(BF16)
    w2 = wr_ref[...]
    nt_dims = (((1,), (1,)), ((), ()))
    parts = (lax.dot_general(w2, hi, nt_dims, preferred_element_type=F32)
             + lax.dot_general(w2, lo, nt_dims, preferred_element_type=F32))
    logits_t = parts[:ROUTER_LANES] + parts[ROUTER_LANES:] + br_ref[...]
    logit_ref[:, tile_rows] = logits_t[:ROUTER_ROWS]


def _mixer(x, mod, cos128, sin128, gmix, win_b, convw, convb, bret, bconv, wout_b, gffn,
           wr2, br, lg, lgl, blk):
    bsz, seq, d = x.shape
    L = SEQ_TILE * MIXER_SUBTILES
    ns = seq // L
    W = RET_HEADS * HEAD_DIM
    n = bsz * seq
    const2 = lambda b, s: (0, 0)
    in_specs = [
        pl.BlockSpec(memory_space=pltpu.SMEM),
        pl.BlockSpec((1, L, d), lambda b, s: (b, s, 0)),
        pl.BlockSpec((1, 6, d), lambda b, s: (b, 0, 0)),
        pl.BlockSpec((L, LANES), lambda b, s: (b * ns + s, 0)),
        pl.BlockSpec((L, LANES), lambda b, s: (b * ns + s, 0)),
        pl.BlockSpec((1, d), const2),
        pl.BlockSpec(win_b.shape, const2),
        pl.BlockSpec(convw.shape, const2),
        pl.BlockSpec((1, W), const2),
        pl.BlockSpec((1, W), const2),
        pl.BlockSpec((1, W), const2),
        pl.BlockSpec(wout_b.shape, const2),
        pl.BlockSpec((1, d), const2),
        pl.BlockSpec(wr2.shape, const2),
        pl.BlockSpec((ROUTER_LANES, 1), const2),
        pl.BlockSpec((1, W), const2),
        pl.BlockSpec((W // 2, W // 2), const2),
    ]
    assert d == SUBLANES * LANES, "one token must fill exactly one (8, 128) f32 tile"
    out_shape = (jax.ShapeDtypeStruct((bsz, seq, d), F32),
                 jax.ShapeDtypeStruct((n * SUBLANES, LANES), F32),
                 jax.ShapeDtypeStruct((ROUTER_ROWS, n), F32))
    out_specs = (pl.BlockSpec((1, L, d), lambda b, s: (b, s, 0)),
                 pl.BlockSpec((L * SUBLANES, LANES), lambda b, s: (b * ns + s, 0)),
                 pl.BlockSpec((ROUTER_ROWS, L), lambda b, s: (0, b * ns + s)))
    return pl.pallas_call(
        _mixer_kernel,
        out_shape=out_shape,
        grid=(bsz, ns),
        in_specs=in_specs,
        out_specs=out_specs,
        scratch_shapes=[pltpu.VMEM((2, W // 2, W // 2), F32),
                        pltpu.VMEM((SEQ_TILE + 2 * SUBLANES, W), F32)],
        compiler_params=pltpu.CompilerParams(
            dimension_semantics=("arbitrary", "arbitrary"),
            vmem_limit_bytes=VMEM_LIMIT_BYTES),
        name="mixer",
    )(lg, x, mod, cos128, sin128, gmix, win_b, convw, convb, bret, bconv, wout_b, gffn,
      wr2, br, lgl, blk)


ROUTER_TILE = 2048
RES_E1, RES_E2, RES_W1, RES_W2, RES_A, RES_B = range(6)
PLAN_FIRST_TILE, PLAN_N_TILE, PLAN_PAD_START, PLAN_PAD_N, PLAN_N_USED = range(5)


def _rows8(vals, width):
    rid = lax.broadcasted_iota(jnp.int32, (SUBLANES, width), 0)
    out = jnp.zeros((SUBLANES, width), F32)
    for r, v in enumerate(vals):
        out = jnp.where(rid == r, v, out)
    return out


def _router_kernel(lt_ref, slab_ref, dest_ref, plan_ref, res_ref, cnt_ref):
    T = ROUTER_TILE
    phase = pl.program_id(0)
    j = pl.program_id(1)
    nblk = T // LANES
    big = F32(1e9)
    rid8 =lax.broadcasted_iota(jnp.int32, (SUBLANES, T), 0).astype(F32)
    rid_e = lax.broadcasted_iota(jnp.int32, (N_EXPERTS, LANES), 0).astype(F32)

    def onehot(e_row, k):
        return jnp.where(rid_e == e_row[:, k * LANES:(k + 1) * LANES], 1.0, 0.0)

    @pl.when((phase == 0) & (j == 0))
    def _():
        cnt_ref[...] = jnp.zeros_like(cnt_ref)

    @pl.when(phase == 0)
    def _():
        lt = lt_ref[...]
        g_rows = lt[ROUTER_GROUP_ROW0:ROUTER_GROUP_ROW0 + SUBLANES]
        gvalid = rid8 < float(N_GROUPS)
        gm = jnp.where(gvalid, g_rows, F32(-jnp.inf))
        gexp = jnp.exp(gm - jnp.max(gm, axis=0, keepdims=True))
        gp = gexp / jnp.sum(gexp, axis=0, keepdims=True)
        g_top = jnp.max(gp, axis=0, keepdims=True)
        g_idx = jnp.min(jnp.where(gvalid & (gp == g_top), rid8, big), axis=0, keepdims=True)

        def group_slab(g):
            r0 = ROUTER_EXPERT_ROW0 + g * EXPERTS_PER_GROUP
            return lt[r0:r0 + EXPERTS_PER_GROUP]
        sel = group_slab(N_GROUPS - 1)
        for g in range(N_GROUPS - 2, -1, -1):
            sel = jnp.where(g_idx == float(g), group_slab(g), sel)
        eexp = jnp.exp(sel - jnp.max(sel, axis=0, keepdims=True))
        ep = eexp / jnp.sum(eexp, axis=0, keepdims=True)
        p1 = jnp.max(ep, axis=0, keepdims=True)
        i1 = jnp.min(jnp.where(ep == p1, rid8, big), axis=0, keepdims=True)
        m2 = rid8 != i1
        p2 = jnp.max(jnp.where(m2, ep, -1.0), axis=0, keepdims=True)
        i2 = jnp.min(jnp.where(m2 & (ep == p2), rid8, big), axis=0, keepdims=True)
        den = p1 + p2
        w1 = p1 / den * g_top
        w2 = p2 / den * g_top
        e1 = g_idx * float(EXPERTS_PER_GROUP) + i1
        e2 = g_idx * float(EXPERTS_PER_GROUP) + i2

        ii = lax.broadcasted_iota(jnp.int32, (LANES, LANES), 0)
        jj = lax.broadcasted_iota(jnp.int32, (LANES, LANES), 1)
        upper = jnp.where(ii < jj, 1.0, 0.0).astype(BF16)
        base = cnt_ref[...]
        ranks1, ranks2 = [], []
        for k in range(nblk):
            o1, o2 = onehot(e1, k), onehot(e2, k)
            r1 = jnp.dot(o1.astype(BF16), upper, preferred_element_type=F32)
            r2 = jnp.dot(o2.astype(BF16), upper, preferred_element_type=F32)
            c1 = jnp.sum(o1, axis=1, keepdims=True)
            c2 = jnp.sum(o2, axis=1, keepdims=True)
            ranks1.append(jnp.sum(o1 * (base + r1), axis=0, keepdims=True))
            ranks2.append(jnp.sum(o2 * (base + c1 + r2), axis=0, keepdims=True))
            base = base + c1 + c2
        cnt_ref[...] = base
        rank1 = jnp.concatenate(ranks1, axis=1)
        rank2 = jnp.concatenate(ranks2, axis=1)
        res_ref[j] = _rows8([e1, e2, w1, w2, rank1, rank2], T)

    @pl.when(phase == 1)
    def _():
        tm = float(EXPERT_TILE)
        cnt = cnt_ref[...]
        tiles = jnp.floor((cnt + (tm - 1.0)) * (1.0 / tm))
        ei = lax.broadcasted_iota(jnp.int32, (N_EXPERTS, N_EXPERTS), 0)
        ej = lax.broadcasted_iota(jnp.int32, (N_EXPERTS, N_EXPERTS), 1)
        lower = jnp.where(ej < ei, 1.0, 0.0).astype(BF16)
        first = jnp.dot(lower, tiles.astype(BF16), preferred_element_type=F32)
        starts = first * tm

        res = res_ref[j]
        e1, e2 = res[RES_E1:RES_E1 + 1], res[RES_E2:RES_E2 + 1]
        d1, d2 = [], []
        for k in range(nblk):
            blk_lanes = slice(k * LANES, (k + 1) * LANES)
            d1.append(res[RES_A:RES_A + 1, blk_lanes]
                      + jnp.sum(onehot(e1, k) * starts, axis=0, keepdims=True))
            d2.append(res[RES_B:RES_B + 1, blk_lanes]
                      + jnp.sum(onehot(e2, k) * starts, axis=0, keepdims=True))
        dest1 = jnp.concatenate(d1, axis=1)
        dest2 = jnp.concatenate(d2, axis=1)
        dest_ref[...] = _rows8([dest1, dest2], T).astype(jnp.int32)

        table = jnp.concatenate(
            [_rows8([e1, e2, res[RES_W1:RES_W1 + 1], res[RES_W2:RES_W2 + 1], dest1, dest2], T),
             jnp.zeros((ROUTER_LANES - SUBLANES, T), F32)], axis=0)
        slab_ref[...] = table.T

        @pl.when(j == 0)
        def _():
            lane_e = lax.broadcasted_iota(jnp.int32, (N_EXPERTS, LANES), 1).astype(F32)

            def as_row(col):
                return jnp.sum(jnp.where(rid_e == lane_e, col, 0.0), axis=0, keepdims=True)
            n_used = jnp.sum(tiles, axis=0, keepdims=True)
            padded = tiles * tm
            plan_ref[...] = _rows8([as_row(first), as_row(tiles), as_row(starts + cnt),
                                    as_row(padded - cnt), n_used], LANES).astype(jnp.int32)


def _router(logits_t):
    n = logits_t.shape[1]
    T = ROUTER_TILE
    nt = n // T
    return pl.pallas_call(
        _router_kernel,
        out_shape=(jax.ShapeDtypeStruct((n, ROUTER_LANES), F32),
                   jax.ShapeDtypeStruct((SUBLANES, n), jnp.int32),
                   jax.ShapeDtypeStruct((SUBLANES, LANES), jnp.int32)),
        grid=(2, nt),
        in_specs=[pl.BlockSpec((ROUTER_ROWS, T), lambda p, j: (0, j * (1 - p) + (nt - 1) * p))],
        out_specs=(pl.BlockSpec((T, ROUTER_LANES), lambda p, j: (j * p, 0)),
                   pl.BlockSpec((SUBLANES, T), lambda p, j: (0, j * p)),
                   pl.BlockSpec((SUBLANES, LANES), lambda p, j: (0, 0))),
        scratch_shapes=[pltpu.VMEM((nt, SUBLANES, T), F32),
                        pltpu.VMEM((N_EXPERTS, LANES), F32)],
        compiler_params=pltpu.CompilerParams(
            dimension_semantics=("arbitrary", "arbitrary"), vmem_limit_bytes=VMEM_LIMIT_BYTES),
        name="router",
    )(logits_t)


DISPATCH_TILE = 512
ROW_DMA_UNROLL = 8
PAD_UNITS = tuple(1 << b for b in reversed(range(EXPERT_TILE.bit_length() - 1)))


DISPATCH_SLOTS = 3


def _dispatch_kernel(d0_ref, d1_ref, plan_ref, h2_hbm, xs_hbm,
                     stage, zbuf, isem, ssem, zsem):
    dt = DISPATCH_TILE
    ps = SUBLANES
    i = pl.program_id(0)
    nsteps = pl.num_programs(0)
    zrows = PAD_UNITS[0]

    def in_copy(blk, slot):
        src = h2_hbm.at[pl.ds(pl.multiple_of(blk * (dt * ps), dt * ps), dt * ps), :]
        return pltpu.make_async_copy(src, stage.at[slot], isem.at[slot])

    def wait_rows(slot):
        for _ in range(2):
            pltpu.make_async_copy(stage.at[slot], xs_hbm.at[pl.ds(0, dt * ps), :],
                                  ssem.at[slot]).wait()

    def pad_copy(start, unit):
        return pltpu.make_async_copy(zbuf.at[pl.ds(0, unit * ps), :],
                                     xs_hbm.at[pl.ds(pl.multiple_of(start * ps, ps), unit * ps), :],
                                     zsem)

    def pad_pass(do):
        def per_expert(e, carry):
            start = plan_ref[PLAN_PAD_START, e]
            npad = plan_ref[PLAN_PAD_N, e]
            for unit in PAD_UNITS:
                @pl.when((npad & unit) != 0)
                def _():
                    do(pad_copy(start + (npad & ~(2 * unit - 1)), unit))
            return carry
        lax.fori_loop(0, N_EXPERTS, per_expert, 0)

    def tail_pass(do):
        def per_unit(k, carry):
            do(pad_copy(k * zrows, zrows))
            return carry
        per_tile = EXPERT_TILE // zrows
        n_units = xs_hbm.shape[0] // (zrows * ps)
        lax.fori_loop(plan_ref[PLAN_N_USED, 0] * per_tile, n_units, per_unit, 0)

    slot = i % DISPATCH_SLOTS

    @pl.when(i == 0)
    def _():
        in_copy(0, 0).start()

        @pl.when(nsteps > 1)
        def _():
            in_copy(1, 1).start()

        zbuf[...] = jnp.zeros_like(zbuf)
        pad_pass(lambda cp: cp.start())
        tail_pass(lambda cp: cp.start())

    @pl.when(i >= 1)
    def _():
        wait_rows((i - 1) % DISPATCH_SLOTS)

    @pl.when(i + 2 < nsteps)
    def _():
        in_copy(i + 2, (i + 2) % DISPATCH_SLOTS).start()

    in_copy(i, slot).wait()
    base = i * dt
    src_ref = stage.at[slot]

    def body(r, carry):
        src = _token_rows(src_ref, r)
        for prio, d_ref in enumerate((d0_ref, d1_ref)):
            pltpu.make_async_copy(src, _token_rows(xs_hbm, d_ref[base + r]),
                                  ssem.at[slot]).start(priority=prio)
        return carry

    lax.fori_loop(0, dt, body, 0, unroll=ROW_DMA_UNROLL)

    @pl.when(i == nsteps - 1)
    def _():
        wait_rows(slot)
        pad_pass(lambda cp: cp.wait())
        tail_pass(lambda cp: cp.wait())


def _dispatch(dest0, dest1, plan, h2t, p_rows):
    n = dest0.shape[0]
    dt = DISPATCH_TILE
    grid_spec = pltpu.PrefetchScalarGridSpec(
        num_scalar_prefetch=3,
        grid=(n // dt,),
        in_specs=[pl.BlockSpec(memory_space=pl.ANY)],
        out_specs=pl.BlockSpec(memory_space=pl.ANY),
        scratch_shapes=[pltpu.VMEM((DISPATCH_SLOTS, dt * SUBLANES, LANES), F32),
                        pltpu.VMEM((PAD_UNITS[0] * SUBLANES, LANES), F32),
                        pltpu.SemaphoreType.DMA((DISPATCH_SLOTS,)),
                        pltpu.SemaphoreType.DMA((DISPATCH_SLOTS,)),
                        pltpu.SemaphoreType.DMA(())],
    )
    return pl.pallas_call(
        _dispatch_kernel,
        out_shape=jax.ShapeDtypeStruct((p_rows * SUBLANES, LANES), F32),
        grid_spec=grid_spec,
        compiler_params=pltpu.CompilerParams(dimension_semantics=("arbitrary",)),
        name="dispatch",
    )(dest0, dest1, plan, h2t)


EXPERT_IN_SLOTS = 4
EXPERT_OUT_SLOTS = 3


def _expert_kernel(plan_ref, xs_hbm, wg_ref, wu_ref, wd_ref, y_hbm,
                   xbuf, ybuf, wgb, wub, wdb, isem, osem):
    tm = EXPERT_TILE
    rows = tm * SUBLANES
    ni, no = EXPERT_IN_SLOTS, EXPERT_OUT_SLOTS
    e = pl.program_id(0)
    n_used = plan_ref[PLAN_N_USED, 0]
    n_mine = plan_ref[PLAN_N_TILE, e]
    n_tiles = y_hbm.shape[0] // rows

    def tile_rows(ref, g):
        return ref.at[pl.ds(pl.multiple_of(g * rows, rows), rows), :]

    def in_copy(g):
        return pltpu.make_async_copy(tile_rows(xs_hbm, g), xbuf.at[g % ni], isem.at[g % ni])

    def out_copy(g):
        return pltpu.make_async_copy(ybuf.at[g % no], tile_rows(y_hbm, g), osem.at[g % no])

    @pl.when(e == 0)
    def _():
        for g0 in range(ni - 1):
            @pl.when(g0 < n_used)
            def _():
                in_copy(g0).start()

    @pl.when(n_mine > 0)
    def _():
        wgb[...] = wg_ref[0].astype(BF16)
        wub[...] = wu_ref[0].astype(BF16)
        wdb[...] = wd_ref[0].astype(BF16)

    def tile(g, carry):
        in_copy(g).wait()

        @pl.when(g + ni - 1 < n_used)
        def _():
            in_copy(g + ni - 1).start()

        xb = _load_token_rows(xbuf.at[g % ni], tm).astype(BF16)
        a = jnp.dot(xb, wgb[...], preferred_element_type=F32)
        u = jnp.dot(xb, wub[...], preferred_element_type=F32)
        hid = (_silu(a) * u).astype(BF16)
        y = jnp.dot(hid, wdb[...], preferred_element_type=F32)

        @pl.when(g >= no)
        def _():
            out_copy(g - no).wait()

        _store_token_rows(ybuf.at[g % no], y)
        out_copy(g).start()
        return carry

    first = plan_ref[PLAN_FIRST_TILE, e]
    lax.fori_loop(first, first + n_mine, tile, 0)

    @pl.when(e == pl.num_programs(0) - 1)
    def _():
        for back in range(no, 0, -1):
            @pl.when(n_used >= back)
            def _():
                out_copy(n_used - back).wait()

        ybuf[0] = jnp.zeros((rows, LANES), F32)

        def zero_copy(g):
            return pltpu.make_async_copy(ybuf.at[0], tile_rows(y_hbm, g), osem.at[0])

        def fill(g, carry):
            zero_copy(g).start()
            return carry

        def drain(g, carry):
            zero_copy(g).wait()
            return carry

        lax.fori_loop(n_used, n_tiles, fill, 0)
        lax.fori_loop(n_used, n_tiles, drain, 0)


def _experts(plan, xs, wg, wu, wd):
    tm = EXPERT_TILE
    n_exp, d, de = wg.shape
    grid_spec = pltpu.PrefetchScalarGridSpec(
        num_scalar_prefetch=1,
        grid=(n_exp,),
        in_specs=[
            pl.BlockSpec(memory_space=pl.ANY),
            pl.BlockSpec((1, d, de), lambda e, *_: (e, 0, 0)),
            pl.BlockSpec((1, d, de), lambda e, *_: (e, 0, 0)),
            pl.BlockSpec((1, de, d), lambda e, *_: (e, 0, 0)),
        ],
        out_specs=pl.BlockSpec(memory_space=pl.ANY),
        scratch_shapes=[pltpu.VMEM((EXPERT_IN_SLOTS, tm * SUBLANES, LANES), F32),
                        pltpu.VMEM((EXPERT_OUT_SLOTS, tm * SUBLANES, LANES), F32),
                        pltpu.VMEM((d, de), BF16),
                        pltpu.VMEM((d, de), BF16),
                        pltpu.VMEM((de, d), BF16),
                        pltpu.SemaphoreType.DMA((EXPERT_IN_SLOTS,)),
                        pltpu.SemaphoreType.DMA((EXPERT_OUT_SLOTS,))],
    )
    return pl.pallas_call(
        _expert_kernel,
        out_shape=jax.ShapeDtypeStruct(xs.shape, F32),
        grid_spec=grid_spec,
        compiler_params=pltpu.CompilerParams(
            dimension_semantics=("arbitrary",),
            vmem_limit_bytes=VMEM_LIMIT_BYTES),
        name="experts",
    )(plan, xs, wg, wu, wd)


COMBINE_SLOTS = 3


def _combine_kernel(p0_ref, p1_ref, y_hbm, x1_ref, slab_ref, mod_ref, gfin_ref, o_ref,
                    *scratch):
    tm = COMBINE_TILE
    ns = COMBINE_SLOTS
    ahead = ns - 1
    ybufs, sem = scratch[:ns], scratch[ns]
    i = pl.program_id(0)
    nt = pl.num_programs(0)

    def row_copies(tile, slot, r):
        for j, p_ref in enumerate((p0_ref, p1_ref)):
            pltpu.make_async_copy(_token_rows(y_hbm, p_ref[tile * tm + r]),
                                  _token_rows(ybufs[slot].at[j], r),
                                  sem.at[slot]).start(priority=j)

    def wait_tile(slot):
        for j in range(2):
            pltpu.make_async_copy(y_hbm.at[pl.ds(0, tm * SUBLANES), :], ybufs[slot].at[j],
                                  sem.at[slot]).wait()

    @pl.when(i == 0)
    def _():
        for t0 in range(ahead):
            @pl.when(t0 < nt)
            def _():
                def body(r, carry):
                    row_copies(t0, t0, r)
                    return carry
                lax.fori_loop(0, tm, body, 0, unroll=ROW_DMA_UNROLL)

    n_chunks = SUBLANES
    batch = tm // n_chunks

    def step(slot):
        wait_tile(slot)
        nxt = jnp.minimum(i + ahead, nt - 1)
        nslot = (slot + ahead) % ns
        slab = slab_ref[...]
        w0 = slab[:, 2:3]
        w1 = slab[:, 3:4]
        gate_f = mod_ref[0][5:6]
        sq = jnp.zeros((tm, LANES), F32)
        for c in range(n_chunks):
            lanes = slice(c * LANES, (c + 1) * LANES)
            y0 = ybufs[slot].at[0][pl.ds(c, tm, stride=SUBLANES), :]
            y1 = ybufs[slot].at[1][pl.ds(c, tm, stride=SUBLANES), :]
            xo = x1_ref[:, lanes] + gate_f[:, lanes] * (w0 * y0 + w1 * y1)
            sq = sq + xo * xo
            o_ref[:, lanes] = xo
            for r in range(c * batch, (c + 1) * batch):
                row_copies(nxt, nslot, r)
        ms = jnp.sum(sq, axis=-1, keepdims=True) * (1.0 / (n_chunks * LANES))
        scale = lax.rsqrt(ms + EPS)
        for c in range(n_chunks):
            lanes = slice(c * LANES, (c + 1) * LANES)
            o_ref[:, lanes] = o_ref[:, lanes] * scale * gfin_ref[:, lanes]

    for slot in range(ns):
        @pl.when(i % ns == slot)
        def _():
            step(slot)

    @pl.when(i == nt - 1)
    def _():
        for k in range(ahead):
            for slot in range(ns):
                @pl.when((nt - 1 - k >= 0) & ((nt - 1 - k + ahead) % ns == slot))
                def _():
                    wait_tile(slot)


def _combine(p0, p1, y, x1, slab, mod, gfin, seq):
    n, d = x1.shape
    tm = COMBINE_TILE
    tiles_per_seq = seq // tm
    grid_spec = pltpu.PrefetchScalarGridSpec(
        num_scalar_prefetch=2,
        grid=(n // tm,),
        in_specs=[
            pl.BlockSpec(memory_space=pl.ANY),
            pl.BlockSpec((tm, d), lambda i, a, b: (i, 0)),
            pl.BlockSpec((tm, ROUTER_LANES), lambda i, a, b: (i, 0)),
            pl.BlockSpec((1, 6, d), lambda i, a, b: (i // tiles_per_seq, 0, 0)),
            pl.BlockSpec((1, d), lambda i, a, b: (0, 0)),
        ],
        out_specs=pl.BlockSpec((tm, d), lambda i, a, b: (i, 0)),
        scratch_shapes=([pltpu.VMEM((2, tm * SUBLANES, LANES), F32)] * COMBINE_SLOTS
                        + [pltpu.SemaphoreType.DMA((COMBINE_SLOTS,))]),
    )
    return pl.pallas_call(
        _combine_kernel,
        out_shape=jax.ShapeDtypeStruct((n, d), F32),
        grid_spec=grid_spec,
        compiler_params=pltpu.CompilerParams(
            dimension_semantics=("arbitrary",),
            vmem_limit_bytes=VMEM_LIMIT_BYTES),
        name="combine",
    )(p0, p1, y, x1, slab, mod, gfin)


def kernel(x, c, positions, ada_w, ada_b, norm_mix_g, norm_ffn_g, w_in, conv_w, conv_b,
           beta_ret, beta_conv, w_out, router_group_w, router_group_b, router_expert_w,
           router_expert_b, expert_w_gate, expert_w_up, expert_w_down, norm_final_g):
    bsz, seq, d = x.shape
    n = bsz * seq
    depth = ada_w.shape[0]
    assert depth == 1, "the combine kernel fuses the trunk's final RMSNorm (single layer)"
    W = RET_HEADS * HEAD_DIM

    cos128, sin128 = _rope_tables(positions)
    heads = jnp.arange(RET_HEADS, dtype=F32)
    lg = jnp.log1p(-jnp.exp2(-5.0 - heads))
    lgl = jnp.repeat(lg, HEAD_DIM).reshape(1, W)
    assert CONV_GROUP_DIM == HEAD_DIM, "conv groups and retention heads share the 64-lane block sums"
    blk_np = np.kron(np.eye(RET_HEADS // 2, dtype=np.float32),
                     np.ones((HEAD_DIM, HEAD_DIM), np.float32))
    blk = jnp.asarray(blk_np, dtype=BF16)

    for l in range(depth):
        mod = _adaln(c, ada_w[l], ada_b[l]).reshape(bsz, 6, d)

        gap = ROUTER_EXPERT_ROW0 - N_GROUPS
        tail = ROUTER_LANES - ROUTER_ROWS
        wr = jnp.concatenate([router_group_w[l].T, jnp.zeros((gap, d), F32),
                              router_expert_w[l].T, jnp.zeros((tail, d), F32)], axis=0)
        wr_hi = wr.astype(BF16)
        wr_lo = (wr - wr_hi.astype(F32)).astype(BF16)
        wr2 = jnp.concatenate([wr_hi, wr_lo], axis=0)
        br = jnp.concatenate([router_group_b[l], jnp.zeros((gap,), F32),
                              router_expert_b[l], jnp.zeros((tail,), F32)]).reshape(ROUTER_LANES, 1)

        x1, h2p, logits_t = _mixer(
            x, mod, cos128, sin128, norm_mix_g[l].reshape(1, d), w_in[l].astype(BF16),
            conv_w[l], conv_b[l].reshape(1, W), beta_ret[l].reshape(1, W),
            beta_conv[l].reshape(1, W), w_out[l].astype(BF16), norm_ffn_g[l].reshape(1, d),
            wr2, br, lg, lgl, blk)
        slab, dest, plan = _router(logits_t)
        p0, p1 = dest[0], dest[1]
        p_rows = 2 * n + N_EXPERTS * EXPERT_TILE

        xs = _dispatch(p0, p1, plan, h2p, p_rows)
        de = expert_w_gate.shape[-1]
        y = _experts(plan, xs,
                     expert_w_gate[l].reshape(N_EXPERTS, d, de),
                     expert_w_up[l].reshape(N_EXPERTS, d, de),
                     expert_w_down[l].reshape(N_EXPERTS, de, d))
        out = _combine(p0, p1, y, x1.reshape(n, d), slab, mod, norm_final_g.reshape(1, d), seq)
        x = out.reshape(bsz, seq, d)
    return x
```

```python
import functools

import jax
import jax.numpy as jnp
import numpy as np
from jax import lax
from jax.experimental import pallas as pl
from jax.experimental.pallas import tpu as pltpu

F32 = jnp.float32
BF16 = jnp.bfloat16

CHUNK = 64
RET_HEADS = 8
HEAD_DIM = 64
CONV_GROUP_DIM = 64
ROPE_BASE = 10000.0
N_GROUPS = 4
EXPERTS_PER_GROUP = 8
N_EXPERTS = N_GROUPS * EXPERTS_PER_GROUP
EPS = 1e-6
GN_EPS = 1e-5

LANES = 128
SUBLANES = 8
VMEM_LIMIT_BYTES = 56 * 1024 * 1024

SEQ_TILE = 256
EXPERT_TILE = 256
COMBINE_TILE = 256
ROUTER_LANES = LANES
ROUTER_GROUP_ROW0 = 0
ROUTER_EXPERT_ROW0 = SUBLANES
ROUTER_ROWS = ROUTER_EXPERT_ROW0 + N_EXPERTS


def _silu(v):
    return v * (1.0 / (1.0 + jnp.exp(-v)))


def _adaln_kernel(c_ref, w_ref, b_ref, o_ref):
    s = _silu(c_ref[...])
    o_ref[...] = jnp.dot(s, w_ref[...], precision=lax.Precision.HIGHEST,
                         preferred_element_type=F32) + b_ref[...]


def _adaln(c, w, b):
    bsz, d = c.shape
    n = w.shape[1]
    tn = 1024
    return pl.pallas_call(
        _adaln_kernel,
        out_shape=jax.ShapeDtypeStruct((bsz, n), F32),
        grid=(n // tn,),
        in_specs=[pl.BlockSpec((bsz, d), lambda j: (0, 0)),
                  pl.BlockSpec((d, tn), lambda j: (0, j)),
                  pl.BlockSpec((1, tn), lambda j: (0, j))],
        out_specs=pl.BlockSpec((bsz, tn), lambda j: (0, j)),
        name="adaln",
    )(c, w, b.reshape(1, n))


ROPE_ROWS = 1024


def _rope_kernel(pos_ref, invf_ref, cos_ref, sin_ref):
    half = HEAD_DIM // 2
    per_row = LANES // half
    r = ROPE_ROWS
    ang = pos_ref[...].astype(F32) * invf_ref[...]
    lane = lax.broadcasted_iota(jnp.int32, (r, LANES), 1)
    quarter = lane // half
    sign = jnp.where(quarter % 2 == 0, -1.0, 1.0)
    for table, out_ref, scale in ((jnp.cos(ang), cos_ref, None), (jnp.sin(ang), sin_ref, sign)):
        rolled = [table] + [pltpu.roll(table, half * k, 1) for k in range(1, per_row)]
        for q in range(per_row):
            val = rolled[(0 - q) % per_row]
            for k in range(1, per_row):
                val = jnp.where(quarter == k, rolled[(k - q) % per_row], val)
            if scale is not None:
                val = val * scale
            out_ref[pl.ds(q, r, stride=per_row), :] = val


def _rope_tables(positions):
    n = positions.size
    half = HEAD_DIM // 2
    inv_freq = ROPE_BASE ** (-jnp.arange(0, HEAD_DIM, 2, dtype=F32) / HEAD_DIM)
    per_row = LANES // half
    rows = n // per_row
    pos_rep = jnp.broadcast_to(positions.reshape(n, 1), (n, half)).reshape(rows, LANES)
    invf = jnp.tile(inv_freq, per_row).reshape(1, LANES)
    tr = ROPE_ROWS
    return pl.pallas_call(
        _rope_kernel,
        out_shape=(jax.ShapeDtypeStruct((n, LANES), F32),) * 2,
        grid=(rows // tr,),
        in_specs=[pl.BlockSpec((tr, LANES), lambda i: (i, 0)),
                  pl.BlockSpec((1, LANES), lambda i: (0, 0))],
        out_specs=(pl.BlockSpec((tr * per_row, LANES), lambda i: (i, 0)),) * 2,
        name="rope_table",
    )(pos_rep, invf)


def _load_token_rows(ref, rows):
    return jnp.concatenate(
        [ref[pl.ds(c, rows, stride=SUBLANES), :] for c in range(SUBLANES)], axis=1)


def _store_token_rows(ref, val, row0=0):
    rows = val.shape[0]
    for c in range(SUBLANES):
        ref[pl.ds(row0 * SUBLANES + c, rows, stride=SUBLANES), :] = val[:, c * LANES:(c + 1) * LANES]


def _token_rows(ref, row):
    return ref.at[pl.ds(pl.multiple_of(row * SUBLANES, SUBLANES), SUBLANES), :]


MIXER_SUBTILES = 2


def _mixer_kernel(*refs):
    state_ref, ubuf_ref = refs[-2:]

    @pl.when(pl.program_id(1) == 0)
    def _():
        state_ref[...] = jnp.zeros_like(state_ref)
        ubuf_ref[0:SUBLANES, :] = jnp.zeros((SUBLANES, ubuf_ref.shape[1]), F32)

    for sub in range(MIXER_SUBTILES):
        _mixer_tile(sub, *refs)


def _mixer_tile(sub, lg_ref, x_ref, mod_ref, cos_ref, sin_ref, gmix_ref, win_ref, convw_ref,
                convb_ref, bret_ref, bconv_ref, wout_ref, gffn_ref, wr_ref, br_ref, lgl_ref,
                blk_ref, x1_ref, h2_ref, logit_ref, state_ref, ubuf_ref):
    L = SEQ_TILE
    W = RET_HEADS * HEAD_DIM
    tile_rows = slice(sub * L, (sub + 1) * L)

    x = x_ref[0, tile_rows, :]
    mod = mod_ref[0]
    shift_m, scale_m, gate_m = mod[0:1], mod[1:2], mod[2:3]
    shift_f, scale_f = mod[3:4], mod[4:5]

    ms = jnp.mean(x * x, axis=-1, keepdims=True)
    h = x * lax.rsqrt(ms + EPS) * gmix_ref[...]
    h = h * (1.0 + scale_m) + shift_m
    hb = h.astype(BF16)

    def proj(i):
        return jnp.dot(hb, win_ref[:, i * W:(i + 1) * W], preferred_element_type=F32)

    cos = jnp.concatenate([cos_ref[tile_rows, :]] * 4, axis=1)
    sin = jnp.concatenate([sin_ref[tile_rows, :]] * 4, axis=1)
    lane_w = lax.broadcasted_iota(jnp.int32, (L, W), 1)
    first_half = (lane_w & (HEAD_DIM - 1)) < (HEAD_DIM // 2)

    def rot(t):
        partner = jnp.where(first_half, pltpu.roll(t, W - HEAD_DIM // 2, 1),
                            pltpu.roll(t, HEAD_DIM // 2, 1))
        return t * cos + partner * sin

    q = rot(proj(0))
    k = rot(proj(1)) * (HEAD_DIM ** -0.5)
    v = proj(2)
    vb = v.astype(BF16)
    kb = k.astype(BF16)

    lgl = lgl_ref[...]
    rowf = lax.broadcasted_iota(jnp.int32, (L, W), 0).astype(F32)
    qd = q * jnp.exp(lgl * (rowf + 1.0))
    kd = k * jnp.exp(lgl * (float(L - 1) - rowf))
    blk = blk_ref[...]
    HW = W // 2
    blk_f = blk.astype(F32)
    qdb = qd.astype(BF16)
    kdb = kd.astype(BF16)
    state_decay = jnp.exp(lgl * float(L))
    inter = []
    for hf in range(2):
        sl = slice(hf * HW, (hf + 1) * HW)
        st = state_ref[hf]
        inter.append(jnp.dot(qdb[:, sl], st.astype(BF16), preferred_element_type=F32))
        kv = lax.dot_general(kdb[:, sl], vb[:, sl], (((0,), (0,)), ((), ())),
                             preferred_element_type=F32)
        state_ref[hf] = st * state_decay[:, sl] + kv * blk_f
    y_inter = jnp.concatenate(inter, axis=1)

    def head_sums(t):
        tb = t.astype(BF16)
        return jnp.concatenate(
            [jnp.dot(tb[:, hf * HW:(hf + 1) * HW], blk, preferred_element_type=F32)
             for hf in range(2)], axis=1)

    ii = lax.broadcasted_iota(jnp.int32, (L, L), 0)
    jj = lax.broadcasted_iota(jnp.int32, (L, L), 1)
    dist = jnp.abs(ii - jj).astype(F32)
    allowed = (jj // CHUNK) <= (ii // CHUNK)
    lane_p = lax.broadcasted_iota(jnp.int32, (L, LANES), 1)
    lo_head = lane_p < HEAD_DIM
    pairs = []
    for p in range(RET_HEADS // 2):
        sl = slice(p * LANES, (p + 1) * LANES)
        qp, kp, vp = q[:, sl], kb[:, sl], vb[:, sl]
        ys = []
        for hh in range(2):
            head = 2 * p + hh
            keep = lo_head if hh == 0 else jnp.logical_not(lo_head)
            qh = jnp.where(keep, qp, 0.0).astype(BF16)
            sc = lax.dot_general(qh, kp, (((1,), (1,)), ((), ())),
                                 preferred_element_type=F32)
            decay = jnp.where(allowed, jnp.exp(lg_ref[head] * dist), 0.0)
            ys.append(jnp.dot((sc * decay).astype(BF16), vp, preferred_element_type=F32))
        pairs.append(jnp.where(lo_head, ys[0], ys[1]))
    y = jnp.concatenate(pairs, axis=1) + y_inter

    inv_hd = 1.0 / HEAD_DIM
    mu = head_sums(y) * inv_hd
    d = y - mu
    var = head_sums(d * d) * inv_hd
    g = proj(3)
    y_ret = _silu(g) * (d * lax.rsqrt(var + GN_EPS)) * bret_ref[...]

    b_gate = proj(4)
    u = proj(5) * proj(6)
    ubuf_ref[SUBLANES:SUBLANES + L, :] = u
    u1 = ubuf_ref[SUBLANES - 1:SUBLANES - 1 + L, :]
    u2 = ubuf_ref[SUBLANES - 2:SUBLANES - 2 + L, :]
    ubuf_ref[0:SUBLANES, :] = ubuf_ref[L:L + SUBLANES, :]
    cw = convw_ref[...]
    conv = u2 * cw[0:1] + u1 * cw[1:2] + u * cw[2:3] + convb_ref[...]
    yc = b_gate * conv
    msc = head_sums(yc * yc) * (1.0 / CONV_GROUP_DIM)
    y_conv = yc * lax.rsqrt(msc + EPS) * bconv_ref[...]

    mix = (jnp.dot(y_ret.astype(BF16), wout_ref[0:W, :], preferred_element_type=F32)
           + jnp.dot(y_conv.astype(BF16), wout_ref[W:2 * W, :], preferred_element_type=F32))
    x1 = x + gate_m * mix
    x1_ref[0, tile_rows, :] = x1

    ms2 = jnp.mean(x1 * x1, axis=-1, keepdims=True)
    h2 = x1 * lax.rsqrt(ms2 + EPS) * gffn_ref[...]
    h2 = h2 * (1.0 + scale_f) + shift_f
    _store_token_rows(h2_ref, h2, row0=sub * L)

    hi = h2.astype(BF16)
    lo = (h2 - hi.astype(F32)).astype(BF16)
    w2 = wr_ref[...]
    nt_dims = (((1,), (1,)), ((), ()))
    parts = (lax.dot_general(w2, hi, nt_dims, preferred_element_type=F32)
             + lax.dot_general(w2, lo, nt_dims, preferred_element_type=F32))
    logits_t = parts[:ROUTER_LANES] + parts[ROUTER_LANES:] + br_ref[...]
    logit_ref[:, tile_rows] = logits_t[:ROUTER_ROWS]


def _mixer(x, mod, cos128, sin128, gmix, win_b, convw, convb, bret, bconv, wout_b, gffn,
           wr2, br, lg, lgl, blk):
    bsz, seq, d = x.shape
    L = SEQ_TILE * MIXER_SUBTILES
    ns = seq // L
    W = RET_HEADS * HEAD_DIM
    n = bsz * seq
    const2 = lambda b, s: (0, 0)
    in_specs = [
        pl.BlockSpec(memory_space=pltpu.SMEM),
        pl.BlockSpec((1, L, d), lambda b, s: (b, s, 0)),
        pl.BlockSpec((1, 6, d), lambda b, s: (b, 0, 0)),
        pl.BlockSpec((L, LANES), lambda b, s: (b * ns + s, 0)),
        pl.BlockSpec((L, LANES), lambda b, s: (b * ns + s, 0)),
        pl.BlockSpec((1, d), const2),
        pl.BlockSpec(win_b.shape, const2),
        pl.BlockSpec(convw.shape, const2),
        pl.BlockSpec((1, W), const2),
        pl.BlockSpec((1, W), const2),
        pl.BlockSpec((1, W), const2),
        pl.BlockSpec(wout_b.shape, const2),
        pl.BlockSpec((1, d), const2),
        pl.BlockSpec(wr2.shape, const2),
        pl.BlockSpec((ROUTER_LANES, 1), const2),
        pl.BlockSpec((1, W), const2),
        pl.BlockSpec((W // 2, W // 2), const2),
    ]
    assert d == SUBLANES * LANES, "one token must fill exactly one (8, 128) f32 tile"
    out_shape = (jax.ShapeDtypeStruct((bsz, seq, d), F32),
                 jax.ShapeDtypeStruct((n * SUBLANES, LANES), F32),
                 jax.ShapeDtypeStruct((ROUTER_ROWS, n), F32))
    out_specs = (pl.BlockSpec((1, L, d), lambda b, s: (b, s, 0)),
                 pl.BlockSpec((L * SUBLANES, LANES), lambda b, s: (b * ns + s, 0)),
                 pl.BlockSpec((ROUTER_ROWS, L), lambda b, s: (0, b * ns + s)))
    return pl.pallas_call(
        _mixer_kernel,
        out_shape=out_shape,
        grid=(bsz, ns),
        in_specs=in_specs,
        out_specs=out_specs,
        scratch_shapes=[pltpu.VMEM((2, W // 2, W // 2), F32),
                        pltpu.VMEM((SEQ_TILE + 2 * SUBLANES, W), F32)],
        compiler_params=pltpu.CompilerParams(
            dimension_semantics=("arbitrary", "arbitrary"),
            vmem_limit_bytes=VMEM_LIMIT_BYTES),
        name="mixer",
    )(lg, x, mod, cos128, sin128, gmix, win_b, convw, convb, bret, bconv, wout_b, gffn,
      wr2, br, lgl, blk)


ROUTER_TILE = 2048
RES_E1, RES_E2, RES_W1, RES_W2, RES_A, RES_B = range(6)
PLAN_FIRST_TILE, PLAN_N_TILE, PLAN_PAD_START, PLAN_PAD_N, PLAN_N_USED = range(5)


def _rows8(vals, width):
    rid = lax.broadcasted_iota(jnp.int32, (SUBLANES, width), 0)
    out = jnp.zeros((SUBLANES, width), F32)
    for r, v in enumerate(vals):
        out = jnp.where(rid == r, v, out)
    return out


def _router_kernel(lt_ref, slab_ref, dest_ref, plan_ref, res_ref, cnt_ref):
    T = ROUTER_TILE
    phase = pl.program_id(0)
    j = pl.program_id(1)
    nblk = T // LANES
    big = F32(1e9)
    rid8 =lax.broadcasted_iota(jnp.int32, (SUBLANES, T), 0).astype(F32)
    rid_e = lax.broadcasted_iota(jnp.int32, (N_EXPERTS, LANES), 0).astype(F32)

    def onehot(e_row, k):
        return jnp.where(rid_e == e_row[:, k * LANES:(k + 1) * LANES], 1.0, 0.0)

    @pl.when((phase == 0) & (j == 0))
    def _():
        cnt_ref[...] = jnp.zeros_like(cnt_ref)

    @pl.when(phase == 0)
    def _():
        lt = lt_ref[...]
        g_rows = lt[ROUTER_GROUP_ROW0:ROUTER_GROUP_ROW0 + SUBLANES]
        gvalid = rid8 < float(N_GROUPS)
        gm = jnp.where(gvalid, g_rows, F32(-jnp.inf))
        gexp = jnp.exp(gm - jnp.max(gm, axis=0, keepdims=True))
        gp = gexp / jnp.sum(gexp, axis=0, keepdims=True)
        g_top = jnp.max(gp, axis=0, keepdims=True)
        g_idx = jnp.min(jnp.where(gvalid & (gp == g_top), rid8, big), axis=0, keepdims=True)

        def group_slab(g):
            r0 = ROUTER_EXPERT_ROW0 + g * EXPERTS_PER_GROUP
            return lt[r0:r0 + EXPERTS_PER_GROUP]
        sel = group_slab(N_GROUPS - 1)
        for g in range(N_GROUPS - 2, -1, -1):
            sel = jnp.where(g_idx == float(g), group_slab(g), sel)
        eexp = jnp.exp(sel - jnp.max(sel, axis=0, keepdims=True))
        ep = eexp / jnp.sum(eexp, axis=0, keepdims=True)
        p1 = jnp.max(ep, axis=0, keepdims=True)
        i1 = jnp.min(jnp.where(ep == p1, rid8, big), axis=0, keepdims=True)
        m2 = rid8 != i1
        p2 = jnp.max(jnp.where(m2, ep, -1.0), axis=0, keepdims=True)
        i2 = jnp.min(jnp.where(m2 & (ep == p2), rid8, big), axis=0, keepdims=True)
        den = p1 + p2
        w1 = p1 / den * g_top
        w2 = p2 / den * g_top
        e1 = g_idx * float(EXPERTS_PER_GROUP) + i1
        e2 = g_idx * float(EXPERTS_PER_GROUP) + i2

        ii = lax.broadcasted_iota(jnp.int32, (LANES, LANES), 0)
        jj = lax.broadcasted_iota(jnp.int32, (LANES, LANES), 1)
        upper = jnp.where(ii < jj, 1.0, 0.0).astype(BF16)
        base = cnt_ref[...]
        ranks1, ranks2 = [], []
        for k in range(nblk):
            o1, o2 = onehot(e1, k), onehot(e2, k)
            r1 = jnp.dot(o1.astype(BF16), upper, preferred_element_type=F32)
            r2 = jnp.dot(o2.astype(BF16), upper, preferred_element_type=F32)
            c1 = jnp.sum(o1, axis=1, keepdims=True)
            c2 = jnp.sum(o2, axis=1, keepdims=True)
            ranks1.append(jnp.sum(o1 * (base + r1), axis=0, keepdims=True))
            ranks2.append(jnp.sum(o2 * (base + c1 + r2), axis=0, keepdims=True))
            base = base + c1 + c2
        cnt_ref[...] = base
        rank1 = jnp.concatenate(ranks1, axis=1)
        rank2 = jnp.concatenate(ranks2, axis=1)
        res_ref[j] = _rows8([e1, e2, w1, w2, rank1, rank2], T)

    @pl.when(phase == 1)
    def _():
        tm = float(EXPERT_TILE)
        cnt = cnt_ref[...]
        tiles = jnp.floor((cnt + (tm - 1.0)) * (1.0 / tm))
        ei = lax.broadcasted_iota(jnp.int32, (N_EXPERTS, N_EXPERTS), 0)
        ej = lax.broadcasted_iota(jnp.int32, (N_EXPERTS, N_EXPERTS), 1)
        lower = jnp.where(ej < ei, 1.0, 0.0).astype(BF16)
        first = jnp.dot(lower, tiles.astype(BF16), preferred_element_type=F32)
        starts = first * tm

        res = res_ref[j]
        e1, e2 = res[RES_E1:RES_E1 + 1], res[RES_E2:RES_E2 + 1]
        d1, d2 = [], []
        for k in range(nblk):
            blk_lanes = slice(k * LANES, (k + 1) * LANES)
            d1.append(res[RES_A:RES_A + 1, blk_lanes]
                      + jnp.sum(onehot(e1, k) * starts, axis=0, keepdims=True))
            d2.append(res[RES_B:RES_B + 1, blk_lanes]
                      + jnp.sum(onehot(e2, k) * starts, axis=0, keepdims=True))
        dest1 = jnp.concatenate(d1, axis=1)
        dest2 = jnp.concatenate(d2, axis=1)
        dest_ref[...] = _rows8([dest1, dest2], T).astype(jnp.int32)

        table = jnp.concatenate(
            [_rows8([e1, e2, res[RES_W1:RES_W1 + 1], res[RES_W2:RES_W2 + 1], dest1, dest2], T),
             jnp.zeros((ROUTER_LANES - SUBLANES, T), F32)], axis=0)
        slab_ref[...] = table.T

        @pl.when(j == 0)
        def _():
            lane_e = lax.broadcasted_iota(jnp.int32, (N_EXPERTS, LANES), 1).astype(F32)

            def as_row(col):
                return jnp.sum(jnp.where(rid_e == lane_e, col, 0.0), axis=0, keepdims=True)
            n_used = jnp.sum(tiles, axis=0, keepdims=True)
            padded = tiles * tm
            plan_ref[...] = _rows8([as_row(first), as_row(tiles), as_row(starts + cnt),
                                    as_row(padded - cnt), n_used], LANES).astype(jnp.int32)


def _router(logits_t):
    n = logits_t.shape[1]
    T = ROUTER_TILE
    nt = n // T
    return pl.pallas_call(
        _router_kernel,
        out_shape=(jax.ShapeDtypeStruct((n, ROUTER_LANES), F32),
                   jax.ShapeDtypeStruct((SUBLANES, n), jnp.int32),
                   jax.ShapeDtypeStruct((SUBLANES, LANES), jnp.int32)),
        grid=(2, nt),
        in_specs=[pl.BlockSpec((ROUTER_ROWS, T), lambda p, j: (0, j * (1 - p) + (nt - 1) * p))],
        out_specs=(pl.BlockSpec((T, ROUTER_LANES), lambda p, j: (j * p, 0)),
                   pl.BlockSpec((SUBLANES, T), lambda p, j: (0, j * p)),
                   pl.BlockSpec((SUBLANES, LANES), lambda p, j: (0, 0))),
        scratch_shapes=[pltpu.VMEM((nt, SUBLANES, T), F32),
                        pltpu.VMEM((N_EXPERTS, LANES), F32)],
        compiler_params=pltpu.CompilerParams(
            dimension_semantics=("arbitrary", "arbitrary"), vmem_limit_bytes=VMEM_LIMIT_BYTES),
        name="router",
    )(logits_t)


DISPATCH_TILE = 512
ROW_DMA_UNROLL = 8
PAD_UNITS = tuple(1 << b for b in reversed(range(EXPERT_TILE.bit_length() - 1)))


DISPATCH_LAG = 2
DISPATCH_SLOTS = DISPATCH_LAG + 2


def _dispatch_kernel(d0_ref, d1_ref, plan_ref, h2_hbm, xs_hbm,
                     stage, zbuf, isem, ssem, zsem):
    dt = DISPATCH_TILE
    ps = SUBLANES
    i = pl.program_id(0)
    nsteps = pl.num_programs(0)
    zrows = PAD_UNITS[0]

    def in_copy(blk, slot):
        src = h2_hbm.at[pl.ds(pl.multiple_of(blk * (dt * ps), dt * ps), dt * ps), :]
        return pltpu.make_async_copy(src, stage.at[slot], isem.at[slot])

    def wait_rows(slot):
        for _ in range(2):
            pltpu.make_async_copy(stage.at[slot], xs_hbm.at[pl.ds(0, dt * ps), :],
                                  ssem.at[slot]).wait()

    def pad_copy(start, unit):
        return pltpu.make_async_copy(zbuf.at[pl.ds(0, unit * ps), :],
                                     xs_hbm.at[pl.ds(pl.multiple_of(start * ps, ps), unit * ps), :],
                                     zsem)

    def pad_pass(do):
        def per_expert(e, carry):
            start = plan_ref[PLAN_PAD_START, e]
            npad = plan_ref[PLAN_PAD_N, e]
            for unit in PAD_UNITS:
                @pl.when((npad & unit) != 0)
                def _():
                    do(pad_copy(start + (npad & ~(2 * unit - 1)), unit))
            return carry
        lax.fori_loop(0, N_EXPERTS, per_expert, 0)

    def tail_pass(do):
        def per_unit(k, carry):
            do(pad_copy(k * zrows, zrows))
            return carry
        per_tile = EXPERT_TILE // zrows
        n_units = xs_hbm.shape[0] // (zrows * ps)
        lax.fori_loop(plan_ref[PLAN_N_USED, 0] * per_tile, n_units, per_unit, 0)

    slot = i % DISPATCH_SLOTS

    @pl.when(i == 0)
    def _():
        in_copy(0, 0).start()

        @pl.when(nsteps > 1)
        def _():
            in_copy(1, 1).start()

        zbuf[...] = jnp.zeros_like(zbuf)
        pad_pass(lambda cp: cp.start())
        tail_pass(lambda cp: cp.start())

    @pl.when(i >= DISPATCH_LAG)
    def _():
        wait_rows((i - DISPATCH_LAG) % DISPATCH_SLOTS)

    @pl.when(i + 2 < nsteps)
    def _():
        in_copy(i + 2, (i + 2) % DISPATCH_SLOTS).start()

    in_copy(i, slot).wait()
    base = i * dt
    src_ref = stage.at[slot]

    def body(r, carry):
        src = _token_rows(src_ref, r)
        for prio, d_ref in enumerate((d0_ref, d1_ref)):
            pltpu.make_async_copy(src, _token_rows(xs_hbm, d_ref[base + r]),
                                  ssem.at[slot]).start(priority=prio)
        return carry

    lax.fori_loop(0, dt, body, 0, unroll=ROW_DMA_UNROLL)

    @pl.when(i == nsteps - 1)
    def _():
        for back in range(DISPATCH_LAG - 1, -1, -1):
            @pl.when(i - back >= 0)
            def _():
                wait_rows((i - back) % DISPATCH_SLOTS)
        pad_pass(lambda cp: cp.wait())
        tail_pass(lambda cp: cp.wait())


def _dispatch(dest0, dest1, plan, h2t, p_rows):
    n = dest0.shape[0]
    dt = DISPATCH_TILE
    grid_spec = pltpu.PrefetchScalarGridSpec(
        num_scalar_prefetch=3,
        grid=(n // dt,),
        in_specs=[pl.BlockSpec(memory_space=pl.ANY)],
        out_specs=pl.BlockSpec(memory_space=pl.ANY),
        scratch_shapes=[pltpu.VMEM((DISPATCH_SLOTS, dt * SUBLANES, LANES), F32),
                        pltpu.VMEM((PAD_UNITS[0] * SUBLANES, LANES), F32),
                        pltpu.SemaphoreType.DMA((DISPATCH_SLOTS,)),
                        pltpu.SemaphoreType.DMA((DISPATCH_SLOTS,)),
                        pltpu.SemaphoreType.DMA(())],
    )
    return pl.pallas_call(
        _dispatch_kernel,
        out_shape=jax.ShapeDtypeStruct((p_rows * SUBLANES, LANES), F32),
        grid_spec=grid_spec,
        compiler_params=pltpu.CompilerParams(dimension_semantics=("arbitrary",)),
        name="dispatch",
    )(dest0, dest1, plan, h2t)


EXPERT_IN_SLOTS = 4
EXPERT_OUT_SLOTS = 3


def _expert_kernel(plan_ref, xs_hbm, wg_ref, wu_ref, wd_ref, y_hbm,
                   xbuf, ybuf, wgb, wub, wdb, isem, osem):
    tm = EXPERT_TILE
    rows = tm * SUBLANES
    ni, no = EXPERT_IN_SLOTS, EXPERT_OUT_SLOTS
    e = pl.program_id(0)
    n_used = plan_ref[PLAN_N_USED, 0]
    n_mine = plan_ref[PLAN_N_TILE, e]
    n_tiles = y_hbm.shape[0] // rows

    def tile_rows(ref, g):
        return ref.at[pl.ds(pl.multiple_of(g * rows, rows), rows), :]

    def in_copy(g):
        return pltpu.make_async_copy(tile_rows(xs_hbm, g), xbuf.at[g % ni], isem.at[g % ni])

    def out_copy(g):
        return pltpu.make_async_copy(ybuf.at[g % no], tile_rows(y_hbm, g), osem.at[g % no])

    @pl.when(e == 0)
    def _():
        for g0 in range(ni - 1):
            @pl.when(g0 < n_used)
            def _():
                in_copy(g0).start()

    @pl.when(n_mine > 0)
    def _():
        wgb[...] = wg_ref[0].astype(BF16)
        wub[...] = wu_ref[0].astype(BF16)
        wdb[...] = wd_ref[0].astype(BF16)

    def tile(g, carry):
        in_copy(g).wait()

        @pl.when(g + ni - 1 < n_used)
        def _():
            in_copy(g + ni - 1).start()

        xb = _load_token_rows(xbuf.at[g % ni], tm).astype(BF16)
        a = jnp.dot(xb, wgb[...], preferred_element_type=F32)
        u = jnp.dot(xb, wub[...], preferred_element_type=F32)
        hid = (_silu(a) * u).astype(BF16)
        y = jnp.dot(hid, wdb[...], preferred_element_type=F32)

        @pl.when(g >= no)
        def _():
            out_copy(g - no).wait()

        _store_token_rows(ybuf.at[g % no], y)
        out_copy(g).start()
        return carry

    first = plan_ref[PLAN_FIRST_TILE, e]
    lax.fori_loop(first, first + n_mine, tile, 0)

    @pl.when(e == pl.num_programs(0) - 1)
    def _():
        for back in range(no, 0, -1):
            @pl.when(n_used >= back)
            def _():
                out_copy(n_used - back).wait()

        ybuf[0] = jnp.zeros((rows, LANES), F32)

        def zero_copy(g):
            return pltpu.make_async_copy(ybuf.at[0], tile_rows(y_hbm, g), osem.at[0])

        def fill(g, carry):
            zero_copy(g).start()
            return carry

        def drain(g, carry):
            zero_copy(g).wait()
            return carry

        lax.fori_loop(n_used, n_tiles, fill, 0)
        lax.fori_loop(n_used, n_tiles, drain, 0)


def _experts(plan, xs, wg, wu, wd):
    tm = EXPERT_TILE
    n_exp, d, de = wg.shape
    grid_spec = pltpu.PrefetchScalarGridSpec(
        num_scalar_prefetch=1,
        grid=(n_exp,),
        in_specs=[
            pl.BlockSpec(memory_space=pl.ANY),
            pl.BlockSpec((1, d, de), lambda e, *_: (e, 0, 0)),
            pl.BlockSpec((1, d, de), lambda e, *_: (e, 0, 0)),
            pl.BlockSpec((1, de, d), lambda e, *_: (e, 0, 0)),
        ],
        out_specs=pl.BlockSpec(memory_space=pl.ANY),
        scratch_shapes=[pltpu.VMEM((EXPERT_IN_SLOTS, tm * SUBLANES, LANES), F32),
                        pltpu.VMEM((EXPERT_OUT_SLOTS, tm * SUBLANES, LANES), F32),
                        pltpu.VMEM((d, de), BF16),
                        pltpu.VMEM((d, de), BF16),
                        pltpu.VMEM((de, d), BF16),
                        pltpu.SemaphoreType.DMA((EXPERT_IN_SLOTS,)),
                        pltpu.SemaphoreType.DMA((EXPERT_OUT_SLOTS,))],
    )
    return pl.pallas_call(
        _expert_kernel,
        out_shape=jax.ShapeDtypeStruct(xs.shape, F32),
        grid_spec=grid_spec,
        compiler_params=pltpu.CompilerParams(
            dimension_semantics=("arbitrary",),
            vmem_limit_bytes=VMEM_LIMIT_BYTES),
        name="experts",
    )(plan, xs, wg, wu, wd)


COMBINE_SLOTS = 3


def _combine_kernel(p0_ref, p1_ref, y_hbm, x1_ref, slab_ref, mod_ref, gfin_ref, o_ref,
                    *scratch):
    tm = COMBINE_TILE
    ns = COMBINE_SLOTS
    ahead = ns - 1
    ybufs, sem = scratch[:ns], scratch[ns]
    i = pl.program_id(0)
    nt = pl.num_programs(0)

    def row_copies(tile, slot, r):
        for j, p_ref in enumerate((p0_ref, p1_ref)):
            pltpu.make_async_copy(_token_rows(y_hbm, p_ref[tile * tm + r]),
                                  _token_rows(ybufs[slot].at[j], r),
                                  sem.at[slot]).start(priority=j)

    def wait_tile(slot):
        for j in range(2):
            pltpu.make_async_copy(y_hbm.at[pl.ds(0, tm * SUBLANES), :], ybufs[slot].at[j],
                                  sem.at[slot]).wait()

    @pl.when(i == 0)
    def _():
        for t0 in range(ahead):
            @pl.when(t0 < nt)
            def _():
                def body(r, carry):
                    row_copies(t0, t0, r)
                    return carry
                lax.fori_loop(0, tm, body, 0, unroll=ROW_DMA_UNROLL)

    n_chunks = SUBLANES
    batch = tm // n_chunks

    def step(slot):
        wait_tile(slot)
        nxt = jnp.minimum(i + ahead, nt - 1)
        nslot = (slot + ahead) % ns
        slab = slab_ref[...]
        w0 = slab[:, 2:3]
        w1 = slab[:, 3:4]
        gate_f = mod_ref[0][5:6]
        sq = jnp.zeros((tm, LANES), F32)
        for c in range(n_chunks):
            lanes = slice(c * LANES, (c + 1) * LANES)
            y0 = ybufs[slot].at[0][pl.ds(c, tm, stride=SUBLANES), :]
            y1 = ybufs[slot].at[1][pl.ds(c, tm, stride=SUBLANES), :]
            xo = x1_ref[:, lanes] + gate_f[:, lanes] * (w0 * y0 + w1 * y1)
            sq = sq + xo * xo
            o_ref[:, lanes] = xo
            for r in range(c * batch, (c + 1) * batch):
                row_copies(nxt, nslot, r)
        ms = jnp.sum(sq, axis=-1, keepdims=True) * (1.0 / (n_chunks * LANES))
        scale = lax.rsqrt(ms + EPS)
        for c in range(n_chunks):
            lanes = slice(c * LANES, (c + 1) * LANES)
            o_ref[:, lanes] = o_ref[:, lanes] * scale * gfin_ref[:, lanes]

    for slot in range(ns):
        @pl.when(i % ns == slot)
        def _():
            step(slot)

    @pl.when(i == nt - 1)
    def _():
        for k in range(ahead):
            for slot in range(ns):
                @pl.when((nt - 1 - k >= 0) & ((nt - 1 - k + ahead) % ns == slot))
                def _():
                    wait_tile(slot)


def _combine(p0, p1, y, x1, slab, mod, gfin, seq):
    n, d = x1.shape
    tm = COMBINE_TILE
    tiles_per_seq = seq // tm
    grid_spec = pltpu.PrefetchScalarGridSpec(
        num_scalar_prefetch=2,
        grid=(n // tm,),
        in_specs=[
            pl.BlockSpec(memory_space=pl.ANY),
            pl.BlockSpec((tm, d), lambda i, a, b: (i, 0)),
            pl.BlockSpec((tm, ROUTER_LANES), lambda i, a, b: (i, 0)),
            pl.BlockSpec((1, 6, d), lambda i, a, b: (i // tiles_per_seq, 0, 0)),
            pl.BlockSpec((1, d), lambda i, a, b: (0, 0)),
        ],
        out_specs=pl.BlockSpec((tm, d), lambda i, a, b: (i, 0)),
        scratch_shapes=([pltpu.VMEM((2, tm * SUBLANES, LANES), F32)] * COMBINE_SLOTS
                        + [pltpu.SemaphoreType.DMA((COMBINE_SLOTS,))]),
    )
    return pl.pallas_call(
        _combine_kernel,
        out_shape=jax.ShapeDtypeStruct((n, d), F32),
        grid_spec=grid_spec,
        compiler_params=pltpu.CompilerParams(
            dimension_semantics=("arbitrary",),
            vmem_limit_bytes=VMEM_LIMIT_BYTES),
        name="combine",
    )(p0, p1, y, x1, slab, mod, gfin)


def kernel(x, c, positions, ada_w, ada_b, norm_mix_g, norm_ffn_g, w_in, conv_w, conv_b,
           beta_ret, beta_conv, w_out, router_group_w, router_group_b, router_expert_w,
           router_expert_b, expert_w_gate, expert_w_up, expert_w_down, norm_final_g):
    bsz, seq, d = x.shape
    n = bsz * seq
    depth = ada_w.shape[0]
    assert depth == 1, "the combine kernel fuses the trunk's final RMSNorm (single layer)"
    W = RET_HEADS * HEAD_DIM

    cos128, sin128 = _rope_tables(positions)
    heads = jnp.arange(RET_HEADS, dtype=F32)
    lg = jnp.log1p(-jnp.exp2(-5.0 - heads))
    lgl = jnp.repeat(lg, HEAD_DIM).reshape(1, W)
    assert CONV_GROUP_DIM == HEAD_DIM, "conv groups and retention heads share the 64-lane block sums"
    blk_np = np.kron(np.eye(RET_HEADS // 2, dtype=np.float32),
                     np.ones((HEAD_DIM, HEAD_DIM), np.float32))
    blk = jnp.asarray(blk_np, dtype=BF16)

    for l in range(depth):
        mod = _adaln(c, ada_w[l], ada_b[l]).reshape(bsz, 6, d)

        gap = ROUTER_EXPERT_ROW0 - N_GROUPS
        tail = ROUTER_LANES - ROUTER_ROWS
        wr = jnp.concatenate([router_group_w[l].T, jnp.zeros((gap, d), F32),
                              router_expert_w[l].T, jnp.zeros((tail, d), F32)], axis=0)
        wr_hi = wr.astype(BF16)
        wr_lo = (wr - wr_hi.astype(F32)).astype(BF16)
        wr2 = jnp.concatenate([wr_hi, wr_lo], axis=0)
        br = jnp.concatenate([router_group_b[l], jnp.zeros((gap,), F32),
                              router_expert_b[l], jnp.zeros((tail,), F32)]).reshape(ROUTER_LANES, 1)

        x1, h2p, logits_t = _mixer(
            x, mod, cos128, sin128, norm_mix_g[l].reshape(1, d), w_in[l].astype(BF16),
            conv_w[l], conv_b[l].reshape(1, W), beta_ret[l].reshape(1, W),
            beta_conv[l].reshape(1, W), w_out[l].astype(BF16), norm_ffn_g[l].reshape(1, d),
            wr2, br, lg, lgl, blk)
        slab, dest, plan = _router(logits_t)
        p0, p1 = dest[0], dest[1]
        p_rows = 2 * n + N_EXPERTS * EXPERT_TILE

        xs = _dispatch(p0, p1, plan, h2p, p_rows)
        de = expert_w_gate.shape[-1]
        y = _experts(plan, xs,
                     expert_w_gate[l].reshape(N_EXPERTS, d, de),
                     expert_w_up[l].reshape(N_EXPERTS, d, de),
                     expert_w_down[l].reshape(N_EXPERTS, de, d))
        out = _combine(p0, p1, y, x1.reshape(n, d), slab, mod, norm_final_g.reshape(1, d), seq)
        x = out.reshape(bsz, seq, d)
    return x
```

```python
import jax
import jax.numpy as jnp
import numpy as np
from jax import lax
from jax.experimental import pallas as pl
from jax.experimental.pallas import tpu as pltpu

F32 = jnp.float32
BF16 = jnp.bfloat16

CHUNK = 64
RET_HEADS = 8
HEAD_DIM = 64
CONV_GROUP_DIM = 64
ROPE_BASE = 10000.0
N_GROUPS = 4
EXPERTS_PER_GROUP = 8
N_EXPERTS = N_GROUPS * EXPERTS_PER_GROUP
EPS = 1e-6
GN_EPS = 1e-5

LANES = 128
SUBLANES = 8
VMEM_LIMIT_BYTES = 56 * 1024 * 1024

SEQ_TILE = 256
EXPERT_TILE = 256
COMBINE_TILE = 256
ROUTER_LANES = LANES
ROUTER_GROUP_ROW0 = 0
ROUTER_EXPERT_ROW0 = SUBLANES
ROUTER_ROWS = ROUTER_EXPERT_ROW0 + N_EXPERTS


def _silu(v):
    return v * (1.0 / (1.0 + jnp.exp(-v)))


def _adaln_kernel(c_ref, w_ref, b_ref, o_ref):
    s = _silu(c_ref[...])
    o_ref[...] = jnp.dot(s, w_ref[...], precision=lax.Precision.HIGHEST,
                         preferred_element_type=F32) + b_ref[...]


def _adaln(c, w, b):
    bsz, d = c.shape
    n = w.shape[1]
    tn = 1024
    return pl.pallas_call(
        _adaln_kernel,
        out_shape=jax.ShapeDtypeStruct((bsz, n), F32),
        grid=(n // tn,),
        in_specs=[pl.BlockSpec((bsz, d), lambda j: (0, 0)),
                  pl.BlockSpec((d, tn), lambda j: (0, j)),
                  pl.BlockSpec((1, tn), lambda j: (0, j))],
        out_specs=pl.BlockSpec((bsz, tn), lambda j: (0, j)),
        name="adaln",
    )(c, w, b.reshape(1, n))


ROPE_ROWS = 1024


def _rope_kernel(pos_ref, invf_ref, cos_ref, sin_ref):
    half = HEAD_DIM // 2
    per_row = LANES // half
    r = ROPE_ROWS
    ang = pos_ref[...].astype(F32) * invf_ref[...]
    lane = lax.broadcasted_iota(jnp.int32, (r, LANES), 1)
    quarter = lane // half
    sign = jnp.where(quarter % 2 == 0, -1.0, 1.0)
    for table, out_ref, scale in ((jnp.cos(ang), cos_ref, None), (jnp.sin(ang), sin_ref, sign)):
        rolled = [table] + [pltpu.roll(table, half * k, 1) for k in range(1, per_row)]
        for q in range(per_row):
            val = rolled[(0 - q) % per_row]
            for k in range(1, per_row):
                val = jnp.where(quarter == k, rolled[(k - q) % per_row], val)
            if scale is not None:
                val = val * scale
            out_ref[pl.ds(q, r, stride=per_row), :] = val


def _rope_tables(positions):
    n = positions.size
    half = HEAD_DIM // 2
    inv_freq = ROPE_BASE ** (-jnp.arange(0, HEAD_DIM, 2, dtype=F32) / HEAD_DIM)
    per_row = LANES // half
    rows = n // per_row
    pos_rep = jnp.broadcast_to(positions.reshape(n, 1), (n, half)).reshape(rows, LANES)
    invf = jnp.tile(inv_freq, per_row).reshape(1, LANES)
    tr = ROPE_ROWS
    return pl.pallas_call(
        _rope_kernel,
        out_shape=(jax.ShapeDtypeStruct((n, LANES), F32),) * 2,
        grid=(rows // tr,),
        in_specs=[pl.BlockSpec((tr, LANES), lambda i: (i, 0)),
                  pl.BlockSpec((1, LANES), lambda i: (0, 0))],
        out_specs=(pl.BlockSpec((tr * per_row, LANES), lambda i: (i, 0)),) * 2,
        name="rope_table",
    )(pos_rep, invf)


def _load_token_rows(ref, rows):
    return jnp.concatenate(
        [ref[pl.ds(c, rows, stride=SUBLANES), :] for c in range(SUBLANES)], axis=1)


def _store_token_rows(ref, val, row0=0):
    rows = val.shape[0]
    for c in range(SUBLANES):
        ref[pl.ds(row0 * SUBLANES + c, rows, stride=SUBLANES), :] = val[:, c * LANES:(c + 1) * LANES]


def _token_rows(ref, row):
    return ref.at[pl.ds(pl.multiple_of(row * SUBLANES, SUBLANES), SUBLANES), :]


MIXER_SUBTILES = 2


MIXER_WIN, MIXER_WOUT = 6, 11


def _mixer_kernel(*refs):
    *refs, winb_ref, woutb_ref = refs
    state_ref, ubuf_ref = refs[-2:]

    @pl.when((pl.program_id(0) == 0) & (pl.program_id(1) == 0))
    def _():
        for src, dst in ((refs[MIXER_WIN], winb_ref), (refs[MIXER_WOUT], woutb_ref)):
            step = RET_HEADS * HEAD_DIM
            for c0 in range(0, src.shape[1], step):
                dst[:, c0:c0 + step] = src[:, c0:c0 + step].astype(BF16)

    @pl.when(pl.program_id(1) == 0)
    def _():
        state_ref[...] = jnp.zeros_like(state_ref)
        ubuf_ref[0:SUBLANES, :] = jnp.zeros((SUBLANES, ubuf_ref.shape[1]), F32)

    refs[MIXER_WIN], refs[MIXER_WOUT] = winb_ref, woutb_ref
    for sub in range(MIXER_SUBTILES):
        _mixer_tile(sub, *refs)


def _mixer_tile(sub, lg_ref, x_ref, mod_ref, cos_ref, sin_ref, gmix_ref, win_ref, convw_ref,
                convb_ref, bret_ref, bconv_ref, wout_ref, gffn_ref, wr_ref, br_ref, lgl_ref,
                blk_ref, x1_ref, h2_ref, logit_ref, state_ref, ubuf_ref):
    L = SEQ_TILE
    W = RET_HEADS * HEAD_DIM
    tile_rows = slice(sub * L, (sub + 1) * L)

    x = x_ref[0, tile_rows, :]
    mod = mod_ref[0]
    shift_m, scale_m, gate_m = mod[0:1], mod[1:2], mod[2:3]
    shift_f, scale_f = mod[3:4], mod[4:5]

    ms = jnp.mean(x * x, axis=-1, keepdims=True)
    h = x * lax.rsqrt(ms + EPS) * gmix_ref[...]
    h = h * (1.0 + scale_m) + shift_m
    hb = h.astype(BF16)

    def proj(i):
        return jnp.dot(hb, win_ref[:, i * W:(i + 1) * W], preferred_element_type=F32)

    cos = jnp.concatenate([cos_ref[tile_rows, :]] * 4, axis=1)
    sin = jnp.concatenate([sin_ref[tile_rows, :]] * 4, axis=1)
    lane_w = lax.broadcasted_iota(jnp.int32, (L, W), 1)
    first_half = (lane_w & (HEAD_DIM - 1)) < (HEAD_DIM // 2)

    def rot(t):
        partner = jnp.where(first_half, pltpu.roll(t, W - HEAD_DIM // 2, 1),
                            pltpu.roll(t, HEAD_DIM // 2, 1))
        return t * cos + partner * sin

    q = rot(proj(0))
    k = rot(proj(1)) * (HEAD_DIM ** -0.5)
    v = proj(2)
    vb = v.astype(BF16)
    kb = k.astype(BF16)

    lgl = lgl_ref[...]
    rowf = lax.broadcasted_iota(jnp.int32, (L, W), 0).astype(F32)
    qd = q * jnp.exp(lgl * (rowf + 1.0))
    kd = k * jnp.exp(lgl * (float(L - 1) - rowf))
    blk = blk_ref[...]
    HW = W // 2
    blk_f = blk.astype(F32)
    qdb = qd.astype(BF16)
    kdb = kd.astype(BF16)
    state_decay = jnp.exp(lgl * float(L))
    inter = []
    for hf in range(2):
        sl = slice(hf * HW, (hf + 1) * HW)
        st = state_ref[hf]
        inter.append(jnp.dot(qdb[:, sl], st.astype(BF16), preferred_element_type=F32))
        kv = lax.dot_general(kdb[:, sl], vb[:, sl], (((0,), (0,)), ((), ())),
                             preferred_element_type=F32)
        state_ref[hf] = st * state_decay[:, sl] + kv * blk_f
    y_inter = jnp.concatenate(inter, axis=1)

    def head_sums(t):
        tb = t.astype(BF16)
        return jnp.concatenate(
            [jnp.dot(tb[:, hf * HW:(hf + 1) * HW], blk, preferred_element_type=F32)
             for hf in range(2)], axis=1)

    ii = lax.broadcasted_iota(jnp.int32, (L, L), 0)
    jj = lax.broadcasted_iota(jnp.int32, (L, L), 1)
    dist = jnp.abs(ii - jj).astype(F32)
    allowed = (jj // CHUNK) <= (ii // CHUNK)
    lane_p = lax.broadcasted_iota(jnp.int32, (L, LANES), 1)
    lo_head = lane_p < HEAD_DIM
    pairs = []
    for p in range(RET_HEADS // 2):
        sl = slice(p * LANES, (p + 1) * LANES)
        qp, kp, vp = q[:, sl], kb[:, sl], vb[:, sl]
        ys = []
        for hh in range(2):
            head = 2 * p + hh
            keep = lo_head if hh == 0 else jnp.logical_not(lo_head)
            qh = jnp.where(keep, qp, 0.0).astype(BF16)
            sc = lax.dot_general(qh, kp, (((1,), (1,)), ((), ())),
                                 preferred_element_type=F32)
            decay = jnp.where(allowed, jnp.exp(lg_ref[head] * dist), 0.0)
            ys.append(jnp.dot((sc * decay).astype(BF16), vp, preferred_element_type=F32))
        pairs.append(jnp.where(lo_head, ys[0], ys[1]))
    y = jnp.concatenate(pairs, axis=1) + y_inter

    inv_hd = 1.0 / HEAD_DIM
    mu = head_sums(y) * inv_hd
    d = y - mu
    var = head_sums(d * d) * inv_hd
    g = proj(3)
    y_ret = _silu(g) * (d * lax.rsqrt(var + GN_EPS)) * bret_ref[...]

    b_gate = proj(4)
    u = proj(5) * proj(6)
    ubuf_ref[SUBLANES:SUBLANES + L, :] = u
    u1 = ubuf_ref[SUBLANES - 1:SUBLANES - 1 + L, :]
    u2 = ubuf_ref[SUBLANES - 2:SUBLANES - 2 + L, :]
    ubuf_ref[0:SUBLANES, :] = ubuf_ref[L:L + SUBLANES, :]
    cw = convw_ref[...]
    conv = u2 * cw[0:1] + u1 * cw[1:2] + u * cw[2:3] + convb_ref[...]
    yc = b_gate * conv
    msc = head_sums(yc * yc) * (1.0 / CONV_GROUP_DIM)
    y_conv = yc * lax.rsqrt(msc + EPS) * bconv_ref[...]

    mix = (jnp.dot(y_ret.astype(BF16), wout_ref[0:W, :], preferred_element_type=F32)
           + jnp.dot(y_conv.astype(BF16), wout_ref[W:2 * W, :], preferred_element_type=F32))
    x1 = x + gate_m * mix
    x1_ref[0, tile_rows, :] = x1

    ms2 = jnp.mean(x1 * x1, axis=-1, keepdims=True)
    h2 = x1 * lax.rsqrt(ms2 + EPS) * gffn_ref[...]
    h2 = h2 * (1.0 + scale_f) + shift_f
    _store_token_rows(h2_ref, h2, row0=sub * L)

    hi = h2.astype(BF16)
    lo = (h2 - hi.astype(F32)).astype(BF16)
    w2 = wr_ref[...]
    nt_dims = (((1,), (1,)), ((), ()))
    parts = (lax.dot_general(w2, hi, nt_dims, preferred_element_type=F32)
             + lax.dot_general(w2, lo, nt_dims, preferred_element_type=F32))
    logits_t = parts[:ROUTER_LANES] + parts[ROUTER_LANES:] + br_ref[...]
    logit_ref[:, tile_rows] = logits_t[:ROUTER_ROWS]


def _mixer(x, mod, cos128, sin128, gmix, win, convw, convb, bret, bconv, wout, gffn,
           wr2, br, lg, lgl, blk):
    bsz, seq, d = x.shape
    L = SEQ_TILE * MIXER_SUBTILES
    ns = seq // L
    W = RET_HEADS * HEAD_DIM
    n = bsz * seq
    const2 = lambda b, s: (0, 0)
    in_specs = [
        pl.BlockSpec(memory_space=pltpu.SMEM),
        pl.BlockSpec((1, L, d), lambda b, s: (b, s, 0)),
        pl.BlockSpec((1, 6, d), lambda b, s: (b, 0, 0)),
        pl.BlockSpec((L, LANES), lambda b, s: (b * ns + s, 0)),
        pl.BlockSpec((L, LANES), lambda b, s: (b * ns + s, 0)),
        pl.BlockSpec((1, d), const2),
        pl.BlockSpec(win.shape, const2, pipeline_mode=pl.Buffered(1)),
        pl.BlockSpec(convw.shape, const2),
        pl.BlockSpec((1, W), const2),
        pl.BlockSpec((1, W), const2),
        pl.BlockSpec((1, W), const2),
        pl.BlockSpec(wout.shape, const2, pipeline_mode=pl.Buffered(1)),
        pl.BlockSpec((1, d), const2),
        pl.BlockSpec(wr2.shape, const2),
        pl.BlockSpec((ROUTER_LANES, 1), const2),
        pl.BlockSpec((1, W), const2),
        pl.BlockSpec((W // 2, W // 2), const2),
    ]
    assert d == SUBLANES * LANES, "one token must fill exactly one (8, 128) f32 tile"
    out_shape = (jax.ShapeDtypeStruct((bsz, seq, d), F32),
                 jax.ShapeDtypeStruct((n * SUBLANES, LANES), F32),
                 jax.ShapeDtypeStruct((ROUTER_ROWS, n), F32))
    out_specs = (pl.BlockSpec((1, L, d), lambda b, s: (b, s, 0)),
                 pl.BlockSpec((L * SUBLANES, LANES), lambda b, s: (b * ns + s, 0)),
                 pl.BlockSpec((ROUTER_ROWS, L), lambda b, s: (0, b * ns + s)))
    return pl.pallas_call(
        _mixer_kernel,
        out_shape=out_shape,
        grid=(bsz, ns),
        in_specs=in_specs,
        out_specs=out_specs,
        scratch_shapes=[pltpu.VMEM((2, W // 2, W // 2), F32),
                        pltpu.VMEM((SEQ_TILE + 2 * SUBLANES, W), F32),
                        pltpu.VMEM(win.shape, BF16),
                        pltpu.VMEM(wout.shape, BF16)],
        compiler_params=pltpu.CompilerParams(
            dimension_semantics=("arbitrary", "arbitrary"),
            vmem_limit_bytes=VMEM_LIMIT_BYTES),
        name="mixer",
    )(lg, x, mod, cos128, sin128, gmix, win, convw, convb, bret, bconv, wout, gffn,
      wr2, br, lgl, blk)


ROUTER_TILE = 2048
RES_E1, RES_E2, RES_W1, RES_W2, RES_A, RES_B = range(6)
PLAN_FIRST_TILE, PLAN_N_TILE, PLAN_PAD_START, PLAN_PAD_N, PLAN_N_USED = range(5)


def _rows8(vals, width):
    rid = lax.broadcasted_iota(jnp.int32, (SUBLANES, width), 0)
    out = jnp.zeros((SUBLANES, width), F32)
    for r, v in enumerate(vals):
        out = jnp.where(rid == r, v, out)
    return out


def _router_kernel(lt_ref, slab_ref, dest_ref, plan_ref, res_ref, cnt_ref):
    T = ROUTER_TILE
    phase = pl.program_id(0)
    j = pl.program_id(1)
    nblk = T // LANES
    big = F32(1e9)
    rid8 =lax.broadcasted_iota(jnp.int32, (SUBLANES, T), 0).astype(F32)
    rid_e = lax.broadcasted_iota(jnp.int32, (N_EXPERTS, LANES), 0).astype(F32)

    def onehot(e_row, k):
        return jnp.where(rid_e == e_row[:, k * LANES:(k + 1) * LANES], 1.0, 0.0)

    @pl.when((phase == 0) & (j == 0))
    def _():
        cnt_ref[...] = jnp.zeros_like(cnt_ref)

    @pl.when(phase == 0)
    def _():
        lt = lt_ref[...]
        g_rows = lt[ROUTER_GROUP_ROW0:ROUTER_GROUP_ROW0 + SUBLANES]
        gvalid = rid8 < float(N_GROUPS)
        gm = jnp.where(gvalid, g_rows, F32(-jnp.inf))
        gexp = jnp.exp(gm - jnp.max(gm, axis=0, keepdims=True))
        gp = gexp / jnp.sum(gexp, axis=0, keepdims=True)
        g_top = jnp.max(gp, axis=0, keepdims=True)
        g_idx = jnp.min(jnp.where(gvalid & (gp == g_top), rid8, big), axis=0, keepdims=True)

        def group_slab(g):
            r0 = ROUTER_EXPERT_ROW0 + g * EXPERTS_PER_GROUP
            return lt[r0:r0 + EXPERTS_PER_GROUP]
        sel = group_slab(N_GROUPS - 1)
        for g in range(N_GROUPS - 2, -1, -1):
            sel = jnp.where(g_idx == float(g), group_slab(g), sel)
        eexp = jnp.exp(sel - jnp.max(sel, axis=0, keepdims=True))
        ep = eexp / jnp.sum(eexp, axis=0, keepdims=True)
        p1 = jnp.max(ep, axis=0, keepdims=True)
        i1 = jnp.min(jnp.where(ep == p1, rid8, big), axis=0, keepdims=True)
        m2 = rid8 != i1
        p2 = jnp.max(jnp.where(m2, ep, -1.0), axis=0, keepdims=True)
        i2 = jnp.min(jnp.where(m2 & (ep == p2), rid8, big), axis=0, keepdims=True)
        den = p1 + p2
        w1 = p1 / den * g_top
        w2 = p2 / den * g_top
        e1 = g_idx * float(EXPERTS_PER_GROUP) + i1
        e2 = g_idx * float(EXPERTS_PER_GROUP) + i2

        ii = lax.broadcasted_iota(jnp.int32, (LANES, LANES), 0)
        jj = lax.broadcasted_iota(jnp.int32, (LANES, LANES), 1)
        upper = jnp.where(ii < jj, 1.0, 0.0).astype(BF16)
        base = cnt_ref[...]
        ranks1, ranks2 = [], []
        for k in range(nblk):
            o1, o2 = onehot(e1, k), onehot(e2, k)
            r1 = jnp.dot(o1.astype(BF16), upper, preferred_element_type=F32)
            r2 = jnp.dot(o2.astype(BF16), upper, preferred_element_type=F32)
            c1 = jnp.sum(o1, axis=1, keepdims=True)
            c2 = jnp.sum(o2, axis=1, keepdims=True)
            ranks1.append(jnp.sum(o1 * (base + r1), axis=0, keepdims=True))
            ranks2.append(jnp.sum(o2 * (base + c1 + r2), axis=0, keepdims=True))
            base = base + c1 + c2
        cnt_ref[...] = base
        rank1 = jnp.concatenate(ranks1, axis=1)
        rank2 = jnp.concatenate(ranks2, axis=1)
        res_ref[j] = _rows8([e1, e2, w1, w2, rank1, rank2], T)

    @pl.when(phase == 1)
    def _():
        tm = float(EXPERT_TILE)
        cnt = cnt_ref[...]
        tiles = jnp.floor((cnt + (tm - 1.0)) * (1.0 / tm))
        ei = lax.broadcasted_iota(jnp.int32, (N_EXPERTS, N_EXPERTS), 0)
        ej = lax.broadcasted_iota(jnp.int32, (N_EXPERTS, N_EXPERTS), 1)
        lower = jnp.where(ej < ei, 1.0, 0.0).astype(BF16)
        first = jnp.dot(lower, tiles.astype(BF16), preferred_element_type=F32)
        starts = first * tm

        res = res_ref[j]
        e1, e2 = res[RES_E1:RES_E1 + 1], res[RES_E2:RES_E2 + 1]
        d1, d2 = [], []
        for k in range(nblk):
            blk_lanes = slice(k * LANES, (k + 1) * LANES)
            d1.append(res[RES_A:RES_A + 1, blk_lanes]
                      + jnp.sum(onehot(e1, k) * starts, axis=0, keepdims=True))
            d2.append(res[RES_B:RES_B + 1, blk_lanes]
                      + jnp.sum(onehot(e2, k) * starts, axis=0, keepdims=True))
        dest1 = jnp.concatenate(d1, axis=1)
        dest2 = jnp.concatenate(d2, axis=1)
        dest_ref[...] = _rows8([dest1, dest2], T).astype(jnp.int32)

        table = jnp.concatenate(
            [_rows8([e1, e2, res[RES_W1:RES_W1 + 1], res[RES_W2:RES_W2 + 1], dest1, dest2], T),
             jnp.zeros((ROUTER_LANES - SUBLANES, T), F32)], axis=0)
        slab_ref[...] = table.T

        @pl.when(j == 0)
        def _():
            lane_e = lax.broadcasted_iota(jnp.int32, (N_EXPERTS, LANES), 1).astype(F32)

            def as_row(col):
                return jnp.sum(jnp.where(rid_e == lane_e, col, 0.0), axis=0, keepdims=True)
            n_used = jnp.sum(tiles, axis=0, keepdims=True)
            padded = tiles * tm
            plan_ref[...] = _rows8([as_row(first), as_row(tiles), as_row(starts + cnt),
                                    as_row(padded - cnt), n_used], LANES).astype(jnp.int32)


def _router(logits_t):
    n = logits_t.shape[1]
    T = ROUTER_TILE
    nt = n // T
    return pl.pallas_call(
        _router_kernel,
        out_shape=(jax.ShapeDtypeStruct((n, ROUTER_LANES), F32),
                   jax.ShapeDtypeStruct((SUBLANES, n), jnp.int32),
                   jax.ShapeDtypeStruct((SUBLANES, LANES), jnp.int32)),
        grid=(2, nt),
        in_specs=[pl.BlockSpec((ROUTER_ROWS, T), lambda p, j: (0, j * (1 - p) + (nt - 1) * p))],
        out_specs=(pl.BlockSpec((T, ROUTER_LANES), lambda p, j: (j * p, 0)),
                   pl.BlockSpec((SUBLANES, T), lambda p, j: (0, j * p)),
                   pl.BlockSpec((SUBLANES, LANES), lambda p, j: (0, 0))),
        scratch_shapes=[pltpu.VMEM((nt, SUBLANES, T), F32),
                        pltpu.VMEM((N_EXPERTS, LANES), F32)],
        compiler_params=pltpu.CompilerParams(
            dimension_semantics=("arbitrary", "arbitrary"), vmem_limit_bytes=VMEM_LIMIT_BYTES),
        name="router",
    )(logits_t)


DISPATCH_TILE = 512
ROW_DMA_UNROLL = 8
PAD_UNITS = tuple(1 << b for b in reversed(range(EXPERT_TILE.bit_length() - 1)))


DISPATCH_LAG = 2
DISPATCH_SLOTS = DISPATCH_LAG + 2


def _dispatch_kernel(d0_ref, d1_ref, plan_ref, h2_hbm, xs_hbm,
                     stage, zbuf, isem, ssem, zsem):
    dt = DISPATCH_TILE
    ps = SUBLANES
    i = pl.program_id(0)
    nsteps = pl.num_programs(0)
    zrows = PAD_UNITS[0]

    def in_copy(blk, slot):
        src = h2_hbm.at[pl.ds(pl.multiple_of(blk * (dt * ps), dt * ps), dt * ps), :]
        return pltpu.make_async_copy(src, stage.at[slot], isem.at[slot])

    def wait_rows(slot):
        for _ in range(2):
            pltpu.make_async_copy(stage.at[slot], xs_hbm.at[pl.ds(0, dt * ps), :],
                                  ssem.at[slot]).wait()

    def pad_copy(start, unit):
        return pltpu.make_async_copy(zbuf.at[pl.ds(0, unit * ps), :],
                                     xs_hbm.at[pl.ds(pl.multiple_of(start * ps, ps), unit * ps), :],
                                     zsem)

    def pad_pass(do):
        def per_expert(e, carry):
            start = plan_ref[PLAN_PAD_START, e]
            npad = plan_ref[PLAN_PAD_N, e]
            for unit in PAD_UNITS:
                @pl.when((npad & unit) != 0)
                def _():
                    do(pad_copy(start + (npad & ~(2 * unit - 1)), unit))
            return carry
        lax.fori_loop(0, N_EXPERTS, per_expert, 0)

    def tail_pass(do):
        def per_unit(k, carry):
            do(pad_copy(k * zrows, zrows))
            return carry
        per_tile = EXPERT_TILE // zrows
        n_units = xs_hbm.shape[0] // (zrows * ps)
        lax.fori_loop(plan_ref[PLAN_N_USED, 0] * per_tile, n_units, per_unit, 0)

    slot = i % DISPATCH_SLOTS

    @pl.when(i == 0)
    def _():
        in_copy(0, 0).start()

        @pl.when(nsteps > 1)
        def _():
            in_copy(1, 1).start()

        zbuf[...] = jnp.zeros_like(zbuf)
        pad_pass(lambda cp: cp.start())
        tail_pass(lambda cp: cp.start())

    @pl.when(i >= DISPATCH_LAG)
    def _():
        wait_rows((i - DISPATCH_LAG) % DISPATCH_SLOTS)

    @pl.when(i + 2 < nsteps)
    def _():
        in_copy(i + 2, (i + 2) % DISPATCH_SLOTS).start()

    in_copy(i, slot).wait()
    base = i * dt
    src_ref = stage.at[slot]

    def body(r, carry):
        src = _token_rows(src_ref, r)
        for prio, d_ref in enumerate((d0_ref, d1_ref)):
            pltpu.make_async_copy(src, _token_rows(xs_hbm, d_ref[base + r]),
                                  ssem.at[slot]).start(priority=prio)
        return carry

    lax.fori_loop(0, dt, body, 0, unroll=ROW_DMA_UNROLL)

    @pl.when(i == nsteps - 1)
    def _():
        for back in range(DISPATCH_LAG - 1, -1, -1):
            @pl.when(i - back >= 0)
            def _():
                wait_rows((i - back) % DISPATCH_SLOTS)
        pad_pass(lambda cp: cp.wait())
        tail_pass(lambda cp: cp.wait())


def _dispatch(dest0, dest1, plan, h2t, p_rows):
    n = dest0.shape[0]
    dt = DISPATCH_TILE
    grid_spec = pltpu.PrefetchScalarGridSpec(
        num_scalar_prefetch=3,
        grid=(n // dt,),
        in_specs=[pl.BlockSpec(memory_space=pl.ANY)],
        out_specs=pl.BlockSpec(memory_space=pl.ANY),
        scratch_shapes=[pltpu.VMEM((DISPATCH_SLOTS, dt * SUBLANES, LANES), F32),
                        pltpu.VMEM((PAD_UNITS[0] * SUBLANES, LANES), F32),
                        pltpu.SemaphoreType.DMA((DISPATCH_SLOTS,)),
                        pltpu.SemaphoreType.DMA((DISPATCH_SLOTS,)),
                        pltpu.SemaphoreType.DMA(())],
    )
    return pl.pallas_call(
        _dispatch_kernel,
        out_shape=jax.ShapeDtypeStruct((p_rows * SUBLANES, LANES), F32),
        grid_spec=grid_spec,
        compiler_params=pltpu.CompilerParams(dimension_semantics=("arbitrary",)),
        name="dispatch",
    )(dest0, dest1, plan, h2t)


EXPERT_IN_SLOTS = 4
EXPERT_OUT_SLOTS = 3


def _expert_kernel(plan_ref, xs_hbm, wg_ref, wu_ref, wd_ref, y_hbm,
                   xbuf, ybuf, wgb, wub, wdb, isem, osem):
    tm = EXPERT_TILE
    rows = tm * SUBLANES
    ni, no = EXPERT_IN_SLOTS, EXPERT_OUT_SLOTS
    e = pl.program_id(0)
    n_used = plan_ref[PLAN_N_USED, 0]
    n_mine = plan_ref[PLAN_N_TILE, e]
    n_tiles = y_hbm.shape[0] // rows

    def tile_rows(ref, g):
        return ref.at[pl.ds(pl.multiple_of(g * rows, rows), rows), :]

    def in_copy(g):
        return pltpu.make_async_copy(tile_rows(xs_hbm, g), xbuf.at[g % ni], isem.at[g % ni])

    def out_copy(g):
        return pltpu.make_async_copy(ybuf.at[g % no], tile_rows(y_hbm, g), osem.at[g % no])

    @pl.when(e == 0)
    def _():
        for g0 in range(ni - 1):
            @pl.when(g0 < n_used)
            def _():
                in_copy(g0).start()

    @pl.when(n_mine > 0)
    def _():
        wgb[...] = wg_ref[0].astype(BF16)
        wub[...] = wu_ref[0].astype(BF16)
        wdb[...] = wd_ref[0].astype(BF16)

    def tile(g, carry):
        in_copy(g).wait()

        @pl.when(g + ni - 1 < n_used)
        def _():
            in_copy(g + ni - 1).start()

        xb = _load_token_rows(xbuf.at[g % ni], tm).astype(BF16)
        a = jnp.dot(xb, wgb[...], preferred_element_type=F32)
        u = jnp.dot(xb, wub[...], preferred_element_type=F32)
        hid = (_silu(a) * u).astype(BF16)
        y = jnp.dot(hid, wdb[...], preferred_element_type=F32)

        @pl.when(g >= no)
        def _():
            out_copy(g - no).wait()

        _store_token_rows(ybuf.at[g % no], y)
        out_copy(g).start()
        return carry

    first = plan_ref[PLAN_FIRST_TILE, e]
    lax.fori_loop(first, first + n_mine, tile, 0)

    @pl.when(e == pl.num_programs(0) - 1)
    def _():
        for back in range(no, 0, -1):
            @pl.when(n_used >= back)
            def _():
                out_copy(n_used - back).wait()

        ybuf[0] = jnp.zeros((rows, LANES), F32)

        def zero_copy(g):
            return pltpu.make_async_copy(ybuf.at[0], tile_rows(y_hbm, g), osem.at[0])

        def fill(g, carry):
            zero_copy(g).start()
            return carry

        def drain(g, carry):
            zero_copy(g).wait()
            return carry

        lax.fori_loop(n_used, n_tiles, fill, 0)
        lax.fori_loop(n_used, n_tiles, drain, 0)


def _experts(plan, xs, wg, wu, wd):
    tm = EXPERT_TILE
    n_exp, d, de = wg.shape
    grid_spec = pltpu.PrefetchScalarGridSpec(
        num_scalar_prefetch=1,
        grid=(n_exp,),
        in_specs=[
            pl.BlockSpec(memory_space=pl.ANY),
            pl.BlockSpec((1, d, de), lambda e, *_: (e, 0, 0)),
            pl.BlockSpec((1, d, de), lambda e, *_: (e, 0, 0)),
            pl.BlockSpec((1, de, d), lambda e, *_: (e, 0, 0)),
        ],
        out_specs=pl.BlockSpec(memory_space=pl.ANY),
        scratch_shapes=[pltpu.VMEM((EXPERT_IN_SLOTS, tm * SUBLANES, LANES), F32),
                        pltpu.VMEM((EXPERT_OUT_SLOTS, tm * SUBLANES, LANES), F32),
                        pltpu.VMEM((d, de), BF16),
                        pltpu.VMEM((d, de), BF16),
                        pltpu.VMEM((de, d), BF16),
                        pltpu.SemaphoreType.DMA((EXPERT_IN_SLOTS,)),
                        pltpu.SemaphoreType.DMA((EXPERT_OUT_SLOTS,))],
    )
    return pl.pallas_call(
        _expert_kernel,
        out_shape=jax.ShapeDtypeStruct(xs.shape, F32),
        grid_spec=grid_spec,
        compiler_params=pltpu.CompilerParams(
            dimension_semantics=("arbitrary",),
            vmem_limit_bytes=VMEM_LIMIT_BYTES),
        name="experts",
    )(plan, xs, wg, wu, wd)


COMBINE_SLOTS = 3


def _combine_kernel(p0_ref, p1_ref, y_hbm, x1_ref, slab_ref, mod_ref, gfin_ref, o_ref,
                    *scratch):
    tm = COMBINE_TILE
    ns = COMBINE_SLOTS
    ahead = ns - 1
    ybufs, sem = scratch[:ns], scratch[ns]
    i = pl.program_id(0)
    nt = pl.num_programs(0)

    def row_copies(tile, slot, r):
        for j, p_ref in enumerate((p0_ref, p1_ref)):
            pltpu.make_async_copy(_token_rows(y_hbm, p_ref[tile * tm + r]),
                                  _token_rows(ybufs[slot].at[j], r),
                                  sem.at[slot]).start(priority=j)

    def wait_tile(slot):
        for j in range(2):
            pltpu.make_async_copy(y_hbm.at[pl.ds(0, tm * SUBLANES), :], ybufs[slot].at[j],
                                  sem.at[slot]).wait()

    @pl.when(i == 0)
    def _():
        for t0 in range(ahead):
            @pl.when(t0 < nt)
            def _():
                def body(r, carry):
                    row_copies(t0, t0, r)
                    return carry
                lax.fori_loop(0, tm, body, 0, unroll=ROW_DMA_UNROLL)

    n_chunks = SUBLANES
    batch = tm // n_chunks

    def step(slot):
        wait_tile(slot)
        nxt = jnp.minimum(i + ahead, nt - 1)
        nslot = (slot + ahead) % ns
        slab = slab_ref[...]
        w0 = slab[:, 2:3]
        w1 = slab[:, 3:4]
        gate_f = mod_ref[0][5:6]
        sq = jnp.zeros((tm, LANES), F32)
        for c in range(n_chunks):
            lanes = slice(c * LANES, (c + 1) * LANES)
            y0 = ybufs[slot].at[0][pl.ds(c, tm, stride=SUBLANES), :]
            y1 = ybufs[slot].at[1][pl.ds(c, tm, stride=SUBLANES), :]
            xo = x1_ref[:, lanes] + gate_f[:, lanes] * (w0 * y0 + w1 * y1)
            sq = sq + xo * xo
            o_ref[:, lanes] = xo
            for r in range(c * batch, (c + 1) * batch):
                row_copies(nxt, nslot, r)
        ms = jnp.sum(sq, axis=-1, keepdims=True) * (1.0 / (n_chunks * LANES))
        scale = lax.rsqrt(ms + EPS)
        for c in range(n_chunks):
            lanes = slice(c * LANES, (c + 1) * LANES)
            o_ref[:, lanes] = o_ref[:, lanes] * scale * gfin_ref[:, lanes]

    for slot in range(ns):
        @pl.when(i % ns == slot)
        def _():
            step(slot)

    @pl.when(i == nt - 1)
    def _():
        for k in range(ahead):
            for slot in range(ns):
                @pl.when((nt - 1 - k >= 0) & ((nt - 1 - k + ahead) % ns == slot))
                def _():
                    wait_tile(slot)


def _combine(p0, p1, y, x1, slab, mod, gfin, seq):
    n, d = x1.shape
    tm = COMBINE_TILE
    tiles_per_seq = seq // tm
    grid_spec = pltpu.PrefetchScalarGridSpec(
        num_scalar_prefetch=2,
        grid=(n // tm,),
        in_specs=[
            pl.BlockSpec(memory_space=pl.ANY),
            pl.BlockSpec((tm, d), lambda i, a, b: (i, 0)),
            pl.BlockSpec((tm, ROUTER_LANES), lambda i, a, b: (i, 0)),
            pl.BlockSpec((1, 6, d), lambda i, a, b: (i // tiles_per_seq, 0, 0)),
            pl.BlockSpec((1, d), lambda i, a, b: (0, 0)),
        ],
        out_specs=pl.BlockSpec((tm, d), lambda i, a, b: (i, 0)),
        scratch_shapes=([pltpu.VMEM((2, tm * SUBLANES, LANES), F32)] * COMBINE_SLOTS
                        + [pltpu.SemaphoreType.DMA((COMBINE_SLOTS,))]),
    )
    return pl.pallas_call(
        _combine_kernel,
        out_shape=jax.ShapeDtypeStruct((n, d), F32),
        grid_spec=grid_spec,
        compiler_params=pltpu.CompilerParams(
            dimension_semantics=("arbitrary",),
            vmem_limit_bytes=VMEM_LIMIT_BYTES),
        name="combine",
    )(p0, p1, y, x1, slab, mod, gfin)


def kernel(x, c, positions, ada_w, ada_b, norm_mix_g, norm_ffn_g, w_in, conv_w, conv_b,
           beta_ret, beta_conv, w_out, router_group_w, router_group_b, router_expert_w,
           router_expert_b, expert_w_gate, expert_w_up, expert_w_down, norm_final_g):
    bsz, seq, d = x.shape
    n = bsz * seq
    depth = ada_w.shape[0]
    assert depth == 1, "the combine kernel fuses the trunk's final RMSNorm (single layer)"
    W = RET_HEADS * HEAD_DIM

    cos128, sin128 = _rope_tables(positions)
    heads = jnp.arange(RET_HEADS, dtype=F32)
    lg = jnp.log1p(-jnp.exp2(-5.0 - heads))
    lgl = jnp.repeat(lg, HEAD_DIM).reshape(1, W)
    assert CONV_GROUP_DIM == HEAD_DIM, "conv groups and retention heads share the 64-lane block sums"
    blk_np = np.kron(np.eye(RET_HEADS // 2, dtype=np.float32),
                     np.ones((HEAD_DIM, HEAD_DIM), np.float32))
    blk = jnp.asarray(blk_np, dtype=BF16)

    l = 0
    mod = _adaln(c, ada_w[l], ada_b[l]).reshape(bsz, 6, d)

    gap = ROUTER_EXPERT_ROW0 - N_GROUPS
    tail = ROUTER_LANES - ROUTER_ROWS
    wr = jnp.concatenate([router_group_w[l].T, jnp.zeros((gap, d), F32),
                          router_expert_w[l].T, jnp.zeros((tail, d), F32)], axis=0)
    wr_hi = wr.astype(BF16)
    wr_lo = (wr - wr_hi.astype(F32)).astype(BF16)
    wr2 = jnp.concatenate([wr_hi, wr_lo], axis=0)
    br = jnp.concatenate([router_group_b[l], jnp.zeros((gap,), F32),
                          router_expert_b[l], jnp.zeros((tail,), F32)]).reshape(ROUTER_LANES, 1)

    x1, h2t, logits_t = _mixer(
        x, mod, cos128, sin128, norm_mix_g[l].reshape(1, d), w_in[l],
        conv_w[l], conv_b[l].reshape(1, W), beta_ret[l].reshape(1, W),
        beta_conv[l].reshape(1, W), w_out[l], norm_ffn_g[l].reshape(1, d),
        wr2, br, lg, lgl, blk)
    slab, dest, plan = _router(logits_t)
    p0, p1 = dest[0], dest[1]
    p_rows = 2 * n + N_EXPERTS * EXPERT_TILE

    xs = _dispatch(p0, p1, plan, h2t, p_rows)
    de = expert_w_gate.shape[-1]
    y = _experts(plan, xs,
                 expert_w_gate[l].reshape(N_EXPERTS, d, de),
                 expert_w_up[l].reshape(N_EXPERTS, d, de),
                 expert_w_down[l].reshape(N_EXPERTS, de, d))
    out = _combine(p0, p1, y, x1.reshape(n, d), slab, mod, norm_final_g.reshape(1, d), seq)
    return out.reshape(bsz, seq, d)
```

```python
import jax
import jax.numpy as jnp
import numpy as np
from jax import lax
from jax.experimental import pallas as pl
from jax.experimental.pallas import tpu as pltpu

F32 = jnp.float32
BF16 = jnp.bfloat16

CHUNK = 64
RET_HEADS = 8
HEAD_DIM = 64
CONV_GROUP_DIM = 64
ROPE_BASE = 10000.0
N_GROUPS = 4
EXPERTS_PER_GROUP = 8
N_EXPERTS = N_GROUPS * EXPERTS_PER_GROUP
EPS = 1e-6
GN_EPS = 1e-5

LANES = 128
SUBLANES = 8
VMEM_LIMIT_BYTES = 56 * 1024 * 1024

SEQ_TILE = 256
EXPERT_TILE = 256
COMBINE_TILE = 256
ROUTER_LANES = LANES
ROUTER_GROUP_ROW0 = 0
ROUTER_EXPERT_ROW0 = SUBLANES
ROUTER_ROWS = ROUTER_EXPERT_ROW0 + N_EXPERTS


def _silu(v):
    return v * (1.0 / (1.0 + jnp.exp(-v)))


def _split_bf16(v):
    hi = v.astype(BF16)
    return hi, (v - hi.astype(F32)).astype(BF16)


def _adaln_kernel(c_ref, w_ref, b_ref, o_ref):
    s_hi, s_lo = _split_bf16(_silu(c_ref[...]))
    w_hi, w_lo = _split_bf16(w_ref[...])
    o_ref[...] = (jnp.dot(s_hi, w_hi, preferred_element_type=F32)
                  + jnp.dot(s_lo, w_hi, preferred_element_type=F32)
                  + jnp.dot(s_hi, w_lo, preferred_element_type=F32) + b_ref[...])


def _adaln(c, w, b):
    bsz, d = c.shape
    n = w.shape[1]
    tn = 1024
    return pl.pallas_call(
        _adaln_kernel,
        out_shape=jax.ShapeDtypeStruct((bsz, n), F32),
        grid=(n // tn,),
        in_specs=[pl.BlockSpec((bsz, d), lambda j: (0, 0)),
                  pl.BlockSpec((d, tn), lambda j: (0, j)),
                  pl.BlockSpec((1, tn), lambda j: (0, j))],
        out_specs=pl.BlockSpec((bsz, tn), lambda j: (0, j)),
        name="adaln",
    )(c, w, b.reshape(1, n))


ROPE_ROWS = 1024


def _rope_kernel(pos_ref, invf_ref, cos_ref, sin_ref):
    half = HEAD_DIM // 2
    per_row = LANES // half
    r = ROPE_ROWS
    ang = pos_ref[...].astype(F32) * invf_ref[...]
    lane = lax.broadcasted_iota(jnp.int32, (r, LANES), 1)
    quarter = lane // half
    sign = jnp.where(quarter % 2 == 0, -1.0, 1.0)
    for table, out_ref, scale in ((jnp.cos(ang), cos_ref, None), (jnp.sin(ang), sin_ref, sign)):
        rolled = [table] + [pltpu.roll(table, half * k, 1) for k in range(1, per_row)]
        for q in range(per_row):
            val = rolled[(0 - q) % per_row]
            for k in range(1, per_row):
                val = jnp.where(quarter == k, rolled[(k - q) % per_row], val)
            if scale is not None:
                val = val * scale
            out_ref[pl.ds(q, r, stride=per_row), :] = val


def _rope_tables(positions):
    n = positions.size
    half = HEAD_DIM // 2
    inv_freq = ROPE_BASE ** (-jnp.arange(0, HEAD_DIM, 2, dtype=F32) / HEAD_DIM)
    per_row = LANES // half
    rows = n // per_row
    pos_rep = jnp.broadcast_to(positions.reshape(n, 1), (n, half)).reshape(rows, LANES)
    invf = jnp.tile(inv_freq, per_row).reshape(1, LANES)
    tr = ROPE_ROWS
    return pl.pallas_call(
        _rope_kernel,
        out_shape=(jax.ShapeDtypeStruct((n, LANES), F32),) * 2,
        grid=(rows // tr,),
        in_specs=[pl.BlockSpec((tr, LANES), lambda i: (i, 0)),
                  pl.BlockSpec((1, LANES), lambda i: (0, 0))],
        out_specs=(pl.BlockSpec((tr * per_row, LANES), lambda i: (i, 0)),) * 2,
        name="rope_table",
    )(pos_rep, invf)


def _load_token_rows(ref, rows):
    return jnp.concatenate(
        [ref[pl.ds(c, rows, stride=SUBLANES), :] for c in range(SUBLANES)], axis=1)


def _store_token_rows(ref, val, row0=0):
    rows = val.shape[0]
    for c in range(SUBLANES):
        ref[pl.ds(row0 * SUBLANES + c, rows, stride=SUBLANES), :] = val[:, c * LANES:(c + 1) * LANES]


def _token_rows(ref, row):
    return ref.at[pl.ds(pl.multiple_of(row * SUBLANES, SUBLANES), SUBLANES), :]


MIXER_SUBTILES = 2


MIXER_WIN, MIXER_WOUT = 6, 11


def _mixer_kernel(*refs):
    *refs, winb_ref, woutb_ref = refs
    state_ref, ubuf_ref = refs[-2:]

    @pl.when((pl.program_id(0) == 0) & (pl.program_id(1) == 0))
    def _():
        for src, dst in ((refs[MIXER_WIN], winb_ref), (refs[MIXER_WOUT], woutb_ref)):
            step = RET_HEADS * HEAD_DIM
            for c0 in range(0, src.shape[1], step):
                dst[:, c0:c0 + step] = src[:, c0:c0 + step].astype(BF16)

    @pl.when(pl.program_id(1) == 0)
    def _():
        state_ref[...] = jnp.zeros_like(state_ref)
        ubuf_ref[0:SUBLANES, :] = jnp.zeros((SUBLANES, ubuf_ref.shape[1]), F32)

    refs[MIXER_WIN], refs[MIXER_WOUT] = winb_ref, woutb_ref
    for sub in range(MIXER_SUBTILES):
        _mixer_tile(sub, *refs)


def _mixer_tile(sub, lg_ref, x_ref, mod_ref, cos_ref, sin_ref, gmix_ref, win_ref, convw_ref,
                convb_ref, bret_ref, bconv_ref, wout_ref, gffn_ref, wr_ref, br_ref, lgl_ref,
                blk_ref, x1_ref, h2_ref, logit_ref, state_ref, ubuf_ref):
    L = SEQ_TILE
    W = RET_HEADS * HEAD_DIM
    tile_rows = slice(sub * L, (sub + 1) * L)

    x = x_ref[0, tile_rows, :]
    mod = mod_ref[0]
    shift_m, scale_m, gate_m = mod[0:1], mod[1:2], mod[2:3]
    shift_f, scale_f = mod[3:4], mod[4:5]

    ms = jnp.mean(x * x, axis=-1, keepdims=True)
    h = x * lax.rsqrt(ms + EPS) * gmix_ref[...]
    h = h * (1.0 + scale_m) + shift_m
    hb = h.astype(BF16)

    def proj(i):
        return jnp.dot(hb, win_ref[:, i * W:(i + 1) * W], preferred_element_type=F32)

    cos = jnp.concatenate([cos_ref[tile_rows, :]] * 4, axis=1)
    sin = jnp.concatenate([sin_ref[tile_rows, :]] * 4, axis=1)
    lane_w = lax.broadcasted_iota(jnp.int32, (L, W), 1)
    first_half = (lane_w & (HEAD_DIM - 1)) < (HEAD_DIM // 2)

    def rot(t):
        partner = jnp.where(first_half, pltpu.roll(t, W - HEAD_DIM // 2, 1),
                            pltpu.roll(t, HEAD_DIM // 2, 1))
        return t * cos + partner * sin

    q = rot(proj(0))
    k = rot(proj(1)) * (HEAD_DIM ** -0.5)
    v = proj(2)
    vb = v.astype(BF16)
    kb = k.astype(BF16)

    lgl = lgl_ref[...]
    rowf = lax.broadcasted_iota(jnp.int32, (L, W), 0).astype(F32)
    qd = q * jnp.exp(lgl * (rowf + 1.0))
    kd = k * jnp.exp(lgl * (float(L - 1) - rowf))
    blk = blk_ref[...]
    HW = W // 2
    blk_f = blk.astype(F32)
    qdb = qd.astype(BF16)
    kdb = kd.astype(BF16)
    state_decay = jnp.exp(lgl * float(L))
    inter = []
    for hf in range(2):
        sl = slice(hf * HW, (hf + 1) * HW)
        st = state_ref[hf]
        inter.append(jnp.dot(qdb[:, sl], st.astype(BF16), preferred_element_type=F32))
        kv = lax.dot_general(kdb[:, sl], vb[:, sl], (((0,), (0,)), ((), ())),
                             preferred_element_type=F32)
        state_ref[hf] = st * state_decay[:, sl] + kv * blk_f
    y_inter = jnp.concatenate(inter, axis=1)

    def head_sums(t):
        tb = t.astype(BF16)
        return jnp.concatenate(
            [jnp.dot(tb[:, hf * HW:(hf + 1) * HW], blk, preferred_element_type=F32)
             for hf in range(2)], axis=1)

    ii = lax.broadcasted_iota(jnp.int32, (L, L), 0)
    jj = lax.broadcasted_iota(jnp.int32, (L, L), 1)
    dist = jnp.abs(ii - jj).astype(F32)
    allowed = (jj // CHUNK) <= (ii // CHUNK)
    lane_p = lax.broadcasted_iota(jnp.int32, (L, LANES), 1)
    lo_head = lane_p < HEAD_DIM
    pairs = []
    for p in range(RET_HEADS // 2):
        sl = slice(p * LANES, (p + 1) * LANES)
        qp, kp, vp = q[:, sl], kb[:, sl], vb[:, sl]
        ys = []
        for hh in range(2):
            head = 2 * p + hh
            keep = lo_head if hh == 0 else jnp.logical_not(lo_head)
            qh = jnp.where(keep, qp, 0.0).astype(BF16)
            sc = lax.dot_general(qh, kp, (((1,), (1,)), ((), ())),
                                 preferred_element_type=F32)
            decay = jnp.where(allowed, jnp.exp(lg_ref[head] * dist), 0.0)
            ys.append(jnp.dot((sc * decay).astype(BF16), vp, preferred_element_type=F32))
        pairs.append(jnp.where(lo_head, ys[0], ys[1]))
    y = jnp.concatenate(pairs, axis=1) + y_inter

    inv_hd = 1.0 / HEAD_DIM
    mu = head_sums(y) * inv_hd
    d = y - mu
    var = head_sums(d * d) * inv_hd
    g = proj(3)
    y_ret = _silu(g) * (d * lax.rsqrt(var + GN_EPS)) * bret_ref[...]

    b_gate = proj(4)
    u = proj(5) * proj(6)
    ubuf_ref[SUBLANES:SUBLANES + L, :] = u
    u1 = ubuf_ref[SUBLANES - 1:SUBLANES - 1 + L, :]
    u2 = ubuf_ref[SUBLANES - 2:SUBLANES - 2 + L, :]
    ubuf_ref[0:SUBLANES, :] = ubuf_ref[L:L + SUBLANES, :]
    cw = convw_ref[...]
    conv = u2 * cw[0:1] + u1 * cw[1:2] + u * cw[2:3] + convb_ref[...]
    yc = b_gate * conv
    msc = head_sums(yc * yc) * (1.0 / CONV_GROUP_DIM)
    y_conv = yc * lax.rsqrt(msc + EPS) * bconv_ref[...]

    mix = (jnp.dot(y_ret.astype(BF16), wout_ref[0:W, :], preferred_element_type=F32)
           + jnp.dot(y_conv.astype(BF16), wout_ref[W:2 * W, :], preferred_element_type=F32))
    x1 = x + gate_m * mix
    x1_ref[0, tile_rows, :] = x1

    ms2 = jnp.mean(x1 * x1, axis=-1, keepdims=True)
    h2 = x1 * lax.rsqrt(ms2 + EPS) * gffn_ref[...]
    h2 = h2 * (1.0 + scale_f) + shift_f
    _store_token_rows(h2_ref, h2, row0=sub * L)

    hi, lo = _split_bf16(h2)
    w2 = wr_ref[...]
    nt_dims = (((1,), (1,)), ((), ()))
    parts = (lax.dot_general(w2, hi, nt_dims, preferred_element_type=F32)
             + lax.dot_general(w2, lo, nt_dims, preferred_element_type=F32))
    logits_t = parts[:ROUTER_LANES] + parts[ROUTER_LANES:] + br_ref[...]
    logit_ref[:, tile_rows] = logits_t[:ROUTER_ROWS]


def _mixer(x, mod, cos128, sin128, gmix, win, convw, convb, bret, bconv, wout, gffn,
           wr2, br, lg, lgl, blk):
    bsz, seq, d = x.shape
    L = SEQ_TILE * MIXER_SUBTILES
    ns = seq // L
    W = RET_HEADS * HEAD_DIM
    n = bsz * seq
    const2 = lambda b, s: (0, 0)
    in_specs = [
        pl.BlockSpec(memory_space=pltpu.SMEM),
        pl.BlockSpec((1, L, d), lambda b, s: (b, s, 0)),
        pl.BlockSpec((1, 6, d), lambda b, s: (b, 0, 0)),
        pl.BlockSpec((L, LANES), lambda b, s: (b * ns + s, 0)),
        pl.BlockSpec((L, LANES), lambda b, s: (b * ns + s, 0)),
        pl.BlockSpec((1, d), const2),
        pl.BlockSpec(win.shape, const2, pipeline_mode=pl.Buffered(1)),
        pl.BlockSpec(convw.shape, const2),
        pl.BlockSpec((1, W), const2),
        pl.BlockSpec((1, W), const2),
        pl.BlockSpec((1, W), const2),
        pl.BlockSpec(wout.shape, const2, pipeline_mode=pl.Buffered(1)),
        pl.BlockSpec((1, d), const2),
        pl.BlockSpec(wr2.shape, const2),
        pl.BlockSpec((ROUTER_LANES, 1), const2),
        pl.BlockSpec((1, W), const2),
        pl.BlockSpec((W // 2, W // 2), const2),
    ]
    assert d == SUBLANES * LANES, "one token must fill exactly one (8, 128) f32 tile"
    out_shape = (jax.ShapeDtypeStruct((bsz, seq, d), F32),
                 jax.ShapeDtypeStruct((n * SUBLANES, LANES), F32),
                 jax.ShapeDtypeStruct((ROUTER_ROWS, n), F32))
    out_specs = (pl.BlockSpec((1, L, d), lambda b, s: (b, s, 0)),
                 pl.BlockSpec((L * SUBLANES, LANES), lambda b, s: (b * ns + s, 0)),
                 pl.BlockSpec((ROUTER_ROWS, L), lambda b, s: (0, b * ns + s)))
    return pl.pallas_call(
        _mixer_kernel,
        out_shape=out_shape,
        grid=(bsz, ns),
        in_specs=in_specs,
        out_specs=out_specs,
        scratch_shapes=[pltpu.VMEM((2, W // 2, W // 2), F32),
                        pltpu.VMEM((SEQ_TILE + 2 * SUBLANES, W), F32),
                        pltpu.VMEM(win.shape, BF16),
                        pltpu.VMEM(wout.shape, BF16)],
        compiler_params=pltpu.CompilerParams(
            dimension_semantics=("arbitrary", "arbitrary"),
            vmem_limit_bytes=VMEM_LIMIT_BYTES),
        name="mixer",
    )(lg, x, mod, cos128, sin128, gmix, win, convw, convb, bret, bconv, wout, gffn,
      wr2, br, lgl, blk)


ROUTER_TILE = 2048
RES_E1, RES_E2, RES_W1, RES_W2, RES_A, RES_B = range(6)
PLAN_FIRST_TILE, PLAN_N_ITEMS, PLAN_ITEM_BASE, PLAN_START, PLAN_END, PLAN_TOTAL_ITEMS = range(6)


def _rows8(vals, width):
    rid = lax.broadcasted_iota(jnp.int32, (SUBLANES, width), 0)
    out = jnp.zeros((SUBLANES, width), F32)
    for r, v in enumerate(vals):
        out = jnp.where(rid == r, v, out)
    return out


def _router_kernel(lt_ref, slab_ref, dest_ref, plan_ref, res_ref, cnt_ref):
    T = ROUTER_TILE
    phase = pl.program_id(0)
    j = pl.program_id(1)
    nblk = T // LANES
    big = F32(1e9)
    rid8 =lax.broadcasted_iota(jnp.int32, (SUBLANES, T), 0).astype(F32)
    rid_e = lax.broadcasted_iota(jnp.int32, (N_EXPERTS, LANES), 0).astype(F32)

    def onehot(e_row, k):
        return jnp.where(rid_e == e_row[:, k * LANES:(k + 1) * LANES], 1.0, 0.0)

    @pl.when((phase == 0) & (j == 0))
    def _():
        cnt_ref[...] = jnp.zeros_like(cnt_ref)

    @pl.when(phase == 0)
    def _():
        lt = lt_ref[...]
        g_rows = lt[ROUTER_GROUP_ROW0:ROUTER_GROUP_ROW0 + SUBLANES]
        gvalid = rid8 < float(N_GROUPS)
        gm = jnp.where(gvalid, g_rows, F32(-jnp.inf))
        gexp = jnp.exp(gm - jnp.max(gm, axis=0, keepdims=True))
        gp = gexp / jnp.sum(gexp, axis=0, keepdims=True)
        g_top = jnp.max(gp, axis=0, keepdims=True)
        g_idx = jnp.min(jnp.where(gvalid & (gp == g_top), rid8, big), axis=0, keepdims=True)

        def group_slab(g):
            r0 = ROUTER_EXPERT_ROW0 + g * EXPERTS_PER_GROUP
            return lt[r0:r0 + EXPERTS_PER_GROUP]
        sel = group_slab(N_GROUPS - 1)
        for g in range(N_GROUPS - 2, -1, -1):
            sel = jnp.where(g_idx == float(g), group_slab(g), sel)
        eexp = jnp.exp(sel - jnp.max(sel, axis=0, keepdims=True))
        ep = eexp / jnp.sum(eexp, axis=0, keepdims=True)
        p1 = jnp.max(ep, axis=0, keepdims=True)
        i1 = jnp.min(jnp.where(ep == p1, rid8, big), axis=0, keepdims=True)
        m2 = rid8 != i1
        p2 = jnp.max(jnp.where(m2, ep, -1.0), axis=0, keepdims=True)
        i2 = jnp.min(jnp.where(m2 & (ep == p2), rid8, big), axis=0, keepdims=True)
        den = p1 + p2
        w1 = p1 / den * g_top
        w2 = p2 / den * g_top
        e1 = g_idx * float(EXPERTS_PER_GROUP) + i1
        e2 = g_idx * float(EXPERTS_PER_GROUP) + i2

        ii = lax.broadcasted_iota(jnp.int32, (LANES, LANES), 0)
        jj = lax.broadcasted_iota(jnp.int32, (LANES, LANES), 1)
        upper = jnp.where(ii < jj, 1.0, 0.0).astype(BF16)
        base = cnt_ref[...]
        ranks1, ranks2 = [], []
        for k in range(nblk):
            o1, o2 = onehot(e1, k), onehot(e2, k)
            r1 = jnp.dot(o1.astype(BF16), upper, preferred_element_type=F32)
            r2 = jnp.dot(o2.astype(BF16), upper, preferred_element_type=F32)
            c1 = jnp.sum(o1, axis=1, keepdims=True)
            c2 = jnp.sum(o2, axis=1, keepdims=True)
            ranks1.append(jnp.sum(o1 * (base + r1), axis=0, keepdims=True))
            ranks2.append(jnp.sum(o2 * (base + c1 + r2), axis=0, keepdims=True))
            base = base + c1 + c2
        cnt_ref[...] = base
        rank1 = jnp.concatenate(ranks1, axis=1)
        rank2 = jnp.concatenate(ranks2, axis=1)
        res_ref[j] = _rows8([e1, e2, w1, w2, rank1, rank2], T)

    @pl.when(phase == 1)
    def _():
        tm = float(EXPERT_TILE)
        cnt = cnt_ref[...]
        ei = lax.broadcasted_iota(jnp.int32, (N_EXPERTS, N_EXPERTS), 0)
        ej = lax.broadcasted_iota(jnp.int32, (N_EXPERTS, N_EXPERTS), 1)
        lower = jnp.where(ej < ei, 1.0, 0.0).astype(BF16)

        def exclusive_sum(col):
            return jnp.dot(lower, col.astype(BF16), preferred_element_type=F32)

        cnt_hi = jnp.floor(cnt * (1.0 / tm))
        starts = exclusive_sum(cnt_hi) * tm + exclusive_sum(cnt - cnt_hi * tm)
        ends = starts + cnt

        res = res_ref[j]
        e1, e2 = res[RES_E1:RES_E1 + 1], res[RES_E2:RES_E2 + 1]
        d1, d2 = [], []
        for k in range(nblk):
            blk_lanes = slice(k * LANES, (k + 1) * LANES)
            d1.append(res[RES_A:RES_A + 1, blk_lanes]
                      + jnp.sum(onehot(e1, k) * starts, axis=0, keepdims=True))
            d2.append(res[RES_B:RES_B + 1, blk_lanes]
                      + jnp.sum(onehot(e2, k) * starts, axis=0, keepdims=True))
        dest1 = jnp.concatenate(d1, axis=1)
        dest2 = jnp.concatenate(d2, axis=1)
        dest_ref[...] = _rows8([dest1, dest2], T).astype(jnp.int32)

        table = jnp.concatenate(
            [_rows8([e1, e2, res[RES_W1:RES_W1 + 1], res[RES_W2:RES_W2 + 1], dest1, dest2], T),
             jnp.zeros((ROUTER_LANES - SUBLANES, T), F32)], axis=0)
        slab_ref[...] = table.T

        @pl.when(j == 0)
        def _():
            lane_e = lax.broadcasted_iota(jnp.int32, (N_EXPERTS, LANES), 1).astype(F32)

            def as_row(col):
                return jnp.sum(jnp.where(rid_e == lane_e, col, 0.0), axis=0, keepdims=True)
            first_tile = jnp.floor(starts * (1.0 / tm))
            end_tile = jnp.floor((ends + (tm - 1.0)) * (1.0 / tm))
            n_items = jnp.where(cnt > 0.0, end_tile - first_tile, 0.0)
            item_base = exclusive_sum(n_items)
            total_items = jnp.sum(n_items, axis=0, keepdims=True)
            plan_ref[...] = _rows8([as_row(first_tile), as_row(n_items), as_row(item_base),
                                    as_row(starts), as_row(ends), total_items],
                                   LANES).astype(jnp.int32)


def _router(logits_t):
    n = logits_t.shape[1]
    T = ROUTER_TILE
    nt = n // T
    return pl.pallas_call(
        _router_kernel,
        out_shape=(jax.ShapeDtypeStruct((n, ROUTER_LANES), F32),
                   jax.ShapeDtypeStruct((SUBLANES, n), jnp.int32),
                   jax.ShapeDtypeStruct((SUBLANES, LANES), jnp.int32)),
        grid=(2, nt),
        in_specs=[pl.BlockSpec((ROUTER_ROWS, T), lambda p, j: (0, j * (1 - p) + (nt - 1) * p))],
        out_specs=(pl.BlockSpec((T, ROUTER_LANES), lambda p, j: (j * p, 0)),
                   pl.BlockSpec((SUBLANES, T), lambda p, j: (0, j * p)),
                   pl.BlockSpec((SUBLANES, LANES), lambda p, j: (0, 0))),
        scratch_shapes=[pltpu.VMEM((nt, SUBLANES, T), F32),
                        pltpu.VMEM((N_EXPERTS, LANES), F32)],
        compiler_params=pltpu.CompilerParams(
            dimension_semantics=("arbitrary", "arbitrary"), vmem_limit_bytes=VMEM_LIMIT_BYTES),
        name="router",
    )(logits_t)


DISPATCH_TILE = 512
ROW_DMA_UNROLL = 8
DISPATCH_LAG = 2
DISPATCH_SLOTS = DISPATCH_LAG + 2


def _dispatch_kernel(d0_ref, d1_ref, h2_hbm, xs_hbm, stage, isem, ssem):
    dt = DISPATCH_TILE
    ps = SUBLANES
    i = pl.program_id(0)
    nsteps = pl.num_programs(0)

    def in_copy(blk, slot):
        src = h2_hbm.at[pl.ds(pl.multiple_of(blk * (dt * ps), dt * ps), dt * ps), :]
        return pltpu.make_async_copy(src, stage.at[slot], isem.at[slot])

    def wait_rows(slot):
        for _ in range(2):
            pltpu.make_async_copy(stage.at[slot], xs_hbm.at[pl.ds(0, dt * ps), :],
                                  ssem.at[slot]).wait()

    slot = i % DISPATCH_SLOTS

    @pl.when(i == 0)
    def _():
        in_copy(0, 0).start()

        @pl.when(nsteps > 1)
        def _():
            in_copy(1, 1).start()

    @pl.when(i >= DISPATCH_LAG)
    def _():
        wait_rows((i - DISPATCH_LAG) % DISPATCH_SLOTS)

    @pl.when(i + 2 < nsteps)
    def _():
        in_copy(i + 2, (i + 2) % DISPATCH_SLOTS).start()

    in_copy(i, slot).wait()
    base = i * dt
    src_ref = stage.at[slot]

    def body(r, carry):
        src = _token_rows(src_ref, r)
        for prio, d_ref in enumerate((d0_ref, d1_ref)):
            pltpu.make_async_copy(src, _token_rows(xs_hbm, d_ref[base + r]),
                                  ssem.at[slot]).start(priority=prio)
        return carry

    lax.fori_loop(0, dt, body, 0, unroll=ROW_DMA_UNROLL)

    @pl.when(i == nsteps - 1)
    def _():
        for back in range(DISPATCH_LAG - 1, -1, -1):
            @pl.when(i - back >= 0)
            def _():
                wait_rows((i - back) % DISPATCH_SLOTS)


def _dispatch(dest0, dest1, h2t):
    n = dest0.shape[0]
    dt = DISPATCH_TILE
    grid_spec = pltpu.PrefetchScalarGridSpec(
        num_scalar_prefetch=2,
        grid=(n // dt,),
        in_specs=[pl.BlockSpec(memory_space=pl.ANY)],
        out_specs=pl.BlockSpec(memory_space=pl.ANY),
        scratch_shapes=[pltpu.VMEM((DISPATCH_SLOTS, dt * SUBLANES, LANES), F32),
                        pltpu.SemaphoreType.DMA((DISPATCH_SLOTS,)),
                        pltpu.SemaphoreType.DMA((DISPATCH_SLOTS,))],
    )
    return pl.pallas_call(
        _dispatch_kernel,
        out_shape=jax.ShapeDtypeStruct((2 * n * SUBLANES, LANES), F32),
        grid_spec=grid_spec,
        compiler_params=pltpu.CompilerParams(dimension_semantics=("arbitrary",)),
        name="dispatch",
    )(dest0, dest1, h2t)


EXPERT_IN_SLOTS = 4
EXPERT_OUT_SLOTS = 3


def _expert_kernel(plan_ref, xs_hbm, wg_ref, wu_ref, wd_ref, y_hbm,
                   xbuf, ybuf, wgb, wub, wdb, item_tile, isem, osem):
    tm = EXPERT_TILE
    rows = tm * SUBLANES
    ni, no = EXPERT_IN_SLOTS, EXPERT_OUT_SLOTS
    e = pl.program_id(0)
    n_exp = pl.num_programs(0)
    n_items = plan_ref[PLAN_TOTAL_ITEMS, 0]
    n_mine = plan_ref[PLAN_N_ITEMS, e]
    n_tiles = y_hbm.shape[0] // rows
    row_lo = plan_ref[PLAN_START, e]
    row_hi = plan_ref[PLAN_END, e]

    def tile_rows(ref, g):
        return ref.at[pl.ds(pl.multiple_of(g * rows, rows), rows), :]

    def in_copy(k):
        return pltpu.make_async_copy(tile_rows(xs_hbm, item_tile[k]), xbuf.at[k % ni],
                                     isem.at[k % ni])

    def out_copy(g):
        return pltpu.make_async_copy(ybuf.at[g % no], tile_rows(y_hbm, g), osem.at[g % no])

    @pl.when(e == 0)
    def _():
        def per_expert(ex, carry):
            def per_item(t, c2):
                item_tile[plan_ref[PLAN_ITEM_BASE, ex] + t] = plan_ref[PLAN_FIRST_TILE, ex] + t
                return c2
            lax.fori_loop(0, plan_ref[PLAN_N_ITEMS, ex], per_item, 0)
            return carry
        lax.fori_loop(0, n_exp, per_expert, 0)

        for k0 in range(ni - 1):
            @pl.when(k0 < n_items)
            def _():
                in_copy(k0).start()

    @pl.when(n_mine > 0)
    def _():
        wgb[...] = wg_ref[0].astype(BF16)
        wub[...] = wu_ref[0].astype(BF16)
        wdb[...] = wd_ref[0].astype(BF16)

    def item(t, carry):
        k = plan_ref[PLAN_ITEM_BASE, e] + t
        g = plan_ref[PLAN_FIRST_TILE, e] + t
        in_copy(k).wait()

        @pl.when(k + ni - 1 < n_items)
        def _():
            in_copy(k + ni - 1).start()

        xb = _load_token_rows(xbuf.at[k % ni], tm).astype(BF16)
        a = jnp.dot(xb, wgb[...], preferred_element_type=F32)
        u = jnp.dot(xb, wub[...], preferred_element_type=F32)
        hid = (_silu(a) * u).astype(BF16)
        y = jnp.dot(hid, wdb[...], preferred_element_type=F32)

        opens = row_lo <= g * tm
        closes = row_hi >= (g + 1) * tm
        out = ybuf.at[g % no]

        @pl.when(opens & (g >= no))
        def _():
            out_copy(g - no).wait()

        @pl.when(opens)
        def _():
            _store_token_rows(out, y)

        @pl.when(jnp.logical_not(opens))
        def _():
            row = g * tm + lax.broadcasted_iota(jnp.int32, (tm, LANES), 0)
            mine = (row >= row_lo) & (row < row_hi)
            for c in range(SUBLANES):
                chunk = pl.ds(c, tm, stride=SUBLANES)
                out[chunk, :] = jnp.where(mine, y[:, c * LANES:(c + 1) * LANES], out[chunk, :])

        @pl.when(closes)
        def _():
            out_copy(g).start()
        return carry

    lax.fori_loop(0, n_mine, item, 0)

    @pl.when(e == n_exp - 1)
    def _():
        for back in range(no, 0, -1):
            @pl.when(n_tiles >= back)
            def _():
                out_copy(n_tiles - back).wait()


def _experts(plan, xs, wg, wu, wd):
    tm = EXPERT_TILE
    n_exp, d, de = wg.shape
    grid_spec = pltpu.PrefetchScalarGridSpec(
        num_scalar_prefetch=1,
        grid=(n_exp,),
        in_specs=[
            pl.BlockSpec(memory_space=pl.ANY),
            pl.BlockSpec((1, d, de), lambda e, *_: (e, 0, 0)),
            pl.BlockSpec((1, d, de), lambda e, *_: (e, 0, 0)),
            pl.BlockSpec((1, de, d), lambda e, *_: (e, 0, 0)),
        ],
        out_specs=pl.BlockSpec(memory_space=pl.ANY),
        scratch_shapes=[pltpu.VMEM((EXPERT_IN_SLOTS, tm * SUBLANES, LANES), F32),
                        pltpu.VMEM((EXPERT_OUT_SLOTS, tm * SUBLANES, LANES), F32),
                        pltpu.VMEM((d, de), BF16),
                        pltpu.VMEM((d, de), BF16),
                        pltpu.VMEM((de, d), BF16),
                        pltpu.SMEM((n_exp + xs.shape[0] // (tm * SUBLANES),), jnp.int32),
                        pltpu.SemaphoreType.DMA((EXPERT_IN_SLOTS,)),
                        pltpu.SemaphoreType.DMA((EXPERT_OUT_SLOTS,))],
    )
    return pl.pallas_call(
        _expert_kernel,
        out_shape=jax.ShapeDtypeStruct(xs.shape, F32),
        grid_spec=grid_spec,
        compiler_params=pltpu.CompilerParams(
            dimension_semantics=("arbitrary",),
            vmem_limit_bytes=VMEM_LIMIT_BYTES),
        name="experts",
    )(plan, xs, wg, wu, wd)


COMBINE_SLOTS = 3


def _combine_kernel(p0_ref, p1_ref, y_hbm, x1_ref, slab_ref, mod_ref, gfin_ref, o_ref,
                    *scratch):
    tm = COMBINE_TILE
    ns = COMBINE_SLOTS
    ahead = ns - 1
    ybufs, sem = scratch[:ns], scratch[ns]
    i = pl.program_id(0)
    nt = pl.num_programs(0)

    def row_copies(tile, slot, r):
        for j, p_ref in enumerate((p0_ref, p1_ref)):
            pltpu.make_async_copy(_token_rows(y_hbm, p_ref[tile * tm + r]),
                                  _token_rows(ybufs[slot].at[j], r),
                                  sem.at[slot]).start(priority=j)

    def wait_tile(slot):
        for j in range(2):
            pltpu.make_async_copy(y_hbm.at[pl.ds(0, tm * SUBLANES), :], ybufs[slot].at[j],
                                  sem.at[slot]).wait()

    @pl.when(i == 0)
    def _():
        for t0 in range(ahead):
            @pl.when(t0 < nt)
            def _():
                def body(r, carry):
                    row_copies(t0, t0, r)
                    return carry
                lax.fori_loop(0, tm, body, 0, unroll=ROW_DMA_UNROLL)

    n_chunks = SUBLANES
    batch = tm // n_chunks

    def step(slot):
        wait_tile(slot)
        nxt = jnp.minimum(i + ahead, nt - 1)
        nslot = (slot + ahead) % ns
        slab = slab_ref[...]
        w0 = slab[:, 2:3]
        w1 = slab[:, 3:4]
        gate_f = mod_ref[0][5:6]
        sq = jnp.zeros((tm, LANES), F32)
        for c in range(n_chunks):
            lanes = slice(c * LANES, (c + 1) * LANES)
            y0 = ybufs[slot].at[0][pl.ds(c, tm, stride=SUBLANES), :]
            y1 = ybufs[slot].at[1][pl.ds(c, tm, stride=SUBLANES), :]
            xo = x1_ref[:, lanes] + gate_f[:, lanes] * (w0 * y0 + w1 * y1)
            sq = sq + xo * xo
            o_ref[:, lanes] = xo
            for r in range(c * batch, (c + 1) * batch):
                row_copies(nxt, nslot, r)
        ms = jnp.sum(sq, axis=-1, keepdims=True) * (1.0 / (n_chunks * LANES))
        scale = lax.rsqrt(ms + EPS)
        for c in range(n_chunks):
            lanes = slice(c * LANES, (c + 1) * LANES)
            o_ref[:, lanes] = o_ref[:, lanes] * scale * gfin_ref[:, lanes]

    for slot in range(ns):
        @pl.when(i % ns == slot)
        def _():
            step(slot)

    @pl.when(i == nt - 1)
    def _():
        for k in range(ahead):
            for slot in range(ns):
                @pl.when((nt - 1 - k >= 0) & ((nt - 1 - k + ahead) % ns == slot))
                def _():
                    wait_tile(slot)


def _combine(p0, p1, y, x1, slab, mod, gfin, seq):
    n, d = x1.shape
    tm = COMBINE_TILE
    tiles_per_seq = seq // tm
    grid_spec = pltpu.PrefetchScalarGridSpec(
        num_scalar_prefetch=2,
        grid=(n // tm,),
        in_specs=[
            pl.BlockSpec(memory_space=pl.ANY),
            pl.BlockSpec((tm, d), lambda i, a, b: (i, 0)),
            pl.BlockSpec((tm, ROUTER_LANES), lambda i, a, b: (i, 0)),
            pl.BlockSpec((1, 6, d), lambda i, a, b: (i // tiles_per_seq, 0, 0)),
            pl.BlockSpec((1, d), lambda i, a, b: (0, 0)),
        ],
        out_specs=pl.BlockSpec((tm, d), lambda i, a, b: (i, 0)),
        scratch_shapes=([pltpu.VMEM((2, tm * SUBLANES, LANES), F32)] * COMBINE_SLOTS
                        + [pltpu.SemaphoreType.DMA((COMBINE_SLOTS,))]),
    )
    return pl.pallas_call(
        _combine_kernel,
        out_shape=jax.ShapeDtypeStruct((n, d), F32),
        grid_spec=grid_spec,
        compiler_params=pltpu.CompilerParams(
            dimension_semantics=("arbitrary",),
            vmem_limit_bytes=VMEM_LIMIT_BYTES),
        name="combine",
    )(p0, p1, y, x1, slab, mod, gfin)


def kernel(x, c, positions, ada_w, ada_b, norm_mix_g, norm_ffn_g, w_in, conv_w, conv_b,
           beta_ret, beta_conv, w_out, router_group_w, router_group_b, router_expert_w,
           router_expert_b, expert_w_gate, expert_w_up, expert_w_down, norm_final_g):
    bsz, seq, d = x.shape
    n = bsz * seq
    depth = ada_w.shape[0]
    assert depth == 1, "the combine kernel fuses the trunk's final RMSNorm (single layer)"
    W = RET_HEADS * HEAD_DIM

    cos128, sin128 = _rope_tables(positions)
    heads = jnp.arange(RET_HEADS, dtype=F32)
    lg = jnp.log1p(-jnp.exp2(-5.0 - heads))
    lgl = jnp.repeat(lg, HEAD_DIM).reshape(1, W)
    assert CONV_GROUP_DIM == HEAD_DIM, "conv groups and retention heads share the 64-lane block sums"
    blk_np = np.kron(np.eye(RET_HEADS // 2, dtype=np.float32),
                     np.ones((HEAD_DIM, HEAD_DIM), np.float32))
    blk = jnp.asarray(blk_np, dtype=BF16)

    l = 0
    mod = _adaln(c, ada_w[l], ada_b[l]).reshape(bsz, 6, d)

    gap = ROUTER_EXPERT_ROW0 - N_GROUPS
    tail = ROUTER_LANES - ROUTER_ROWS
    wr = jnp.concatenate([router_group_w[l].T, jnp.zeros((gap, d), F32),
                          router_expert_w[l].T, jnp.zeros((tail, d), F32)], axis=0)
    wr2 = jnp.concatenate(_split_bf16(wr), axis=0)
    br = jnp.concatenate([router_group_b[l], jnp.zeros((gap,), F32),
                          router_expert_b[l], jnp.zeros((tail,), F32)]).reshape(ROUTER_LANES, 1)

    x1, h2t, logits_t = _mixer(
        x, mod, cos128, sin128, norm_mix_g[l].reshape(1, d), w_in[l],
        conv_w[l], conv_b[l].reshape(1, W), beta_ret[l].reshape(1, W),
        beta_conv[l].reshape(1, W), w_out[l], norm_ffn_g[l].reshape(1, d),
        wr2, br, lg, lgl, blk)
    slab, dest, plan = _router(logits_t)
    p0, p1 = dest[0], dest[1]
    assert (2 * n) % EXPERT_TILE == 0, "the sorted rows must fill whole row tiles"

    xs = _dispatch(p0, p1, h2t)
    de = expert_w_gate.shape[-1]
    y = _experts(plan, xs,
                 expert_w_gate[l].reshape(N_EXPERTS, d, de),
                 expert_w_up[l].reshape(N_EXPERTS, d, de),
                 expert_w_down[l].reshape(N_EXPERTS, de, d))
    out = _combine(p0, p1, y, x1.reshape(n, d), slab, mod, norm_final_g.reshape(1, d), seq)
    return out.reshape(bsz, seq, d)
```

```python
import jax
import jax.numpy as jnp
import numpy as np
from jax import lax
from jax.experimental import pallas as pl
from jax.experimental.pallas import tpu as pltpu

F32 = jnp.float32
BF16 = jnp.bfloat16

CHUNK = 64
RET_HEADS = 8
HEAD_DIM = 64
CONV_GROUP_DIM = 64
ROPE_BASE = 10000.0
N_GROUPS = 4
EXPERTS_PER_GROUP = 8
N_EXPERTS = N_GROUPS * EXPERTS_PER_GROUP
EPS = 1e-6
GN_EPS = 1e-5

LANES = 128
SUBLANES = 8
VMEM_LIMIT_BYTES = 56 * 1024 * 1024

SEQ_TILE = 256
EXPERT_TILE = 256
COMBINE_TILE = 256
ROUTER_LANES = LANES
ROUTER_GROUP_ROW0 = 0
ROUTER_EXPERT_ROW0 = SUBLANES
ROUTER_ROWS = ROUTER_EXPERT_ROW0 + N_EXPERTS


def _silu(v):
    return v * (1.0 / (1.0 + jnp.exp(-v)))


def _split_bf16(v):
    hi = v.astype(BF16)
    return hi, (v - hi.astype(F32)).astype(BF16)


def _adaln_kernel(c_ref, w_ref, b_ref, o_ref):
    s_hi, s_lo = _split_bf16(_silu(c_ref[...]))
    w_hi, w_lo = _split_bf16(w_ref[...])
    o_ref[...] = (jnp.dot(s_hi, w_hi, preferred_element_type=F32)
                  + jnp.dot(s_lo, w_hi, preferred_element_type=F32)
                  + jnp.dot(s_hi, w_lo, preferred_element_type=F32) + b_ref[...])


def _adaln(c, w, b):
    bsz, d = c.shape
    n = w.shape[1]
    tn = 1024
    return pl.pallas_call(
        _adaln_kernel,
        out_shape=jax.ShapeDtypeStruct((bsz, n), F32),
        grid=(n // tn,),
        in_specs=[pl.BlockSpec((bsz, d), lambda j: (0, 0)),
                  pl.BlockSpec((d, tn), lambda j: (0, j)),
                  pl.BlockSpec((1, tn), lambda j: (0, j))],
        out_specs=pl.BlockSpec((bsz, tn), lambda j: (0, j)),
        name="adaln",
    )(c, w, b.reshape(1, n))


ROPE_ROWS = 1024


def _rope_kernel(pos_ref, invf_ref, cos_ref, sin_ref):
    half = HEAD_DIM // 2
    per_row = LANES // half
    r = ROPE_ROWS
    ang = pos_ref[...].astype(F32) * invf_ref[...]
    lane = lax.broadcasted_iota(jnp.int32, (r, LANES), 1)
    quarter = lane // half
    sign = jnp.where(quarter % 2 == 0, -1.0, 1.0)
    for table, out_ref, scale in ((jnp.cos(ang), cos_ref, None), (jnp.sin(ang), sin_ref, sign)):
        rolled = [table] + [pltpu.roll(table, half * k, 1) for k in range(1, per_row)]
        for q in range(per_row):
            val = rolled[(0 - q) % per_row]
            for k in range(1, per_row):
                val = jnp.where(quarter == k, rolled[(k - q) % per_row], val)
            if scale is not None:
                val = val * scale
            out_ref[pl.ds(q, r, stride=per_row), :] = val


def _rope_tables(positions):
    n = positions.size
    half = HEAD_DIM // 2
    inv_freq = ROPE_BASE ** (-jnp.arange(0, HEAD_DIM, 2, dtype=F32) / HEAD_DIM)
    per_row = LANES // half
    rows = n // per_row
    pos_rep = jnp.broadcast_to(positions.reshape(n, 1), (n, half)).reshape(rows, LANES)
    invf = jnp.tile(inv_freq, per_row).reshape(1, LANES)
    tr = ROPE_ROWS
    return pl.pallas_call(
        _rope_kernel,
        out_shape=(jax.ShapeDtypeStruct((n, LANES), F32),) * 2,
        grid=(rows // tr,),
        in_specs=[pl.BlockSpec((tr, LANES), lambda i: (i, 0)),
                  pl.BlockSpec((1, LANES), lambda i: (0, 0))],
        out_specs=(pl.BlockSpec((tr * per_row, LANES), lambda i: (i, 0)),) * 2,
        name="rope_table",
    )(pos_rep, invf)


def _load_token_rows(ref, rows):
    return jnp.concatenate(
        [ref[pl.ds(c, rows, stride=SUBLANES), :] for c in range(SUBLANES)], axis=1)


def _store_token_rows(ref, val, row0=0):
    rows = val.shape[0]
    for c in range(SUBLANES):
        ref[pl.ds(row0 * SUBLANES + c, rows, stride=SUBLANES), :] = val[:, c * LANES:(c + 1) * LANES]


def _token_rows(ref, row):
    return ref.at[pl.ds(pl.multiple_of(row * SUBLANES, SUBLANES), SUBLANES), :]


MIXER_SUBTILES = 2


MIXER_WIN, MIXER_WOUT = 6, 11


def _mixer_kernel(*refs):
    *refs, winb_ref, woutb_ref = refs
    state_ref, ubuf_ref = refs[-2:]

    @pl.when((pl.program_id(0) == 0) & (pl.program_id(1) == 0))
    def _():
        for src, dst in ((refs[MIXER_WIN], winb_ref), (refs[MIXER_WOUT], woutb_ref)):
            step = RET_HEADS * HEAD_DIM
            for c0 in range(0, src.shape[1], step):
                dst[:, c0:c0 + step] = src[:, c0:c0 + step].astype(BF16)

    @pl.when(pl.program_id(1) == 0)
    def _():
        state_ref[...] = jnp.zeros_like(state_ref)
        ubuf_ref[0:SUBLANES, :] = jnp.zeros((SUBLANES, ubuf_ref.shape[1]), F32)

    refs[MIXER_WIN], refs[MIXER_WOUT] = winb_ref, woutb_ref
    for sub in range(MIXER_SUBTILES):
        _mixer_tile(sub, *refs)


def _mixer_tile(sub, lg_ref, x_ref, mod_ref, cos_ref, sin_ref, gmix_ref, win_ref, convw_ref,
                convb_ref, bret_ref, bconv_ref, wout_ref, gffn_ref, wr_ref, br_ref, lgl_ref,
                blk_ref, x1_ref, h2_ref, logit_ref, state_ref, ubuf_ref):
    L = SEQ_TILE
    W = RET_HEADS * HEAD_DIM
    tile_rows = slice(sub * L, (sub + 1) * L)

    x = x_ref[0, tile_rows, :]
    mod = mod_ref[0]
    shift_m, scale_m, gate_m = mod[0:1], mod[1:2], mod[2:3]
    shift_f, scale_f = mod[3:4], mod[4:5]

    ms = jnp.mean(x * x, axis=-1, keepdims=True)
    h = x * lax.rsqrt(ms + EPS) * gmix_ref[...]
    h = h * (1.0 + scale_m) + shift_m
    hb = h.astype(BF16)

    def proj(i):
        return jnp.dot(hb, win_ref[:, i * W:(i + 1) * W], preferred_element_type=F32)

    cos = jnp.concatenate([cos_ref[tile_rows, :]] * 4, axis=1)
    sin = jnp.concatenate([sin_ref[tile_rows, :]] * 4, axis=1)
    lane_w = lax.broadcasted_iota(jnp.int32, (L, W), 1)
    first_half = (lane_w & (HEAD_DIM - 1)) < (HEAD_DIM // 2)

    def rot(t):
        partner = jnp.where(first_half, pltpu.roll(t, W - HEAD_DIM // 2, 1),
                            pltpu.roll(t, HEAD_DIM // 2, 1))
        return t * cos + partner * sin

    q = rot(proj(0))
    k = rot(proj(1)) * (HEAD_DIM ** -0.5)
    v = proj(2)
    vb = v.astype(BF16)
    kb = k.astype(BF16)

    lgl = lgl_ref[...]
    rowf = lax.broadcasted_iota(jnp.int32, (L, W), 0).astype(F32)
    qd = q * jnp.exp(lgl * (rowf + 1.0))
    kd = k * jnp.exp(lgl * (float(L - 1) - rowf))
    blk = blk_ref[...]
    HW = W // 2
    blk_f = blk.astype(F32)
    qdb = qd.astype(BF16)
    kdb = kd.astype(BF16)
    state_decay = jnp.exp(lgl * float(L))
    inter = []
    for hf in range(2):
        sl = slice(hf * HW, (hf + 1) * HW)
        st = state_ref[hf]
        inter.append(jnp.dot(qdb[:, sl], st.astype(BF16), preferred_element_type=F32))
        kv = lax.dot_general(kdb[:, sl], vb[:, sl], (((0,), (0,)), ((), ())),
                             preferred_element_type=F32)
        state_ref[hf] = st * state_decay[:, sl] + kv * blk_f
    y_inter = jnp.concatenate(inter, axis=1)

    def head_sums(t):
        tb = t.astype(BF16)
        return jnp.concatenate(
            [jnp.dot(tb[:, hf * HW:(hf + 1) * HW], blk, preferred_element_type=F32)
             for hf in range(2)], axis=1)

    ii = lax.broadcasted_iota(jnp.int32, (L, L), 0)
    jj = lax.broadcasted_iota(jnp.int32, (L, L), 1)
    dist = jnp.abs(ii - jj).astype(F32)
    allowed = (jj // CHUNK) <= (ii // CHUNK)
    lane_p = lax.broadcasted_iota(jnp.int32, (L, LANES), 1)
    lo_head = lane_p < HEAD_DIM
    pairs = []
    for p in range(RET_HEADS // 2):
        sl = slice(p * LANES, (p + 1) * LANES)
        qp, kp, vp = q[:, sl], kb[:, sl], vb[:, sl]
        ys = []
        for hh in range(2):
            head = 2 * p + hh
            keep = lo_head if hh == 0 else jnp.logical_not(lo_head)
            qh = jnp.where(keep, qp, 0.0).astype(BF16)
            sc = lax.dot_general(qh, kp, (((1,), (1,)), ((), ())),
                                 preferred_element_type=F32)
            decay = jnp.where(allowed, jnp.exp(lg_ref[head] * dist), 0.0)
            ys.append(jnp.dot((sc * decay).astype(BF16), vp, preferred_element_type=F32))
        pairs.append(jnp.where(lo_head, ys[0], ys[1]))
    y = jnp.concatenate(pairs, axis=1) + y_inter

    inv_hd = 1.0 / HEAD_DIM
    mu = head_sums(y) * inv_hd
    d = y - mu
    var = head_sums(d * d) * inv_hd
    g = proj(3)
    y_ret = _silu(g) * (d * lax.rsqrt(var + GN_EPS)) * bret_ref[...]

    b_gate = proj(4)
    u = proj(5) * proj(6)
    ubuf_ref[SUBLANES:SUBLANES + L, :] = u
    u1 = ubuf_ref[SUBLANES - 1:SUBLANES - 1 + L, :]
    u2 = ubuf_ref[SUBLANES - 2:SUBLANES - 2 + L, :]
    ubuf_ref[0:SUBLANES, :] = ubuf_ref[L:L + SUBLANES, :]
    cw = convw_ref[...]
    conv = u2 * cw[0:1] + u1 * cw[1:2] + u * cw[2:3] + convb_ref[...]
    yc = b_gate * conv
    msc = head_sums(yc * yc) * (1.0 / CONV_GROUP_DIM)
    y_conv = yc * lax.rsqrt(msc + EPS) * bconv_ref[...]

    mix = (jnp.dot(y_ret.astype(BF16), wout_ref[0:W, :], preferred_element_type=F32)
           + jnp.dot(y_conv.astype(BF16), wout_ref[W:2 * W, :], preferred_element_type=F32))
    x1 = x + gate_m * mix
    x1_ref[0, tile_rows, :] = x1

    ms2 = jnp.mean(x1 * x1, axis=-1, keepdims=True)
    h2 = x1 * lax.rsqrt(ms2 + EPS) * gffn_ref[...]
    h2 = h2 * (1.0 + scale_f) + shift_f
    _store_token_rows(h2_ref, h2, row0=sub * L)

    hi, lo = _split_bf16(h2)
    w2 = wr_ref[...]
    nt_dims = (((1,), (1,)), ((), ()))
    parts = (lax.dot_general(w2, hi, nt_dims, preferred_element_type=F32)
             + lax.dot_general(w2, lo, nt_dims, preferred_element_type=F32))
    logits_t = parts[:ROUTER_LANES] + parts[ROUTER_LANES:] + br_ref[...]
    logit_ref[:, tile_rows] = logits_t[:ROUTER_ROWS]


def _mixer(x, mod, cos128, sin128, gmix, win, convw, convb, bret, bconv, wout, gffn,
           wr2, br, lg, lgl, blk):
    bsz, seq, d = x.shape
    L = SEQ_TILE * MIXER_SUBTILES
    ns = seq // L
    W = RET_HEADS * HEAD_DIM
    n = bsz * seq
    const2 = lambda b, s: (0, 0)
    in_specs = [
        pl.BlockSpec(memory_space=pltpu.SMEM),
        pl.BlockSpec((1, L, d), lambda b, s: (b, s, 0)),
        pl.BlockSpec((1, 6, d), lambda b, s: (b, 0, 0)),
        pl.BlockSpec((L, LANES), lambda b, s: (b * ns + s, 0)),
        pl.BlockSpec((L, LANES), lambda b, s: (b * ns + s, 0)),
        pl.BlockSpec((1, d), const2),
        pl.BlockSpec(win.shape, const2, pipeline_mode=pl.Buffered(1)),
        pl.BlockSpec(convw.shape, const2),
        pl.BlockSpec((1, W), const2),
        pl.BlockSpec((1, W), const2),
        pl.BlockSpec((1, W), const2),
        pl.BlockSpec(wout.shape, const2, pipeline_mode=pl.Buffered(1)),
        pl.BlockSpec((1, d), const2),
        pl.BlockSpec(wr2.shape, const2),
        pl.BlockSpec((ROUTER_LANES, 1), const2),
        pl.BlockSpec((1, W), const2),
        pl.BlockSpec((W // 2, W // 2), const2),
    ]
    assert d == SUBLANES * LANES, "one token must fill exactly one (8, 128) f32 tile"
    out_shape = (jax.ShapeDtypeStruct((bsz, seq, d), F32),
                 jax.ShapeDtypeStruct((n * SUBLANES, LANES), F32),
                 jax.ShapeDtypeStruct((ROUTER_ROWS, n), F32))
    out_specs = (pl.BlockSpec((1, L, d), lambda b, s: (b, s, 0)),
                 pl.BlockSpec((L * SUBLANES, LANES), lambda b, s: (b * ns + s, 0)),
                 pl.BlockSpec((ROUTER_ROWS, L), lambda b, s: (0, b * ns + s)))
    return pl.pallas_call(
        _mixer_kernel,
        out_shape=out_shape,
        grid=(bsz, ns),
        in_specs=in_specs,
        out_specs=out_specs,
        scratch_shapes=[pltpu.VMEM((2, W // 2, W // 2), F32),
                        pltpu.VMEM((SEQ_TILE + 2 * SUBLANES, W), F32),
                        pltpu.VMEM(win.shape, BF16),
                        pltpu.VMEM(wout.shape, BF16)],
        compiler_params=pltpu.CompilerParams(
            dimension_semantics=("arbitrary", "arbitrary"),
            vmem_limit_bytes=VMEM_LIMIT_BYTES),
        name="mixer",
    )(lg, x, mod, cos128, sin128, gmix, win, convw, convb, bret, bconv, wout, gffn,
      wr2, br, lgl, blk)


ROUTER_TILE = 2048
RES_E1, RES_E2, RES_W1, RES_W2, RES_A, RES_B = range(6)
PLAN_FIRST_TILE, PLAN_N_TILE, PLAN_PAD_START, PLAN_PAD_N, PLAN_N_USED = range(5)


def _rows8(vals, width):
    rid = lax.broadcasted_iota(jnp.int32, (SUBLANES, width), 0)
    out = jnp.zeros((SUBLANES, width), F32)
    for r, v in enumerate(vals):
        out = jnp.where(rid == r, v, out)
    return out


def _router_kernel(lt_ref, slab_ref, dest_ref, plan_ref, res_ref, cnt_ref):
    T = ROUTER_TILE
    phase = pl.program_id(0)
    j = pl.program_id(1)
    nblk = T // LANES
    big = F32(1e9)
    rid8 =lax.broadcasted_iota(jnp.int32, (SUBLANES, T), 0).astype(F32)
    rid_e = lax.broadcasted_iota(jnp.int32, (N_EXPERTS, LANES), 0).astype(F32)

    def onehot(e_row, k):
        return jnp.where(rid_e == e_row[:, k * LANES:(k + 1) * LANES], 1.0, 0.0)

    @pl.when((phase == 0) & (j == 0))
    def _():
        cnt_ref[...] = jnp.zeros_like(cnt_ref)

    @pl.when(phase == 0)
    def _():
        lt = lt_ref[...]
        g_rows = lt[ROUTER_GROUP_ROW0:ROUTER_GROUP_ROW0 + SUBLANES]
        gvalid = rid8 < float(N_GROUPS)
        gm = jnp.where(gvalid, g_rows, F32(-jnp.inf))
        gexp = jnp.exp(gm - jnp.max(gm, axis=0, keepdims=True))
        gp = gexp / jnp.sum(gexp, axis=0, keepdims=True)
        g_top = jnp.max(gp, axis=0, keepdims=True)
        g_idx = jnp.min(jnp.where(gvalid & (gp == g_top), rid8, big), axis=0, keepdims=True)

        def group_slab(g):
            r0 = ROUTER_EXPERT_ROW0 + g * EXPERTS_PER_GROUP
            return lt[r0:r0 + EXPERTS_PER_GROUP]
        sel = group_slab(N_GROUPS - 1)
        for g in range(N_GROUPS - 2, -1, -1):
            sel = jnp.where(g_idx == float(g), group_slab(g), sel)
        eexp = jnp.exp(sel - jnp.max(sel, axis=0, keepdims=True))
        ep = eexp / jnp.sum(eexp, axis=0, keepdims=True)
        p1 = jnp.max(ep, axis=0, keepdims=True)
        i1 = jnp.min(jnp.where(ep == p1, rid8, big), axis=0, keepdims=True)
        m2 = rid8 != i1
        p2 = jnp.max(jnp.where(m2, ep, -1.0), axis=0, keepdims=True)
        i2 = jnp.min(jnp.where(m2 & (ep == p2), rid8, big), axis=0, keepdims=True)
        den = p1 + p2
        w1 = p1 / den * g_top
        w2 = p2 / den * g_top
        e1 = g_idx * float(EXPERTS_PER_GROUP) + i1
        e2 = g_idx * float(EXPERTS_PER_GROUP) + i2

        ii = lax.broadcasted_iota(jnp.int32, (LANES, LANES), 0)
        jj = lax.broadcasted_iota(jnp.int32, (LANES, LANES), 1)
        upper = jnp.where(ii < jj, 1.0, 0.0).astype(BF16)
        base = cnt_ref[...]
        ranks1, ranks2 = [], []
        for k in range(nblk):
            o1, o2 = onehot(e1, k), onehot(e2, k)
            r1 = jnp.dot(o1.astype(BF16), upper, preferred_element_type=F32)
            r2 = jnp.dot(o2.astype(BF16), upper, preferred_element_type=F32)
            c1 = jnp.sum(o1, axis=1, keepdims=True)
            c2 = jnp.sum(o2, axis=1, keepdims=True)
            ranks1.append(jnp.sum(o1 * (base + r1), axis=0, keepdims=True))
            ranks2.append(jnp.sum(o2 * (base + c1 + r2), axis=0, keepdims=True))
            base = base + c1 + c2
        cnt_ref[...] = base
        rank1 = jnp.concatenate(ranks1, axis=1)
        rank2 = jnp.concatenate(ranks2, axis=1)
        res_ref[j] = _rows8([e1, e2, w1, w2, rank1, rank2], T)

    @pl.when(phase == 1)
    def _():
        tm = float(EXPERT_TILE)
        cnt = cnt_ref[...]
        tiles = jnp.floor((cnt + (tm - 1.0)) * (1.0 / tm))
        ei = lax.broadcasted_iota(jnp.int32, (N_EXPERTS, N_EXPERTS), 0)
        ej = lax.broadcasted_iota(jnp.int32, (N_EXPERTS, N_EXPERTS), 1)
        lower = jnp.where(ej < ei, 1.0, 0.0).astype(BF16)
        first = jnp.dot(lower, tiles.astype(BF16), preferred_element_type=F32)
        starts = first * tm

        res = res_ref[j]
        e1, e2 = res[RES_E1:RES_E1 + 1], res[RES_E2:RES_E2 + 1]
        d1, d2 = [], []
        for k in range(nblk):
            blk_lanes = slice(k * LANES, (k + 1) * LANES)
            d1.append(res[RES_A:RES_A + 1, blk_lanes]
                      + jnp.sum(onehot(e1, k) * starts, axis=0, keepdims=True))
            d2.append(res[RES_B:RES_B + 1, blk_lanes]
                      + jnp.sum(onehot(e2, k) * starts, axis=0, keepdims=True))
        dest1 = jnp.concatenate(d1, axis=1)
        dest2 = jnp.concatenate(d2, axis=1)
        dest_ref[...] = _rows8([dest1, dest2], T).astype(jnp.int32)

        table = jnp.concatenate(
            [_rows8([e1, e2, res[RES_W1:RES_W1 + 1], res[RES_W2:RES_W2 + 1], dest1, dest2], T),
             jnp.zeros((ROUTER_LANES - SUBLANES, T), F32)], axis=0)
        slab_ref[...] = table.T

        @pl.when(j == 0)
        def _():
            lane_e = lax.broadcasted_iota(jnp.int32, (N_EXPERTS, LANES), 1).astype(F32)

            def as_row(col):
                return jnp.sum(jnp.where(rid_e == lane_e, col, 0.0), axis=0, keepdims=True)
            n_used = jnp.sum(tiles, axis=0, keepdims=True)
            padded = tiles * tm
            plan_ref[...] = _rows8([as_row(first), as_row(tiles), as_row(starts + cnt),
                                    as_row(padded - cnt), n_used], LANES).astype(jnp.int32)


def _router(logits_t):
    n = logits_t.shape[1]
    T = ROUTER_TILE
    nt = n // T
    return pl.pallas_call(
        _router_kernel,
        out_shape=(jax.ShapeDtypeStruct((n, ROUTER_LANES), F32),
                   jax.ShapeDtypeStruct((SUBLANES, n), jnp.int32),
                   jax.ShapeDtypeStruct((SUBLANES, LANES), jnp.int32)),
        grid=(2, nt),
        in_specs=[pl.BlockSpec((ROUTER_ROWS, T), lambda p, j: (0, j * (1 - p) + (nt - 1) * p))],
        out_specs=(pl.BlockSpec((T, ROUTER_LANES), lambda p, j: (j * p, 0)),
                   pl.BlockSpec((SUBLANES, T), lambda p, j: (0, j * p)),
                   pl.BlockSpec((SUBLANES, LANES), lambda p, j: (0, 0))),
        scratch_shapes=[pltpu.VMEM((nt, SUBLANES, T), F32),
                        pltpu.VMEM((N_EXPERTS, LANES), F32)],
        compiler_params=pltpu.CompilerParams(
            dimension_semantics=("arbitrary", "arbitrary"), vmem_limit_bytes=VMEM_LIMIT_BYTES),
        name="router",
    )(logits_t)


DISPATCH_TILE = 512
ROW_DMA_UNROLL = 8
PAD_UNITS = tuple(1 << b for b in reversed(range(EXPERT_TILE.bit_length() - 1)))


DISPATCH_LAG = 2
DISPATCH_SLOTS = DISPATCH_LAG + 2


def _dispatch_kernel(d0_ref, d1_ref, plan_ref, h2_hbm, xs_hbm,
                     stage, zbuf, isem, ssem, zsem):
    dt = DISPATCH_TILE
    ps = SUBLANES
    i = pl.program_id(0)
    nsteps = pl.num_programs(0)
    zrows = PAD_UNITS[0]

    def in_copy(blk, slot):
        src = h2_hbm.at[pl.ds(pl.multiple_of(blk * (dt * ps), dt * ps), dt * ps), :]
        return pltpu.make_async_copy(src, stage.at[slot], isem.at[slot])

    def wait_rows(slot):
        for _ in range(2):
            pltpu.make_async_copy(stage.at[slot], xs_hbm.at[pl.ds(0, dt * ps), :],
                                  ssem.at[slot]).wait()

    def pad_copy(start, unit):
        return pltpu.make_async_copy(zbuf.at[pl.ds(0, unit * ps), :],
                                     xs_hbm.at[pl.ds(pl.multiple_of(start * ps, ps), unit * ps), :],
                                     zsem)

    def pad_pass(do):
        def per_expert(e, carry):
            start = plan_ref[PLAN_PAD_START, e]
            npad = plan_ref[PLAN_PAD_N, e]
            for unit in PAD_UNITS:
                @pl.when((npad & unit) != 0)
                def _():
                    do(pad_copy(start + (npad & ~(2 * unit - 1)), unit))
            return carry
        lax.fori_loop(0, N_EXPERTS, per_expert, 0)

    def tail_pass(do):
        def per_unit(k, carry):
            do(pad_copy(k * zrows, zrows))
            return carry
        per_tile = EXPERT_TILE // zrows
        n_units = xs_hbm.shape[0] // (zrows * ps)
        lax.fori_loop(plan_ref[PLAN_N_USED, 0] * per_tile, n_units, per_unit, 0)

    slot = i % DISPATCH_SLOTS

    @pl.when(i == 0)
    def _():
        in_copy(0, 0).start()

        @pl.when(nsteps > 1)
        def _():
            in_copy(1, 1).start()

        zbuf[...] = jnp.zeros_like(zbuf)
        pad_pass(lambda cp: cp.start())
        tail_pass(lambda cp: cp.start())

    @pl.when(i >= DISPATCH_LAG)
    def _():
        wait_rows((i - DISPATCH_LAG) % DISPATCH_SLOTS)

    @pl.when(i + 2 < nsteps)
    def _():
        in_copy(i + 2, (i + 2) % DISPATCH_SLOTS).start()

    in_copy(i, slot).wait()
    base = i * dt
    src_ref = stage.at[slot]

    def body(r, carry):
        src = _token_rows(src_ref, r)
        for prio, d_ref in enumerate((d0_ref, d1_ref)):
            pltpu.make_async_copy(src, _token_rows(xs_hbm, d_ref[base + r]),
                                  ssem.at[slot]).start(priority=prio)
        return carry

    lax.fori_loop(0, dt, body, 0, unroll=ROW_DMA_UNROLL)

    @pl.when(i == nsteps - 1)
    def _():
        for back in range(DISPATCH_LAG - 1, -1, -1):
            @pl.when(i - back >= 0)
            def _():
                wait_rows((i - back) % DISPATCH_SLOTS)
        pad_pass(lambda cp: cp.wait())
        tail_pass(lambda cp: cp.wait())


def _dispatch(dest0, dest1, plan, h2t, p_rows):
    n = dest0.shape[0]
    dt = DISPATCH_TILE
    grid_spec = pltpu.PrefetchScalarGridSpec(
        num_scalar_prefetch=3,
        grid=(n // dt,),
        in_specs=[pl.BlockSpec(memory_space=pl.ANY)],
        out_specs=pl.BlockSpec(memory_space=pl.ANY),
        scratch_shapes=[pltpu.VMEM((DISPATCH_SLOTS, dt * SUBLANES, LANES), F32),
                        pltpu.VMEM((PAD_UNITS[0] * SUBLANES, LANES), F32),
                        pltpu.SemaphoreType.DMA((DISPATCH_SLOTS,)),
                        pltpu.SemaphoreType.DMA((DISPATCH_SLOTS,)),
                        pltpu.SemaphoreType.DMA(())],
    )
    return pl.pallas_call(
        _dispatch_kernel,
        out_shape=jax.ShapeDtypeStruct((p_rows * SUBLANES, LANES), F32),
        grid_spec=grid_spec,
        compiler_params=pltpu.CompilerParams(dimension_semantics=("arbitrary",)),
        name="dispatch",
    )(dest0, dest1, plan, h2t)


EXPERT_IN_SLOTS = 4
EXPERT_OUT_SLOTS = 3
MXU_COLS = 256


def _expert_kernel(plan_ref, xs_hbm, wg_ref, wu_ref, wd_ref, y_hbm,
                   xbuf, ybuf, wgb, wub, wdb, isem, osem):
    tm = EXPERT_TILE
    rows = tm * SUBLANES
    ni, no = EXPERT_IN_SLOTS, EXPERT_OUT_SLOTS
    e = pl.program_id(0)
    n_used = plan_ref[PLAN_N_USED, 0]
    n_mine = plan_ref[PLAN_N_TILE, e]
    n_tiles = y_hbm.shape[0] // rows

    def tile_rows(ref, g):
        return ref.at[pl.ds(pl.multiple_of(g * rows, rows), rows), :]

    def in_copy(g):
        return pltpu.make_async_copy(tile_rows(xs_hbm, g), xbuf.at[g % ni], isem.at[g % ni])

    def out_copy(g):
        return pltpu.make_async_copy(ybuf.at[g % no], tile_rows(y_hbm, g), osem.at[g % no])

    @pl.when(e == 0)
    def _():
        for g0 in range(ni - 1):
            @pl.when(g0 < n_used)
            def _():
                in_copy(g0).start()

    @pl.when(n_mine > 0)
    def _():
        wgb[...] = wg_ref[0].astype(BF16)
        wub[...] = wu_ref[0].astype(BF16)
        wdb[...] = wd_ref[0].astype(BF16)

    def tile(g, carry):
        in_copy(g).wait()

        @pl.when(g >= no)
        def _():
            out_copy(g - no).wait()

        @pl.when(g + ni - 1 < n_used)
        def _():
            in_copy(g + ni - 1).start()

        xb = _load_token_rows(xbuf.at[g % ni], tm).astype(BF16)
        a = jnp.dot(xb, wgb[...], preferred_element_type=F32)
        u = jnp.dot(xb, wub[...], preferred_element_type=F32)
        hid = (_silu(a) * u).astype(BF16)
        out = ybuf.at[g % no]
        per_piece = MXU_COLS // LANES
        for p in range(wdb.shape[1] // MXU_COLS):
            y = jnp.dot(hid, wdb[:, p * MXU_COLS:(p + 1) * MXU_COLS], preferred_element_type=F32)
            for q in range(per_piece):
                out[pl.ds(p * per_piece + q, tm, stride=SUBLANES), :] = y[:, q * LANES:(q + 1) * LANES]
        out_copy(g).start()
        return carry

    first = plan_ref[PLAN_FIRST_TILE, e]
    lax.fori_loop(first, first + n_mine, tile, 0)

    @pl.when(e == pl.num_programs(0) - 1)
    def _():
        for back in range(no, 0, -1):
            @pl.when(n_used >= back)
            def _():
                out_copy(n_used - back).wait()

        ybuf[0] = jnp.zeros((rows, LANES), F32)

        def zero_copy(g):
            return pltpu.make_async_copy(ybuf.at[0], tile_rows(y_hbm, g), osem.at[0])

        def fill(g, carry):
            zero_copy(g).start()
            return carry

        def drain(g, carry):
            zero_copy(g).wait()
            return carry

        lax.fori_loop(n_used, n_tiles, fill, 0)
        lax.fori_loop(n_used, n_tiles, drain, 0)


def _experts(plan, xs, wg, wu, wd):
    tm = EXPERT_TILE
    n_exp, d, de = wg.shape
    grid_spec = pltpu.PrefetchScalarGridSpec(
        num_scalar_prefetch=1,
        grid=(n_exp,),
        in_specs=[
            pl.BlockSpec(memory_space=pl.ANY),
            pl.BlockSpec((1, d, de), lambda e, *_: (e, 0, 0)),
            pl.BlockSpec((1, d, de), lambda e, *_: (e, 0, 0)),
            pl.BlockSpec((1, de, d), lambda e, *_: (e, 0, 0)),
        ],
        out_specs=pl.BlockSpec(memory_space=pl.ANY),
        scratch_shapes=[pltpu.VMEM((EXPERT_IN_SLOTS, tm * SUBLANES, LANES), F32),
                        pltpu.VMEM((EXPERT_OUT_SLOTS, tm * SUBLANES, LANES), F32),
                        pltpu.VMEM((d, de), BF16),
                        pltpu.VMEM((d, de), BF16),
                        pltpu.VMEM((de, d), BF16),
                        pltpu.SemaphoreType.DMA((EXPERT_IN_SLOTS,)),
                        pltpu.SemaphoreType.DMA((EXPERT_OUT_SLOTS,))],
    )
    return pl.pallas_call(
        _expert_kernel,
        out_shape=jax.ShapeDtypeStruct(xs.shape, F32),
        grid_spec=grid_spec,
        compiler_params=pltpu.CompilerParams(
            dimension_semantics=("arbitrary",),
            vmem_limit_bytes=VMEM_LIMIT_BYTES),
        name="experts",
    )(plan, xs, wg, wu, wd)


COMBINE_SLOTS = 3


def _combine_kernel(p0_ref, p1_ref, y_hbm, x1_ref, slab_ref, mod_ref, gfin_ref, o_ref,
                    *scratch):
    tm = COMBINE_TILE
    ns = COMBINE_SLOTS
    ahead = ns - 1
    ybufs, sem = scratch[:ns], scratch[ns]
    i = pl.program_id(0)
    nt = pl.num_programs(0)

    def row_copies(tile, slot, r):
        for j, p_ref in enumerate((p0_ref, p1_ref)):
            pltpu.make_async_copy(_token_rows(y_hbm, p_ref[tile * tm + r]),
                                  _token_rows(ybufs[slot].at[j], r),
                                  sem.at[slot]).start(priority=j)

    def wait_tile(slot):
        for j in range(2):
            pltpu.make_async_copy(y_hbm.at[pl.ds(0, tm * SUBLANES), :], ybufs[slot].at[j],
                                  sem.at[slot]).wait()

    @pl.when(i == 0)
    def _():
        for t0 in range(ahead):
            @pl.when(t0 < nt)
            def _():
                def body(r, carry):
                    row_copies(t0, t0, r)
                    return carry
                lax.fori_loop(0, tm, body, 0, unroll=ROW_DMA_UNROLL)

    n_chunks = SUBLANES
    batch = tm // n_chunks

    def step(slot):
        wait_tile(slot)
        nxt = jnp.minimum(i + ahead, nt - 1)
        nslot = (slot + ahead) % ns
        slab = slab_ref[...]
        w0 = slab[:, 2:3]
        w1 = slab[:, 3:4]
        gate_f = mod_ref[0][5:6]
        sq = jnp.zeros((tm, LANES), F32)
        for c in range(n_chunks):
            lanes = slice(c * LANES, (c + 1) * LANES)
            y0 = ybufs[slot].at[0][pl.ds(c, tm, stride=SUBLANES), :]
            y1 = ybufs[slot].at[1][pl.ds(c, tm, stride=SUBLANES), :]
            xo = x1_ref[:, lanes] + gate_f[:, lanes] * (w0 * y0 + w1 * y1)
            sq = sq + xo * xo
            o_ref[:, lanes] = xo
            for r in range(c * batch, (c + 1) * batch):
                row_copies(nxt, nslot, r)
        ms = jnp.sum(sq, axis=-1, keepdims=True) * (1.0 / (n_chunks * LANES))
        scale = lax.rsqrt(ms + EPS)
        for c in range(n_chunks):
            lanes = slice(c * LANES, (c + 1) * LANES)
            o_ref[:, lanes] = o_ref[:, lanes] * scale * gfin_ref[:, lanes]

    for slot in range(ns):
        @pl.when(i % ns == slot)
        def _():
            step(slot)

    @pl.when(i == nt - 1)
    def _():
        for k in range(ahead):
            for slot in range(ns):
                @pl.when((nt - 1 - k >= 0) & ((nt - 1 - k + ahead) % ns == slot))
                def _():
                    wait_tile(slot)


def _combine(p0, p1, y, x1, slab, mod, gfin, seq):
    n, d = x1.shape
    tm = COMBINE_TILE
    tiles_per_seq = seq // tm
    grid_spec = pltpu.PrefetchScalarGridSpec(
        num_scalar_prefetch=2,
        grid=(n // tm,),
        in_specs=[
            pl.BlockSpec(memory_space=pl.ANY),
            pl.BlockSpec((tm, d), lambda i, a, b: (i, 0)),
            pl.BlockSpec((tm, ROUTER_LANES), lambda i, a, b: (i, 0)),
            pl.BlockSpec((1, 6, d), lambda i, a, b: (i // tiles_per_seq, 0, 0)),
            pl.BlockSpec((1, d), lambda i, a, b: (0, 0)),
        ],
        out_specs=pl.BlockSpec((tm, d), lambda i, a, b: (i, 0)),
        scratch_shapes=([pltpu.VMEM((2, tm * SUBLANES, LANES), F32)] * COMBINE_SLOTS
                        + [pltpu.SemaphoreType.DMA((COMBINE_SLOTS,))]),
    )
    return pl.pallas_call(
        _combine_kernel,
        out_shape=jax.ShapeDtypeStruct((n, d), F32),
        grid_spec=grid_spec,
        compiler_params=pltpu.CompilerParams(
            dimension_semantics=("arbitrary",),
            vmem_limit_bytes=VMEM_LIMIT_BYTES),
        name="combine",
    )(p0, p1, y, x1, slab, mod, gfin)


def kernel(x, c, positions, ada_w, ada_b, norm_mix_g, norm_ffn_g, w_in, conv_w, conv_b,
           beta_ret, beta_conv, w_out, router_group_w, router_group_b, router_expert_w,
           router_expert_b, expert_w_gate, expert_w_up, expert_w_down, norm_final_g):
    bsz, seq, d = x.shape
    n = bsz * seq
    depth = ada_w.shape[0]
    assert depth == 1, "the combine kernel fuses the trunk's final RMSNorm (single layer)"
    W = RET_HEADS * HEAD_DIM

    cos128, sin128 = _rope_tables(positions)
    heads = jnp.arange(RET_HEADS, dtype=F32)
    lg = jnp.log1p(-jnp.exp2(-5.0 - heads))
    lgl = jnp.repeat(lg, HEAD_DIM).reshape(1, W)
    assert CONV_GROUP_DIM == HEAD_DIM, "conv groups and retention heads share the 64-lane block sums"
    blk_np = np.kron(np.eye(RET_HEADS // 2, dtype=np.float32),
                     np.ones((HEAD_DIM, HEAD_DIM), np.float32))
    blk = jnp.asarray(blk_np, dtype=BF16)

    l = 0
    mod = _adaln(c, ada_w[l], ada_b[l]).reshape(bsz, 6, d)

    gap = ROUTER_EXPERT_ROW0 - N_GROUPS
    tail = ROUTER_LANES - ROUTER_ROWS
    wr = jnp.concatenate([router_group_w[l].T, jnp.zeros((gap, d), F32),
                          router_expert_w[l].T, jnp.zeros((tail, d), F32)], axis=0)
    wr2 = jnp.concatenate(_split_bf16(wr), axis=0)
    br = jnp.concatenate([router_group_b[l], jnp.zeros((gap,), F32),
                          router_expert_b[l], jnp.zeros((tail,), F32)]).reshape(ROUTER_LANES, 1)

    x1, h2t, logits_t = _mixer(
        x, mod, cos128, sin128, norm_mix_g[l].reshape(1, d), w_in[l],
        conv_w[l], conv_b[l].reshape(1, W), beta_ret[l].reshape(1, W),
        beta_conv[l].reshape(1, W), w_out[l], norm_ffn_g[l].reshape(1, d),
        wr2, br, lg, lgl, blk)
    slab, dest, plan = _router(logits_t)
    p0, p1 = dest[0], dest[1]
    p_rows = 2 * n + N_EXPERTS * EXPERT_TILE

    xs = _dispatch(p0, p1, plan, h2t, p_rows)
    de = expert_w_gate.shape[-1]
    y = _experts(plan, xs,
                 expert_w_gate[l].reshape(N_EXPERTS, d, de),
                 expert_w_up[l].reshape(N_EXPERTS, d, de),
                 expert_w_down[l].reshape(N_EXPERTS, de, d))
    out = _combine(p0, p1, y, x1.reshape(n, d), slab, mod, norm_final_g.reshape(1, d), seq)
    return out.reshape(bsz, seq, d)
```

```python
import jax
import jax.numpy as jnp
import numpy as np
from jax import lax
from jax.experimental import pallas as pl
from jax.experimental.pallas import tpu as pltpu

F32 = jnp.float32
BF16 = jnp.bfloat16

CHUNK = 64
RET_HEADS = 8
HEAD_DIM = 64
CONV_GROUP_DIM = 64
ROPE_BASE = 10000.0
N_GROUPS = 4
EXPERTS_PER_GROUP = 8
N_EXPERTS = N_GROUPS * EXPERTS_PER_GROUP
EPS = 1e-6
GN_EPS = 1e-5

LANES = 128
SUBLANES = 8
VMEM_LIMIT_BYTES = 56 * 1024 * 1024

SEQ_TILE = 256
EXPERT_TILE = 256
COMBINE_TILE = 256
ROUTER_LANES = LANES
ROUTER_GROUP_ROW0 = 0
ROUTER_EXPERT_ROW0 = SUBLANES
ROUTER_ROWS = ROUTER_EXPERT_ROW0 + N_EXPERTS


def _silu(v):
    return v * (1.0 / (1.0 + jnp.exp(-v)))


def _split_bf16(v):
    hi = v.astype(BF16)
    return hi, (v - hi.astype(F32)).astype(BF16)


def _adaln_block(c_ref, w_ref, b_ref, o_ref):
    s_hi, s_lo = _split_bf16(_silu(c_ref[...]))
    w_hi, w_lo = _split_bf16(w_ref[...])
    o_ref[...] = (jnp.dot(s_hi, w_hi, preferred_element_type=F32)
                  + jnp.dot(s_lo, w_hi, preferred_element_type=F32)
                  + jnp.dot(s_hi, w_lo, preferred_element_type=F32) + b_ref[...])


def _rope_block(pos_ref, invf_ref, cos_ref, sin_ref):
    half = HEAD_DIM // 2
    per_row = LANES // half
    r = pos_ref.shape[0]
    ang = pos_ref[...].astype(F32) * invf_ref[...]
    lane = lax.broadcasted_iota(jnp.int32, (r, LANES), 1)
    quarter = lane // half
    sign = jnp.where(quarter % 2 == 0, -1.0, 1.0)
    for table, out_ref, scale in ((jnp.cos(ang), cos_ref, None), (jnp.sin(ang), sin_ref, sign)):
        rolled = [table] + [pltpu.roll(table, half * k, 1) for k in range(1, per_row)]
        for q in range(per_row):
            val = rolled[(0 - q) % per_row]
            for k in range(1, per_row):
                val = jnp.where(quarter == k, rolled[(k - q) % per_row], val)
            if scale is not None:
                val = val * scale
            out_ref[pl.ds(q, r, stride=per_row), :] = val


PROLOGUE_STEPS = 8


def _prologue_kernel(c_ref, w_ref, b_ref, pos_ref, invf_ref, mod_ref, cos_ref, sin_ref):
    _adaln_block(c_ref, w_ref, b_ref, mod_ref)
    _rope_block(pos_ref, invf_ref, cos_ref, sin_ref)


def _prologue(c, w, b, positions):
    bsz, d = c.shape
    n_mod = w.shape[1]
    n = positions.size
    half = HEAD_DIM // 2
    inv_freq = ROPE_BASE ** (-jnp.arange(0, HEAD_DIM, 2, dtype=F32) / HEAD_DIM)
    per_row = LANES // half
    rows = n // per_row
    pos_rep = jnp.broadcast_to(positions.reshape(n, 1), (n, half)).reshape(rows, LANES)
    invf = jnp.tile(inv_freq, per_row).reshape(1, LANES)
    steps = PROLOGUE_STEPS
    tn, tr = n_mod // steps, rows // steps
    assert tn % LANES == 0 and tr % SUBLANES == 0
    return pl.pallas_call(
        _prologue_kernel,
        out_shape=(jax.ShapeDtypeStruct((bsz, n_mod), F32),
                   jax.ShapeDtypeStruct((n, LANES), F32),
                   jax.ShapeDtypeStruct((n, LANES), F32)),
        grid=(steps,),
        in_specs=[pl.BlockSpec((bsz, d), lambda j: (0, 0)),
                  pl.BlockSpec((d, tn), lambda j: (0, j)),
                  pl.BlockSpec((1, tn), lambda j: (0, j)),
                  pl.BlockSpec((tr, LANES), lambda j: (j, 0)),
                  pl.BlockSpec((1, LANES), lambda j: (0, 0))],
        out_specs=(pl.BlockSpec((bsz, tn), lambda j: (0, j)),
                   pl.BlockSpec((tr * per_row, LANES), lambda j: (j, 0)),
                   pl.BlockSpec((tr * per_row, LANES), lambda j: (j, 0))),
        name="prologue",
    )(c, w, b.reshape(1, n_mod), pos_rep, invf)


def _load_token_rows(ref, rows):
    return jnp.concatenate(
        [ref[pl.ds(c, rows, stride=SUBLANES), :] for c in range(SUBLANES)], axis=1)


def _store_token_rows(ref, val, row0=0):
    rows = val.shape[0]
    for c in range(SUBLANES):
        ref[pl.ds(row0 * SUBLANES + c, rows, stride=SUBLANES), :] = val[:, c * LANES:(c + 1) * LANES]


def _token_rows(ref, row):
    return ref.at[pl.ds(pl.multiple_of(row * SUBLANES, SUBLANES), SUBLANES), :]


MIXER_SUBTILES = 2


MIXER_WIN, MIXER_WOUT = 6, 11


def _mixer_kernel(*refs):
    *refs, winb_ref, woutb_ref = refs
    state_ref, ubuf_ref = refs[-2:]

    @pl.when((pl.program_id(0) == 0) & (pl.program_id(1) == 0))
    def _():
        for src, dst in ((refs[MIXER_WIN], winb_ref), (refs[MIXER_WOUT], woutb_ref)):
            step = RET_HEADS * HEAD_DIM
            for c0 in range(0, src.shape[1], step):
                dst[:, c0:c0 + step] = src[:, c0:c0 + step].astype(BF16)

    @pl.when(pl.program_id(1) == 0)
    def _():
        state_ref[...] = jnp.zeros_like(state_ref)
        ubuf_ref[0:SUBLANES, :] = jnp.zeros((SUBLANES, ubuf_ref.shape[1]), F32)

    refs[MIXER_WIN], refs[MIXER_WOUT] = winb_ref, woutb_ref
    for sub in range(MIXER_SUBTILES):
        _mixer_tile(sub, *refs)


def _mixer_tile(sub, lg_ref, x_ref, mod_ref, cos_ref, sin_ref, gmix_ref, win_ref, convw_ref,
                convb_ref, bret_ref, bconv_ref, wout_ref, gffn_ref, wr_ref, br_ref, lgl_ref,
                blk_ref, x1_ref, h2_ref, logit_ref, state_ref, ubuf_ref):
    L = SEQ_TILE
    W = RET_HEADS * HEAD_DIM
    tile_rows = slice(sub * L, (sub + 1) * L)

    x = x_ref[0, tile_rows, :]
    mod = mod_ref[0]
    shift_m, scale_m, gate_m = mod[0:1], mod[1:2], mod[2:3]
    shift_f, scale_f = mod[3:4], mod[4:5]

    ms = jnp.mean(x * x, axis=-1, keepdims=True)
    h = x * lax.rsqrt(ms + EPS) * gmix_ref[...]
    h = h * (1.0 + scale_m) + shift_m
    hb = h.astype(BF16)

    def proj(i):
        return jnp.dot(hb, win_ref[:, i * W:(i + 1) * W], preferred_element_type=F32)

    cos = jnp.concatenate([cos_ref[tile_rows, :]] * 4, axis=1)
    sin = jnp.concatenate([sin_ref[tile_rows, :]] * 4, axis=1)
    lane_w = lax.broadcasted_iota(jnp.int32, (L, W), 1)
    first_half = (lane_w & (HEAD_DIM - 1)) < (HEAD_DIM // 2)

    def rot(t):
        partner = jnp.where(first_half, pltpu.roll(t, W - HEAD_DIM // 2, 1),
                            pltpu.roll(t, HEAD_DIM // 2, 1))
        return t * cos + partner * sin

    q = rot(proj(0))
    k = rot(proj(1)) * (HEAD_DIM ** -0.5)
    v = proj(2)
    vb = v.astype(BF16)
    kb = k.astype(BF16)

    lgl = lgl_ref[...]
    rowf = lax.broadcasted_iota(jnp.int32, (L, W), 0).astype(F32)
    qd = q * jnp.exp(lgl * (rowf + 1.0))
    kd = k * jnp.exp(lgl * (float(L - 1) - rowf))
    blk = blk_ref[...]
    HW = W // 2
    blk_f = blk.astype(F32)
    qdb = qd.astype(BF16)
    kdb = kd.astype(BF16)
    state_decay = jnp.exp(lgl * float(L))
    inter = []
    for hf in range(2):
        sl = slice(hf * HW, (hf + 1) * HW)
        st = state_ref[hf]
        inter.append(jnp.dot(qdb[:, sl], st.astype(BF16), preferred_element_type=F32))
        kv = lax.dot_general(kdb[:, sl], vb[:, sl], (((0,), (0,)), ((), ())),
                             preferred_element_type=F32)
        state_ref[hf] = st * state_decay[:, sl] + kv * blk_f
    y_inter = jnp.concatenate(inter, axis=1)

    def head_sums(t):
        tb = t.astype(BF16)
        return jnp.concatenate(
            [jnp.dot(tb[:, hf * HW:(hf + 1) * HW], blk, preferred_element_type=F32)
             for hf in range(2)], axis=1)

    ii = lax.broadcasted_iota(jnp.int32, (L, L), 0)
    jj = lax.broadcasted_iota(jnp.int32, (L, L), 1)
    dist = jnp.abs(ii - jj).astype(F32)
    allowed = (jj // CHUNK) <= (ii // CHUNK)
    lane_p = lax.broadcasted_iota(jnp.int32, (L, LANES), 1)
    lo_head = lane_p < HEAD_DIM
    pairs = []
    for p in range(RET_HEADS // 2):
        sl = slice(p * LANES, (p + 1) * LANES)
        qp, kp, vp = q[:, sl], kb[:, sl], vb[:, sl]
        ys = []
        for hh in range(2):
            head = 2 * p + hh
            keep = lo_head if hh == 0 else jnp.logical_not(lo_head)
            qh = jnp.where(keep, qp, 0.0).astype(BF16)
            sc = lax.dot_general(qh, kp, (((1,), (1,)), ((), ())),
                                 preferred_element_type=F32)
            decay = jnp.where(allowed, jnp.exp(lg_ref[head] * dist), 0.0)
            ys.append(jnp.dot((sc * decay).astype(BF16), vp, preferred_element_type=F32))
        pairs.append(jnp.where(lo_head, ys[0], ys[1]))
    y = jnp.concatenate(pairs, axis=1) + y_inter

    inv_hd = 1.0 / HEAD_DIM
    mu = head_sums(y) * inv_hd
    d = y - mu
    var = head_sums(d * d) * inv_hd
    g = proj(3)
    y_ret = _silu(g) * (d * lax.rsqrt(var + GN_EPS)) * bret_ref[...]

    b_gate = proj(4)
    u = proj(5) * proj(6)
    ubuf_ref[SUBLANES:SUBLANES + L, :] = u
    u1 = ubuf_ref[SUBLANES - 1:SUBLANES - 1 + L, :]
    u2 = ubuf_ref[SUBLANES - 2:SUBLANES - 2 + L, :]
    ubuf_ref[0:SUBLANES, :] = ubuf_ref[L:L + SUBLANES, :]
    cw = convw_ref[...]
    conv = u2 * cw[0:1] + u1 * cw[1:2] + u * cw[2:3] + convb_ref[...]
    yc = b_gate * conv
    msc = head_sums(yc * yc) * (1.0 / CONV_GROUP_DIM)
    y_conv = yc * lax.rsqrt(msc + EPS) * bconv_ref[...]

    mix = (jnp.dot(y_ret.astype(BF16), wout_ref[0:W, :], preferred_element_type=F32)
           + jnp.dot(y_conv.astype(BF16), wout_ref[W:2 * W, :], preferred_element_type=F32))
    x1 = x + gate_m * mix
    x1_ref[0, tile_rows, :] = x1

    ms2 = jnp.mean(x1 * x1, axis=-1, keepdims=True)
    h2 = x1 * lax.rsqrt(ms2 + EPS) * gffn_ref[...]
    h2 = h2 * (1.0 + scale_f) + shift_f
    _store_token_rows(h2_ref, h2, row0=sub * L)

    hi, lo = _split_bf16(h2)
    w2 = wr_ref[...]
    nt_dims = (((1,), (1,)), ((), ()))
    parts = (lax.dot_general(w2, hi, nt_dims, preferred_element_type=F32)
             + lax.dot_general(w2, lo, nt_dims, preferred_element_type=F32))
    logits_t = parts[:ROUTER_LANES] + parts[ROUTER_LANES:] + br_ref[...]
    logit_ref[:, tile_rows] = logits_t[:ROUTER_ROWS]


def _mixer(x, mod, cos128, sin128, gmix, win, convw, convb, bret, bconv, wout, gffn,
           wr2, br, lg, lgl, blk):
    bsz, seq, d = x.shape
    L = SEQ_TILE * MIXER_SUBTILES
    ns = seq // L
    W = RET_HEADS * HEAD_DIM
    n = bsz * seq
    const2 = lambda b, s: (0, 0)
    in_specs = [
        pl.BlockSpec(memory_space=pltpu.SMEM),
        pl.BlockSpec((1, L, d), lambda b, s: (b, s, 0)),
        pl.BlockSpec((1, 6, d), lambda b, s: (b, 0, 0)),
        pl.BlockSpec((L, LANES), lambda b, s: (b * ns + s, 0)),
        pl.BlockSpec((L, LANES), lambda b, s: (b * ns + s, 0)),
        pl.BlockSpec((1, d), const2),
        pl.BlockSpec(win.shape, const2, pipeline_mode=pl.Buffered(1)),
        pl.BlockSpec(convw.shape, const2),
        pl.BlockSpec((1, W), const2),
        pl.BlockSpec((1, W), const2),
        pl.BlockSpec((1, W), const2),
        pl.BlockSpec(wout.shape, const2, pipeline_mode=pl.Buffered(1)),
        pl.BlockSpec((1, d), const2),
        pl.BlockSpec(wr2.shape, const2),
        pl.BlockSpec((ROUTER_LANES, 1), const2),
        pl.BlockSpec((1, W), const2),
        pl.BlockSpec((W // 2, W // 2), const2),
    ]
    assert d == SUBLANES * LANES, "one token must fill exactly one (8, 128) f32 tile"
    out_shape = (jax.ShapeDtypeStruct((bsz, seq, d), F32),
                 jax.ShapeDtypeStruct((n * SUBLANES, LANES), F32),
                 jax.ShapeDtypeStruct((ROUTER_ROWS, n), F32))
    out_specs = (pl.BlockSpec((1, L, d), lambda b, s: (b, s, 0)),
                 pl.BlockSpec((L * SUBLANES, LANES), lambda b, s: (b * ns + s, 0)),
                 pl.BlockSpec((ROUTER_ROWS, L), lambda b, s: (0, b * ns + s)))
    return pl.pallas_call(
        _mixer_kernel,
        out_shape=out_shape,
        grid=(bsz, ns),
        in_specs=in_specs,
        out_specs=out_specs,
        scratch_shapes=[pltpu.VMEM((2, W // 2, W // 2), F32),
                        pltpu.VMEM((SEQ_TILE + 2 * SUBLANES, W), F32),
                        pltpu.VMEM(win.shape, BF16),
                        pltpu.VMEM(wout.shape, BF16)],
        compiler_params=pltpu.CompilerParams(
            dimension_semantics=("arbitrary", "arbitrary"),
            vmem_limit_bytes=VMEM_LIMIT_BYTES),
        name="mixer",
    )(lg, x, mod, cos128, sin128, gmix, win, convw, convb, bret, bconv, wout, gffn,
      wr2, br, lgl, blk)


ROUTER_TILE = 2048
RES_E1, RES_E2, RES_W1, RES_W2, RES_A, RES_B = range(6)
PLAN_FIRST_TILE, PLAN_N_TILE, PLAN_PAD_START, PLAN_PAD_N, PLAN_N_USED = range(5)


def _rows8(vals, width):
    rid = lax.broadcasted_iota(jnp.int32, (SUBLANES, width), 0)
    out = jnp.zeros((SUBLANES, width), F32)
    for r, v in enumerate(vals):
        out = jnp.where(rid == r, v, out)
    return out


def _router_kernel(lt_ref, slab_ref, dest_ref, plan_ref, res_ref, cnt_ref):
    T = ROUTER_TILE
    phase = pl.program_id(0)
    j = pl.program_id(1)
    nblk = T // LANES
    big = F32(1e9)
    rid8 =lax.broadcasted_iota(jnp.int32, (SUBLANES, T), 0).astype(F32)
    rid_e = lax.broadcasted_iota(jnp.int32, (N_EXPERTS, LANES), 0).astype(F32)

    def onehot(e_row, k):
        return jnp.where(rid_e == e_row[:, k * LANES:(k + 1) * LANES], 1.0, 0.0)

    @pl.when((phase == 0) & (j == 0))
    def _():
        cnt_ref[...] = jnp.zeros_like(cnt_ref)

    @pl.when(phase == 0)
    def _():
        lt = lt_ref[...]
        g_rows = lt[ROUTER_GROUP_ROW0:ROUTER_GROUP_ROW0 + SUBLANES]
        gvalid = rid8 < float(N_GROUPS)
        gm = jnp.where(gvalid, g_rows, F32(-jnp.inf))
        gexp = jnp.exp(gm - jnp.max(gm, axis=0, keepdims=True))
        gp = gexp / jnp.sum(gexp, axis=0, keepdims=True)
        g_top = jnp.max(gp, axis=0, keepdims=True)
        g_idx = jnp.min(jnp.where(gvalid & (gp == g_top), rid8, big), axis=0, keepdims=True)

        def group_slab(g):
            r0 = ROUTER_EXPERT_ROW0 + g * EXPERTS_PER_GROUP
            return lt[r0:r0 + EXPERTS_PER_GROUP]
        sel = group_slab(N_GROUPS - 1)
        for g in range(N_GROUPS - 2, -1, -1):
            sel = jnp.where(g_idx == float(g), group_slab(g), sel)
        eexp = jnp.exp(sel - jnp.max(sel, axis=0, keepdims=True))
        ep = eexp / jnp.sum(eexp, axis=0, keepdims=True)
        p1 = jnp.max(ep, axis=0, keepdims=True)
        i1 = jnp.min(jnp.where(ep == p1, rid8, big), axis=0, keepdims=True)
        m2 = rid8 != i1
        p2 = jnp.max(jnp.where(m2, ep, -1.0), axis=0, keepdims=True)
        i2 = jnp.min(jnp.where(m2 & (ep == p2), rid8, big), axis=0, keepdims=True)
        den = p1 + p2
        w1 = p1 / den * g_top
        w2 = p2 / den * g_top
        e1 = g_idx * float(EXPERTS_PER_GROUP) + i1
        e2 = g_idx * float(EXPERTS_PER_GROUP) + i2

        ii = lax.broadcasted_iota(jnp.int32, (LANES, LANES), 0)
        jj = lax.broadcasted_iota(jnp.int32, (LANES, LANES), 1)
        upper = jnp.where(ii < jj, 1.0, 0.0).astype(BF16)
        base = cnt_ref[...]
        ranks1, ranks2 = [], []
        for k in range(nblk):
            o1, o2 = onehot(e1, k), onehot(e2, k)
            r1 = jnp.dot(o1.astype(BF16), upper, preferred_element_type=F32)
            r2 = jnp.dot(o2.astype(BF16), upper, preferred_element_type=F32)
            c1 = jnp.sum(o1, axis=1, keepdims=True)
            c2 = jnp.sum(o2, axis=1, keepdims=True)
            ranks1.append(jnp.sum(o1 * (base + r1), axis=0, keepdims=True))
            ranks2.append(jnp.sum(o2 * (base + c1 + r2), axis=0, keepdims=True))
            base = base + c1 + c2
        cnt_ref[...] = base
        rank1 = jnp.concatenate(ranks1, axis=1)
        rank2 = jnp.concatenate(ranks2, axis=1)
        res_ref[j] = _rows8([e1, e2, w1, w2, rank1, rank2], T)

    @pl.when(phase == 1)
    def _():
        tm = float(EXPERT_TILE)
        cnt = cnt_ref[...]
        tiles = jnp.floor((cnt + (tm - 1.0)) * (1.0 / tm))
        ei = lax.broadcasted_iota(jnp.int32, (N_EXPERTS, N_EXPERTS), 0)
        ej = lax.broadcasted_iota(jnp.int32, (N_EXPERTS, N_EXPERTS), 1)
        lower = jnp.where(ej < ei, 1.0, 0.0).astype(BF16)
        first = jnp.dot(lower, tiles.astype(BF16), preferred_element_type=F32)
        starts = first * tm

        res = res_ref[j]
        e1, e2 = res[RES_E1:RES_E1 + 1], res[RES_E2:RES_E2 + 1]
        d1, d2 = [], []
        for k in range(nblk):
            blk_lanes = slice(k * LANES, (k + 1) * LANES)
            d1.append(res[RES_A:RES_A + 1, blk_lanes]
                      + jnp.sum(onehot(e1, k) * starts, axis=0, keepdims=True))
            d2.append(res[RES_B:RES_B + 1, blk_lanes]
                      + jnp.sum(onehot(e2, k) * starts, axis=0, keepdims=True))
        dest1 = jnp.concatenate(d1, axis=1)
        dest2 = jnp.concatenate(d2, axis=1)
        dest_ref[...] = _rows8([dest1, dest2], T).astype(jnp.int32)

        table = jnp.concatenate(
            [_rows8([e1, e2, res[RES_W1:RES_W1 + 1], res[RES_W2:RES_W2 + 1], dest1, dest2], T),
             jnp.zeros((ROUTER_LANES - SUBLANES, T), F32)], axis=0)
        slab_ref[...] = table.T

        @pl.when(j == 0)
        def _():
            lane_e = lax.broadcasted_iota(jnp.int32, (N_EXPERTS, LANES), 1).astype(F32)

            def as_row(col):
                return jnp.sum(jnp.where(rid_e == lane_e, col, 0.0), axis=0, keepdims=True)
            n_used = jnp.sum(tiles, axis=0, keepdims=True)
            padded = tiles * tm
            plan_ref[...] = _rows8([as_row(first), as_row(tiles), as_row(starts + cnt),
                                    as_row(padded - cnt), n_used], LANES).astype(jnp.int32)


def _router(logits_t):
    n = logits_t.shape[1]
    T = ROUTER_TILE
    nt = n // T
    return pl.pallas_call(
        _router_kernel,
        out_shape=(jax.ShapeDtypeStruct((n, ROUTER_LANES), F32),
                   jax.ShapeDtypeStruct((SUBLANES, n), jnp.int32),
                   jax.ShapeDtypeStruct((SUBLANES, LANES), jnp.int32)),
        grid=(2, nt),
        in_specs=[pl.BlockSpec((ROUTER_ROWS, T), lambda p, j: (0, j * (1 - p) + (nt - 1) * p))],
        out_specs=(pl.BlockSpec((T, ROUTER_LANES), lambda p, j: (j * p, 0)),
                   pl.BlockSpec((SUBLANES, T), lambda p, j: (0, j * p)),
                   pl.BlockSpec((SUBLANES, LANES), lambda p, j: (0, 0))),
        scratch_shapes=[pltpu.VMEM((nt, SUBLANES, T), F32),
                        pltpu.VMEM((N_EXPERTS, LANES), F32)],
        compiler_params=pltpu.CompilerParams(
            dimension_semantics=("arbitrary", "arbitrary"), vmem_limit_bytes=VMEM_LIMIT_BYTES),
        name="router",
    )(logits_t)


DISPATCH_TILE = 512
ROW_DMA_UNROLL = 8
PAD_UNITS = tuple(1 << b for b in reversed(range(EXPERT_TILE.bit_length() - 1)))


DISPATCH_LAG = 2
DISPATCH_SLOTS = DISPATCH_LAG + 2


def _dispatch_kernel(d0_ref, d1_ref, plan_ref, h2_hbm, xs_hbm,
                     stage, zbuf, isem, ssem, zsem):
    dt = DISPATCH_TILE
    ps = SUBLANES
    i = pl.program_id(0)
    nsteps = pl.num_programs(0)
    zrows = PAD_UNITS[0]

    def in_copy(blk, slot):
        src = h2_hbm.at[pl.ds(pl.multiple_of(blk * (dt * ps), dt * ps), dt * ps), :]
        return pltpu.make_async_copy(src, stage.at[slot], isem.at[slot])

    def wait_rows(slot):
        for _ in range(2):
            pltpu.make_async_copy(stage.at[slot], xs_hbm.at[pl.ds(0, dt * ps), :],
                                  ssem.at[slot]).wait()

    def pad_copy(start, unit):
        return pltpu.make_async_copy(zbuf.at[pl.ds(0, unit * ps), :],
                                     xs_hbm.at[pl.ds(pl.multiple_of(start * ps, ps), unit * ps), :],
                                     zsem)

    def pad_pass(do):
        def per_expert(e, carry):
            start = plan_ref[PLAN_PAD_START, e]
            npad = plan_ref[PLAN_PAD_N, e]
            for unit in PAD_UNITS:
                @pl.when((npad & unit) != 0)
                def _():
                    do(pad_copy(start + (npad & ~(2 * unit - 1)), unit))
            return carry
        lax.fori_loop(0, N_EXPERTS, per_expert, 0)

    def tail_pass(do):
        def per_unit(k, carry):
            do(pad_copy(k * zrows, zrows))
            return carry
        per_tile = EXPERT_TILE // zrows
        n_units = xs_hbm.shape[0] // (zrows * ps)
        lax.fori_loop(plan_ref[PLAN_N_USED, 0] * per_tile, n_units, per_unit, 0)

    slot = i % DISPATCH_SLOTS

    @pl.when(i == 0)
    def _():
        in_copy(0, 0).start()

        @pl.when(nsteps > 1)
        def _():
            in_copy(1, 1).start()

        zbuf[...] = jnp.zeros_like(zbuf)
        pad_pass(lambda cp: cp.start())
        tail_pass(lambda cp: cp.start())

    @pl.when(i >= DISPATCH_LAG)
    def _():
        wait_rows((i - DISPATCH_LAG) % DISPATCH_SLOTS)

    @pl.when(i + 2 < nsteps)
    def _():
        in_copy(i + 2, (i + 2) % DISPATCH_SLOTS).start()

    in_copy(i, slot).wait()
    base = i * dt
    src_ref = stage.at[slot]

    def body(r, carry):
        src = _token_rows(src_ref, r)
        for prio, d_ref in enumerate((d0_ref, d1_ref)):
            pltpu.make_async_copy(src, _token_rows(xs_hbm, d_ref[base + r]),
                                  ssem.at[slot]).start(priority=prio)
        return carry

    lax.fori_loop(0, dt, body, 0, unroll=ROW_DMA_UNROLL)

    @pl.when(i == nsteps - 1)
    def _():
        for back in range(DISPATCH_LAG - 1, -1, -1):
            @pl.when(i - back >= 0)
            def _():
                wait_rows((i - back) % DISPATCH_SLOTS)
        pad_pass(lambda cp: cp.wait())
        tail_pass(lambda cp: cp.wait())


def _dispatch(dest0, dest1, plan, h2t, p_rows):
    n = dest0.shape[0]
    dt = DISPATCH_TILE
    grid_spec = pltpu.PrefetchScalarGridSpec(
        num_scalar_prefetch=3,
        grid=(n // dt,),
        in_specs=[pl.BlockSpec(memory_space=pl.ANY)],
        out_specs=pl.BlockSpec(memory_space=pl.ANY),
        scratch_shapes=[pltpu.VMEM((DISPATCH_SLOTS, dt * SUBLANES, LANES), F32),
                        pltpu.VMEM((PAD_UNITS[0] * SUBLANES, LANES), F32),
                        pltpu.SemaphoreType.DMA((DISPATCH_SLOTS,)),
                        pltpu.SemaphoreType.DMA((DISPATCH_SLOTS,)),
                        pltpu.SemaphoreType.DMA(())],
    )
    return pl.pallas_call(
        _dispatch_kernel,
        out_shape=jax.ShapeDtypeStruct((p_rows * SUBLANES, LANES), F32),
        grid_spec=grid_spec,
        compiler_params=pltpu.CompilerParams(dimension_semantics=("arbitrary",)),
        name="dispatch",
    )(dest0, dest1, plan, h2t)


EXPERT_IN_SLOTS = 4
EXPERT_OUT_SLOTS = 3
MXU_COLS = 256


def _expert_kernel(plan_ref, xs_hbm, wg_ref, wu_ref, wd_ref, y_hbm,
                   xbuf, ybuf, wgb, wub, wdb, isem, osem):
    tm = EXPERT_TILE
    rows = tm * SUBLANES
    ni, no = EXPERT_IN_SLOTS, EXPERT_OUT_SLOTS
    e = pl.program_id(0)
    n_used = plan_ref[PLAN_N_USED, 0]
    n_mine = plan_ref[PLAN_N_TILE, e]
    n_tiles = y_hbm.shape[0] // rows

    def tile_rows(ref, g):
        return ref.at[pl.ds(pl.multiple_of(g * rows, rows), rows), :]

    def in_copy(g):
        return pltpu.make_async_copy(tile_rows(xs_hbm, g), xbuf.at[g % ni], isem.at[g % ni])

    def out_copy(g):
        return pltpu.make_async_copy(ybuf.at[g % no], tile_rows(y_hbm, g), osem.at[g % no])

    @pl.when(e == 0)
    def _():
        for g0 in range(ni - 1):
            @pl.when(g0 < n_used)
            def _():
                in_copy(g0).start()

    @pl.when(n_mine > 0)
    def _():
        wgb[...] = wg_ref[0].astype(BF16)
        wub[...] = wu_ref[0].astype(BF16)
        wdb[...] = wd_ref[0].astype(BF16)

    def tile(g, carry):
        in_copy(g).wait()

        @pl.when(g >= no)
        def _():
            out_copy(g - no).wait()

        @pl.when(g + ni - 1 < n_used)
        def _():
            in_copy(g + ni - 1).start()

        xb = _load_token_rows(xbuf.at[g % ni], tm).astype(BF16)
        a = jnp.dot(xb, wgb[...], preferred_element_type=F32)
        u = jnp.dot(xb, wub[...], preferred_element_type=F32)
        hid = (_silu(a) * u).astype(BF16)
        out = ybuf.at[g % no]
        per_piece = MXU_COLS // LANES
        for p in range(wdb.shape[1] // MXU_COLS):
            y = jnp.dot(hid, wdb[:, p * MXU_COLS:(p + 1) * MXU_COLS], preferred_element_type=F32)
            for q in range(per_piece):
                out[pl.ds(p * per_piece + q, tm, stride=SUBLANES), :] = y[:, q * LANES:(q + 1) * LANES]
        out_copy(g).start()
        return carry

    first = plan_ref[PLAN_FIRST_TILE, e]
    lax.fori_loop(first, first + n_mine, tile, 0)

    @pl.when(e == pl.num_programs(0) - 1)
    def _():
        for back in range(no, 0, -1):
            @pl.when(n_used >= back)
            def _():
                out_copy(n_used - back).wait()

        ybuf[0] = jnp.zeros((rows, LANES), F32)

        def zero_copy(g):
            return pltpu.make_async_copy(ybuf.at[0], tile_rows(y_hbm, g), osem.at[0])

        def fill(g, carry):
            zero_copy(g).start()
            return carry

        def drain(g, carry):
            zero_copy(g).wait()
            return carry

        lax.fori_loop(n_used, n_tiles, fill, 0)
        lax.fori_loop(n_used, n_tiles, drain, 0)


def _experts(plan, xs, wg, wu, wd):
    tm = EXPERT_TILE
    n_exp, d, de = wg.shape
    grid_spec = pltpu.PrefetchScalarGridSpec(
        num_scalar_prefetch=1,
        grid=(n_exp,),
        in_specs=[
            pl.BlockSpec(memory_space=pl.ANY),
            pl.BlockSpec((1, d, de), lambda e, *_: (e, 0, 0)),
            pl.BlockSpec((1, d, de), lambda e, *_: (e, 0, 0)),
            pl.BlockSpec((1, de, d), lambda e, *_: (e, 0, 0)),
        ],
        out_specs=pl.BlockSpec(memory_space=pl.ANY),
        scratch_shapes=[pltpu.VMEM((EXPERT_IN_SLOTS, tm * SUBLANES, LANES), F32),
                        pltpu.VMEM((EXPERT_OUT_SLOTS, tm * SUBLANES, LANES), F32),
                        pltpu.VMEM((d, de), BF16),
                        pltpu.VMEM((d, de), BF16),
                        pltpu.VMEM((de, d), BF16),
                        pltpu.SemaphoreType.DMA((EXPERT_IN_SLOTS,)),
                        pltpu.SemaphoreType.DMA((EXPERT_OUT_SLOTS,))],
    )
    return pl.pallas_call(
        _expert_kernel,
        out_shape=jax.ShapeDtypeStruct(xs.shape, F32),
        grid_spec=grid_spec,
        compiler_params=pltpu.CompilerParams(
            dimension_semantics=("arbitrary",),
            vmem_limit_bytes=VMEM_LIMIT_BYTES),
        name="experts",
    )(plan, xs, wg, wu, wd)


COMBINE_SLOTS = 3


def _combine_kernel(p0_ref, p1_ref, y_hbm, x1_ref, slab_ref, mod_ref, gfin_ref, o_ref,
                    *scratch):
    tm = COMBINE_TILE
    ns = COMBINE_SLOTS
    ahead = ns - 1
    ybufs, sem = scratch[:ns], scratch[ns]
    i = pl.program_id(0)
    nt = pl.num_programs(0)

    def row_copies(tile, slot, r):
        for j, p_ref in enumerate((p0_ref, p1_ref)):
            pltpu.make_async_copy(_token_rows(y_hbm, p_ref[tile * tm + r]),
                                  _token_rows(ybufs[slot].at[j], r),
                                  sem.at[slot]).start(priority=j)

    def wait_tile(slot):
        for j in range(2):
            pltpu.make_async_copy(y_hbm.at[pl.ds(0, tm * SUBLANES), :], ybufs[slot].at[j],
                                  sem.at[slot]).wait()

    @pl.when(i == 0)
    def _():
        for t0 in range(ahead):
            @pl.when(t0 < nt)
            def _():
                def body(r, carry):
                    row_copies(t0, t0, r)
                    return carry
                lax.fori_loop(0, tm, body, 0, unroll=ROW_DMA_UNROLL)

    n_chunks = SUBLANES
    batch = tm // n_chunks

    def step(slot):
        wait_tile(slot)
        nxt = jnp.minimum(i + ahead, nt - 1)
        nslot = (slot + ahead) % ns
        slab = slab_ref[...]
        w0 = slab[:, 2:3]
        w1 = slab[:, 3:4]
        gate_f = mod_ref[0][5:6]
        sq = jnp.zeros((tm, LANES), F32)
        for c in range(n_chunks):
            lanes = slice(c * LANES, (c + 1) * LANES)
            y0 = ybufs[slot].at[0][pl.ds(c, tm, stride=SUBLANES), :]
            y1 = ybufs[slot].at[1][pl.ds(c, tm, stride=SUBLANES), :]
            xo = x1_ref[:, lanes] + gate_f[:, lanes] * (w0 * y0 + w1 * y1)
            sq = sq + xo * xo
            o_ref[:, lanes] = xo
            for r in range(c * batch, (c + 1) * batch):
                row_copies(nxt, nslot, r)
        ms = jnp.sum(sq, axis=-1, keepdims=True) * (1.0 / (n_chunks * LANES))
        scale = lax.rsqrt(ms + EPS)
        for c in range(n_chunks):
            lanes = slice(c * LANES, (c + 1) * LANES)
            o_ref[:, lanes] = o_ref[:, lanes] * scale * gfin_ref[:, lanes]

    for slot in range(ns):
        @pl.when(i % ns == slot)
        def _():
            step(slot)

    @pl.when(i == nt - 1)
    def _():
        for k in range(ahead):
            for slot in range(ns):
                @pl.when((nt - 1 - k >= 0) & ((nt - 1 - k + ahead) % ns == slot))
                def _():
                    wait_tile(slot)


def _combine(p0, p1, y, x1, slab, mod, gfin, seq):
    n, d = x1.shape
    tm = COMBINE_TILE
    tiles_per_seq = seq // tm
    grid_spec = pltpu.PrefetchScalarGridSpec(
        num_scalar_prefetch=2,
        grid=(n // tm,),
        in_specs=[
            pl.BlockSpec(memory_space=pl.ANY),
            pl.BlockSpec((tm, d), lambda i, a, b: (i, 0)),
            pl.BlockSpec((tm, ROUTER_LANES), lambda i, a, b: (i, 0)),
            pl.BlockSpec((1, 6, d), lambda i, a, b: (i // tiles_per_seq, 0, 0)),
            pl.BlockSpec((1, d), lambda i, a, b: (0, 0)),
        ],
        out_specs=pl.BlockSpec((tm, d), lambda i, a, b: (i, 0)),
        scratch_shapes=([pltpu.VMEM((2, tm * SUBLANES, LANES), F32)] * COMBINE_SLOTS
                        + [pltpu.SemaphoreType.DMA((COMBINE_SLOTS,))]),
    )
    return pl.pallas_call(
        _combine_kernel,
        out_shape=jax.ShapeDtypeStruct((n, d), F32),
        grid_spec=grid_spec,
        compiler_params=pltpu.CompilerParams(
            dimension_semantics=("arbitrary",),
            vmem_limit_bytes=VMEM_LIMIT_BYTES),
        name="combine",
    )(p0, p1, y, x1, slab, mod, gfin)


def kernel(x, c, positions, ada_w, ada_b, norm_mix_g, norm_ffn_g, w_in, conv_w, conv_b,
           beta_ret, beta_conv, w_out, router_group_w, router_group_b, router_expert_w,
           router_expert_b, expert_w_gate, expert_w_up, expert_w_down, norm_final_g):
    bsz, seq, d = x.shape
    n = bsz * seq
    depth = ada_w.shape[0]
    assert depth == 1, "the combine kernel fuses the trunk's final RMSNorm (single layer)"
    W = RET_HEADS * HEAD_DIM

    l = 0
    mod, cos128, sin128 = _prologue(c, ada_w[l], ada_b[l], positions)
    mod = mod.reshape(bsz, 6, d)
    heads = jnp.arange(RET_HEADS, dtype=F32)
    lg = jnp.log1p(-jnp.exp2(-5.0 - heads))
    lgl = jnp.repeat(lg, HEAD_DIM).reshape(1, W)
    assert CONV_GROUP_DIM == HEAD_DIM, "conv groups and retention heads share the 64-lane block sums"
    blk_np = np.kron(np.eye(RET_HEADS // 2, dtype=np.float32),
                     np.ones((HEAD_DIM, HEAD_DIM), np.float32))
    blk = jnp.asarray(blk_np, dtype=BF16)

    gap = ROUTER_EXPERT_ROW0 - N_GROUPS
    tail = ROUTER_LANES - ROUTER_ROWS
    wr = jnp.concatenate([router_group_w[l].T, jnp.zeros((gap, d), F32),
                          router_expert_w[l].T, jnp.zeros((tail, d), F32)], axis=0)
    wr2 = jnp.concatenate(_split_bf16(wr), axis=0)
    br = jnp.concatenate([router_group_b[l], jnp.zeros((gap,), F32),
                          router_expert_b[l], jnp.zeros((tail,), F32)]).reshape(ROUTER_LANES, 1)

    x1, h2t, logits_t = _mixer(
        x, mod, cos128, sin128, norm_mix_g[l].reshape(1, d), w_in[l],
        conv_w[l], conv_b[l].reshape(1, W), beta_ret[l].reshape(1, W),
        beta_conv[l].reshape(1, W), w_out[l], norm_ffn_g[l].reshape(1, d),
        wr2, br, lg, lgl, blk)
    slab, dest, plan = _router(logits_t)
    p0, p1 = dest[0], dest[1]
    p_rows = 2 * n + N_EXPERTS * EXPERT_TILE

    xs = _dispatch(p0, p1, plan, h2t, p_rows)
    de = expert_w_gate.shape[-1]
    y = _experts(plan, xs,
                 expert_w_gate[l].reshape(N_EXPERTS, d, de),
                 expert_w_up[l].reshape(N_EXPERTS, d, de),
                 expert_w_down[l].reshape(N_EXPERTS, de, d))
    out = _combine(p0, p1, y, x1.reshape(n, d), slab, mod, norm_final_g.reshape(1, d), seq)
    return out.reshape(bsz, seq, d)
```

```python
import jax
import jax.numpy as jnp
import numpy as np
from jax import lax
from jax.experimental import pallas as pl
from jax.experimental.pallas import tpu as pltpu

F32 = jnp.float32
BF16 = jnp.bfloat16

CHUNK = 64
RET_HEADS = 8
HEAD_DIM = 64
CONV_GROUP_DIM = 64
ROPE_BASE = 10000.0
N_GROUPS = 4
EXPERTS_PER_GROUP = 8
N_EXPERTS = N_GROUPS * EXPERTS_PER_GROUP
EPS = 1e-6
GN_EPS = 1e-5

LANES = 128
SUBLANES = 8
VMEM_LIMIT_BYTES = 56 * 1024 * 1024

SEQ_TILE = 256
EXPERT_TILE = 256
COMBINE_TILE = 256
ROUTER_LANES = LANES
ROUTER_GROUP_ROW0 = 0
ROUTER_EXPERT_ROW0 = SUBLANES
ROUTER_ROWS = ROUTER_EXPERT_ROW0 + N_EXPERTS


def _silu(v):
    return v * (1.0 / (1.0 + jnp.exp(-v)))


def _split_bf16(v):
    hi = v.astype(BF16)
    return hi, (v - hi.astype(F32)).astype(BF16)


def _adaln_block(c_ref, w_ref, b_ref, o_ref):
    s_hi, s_lo = _split_bf16(_silu(c_ref[...]))
    w_hi, w_lo = _split_bf16(w_ref[...])
    o_ref[...] = (jnp.dot(s_hi, w_hi, preferred_element_type=F32)
                  + jnp.dot(s_lo, w_hi, preferred_element_type=F32)
                  + jnp.dot(s_hi, w_lo, preferred_element_type=F32) + b_ref[...])


def _rope_block(pos_ref, invf_ref, cos_ref, sin_ref):
    half = HEAD_DIM // 2
    per_row = LANES // half
    r = pos_ref.shape[0]
    ang = pos_ref[...].astype(F32) * invf_ref[...]
    lane = lax.broadcasted_iota(jnp.int32, (r, LANES), 1)
    quarter = lane // half
    sign = jnp.where(quarter % 2 == 0, -1.0, 1.0)
    for table, out_ref, scale in ((jnp.cos(ang), cos_ref, None), (jnp.sin(ang), sin_ref, sign)):
        rolled = [table] + [pltpu.roll(table, half * k, 1) for k in range(1, per_row)]
        for q in range(per_row):
            val = rolled[(0 - q) % per_row]
            for k in range(1, per_row):
                val = jnp.where(quarter == k, rolled[(k - q) % per_row], val)
            if scale is not None:
                val = val * scale
            out_ref[pl.ds(q, r, stride=per_row), :] = val


PROLOGUE_STEPS = 8


def _prologue_kernel(c_ref, w_ref, b_ref, pos_ref, invf_ref, mod_ref, cos_ref, sin_ref):
    _adaln_block(c_ref, w_ref, b_ref, mod_ref)
    _rope_block(pos_ref, invf_ref, cos_ref, sin_ref)


def _prologue(c, w, b, positions):
    bsz, d = c.shape
    n_mod = w.shape[1]
    n = positions.size
    half = HEAD_DIM // 2
    inv_freq = ROPE_BASE ** (-jnp.arange(0, HEAD_DIM, 2, dtype=F32) / HEAD_DIM)
    per_row = LANES // half
    rows = n // per_row
    pos_rep = jnp.broadcast_to(positions.reshape(n, 1), (n, half)).reshape(rows, LANES)
    invf = jnp.tile(inv_freq, per_row).reshape(1, LANES)
    steps = PROLOGUE_STEPS
    tn, tr = n_mod // steps, rows // steps
    assert tn % LANES == 0 and tr % SUBLANES == 0
    return pl.pallas_call(
        _prologue_kernel,
        out_shape=(jax.ShapeDtypeStruct((bsz, n_mod), F32),
                   jax.ShapeDtypeStruct((n, LANES), F32),
                   jax.ShapeDtypeStruct((n, LANES), F32)),
        grid=(steps,),
        in_specs=[pl.BlockSpec((bsz, d), lambda j: (0, 0)),
                  pl.BlockSpec((d, tn), lambda j: (0, j)),
                  pl.BlockSpec((1, tn), lambda j: (0, j)),
                  pl.BlockSpec((tr, LANES), lambda j: (j, 0)),
                  pl.BlockSpec((1, LANES), lambda j: (0, 0))],
        out_specs=(pl.BlockSpec((bsz, tn), lambda j: (0, j)),
                   pl.BlockSpec((tr * per_row, LANES), lambda j: (j, 0)),
                   pl.BlockSpec((tr * per_row, LANES), lambda j: (j, 0))),
        name="prologue",
    )(c, w, b.reshape(1, n_mod), pos_rep, invf)


def _load_token_rows(ref, rows):
    return jnp.concatenate(
        [ref[pl.ds(c, rows, stride=SUBLANES), :] for c in range(SUBLANES)], axis=1)


def _store_token_rows(ref, val, row0=0):
    rows = val.shape[0]
    for c in range(SUBLANES):
        ref[pl.ds(row0 * SUBLANES + c, rows, stride=SUBLANES), :] = val[:, c * LANES:(c + 1) * LANES]


def _token_rows(ref, row):
    return ref.at[pl.ds(pl.multiple_of(row * SUBLANES, SUBLANES), SUBLANES), :]


MIXER_SUBTILES = 2


MIXER_WIN, MIXER_WOUT = 6, 11


def _mixer_kernel(*refs):
    *refs, winb_ref, woutb_ref, dec_ref, qdec_ref, kdec_ref = refs
    state_ref, ubuf_ref = refs[-2:]
    L = SEQ_TILE

    @pl.when((pl.program_id(0) == 0) & (pl.program_id(1) == 0))
    def _():
        for src, dst in ((refs[MIXER_WIN], winb_ref), (refs[MIXER_WOUT], woutb_ref)):
            step = RET_HEADS * HEAD_DIM
            for c0 in range(0, src.shape[1], step):
                dst[:, c0:c0 + step] = src[:, c0:c0 + step].astype(BF16)
        lg_ref, lgl_ref = refs[0], refs[15]
        ii = lax.broadcasted_iota(jnp.int32, (L, L), 0)
        jj = lax.broadcasted_iota(jnp.int32, (L, L), 1)
        dist = jnp.abs(ii - jj).astype(F32)
        allowed = (jj // CHUNK) <= (ii // CHUNK)
        for head in range(RET_HEADS):
            dec_ref[head] = jnp.where(allowed, jnp.exp(lg_ref[head] * dist), 0.0)
        rowf = lax.broadcasted_iota(jnp.int32, qdec_ref.shape, 0).astype(F32)
        qdec_ref[...] = jnp.exp(lgl_ref[...] * (rowf + 1.0))
        kdec_ref[...] = jnp.exp(lgl_ref[...] * (float(L - 1) - rowf))

    @pl.when(pl.program_id(1) == 0)
    def _():
        state_ref[...] = jnp.zeros_like(state_ref)
        ubuf_ref[0:SUBLANES, :] = jnp.zeros((SUBLANES, ubuf_ref.shape[1]), F32)

    refs[MIXER_WIN], refs[MIXER_WOUT] = winb_ref, woutb_ref
    for sub in range(MIXER_SUBTILES):
        _mixer_tile(sub, *refs, dec_ref, qdec_ref, kdec_ref)


def _mixer_tile(sub, lg_ref, x_ref, mod_ref, cos_ref, sin_ref, gmix_ref, win_ref, convw_ref,
                convb_ref, bret_ref, bconv_ref, wout_ref, gffn_ref, wr_ref, br_ref, lgl_ref,
                blk_ref, x1_ref, h2_ref, logit_ref, state_ref, ubuf_ref,
                dec_ref, qdec_ref, kdec_ref):
    L = SEQ_TILE
    W = RET_HEADS * HEAD_DIM
    tile_rows = slice(sub * L, (sub + 1) * L)

    x = x_ref[0, tile_rows, :]
    mod = mod_ref[0]
    shift_m, scale_m, gate_m = mod[0:1], mod[1:2], mod[2:3]
    shift_f, scale_f = mod[3:4], mod[4:5]

    ms = jnp.mean(x * x, axis=-1, keepdims=True)
    h = x * lax.rsqrt(ms + EPS) * gmix_ref[...]
    h = h * (1.0 + scale_m) + shift_m
    hb = h.astype(BF16)

    def proj(i):
        return jnp.dot(hb, win_ref[:, i * W:(i + 1) * W], preferred_element_type=F32)

    cos = jnp.concatenate([cos_ref[tile_rows, :]] * 4, axis=1)
    sin = jnp.concatenate([sin_ref[tile_rows, :]] * 4, axis=1)
    lane_w = lax.broadcasted_iota(jnp.int32, (L, W), 1)
    first_half = (lane_w & (HEAD_DIM - 1)) < (HEAD_DIM // 2)

    def rot(t):
        partner = jnp.where(first_half, pltpu.roll(t, W - HEAD_DIM // 2, 1),
                            pltpu.roll(t, HEAD_DIM // 2, 1))
        return t * cos + partner * sin

    q = rot(proj(0))
    k = rot(proj(1)) * (HEAD_DIM ** -0.5)
    v = proj(2)
    vb = v.astype(BF16)
    kb = k.astype(BF16)

    lgl = lgl_ref[...]
    qd = q * qdec_ref[...]
    kd = k * kdec_ref[...]
    blk = blk_ref[...]
    HW = W // 2
    blk_f = blk.astype(F32)
    qdb = qd.astype(BF16)
    kdb = kd.astype(BF16)
    state_decay = jnp.exp(lgl * float(L))
    inter = []
    for hf in range(2):
        sl = slice(hf * HW, (hf + 1) * HW)
        st = state_ref[hf]
        inter.append(jnp.dot(qdb[:, sl], st.astype(BF16), preferred_element_type=F32))
        kv = lax.dot_general(kdb[:, sl], vb[:, sl], (((0,), (0,)), ((), ())),
                             preferred_element_type=F32)
        state_ref[hf] = st * state_decay[:, sl] + kv * blk_f
    y_inter = jnp.concatenate(inter, axis=1)

    def head_sums(t):
        tb = t.astype(BF16)
        return jnp.concatenate(
            [jnp.dot(tb[:, hf * HW:(hf + 1) * HW], blk, preferred_element_type=F32)
             for hf in range(2)], axis=1)

    lane_p = lax.broadcasted_iota(jnp.int32, (L, LANES), 1)
    lo_head = lane_p < HEAD_DIM
    pairs = []
    for p in range(RET_HEADS // 2):
        sl = slice(p * LANES, (p + 1) * LANES)
        qp, kp, vp = q[:, sl], kb[:, sl], vb[:, sl]
        ys = []
        for hh in range(2):
            head = 2 * p + hh
            keep = lo_head if hh == 0 else jnp.logical_not(lo_head)
            qh = jnp.where(keep, qp, 0.0).astype(BF16)
            sc = lax.dot_general(qh, kp, (((1,), (1,)), ((), ())),
                                 preferred_element_type=F32)
            ys.append(jnp.dot((sc * dec_ref[head]).astype(BF16), vp, preferred_element_type=F32))
        pairs.append(jnp.where(lo_head, ys[0], ys[1]))
    y = jnp.concatenate(pairs, axis=1) + y_inter

    inv_hd = 1.0 / HEAD_DIM
    mu = head_sums(y) * inv_hd
    d = y - mu
    var = head_sums(d * d) * inv_hd
    g = proj(3)
    y_ret = _silu(g) * (d * lax.rsqrt(var + GN_EPS)) * bret_ref[...]

    b_gate = proj(4)
    u = proj(5) * proj(6)
    ubuf_ref[SUBLANES:SUBLANES + L, :] = u
    u1 = ubuf_ref[SUBLANES - 1:SUBLANES - 1 + L, :]
    u2 = ubuf_ref[SUBLANES - 2:SUBLANES - 2 + L, :]
    ubuf_ref[0:SUBLANES, :] = ubuf_ref[L:L + SUBLANES, :]
    cw = convw_ref[...]
    conv = u2 * cw[0:1] + u1 * cw[1:2] + u * cw[2:3] + convb_ref[...]
    yc = b_gate * conv
    msc = head_sums(yc * yc) * (1.0 / CONV_GROUP_DIM)
    y_conv = yc * lax.rsqrt(msc + EPS) * bconv_ref[...]

    mix = (jnp.dot(y_ret.astype(BF16), wout_ref[0:W, :], preferred_element_type=F32)
           + jnp.dot(y_conv.astype(BF16), wout_ref[W:2 * W, :], preferred_element_type=F32))
    x1 = x + gate_m * mix
    x1_ref[0, tile_rows, :] = x1

    ms2 = jnp.mean(x1 * x1, axis=-1, keepdims=True)
    h2 = x1 * lax.rsqrt(ms2 + EPS) * gffn_ref[...]
    h2 = h2 * (1.0 + scale_f) + shift_f
    _store_token_rows(h2_ref, h2, row0=sub * L)

    hi, lo = _split_bf16(h2)
    w2 = wr_ref[...]
    nt_dims = (((1,), (1,)), ((), ()))
    parts = (lax.dot_general(w2, hi, nt_dims, preferred_element_type=F32)
             + lax.dot_general(w2, lo, nt_dims, preferred_element_type=F32))
    logits_t = parts[:ROUTER_LANES] + parts[ROUTER_LANES:] + br_ref[...]
    logit_ref[:, tile_rows] = logits_t[:ROUTER_ROWS]


def _mixer(x, mod, cos128, sin128, gmix, win, convw, convb, bret, bconv, wout, gffn,
           wr2, br, lg, lgl, blk):
    bsz, seq, d = x.shape
    L = SEQ_TILE * MIXER_SUBTILES
    ns = seq // L
    W = RET_HEADS * HEAD_DIM
    n = bsz * seq
    const2 = lambda b, s: (0, 0)
    in_specs = [
        pl.BlockSpec(memory_space=pltpu.SMEM),
        pl.BlockSpec((1, L, d), lambda b, s: (b, s, 0)),
        pl.BlockSpec((1, 6, d), lambda b, s: (b, 0, 0)),
        pl.BlockSpec((L, LANES), lambda b, s: (b * ns + s, 0)),
        pl.BlockSpec((L, LANES), lambda b, s: (b * ns + s, 0)),
        pl.BlockSpec((1, d), const2),
        pl.BlockSpec(win.shape, const2, pipeline_mode=pl.Buffered(1)),
        pl.BlockSpec(convw.shape, const2),
        pl.BlockSpec((1, W), const2),
        pl.BlockSpec((1, W), const2),
        pl.BlockSpec((1, W), const2),
        pl.BlockSpec(wout.shape, const2, pipeline_mode=pl.Buffered(1)),
        pl.BlockSpec((1, d), const2),
        pl.BlockSpec(wr2.shape, const2),
        pl.BlockSpec((ROUTER_LANES, 1), const2),
        pl.BlockSpec((1, W), const2),
        pl.BlockSpec((W // 2, W // 2), const2),
    ]
    assert d == SUBLANES * LANES, "one token must fill exactly one (8, 128) f32 tile"
    out_shape = (jax.ShapeDtypeStruct((bsz, seq, d), F32),
                 jax.ShapeDtypeStruct((n * SUBLANES, LANES), F32),
                 jax.ShapeDtypeStruct((ROUTER_ROWS, n), F32))
    out_specs = (pl.BlockSpec((1, L, d), lambda b, s: (b, s, 0)),
                 pl.BlockSpec((L * SUBLANES, LANES), lambda b, s: (b * ns + s, 0)),
                 pl.BlockSpec((ROUTER_ROWS, L), lambda b, s: (0, b * ns + s)))
    return pl.pallas_call(
        _mixer_kernel,
        out_shape=out_shape,
        grid=(bsz, ns),
        in_specs=in_specs,
        out_specs=out_specs,
        scratch_shapes=[pltpu.VMEM((2, W // 2, W // 2), F32),
                        pltpu.VMEM((SEQ_TILE + 2 * SUBLANES, W), F32),
                        pltpu.VMEM(win.shape, BF16),
                        pltpu.VMEM(wout.shape, BF16),
                        pltpu.VMEM((RET_HEADS, SEQ_TILE, SEQ_TILE), F32),
                        pltpu.VMEM((SEQ_TILE, W), F32),
                        pltpu.VMEM((SEQ_TILE, W), F32)],
        compiler_params=pltpu.CompilerParams(
            dimension_semantics=("arbitrary", "arbitrary"),
            vmem_limit_bytes=VMEM_LIMIT_BYTES),
        name="mixer",
    )(lg, x, mod, cos128, sin128, gmix, win, convw, convb, bret, bconv, wout, gffn,
      wr2, br, lgl, blk)


ROUTER_TILE = 2048
RES_E1, RES_E2, RES_W1, RES_W2, RES_A, RES_B = range(6)
PLAN_FIRST_TILE, PLAN_N_TILE, PLAN_PAD_START, PLAN_PAD_N, PLAN_N_USED = range(5)


def _rows8(vals, width):
    rid = lax.broadcasted_iota(jnp.int32, (SUBLANES, width), 0)
    out = jnp.zeros((SUBLANES, width), F32)
    for r, v in enumerate(vals):
        out = jnp.where(rid == r, v, out)
    return out


def _router_kernel(lt_ref, slab_ref, dest_ref, plan_ref, res_ref, cnt_ref):
    T = ROUTER_TILE
    phase = pl.program_id(0)
    j = pl.program_id(1)
    nblk = T // LANES
    big = F32(1e9)
    rid8 =lax.broadcasted_iota(jnp.int32, (SUBLANES, T), 0).astype(F32)
    rid_e = lax.broadcasted_iota(jnp.int32, (N_EXPERTS, LANES), 0).astype(F32)

    def onehot(e_row, k):
        return jnp.where(rid_e == e_row[:, k * LANES:(k + 1) * LANES], 1.0, 0.0)

    @pl.when((phase == 0) & (j == 0))
    def _():
        cnt_ref[...] = jnp.zeros_like(cnt_ref)

    @pl.when(phase == 0)
    def _():
        lt = lt_ref[...]
        g_rows = lt[ROUTER_GROUP_ROW0:ROUTER_GROUP_ROW0 + SUBLANES]
        gvalid = rid8 < float(N_GROUPS)
        gm = jnp.where(gvalid, g_rows, F32(-jnp.inf))
        gexp = jnp.exp(gm - jnp.max(gm, axis=0, keepdims=True))
        gp = gexp / jnp.sum(gexp, axis=0, keepdims=True)
        g_top = jnp.max(gp, axis=0, keepdims=True)
        g_idx = jnp.min(jnp.where(gvalid & (gp == g_top), rid8, big), axis=0, keepdims=True)

        def group_slab(g):
            r0 = ROUTER_EXPERT_ROW0 + g * EXPERTS_PER_GROUP
            return lt[r0:r0 + EXPERTS_PER_GROUP]
        sel = group_slab(N_GROUPS - 1)
        for g in range(N_GROUPS - 2, -1, -1):
            sel = jnp.where(g_idx == float(g), group_slab(g), sel)
        eexp = jnp.exp(sel - jnp.max(sel, axis=0, keepdims=True))
        ep = eexp / jnp.sum(eexp, axis=0, keepdims=True)
        p1 = jnp.max(ep, axis=0, keepdims=True)
        i1 = jnp.min(jnp.where(ep == p1, rid8, big), axis=0, keepdims=True)
        m2 = rid8 != i1
        p2 = jnp.max(jnp.where(m2, ep, -1.0), axis=0, keepdims=True)
        i2 = jnp.min(jnp.where(m2 & (ep == p2), rid8, big), axis=0, keepdims=True)
        den = p1 + p2
        w1 = p1 / den * g_top
        w2 = p2 / den * g_top
        e1 = g_idx * float(EXPERTS_PER_GROUP) + i1
        e2 = g_idx * float(EXPERTS_PER_GROUP) + i2

        ii = lax.broadcasted_iota(jnp.int32, (LANES, LANES), 0)
        jj = lax.broadcasted_iota(jnp.int32, (LANES, LANES), 1)
        upper = jnp.where(ii < jj, 1.0, 0.0).astype(BF16)
        base = cnt_ref[...]
        ranks1, ranks2 = [], []
        for k in range(nblk):
            o1, o2 = onehot(e1, k), onehot(e2, k)
            r1 = jnp.dot(o1.astype(BF16), upper, preferred_element_type=F32)
            r2 = jnp.dot(o2.astype(BF16), upper, preferred_element_type=F32)
            c1 = jnp.sum(o1, axis=1, keepdims=True)
            c2 = jnp.sum(o2, axis=1, keepdims=True)
            ranks1.append(jnp.sum(o1 * (base + r1), axis=0, keepdims=True))
            ranks2.append(jnp.sum(o2 * (base + c1 + r2), axis=0, keepdims=True))
            base = base + c1 + c2
        cnt_ref[...] = base
        rank1 = jnp.concatenate(ranks1, axis=1)
        rank2 = jnp.concatenate(ranks2, axis=1)
        res_ref[j] = _rows8([e1, e2, w1, w2, rank1, rank2], T)

    @pl.when(phase == 1)
    def _():
        tm = float(EXPERT_TILE)
        cnt = cnt_ref[...]
        tiles = jnp.floor((cnt + (tm - 1.0)) * (1.0 / tm))
        ei = lax.broadcasted_iota(jnp.int32, (N_EXPERTS, N_EXPERTS), 0)
        ej = lax.broadcasted_iota(jnp.int32, (N_EXPERTS, N_EXPERTS), 1)
        lower = jnp.where(ej < ei, 1.0, 0.0).astype(BF16)
        first = jnp.dot(lower, tiles.astype(BF16), preferred_element_type=F32)
        starts = first * tm

        res = res_ref[j]
        e1, e2 = res[RES_E1:RES_E1 + 1], res[RES_E2:RES_E2 + 1]
        d1, d2 = [], []
        for k in range(nblk):
            blk_lanes = slice(k * LANES, (k + 1) * LANES)
            d1.append(res[RES_A:RES_A + 1, blk_lanes]
                      + jnp.sum(onehot(e1, k) * starts, axis=0, keepdims=True))
            d2.append(res[RES_B:RES_B + 1, blk_lanes]
                      + jnp.sum(onehot(e2, k) * starts, axis=0, keepdims=True))
        dest1 = jnp.concatenate(d1, axis=1)
        dest2 = jnp.concatenate(d2, axis=1)
        dest_ref[...] = _rows8([dest1, dest2], T).astype(jnp.int32)

        table = jnp.concatenate(
            [_rows8([e1, e2, res[RES_W1:RES_W1 + 1], res[RES_W2:RES_W2 + 1], dest1, dest2], T),
             jnp.zeros((ROUTER_LANES - SUBLANES, T), F32)], axis=0)
        slab_ref[...] = table.T

        @pl.when(j == 0)
        def _():
            lane_e = lax.broadcasted_iota(jnp.int32, (N_EXPERTS, LANES), 1).astype(F32)

            def as_row(col):
                return jnp.sum(jnp.where(rid_e == lane_e, col, 0.0), axis=0, keepdims=True)
            n_used = jnp.sum(tiles, axis=0, keepdims=True)
            padded = tiles * tm
            plan_ref[...] = _rows8([as_row(first), as_row(tiles), as_row(starts + cnt),
                                    as_row(padded - cnt), n_used], LANES).astype(jnp.int32)


def _router(logits_t):
    n = logits_t.shape[1]
    T = ROUTER_TILE
    nt = n // T
    return pl.pallas_call(
        _router_kernel,
        out_shape=(jax.ShapeDtypeStruct((n, ROUTER_LANES), F32),
                   jax.ShapeDtypeStruct((SUBLANES, n), jnp.int32),
                   jax.ShapeDtypeStruct((SUBLANES, LANES), jnp.int32)),
        grid=(2, nt),
        in_specs=[pl.BlockSpec((ROUTER_ROWS, T), lambda p, j: (0, j * (1 - p) + (nt - 1) * p))],
        out_specs=(pl.BlockSpec((T, ROUTER_LANES), lambda p, j: (j * p, 0)),
                   pl.BlockSpec((SUBLANES, T), lambda p, j: (0, j * p)),
                   pl.BlockSpec((SUBLANES, LANES), lambda p, j: (0, 0))),
        scratch_shapes=[pltpu.VMEM((nt, SUBLANES, T), F32),
                        pltpu.VMEM((N_EXPERTS, LANES), F32)],
        compiler_params=pltpu.CompilerParams(
            dimension_semantics=("arbitrary", "arbitrary"), vmem_limit_bytes=VMEM_LIMIT_BYTES),
        name="router",
    )(logits_t)


DISPATCH_TILE = 512
ROW_DMA_UNROLL = 8
PAD_UNITS = tuple(1 << b for b in reversed(range(EXPERT_TILE.bit_length() - 1)))


DISPATCH_LAG = 2
DISPATCH_SLOTS = DISPATCH_LAG + 2


def _dispatch_kernel(d0_ref, d1_ref, plan_ref, h2_hbm, xs_hbm,
                     stage, zbuf, isem, ssem, zsem):
    dt = DISPATCH_TILE
    ps = SUBLANES
    i = pl.program_id(0)
    nsteps = pl.num_programs(0)
    zrows = PAD_UNITS[0]

    def in_copy(blk, slot):
        src = h2_hbm.at[pl.ds(pl.multiple_of(blk * (dt * ps), dt * ps), dt * ps), :]
        return pltpu.make_async_copy(src, stage.at[slot], isem.at[slot])

    def wait_rows(slot):
        for _ in range(2):
            pltpu.make_async_copy(stage.at[slot], xs_hbm.at[pl.ds(0, dt * ps), :],
                                  ssem.at[slot]).wait()

    def pad_copy(start, unit):
        return pltpu.make_async_copy(zbuf.at[pl.ds(0, unit * ps), :],
                                     xs_hbm.at[pl.ds(pl.multiple_of(start * ps, ps), unit * ps), :],
                                     zsem)

    def pad_pass(do):
        def per_expert(e, carry):
            start = plan_ref[PLAN_PAD_START, e]
            npad = plan_ref[PLAN_PAD_N, e]
            for unit in PAD_UNITS:
                @pl.when((npad & unit) != 0)
                def _():
                    do(pad_copy(start + (npad & ~(2 * unit - 1)), unit))
            return carry
        lax.fori_loop(0, N_EXPERTS, per_expert, 0)

    def tail_pass(do):
        def per_unit(k, carry):
            do(pad_copy(k * zrows, zrows))
            return carry
        per_tile = EXPERT_TILE // zrows
        n_units = xs_hbm.shape[0] // (zrows * ps)
        lax.fori_loop(plan_ref[PLAN_N_USED, 0] * per_tile, n_units, per_unit, 0)

    slot = i % DISPATCH_SLOTS

    @pl.when(i == 0)
    def _():
        in_copy(0, 0).start()

        @pl.when(nsteps > 1)
        def _():
            in_copy(1, 1).start()

        zbuf[...] = jnp.zeros_like(zbuf)
        pad_pass(lambda cp: cp.start())
        tail_pass(lambda cp: cp.start())

    @pl.when(i >= DISPATCH_LAG)
    def _():
        wait_rows((i - DISPATCH_LAG) % DISPATCH_SLOTS)

    @pl.when(i + 2 < nsteps)
    def _():
        in_copy(i + 2, (i + 2) % DISPATCH_SLOTS).start()

    in_copy(i, slot).wait()
    base = i * dt
    src_ref = stage.at[slot]

    def body(r, carry):
        src = _token_rows(src_ref, r)
        for prio, d_ref in enumerate((d0_ref, d1_ref)):
            pltpu.make_async_copy(src, _token_rows(xs_hbm, d_ref[base + r]),
                                  ssem.at[slot]).start(priority=prio)
        return carry

    lax.fori_loop(0, dt, body, 0, unroll=ROW_DMA_UNROLL)

    @pl.when(i == nsteps - 1)
    def _():
        for back in range(DISPATCH_LAG - 1, -1, -1):
            @pl.when(i - back >= 0)
            def _():
                wait_rows((i - back) % DISPATCH_SLOTS)
        pad_pass(lambda cp: cp.wait())
        tail_pass(lambda cp: cp.wait())


def _dispatch(dest0, dest1, plan, h2t, p_rows):
    n = dest0.shape[0]
    dt = DISPATCH_TILE
    grid_spec = pltpu.PrefetchScalarGridSpec(
        num_scalar_prefetch=3,
        grid=(n // dt,),
        in_specs=[pl.BlockSpec(memory_space=pl.ANY)],
        out_specs=pl.BlockSpec(memory_space=pl.ANY),
        scratch_shapes=[pltpu.VMEM((DISPATCH_SLOTS, dt * SUBLANES, LANES), F32),
                        pltpu.VMEM((PAD_UNITS[0] * SUBLANES, LANES), F32),
                        pltpu.SemaphoreType.DMA((DISPATCH_SLOTS,)),
                        pltpu.SemaphoreType.DMA((DISPATCH_SLOTS,)),
                        pltpu.SemaphoreType.DMA(())],
    )
    return pl.pallas_call(
        _dispatch_kernel,
        out_shape=jax.ShapeDtypeStruct((p_rows * SUBLANES, LANES), F32),
        grid_spec=grid_spec,
        compiler_params=pltpu.CompilerParams(dimension_semantics=("arbitrary",)),
        name="dispatch",
    )(dest0, dest1, plan, h2t)


EXPERT_IN_SLOTS = 4
EXPERT_OUT_SLOTS = 3
MXU_COLS = 256


def _expert_kernel(plan_ref, xs_hbm, wg_ref, wu_ref, wd_ref, y_hbm,
                   xbuf, ybuf, wgb, wub, wdb, isem, osem):
    tm = EXPERT_TILE
    rows = tm * SUBLANES
    ni, no = EXPERT_IN_SLOTS, EXPERT_OUT_SLOTS
    e = pl.program_id(0)
    n_used = plan_ref[PLAN_N_USED, 0]
    n_mine = plan_ref[PLAN_N_TILE, e]
    n_tiles = y_hbm.shape[0] // rows

    def tile_rows(ref, g):
        return ref.at[pl.ds(pl.multiple_of(g * rows, rows), rows), :]

    def in_copy(g):
        return pltpu.make_async_copy(tile_rows(xs_hbm, g), xbuf.at[g % ni], isem.at[g % ni])

    def out_copy(g):
        return pltpu.make_async_copy(ybuf.at[g % no], tile_rows(y_hbm, g), osem.at[g % no])

    @pl.when(e == 0)
    def _():
        for g0 in range(ni - 1):
            @pl.when(g0 < n_used)
            def _():
                in_copy(g0).start()

    @pl.when(n_mine > 0)
    def _():
        wgb[...] = wg_ref[0].astype(BF16)
        wub[...] = wu_ref[0].astype(BF16)
        wdb[...] = wd_ref[0].astype(BF16)

    def tile(g, carry):
        in_copy(g).wait()

        @pl.when(g >= no)
        def _():
            out_copy(g - no).wait()

        @pl.when(g + ni - 1 < n_used)
        def _():
            in_copy(g + ni - 1).start()

        xb = _load_token_rows(xbuf.at[g % ni], tm).astype(BF16)
        a = jnp.dot(xb, wgb[...], preferred_element_type=F32)
        u = jnp.dot(xb, wub[...], preferred_element_type=F32)
        hid = (_silu(a) * u).astype(BF16)
        out = ybuf.at[g % no]
        per_piece = MXU_COLS // LANES
        for p in range(wdb.shape[1] // MXU_COLS):
            y = jnp.dot(hid, wdb[:, p * MXU_COLS:(p + 1) * MXU_COLS], preferred_element_type=F32)
            for q in range(per_piece):
                out[pl.ds(p * per_piece + q, tm, stride=SUBLANES), :] = y[:, q * LANES:(q + 1) * LANES]
        out_copy(g).start()
        return carry

    first = plan_ref[PLAN_FIRST_TILE, e]
    lax.fori_loop(first, first + n_mine, tile, 0)

    @pl.when(e == pl.num_programs(0) - 1)
    def _():
        for back in range(no, 0, -1):
            @pl.when(n_used >= back)
            def _():
                out_copy(n_used - back).wait()

        ybuf[0] = jnp.zeros((rows, LANES), F32)

        def zero_copy(g):
            return pltpu.make_async_copy(ybuf.at[0], tile_rows(y_hbm, g), osem.at[0])

        def fill(g, carry):
            zero_copy(g).start()
            return carry

        def drain(g, carry):
            zero_copy(g).wait()
            return carry

        lax.fori_loop(n_used, n_tiles, fill, 0)
        lax.fori_loop(n_used, n_tiles, drain, 0)


def _experts(plan, xs, wg, wu, wd):
    tm = EXPERT_TILE
    n_exp, d, de = wg.shape
    grid_spec = pltpu.PrefetchScalarGridSpec(
        num_scalar_prefetch=1,
        grid=(n_exp,),
        in_specs=[
            pl.BlockSpec(memory_space=pl.ANY),
            pl.BlockSpec((1, d, de), lambda e, *_: (e, 0, 0)),
            pl.BlockSpec((1, d, de), lambda e, *_: (e, 0, 0)),
            pl.BlockSpec((1, de, d), lambda e, *_: (e, 0, 0)),
        ],
        out_specs=pl.BlockSpec(memory_space=pl.ANY),
        scratch_shapes=[pltpu.VMEM((EXPERT_IN_SLOTS, tm * SUBLANES, LANES), F32),
                        pltpu.VMEM((EXPERT_OUT_SLOTS, tm * SUBLANES, LANES), F32),
                        pltpu.VMEM((d, de), BF16),
                        pltpu.VMEM((d, de), BF16),
                        pltpu.VMEM((de, d), BF16),
                        pltpu.SemaphoreType.DMA((EXPERT_IN_SLOTS,)),
                        pltpu.SemaphoreType.DMA((EXPERT_OUT_SLOTS,))],
    )
    return pl.pallas_call(
        _expert_kernel,
        out_shape=jax.ShapeDtypeStruct(xs.shape, F32),
        grid_spec=grid_spec,
        compiler_params=pltpu.CompilerParams(
            dimension_semantics=("arbitrary",),
            vmem_limit_bytes=VMEM_LIMIT_BYTES),
        name="experts",
    )(plan, xs, wg, wu, wd)


COMBINE_SLOTS = 3


def _combine_kernel(p0_ref, p1_ref, y_hbm, x1_ref, slab_ref, mod_ref, gfin_ref, o_ref,
                    *scratch):
    tm = COMBINE_TILE
    ns = COMBINE_SLOTS
    ahead = ns - 1
    ybufs, sem = scratch[:ns], scratch[ns]
    i = pl.program_id(0)
    nt = pl.num_programs(0)

    def row_copies(tile, slot, r):
        for j, p_ref in enumerate((p0_ref, p1_ref)):
            pltpu.make_async_copy(_token_rows(y_hbm, p_ref[tile * tm + r]),
                                  _token_rows(ybufs[slot].at[j], r),
                                  sem.at[slot]).start(priority=j)

    def wait_tile(slot):
        for j in range(2):
            pltpu.make_async_copy(y_hbm.at[pl.ds(0, tm * SUBLANES), :], ybufs[slot].at[j],
                                  sem.at[slot]).wait()

    @pl.when(i == 0)
    def _():
        for t0 in range(ahead):
            @pl.when(t0 < nt)
            def _():
                def body(r, carry):
                    row_copies(t0, t0, r)
                    return carry
                lax.fori_loop(0, tm, body, 0, unroll=ROW_DMA_UNROLL)

    n_chunks = SUBLANES
    batch = tm // n_chunks

    def step(slot):
        wait_tile(slot)
        nxt = jnp.minimum(i + ahead, nt - 1)
        nslot = (slot + ahead) % ns
        slab = slab_ref[...]
        w0 = slab[:, 2:3]
        w1 = slab[:, 3:4]
        gate_f = mod_ref[0][5:6]
        sq = jnp.zeros((tm, LANES), F32)
        for c in range(n_chunks):
            lanes = slice(c * LANES, (c + 1) * LANES)
            y0 = ybufs[slot].at[0][pl.ds(c, tm, stride=SUBLANES), :]
            y1 = ybufs[slot].at[1][pl.ds(c, tm, stride=SUBLANES), :]
            xo = x1_ref[:, lanes] + gate_f[:, lanes] * (w0 * y0 + w1 * y1)
            sq = sq + xo * xo
            o_ref[:, lanes] = xo
            for r in range(c * batch, (c + 1) * batch):
                row_copies(nxt, nslot, r)
        ms = jnp.sum(sq, axis=-1, keepdims=True) * (1.0 / (n_chunks * LANES))
        scale = lax.rsqrt(ms + EPS)
        for c in range(n_chunks):
            lanes = slice(c * LANES, (c + 1) * LANES)
            o_ref[:, lanes] = o_ref[:, lanes] * scale * gfin_ref[:, lanes]

    for slot in range(ns):
        @pl.when(i % ns == slot)
        def _():
            step(slot)

    @pl.when(i == nt - 1)
    def _():
        for k in range(ahead):
            for slot in range(ns):
                @pl.when((nt - 1 - k >= 0) & ((nt - 1 - k + ahead) % ns == slot))
                def _():
                    wait_tile(slot)


def _combine(p0, p1, y, x1, slab, mod, gfin, seq):
    n, d = x1.shape
    tm = COMBINE_TILE
    tiles_per_seq = seq // tm
    grid_spec = pltpu.PrefetchScalarGridSpec(
        num_scalar_prefetch=2,
        grid=(n // tm,),
        in_specs=[
            pl.BlockSpec(memory_space=pl.ANY),
            pl.BlockSpec((tm, d), lambda i, a, b: (i, 0)),
            pl.BlockSpec((tm, ROUTER_LANES), lambda i, a, b: (i, 0)),
            pl.BlockSpec((1, 6, d), lambda i, a, b: (i // tiles_per_seq, 0, 0)),
            pl.BlockSpec((1, d), lambda i, a, b: (0, 0)),
        ],
        out_specs=pl.BlockSpec((tm, d), lambda i, a, b: (i, 0)),
        scratch_shapes=([pltpu.VMEM((2, tm * SUBLANES, LANES), F32)] * COMBINE_SLOTS
                        + [pltpu.SemaphoreType.DMA((COMBINE_SLOTS,))]),
    )
    return pl.pallas_call(
        _combine_kernel,
        out_shape=jax.ShapeDtypeStruct((n, d), F32),
        grid_spec=grid_spec,
        compiler_params=pltpu.CompilerParams(
            dimension_semantics=("arbitrary",),
            vmem_limit_bytes=VMEM_LIMIT_BYTES),
        name="combine",
    )(p0, p1, y, x1, slab, mod, gfin)


def kernel(x, c, positions, ada_w, ada_b, norm_mix_g, norm_ffn_g, w_in, conv_w, conv_b,
           beta_ret, beta_conv, w_out, router_group_w, router_group_b, router_expert_w,
           router_expert_b, expert_w_gate, expert_w_up, expert_w_down, norm_final_g):
    bsz, seq, d = x.shape
    n = bsz * seq
    depth = ada_w.shape[0]
    assert depth == 1, "the combine kernel fuses the trunk's final RMSNorm (single layer)"
    W = RET_HEADS * HEAD_DIM

    l = 0
    mod, cos128, sin128 = _prologue(c, ada_w[l], ada_b[l], positions)
    mod = mod.reshape(bsz, 6, d)
    heads = jnp.arange(RET_HEADS, dtype=F32)
    lg = jnp.log1p(-jnp.exp2(-5.0 - heads))
    lgl = jnp.repeat(lg, HEAD_DIM).reshape(1, W)
    assert CONV_GROUP_DIM == HEAD_DIM, "conv groups and retention heads share the 64-lane block sums"
    blk_np = np.kron(np.eye(RET_HEADS // 2, dtype=np.float32),
                     np.ones((HEAD_DIM, HEAD_DIM), np.float32))
    blk = jnp.asarray(blk_np, dtype=BF16)

    gap = ROUTER_EXPERT_ROW0 - N_GROUPS
    tail = ROUTER_LANES - ROUTER_ROWS
    wr = jnp.concatenate([router_group_w[l].T, jnp.zeros((gap, d), F32),
                          router_expert_w[l].T, jnp.zeros((tail, d), F32)], axis=0)
    wr2 = jnp.concatenate(_split_bf16(wr), axis=0)
    br = jnp.concatenate([router_group_b[l], jnp.zeros((gap,), F32),
                          router_expert_b[l], jnp.zeros((tail,), F32)]).reshape(ROUTER_LANES, 1)

    x1, h2t, logits_t = _mixer(
        x, mod, cos128, sin128, norm_mix_g[l].reshape(1, d), w_in[l],
        conv_w[l], conv_b[l].reshape(1, W), beta_ret[l].reshape(1, W),
        beta_conv[l].reshape(1, W), w_out[l], norm_ffn_g[l].reshape(1, d),
        wr2, br, lg, lgl, blk)
    slab, dest, plan = _router(logits_t)
    p0, p1 = dest[0], dest[1]
    p_rows = 2 * n + N_EXPERTS * EXPERT_TILE

    xs = _dispatch(p0, p1, plan, h2t, p_rows)
    de = expert_w_gate.shape[-1]
    y = _experts(plan, xs,
                 expert_w_gate[l].reshape(N_EXPERTS, d, de),
                 expert_w_up[l].reshape(N_EXPERTS, d, de),
                 expert_w_down[l].reshape(N_EXPERTS, de, d))
    out = _combine(p0, p1, y, x1.reshape(n, d), slab, mod, norm_final_g.reshape(1, d), seq)
    return out.reshape(bsz, seq, d)
```

```python
import jax
import jax.numpy as jnp
import numpy as np
from jax import lax
from jax.experimental import pallas as pl
from jax.experimental.pallas import tpu as pltpu

F32 = jnp.float32
BF16 = jnp.bfloat16

CHUNK = 64
RET_HEADS = 8
HEAD_DIM = 64
CONV_GROUP_DIM = 64
ROPE_BASE = 10000.0
N_GROUPS = 4
EXPERTS_PER_GROUP = 8
N_EXPERTS = N_GROUPS * EXPERTS_PER_GROUP
EPS = 1e-6
GN_EPS = 1e-5

LANES = 128
SUBLANES = 8
VMEM_LIMIT_BYTES = 56 * 1024 * 1024

SEQ_TILE = 256
EXPERT_TILE = 256
COMBINE_TILE = 256
ROUTER_LANES = LANES
ROUTER_GROUP_ROW0 = 0
ROUTER_EXPERT_ROW0 = SUBLANES
ROUTER_ROWS = ROUTER_EXPERT_ROW0 + N_EXPERTS


def _silu(v):
    return v * (1.0 / (1.0 + jnp.exp(-v)))


def _split_bf16(v):
    hi = v.astype(BF16)
    return hi, (v - hi.astype(F32)).astype(BF16)


def _adaln_block(c_ref, w_ref, b_ref, o_ref):
    s_hi, s_lo = _split_bf16(_silu(c_ref[...]))
    w_hi, w_lo = _split_bf16(w_ref[...])
    o_ref[...] = (jnp.dot(s_hi, w_hi, preferred_element_type=F32)
                  + jnp.dot(s_lo, w_hi, preferred_element_type=F32)
                  + jnp.dot(s_hi, w_lo, preferred_element_type=F32) + b_ref[...])


def _rope_block(pos_ref, invf_ref, cos_ref, sin_ref):
    half = HEAD_DIM // 2
    per_row = LANES // half
    r = pos_ref.shape[0]
    ang = pos_ref[...].astype(F32) * invf_ref[...]
    lane = lax.broadcasted_iota(jnp.int32, (r, LANES), 1)
    quarter = lane // half
    sign = jnp.where(quarter % 2 == 0, -1.0, 1.0)
    for table, out_ref, scale in ((jnp.cos(ang), cos_ref, None), (jnp.sin(ang), sin_ref, sign)):
        rolled = [table] + [pltpu.roll(table, half * k, 1) for k in range(1, per_row)]
        for q in range(per_row):
            val = rolled[(0 - q) % per_row]
            for k in range(1, per_row):
                val = jnp.where(quarter == k, rolled[(k - q) % per_row], val)
            if scale is not None:
                val = val * scale
            out_ref[pl.ds(q, r, stride=per_row), :] = val


PROLOGUE_STEPS = 8


def _prologue_kernel(c_ref, w_ref, b_ref, pos_ref, invf_ref, mod_ref, cos_ref, sin_ref):
    _adaln_block(c_ref, w_ref, b_ref, mod_ref)
    _rope_block(pos_ref, invf_ref, cos_ref, sin_ref)


def _prologue(c, w, b, positions):
    bsz, d = c.shape
    n_mod = w.shape[1]
    n = positions.size
    half = HEAD_DIM // 2
    inv_freq = ROPE_BASE ** (-jnp.arange(0, HEAD_DIM, 2, dtype=F32) / HEAD_DIM)
    per_row = LANES // half
    rows = n // per_row
    pos_rep = jnp.broadcast_to(positions.reshape(n, 1), (n, half)).reshape(rows, LANES)
    invf = jnp.tile(inv_freq, per_row).reshape(1, LANES)
    steps = PROLOGUE_STEPS
    tn, tr = n_mod // steps, rows // steps
    assert tn % LANES == 0 and tr % SUBLANES == 0
    return pl.pallas_call(
        _prologue_kernel,
        out_shape=(jax.ShapeDtypeStruct((bsz, n_mod), F32),
                   jax.ShapeDtypeStruct((n, LANES), F32),
                   jax.ShapeDtypeStruct((n, LANES), F32)),
        grid=(steps,),
        in_specs=[pl.BlockSpec((bsz, d), lambda j: (0, 0)),
                  pl.BlockSpec((d, tn), lambda j: (0, j)),
                  pl.BlockSpec((1, tn), lambda j: (0, j)),
                  pl.BlockSpec((tr, LANES), lambda j: (j, 0)),
                  pl.BlockSpec((1, LANES), lambda j: (0, 0))],
        out_specs=(pl.BlockSpec((bsz, tn), lambda j: (0, j)),
                   pl.BlockSpec((tr * per_row, LANES), lambda j: (j, 0)),
                   pl.BlockSpec((tr * per_row, LANES), lambda j: (j, 0))),
        name="prologue",
    )(c, w, b.reshape(1, n_mod), pos_rep, invf)


def _load_token_rows(ref, rows):
    return jnp.concatenate(
        [ref[pl.ds(c, rows, stride=SUBLANES), :] for c in range(SUBLANES)], axis=1)


def _store_token_rows(ref, val, row0=0):
    rows = val.shape[0]
    for c in range(SUBLANES):
        ref[pl.ds(row0 * SUBLANES + c, rows, stride=SUBLANES), :] = val[:, c * LANES:(c + 1) * LANES]


def _token_rows(ref, row):
    return ref.at[pl.ds(pl.multiple_of(row * SUBLANES, SUBLANES), SUBLANES), :]


MIXER_SUBTILES = 2


MIXER_WIN, MIXER_WOUT = 6, 11


def _mixer_kernel(*refs):
    *refs, winb_ref, woutb_ref, dec_ref, qdec_ref, kdec_ref = refs
    state_ref, ubuf_ref = refs[-2:]
    L = SEQ_TILE

    @pl.when((pl.program_id(0) == 0) & (pl.program_id(1) == 0))
    def _():
        for src, dst in ((refs[MIXER_WIN], winb_ref), (refs[MIXER_WOUT], woutb_ref)):
            step = RET_HEADS * HEAD_DIM
            for c0 in range(0, src.shape[1], step):
                dst[:, c0:c0 + step] = src[:, c0:c0 + step].astype(BF16)
        lg_ref, lgl_ref = refs[0], refs[15]
        ii = lax.broadcasted_iota(jnp.int32, (L, L), 0)
        jj = lax.broadcasted_iota(jnp.int32, (L, L), 1)
        dist = jnp.abs(ii - jj).astype(F32)
        allowed = (jj // CHUNK) <= (ii // CHUNK)
        for head in range(RET_HEADS):
            dec_ref[head] = jnp.where(allowed, jnp.exp(lg_ref[head] * dist), 0.0)
        rowf = lax.broadcasted_iota(jnp.int32, qdec_ref.shape, 0).astype(F32)
        qdec_ref[...] = jnp.exp(lgl_ref[...] * (rowf + 1.0))
        kdec_ref[...] = jnp.exp(lgl_ref[...] * (float(L - 1) - rowf))

    @pl.when(pl.program_id(1) == 0)
    def _():
        state_ref[...] = jnp.zeros_like(state_ref)
        ubuf_ref[0:SUBLANES, :] = jnp.zeros((SUBLANES, ubuf_ref.shape[1]), F32)

    refs[MIXER_WIN], refs[MIXER_WOUT] = winb_ref, woutb_ref
    for sub in range(MIXER_SUBTILES):
        _mixer_tile(sub, *refs, dec_ref, qdec_ref, kdec_ref)


def _mixer_tile(sub, lg_ref, x_ref, mod_ref, cos_ref, sin_ref, gmix_ref, win_ref, convw_ref,
                convb_ref, bret_ref, bconv_ref, wout_ref, gffn_ref, wr_ref, br_ref, lgl_ref,
                blk_ref, x1_ref, h2_ref, logit_ref, state_ref, ubuf_ref,
                dec_ref, qdec_ref, kdec_ref):
    L = SEQ_TILE
    W = RET_HEADS * HEAD_DIM
    tile_rows = slice(sub * L, (sub + 1) * L)

    x = x_ref[0, tile_rows, :]
    mod = mod_ref[0]
    shift_m, scale_m, gate_m = mod[0:1], mod[1:2], mod[2:3]
    shift_f, scale_f = mod[3:4], mod[4:5]

    ms = jnp.mean(x * x, axis=-1, keepdims=True)
    h = x * lax.rsqrt(ms + EPS) * gmix_ref[...]
    h = h * (1.0 + scale_m) + shift_m
    hb = h.astype(BF16)

    def proj(i):
        return jnp.dot(hb, win_ref[:, i * W:(i + 1) * W], preferred_element_type=F32)

    cos = jnp.concatenate([cos_ref[tile_rows, :]] * 4, axis=1)
    sin = jnp.concatenate([sin_ref[tile_rows, :]] * 4, axis=1)
    lane_w = lax.broadcasted_iota(jnp.int32, (L, W), 1)
    first_half = (lane_w & (HEAD_DIM - 1)) < (HEAD_DIM // 2)

    def rot(t):
        partner = jnp.where(first_half, pltpu.roll(t, W - HEAD_DIM // 2, 1),
                            pltpu.roll(t, HEAD_DIM // 2, 1))
        return t * cos + partner * sin

    q = rot(proj(0))
    k = rot(proj(1)) * (HEAD_DIM ** -0.5)
    v = proj(2)
    vb = v.astype(BF16)
    kb = k.astype(BF16)

    lgl = lgl_ref[...]
    qd = q * qdec_ref[...]
    kd = k * kdec_ref[...]
    blk = blk_ref[...]
    HW = W // 2
    blk_f = blk.astype(F32)
    qdb = qd.astype(BF16)
    kdb = kd.astype(BF16)
    state_decay = jnp.exp(lgl * float(L))
    inter = []
    for hf in range(2):
        sl = slice(hf * HW, (hf + 1) * HW)
        st = state_ref[hf]
        inter.append(jnp.dot(qdb[:, sl], st.astype(BF16), preferred_element_type=F32))
        kv = lax.dot_general(kdb[:, sl], vb[:, sl], (((0,), (0,)), ((), ())),
                             preferred_element_type=F32)
        state_ref[hf] = st * state_decay[:, sl] + kv * blk_f
    y_inter = jnp.concatenate(inter, axis=1)

    def head_sums(t):
        tb = t.astype(BF16)
        return jnp.concatenate(
            [jnp.dot(tb[:, hf * HW:(hf + 1) * HW], blk, preferred_element_type=F32)
             for hf in range(2)], axis=1)

    lane_p = lax.broadcasted_iota(jnp.int32, (L, LANES), 1)
    lo_head = lane_p < HEAD_DIM
    pairs = []
    for p in range(RET_HEADS // 2):
        sl = slice(p * LANES, (p + 1) * LANES)
        qp, kp, vp = q[:, sl], kb[:, sl], vb[:, sl]
        ys = []
        for hh in range(2):
            head = 2 * p + hh
            keep = lo_head if hh == 0 else jnp.logical_not(lo_head)
            qh = jnp.where(keep, qp, 0.0).astype(BF16)
            sc = lax.dot_general(qh, kp, (((1,), (1,)), ((), ())),
                                 preferred_element_type=F32)
            ys.append(jnp.dot((sc * dec_ref[head]).astype(BF16), vp, preferred_element_type=F32))
        pairs.append(jnp.where(lo_head, ys[0], ys[1]))
    y = jnp.concatenate(pairs, axis=1) + y_inter

    inv_hd = 1.0 / HEAD_DIM
    mu = head_sums(y) * inv_hd
    d = y - mu
    var = head_sums(d * d) * inv_hd
    g = proj(3)
    y_ret = _silu(g) * (d * lax.rsqrt(var + GN_EPS)) * bret_ref[...]

    b_gate = proj(4)
    u = proj(5) * proj(6)
    ubuf_ref[SUBLANES:SUBLANES + L, :] = u
    u1 = ubuf_ref[SUBLANES - 1:SUBLANES - 1 + L, :]
    u2 = ubuf_ref[SUBLANES - 2:SUBLANES - 2 + L, :]
    ubuf_ref[0:SUBLANES, :] = ubuf_ref[L:L + SUBLANES, :]
    cw = convw_ref[...]
    conv = u2 * cw[0:1] + u1 * cw[1:2] + u * cw[2:3] + convb_ref[...]
    yc = b_gate * conv
    msc = head_sums(yc * yc) * (1.0 / CONV_GROUP_DIM)
    y_conv = yc * lax.rsqrt(msc + EPS) * bconv_ref[...]

    mix = (jnp.dot(y_ret.astype(BF16), wout_ref[0:W, :], preferred_element_type=F32)
           + jnp.dot(y_conv.astype(BF16), wout_ref[W:2 * W, :], preferred_element_type=F32))
    x1 = x + gate_m * mix
    x1_ref[0, tile_rows, :] = x1

    ms2 = jnp.mean(x1 * x1, axis=-1, keepdims=True)
    h2 = x1 * lax.rsqrt(ms2 + EPS) * gffn_ref[...]
    h2 = h2 * (1.0 + scale_f) + shift_f
    _store_token_rows(h2_ref, h2, row0=sub * L)

    hi, lo = _split_bf16(h2)
    w2 = wr_ref[...]
    nt_dims = (((1,), (1,)), ((), ()))
    parts = (lax.dot_general(w2, hi, nt_dims, preferred_element_type=F32)
             + lax.dot_general(w2, lo, nt_dims, preferred_element_type=F32))
    logits_t = parts[:ROUTER_LANES] + parts[ROUTER_LANES:] + br_ref[...]
    logit_ref[:, tile_rows] = logits_t[:ROUTER_ROWS]


def _mixer(x, mod, cos128, sin128, gmix, win, convw, convb, bret, bconv, wout, gffn,
           wr2, br, lg, lgl, blk):
    bsz, seq, d = x.shape
    L = SEQ_TILE * MIXER_SUBTILES
    ns = seq // L
    W = RET_HEADS * HEAD_DIM
    n = bsz * seq
    const2 = lambda b, s: (0, 0)
    in_specs = [
        pl.BlockSpec(memory_space=pltpu.SMEM),
        pl.BlockSpec((1, L, d), lambda b, s: (b, s, 0)),
        pl.BlockSpec((1, 6, d), lambda b, s: (b, 0, 0)),
        pl.BlockSpec((L, LANES), lambda b, s: (b * ns + s, 0)),
        pl.BlockSpec((L, LANES), lambda b, s: (b * ns + s, 0)),
        pl.BlockSpec((1, d), const2),
        pl.BlockSpec(win.shape, const2, pipeline_mode=pl.Buffered(1)),
        pl.BlockSpec(convw.shape, const2),
        pl.BlockSpec((1, W), const2),
        pl.BlockSpec((1, W), const2),
        pl.BlockSpec((1, W), const2),
        pl.BlockSpec(wout.shape, const2, pipeline_mode=pl.Buffered(1)),
        pl.BlockSpec((1, d), const2),
        pl.BlockSpec(wr2.shape, const2),
        pl.BlockSpec((ROUTER_LANES, 1), const2),
        pl.BlockSpec((1, W), const2),
        pl.BlockSpec((W // 2, W // 2), const2),
    ]
    assert d == SUBLANES * LANES, "one token must fill exactly one (8, 128) f32 tile"
    out_shape = (jax.ShapeDtypeStruct((bsz, seq, d), F32),
                 jax.ShapeDtypeStruct((n * SUBLANES, LANES), F32),
                 jax.ShapeDtypeStruct((ROUTER_ROWS, n), F32))
    out_specs = (pl.BlockSpec((1, L, d), lambda b, s: (b, s, 0)),
                 pl.BlockSpec((L * SUBLANES, LANES), lambda b, s: (b * ns + s, 0)),
                 pl.BlockSpec((ROUTER_ROWS, L), lambda b, s: (0, b * ns + s)))
    return pl.pallas_call(
        _mixer_kernel,
        out_shape=out_shape,
        grid=(bsz, ns),
        in_specs=in_specs,
        out_specs=out_specs,
        scratch_shapes=[pltpu.VMEM((2, W // 2, W // 2), F32),
                        pltpu.VMEM((SEQ_TILE + 2 * SUBLANES, W), F32),
                        pltpu.VMEM(win.shape, BF16),
                        pltpu.VMEM(wout.shape, BF16),
                        pltpu.VMEM((RET_HEADS, SEQ_TILE, SEQ_TILE), F32),
                        pltpu.VMEM((SEQ_TILE, W), F32),
                        pltpu.VMEM((SEQ_TILE, W), F32)],
        compiler_params=pltpu.CompilerParams(
            dimension_semantics=("arbitrary", "arbitrary"),
            vmem_limit_bytes=VMEM_LIMIT_BYTES),
        name="mixer",
    )(lg, x, mod, cos128, sin128, gmix, win, convw, convb, bret, bconv, wout, gffn,
      wr2, br, lgl, blk)


ROUTER_TILE = 2048
RES_E1, RES_E2, RES_W1, RES_W2, RES_A, RES_B = range(6)
PLAN_FIRST_TILE, PLAN_N_TILE, PLAN_PAD_START, PLAN_PAD_N, PLAN_N_USED = range(5)


def _rows8(vals, width):
    rid = lax.broadcasted_iota(jnp.int32, (SUBLANES, width), 0)
    out = jnp.zeros((SUBLANES, width), F32)
    for r, v in enumerate(vals):
        out = jnp.where(rid == r, v, out)
    return out


def _router_kernel(lt_ref, slab_ref, dest_ref, plan_ref, res_ref, cnt_ref):
    T = ROUTER_TILE
    phase = pl.program_id(0)
    j = pl.program_id(1)
    nblk = T // LANES
    big = F32(1e9)
    rid8 =lax.broadcasted_iota(jnp.int32, (SUBLANES, T), 0).astype(F32)
    rid_e = lax.broadcasted_iota(jnp.int32, (N_EXPERTS, LANES), 0).astype(F32)

    def onehot(e_row, k):
        return jnp.where(rid_e == e_row[:, k * LANES:(k + 1) * LANES], 1.0, 0.0)

    @pl.when((phase == 0) & (j == 0))
    def _():
        cnt_ref[...] = jnp.zeros_like(cnt_ref)

    @pl.when(phase == 0)
    def _():
        lt = lt_ref[...]
        g_rows = lt[ROUTER_GROUP_ROW0:ROUTER_GROUP_ROW0 + SUBLANES]
        gvalid = rid8 < float(N_GROUPS)
        gm = jnp.where(gvalid, g_rows, F32(-jnp.inf))
        gexp = jnp.exp(gm - jnp.max(gm, axis=0, keepdims=True))
        gp = gexp / jnp.sum(gexp, axis=0, keepdims=True)
        g_top = jnp.max(gp, axis=0, keepdims=True)
        g_idx = jnp.min(jnp.where(gvalid & (gp == g_top), rid8, big), axis=0, keepdims=True)

        def group_slab(g):
            r0 = ROUTER_EXPERT_ROW0 + g * EXPERTS_PER_GROUP
            return lt[r0:r0 + EXPERTS_PER_GROUP]
        sel = group_slab(N_GROUPS - 1)
        for g in range(N_GROUPS - 2, -1, -1):
            sel = jnp.where(g_idx == float(g), group_slab(g), sel)
        eexp = jnp.exp(sel - jnp.max(sel, axis=0, keepdims=True))
        ep = eexp / jnp.sum(eexp, axis=0, keepdims=True)
        p1 = jnp.max(ep, axis=0, keepdims=True)
        i1 = jnp.min(jnp.where(ep == p1, rid8, big), axis=0, keepdims=True)
        m2 = rid8 != i1
        p2 = jnp.max(jnp.where(m2, ep, -1.0), axis=0, keepdims=True)
        i2 = jnp.min(jnp.where(m2 & (ep == p2), rid8, big), axis=0, keepdims=True)
        den = p1 + p2
        w1 = p1 / den * g_top
        w2 = p2 / den * g_top
        e1 = g_idx * float(EXPERTS_PER_GROUP) + i1
        e2 = g_idx * float(EXPERTS_PER_GROUP) + i2

        ii = lax.broadcasted_iota(jnp.int32, (LANES, LANES), 0)
        jj = lax.broadcasted_iota(jnp.int32, (LANES, LANES), 1)
        upper = jnp.where(ii < jj, 1.0, 0.0).astype(BF16)
        base = cnt_ref[...]
        ranks1, ranks2 = [], []
        for k in range(nblk):
            o1, o2 = onehot(e1, k), onehot(e2, k)
            r1 = jnp.dot(o1.astype(BF16), upper, preferred_element_type=F32)
            r2 = jnp.dot(o2.astype(BF16), upper, preferred_element_type=F32)
            c1 = jnp.sum(o1, axis=1, keepdims=True)
            c2 = jnp.sum(o2, axis=1, keepdims=True)
            ranks1.append(jnp.sum(o1 * (base + r1), axis=0, keepdims=True))
            ranks2.append(jnp.sum(o2 * (base + c1 + r2), axis=0, keepdims=True))
            base = base + c1 + c2
        cnt_ref[...] = base
        rank1 = jnp.concatenate(ranks1, axis=1)
        rank2 = jnp.concatenate(ranks2, axis=1)
        res_ref[j] = _rows8([e1, e2, w1, w2, rank1, rank2], T)

    @pl.when(phase == 1)
    def _():
        tm = float(EXPERT_TILE)
        cnt = cnt_ref[...]
        tiles = jnp.floor((cnt + (tm - 1.0)) * (1.0 / tm))
        ei = lax.broadcasted_iota(jnp.int32, (N_EXPERTS, N_EXPERTS), 0)
        ej = lax.broadcasted_iota(jnp.int32, (N_EXPERTS, N_EXPERTS), 1)
        lower = jnp.where(ej < ei, 1.0, 0.0).astype(BF16)
        first = jnp.dot(lower, tiles.astype(BF16), preferred_element_type=F32)
        starts = first * tm

        res = res_ref[j]
        e1, e2 = res[RES_E1:RES_E1 + 1], res[RES_E2:RES_E2 + 1]
        d1, d2 = [], []
        for k in range(nblk):
            blk_lanes = slice(k * LANES, (k + 1) * LANES)
            d1.append(res[RES_A:RES_A + 1, blk_lanes]
                      + jnp.sum(onehot(e1, k) * starts, axis=0, keepdims=True))
            d2.append(res[RES_B:RES_B + 1, blk_lanes]
                      + jnp.sum(onehot(e2, k) * starts, axis=0, keepdims=True))
        dest1 = jnp.concatenate(d1, axis=1)
        dest2 = jnp.concatenate(d2, axis=1)
        dest_ref[...] = _rows8([dest1, dest2], T).astype(jnp.int32)

        table = jnp.concatenate(
            [_rows8([e1, e2, res[RES_W1:RES_W1 + 1], res[RES_W2:RES_W2 + 1], dest1, dest2], T),
             jnp.zeros((ROUTER_LANES - SUBLANES, T), F32)], axis=0)
        slab_ref[...] = table.T

        @pl.when(j == 0)
        def _():
            lane_e = lax.broadcasted_iota(jnp.int32, (N_EXPERTS, LANES), 1).astype(F32)

            def as_row(col):
                return jnp.sum(jnp.where(rid_e == lane_e, col, 0.0), axis=0, keepdims=True)
            n_used = jnp.sum(tiles, axis=0, keepdims=True)
            padded = tiles * tm
            plan_ref[...] = _rows8([as_row(first), as_row(tiles), as_row(starts + cnt),
                                    as_row(padded - cnt), n_used], LANES).astype(jnp.int32)


def _router(logits_t):
    n = logits_t.shape[1]
    T = ROUTER_TILE
    nt = n // T
    return pl.pallas_call(
        _router_kernel,
        out_shape=(jax.ShapeDtypeStruct((n, ROUTER_LANES), F32),
                   jax.ShapeDtypeStruct((SUBLANES, n), jnp.int32),
                   jax.ShapeDtypeStruct((SUBLANES, LANES), jnp.int32)),
        grid=(2, nt),
        in_specs=[pl.BlockSpec((ROUTER_ROWS, T), lambda p, j: (0, j * (1 - p) + (nt - 1) * p))],
        out_specs=(pl.BlockSpec((T, ROUTER_LANES), lambda p, j: (j * p, 0)),
                   pl.BlockSpec((SUBLANES, T), lambda p, j: (0, j * p)),
                   pl.BlockSpec((SUBLANES, LANES), lambda p, j: (0, 0))),
        scratch_shapes=[pltpu.VMEM((nt, SUBLANES, T), F32),
                        pltpu.VMEM((N_EXPERTS, LANES), F32)],
        compiler_params=pltpu.CompilerParams(
            dimension_semantics=("arbitrary", "arbitrary"), vmem_limit_bytes=VMEM_LIMIT_BYTES),
        name="router",
    )(logits_t)


DISPATCH_TILE = 512
ROW_DMA_UNROLL = 8
PAD_UNITS = tuple(1 << b for b in reversed(range(EXPERT_TILE.bit_length() - 1)))


DISPATCH_LAG = 2
DISPATCH_SLOTS = DISPATCH_LAG + 2


def _dispatch_kernel(d0_ref, d1_ref, plan_ref, h2_hbm, xs_hbm,
                     stage, zbuf, isem, ssem, zsem):
    dt = DISPATCH_TILE
    ps = SUBLANES
    i = pl.program_id(0)
    nsteps = pl.num_programs(0)
    zrows = PAD_UNITS[0]

    def in_copy(blk, slot):
        src = h2_hbm.at[pl.ds(pl.multiple_of(blk * (dt * ps), dt * ps), dt * ps), :]
        return pltpu.make_async_copy(src, stage.at[slot], isem.at[slot])

    def wait_rows(slot):
        for _ in range(2):
            pltpu.make_async_copy(stage.at[slot], xs_hbm.at[pl.ds(0, dt * ps), :],
                                  ssem.at[slot]).wait()

    def pad_copy(start, unit):
        return pltpu.make_async_copy(zbuf.at[pl.ds(0, unit * ps), :],
                                     xs_hbm.at[pl.ds(pl.multiple_of(start * ps, ps), unit * ps), :],
                                     zsem)

    def pad_pass(do):
        def per_expert(e, carry):
            start = plan_ref[PLAN_PAD_START, e]
            npad = plan_ref[PLAN_PAD_N, e]
            for unit in PAD_UNITS:
                @pl.when((npad & unit) != 0)
                def _():
                    do(pad_copy(start + (npad & ~(2 * unit - 1)), unit))
            return carry
        lax.fori_loop(0, N_EXPERTS, per_expert, 0)

    def tail_pass(do):
        def per_unit(k, carry):
            do(pad_copy(k * zrows, zrows))
            return carry
        per_tile = EXPERT_TILE // zrows
        n_units = xs_hbm.shape[0] // (zrows * ps)
        lax.fori_loop(plan_ref[PLAN_N_USED, 0] * per_tile, n_units, per_unit, 0)

    slot = i % DISPATCH_SLOTS

    @pl.when(i == 0)
    def _():
        in_copy(0, 0).start()

        @pl.when(nsteps > 1)
        def _():
            in_copy(1, 1).start()

        zbuf[...] = jnp.zeros_like(zbuf)
        pad_pass(lambda cp: cp.start())
        tail_pass(lambda cp: cp.start())

    @pl.when(i >= DISPATCH_LAG)
    def _():
        wait_rows((i - DISPATCH_LAG) % DISPATCH_SLOTS)

    @pl.when(i + 2 < nsteps)
    def _():
        in_copy(i + 2, (i + 2) % DISPATCH_SLOTS).start()

    in_copy(i, slot).wait()
    base = i * dt
    src_ref = stage.at[slot]

    def body(r, carry):
        src = _token_rows(src_ref, r)
        for prio, d_ref in enumerate((d0_ref, d1_ref)):
            pltpu.make_async_copy(src, _token_rows(xs_hbm, d_ref[base + r]),
                                  ssem.at[slot]).start(priority=prio)
        return carry

    lax.fori_loop(0, dt, body, 0, unroll=ROW_DMA_UNROLL)

    @pl.when(i == nsteps - 1)
    def _():
        for back in range(DISPATCH_LAG - 1, -1, -1):
            @pl.when(i - back >= 0)
            def _():
                wait_rows((i - back) % DISPATCH_SLOTS)
        pad_pass(lambda cp: cp.wait())
        tail_pass(lambda cp: cp.wait())


def _dispatch(dest0, dest1, plan, h2t, p_rows):
    n = dest0.shape[0]
    dt = DISPATCH_TILE
    grid_spec = pltpu.PrefetchScalarGridSpec(
        num_scalar_prefetch=3,
        grid=(n // dt,),
        in_specs=[pl.BlockSpec(memory_space=pl.ANY)],
        out_specs=pl.BlockSpec(memory_space=pl.ANY),
        scratch_shapes=[pltpu.VMEM((DISPATCH_SLOTS, dt * SUBLANES, LANES), F32),
                        pltpu.VMEM((PAD_UNITS[0] * SUBLANES, LANES), F32),
                        pltpu.SemaphoreType.DMA((DISPATCH_SLOTS,)),
                        pltpu.SemaphoreType.DMA((DISPATCH_SLOTS,)),
                        pltpu.SemaphoreType.DMA(())],
    )
    return pl.pallas_call(
        _dispatch_kernel,
        out_shape=jax.ShapeDtypeStruct((p_rows * SUBLANES, LANES), F32),
        grid_spec=grid_spec,
        compiler_params=pltpu.CompilerParams(dimension_semantics=("arbitrary",)),
        name="dispatch",
    )(dest0, dest1, plan, h2t)


EXPERT_IN_SLOTS = 4
EXPERT_OUT_SLOTS = 3
MXU_COLS = 256


def _expert_kernel(plan_ref, xs_hbm, wg_ref, wu_ref, wd_ref, y_hbm,
                   xbuf, ybuf, zbuf, wgb, wub, wdb, isem, osem, zsem):
    tm = EXPERT_TILE
    rows = tm * SUBLANES
    ni, no = EXPERT_IN_SLOTS, EXPERT_OUT_SLOTS
    e = pl.program_id(0)
    n_used = plan_ref[PLAN_N_USED, 0]
    n_mine = plan_ref[PLAN_N_TILE, e]
    n_tiles = y_hbm.shape[0] // rows

    def tile_rows(ref, g):
        return ref.at[pl.ds(pl.multiple_of(g * rows, rows), rows), :]

    def in_copy(g):
        return pltpu.make_async_copy(tile_rows(xs_hbm, g), xbuf.at[g % ni], isem.at[g % ni])

    def out_copy(g):
        return pltpu.make_async_copy(ybuf.at[g % no], tile_rows(y_hbm, g), osem.at[g % no])

    def zero_tail(do):
        def per_tile(g, carry):
            do(pltpu.make_async_copy(zbuf, tile_rows(y_hbm, g), zsem))
            return carry
        lax.fori_loop(n_used, n_tiles, per_tile, 0)

    @pl.when(e == 0)
    def _():
        for g0 in range(ni - 1):
            @pl.when(g0 < n_used)
            def _():
                in_copy(g0).start()
        zbuf[...] = jnp.zeros_like(zbuf)
        zero_tail(lambda cp: cp.start())

    @pl.when(n_mine > 0)
    def _():
        wgb[...] = wg_ref[0].astype(BF16)
        wub[...] = wu_ref[0].astype(BF16)
        wdb[...] = wd_ref[0].astype(BF16)

    def tile(g, carry):
        in_copy(g).wait()

        @pl.when(g >= no)
        def _():
            out_copy(g - no).wait()

        @pl.when(g + ni - 1 < n_used)
        def _():
            in_copy(g + ni - 1).start()

        xb = _load_token_rows(xbuf.at[g % ni], tm).astype(BF16)
        a = jnp.dot(xb, wgb[...], preferred_element_type=F32)
        u = jnp.dot(xb, wub[...], preferred_element_type=F32)
        hid = (_silu(a) * u).astype(BF16)
        out = ybuf.at[g % no]
        per_piece = MXU_COLS // LANES
        for p in range(wdb.shape[1] // MXU_COLS):
            y = jnp.dot(hid, wdb[:, p * MXU_COLS:(p + 1) * MXU_COLS], preferred_element_type=F32)
            for q in range(per_piece):
                out[pl.ds(p * per_piece + q, tm, stride=SUBLANES), :] = y[:, q * LANES:(q + 1) * LANES]
        out_copy(g).start()
        return carry

    first = plan_ref[PLAN_FIRST_TILE, e]
    lax.fori_loop(first, first + n_mine, tile, 0)

    @pl.when(e == pl.num_programs(0) - 1)
    def _():
        for back in range(no, 0, -1):
            @pl.when(n_used >= back)
            def _():
                out_copy(n_used - back).wait()
        zero_tail(lambda cp: cp.wait())


def _experts(plan, xs, wg, wu, wd):
    tm = EXPERT_TILE
    n_exp, d, de = wg.shape
    grid_spec = pltpu.PrefetchScalarGridSpec(
        num_scalar_prefetch=1,
        grid=(n_exp,),
        in_specs=[
            pl.BlockSpec(memory_space=pl.ANY),
            pl.BlockSpec((1, d, de), lambda e, *_: (e, 0, 0)),
            pl.BlockSpec((1, d, de), lambda e, *_: (e, 0, 0)),
            pl.BlockSpec((1, de, d), lambda e, *_: (e, 0, 0)),
        ],
        out_specs=pl.BlockSpec(memory_space=pl.ANY),
        scratch_shapes=[pltpu.VMEM((EXPERT_IN_SLOTS, tm * SUBLANES, LANES), F32),
                        pltpu.VMEM((EXPERT_OUT_SLOTS, tm * SUBLANES, LANES), F32),
                        pltpu.VMEM((tm * SUBLANES, LANES), F32),
                        pltpu.VMEM((d, de), BF16),
                        pltpu.VMEM((d, de), BF16),
                        pltpu.VMEM((de, d), BF16),
                        pltpu.SemaphoreType.DMA((EXPERT_IN_SLOTS,)),
                        pltpu.SemaphoreType.DMA((EXPERT_OUT_SLOTS,)),
                        pltpu.SemaphoreType.DMA(())],
    )
    return pl.pallas_call(
        _expert_kernel,
        out_shape=jax.ShapeDtypeStruct(xs.shape, F32),
        grid_spec=grid_spec,
        compiler_params=pltpu.CompilerParams(
            dimension_semantics=("arbitrary",),
            vmem_limit_bytes=VMEM_LIMIT_BYTES),
        name="experts",
    )(plan, xs, wg, wu, wd)


COMBINE_SLOTS = 3


def _combine_kernel(p0_ref, p1_ref, y_hbm, x1_ref, slab_ref, mod_ref, gfin_ref, o_ref,
                    *scratch):
    tm = COMBINE_TILE
    ns = COMBINE_SLOTS
    ahead = ns - 1
    ybufs, sem = scratch[:ns], scratch[ns]
    i = pl.program_id(0)
    nt = pl.num_programs(0)

    def row_copies(tile, slot, r):
        for j, p_ref in enumerate((p0_ref, p1_ref)):
            pltpu.make_async_copy(_token_rows(y_hbm, p_ref[tile * tm + r]),
                                  _token_rows(ybufs[slot].at[j], r),
                                  sem.at[slot]).start(priority=j)

    def wait_tile(slot):
        for j in range(2):
            pltpu.make_async_copy(y_hbm.at[pl.ds(0, tm * SUBLANES), :], ybufs[slot].at[j],
                                  sem.at[slot]).wait()

    @pl.when(i == 0)
    def _():
        for t0 in range(ahead):
            @pl.when(t0 < nt)
            def _():
                def body(r, carry):
                    row_copies(t0, t0, r)
                    return carry
                lax.fori_loop(0, tm, body, 0, unroll=ROW_DMA_UNROLL)

    n_chunks = SUBLANES
    batch = tm // n_chunks

    def step(slot):
        wait_tile(slot)
        nxt = jnp.minimum(i + ahead, nt - 1)
        nslot = (slot + ahead) % ns
        slab = slab_ref[...]
        w0 = slab[:, 2:3]
        w1 = slab[:, 3:4]
        gate_f = mod_ref[0][5:6]
        sq = jnp.zeros((tm, LANES), F32)
        for c in range(n_chunks):
            lanes = slice(c * LANES, (c + 1) * LANES)
            y0 = ybufs[slot].at[0][pl.ds(c, tm, stride=SUBLANES), :]
            y1 = ybufs[slot].at[1][pl.ds(c, tm, stride=SUBLANES), :]
            xo = x1_ref[:, lanes] + gate_f[:, lanes] * (w0 * y0 + w1 * y1)
            sq = sq + xo * xo
            o_ref[:, lanes] = xo
            for r in range(c * batch, (c + 1) * batch):
                row_copies(nxt, nslot, r)
        ms = jnp.sum(sq, axis=-1, keepdims=True) * (1.0 / (n_chunks * LANES))
        scale = lax.rsqrt(ms + EPS)
        for c in range(n_chunks):
            lanes = slice(c * LANES, (c + 1) * LANES)
            o_ref[:, lanes] = o_ref[:, lanes] * scale * gfin_ref[:, lanes]

    for slot in range(ns):
        @pl.when(i % ns == slot)
        def _():
            step(slot)

    @pl.when(i == nt - 1)
    def _():
        for k in range(ahead):
            for slot in range(ns):
                @pl.when((nt - 1 - k >= 0) & ((nt - 1 - k + ahead) % ns == slot))
                def _():
                    wait_tile(slot)


def _combine(p0, p1, y, x1, slab, mod, gfin, seq):
    n, d = x1.shape
    tm = COMBINE_TILE
    tiles_per_seq = seq // tm
    grid_spec = pltpu.PrefetchScalarGridSpec(
        num_scalar_prefetch=2,
        grid=(n // tm,),
        in_specs=[
            pl.BlockSpec(memory_space=pl.ANY),
            pl.BlockSpec((tm, d), lambda i, a, b: (i, 0)),
            pl.BlockSpec((tm, ROUTER_LANES), lambda i, a, b: (i, 0)),
            pl.BlockSpec((1, 6, d), lambda i, a, b: (i // tiles_per_seq, 0, 0)),
            pl.BlockSpec((1, d), lambda i, a, b: (0, 0)),
        ],
        out_specs=pl.BlockSpec((tm, d), lambda i, a, b: (i, 0)),
        scratch_shapes=([pltpu.VMEM((2, tm * SUBLANES, LANES), F32)] * COMBINE_SLOTS
                        + [pltpu.SemaphoreType.DMA((COMBINE_SLOTS,))]),
    )
    return pl.pallas_call(
        _combine_kernel,
        out_shape=jax.ShapeDtypeStruct((n, d), F32),
        grid_spec=grid_spec,
        compiler_params=pltpu.CompilerParams(
            dimension_semantics=("arbitrary",),
            vmem_limit_bytes=VMEM_LIMIT_BYTES),
        name="combine",
    )(p0, p1, y, x1, slab, mod, gfin)


def kernel(x, c, positions, ada_w, ada_b, norm_mix_g, norm_ffn_g, w_in, conv_w, conv_b,
           beta_ret, beta_conv, w_out, router_group_w, router_group_b, router_expert_w,
           router_expert_b, expert_w_gate, expert_w_up, expert_w_down, norm_final_g):
    bsz, seq, d = x.shape
    n = bsz * seq
    depth = ada_w.shape[0]
    assert depth == 1, "the combine kernel fuses the trunk's final RMSNorm (single layer)"
    W = RET_HEADS * HEAD_DIM

    l = 0
    mod, cos128, sin128 = _prologue(c, ada_w[l], ada_b[l], positions)
    mod = mod.reshape(bsz, 6, d)
    heads = jnp.arange(RET_HEADS, dtype=F32)
    lg = jnp.log1p(-jnp.exp2(-5.0 - heads))
    lgl = jnp.repeat(lg, HEAD_DIM).reshape(1, W)
    assert CONV_GROUP_DIM == HEAD_DIM, "conv groups and retention heads share the 64-lane block sums"
    blk_np = np.kron(np.eye(RET_HEADS // 2, dtype=np.float32),
                     np.ones((HEAD_DIM, HEAD_DIM), np.float32))
    blk = jnp.asarray(blk_np, dtype=BF16)

    gap = ROUTER_EXPERT_ROW0 - N_GROUPS
    tail = ROUTER_LANES - ROUTER_ROWS
    wr = jnp.concatenate([router_group_w[l].T, jnp.zeros((gap, d), F32),
                          router_expert_w[l].T, jnp.zeros((tail, d), F32)], axis=0)
    wr2 = jnp.concatenate(_split_bf16(wr), axis=0)
    br = jnp.concatenate([router_group_b[l], jnp.zeros((gap,), F32),
                          router_expert_b[l], jnp.zeros((tail,), F32)]).reshape(ROUTER_LANES, 1)

    x1, h2t, logits_t = _mixer(
        x, mod, cos128, sin128, norm_mix_g[l].reshape(1, d), w_in[l],
        conv_w[l], conv_b[l].reshape(1, W), beta_ret[l].reshape(1, W),
        beta_conv[l].reshape(1, W), w_out[l], norm_ffn_g[l].reshape(1, d),
        wr2, br, lg, lgl, blk)
    slab, dest, plan = _router(logits_t)
    p0, p1 = dest[0], dest[1]
    p_rows = 2 * n + N_EXPERTS * EXPERT_TILE

    xs = _dispatch(p0, p1, plan, h2t, p_rows)
    de = expert_w_gate.shape[-1]
    y = _experts(plan, xs,
                 expert_w_gate[l].reshape(N_EXPERTS, d, de),
                 expert_w_up[l].reshape(N_EXPERTS, d, de),
                 expert_w_down[l].reshape(N_EXPERTS, de, d))
    out = _combine(p0, p1, y, x1.reshape(n, d), slab, mod, norm_final_g.reshape(1, d), seq)
    return out.reshape(bsz, seq, d)
```

```python
import jax
import jax.numpy as jnp
import numpy as np
from jax import lax
from jax.experimental import pallas as pl
from jax.experimental.pallas import tpu as pltpu

F32 = jnp.float32
BF16 = jnp.bfloat16

CHUNK = 64
RET_HEADS = 8
HEAD_DIM = 64
CONV_GROUP_DIM = 64
ROPE_BASE = 10000.0
N_GROUPS = 4
EXPERTS_PER_GROUP = 8
N_EXPERTS = N_GROUPS * EXPERTS_PER_GROUP
EPS = 1e-6
GN_EPS = 1e-5

LANES = 128
SUBLANES = 8
VMEM_LIMIT_BYTES = 56 * 1024 * 1024

SEQ_TILE = 256
EXPERT_TILE = 256
COMBINE_TILE = 256
ROUTER_LANES = LANES
ROUTER_GROUP_ROW0 = 0
ROUTER_EXPERT_ROW0 = SUBLANES
ROUTER_ROWS = ROUTER_EXPERT_ROW0 + N_EXPERTS


def _silu(v):
    return v * (1.0 / (1.0 + jnp.exp(-v)))


def _split_bf16(v):
    hi = v.astype(BF16)
    return hi, (v - hi.astype(F32)).astype(BF16)


def _adaln_block(c_ref, w_ref, b_ref, o_ref):
    s_hi, s_lo = _split_bf16(_silu(c_ref[...]))
    w_hi, w_lo = _split_bf16(w_ref[...])
    o_ref[...] = (jnp.dot(s_hi, w_hi, preferred_element_type=F32)
                  + jnp.dot(s_lo, w_hi, preferred_element_type=F32)
                  + jnp.dot(s_hi, w_lo, preferred_element_type=F32) + b_ref[...])


def _rope_block(pos_ref, invf_ref, cos_ref, sin_ref):
    half = HEAD_DIM // 2
    per_row = LANES // half
    r = pos_ref.shape[0]
    ang = pos_ref[...].astype(F32) * invf_ref[...]
    lane = lax.broadcasted_iota(jnp.int32, (r, LANES), 1)
    quarter = lane // half
    sign = jnp.where(quarter % 2 == 0, -1.0, 1.0)
    for table, out_ref, scale in ((jnp.cos(ang), cos_ref, None), (jnp.sin(ang), sin_ref, sign)):
        rolled = [table] + [pltpu.roll(table, half * k, 1) for k in range(1, per_row)]
        for q in range(per_row):
            val = rolled[(0 - q) % per_row]
            for k in range(1, per_row):
                val = jnp.where(quarter == k, rolled[(k - q) % per_row], val)
            if scale is not None:
                val = val * scale
            out_ref[pl.ds(q, r, stride=per_row), :] = val


PROLOGUE_STEPS = 8


def _prologue_kernel(c_ref, w_ref, b_ref, pos_ref, invf_ref, mod_ref, cos_ref, sin_ref):
    _adaln_block(c_ref, w_ref, b_ref, mod_ref)
    _rope_block(pos_ref, invf_ref, cos_ref, sin_ref)


def _prologue(c, w, b, positions):
    bsz, d = c.shape
    n_mod = w.shape[1]
    n = positions.size
    half = HEAD_DIM // 2
    inv_freq = ROPE_BASE ** (-jnp.arange(0, HEAD_DIM, 2, dtype=F32) / HEAD_DIM)
    per_row = LANES // half
    rows = n // per_row
    pos_rep = jnp.broadcast_to(positions.reshape(n, 1), (n, half)).reshape(rows, LANES)
    invf = jnp.tile(inv_freq, per_row).reshape(1, LANES)
    steps = PROLOGUE_STEPS
    tn, tr = n_mod // steps, rows // steps
    assert tn % LANES == 0 and tr % SUBLANES == 0
    return pl.pallas_call(
        _prologue_kernel,
        out_shape=(jax.ShapeDtypeStruct((bsz, n_mod), F32),
                   jax.ShapeDtypeStruct((n, LANES), F32),
                   jax.ShapeDtypeStruct((n, LANES), F32)),
        grid=(steps,),
        in_specs=[pl.BlockSpec((bsz, d), lambda j: (0, 0)),
                  pl.BlockSpec((d, tn), lambda j: (0, j)),
                  pl.BlockSpec((1, tn), lambda j: (0, j)),
                  pl.BlockSpec((tr, LANES), lambda j: (j, 0)),
                  pl.BlockSpec((1, LANES), lambda j: (0, 0))],
        out_specs=(pl.BlockSpec((bsz, tn), lambda j: (0, j)),
                   pl.BlockSpec((tr * per_row, LANES), lambda j: (j, 0)),
                   pl.BlockSpec((tr * per_row, LANES), lambda j: (j, 0))),
        name="prologue",
    )(c, w, b.reshape(1, n_mod), pos_rep, invf)


def _load_token_rows(ref, rows):
    return jnp.concatenate(
        [ref[pl.ds(c, rows, stride=SUBLANES), :] for c in range(SUBLANES)], axis=1)


def _store_token_rows(ref, val, row0=0):
    rows = val.shape[0]
    for c in range(SUBLANES):
        ref[pl.ds(row0 * SUBLANES + c, rows, stride=SUBLANES), :] = val[:, c * LANES:(c + 1) * LANES]


def _token_rows(ref, row):
    return ref.at[pl.ds(pl.multiple_of(row * SUBLANES, SUBLANES), SUBLANES), :]


MIXER_SUBTILES = 2


MIXER_WIN, MIXER_WOUT = 6, 11


def _mixer_kernel(*refs):
    *refs, winb_ref, woutb_ref, dec_ref, qdec_ref, kdec_ref = refs
    state_ref, ubuf_ref = refs[-2:]
    L = SEQ_TILE

    @pl.when((pl.program_id(0) == 0) & (pl.program_id(1) == 0))
    def _():
        for src, dst in ((refs[MIXER_WIN], winb_ref), (refs[MIXER_WOUT], woutb_ref)):
            step = RET_HEADS * HEAD_DIM
            for c0 in range(0, src.shape[1], step):
                dst[:, c0:c0 + step] = src[:, c0:c0 + step].astype(BF16)
        lg_ref, lgl_ref = refs[0], refs[15]
        ii = lax.broadcasted_iota(jnp.int32, (L, L), 0)
        jj = lax.broadcasted_iota(jnp.int32, (L, L), 1)
        dist = jnp.abs(ii - jj).astype(F32)
        allowed = (jj // CHUNK) <= (ii // CHUNK)
        for head in range(RET_HEADS):
            dec_ref[head] = jnp.where(allowed, jnp.exp(lg_ref[head] * dist), 0.0)
        rowf = lax.broadcasted_iota(jnp.int32, qdec_ref.shape, 0).astype(F32)
        qdec_ref[...] = jnp.exp(lgl_ref[...] * (rowf + 1.0))
        kdec_ref[...] = jnp.exp(lgl_ref[...] * (float(L - 1) - rowf))

    @pl.when(pl.program_id(1) == 0)
    def _():
        state_ref[...] = jnp.zeros_like(state_ref)
        ubuf_ref[0:SUBLANES, :] = jnp.zeros((SUBLANES, ubuf_ref.shape[1]), F32)

    refs[MIXER_WIN], refs[MIXER_WOUT] = winb_ref, woutb_ref
    for sub in range(MIXER_SUBTILES):
        _mixer_tile(sub, *refs, dec_ref, qdec_ref, kdec_ref)


def _mixer_tile(sub, lg_ref, x_ref, mod_ref, cos_ref, sin_ref, gmix_ref, win_ref, convw_ref,
                convb_ref, bret_ref, bconv_ref, wout_ref, gffn_ref, wr_ref, br_ref, lgl_ref,
                blk_ref, x1_ref, h2_ref, logit_ref, state_ref, ubuf_ref,
                dec_ref, qdec_ref, kdec_ref):
    L = SEQ_TILE
    W = RET_HEADS * HEAD_DIM
    tile_rows = slice(sub * L, (sub + 1) * L)

    x = x_ref[0, tile_rows, :]
    mod = mod_ref[0]
    shift_m, scale_m, gate_m = mod[0:1], mod[1:2], mod[2:3]
    shift_f, scale_f = mod[3:4], mod[4:5]

    ms = jnp.mean(x * x, axis=-1, keepdims=True)
    h = x * lax.rsqrt(ms + EPS) * gmix_ref[...]
    h = h * (1.0 + scale_m) + shift_m
    hb = h.astype(BF16)

    def proj(i):
        return jnp.dot(hb, win_ref[:, i * W:(i + 1) * W], preferred_element_type=F32)

    cos = jnp.concatenate([cos_ref[tile_rows, :]] * 4, axis=1)
    sin = jnp.concatenate([sin_ref[tile_rows, :]] * 4, axis=1)
    lane_w = lax.broadcasted_iota(jnp.int32, (L, W), 1)
    first_half = (lane_w & (HEAD_DIM - 1)) < (HEAD_DIM // 2)

    def rot(t):
        partner = jnp.where(first_half, pltpu.roll(t, W - HEAD_DIM // 2, 1),
                            pltpu.roll(t, HEAD_DIM // 2, 1))
        return t * cos + partner * sin

    q = rot(proj(0))
    k = rot(proj(1)) * (HEAD_DIM ** -0.5)
    v = proj(2)
    vb = v.astype(BF16)
    kb = k.astype(BF16)

    lgl = lgl_ref[...]
    qd = q * qdec_ref[...]
    kd = k * kdec_ref[...]
    blk = blk_ref[...]
    HW = W // 2
    blk_f = blk.astype(F32)
    qdb = qd.astype(BF16)
    kdb = kd.astype(BF16)
    state_decay = jnp.exp(lgl * float(L))
    inter = []
    for hf in range(2):
        sl = slice(hf * HW, (hf + 1) * HW)
        st = state_ref[hf]
        inter.append(jnp.dot(qdb[:, sl], st.astype(BF16), preferred_element_type=F32))
        kv = lax.dot_general(kdb[:, sl], vb[:, sl], (((0,), (0,)), ((), ())),
                             preferred_element_type=F32)
        state_ref[hf] = st * state_decay[:, sl] + kv * blk_f
    y_inter = jnp.concatenate(inter, axis=1)

    def head_sums(t):
        tb = t.astype(BF16)
        return jnp.concatenate(
            [jnp.dot(tb[:, hf * HW:(hf + 1) * HW], blk, preferred_element_type=F32)
             for hf in range(2)], axis=1)

    lane_p = lax.broadcasted_iota(jnp.int32, (L, LANES), 1)
    lo_head = lane_p < HEAD_DIM
    pairs = []
    for p in range(RET_HEADS // 2):
        sl = slice(p * LANES, (p + 1) * LANES)
        qp, kp, vp = q[:, sl], kb[:, sl], vb[:, sl]
        ys = []
        for hh in range(2):
            head = 2 * p + hh
            keep = lo_head if hh == 0 else jnp.logical_not(lo_head)
            qh = jnp.where(keep, qp, 0.0).astype(BF16)
            sc = lax.dot_general(qh, kp, (((1,), (1,)), ((), ())),
                                 preferred_element_type=F32)
            ys.append(jnp.dot((sc * dec_ref[head]).astype(BF16), vp, preferred_element_type=F32))
        pairs.append(jnp.where(lo_head, ys[0], ys[1]))
    y = jnp.concatenate(pairs, axis=1) + y_inter

    inv_hd = 1.0 / HEAD_DIM
    mu = head_sums(y) * inv_hd
    d = y - mu
    var = head_sums(d * d) * inv_hd
    g = proj(3)
    y_ret = _silu(g) * (d * lax.rsqrt(var + GN_EPS)) * bret_ref[...]

    b_gate = proj(4)
    u = proj(5) * proj(6)
    ubuf_ref[SUBLANES:SUBLANES + L, :] = u
    u1 = ubuf_ref[SUBLANES - 1:SUBLANES - 1 + L, :]
    u2 = ubuf_ref[SUBLANES - 2:SUBLANES - 2 + L, :]
    ubuf_ref[0:SUBLANES, :] = ubuf_ref[L:L + SUBLANES, :]
    cw = convw_ref[...]
    conv = u2 * cw[0:1] + u1 * cw[1:2] + u * cw[2:3] + convb_ref[...]
    yc = b_gate * conv
    msc = head_sums(yc * yc) * (1.0 / CONV_GROUP_DIM)
    y_conv = yc * lax.rsqrt(msc + EPS) * bconv_ref[...]

    mix = (jnp.dot(y_ret.astype(BF16), wout_ref[0:W, :], preferred_element_type=F32)
           + jnp.dot(y_conv.astype(BF16), wout_ref[W:2 * W, :], preferred_element_type=F32))
    x1 = x + gate_m * mix
    x1_ref[0, tile_rows, :] = x1

    ms2 = jnp.mean(x1 * x1, axis=-1, keepdims=True)
    h2 = x1 * lax.rsqrt(ms2 + EPS) * gffn_ref[...]
    h2 = h2 * (1.0 + scale_f) + shift_f
    _store_token_rows(h2_ref, h2, row0=sub * L)

    hi, lo = _split_bf16(h2)
    w2 = wr_ref[...]
    nt_dims = (((1,), (1,)), ((), ()))
    parts = (lax.dot_general(w2, hi, nt_dims, preferred_element_type=F32)
             + lax.dot_general(w2, lo, nt_dims, preferred_element_type=F32))
    logits_t = parts[:ROUTER_LANES] + parts[ROUTER_LANES:] + br_ref[...]
    logit_ref[:, tile_rows] = logits_t[:ROUTER_ROWS]


def _mixer(x, mod, cos128, sin128, gmix, win, convw, convb, bret, bconv, wout, gffn,
           wr2, br, lg, lgl, blk):
    bsz, seq, d = x.shape
    L = SEQ_TILE * MIXER_SUBTILES
    ns = seq // L
    W = RET_HEADS * HEAD_DIM
    n = bsz * seq
    const2 = lambda b, s: (0, 0)
    in_specs = [
        pl.BlockSpec(memory_space=pltpu.SMEM),
        pl.BlockSpec((1, L, d), lambda b, s: (b, s, 0)),
        pl.BlockSpec((1, 6, d), lambda b, s: (b, 0, 0)),
        pl.BlockSpec((L, LANES), lambda b, s: (b * ns + s, 0)),
        pl.BlockSpec((L, LANES), lambda b, s: (b * ns + s, 0)),
        pl.BlockSpec((1, d), const2),
        pl.BlockSpec(win.shape, const2, pipeline_mode=pl.Buffered(1)),
        pl.BlockSpec(convw.shape, const2),
        pl.BlockSpec((1, W), const2),
        pl.BlockSpec((1, W), const2),
        pl.BlockSpec((1, W), const2),
        pl.BlockSpec(wout.shape, const2, pipeline_mode=pl.Buffered(1)),
        pl.BlockSpec((1, d), const2),
        pl.BlockSpec(wr2.shape, const2),
        pl.BlockSpec((ROUTER_LANES, 1), const2),
        pl.BlockSpec((1, W), const2),
        pl.BlockSpec((W // 2, W // 2), const2),
    ]
    assert d == SUBLANES * LANES, "one token must fill exactly one (8, 128) f32 tile"
    out_shape = (jax.ShapeDtypeStruct((bsz, seq, d), F32),
                 jax.ShapeDtypeStruct((n * SUBLANES, LANES), F32),
                 jax.ShapeDtypeStruct((ROUTER_ROWS, n), F32))
    out_specs = (pl.BlockSpec((1, L, d), lambda b, s: (b, s, 0)),
                 pl.BlockSpec((L * SUBLANES, LANES), lambda b, s: (b * ns + s, 0)),
                 pl.BlockSpec((ROUTER_ROWS, L), lambda b, s: (0, b * ns + s)))
    return pl.pallas_call(
        _mixer_kernel,
        out_shape=out_shape,
        grid=(bsz, ns),
        in_specs=in_specs,
        out_specs=out_specs,
        scratch_shapes=[pltpu.VMEM((2, W // 2, W // 2), F32),
                        pltpu.VMEM((SEQ_TILE + 2 * SUBLANES, W), F32),
                        pltpu.VMEM(win.shape, BF16),
                        pltpu.VMEM(wout.shape, BF16),
                        pltpu.VMEM((RET_HEADS, SEQ_TILE, SEQ_TILE), F32),
                        pltpu.VMEM((SEQ_TILE, W), F32),
                        pltpu.VMEM((SEQ_TILE, W), F32)],
        compiler_params=pltpu.CompilerParams(
            dimension_semantics=("arbitrary", "arbitrary"),
            vmem_limit_bytes=VMEM_LIMIT_BYTES),
        name="mixer",
    )(lg, x, mod, cos128, sin128, gmix, win, convw, convb, bret, bconv, wout, gffn,
      wr2, br, lgl, blk)


ROUTER_TILE = 4096
RES_E1, RES_E2, RES_W1, RES_W2, RES_A, RES_B = range(6)
PLAN_FIRST_TILE, PLAN_N_TILE, PLAN_PAD_START, PLAN_PAD_N, PLAN_N_USED = range(5)


def _rows8(vals, width):
    rid = lax.broadcasted_iota(jnp.int32, (SUBLANES, width), 0)
    out = jnp.zeros((SUBLANES, width), F32)
    for r, v in enumerate(vals):
        out = jnp.where(rid == r, v, out)
    return out


def _router_kernel(lt_ref, slab_ref, dest1_ref, dest2_ref, plan_ref, res_ref, cnt_ref):
    T = ROUTER_TILE
    phase = pl.program_id(0)
    j = pl.program_id(1)
    nblk = T // LANES
    big = F32(1e9)
    rid8 =lax.broadcasted_iota(jnp.int32, (SUBLANES, T), 0).astype(F32)
    rid_e = lax.broadcasted_iota(jnp.int32, (N_EXPERTS, LANES), 0).astype(F32)

    def onehot(e_row, k):
        return jnp.where(rid_e == e_row[:, k * LANES:(k + 1) * LANES], 1.0, 0.0)

    @pl.when((phase == 0) & (j == 0))
    def _():
        cnt_ref[...] = jnp.zeros_like(cnt_ref)

    @pl.when(phase == 0)
    def _():
        lt = lt_ref[...]
        g_rows = lt[ROUTER_GROUP_ROW0:ROUTER_GROUP_ROW0 + SUBLANES]
        gvalid = rid8 < float(N_GROUPS)
        gm = jnp.where(gvalid, g_rows, F32(-jnp.inf))
        gexp = jnp.exp(gm - jnp.max(gm, axis=0, keepdims=True))
        gp = gexp / jnp.sum(gexp, axis=0, keepdims=True)
        g_top = jnp.max(gp, axis=0, keepdims=True)
        g_idx = jnp.min(jnp.where(gvalid & (gp == g_top), rid8, big), axis=0, keepdims=True)

        def group_slab(g):
            r0 = ROUTER_EXPERT_ROW0 + g * EXPERTS_PER_GROUP
            return lt[r0:r0 + EXPERTS_PER_GROUP]
        sel = group_slab(N_GROUPS - 1)
        for g in range(N_GROUPS - 2, -1, -1):
            sel = jnp.where(g_idx == float(g), group_slab(g), sel)
        eexp = jnp.exp(sel - jnp.max(sel, axis=0, keepdims=True))
        ep = eexp / jnp.sum(eexp, axis=0, keepdims=True)
        p1 = jnp.max(ep, axis=0, keepdims=True)
        i1 = jnp.min(jnp.where(ep == p1, rid8, big), axis=0, keepdims=True)
        m2 = rid8 != i1
        p2 = jnp.max(jnp.where(m2, ep, -1.0), axis=0, keepdims=True)
        i2 = jnp.min(jnp.where(m2 & (ep == p2), rid8, big), axis=0, keepdims=True)
        den = p1 + p2
        w1 = p1 / den * g_top
        w2 = p2 / den * g_top
        e1 = g_idx * float(EXPERTS_PER_GROUP) + i1
        e2 = g_idx * float(EXPERTS_PER_GROUP) + i2

        ii = lax.broadcasted_iota(jnp.int32, (LANES, LANES), 0)
        jj = lax.broadcasted_iota(jnp.int32, (LANES, LANES), 1)
        upper = jnp.where(ii < jj, 1.0, 0.0).astype(BF16)
        base = cnt_ref[...]
        ranks1, ranks2 = [], []
        for k in range(nblk):
            o1, o2 = onehot(e1, k), onehot(e2, k)
            r1 = jnp.dot(o1.astype(BF16), upper, preferred_element_type=F32)
            r2 = jnp.dot(o2.astype(BF16), upper, preferred_element_type=F32)
            c1 = jnp.sum(o1, axis=1, keepdims=True)
            c2 = jnp.sum(o2, axis=1, keepdims=True)
            ranks1.append(jnp.sum(o1 * (base + r1), axis=0, keepdims=True))
            ranks2.append(jnp.sum(o2 * (base + c1 + r2), axis=0, keepdims=True))
            base = base + c1 + c2
        cnt_ref[...] = base
        rank1 = jnp.concatenate(ranks1, axis=1)
        rank2 = jnp.concatenate(ranks2, axis=1)
        res_ref[j] = _rows8([e1, e2, w1, w2, rank1, rank2], T)

    @pl.when(phase == 1)
    def _():
        tm = float(EXPERT_TILE)
        cnt = cnt_ref[...]
        tiles = jnp.floor((cnt + (tm - 1.0)) * (1.0 / tm))
        ei = lax.broadcasted_iota(jnp.int32, (N_EXPERTS, N_EXPERTS), 0)
        ej = lax.broadcasted_iota(jnp.int32, (N_EXPERTS, N_EXPERTS), 1)
        lower = jnp.where(ej < ei, 1.0, 0.0).astype(BF16)
        first = jnp.dot(lower, tiles.astype(BF16), preferred_element_type=F32)
        starts = first * tm

        res = res_ref[j]
        e1, e2 = res[RES_E1:RES_E1 + 1], res[RES_E2:RES_E2 + 1]
        d1, d2 = [], []
        for k in range(nblk):
            blk_lanes = slice(k * LANES, (k + 1) * LANES)
            d1.append(res[RES_A:RES_A + 1, blk_lanes]
                      + jnp.sum(onehot(e1, k) * starts, axis=0, keepdims=True))
            d2.append(res[RES_B:RES_B + 1, blk_lanes]
                      + jnp.sum(onehot(e2, k) * starts, axis=0, keepdims=True))
        dest1 = jnp.concatenate(d1, axis=1)
        dest2 = jnp.concatenate(d2, axis=1)
        dest1_ref[...] = dest1.astype(jnp.int32)
        dest2_ref[...] = dest2.astype(jnp.int32)

        table = jnp.concatenate(
            [_rows8([e1, e2, res[RES_W1:RES_W1 + 1], res[RES_W2:RES_W2 + 1], dest1, dest2], T),
             jnp.zeros((ROUTER_LANES - SUBLANES, T), F32)], axis=0)
        slab_ref[...] = table.T

        @pl.when(j == 0)
        def _():
            lane_e = lax.broadcasted_iota(jnp.int32, (N_EXPERTS, LANES), 1).astype(F32)

            def as_row(col):
                return jnp.sum(jnp.where(rid_e == lane_e, col, 0.0), axis=0, keepdims=True)
            n_used = jnp.sum(tiles, axis=0, keepdims=True)
            padded = tiles * tm
            plan_ref[...] = _rows8([as_row(first), as_row(tiles), as_row(starts + cnt),
                                    as_row(padded - cnt), n_used], LANES).astype(jnp.int32)


def _router(logits_t):
    n = logits_t.shape[1]
    T = ROUTER_TILE
    nt = n // T
    return pl.pallas_call(
        _router_kernel,
        out_shape=(jax.ShapeDtypeStruct((n, ROUTER_LANES), F32),
                   jax.ShapeDtypeStruct((1, n), jnp.int32),
                   jax.ShapeDtypeStruct((1, n), jnp.int32),
                   jax.ShapeDtypeStruct((SUBLANES, LANES), jnp.int32)),
        grid=(2, nt),
        in_specs=[pl.BlockSpec((ROUTER_ROWS, T), lambda p, j: (0, j * (1 - p) + (nt - 1) * p))],
        out_specs=(pl.BlockSpec((T, ROUTER_LANES), lambda p, j: (j * p, 0)),
                   pl.BlockSpec((1, T), lambda p, j: (0, j * p)),
                   pl.BlockSpec((1, T), lambda p, j: (0, j * p)),
                   pl.BlockSpec((SUBLANES, LANES), lambda p, j: (0, 0))),
        scratch_shapes=[pltpu.VMEM((nt, SUBLANES, T), F32),
                        pltpu.VMEM((N_EXPERTS, LANES), F32)],
        compiler_params=pltpu.CompilerParams(
            dimension_semantics=("arbitrary", "arbitrary"), vmem_limit_bytes=VMEM_LIMIT_BYTES),
        name="router",
    )(logits_t)


DISPATCH_TILE = 512
ROW_DMA_UNROLL = 8
PAD_UNITS = tuple(1 << b for b in reversed(range(EXPERT_TILE.bit_length() - 1)))


DISPATCH_LAG = 2
DISPATCH_SLOTS = DISPATCH_LAG + 2


def _dispatch_kernel(d0_ref, d1_ref, plan_ref, h2_hbm, xs_hbm,
                     stage, zbuf, isem, ssem, zsem):
    dt = DISPATCH_TILE
    ps = SUBLANES
    i = pl.program_id(0)
    nsteps = pl.num_programs(0)
    zrows = PAD_UNITS[0]

    def in_copy(blk, slot):
        src = h2_hbm.at[pl.ds(pl.multiple_of(blk * (dt * ps), dt * ps), dt * ps), :]
        return pltpu.make_async_copy(src, stage.at[slot], isem.at[slot])

    def wait_rows(slot):
        for _ in range(2):
            pltpu.make_async_copy(stage.at[slot], xs_hbm.at[pl.ds(0, dt * ps), :],
                                  ssem.at[slot]).wait()

    def pad_copy(start, unit):
        return pltpu.make_async_copy(zbuf.at[pl.ds(0, unit * ps), :],
                                     xs_hbm.at[pl.ds(pl.multiple_of(start * ps, ps), unit * ps), :],
                                     zsem)

    def pad_pass(do):
        def per_expert(e, carry):
            start = plan_ref[PLAN_PAD_START, e]
            npad = plan_ref[PLAN_PAD_N, e]
            for unit in PAD_UNITS:
                @pl.when((npad & unit) != 0)
                def _():
                    do(pad_copy(start + (npad & ~(2 * unit - 1)), unit))
            return carry
        lax.fori_loop(0, N_EXPERTS, per_expert, 0)

    def tail_pass(do):
        def per_unit(k, carry):
            do(pad_copy(k * zrows, zrows))
            return carry
        per_tile = EXPERT_TILE // zrows
        n_units = xs_hbm.shape[0] // (zrows * ps)
        lax.fori_loop(plan_ref[PLAN_N_USED, 0] * per_tile, n_units, per_unit, 0)

    slot = i % DISPATCH_SLOTS

    @pl.when(i == 0)
    def _():
        in_copy(0, 0).start()

        @pl.when(nsteps > 1)
        def _():
            in_copy(1, 1).start()

        zbuf[...] = jnp.zeros_like(zbuf)
        pad_pass(lambda cp: cp.start())
        tail_pass(lambda cp: cp.start())

    @pl.when(i >= DISPATCH_LAG)
    def _():
        wait_rows((i - DISPATCH_LAG) % DISPATCH_SLOTS)

    @pl.when(i + 2 < nsteps)
    def _():
        in_copy(i + 2, (i + 2) % DISPATCH_SLOTS).start()

    in_copy(i, slot).wait()
    base = i * dt
    src_ref = stage.at[slot]

    def body(r, carry):
        src = _token_rows(src_ref, r)
        for prio, d_ref in enumerate((d0_ref, d1_ref)):
            pltpu.make_async_copy(src, _token_rows(xs_hbm, d_ref[base + r]),
                                  ssem.at[slot]).start(priority=prio)
        return carry

    lax.fori_loop(0, dt, body, 0, unroll=ROW_DMA_UNROLL)

    @pl.when(i == nsteps - 1)
    def _():
        for back in range(DISPATCH_LAG - 1, -1, -1):
            @pl.when(i - back >= 0)
            def _():
                wait_rows((i - back) % DISPATCH_SLOTS)
        pad_pass(lambda cp: cp.wait())
        tail_pass(lambda cp: cp.wait())


def _dispatch(dest0, dest1, plan, h2t, p_rows):
    n = dest0.shape[0]
    dt = DISPATCH_TILE
    grid_spec = pltpu.PrefetchScalarGridSpec(
        num_scalar_prefetch=3,
        grid=(n // dt,),
        in_specs=[pl.BlockSpec(memory_space=pl.ANY)],
        out_specs=pl.BlockSpec(memory_space=pl.ANY),
        scratch_shapes=[pltpu.VMEM((DISPATCH_SLOTS, dt * SUBLANES, LANES), F32),
                        pltpu.VMEM((PAD_UNITS[0] * SUBLANES, LANES), F32),
                        pltpu.SemaphoreType.DMA((DISPATCH_SLOTS,)),
                        pltpu.SemaphoreType.DMA((DISPATCH_SLOTS,)),
                        pltpu.SemaphoreType.DMA(())],
    )
    return pl.pallas_call(
        _dispatch_kernel,
        out_shape=jax.ShapeDtypeStruct((p_rows * SUBLANES, LANES), F32),
        grid_spec=grid_spec,
        compiler_params=pltpu.CompilerParams(dimension_semantics=("arbitrary",)),
        name="dispatch",
    )(dest0, dest1, plan, h2t)


EXPERT_IN_SLOTS = 4
EXPERT_OUT_SLOTS = 3
MXU_COLS = 256


def _expert_kernel(plan_ref, xs_hbm, wg_ref, wu_ref, wd_ref, y_hbm,
                   xbuf, ybuf, zbuf, wgb, wub, wdb, isem, osem, zsem):
    tm = EXPERT_TILE
    rows = tm * SUBLANES
    ni, no = EXPERT_IN_SLOTS, EXPERT_OUT_SLOTS
    e = pl.program_id(0)
    n_used = plan_ref[PLAN_N_USED, 0]
    n_mine = plan_ref[PLAN_N_TILE, e]
    n_tiles = y_hbm.shape[0] // rows

    def tile_rows(ref, g):
        return ref.at[pl.ds(pl.multiple_of(g * rows, rows), rows), :]

    def in_copy(g):
        return pltpu.make_async_copy(tile_rows(xs_hbm, g), xbuf.at[g % ni], isem.at[g % ni])

    def out_copy(g):
        return pltpu.make_async_copy(ybuf.at[g % no], tile_rows(y_hbm, g), osem.at[g % no])

    def zero_tail(do):
        def per_tile(g, carry):
            do(pltpu.make_async_copy(zbuf, tile_rows(y_hbm, g), zsem))
            return carry
        lax.fori_loop(n_used, n_tiles, per_tile, 0)

    @pl.when(e == 0)
    def _():
        for g0 in range(ni - 1):
            @pl.when(g0 < n_used)
            def _():
                in_copy(g0).start()
        zbuf[...] = jnp.zeros_like(zbuf)
        zero_tail(lambda cp: cp.start())

    @pl.when(n_mine > 0)
    def _():
        wgb[...] = wg_ref[0].astype(BF16)
        wub[...] = wu_ref[0].astype(BF16)
        wdb[...] = wd_ref[0].astype(BF16)

    def tile(g, carry):
        in_copy(g).wait()

        @pl.when(g >= no)
        def _():
            out_copy(g - no).wait()

        @pl.when(g + ni - 1 < n_used)
        def _():
            in_copy(g + ni - 1).start()

        xb = _load_token_rows(xbuf.at[g % ni], tm).astype(BF16)
        a = jnp.dot(xb, wgb[...], preferred_element_type=F32)
        u = jnp.dot(xb, wub[...], preferred_element_type=F32)
        hid = (_silu(a) * u).astype(BF16)
        out = ybuf.at[g % no]
        per_piece = MXU_COLS // LANES
        for p in range(wdb.shape[1] // MXU_COLS):
            y = jnp.dot(hid, wdb[:, p * MXU_COLS:(p + 1) * MXU_COLS], preferred_element_type=F32)
            for q in range(per_piece):
                out[pl.ds(p * per_piece + q, tm, stride=SUBLANES), :] = y[:, q * LANES:(q + 1) * LANES]
        out_copy(g).start()
        return carry

    first = plan_ref[PLAN_FIRST_TILE, e]
    lax.fori_loop(first, first + n_mine, tile, 0)

    @pl.when(e == pl.num_programs(0) - 1)
    def _():
        for back in range(no, 0, -1):
            @pl.when(n_used >= back)
            def _():
                out_copy(n_used - back).wait()
        zero_tail(lambda cp: cp.wait())


def _experts(plan, xs, wg, wu, wd):
    tm = EXPERT_TILE
    n_exp, d, de = wg.shape
    grid_spec = pltpu.PrefetchScalarGridSpec(
        num_scalar_prefetch=1,
        grid=(n_exp,),
        in_specs=[
            pl.BlockSpec(memory_space=pl.ANY),
            pl.BlockSpec((1, d, de), lambda e, *_: (e, 0, 0)),
            pl.BlockSpec((1, d, de), lambda e, *_: (e, 0, 0)),
            pl.BlockSpec((1, de, d), lambda e, *_: (e, 0, 0)),
        ],
        out_specs=pl.BlockSpec(memory_space=pl.ANY),
        scratch_shapes=[pltpu.VMEM((EXPERT_IN_SLOTS, tm * SUBLANES, LANES), F32),
                        pltpu.VMEM((EXPERT_OUT_SLOTS, tm * SUBLANES, LANES), F32),
                        pltpu.VMEM((tm * SUBLANES, LANES), F32),
                        pltpu.VMEM((d, de), BF16),
                        pltpu.VMEM((d, de), BF16),
                        pltpu.VMEM((de, d), BF16),
                        pltpu.SemaphoreType.DMA((EXPERT_IN_SLOTS,)),
                        pltpu.SemaphoreType.DMA((EXPERT_OUT_SLOTS,)),
                        pltpu.SemaphoreType.DMA(())],
    )
    return pl.pallas_call(
        _expert_kernel,
        out_shape=jax.ShapeDtypeStruct(xs.shape, F32),
        grid_spec=grid_spec,
        compiler_params=pltpu.CompilerParams(
            dimension_semantics=("arbitrary",),
            vmem_limit_bytes=VMEM_LIMIT_BYTES),
        name="experts",
    )(plan, xs, wg, wu, wd)


COMBINE_SLOTS = 3


def _combine_kernel(p0_ref, p1_ref, y_hbm, x1_ref, slab_ref, mod_ref, gfin_ref, o_ref,
                    *scratch):
    tm = COMBINE_TILE
    ns = COMBINE_SLOTS
    ahead = ns - 1
    ybufs, sem = scratch[:ns], scratch[ns]
    i = pl.program_id(0)
    nt = pl.num_programs(0)

    def row_copies(tile, slot, r):
        for j, p_ref in enumerate((p0_ref, p1_ref)):
            pltpu.make_async_copy(_token_rows(y_hbm, p_ref[tile * tm + r]),
                                  _token_rows(ybufs[slot].at[j], r),
                                  sem.at[slot]).start(priority=j)

    def wait_tile(slot):
        for j in range(2):
            pltpu.make_async_copy(y_hbm.at[pl.ds(0, tm * SUBLANES), :], ybufs[slot].at[j],
                                  sem.at[slot]).wait()

    @pl.when(i == 0)
    def _():
        for t0 in range(ahead):
            @pl.when(t0 < nt)
            def _():
                def body(r, carry):
                    row_copies(t0, t0, r)
                    return carry
                lax.fori_loop(0, tm, body, 0, unroll=ROW_DMA_UNROLL)

    n_chunks = SUBLANES
    batch = tm // n_chunks

    def step(slot):
        wait_tile(slot)
        nxt = jnp.minimum(i + ahead, nt - 1)
        nslot = (slot + ahead) % ns
        slab = slab_ref[...]
        w0 = slab[:, 2:3]
        w1 = slab[:, 3:4]
        gate_f = mod_ref[0][5:6]
        sq = jnp.zeros((tm, LANES), F32)
        for c in range(n_chunks):
            lanes = slice(c * LANES, (c + 1) * LANES)
            y0 = ybufs[slot].at[0][pl.ds(c, tm, stride=SUBLANES), :]
            y1 = ybufs[slot].at[1][pl.ds(c, tm, stride=SUBLANES), :]
            xo = x1_ref[:, lanes] + gate_f[:, lanes] * (w0 * y0 + w1 * y1)
            sq = sq + xo * xo
            o_ref[:, lanes] = xo
            for r in range(c * batch, (c + 1) * batch):
                row_copies(nxt, nslot, r)
        ms = jnp.sum(sq, axis=-1, keepdims=True) * (1.0 / (n_chunks * LANES))
        scale = lax.rsqrt(ms + EPS)
        for c in range(n_chunks):
            lanes = slice(c * LANES, (c + 1) * LANES)
            o_ref[:, lanes] = o_ref[:, lanes] * scale * gfin_ref[:, lanes]

    for slot in range(ns):
        @pl.when(i % ns == slot)
        def _():
            step(slot)

    @pl.when(i == nt - 1)
    def _():
        for k in range(ahead):
            for slot in range(ns):
                @pl.when((nt - 1 - k >= 0) & ((nt - 1 - k + ahead) % ns == slot))
                def _():
                    wait_tile(slot)


def _combine(p0, p1, y, x1, slab, mod, gfin, seq):
    n, d = x1.shape
    tm = COMBINE_TILE
    tiles_per_seq = seq // tm
    grid_spec = pltpu.PrefetchScalarGridSpec(
        num_scalar_prefetch=2,
        grid=(n // tm,),
        in_specs=[
            pl.BlockSpec(memory_space=pl.ANY),
            pl.BlockSpec((tm, d), lambda i, a, b: (i, 0)),
            pl.BlockSpec((tm, ROUTER_LANES), lambda i, a, b: (i, 0)),
            pl.BlockSpec((1, 6, d), lambda i, a, b: (i // tiles_per_seq, 0, 0)),
            pl.BlockSpec((1, d), lambda i, a, b: (0, 0)),
        ],
        out_specs=pl.BlockSpec((tm, d), lambda i, a, b: (i, 0)),
        scratch_shapes=([pltpu.VMEM((2, tm * SUBLANES, LANES), F32)] * COMBINE_SLOTS
                        + [pltpu.SemaphoreType.DMA((COMBINE_SLOTS,))]),
    )
    return pl.pallas_call(
        _combine_kernel,
        out_shape=jax.ShapeDtypeStruct((n, d), F32),
        grid_spec=grid_spec,
        compiler_params=pltpu.CompilerParams(
            dimension_semantics=("arbitrary",),
            vmem_limit_bytes=VMEM_LIMIT_BYTES),
        name="combine",
    )(p0, p1, y, x1, slab, mod, gfin)


def kernel(x, c, positions, ada_w, ada_b, norm_mix_g, norm_ffn_g, w_in, conv_w, conv_b,
           beta_ret, beta_conv, w_out, router_group_w, router_group_b, router_expert_w,
           router_expert_b, expert_w_gate, expert_w_up, expert_w_down, norm_final_g):
    bsz, seq, d = x.shape
    n = bsz * seq
    depth = ada_w.shape[0]
    assert depth == 1, "the combine kernel fuses the trunk's final RMSNorm (single layer)"
    W = RET_HEADS * HEAD_DIM

    l = 0
    mod, cos128, sin128 = _prologue(c, ada_w[l], ada_b[l], positions)
    mod = mod.reshape(bsz, 6, d)
    heads = jnp.arange(RET_HEADS, dtype=F32)
    lg = jnp.log1p(-jnp.exp2(-5.0 - heads))
    lgl = jnp.repeat(lg, HEAD_DIM).reshape(1, W)
    assert CONV_GROUP_DIM == HEAD_DIM, "conv groups and retention heads share the 64-lane block sums"
    blk_np = np.kron(np.eye(RET_HEADS // 2, dtype=np.float32),
                     np.ones((HEAD_DIM, HEAD_DIM), np.float32))
    blk = jnp.asarray(blk_np, dtype=BF16)

    gap = ROUTER_EXPERT_ROW0 - N_GROUPS
    tail = ROUTER_LANES - ROUTER_ROWS
    wr = jnp.concatenate([router_group_w[l].T, jnp.zeros((gap, d), F32),
                          router_expert_w[l].T, jnp.zeros((tail, d), F32)], axis=0)
    wr2 = jnp.concatenate(_split_bf16(wr), axis=0)
    br = jnp.concatenate([router_group_b[l], jnp.zeros((gap,), F32),
                          router_expert_b[l], jnp.zeros((tail,), F32)]).reshape(ROUTER_LANES, 1)

    x1, h2t, logits_t = _mixer(
        x, mod, cos128, sin128, norm_mix_g[l].reshape(1, d), w_in[l],
        conv_w[l], conv_b[l].reshape(1, W), beta_ret[l].reshape(1, W),
        beta_conv[l].reshape(1, W), w_out[l], norm_ffn_g[l].reshape(1, d),
        wr2, br, lg, lgl, blk)
    slab, p0, p1, plan = _router(logits_t)
    p0, p1 = p0.reshape(n), p1.reshape(n)
    p_rows = 2 * n + N_EXPERTS * EXPERT_TILE

    xs = _dispatch(p0, p1, plan, h2t, p_rows)
    de = expert_w_gate.shape[-1]
    y = _experts(plan, xs,
                 expert_w_gate[l].reshape(N_EXPERTS, d, de),
                 expert_w_up[l].reshape(N_EXPERTS, d, de),
                 expert_w_down[l].reshape(N_EXPERTS, de, d))
    out = _combine(p0, p1, y, x1.reshape(n, d), slab, mod, norm_final_g.reshape(1, d), seq)
    return out.reshape(bsz, seq, d)
```

```python
import jax
import jax.numpy as jnp
import numpy as np
from jax import lax
from jax.experimental import pallas as pl
from jax.experimental.pallas import tpu as pltpu

F32 = jnp.float32
BF16 = jnp.bfloat16

CHUNK = 64
RET_HEADS = 8
HEAD_DIM = 64
CONV_GROUP_DIM = 64
ROPE_BASE = 10000.0
N_GROUPS = 4
EXPERTS_PER_GROUP = 8
N_EXPERTS = N_GROUPS * EXPERTS_PER_GROUP
EPS = 1e-6
GN_EPS = 1e-5

LANES = 128
SUBLANES = 8
VMEM_LIMIT_BYTES = 56 * 1024 * 1024

SEQ_TILE = 256
EXPERT_TILE = 256
COMBINE_TILE = 256
ROUTER_LANES = LANES
ROUTER_GROUP_ROW0 = 0
ROUTER_EXPERT_ROW0 = SUBLANES
ROUTER_ROWS = ROUTER_EXPERT_ROW0 + N_EXPERTS


def _silu(v):
    return v * (1.0 / (1.0 + jnp.exp(-v)))


def _split_bf16(v):
    hi = v.astype(BF16)
    return hi, (v - hi.astype(F32)).astype(BF16)


def _adaln_block(c_ref, w_ref, b_ref, o_ref):
    s_hi, s_lo = _split_bf16(_silu(c_ref[...]))
    w_hi, w_lo = _split_bf16(w_ref[...])
    o_ref[...] = (jnp.dot(s_hi, w_hi, preferred_element_type=F32)
                  + jnp.dot(s_lo, w_hi, preferred_element_type=F32)
                  + jnp.dot(s_hi, w_lo, preferred_element_type=F32) + b_ref[...])


def _rope_block(pos_ref, invf_ref, cos_ref, sin_ref):
    half = HEAD_DIM // 2
    per_row = LANES // half
    r = pos_ref.shape[0]
    ang = pos_ref[...].astype(F32) * invf_ref[...]
    lane = lax.broadcasted_iota(jnp.int32, (r, LANES), 1)
    quarter = lane // half
    sign = jnp.where(quarter % 2 == 0, -1.0, 1.0)
    for table, out_ref, scale in ((jnp.cos(ang), cos_ref, None), (jnp.sin(ang), sin_ref, sign)):
        rolled = [table] + [pltpu.roll(table, half * k, 1) for k in range(1, per_row)]
        for q in range(per_row):
            val = rolled[(0 - q) % per_row]
            for k in range(1, per_row):
                val = jnp.where(quarter == k, rolled[(k - q) % per_row], val)
            if scale is not None:
                val = val * scale
            out_ref[pl.ds(q, r, stride=per_row), :] = val


PROLOGUE_STEPS = 8


def _prologue_kernel(c_ref, w_ref, b_ref, pos_ref, invf_ref, mod_ref, cos_ref, sin_ref):
    _adaln_block(c_ref, w_ref, b_ref, mod_ref)
    _rope_block(pos_ref, invf_ref, cos_ref, sin_ref)


def _prologue(c, w, b, positions):
    bsz, d = c.shape
    n_mod = w.shape[1]
    n = positions.size
    half = HEAD_DIM // 2
    inv_freq = ROPE_BASE ** (-jnp.arange(0, HEAD_DIM, 2, dtype=F32) / HEAD_DIM)
    per_row = LANES // half
    rows = n // per_row
    pos_rep = jnp.broadcast_to(positions.reshape(n, 1), (n, half)).reshape(rows, LANES)
    invf = jnp.tile(inv_freq, per_row).reshape(1, LANES)
    steps = PROLOGUE_STEPS
    tn, tr = n_mod // steps, rows // steps
    assert tn % LANES == 0 and tr % SUBLANES == 0
    return pl.pallas_call(
        _prologue_kernel,
        out_shape=(jax.ShapeDtypeStruct((bsz, n_mod), F32),
                   jax.ShapeDtypeStruct((n, LANES), F32),
                   jax.ShapeDtypeStruct((n, LANES), F32)),
        grid=(steps,),
        in_specs=[pl.BlockSpec((bsz, d), lambda j: (0, 0)),
                  pl.BlockSpec((d, tn), lambda j: (0, j)),
                  pl.BlockSpec((1, tn), lambda j: (0, j)),
                  pl.BlockSpec((tr, LANES), lambda j: (j, 0)),
                  pl.BlockSpec((1, LANES), lambda j: (0, 0))],
        out_specs=(pl.BlockSpec((bsz, tn), lambda j: (0, j)),
                   pl.BlockSpec((tr * per_row, LANES), lambda j: (j, 0)),
                   pl.BlockSpec((tr * per_row, LANES), lambda j: (j, 0))),
        name="prologue",
    )(c, w, b.reshape(1, n_mod), pos_rep, invf)


def _load_token_rows(ref, rows):
    return jnp.concatenate(
        [ref[pl.ds(c, rows, stride=SUBLANES), :] for c in range(SUBLANES)], axis=1)


def _store_token_rows(ref, val, row0=0):
    rows = val.shape[0]
    for c in range(SUBLANES):
        ref[pl.ds(row0 * SUBLANES + c, rows, stride=SUBLANES), :] = val[:, c * LANES:(c + 1) * LANES]


def _token_rows(ref, row):
    return ref.at[pl.ds(pl.multiple_of(row * SUBLANES, SUBLANES), SUBLANES), :]


MIXER_SUBTILES = 2


MIXER_WIN, MIXER_WOUT = 6, 11


def _mixer_kernel(*refs):
    *refs, winb_ref, woutb_ref, dec_ref, qdec_ref, kdec_ref = refs
    state_ref, ubuf_ref = refs[-2:]
    L = SEQ_TILE

    @pl.when((pl.program_id(0) == 0) & (pl.program_id(1) == 0))
    def _():
        for src, dst in ((refs[MIXER_WIN], winb_ref), (refs[MIXER_WOUT], woutb_ref)):
            step = RET_HEADS * HEAD_DIM
            for c0 in range(0, src.shape[1], step):
                dst[:, c0:c0 + step] = src[:, c0:c0 + step].astype(BF16)
        lg_ref, lgl_ref = refs[0], refs[15]
        ii = lax.broadcasted_iota(jnp.int32, (L, L), 0)
        jj = lax.broadcasted_iota(jnp.int32, (L, L), 1)
        dist = jnp.abs(ii - jj).astype(F32)
        allowed = (jj // CHUNK) <= (ii // CHUNK)
        for head in range(RET_HEADS):
            dec_ref[head] = jnp.where(allowed, jnp.exp(lg_ref[head] * dist), 0.0)
        rowf = lax.broadcasted_iota(jnp.int32, qdec_ref.shape, 0).astype(F32)
        qdec_ref[...] = jnp.exp(lgl_ref[...] * (rowf + 1.0))
        kdec_ref[...] = jnp.exp(lgl_ref[...] * (float(L - 1) - rowf))

    @pl.when(pl.program_id(1) == 0)
    def _():
        state_ref[...] = jnp.zeros_like(state_ref)
        ubuf_ref[0:SUBLANES, :] = jnp.zeros((SUBLANES, ubuf_ref.shape[1]), F32)

    refs[MIXER_WIN], refs[MIXER_WOUT] = winb_ref, woutb_ref
    for sub in range(MIXER_SUBTILES):
        _mixer_tile(sub, *refs, dec_ref, qdec_ref, kdec_ref)


def _mixer_tile(sub, lg_ref, x_ref, mod_ref, cos_ref, sin_ref, gmix_ref, win_ref, convw_ref,
                convb_ref, bret_ref, bconv_ref, wout_ref, gffn_ref, wr_ref, br_ref, lgl_ref,
                blk_ref, x1_ref, h2_ref, logit_ref, state_ref, ubuf_ref,
                dec_ref, qdec_ref, kdec_ref):
    L = SEQ_TILE
    W = RET_HEADS * HEAD_DIM
    tile_rows = slice(sub * L, (sub + 1) * L)

    x = x_ref[0, tile_rows, :]
    mod = mod_ref[0]
    shift_m, scale_m, gate_m = mod[0:1], mod[1:2], mod[2:3]
    shift_f, scale_f = mod[3:4], mod[4:5]

    ms = jnp.mean(x * x, axis=-1, keepdims=True)
    h = x * lax.rsqrt(ms + EPS) * gmix_ref[...]
    h = h * (1.0 + scale_m) + shift_m
    hb = h.astype(BF16)

    def proj(i):
        return jnp.dot(hb, win_ref[:, i * W:(i + 1) * W], preferred_element_type=F32)

    cos = jnp.concatenate([cos_ref[tile_rows, :]] * 4, axis=1)
    sin = jnp.concatenate([sin_ref[tile_rows, :]] * 4, axis=1)
    lane_w = lax.broadcasted_iota(jnp.int32, (L, W), 1)
    first_half = (lane_w & (HEAD_DIM - 1)) < (HEAD_DIM // 2)

    def rot(t):
        partner = jnp.where(first_half, pltpu.roll(t, W - HEAD_DIM // 2, 1),
                            pltpu.roll(t, HEAD_DIM // 2, 1))
        return t * cos + partner * sin

    q = rot(proj(0))
    k = rot(proj(1)) * (HEAD_DIM ** -0.5)
    v = proj(2)
    vb = v.astype(BF16)
    kb = k.astype(BF16)

    lgl = lgl_ref[...]
    qd = q * qdec_ref[...]
    kd = k * kdec_ref[...]
    blk = blk_ref[...]
    HW = W // 2
    blk_f = blk.astype(F32)
    qdb = qd.astype(BF16)
    kdb = kd.astype(BF16)
    state_decay = jnp.exp(lgl * float(L))
    inter = []
    for hf in range(2):
        sl = slice(hf * HW, (hf + 1) * HW)
        st = state_ref[hf]
        inter.append(jnp.dot(qdb[:, sl], st.astype(BF16), preferred_element_type=F32))
        kv = lax.dot_general(kdb[:, sl], vb[:, sl], (((0,), (0,)), ((), ())),
                             preferred_element_type=F32)
        state_ref[hf] = st * state_decay[:, sl] + kv * blk_f
    y_inter = jnp.concatenate(inter, axis=1)

    def head_sums(t):
        tb = t.astype(BF16)
        return jnp.concatenate(
            [jnp.dot(tb[:, hf * HW:(hf + 1) * HW], blk, preferred_element_type=F32)
             for hf in range(2)], axis=1)

    lane_p = lax.broadcasted_iota(jnp.int32, (L, LANES), 1)
    lo_head = lane_p < HEAD_DIM
    pairs = []
    for p in range(RET_HEADS // 2):
        sl = slice(p * LANES, (p + 1) * LANES)
        qp, kp, vp = q[:, sl], kb[:, sl], vb[:, sl]
        ys = []
        for hh in range(2):
            head = 2 * p + hh
            keep = lo_head if hh == 0 else jnp.logical_not(lo_head)
            qh = jnp.where(keep, qp, 0.0).astype(BF16)
            sc = lax.dot_general(qh, kp, (((1,), (1,)), ((), ())),
                                 preferred_element_type=F32)
            ys.append(jnp.dot((sc * dec_ref[head]).astype(BF16), vp, preferred_element_type=F32))
        pairs.append(jnp.where(lo_head, ys[0], ys[1]))
    y = jnp.concatenate(pairs, axis=1) + y_inter

    inv_hd = 1.0 / HEAD_DIM
    mu = head_sums(y) * inv_hd
    d = y - mu
    var = head_sums(d * d) * inv_hd
    g = proj(3)
    y_ret = _silu(g) * (d * lax.rsqrt(var + GN_EPS)) * bret_ref[...]

    b_gate = proj(4)
    u = proj(5) * proj(6)
    ubuf_ref[SUBLANES:SUBLANES + L, :] = u
    u1 = ubuf_ref[SUBLANES - 1:SUBLANES - 1 + L, :]
    u2 = ubuf_ref[SUBLANES - 2:SUBLANES - 2 + L, :]
    ubuf_ref[0:SUBLANES, :] = ubuf_ref[L:L + SUBLANES, :]
    cw = convw_ref[...]
    conv = u2 * cw[0:1] + u1 * cw[1:2] + u * cw[2:3] + convb_ref[...]
    yc = b_gate * conv
    msc = head_sums(yc * yc) * (1.0 / CONV_GROUP_DIM)
    y_conv = yc * lax.rsqrt(msc + EPS) * bconv_ref[...]

    mix = (jnp.dot(y_ret.astype(BF16), wout_ref[0:W, :], preferred_element_type=F32)
           + jnp.dot(y_conv.astype(BF16), wout_ref[W:2 * W, :], preferred_element_type=F32))
    x1 = x + gate_m * mix
    x1_ref[0, tile_rows, :] = x1

    ms2 = jnp.mean(x1 * x1, axis=-1, keepdims=True)
    h2 = x1 * lax.rsqrt(ms2 + EPS) * gffn_ref[...]
    h2 = h2 * (1.0 + scale_f) + shift_f
    _store_token_rows(h2_ref, h2, row0=sub * L)

    hi, lo = _split_bf16(h2)
    w2 = wr_ref[...]
    nt_dims = (((1,), (1,)), ((), ()))
    parts = (lax.dot_general(w2, hi, nt_dims, preferred_element_type=F32)
             + lax.dot_general(w2, lo, nt_dims, preferred_element_type=F32))
    logits_t = parts[:ROUTER_LANES] + parts[ROUTER_LANES:] + br_ref[...]
    logit_ref[:, tile_rows] = logits_t[:ROUTER_ROWS]


def _mixer(x, mod, cos128, sin128, gmix, win, convw, convb, bret, bconv, wout, gffn,
           wr2, br, lg, lgl, blk):
    bsz, seq, d = x.shape
    L = SEQ_TILE * MIXER_SUBTILES
    ns = seq // L
    W = RET_HEADS * HEAD_DIM
    n = bsz * seq
    const2 = lambda b, s: (0, 0)
    in_specs = [
        pl.BlockSpec(memory_space=pltpu.SMEM),
        pl.BlockSpec((1, L, d), lambda b, s: (b, s, 0)),
        pl.BlockSpec((1, 6, d), lambda b, s: (b, 0, 0)),
        pl.BlockSpec((L, LANES), lambda b, s: (b * ns + s, 0)),
        pl.BlockSpec((L, LANES), lambda b, s: (b * ns + s, 0)),
        pl.BlockSpec((1, d), const2),
        pl.BlockSpec(win.shape, const2, pipeline_mode=pl.Buffered(1)),
        pl.BlockSpec(convw.shape, const2),
        pl.BlockSpec((1, W), const2),
        pl.BlockSpec((1, W), const2),
        pl.BlockSpec((1, W), const2),
        pl.BlockSpec(wout.shape, const2, pipeline_mode=pl.Buffered(1)),
        pl.BlockSpec((1, d), const2),
        pl.BlockSpec(wr2.shape, const2),
        pl.BlockSpec((ROUTER_LANES, 1), const2),
        pl.BlockSpec((1, W), const2),
        pl.BlockSpec((W // 2, W // 2), const2),
    ]
    assert d == SUBLANES * LANES, "one token must fill exactly one (8, 128) f32 tile"
    out_shape = (jax.ShapeDtypeStruct((bsz, seq, d), F32),
                 jax.ShapeDtypeStruct((n * SUBLANES, LANES), F32),
                 jax.ShapeDtypeStruct((ROUTER_ROWS, n), F32))
    out_specs = (pl.BlockSpec((1, L, d), lambda b, s: (b, s, 0)),
                 pl.BlockSpec((L * SUBLANES, LANES), lambda b, s: (b * ns + s, 0)),
                 pl.BlockSpec((ROUTER_ROWS, L), lambda b, s: (0, b * ns + s)))
    return pl.pallas_call(
        _mixer_kernel,
        out_shape=out_shape,
        grid=(bsz, ns),
        in_specs=in_specs,
        out_specs=out_specs,
        scratch_shapes=[pltpu.VMEM((2, W // 2, W // 2), F32),
                        pltpu.VMEM((SEQ_TILE + 2 * SUBLANES, W), F32),
                        pltpu.VMEM(win.shape, BF16),
                        pltpu.VMEM(wout.shape, BF16),
                        pltpu.VMEM((RET_HEADS, SEQ_TILE, SEQ_TILE), F32),
                        pltpu.VMEM((SEQ_TILE, W), F32),
                        pltpu.VMEM((SEQ_TILE, W), F32)],
        compiler_params=pltpu.CompilerParams(
            dimension_semantics=("arbitrary", "arbitrary"),
            vmem_limit_bytes=VMEM_LIMIT_BYTES),
        name="mixer",
    )(lg, x, mod, cos128, sin128, gmix, win, convw, convb, bret, bconv, wout, gffn,
      wr2, br, lgl, blk)


ROUTER_TILE = 4096
RES_E1, RES_E2, RES_W1, RES_W2, RES_A, RES_B = range(6)
PLAN_FIRST_TILE, PLAN_N_TILE, PLAN_PAD_START, PLAN_PAD_N, PLAN_N_USED = range(5)


def _rows8(vals, width):
    rid = lax.broadcasted_iota(jnp.int32, (SUBLANES, width), 0)
    out = jnp.zeros((SUBLANES, width), F32)
    for r, v in enumerate(vals):
        out = jnp.where(rid == r, v, out)
    return out


def _router_kernel(lt_ref, slab_ref, dest1_ref, dest2_ref, plan_ref, res_ref, cnt_ref):
    T = ROUTER_TILE
    phase = pl.program_id(0)
    j = pl.program_id(1)
    nblk = T // LANES
    big = F32(1e9)
    rid8 =lax.broadcasted_iota(jnp.int32, (SUBLANES, T), 0).astype(F32)
    rid_e = lax.broadcasted_iota(jnp.int32, (N_EXPERTS, LANES), 0).astype(F32)

    def onehot(e_row, k):
        return jnp.where(rid_e == e_row[:, k * LANES:(k + 1) * LANES], 1.0, 0.0)

    @pl.when((phase == 0) & (j == 0))
    def _():
        cnt_ref[...] = jnp.zeros_like(cnt_ref)

    @pl.when(phase == 0)
    def _():
        lt = lt_ref[...]
        g_rows = lt[ROUTER_GROUP_ROW0:ROUTER_GROUP_ROW0 + SUBLANES]
        gvalid = rid8 < float(N_GROUPS)
        gm = jnp.where(gvalid, g_rows, F32(-jnp.inf))
        gexp = jnp.exp(gm - jnp.max(gm, axis=0, keepdims=True))
        gp = gexp / jnp.sum(gexp, axis=0, keepdims=True)
        g_top = jnp.max(gp, axis=0, keepdims=True)
        g_idx = jnp.min(jnp.where(gvalid & (gp == g_top), rid8, big), axis=0, keepdims=True)

        def group_slab(g):
            r0 = ROUTER_EXPERT_ROW0 + g * EXPERTS_PER_GROUP
            return lt[r0:r0 + EXPERTS_PER_GROUP]
        sel = group_slab(N_GROUPS - 1)
        for g in range(N_GROUPS - 2, -1, -1):
            sel = jnp.where(g_idx == float(g), group_slab(g), sel)
        eexp = jnp.exp(sel - jnp.max(sel, axis=0, keepdims=True))
        ep = eexp / jnp.sum(eexp, axis=0, keepdims=True)
        p1 = jnp.max(ep, axis=0, keepdims=True)
        i1 = jnp.min(jnp.where(ep == p1, rid8, big), axis=0, keepdims=True)
        m2 = rid8 != i1
        p2 = jnp.max(jnp.where(m2, ep, -1.0), axis=0, keepdims=True)
        i2 = jnp.min(jnp.where(m2 & (ep == p2), rid8, big), axis=0, keepdims=True)
        den = p1 + p2
        w1 = p1 / den * g_top
        w2 = p2 / den * g_top
        e1 = g_idx * float(EXPERTS_PER_GROUP) + i1
        e2 = g_idx * float(EXPERTS_PER_GROUP) + i2

        ii = lax.broadcasted_iota(jnp.int32, (LANES, LANES), 0)
        jj = lax.broadcasted_iota(jnp.int32, (LANES, LANES), 1)
        upper = jnp.where(ii < jj, 1.0, 0.0).astype(BF16)
        base = cnt_ref[...]
        ranks1, ranks2 = [], []
        for k in range(nblk):
            o1, o2 = onehot(e1, k), onehot(e2, k)
            r1 = jnp.dot(o1.astype(BF16), upper, preferred_element_type=F32)
            r2 = jnp.dot(o2.astype(BF16), upper, preferred_element_type=F32)
            c1 = jnp.sum(o1, axis=1, keepdims=True)
            c2 = jnp.sum(o2, axis=1, keepdims=True)
            ranks1.append(jnp.sum(o1 * (base + r1), axis=0, keepdims=True))
            ranks2.append(jnp.sum(o2 * (base + c1 + r2), axis=0, keepdims=True))
            base = base + c1 + c2
        cnt_ref[...] = base
        rank1 = jnp.concatenate(ranks1, axis=1)
        rank2 = jnp.concatenate(ranks2, axis=1)
        res_ref[j] = _rows8([e1, e2, w1, w2, rank1, rank2], T)

    @pl.when(phase == 1)
    def _():
        tm = float(EXPERT_TILE)
        cnt = cnt_ref[...]
        tiles = jnp.floor((cnt + (tm - 1.0)) * (1.0 / tm))
        ei = lax.broadcasted_iota(jnp.int32, (N_EXPERTS, N_EXPERTS), 0)
        ej = lax.broadcasted_iota(jnp.int32, (N_EXPERTS, N_EXPERTS), 1)
        lower = jnp.where(ej < ei, 1.0, 0.0).astype(BF16)
        first = jnp.dot(lower, tiles.astype(BF16), preferred_element_type=F32)
        starts = first * tm

        res = res_ref[j]
        e1, e2 = res[RES_E1:RES_E1 + 1], res[RES_E2:RES_E2 + 1]
        d1, d2 = [], []
        for k in range(nblk):
            blk_lanes = slice(k * LANES, (k + 1) * LANES)
            d1.append(res[RES_A:RES_A + 1, blk_lanes]
                      + jnp.sum(onehot(e1, k) * starts, axis=0, keepdims=True))
            d2.append(res[RES_B:RES_B + 1, blk_lanes]
                      + jnp.sum(onehot(e2, k) * starts, axis=0, keepdims=True))
        dest1 = jnp.concatenate(d1, axis=1)
        dest2 = jnp.concatenate(d2, axis=1)
        dest1_ref[...] = dest1.astype(jnp.int32)
        dest2_ref[...] = dest2.astype(jnp.int32)

        table = jnp.concatenate(
            [_rows8([e1, e2, res[RES_W1:RES_W1 + 1], res[RES_W2:RES_W2 + 1], dest1, dest2], T),
             jnp.zeros((ROUTER_LANES - SUBLANES, T), F32)], axis=0)
        slab_ref[...] = table.T

        @pl.when(j == 0)
        def _():
            lane_e = lax.broadcasted_iota(jnp.int32, (N_EXPERTS, LANES), 1).astype(F32)

            def as_row(col):
                return jnp.sum(jnp.where(rid_e == lane_e, col, 0.0), axis=0, keepdims=True)
            n_used = jnp.sum(tiles, axis=0, keepdims=True)
            padded = tiles * tm
            plan_ref[...] = _rows8([as_row(first), as_row(tiles), as_row(starts + cnt),
                                    as_row(padded - cnt), n_used], LANES).astype(jnp.int32)


def _router(logits_t):
    n = logits_t.shape[1]
    T = ROUTER_TILE
    nt = n // T
    return pl.pallas_call(
        _router_kernel,
        out_shape=(jax.ShapeDtypeStruct((n, ROUTER_LANES), F32),
                   jax.ShapeDtypeStruct((1, n), jnp.int32),
                   jax.ShapeDtypeStruct((1, n), jnp.int32),
                   jax.ShapeDtypeStruct((SUBLANES, LANES), jnp.int32)),
        grid=(2, nt),
        in_specs=[pl.BlockSpec((ROUTER_ROWS, T), lambda p, j: (0, j * (1 - p) + (nt - 1) * p))],
        out_specs=(pl.BlockSpec((T, ROUTER_LANES), lambda p, j: (j * p, 0)),
                   pl.BlockSpec((1, T), lambda p, j: (0, j * p)),
                   pl.BlockSpec((1, T), lambda p, j: (0, j * p)),
                   pl.BlockSpec((SUBLANES, LANES), lambda p, j: (0, 0))),
        scratch_shapes=[pltpu.VMEM((nt, SUBLANES, T), F32),
                        pltpu.VMEM((N_EXPERTS, LANES), F32)],
        compiler_params=pltpu.CompilerParams(
            dimension_semantics=("arbitrary", "arbitrary"), vmem_limit_bytes=VMEM_LIMIT_BYTES),
        name="router",
    )(logits_t)


DISPATCH_TILE = 512
ROW_DMA_UNROLL = 8
PAD_UNITS = tuple(1 << b for b in reversed(range(EXPERT_TILE.bit_length() - 1)))


DISPATCH_LAG = 2
DISPATCH_SLOTS = DISPATCH_LAG + 2


def _dispatch_kernel(d0_ref, d1_ref, plan_ref, h2_hbm, xs_hbm,
                     stage, zbuf, isem, ssem, zsem):
    dt = DISPATCH_TILE
    ps = SUBLANES
    i = pl.program_id(0)
    nsteps = pl.num_programs(0)
    zrows = PAD_UNITS[0]

    def in_copy(blk, slot):
        src = h2_hbm.at[pl.ds(pl.multiple_of(blk * (dt * ps), dt * ps), dt * ps), :]
        return pltpu.make_async_copy(src, stage.at[slot], isem.at[slot])

    def wait_rows(slot):
        for _ in range(2):
            pltpu.make_async_copy(stage.at[slot], xs_hbm.at[pl.ds(0, dt * ps), :],
                                  ssem.at[slot]).wait()

    def pad_copy(start, unit):
        return pltpu.make_async_copy(zbuf.at[pl.ds(0, unit * ps), :],
                                     xs_hbm.at[pl.ds(pl.multiple_of(start * ps, ps), unit * ps), :],
                                     zsem)

    def pad_pass(do):
        def per_expert(e, carry):
            start = plan_ref[PLAN_PAD_START, e]
            npad = plan_ref[PLAN_PAD_N, e]
            for unit in PAD_UNITS:
                @pl.when((npad & unit) != 0)
                def _():
                    do(pad_copy(start + (npad & ~(2 * unit - 1)), unit))
            return carry
        lax.fori_loop(0, N_EXPERTS, per_expert, 0)

    def tail_pass(do):
        def per_unit(k, carry):
            do(pad_copy(k * zrows, zrows))
            return carry
        per_tile = EXPERT_TILE // zrows
        n_units = xs_hbm.shape[0] // (zrows * ps)
        lax.fori_loop(plan_ref[PLAN_N_USED, 0] * per_tile, n_units, per_unit, 0)

    slot = i % DISPATCH_SLOTS

    @pl.when(i == 0)
    def _():
        in_copy(0, 0).start()

        @pl.when(nsteps > 1)
        def _():
            in_copy(1, 1).start()

        zbuf[...] = jnp.zeros_like(zbuf)
        pad_pass(lambda cp: cp.start())
        tail_pass(lambda cp: cp.start())

    @pl.when(i >= DISPATCH_LAG)
    def _():
        wait_rows((i - DISPATCH_LAG) % DISPATCH_SLOTS)

    @pl.when(i + 2 < nsteps)
    def _():
        in_copy(i + 2, (i + 2) % DISPATCH_SLOTS).start()

    in_copy(i, slot).wait()
    base = i * dt
    src_ref = stage.at[slot]

    def body(r, carry):
        src = _token_rows(src_ref, r)
        for prio, d_ref in enumerate((d0_ref, d1_ref)):
            pltpu.make_async_copy(src, _token_rows(xs_hbm, d_ref[base + r]),
                                  ssem.at[slot]).start(priority=prio)
        return carry

    lax.fori_loop(0, dt, body, 0, unroll=ROW_DMA_UNROLL)

    @pl.when(i == nsteps - 1)
    def _():
        for back in range(DISPATCH_LAG - 1, -1, -1):
            @pl.when(i - back >= 0)
            def _():
                wait_rows((i - back) % DISPATCH_SLOTS)
        pad_pass(lambda cp: cp.wait())
        tail_pass(lambda cp: cp.wait())


def _dispatch(dest0, dest1, plan, h2t, p_rows):
    n = dest0.shape[0]
    dt = DISPATCH_TILE
    grid_spec = pltpu.PrefetchScalarGridSpec(
        num_scalar_prefetch=3,
        grid=(n // dt,),
        in_specs=[pl.BlockSpec(memory_space=pl.ANY)],
        out_specs=pl.BlockSpec(memory_space=pl.ANY),
        scratch_shapes=[pltpu.VMEM((DISPATCH_SLOTS, dt * SUBLANES, LANES), F32),
                        pltpu.VMEM((PAD_UNITS[0] * SUBLANES, LANES), F32),
                        pltpu.SemaphoreType.DMA((DISPATCH_SLOTS,)),
                        pltpu.SemaphoreType.DMA((DISPATCH_SLOTS,)),
                        pltpu.SemaphoreType.DMA(())],
    )
    return pl.pallas_call(
        _dispatch_kernel,
        out_shape=jax.ShapeDtypeStruct((p_rows * SUBLANES, LANES), F32),
        grid_spec=grid_spec,
        compiler_params=pltpu.CompilerParams(dimension_semantics=("arbitrary",)),
        name="dispatch",
    )(dest0, dest1, plan, h2t)


EXPERT_IN_SLOTS = 4
EXPERT_OUT_SLOTS = 3
MXU_COLS = 256


def _expert_kernel(plan_ref, xs_hbm, wg_ref, wu_ref, wd_ref, y_hbm,
                   xbuf, ybuf, wgb, wub, wdb, isem, osem):
    tm = EXPERT_TILE
    rows = tm * SUBLANES
    ni, no = EXPERT_IN_SLOTS, EXPERT_OUT_SLOTS
    e = pl.program_id(0)
    n_used = plan_ref[PLAN_N_USED, 0]
    n_mine = plan_ref[PLAN_N_TILE, e]

    def tile_rows(ref, g):
        return ref.at[pl.ds(pl.multiple_of(g * rows, rows), rows), :]

    def in_copy(g):
        return pltpu.make_async_copy(tile_rows(xs_hbm, g), xbuf.at[g % ni], isem.at[g % ni])

    def out_copy(g):
        return pltpu.make_async_copy(ybuf.at[g % no], tile_rows(y_hbm, g), osem.at[g % no])

    @pl.when(e == 0)
    def _():
        for g0 in range(ni - 1):
            @pl.when(g0 < n_used)
            def _():
                in_copy(g0).start()

    @pl.when(n_mine > 0)
    def _():
        wgb[...] = wg_ref[0].astype(BF16)
        wub[...] = wu_ref[0].astype(BF16)
        wdb[...] = wd_ref[0].astype(BF16)

    def tile(g, carry):
        in_copy(g).wait()

        @pl.when(g >= no)
        def _():
            out_copy(g - no).wait()

        @pl.when(g + ni - 1 < n_used)
        def _():
            in_copy(g + ni - 1).start()

        xb = _load_token_rows(xbuf.at[g % ni], tm).astype(BF16)
        a = jnp.dot(xb, wgb[...], preferred_element_type=F32)
        u = jnp.dot(xb, wub[...], preferred_element_type=F32)
        hid = (_silu(a) * u).astype(BF16)
        out = ybuf.at[g % no]
        per_piece = MXU_COLS // LANES
        for p in range(wdb.shape[1] // MXU_COLS):
            y = jnp.dot(hid, wdb[:, p * MXU_COLS:(p + 1) * MXU_COLS], preferred_element_type=F32)
            for q in range(per_piece):
                out[pl.ds(p * per_piece + q, tm, stride=SUBLANES), :] = y[:, q * LANES:(q + 1) * LANES]
        out_copy(g).start()
        return carry

    first = plan_ref[PLAN_FIRST_TILE, e]
    lax.fori_loop(first, first + n_mine, tile, 0)

    @pl.when(e == pl.num_programs(0) - 1)
    def _():
        for back in range(no, 0, -1):
            @pl.when(n_used >= back)
            def _():
                out_copy(n_used - back).wait()


def _experts(plan, xs, wg, wu, wd):
    tm = EXPERT_TILE
    n_exp, d, de = wg.shape
    grid_spec = pltpu.PrefetchScalarGridSpec(
        num_scalar_prefetch=1,
        grid=(n_exp,),
        in_specs=[
            pl.BlockSpec(memory_space=pl.ANY),
            pl.BlockSpec((1, d, de), lambda e, *_: (e, 0, 0)),
            pl.BlockSpec((1, d, de), lambda e, *_: (e, 0, 0)),
            pl.BlockSpec((1, de, d), lambda e, *_: (e, 0, 0)),
        ],
        out_specs=pl.BlockSpec(memory_space=pl.ANY),
        scratch_shapes=[pltpu.VMEM((EXPERT_IN_SLOTS, tm * SUBLANES, LANES), F32),
                        pltpu.VMEM((EXPERT_OUT_SLOTS, tm * SUBLANES, LANES), F32),
                        pltpu.VMEM((d, de), BF16),
                        pltpu.VMEM((d, de), BF16),
                        pltpu.VMEM((de, d), BF16),
                        pltpu.SemaphoreType.DMA((EXPERT_IN_SLOTS,)),
                        pltpu.SemaphoreType.DMA((EXPERT_OUT_SLOTS,))],
    )
    return pl.pallas_call(
        _expert_kernel,
        out_shape=jax.ShapeDtypeStruct(xs.shape, F32),
        input_output_aliases={1: 0},
        grid_spec=grid_spec,
        compiler_params=pltpu.CompilerParams(
            dimension_semantics=("arbitrary",),
            vmem_limit_bytes=VMEM_LIMIT_BYTES),
        name="experts",
    )(plan, xs, wg, wu, wd)


COMBINE_SLOTS = 3


def _combine_kernel(p0_ref, p1_ref, y_hbm, x1_ref, slab_ref, mod_ref, gfin_ref, o_ref,
                    *scratch):
    tm = COMBINE_TILE
    ns = COMBINE_SLOTS
    ahead = ns - 1
    ybufs, sem = scratch[:ns], scratch[ns]
    i = pl.program_id(0)
    nt = pl.num_programs(0)

    def row_copies(tile, slot, r):
        for j, p_ref in enumerate((p0_ref, p1_ref)):
            pltpu.make_async_copy(_token_rows(y_hbm, p_ref[tile * tm + r]),
                                  _token_rows(ybufs[slot].at[j], r),
                                  sem.at[slot]).start(priority=j)

    def wait_tile(slot):
        for j in range(2):
            pltpu.make_async_copy(y_hbm.at[pl.ds(0, tm * SUBLANES), :], ybufs[slot].at[j],
                                  sem.at[slot]).wait()

    @pl.when(i == 0)
    def _():
        for t0 in range(ahead):
            @pl.when(t0 < nt)
            def _():
                def body(r, carry):
                    row_copies(t0, t0, r)
                    return carry
                lax.fori_loop(0, tm, body, 0, unroll=ROW_DMA_UNROLL)

    n_chunks = SUBLANES
    batch = tm // n_chunks

    def step(slot):
        wait_tile(slot)
        nxt = jnp.minimum(i + ahead, nt - 1)
        nslot = (slot + ahead) % ns
        slab = slab_ref[...]
        w0 = slab[:, 2:3]
        w1 = slab[:, 3:4]
        gate_f = mod_ref[0][5:6]
        sq = jnp.zeros((tm, LANES), F32)
        for c in range(n_chunks):
            lanes = slice(c * LANES, (c + 1) * LANES)
            y0 = ybufs[slot].at[0][pl.ds(c, tm, stride=SUBLANES), :]
            y1 = ybufs[slot].at[1][pl.ds(c, tm, stride=SUBLANES), :]
            xo = x1_ref[:, lanes] + gate_f[:, lanes] * (w0 * y0 + w1 * y1)
            sq = sq + xo * xo
            o_ref[:, lanes] = xo
            for r in range(c * batch, (c + 1) * batch):
                row_copies(nxt, nslot, r)
        ms = jnp.sum(sq, axis=-1, keepdims=True) * (1.0 / (n_chunks * LANES))
        scale = lax.rsqrt(ms + EPS)
        for c in range(n_chunks):
            lanes = slice(c * LANES, (c + 1) * LANES)
            o_ref[:, lanes] = o_ref[:, lanes] * scale * gfin_ref[:, lanes]

    for slot in range(ns):
        @pl.when(i % ns == slot)
        def _():
            step(slot)

    @pl.when(i == nt - 1)
    def _():
        for k in range(ahead):
            for slot in range(ns):
                @pl.when((nt - 1 - k >= 0) & ((nt - 1 - k + ahead) % ns == slot))
                def _():
                    wait_tile(slot)


def _combine(p0, p1, y, x1, slab, mod, gfin, seq):
    n, d = x1.shape
    tm = COMBINE_TILE
    tiles_per_seq = seq // tm
    grid_spec = pltpu.PrefetchScalarGridSpec(
        num_scalar_prefetch=2,
        grid=(n // tm,),
        in_specs=[
            pl.BlockSpec(memory_space=pl.ANY),
            pl.BlockSpec((tm, d), lambda i, a, b: (i, 0)),
            pl.BlockSpec((tm, ROUTER_LANES), lambda i, a, b: (i, 0)),
            pl.BlockSpec((1, 6, d), lambda i, a, b: (i // tiles_per_seq, 0, 0)),
            pl.BlockSpec((1, d), lambda i, a, b: (0, 0)),
        ],
        out_specs=pl.BlockSpec((tm, d), lambda i, a, b: (i, 0)),
        scratch_shapes=([pltpu.VMEM((2, tm * SUBLANES, LANES), F32)] * COMBINE_SLOTS
                        + [pltpu.SemaphoreType.DMA((COMBINE_SLOTS,))]),
    )
    return pl.pallas_call(
        _combine_kernel,
        out_shape=jax.ShapeDtypeStruct((n, d), F32),
        grid_spec=grid_spec,
        compiler_params=pltpu.CompilerParams(
            dimension_semantics=("arbitrary",),
            vmem_limit_bytes=VMEM_LIMIT_BYTES),
        name="combine",
    )(p0, p1, y, x1, slab, mod, gfin)


def kernel(x, c, positions, ada_w, ada_b, norm_mix_g, norm_ffn_g, w_in, conv_w, conv_b,
           beta_ret, beta_conv, w_out, router_group_w, router_group_b, router_expert_w,
           router_expert_b, expert_w_gate, expert_w_up, expert_w_down, norm_final_g):
    bsz, seq, d = x.shape
    n = bsz * seq
    depth = ada_w.shape[0]
    assert depth == 1, "the combine kernel fuses the trunk's final RMSNorm (single layer)"
    W = RET_HEADS * HEAD_DIM

    l = 0
    mod, cos128, sin128 = _prologue(c, ada_w[l], ada_b[l], positions)
    mod = mod.reshape(bsz, 6, d)
    heads = jnp.arange(RET_HEADS, dtype=F32)
    lg = jnp.log1p(-jnp.exp2(-5.0 - heads))
    lgl = jnp.repeat(lg, HEAD_DIM).reshape(1, W)
    assert CONV_GROUP_DIM == HEAD_DIM, "conv groups and retention heads share the 64-lane block sums"
    blk_np = np.kron(np.eye(RET_HEADS // 2, dtype=np.float32),
                     np.ones((HEAD_DIM, HEAD_DIM), np.float32))
    blk = jnp.asarray(blk_np, dtype=BF16)

    gap = ROUTER_EXPERT_ROW0 - N_GROUPS
    tail = ROUTER_LANES - ROUTER_ROWS
    wr = jnp.concatenate([router_group_w[l].T, jnp.zeros((gap, d), F32),
                          router_expert_w[l].T, jnp.zeros((tail, d), F32)], axis=0)
    wr2 = jnp.concatenate(_split_bf16(wr), axis=0)
    br = jnp.concatenate([router_group_b[l], jnp.zeros((gap,), F32),
                          router_expert_b[l], jnp.zeros((tail,), F32)]).reshape(ROUTER_LANES, 1)

    x1, h2t, logits_t = _mixer(
        x, mod, cos128, sin128, norm_mix_g[l].reshape(1, d), w_in[l],
        conv_w[l], conv_b[l].reshape(1, W), beta_ret[l].reshape(1, W),
        beta_conv[l].reshape(1, W), w_out[l], norm_ffn_g[l].reshape(1, d),
        wr2, br, lg, lgl, blk)
    slab, p0, p1, plan = _router(logits_t)
    p0, p1 = p0.reshape(n), p1.reshape(n)
    p_rows = 2 * n + N_EXPERTS * EXPERT_TILE

    xs = _dispatch(p0, p1, plan, h2t, p_rows)
    de = expert_w_gate.shape[-1]
    y = _experts(plan, xs,
                 expert_w_gate[l].reshape(N_EXPERTS, d, de),
                 expert_w_up[l].reshape(N_EXPERTS, d, de),
                 expert_w_down[l].reshape(N_EXPERTS, de, d))
    out = _combine(p0, p1, y, x1.reshape(n, d), slab, mod, norm_final_g.reshape(1, d), seq)
    return out.reshape(bsz, seq, d)
```

```python
import jax
import jax.numpy as jnp
import numpy as np
from jax import lax
from jax.experimental import pallas as pl
from jax.experimental.pallas import tpu as pltpu

F32 = jnp.float32
BF16 = jnp.bfloat16

CHUNK = 64
RET_HEADS = 8
HEAD_DIM = 64
CONV_GROUP_DIM = 64
ROPE_BASE = 10000.0
N_GROUPS = 4
EXPERTS_PER_GROUP = 8
N_EXPERTS = N_GROUPS * EXPERTS_PER_GROUP
EPS = 1e-6
GN_EPS = 1e-5

LANES = 128
SUBLANES = 8
VMEM_LIMIT_BYTES = 56 * 1024 * 1024

SEQ_TILE = 256
EXPERT_TILE = 256
COMBINE_TILE = 256
ROUTER_LANES = LANES
ROUTER_GROUP_ROW0 = 0
ROUTER_EXPERT_ROW0 = SUBLANES
ROUTER_ROWS = ROUTER_EXPERT_ROW0 + N_EXPERTS


def _silu(v):
    return v * (1.0 / (1.0 + jnp.exp(-v)))


def _split_bf16(v):
    hi = v.astype(BF16)
    return hi, (v - hi.astype(F32)).astype(BF16)


def _adaln_block(c_ref, w_ref, b_ref, o_ref):
    s_hi, s_lo = _split_bf16(_silu(c_ref[...]))
    w_hi, w_lo = _split_bf16(w_ref[...])
    o_ref[...] = (jnp.dot(s_hi, w_hi, preferred_element_type=F32)
                  + jnp.dot(s_lo, w_hi, preferred_element_type=F32)
                  + jnp.dot(s_hi, w_lo, preferred_element_type=F32) + b_ref[...])


def _rope_block(pos_ref, invf_ref, cos_ref, sin_ref):
    half = HEAD_DIM // 2
    per_row = LANES // half
    r = pos_ref.shape[0]
    ang = pos_ref[...].astype(F32) * invf_ref[...]
    lane = lax.broadcasted_iota(jnp.int32, (r, LANES), 1)
    quarter = lane // half
    sign = jnp.where(quarter % 2 == 0, -1.0, 1.0)
    for table, out_ref, scale in ((jnp.cos(ang), cos_ref, None), (jnp.sin(ang), sin_ref, sign)):
        rolled = [table] + [pltpu.roll(table, half * k, 1) for k in range(1, per_row)]
        for q in range(per_row):
            val = rolled[(0 - q) % per_row]
            for k in range(1, per_row):
                val = jnp.where(quarter == k, rolled[(k - q) % per_row], val)
            if scale is not None:
                val = val * scale
            out_ref[pl.ds(q, r, stride=per_row), :] = val


PROLOGUE_STEPS = 8


def _prologue_kernel(c_ref, w_ref, b_ref, pos_ref, invf_ref, mod_ref, cos_ref, sin_ref):
    _adaln_block(c_ref, w_ref, b_ref, mod_ref)
    _rope_block(pos_ref, invf_ref, cos_ref, sin_ref)


def _prologue(c, w, b, positions):
    bsz, d = c.shape
    n_mod = w.shape[1]
    n = positions.size
    half = HEAD_DIM // 2
    inv_freq = ROPE_BASE ** (-jnp.arange(0, HEAD_DIM, 2, dtype=F32) / HEAD_DIM)
    per_row = LANES // half
    rows = n // per_row
    pos_rep = jnp.broadcast_to(positions.reshape(n, 1), (n, half)).reshape(rows, LANES)
    invf = jnp.tile(inv_freq, per_row).reshape(1, LANES)
    steps = PROLOGUE_STEPS
    tn, tr = n_mod // steps, rows // steps
    assert tn % LANES == 0 and tr % SUBLANES == 0
    return pl.pallas_call(
        _prologue_kernel,
        out_shape=(jax.ShapeDtypeStruct((bsz, n_mod), F32),
                   jax.ShapeDtypeStruct((n, LANES), F32),
                   jax.ShapeDtypeStruct((n, LANES), F32)),
        grid=(steps,),
        in_specs=[pl.BlockSpec((bsz, d), lambda j: (0, 0)),
                  pl.BlockSpec((d, tn), lambda j: (0, j)),
                  pl.BlockSpec((1, tn), lambda j: (0, j)),
                  pl.BlockSpec((tr, LANES), lambda j: (j, 0)),
                  pl.BlockSpec((1, LANES), lambda j: (0, 0))],
        out_specs=(pl.BlockSpec((bsz, tn), lambda j: (0, j)),
                   pl.BlockSpec((tr * per_row, LANES), lambda j: (j, 0)),
                   pl.BlockSpec((tr * per_row, LANES), lambda j: (j, 0))),
        name="prologue",
    )(c, w, b.reshape(1, n_mod), pos_rep, invf)


def _load_token_rows(ref, rows):
    return jnp.concatenate(
        [ref[pl.ds(c, rows, stride=SUBLANES), :] for c in range(SUBLANES)], axis=1)


def _store_token_rows(ref, val, row0=0):
    rows = val.shape[0]
    for c in range(SUBLANES):
        ref[pl.ds(row0 * SUBLANES + c, rows, stride=SUBLANES), :] = val[:, c * LANES:(c + 1) * LANES]


def _token_rows(ref, row):
    return ref.at[pl.ds(pl.multiple_of(row * SUBLANES, SUBLANES), SUBLANES), :]


MIXER_SUBTILES = 2


MIXER_WIN, MIXER_WOUT = 6, 11


def _mixer_kernel(*refs):
    *refs, winb_ref, woutb_ref, dec_ref, qdec_ref, kdec_ref = refs
    state_ref, ubuf_ref = refs[-2:]
    L = SEQ_TILE

    @pl.when((pl.program_id(0) == 0) & (pl.program_id(1) == 0))
    def _():
        for src, dst in ((refs[MIXER_WIN], winb_ref), (refs[MIXER_WOUT], woutb_ref)):
            step = RET_HEADS * HEAD_DIM
            for c0 in range(0, src.shape[1], step):
                dst[:, c0:c0 + step] = src[:, c0:c0 + step].astype(BF16)
        lg_ref, lgl_ref = refs[0], refs[15]
        ii = lax.broadcasted_iota(jnp.int32, (L, L), 0)
        jj = lax.broadcasted_iota(jnp.int32, (L, L), 1)
        dist = jnp.abs(ii - jj).astype(F32)
        allowed = (jj // CHUNK) <= (ii // CHUNK)
        for head in range(RET_HEADS):
            dec_ref[head] = jnp.where(allowed, jnp.exp(lg_ref[head] * dist), 0.0)
        rowf = lax.broadcasted_iota(jnp.int32, qdec_ref.shape, 0).astype(F32)
        qdec_ref[...] = jnp.exp(lgl_ref[...] * (rowf + 1.0))
        kdec_ref[...] = jnp.exp(lgl_ref[...] * (float(L - 1) - rowf))

    @pl.when(pl.program_id(1) == 0)
    def _():
        state_ref[...] = jnp.zeros_like(state_ref)
        ubuf_ref[0:SUBLANES, :] = jnp.zeros((SUBLANES, ubuf_ref.shape[1]), F32)

    refs[MIXER_WIN], refs[MIXER_WOUT] = winb_ref, woutb_ref
    for sub in range(MIXER_SUBTILES):
        _mixer_tile(sub, *refs, dec_ref, qdec_ref, kdec_ref)


def _mixer_tile(sub, lg_ref, x_ref, mod_ref, cos_ref, sin_ref, gmix_ref, win_ref, convw_ref,
                convb_ref, bret_ref, bconv_ref, wout_ref, gffn_ref, wr_ref, br_ref, lgl_ref,
                blk_ref, x1_ref, h2_ref, logit_ref, state_ref, ubuf_ref,
                dec_ref, qdec_ref, kdec_ref):
    L = SEQ_TILE
    W = RET_HEADS * HEAD_DIM
    tile_rows = slice(sub * L, (sub + 1) * L)

    x = x_ref[0, tile_rows, :]
    mod = mod_ref[0]
    shift_m, scale_m, gate_m = mod[0:1], mod[1:2], mod[2:3]
    shift_f, scale_f = mod[3:4], mod[4:5]

    ms = jnp.mean(x * x, axis=-1, keepdims=True)
    h = x * lax.rsqrt(ms + EPS) * gmix_ref[...]
    h = h * (1.0 + scale_m) + shift_m
    hb = h.astype(BF16)

    def proj(i):
        return jnp.dot(hb, win_ref[:, i * W:(i + 1) * W], preferred_element_type=F32)

    cos = jnp.concatenate([cos_ref[tile_rows, :]] * 4, axis=1)
    sin = jnp.concatenate([sin_ref[tile_rows, :]] * 4, axis=1)
    lane_w = lax.broadcasted_iota(jnp.int32, (L, W), 1)
    first_half = (lane_w & (HEAD_DIM - 1)) < (HEAD_DIM // 2)

    def rot(t):
        partner = jnp.where(first_half, pltpu.roll(t, W - HEAD_DIM // 2, 1),
                            pltpu.roll(t, HEAD_DIM // 2, 1))
        return t * cos + partner * sin

    q = rot(proj(0))
    k = rot(proj(1)) * (HEAD_DIM ** -0.5)
    v = proj(2)
    vb = v.astype(BF16)
    kb = k.astype(BF16)

    lgl = lgl_ref[...]
    qd = q * qdec_ref[...]
    kd = k * kdec_ref[...]
    blk = blk_ref[...]
    HW = W // 2
    blk_f = blk.astype(F32)
    qdb = qd.astype(BF16)
    kdb = kd.astype(BF16)
    state_decay = jnp.exp(lgl * float(L))
    inter = []
    for hf in range(2):
        sl = slice(hf * HW, (hf + 1) * HW)
        st = state_ref[hf]
        inter.append(jnp.dot(qdb[:, sl], st.astype(BF16), preferred_element_type=F32))
        kv = lax.dot_general(kdb[:, sl], vb[:, sl], (((0,), (0,)), ((), ())),
                             preferred_element_type=F32)
        state_ref[hf] = st * state_decay[:, sl] + kv * blk_f
    y_inter = jnp.concatenate(inter, axis=1)

    def head_sums(t):
        tb = t.astype(BF16)
        return jnp.concatenate(
            [jnp.dot(tb[:, hf * HW:(hf + 1) * HW], blk, preferred_element_type=F32)
             for hf in range(2)], axis=1)

    lane_p = lax.broadcasted_iota(jnp.int32, (L, LANES), 1)
    lo_head = lane_p < HEAD_DIM
    pairs = []
    for p in range(RET_HEADS // 2):
        sl = slice(p * LANES, (p + 1) * LANES)
        qp, kp, vp = q[:, sl], kb[:, sl], vb[:, sl]
        ys = []
        for hh in range(2):
            head = 2 * p + hh
            keep = lo_head if hh == 0 else jnp.logical_not(lo_head)
            qh = jnp.where(keep, qp, 0.0).astype(BF16)
            sc = lax.dot_general(qh, kp, (((1,), (1,)), ((), ())),
                                 preferred_element_type=F32)
            ys.append(jnp.dot((sc * dec_ref[head]).astype(BF16), vp, preferred_element_type=F32))
        pairs.append(jnp.where(lo_head, ys[0], ys[1]))
    y = jnp.concatenate(pairs, axis=1) + y_inter

    inv_hd = 1.0 / HEAD_DIM
    mu = head_sums(y) * inv_hd
    d = y - mu
    var = head_sums(d * d) * inv_hd
    g = proj(3)
    y_ret = _silu(g) * (d * lax.rsqrt(var + GN_EPS)) * bret_ref[...]

    b_gate = proj(4)
    u = proj(5) * proj(6)
    ubuf_ref[SUBLANES:SUBLANES + L, :] = u
    u1 = ubuf_ref[SUBLANES - 1:SUBLANES - 1 + L, :]
    u2 = ubuf_ref[SUBLANES - 2:SUBLANES - 2 + L, :]
    ubuf_ref[0:SUBLANES, :] = ubuf_ref[L:L + SUBLANES, :]
    cw = convw_ref[...]
    conv = u2 * cw[0:1] + u1 * cw[1:2] + u * cw[2:3] + convb_ref[...]
    yc = b_gate * conv
    msc = head_sums(yc * yc) * (1.0 / CONV_GROUP_DIM)
    y_conv = yc * lax.rsqrt(msc + EPS) * bconv_ref[...]

    mix = (jnp.dot(y_ret.astype(BF16), wout_ref[0:W, :], preferred_element_type=F32)
           + jnp.dot(y_conv.astype(BF16), wout_ref[W:2 * W, :], preferred_element_type=F32))
    x1 = x + gate_m * mix
    x1_ref[0, tile_rows, :] = x1

    ms2 = jnp.mean(x1 * x1, axis=-1, keepdims=True)
    h2 = x1 * lax.rsqrt(ms2 + EPS) * gffn_ref[...]
    h2 = h2 * (1.0 + scale_f) + shift_f
    _store_token_rows(h2_ref, h2, row0=sub * L)

    hi, lo = _split_bf16(h2)
    w2 = wr_ref[...]
    nt_dims = (((1,), (1,)), ((), ()))
    parts = (lax.dot_general(w2, hi, nt_dims, preferred_element_type=F32)
             + lax.dot_general(w2, lo, nt_dims, preferred_element_type=F32))
    logits_t = parts[:ROUTER_LANES] + parts[ROUTER_LANES:] + br_ref[...]
    logit_ref[:, tile_rows] = logits_t[:ROUTER_ROWS]


def _mixer(x, mod, cos128, sin128, gmix, win, convw, convb, bret, bconv, wout, gffn,
           wr2, br, lg, lgl, blk):
    bsz, seq, d = x.shape
    L = SEQ_TILE * MIXER_SUBTILES
    ns = seq // L
    W = RET_HEADS * HEAD_DIM
    n = bsz * seq
    const2 = lambda b, s: (0, 0)
    in_specs = [
        pl.BlockSpec(memory_space=pltpu.SMEM),
        pl.BlockSpec((1, L, d), lambda b, s: (b, s, 0)),
        pl.BlockSpec((1, 6, d), lambda b, s: (b, 0, 0)),
        pl.BlockSpec((L, LANES), lambda b, s: (b * ns + s, 0)),
        pl.BlockSpec((L, LANES), lambda b, s: (b * ns + s, 0)),
        pl.BlockSpec((1, d), const2),
        pl.BlockSpec(win.shape, const2, pipeline_mode=pl.Buffered(1)),
        pl.BlockSpec(convw.shape, const2),
        pl.BlockSpec((1, W), const2),
        pl.BlockSpec((1, W), const2),
        pl.BlockSpec((1, W), const2),
        pl.BlockSpec(wout.shape, const2, pipeline_mode=pl.Buffered(1)),
        pl.BlockSpec((1, d), const2),
        pl.BlockSpec(wr2.shape, const2),
        pl.BlockSpec((ROUTER_LANES, 1), const2),
        pl.BlockSpec((1, W), const2),
        pl.BlockSpec((W // 2, W // 2), const2),
    ]
    assert d == SUBLANES * LANES, "one token must fill exactly one (8, 128) f32 tile"
    out_shape = (jax.ShapeDtypeStruct((bsz, seq, d), F32),
                 jax.ShapeDtypeStruct((n * SUBLANES, LANES), F32),
                 jax.ShapeDtypeStruct((ROUTER_ROWS, n), F32))
    out_specs = (pl.BlockSpec((1, L, d), lambda b, s: (b, s, 0)),
                 pl.BlockSpec((L * SUBLANES, LANES), lambda b, s: (b * ns + s, 0)),
                 pl.BlockSpec((ROUTER_ROWS, L), lambda b, s: (0, b * ns + s)))
    return pl.pallas_call(
        _mixer_kernel,
        out_shape=out_shape,
        grid=(bsz, ns),
        in_specs=in_specs,
        out_specs=out_specs,
        scratch_shapes=[pltpu.VMEM((2, W // 2, W // 2), F32),
                        pltpu.VMEM((SEQ_TILE + 2 * SUBLANES, W), F32),
                        pltpu.VMEM(win.shape, BF16),
                        pltpu.VMEM(wout.shape, BF16),
                        pltpu.VMEM((RET_HEADS, SEQ_TILE, SEQ_TILE), F32),
                        pltpu.VMEM((SEQ_TILE, W), F32),
                        pltpu.VMEM((SEQ_TILE, W), F32)],
        compiler_params=pltpu.CompilerParams(
            dimension_semantics=("arbitrary", "arbitrary"),
            vmem_limit_bytes=VMEM_LIMIT_BYTES),
        name="mixer",
    )(lg, x, mod, cos128, sin128, gmix, win, convw, convb, bret, bconv, wout, gffn,
      wr2, br, lgl, blk)


ROUTER_TILE = 4096
RES_E1, RES_E2, RES_W1, RES_W2, RES_A, RES_B = range(6)
PLAN_FIRST_TILE, PLAN_N_TILE, PLAN_PAD_START, PLAN_PAD_N, PLAN_N_USED = range(5)


def _rows8(vals, width):
    rid = lax.broadcasted_iota(jnp.int32, (SUBLANES, width), 0)
    out = jnp.zeros((SUBLANES, width), F32)
    for r, v in enumerate(vals):
        out = jnp.where(rid == r, v, out)
    return out


def _router_kernel(lt_ref, slab_ref, dest1_ref, dest2_ref, plan_ref, res_ref, cnt_ref):
    T = ROUTER_TILE
    phase = pl.program_id(0)
    j = pl.program_id(1)
    nblk = T // LANES
    big = F32(1e9)
    rid8 =lax.broadcasted_iota(jnp.int32, (SUBLANES, T), 0).astype(F32)
    rid_e = lax.broadcasted_iota(jnp.int32, (N_EXPERTS, LANES), 0).astype(F32)

    def onehot(e_row, k):
        return jnp.where(rid_e == e_row[:, k * LANES:(k + 1) * LANES], 1.0, 0.0)

    @pl.when((phase == 0) & (j == 0))
    def _():
        cnt_ref[...] = jnp.zeros_like(cnt_ref)

    @pl.when(phase == 0)
    def _():
        lt = lt_ref[...]
        g_rows = lt[ROUTER_GROUP_ROW0:ROUTER_GROUP_ROW0 + SUBLANES]
        gvalid = rid8 < float(N_GROUPS)
        gm = jnp.where(gvalid, g_rows, F32(-jnp.inf))
        gexp = jnp.exp(gm - jnp.max(gm, axis=0, keepdims=True))
        gp = gexp / jnp.sum(gexp, axis=0, keepdims=True)
        g_top = jnp.max(gp, axis=0, keepdims=True)
        g_idx = jnp.min(jnp.where(gvalid & (gp == g_top), rid8, big), axis=0, keepdims=True)

        def group_slab(g):
            r0 = ROUTER_EXPERT_ROW0 + g * EXPERTS_PER_GROUP
            return lt[r0:r0 + EXPERTS_PER_GROUP]
        sel = group_slab(N_GROUPS - 1)
        for g in range(N_GROUPS - 2, -1, -1):
            sel = jnp.where(g_idx == float(g), group_slab(g), sel)
        eexp = jnp.exp(sel - jnp.max(sel, axis=0, keepdims=True))
        ep = eexp / jnp.sum(eexp, axis=0, keepdims=True)
        p1 = jnp.max(ep, axis=0, keepdims=True)
        i1 = jnp.min(jnp.where(ep == p1, rid8, big), axis=0, keepdims=True)
        m2 = rid8 != i1
        p2 = jnp.max(jnp.where(m2, ep, -1.0), axis=0, keepdims=True)
        i2 = jnp.min(jnp.where(m2 & (ep == p2), rid8, big), axis=0, keepdims=True)
        den = p1 + p2
        w1 = p1 / den * g_top
        w2 = p2 / den * g_top
        e1 = g_idx * float(EXPERTS_PER_GROUP) + i1
        e2 = g_idx * float(EXPERTS_PER_GROUP) + i2

        ii = lax.broadcasted_iota(jnp.int32, (LANES, LANES), 0)
        jj = lax.broadcasted_iota(jnp.int32, (LANES, LANES), 1)
        upper = jnp.where(ii < jj, 1.0, 0.0).astype(BF16)
        base = cnt_ref[...]
        ranks1, ranks2 = [], []
        for k in range(nblk):
            o1, o2 = onehot(e1, k), onehot(e2, k)
            r1 = jnp.dot(o1.astype(BF16), upper, preferred_element_type=F32)
            r2 = jnp.dot(o2.astype(BF16), upper, preferred_element_type=F32)
            c1 = jnp.sum(o1, axis=1, keepdims=True)
            c2 = jnp.sum(o2, axis=1, keepdims=True)
            ranks1.append(jnp.sum(o1 * (base + r1), axis=0, keepdims=True))
            ranks2.append(jnp.sum(o2 * (base + c1 + r2), axis=0, keepdims=True))
            base = base + c1 + c2
        cnt_ref[...] = base
        rank1 = jnp.concatenate(ranks1, axis=1)
        rank2 = jnp.concatenate(ranks2, axis=1)
        res_ref[j] = _rows8([e1, e2, w1, w2, rank1, rank2], T)

    @pl.when(phase == 1)
    def _():
        tm = float(EXPERT_TILE)
        cnt = cnt_ref[...]
        tiles = jnp.floor((cnt + (tm - 1.0)) * (1.0 / tm))
        ei = lax.broadcasted_iota(jnp.int32, (N_EXPERTS, N_EXPERTS), 0)
        ej = lax.broadcasted_iota(jnp.int32, (N_EXPERTS, N_EXPERTS), 1)
        lower = jnp.where(ej < ei, 1.0, 0.0).astype(BF16)
        first = jnp.dot(lower, tiles.astype(BF16), preferred_element_type=F32)
        starts = first * tm

        res = res_ref[j]
        e1, e2 = res[RES_E1:RES_E1 + 1], res[RES_E2:RES_E2 + 1]
        d1, d2 = [], []
        for k in range(nblk):
            blk_lanes = slice(k * LANES, (k + 1) * LANES)
            d1.append(res[RES_A:RES_A + 1, blk_lanes]
                      + jnp.sum(onehot(e1, k) * starts, axis=0, keepdims=True))
            d2.append(res[RES_B:RES_B + 1, blk_lanes]
                      + jnp.sum(onehot(e2, k) * starts, axis=0, keepdims=True))
        dest1 = jnp.concatenate(d1, axis=1)
        dest2 = jnp.concatenate(d2, axis=1)
        dest1_ref[...] = dest1.astype(jnp.int32)
        dest2_ref[...] = dest2.astype(jnp.int32)

        table = jnp.concatenate(
            [_rows8([e1, e2, res[RES_W1:RES_W1 + 1], res[RES_W2:RES_W2 + 1], dest1, dest2], T),
             jnp.zeros((ROUTER_LANES - SUBLANES, T), F32)], axis=0)
        slab_ref[...] = table.T

        @pl.when(j == 0)
        def _():
            lane_e = lax.broadcasted_iota(jnp.int32, (N_EXPERTS, LANES), 1).astype(F32)

            def as_row(col):
                return jnp.sum(jnp.where(rid_e == lane_e, col, 0.0), axis=0, keepdims=True)
            n_used = jnp.sum(tiles, axis=0, keepdims=True)
            padded = tiles * tm
            plan_ref[...] = _rows8([as_row(first), as_row(tiles), as_row(starts + cnt),
                                    as_row(padded - cnt), n_used], LANES).astype(jnp.int32)


def _router(logits_t):
    n = logits_t.shape[1]
    T = ROUTER_TILE
    nt = n // T
    return pl.pallas_call(
        _router_kernel,
        out_shape=(jax.ShapeDtypeStruct((n, ROUTER_LANES), F32),
                   jax.ShapeDtypeStruct((1, n), jnp.int32),
                   jax.ShapeDtypeStruct((1, n), jnp.int32),
                   jax.ShapeDtypeStruct((SUBLANES, LANES), jnp.int32)),
        grid=(2, nt),
        in_specs=[pl.BlockSpec((ROUTER_ROWS, T), lambda p, j: (0, j * (1 - p) + (nt - 1) * p))],
        out_specs=(pl.BlockSpec((T, ROUTER_LANES), lambda p, j: (j * p, 0)),
                   pl.BlockSpec((1, T), lambda p, j: (0, j * p)),
                   pl.BlockSpec((1, T), lambda p, j: (0, j * p)),
                   pl.BlockSpec((SUBLANES, LANES), lambda p, j: (0, 0))),
        scratch_shapes=[pltpu.VMEM((nt, SUBLANES, T), F32),
                        pltpu.VMEM((N_EXPERTS, LANES), F32)],
        compiler_params=pltpu.CompilerParams(
            dimension_semantics=("arbitrary", "arbitrary"), vmem_limit_bytes=VMEM_LIMIT_BYTES),
        name="router",
    )(logits_t)


DISPATCH_TILE = 512
ROW_DMA_UNROLL = 8
PAD_UNITS = tuple(1 << b for b in reversed(range(EXPERT_TILE.bit_length() - 1)))


DISPATCH_LAG = 2
DISPATCH_SLOTS = DISPATCH_LAG + 2


def _dispatch_kernel(d0_ref, d1_ref, plan_ref, h2_hbm, xs_hbm,
                     stage, zbuf, isem, ssem, zsem):
    dt = DISPATCH_TILE
    ps = SUBLANES
    i = pl.program_id(0)
    nsteps = pl.num_programs(0)
    zrows = PAD_UNITS[0]

    def in_copy(blk, slot):
        src = h2_hbm.at[pl.ds(pl.multiple_of(blk * (dt * ps), dt * ps), dt * ps), :]
        return pltpu.make_async_copy(src, stage.at[slot], isem.at[slot])

    def wait_rows(slot):
        for _ in range(2):
            pltpu.make_async_copy(stage.at[slot], xs_hbm.at[pl.ds(0, dt * ps), :],
                                  ssem.at[slot]).wait()

    def pad_copy(start, unit):
        return pltpu.make_async_copy(zbuf.at[pl.ds(0, unit * ps), :],
                                     xs_hbm.at[pl.ds(pl.multiple_of(start * ps, ps), unit * ps), :],
                                     zsem)

    def pad_pass(do):
        def per_expert(e, carry):
            start = plan_ref[PLAN_PAD_START, e]
            npad = plan_ref[PLAN_PAD_N, e]
            for unit in PAD_UNITS:
                @pl.when((npad & unit) != 0)
                def _():
                    do(pad_copy(start + (npad & ~(2 * unit - 1)), unit))
            return carry
        lax.fori_loop(0, N_EXPERTS, per_expert, 0)

    def tail_pass(do):
        def per_unit(k, carry):
            do(pad_copy(k * zrows, zrows))
            return carry
        per_tile = EXPERT_TILE // zrows
        n_units = xs_hbm.shape[0] // (zrows * ps)
        lax.fori_loop(plan_ref[PLAN_N_USED, 0] * per_tile, n_units, per_unit, 0)

    slot = i % DISPATCH_SLOTS

    @pl.when(i == 0)
    def _():
        in_copy(0, 0).start()

        @pl.when(nsteps > 1)
        def _():
            in_copy(1, 1).start()

        zbuf[...] = jnp.zeros_like(zbuf)
        pad_pass(lambda cp: cp.start())
        tail_pass(lambda cp: cp.start())

    @pl.when(i >= DISPATCH_LAG)
    def _():
        wait_rows((i - DISPATCH_LAG) % DISPATCH_SLOTS)

    @pl.when(i + 2 < nsteps)
    def _():
        in_copy(i + 2, (i + 2) % DISPATCH_SLOTS).start()

    in_copy(i, slot).wait()
    base = i * dt
    src_ref = stage.at[slot]

    def body(r, carry):
        src = _token_rows(src_ref, r)
        for prio, d_ref in enumerate((d0_ref, d1_ref)):
            pltpu.make_async_copy(src, _token_rows(xs_hbm, d_ref[base + r]),
                                  ssem.at[slot]).start(priority=prio)
        return carry

    lax.fori_loop(0, dt, body, 0, unroll=ROW_DMA_UNROLL)

    @pl.when(i == nsteps - 1)
    def _():
        for back in range(DISPATCH_LAG - 1, -1, -1):
            @pl.when(i - back >= 0)
            def _():
                wait_rows((i - back) % DISPATCH_SLOTS)
        pad_pass(lambda cp: cp.wait())
        tail_pass(lambda cp: cp.wait())


def _dispatch(dest0, dest1, plan, h2t, p_rows):
    n = dest0.shape[0]
    dt = DISPATCH_TILE
    grid_spec = pltpu.PrefetchScalarGridSpec(
        num_scalar_prefetch=3,
        grid=(n // dt,),
        in_specs=[pl.BlockSpec(memory_space=pl.ANY)],
        out_specs=pl.BlockSpec(memory_space=pl.ANY),
        scratch_shapes=[pltpu.VMEM((DISPATCH_SLOTS, dt * SUBLANES, LANES), F32),
                        pltpu.VMEM((PAD_UNITS[0] * SUBLANES, LANES), F32),
                        pltpu.SemaphoreType.DMA((DISPATCH_SLOTS,)),
                        pltpu.SemaphoreType.DMA((DISPATCH_SLOTS,)),
                        pltpu.SemaphoreType.DMA(())],
    )
    return pl.pallas_call(
        _dispatch_kernel,
        out_shape=jax.ShapeDtypeStruct((p_rows * SUBLANES, LANES), F32),
        grid_spec=grid_spec,
        compiler_params=pltpu.CompilerParams(dimension_semantics=("arbitrary",)),
        name="dispatch",
    )(dest0, dest1, plan, h2t)


EXPERT_IN_SLOTS = 4
EXPERT_OUT_SLOTS = 3
MXU_COLS = 256


def _expert_kernel(plan_ref, xs_hbm, wg_ref, wu_ref, wd_ref, y_hbm,
                   xbuf, ybuf, wgb, wub, wdb, isem, osem):
    tm = EXPERT_TILE
    rows = tm * SUBLANES
    ni, no = EXPERT_IN_SLOTS, EXPERT_OUT_SLOTS
    e = pl.program_id(0)
    n_used = plan_ref[PLAN_N_USED, 0]
    n_mine = plan_ref[PLAN_N_TILE, e]

    def tile_rows(ref, g):
        return ref.at[pl.ds(pl.multiple_of(g * rows, rows), rows), :]

    def in_copy(g):
        return pltpu.make_async_copy(tile_rows(xs_hbm, g), xbuf.at[g % ni], isem.at[g % ni])

    def out_copy(g):
        return pltpu.make_async_copy(ybuf.at[g % no], tile_rows(y_hbm, g), osem.at[g % no])

    @pl.when(e == 0)
    def _():
        for g0 in range(ni - 1):
            @pl.when(g0 < n_used)
            def _():
                in_copy(g0).start()

    def tile(g, carry, cast_weights=False):
        in_copy(g).wait()

        @pl.when(g >= no)
        def _():
            out_copy(g - no).wait()

        @pl.when(g + ni - 1 < n_used)
        def _():
            in_copy(g + ni - 1).start()

        if cast_weights:
            wgb[...] = wg_ref[0].astype(BF16)
            wub[...] = wu_ref[0].astype(BF16)
            wdb[...] = wd_ref[0].astype(BF16)
        xb = _load_token_rows(xbuf.at[g % ni], tm).astype(BF16)
        a = jnp.dot(xb, wgb[...], preferred_element_type=F32)
        u = jnp.dot(xb, wub[...], preferred_element_type=F32)
        hid = (_silu(a) * u).astype(BF16)
        out = ybuf.at[g % no]
        per_piece = MXU_COLS // LANES
        for p in range(wdb.shape[1] // MXU_COLS):
            y = jnp.dot(hid, wdb[:, p * MXU_COLS:(p + 1) * MXU_COLS], preferred_element_type=F32)
            for q in range(per_piece):
                out[pl.ds(p * per_piece + q, tm, stride=SUBLANES), :] = y[:, q * LANES:(q + 1) * LANES]
        out_copy(g).start()
        return carry

    first = plan_ref[PLAN_FIRST_TILE, e]

    @pl.when(n_mine > 0)
    def _():
        tile(first, 0, cast_weights=True)

    lax.fori_loop(first + 1, first + n_mine, tile, 0)

    @pl.when(e == pl.num_programs(0) - 1)
    def _():
        for back in range(no, 0, -1):
            @pl.when(n_used >= back)
            def _():
                out_copy(n_used - back).wait()


def _experts(plan, xs, wg, wu, wd):
    tm = EXPERT_TILE
    n_exp, d, de = wg.shape
    grid_spec = pltpu.PrefetchScalarGridSpec(
        num_scalar_prefetch=1,
        grid=(n_exp,),
        in_specs=[
            pl.BlockSpec(memory_space=pl.ANY),
            pl.BlockSpec((1, d, de), lambda e, *_: (e, 0, 0)),
            pl.BlockSpec((1, d, de), lambda e, *_: (e, 0, 0)),
            pl.BlockSpec((1, de, d), lambda e, *_: (e, 0, 0)),
        ],
        out_specs=pl.BlockSpec(memory_space=pl.ANY),
        scratch_shapes=[pltpu.VMEM((EXPERT_IN_SLOTS, tm * SUBLANES, LANES), F32),
                        pltpu.VMEM((EXPERT_OUT_SLOTS, tm * SUBLANES, LANES), F32),
                        pltpu.VMEM((d, de), BF16),
                        pltpu.VMEM((d, de), BF16),
                        pltpu.VMEM((de, d), BF16),
                        pltpu.SemaphoreType.DMA((EXPERT_IN_SLOTS,)),
                        pltpu.SemaphoreType.DMA((EXPERT_OUT_SLOTS,))],
    )
    return pl.pallas_call(
        _expert_kernel,
        out_shape=jax.ShapeDtypeStruct(xs.shape, F32),
        input_output_aliases={1: 0},
        grid_spec=grid_spec,
        compiler_params=pltpu.CompilerParams(
            dimension_semantics=("arbitrary",),
            vmem_limit_bytes=VMEM_LIMIT_BYTES),
        name="experts",
    )(plan, xs, wg, wu, wd)


COMBINE_SLOTS = 3


def _combine_kernel(p0_ref, p1_ref, y_hbm, x1_ref, slab_ref, mod_ref, gfin_ref, o_ref,
                    *scratch):
    tm = COMBINE_TILE
    ns = COMBINE_SLOTS
    ahead = ns - 1
    ybufs, sem = scratch[:ns], scratch[ns]
    i = pl.program_id(0)
    nt = pl.num_programs(0)

    def row_copies(tile, slot, r):
        for j, p_ref in enumerate((p0_ref, p1_ref)):
            pltpu.make_async_copy(_token_rows(y_hbm, p_ref[tile * tm + r]),
                                  _token_rows(ybufs[slot].at[j], r),
                                  sem.at[slot]).start(priority=j)

    def wait_tile(slot):
        for j in range(2):
            pltpu.make_async_copy(y_hbm.at[pl.ds(0, tm * SUBLANES), :], ybufs[slot].at[j],
                                  sem.at[slot]).wait()

    @pl.when(i == 0)
    def _():
        for t0 in range(ahead):
            @pl.when(t0 < nt)
            def _():
                def body(r, carry):
                    row_copies(t0, t0, r)
                    return carry
                lax.fori_loop(0, tm, body, 0, unroll=ROW_DMA_UNROLL)

    n_chunks = SUBLANES
    batch = tm // n_chunks

    def step(slot):
        wait_tile(slot)
        nxt = jnp.minimum(i + ahead, nt - 1)
        nslot = (slot + ahead) % ns
        slab = slab_ref[...]
        w0 = slab[:, 2:3]
        w1 = slab[:, 3:4]
        gate_f = mod_ref[0][5:6]
        sq = jnp.zeros((tm, LANES), F32)
        for c in range(n_chunks):
            lanes = slice(c * LANES, (c + 1) * LANES)
            y0 = ybufs[slot].at[0][pl.ds(c, tm, stride=SUBLANES), :]
            y1 = ybufs[slot].at[1][pl.ds(c, tm, stride=SUBLANES), :]
            xo = x1_ref[:, lanes] + gate_f[:, lanes] * (w0 * y0 + w1 * y1)
            sq = sq + xo * xo
            o_ref[:, lanes] = xo
            for r in range(c * batch, (c + 1) * batch):
                row_copies(nxt, nslot, r)
        ms = jnp.sum(sq, axis=-1, keepdims=True) * (1.0 / (n_chunks * LANES))
        scale = lax.rsqrt(ms + EPS)
        for c in range(n_chunks):
            lanes = slice(c * LANES, (c + 1) * LANES)
            o_ref[:, lanes] = o_ref[:, lanes] * scale * gfin_ref[:, lanes]

    for slot in range(ns):
        @pl.when(i % ns == slot)
        def _():
            step(slot)

    @pl.when(i == nt - 1)
    def _():
        for k in range(ahead):
            for slot in range(ns):
                @pl.when((nt - 1 - k >= 0) & ((nt - 1 - k + ahead) % ns == slot))
                def _():
                    wait_tile(slot)


def _combine(p0, p1, y, x1, slab, mod, gfin, seq):
    n, d = x1.shape
    tm = COMBINE_TILE
    tiles_per_seq = seq // tm
    grid_spec = pltpu.PrefetchScalarGridSpec(
        num_scalar_prefetch=2,
        grid=(n // tm,),
        in_specs=[
            pl.BlockSpec(memory_space=pl.ANY),
            pl.BlockSpec((tm, d), lambda i, a, b: (i, 0)),
            pl.BlockSpec((tm, ROUTER_LANES), lambda i, a, b: (i, 0)),
            pl.BlockSpec((1, 6, d), lambda i, a, b: (i // tiles_per_seq, 0, 0)),
            pl.BlockSpec((1, d), lambda i, a, b: (0, 0)),
        ],
        out_specs=pl.BlockSpec((tm, d), lambda i, a, b: (i, 0)),
        scratch_shapes=([pltpu.VMEM((2, tm * SUBLANES, LANES), F32)] * COMBINE_SLOTS
                        + [pltpu.SemaphoreType.DMA((COMBINE_SLOTS,))]),
    )
    return pl.pallas_call(
        _combine_kernel,
        out_shape=jax.ShapeDtypeStruct((n, d), F32),
        grid_spec=grid_spec,
        compiler_params=pltpu.CompilerParams(
            dimension_semantics=("arbitrary",),
            vmem_limit_bytes=VMEM_LIMIT_BYTES),
        name="combine",
    )(p0, p1, y, x1, slab, mod, gfin)


def kernel(x, c, positions, ada_w, ada_b, norm_mix_g, norm_ffn_g, w_in, conv_w, conv_b,
           beta_ret, beta_conv, w_out, router_group_w, router_group_b, router_expert_w,
           router_expert_b, expert_w_gate, expert_w_up, expert_w_down, norm_final_g):
    bsz, seq, d = x.shape
    n = bsz * seq
    depth = ada_w.shape[0]
    assert depth == 1, "the combine kernel fuses the trunk's final RMSNorm (single layer)"
    W = RET_HEADS * HEAD_DIM

    l = 0
    mod, cos128, sin128 = _prologue(c, ada_w[l], ada_b[l], positions)
    mod = mod.reshape(bsz, 6, d)
    heads = jnp.arange(RET_HEADS, dtype=F32)
    lg = jnp.log1p(-jnp.exp2(-5.0 - heads))
    lgl = jnp.repeat(lg, HEAD_DIM).reshape(1, W)
    assert CONV_GROUP_DIM == HEAD_DIM, "conv groups and retention heads share the 64-lane block sums"
    blk_np = np.kron(np.eye(RET_HEADS // 2, dtype=np.float32),
                     np.ones((HEAD_DIM, HEAD_DIM), np.float32))
    blk = jnp.asarray(blk_np, dtype=BF16)

    gap = ROUTER_EXPERT_ROW0 - N_GROUPS
    tail = ROUTER_LANES - ROUTER_ROWS
    wr = jnp.concatenate([router_group_w[l].T, jnp.zeros((gap, d), F32),
                          router_expert_w[l].T, jnp.zeros((tail, d), F32)], axis=0)
    wr2 = jnp.concatenate(_split_bf16(wr), axis=0)
    br = jnp.concatenate([router_group_b[l], jnp.zeros((gap,), F32),
                          router_expert_b[l], jnp.zeros((tail,), F32)]).reshape(ROUTER_LANES, 1)

    x1, h2t, logits_t = _mixer(
        x, mod, cos128, sin128, norm_mix_g[l].reshape(1, d), w_in[l],
        conv_w[l], conv_b[l].reshape(1, W), beta_ret[l].reshape(1, W),
        beta_conv[l].reshape(1, W), w_out[l], norm_ffn_g[l].reshape(1, d),
        wr2, br, lg, lgl, blk)
    slab, p0, p1, plan = _router(logits_t)
    p0, p1 = p0.reshape(n), p1.reshape(n)
    p_rows = 2 * n + N_EXPERTS * EXPERT_TILE

    xs = _dispatch(p0, p1, plan, h2t, p_rows)
    de = expert_w_gate.shape[-1]
    y = _experts(plan, xs,
                 expert_w_gate[l].reshape(N_EXPERTS, d, de),
                 expert_w_up[l].reshape(N_EXPERTS, d, de),
                 expert_w_down[l].reshape(N_EXPERTS, de, d))
    out = _combine(p0, p1, y, x1.reshape(n, d), slab, mod, norm_final_g.reshape(1, d), seq)
    return out.reshape(bsz, seq, d)
```

```python
import jax
import jax.numpy as jnp
import numpy as np
from jax import lax
from jax.experimental import pallas as pl
from jax.experimental.pallas import tpu as pltpu

F32 = jnp.float32
BF16 = jnp.bfloat16

CHUNK = 64
RET_HEADS = 8
HEAD_DIM = 64
CONV_GROUP_DIM = 64
ROPE_BASE = 10000.0
N_GROUPS = 4
EXPERTS_PER_GROUP = 8
N_EXPERTS = N_GROUPS * EXPERTS_PER_GROUP
EPS = 1e-6
GN_EPS = 1e-5

LANES = 128
SUBLANES = 8
VMEM_LIMIT_BYTES = 56 * 1024 * 1024

SEQ_TILE = 256
EXPERT_TILE = 256
COMBINE_TILE = 256
ROUTER_LANES = LANES
ROUTER_GROUP_ROW0 = 0
ROUTER_EXPERT_ROW0 = SUBLANES
ROUTER_ROWS = ROUTER_EXPERT_ROW0 + N_EXPERTS


def _silu(v):
    return v * (1.0 / (1.0 + jnp.exp(-v)))


def _split_bf16(v):
    hi = v.astype(BF16)
    return hi, (v - hi.astype(F32)).astype(BF16)


def _adaln_block(c_ref, w_ref, b_ref, o_ref):
    s_hi, s_lo = _split_bf16(_silu(c_ref[...]))
    w_hi, w_lo = _split_bf16(w_ref[...])
    o_ref[...] = (jnp.dot(s_hi, w_hi, preferred_element_type=F32)
                  + jnp.dot(s_lo, w_hi, preferred_element_type=F32)
                  + jnp.dot(s_hi, w_lo, preferred_element_type=F32) + b_ref[...])


def _rope_block(pos_ref, invf_ref, cos_ref, sin_ref):
    half = HEAD_DIM // 2
    per_row = LANES // half
    r = pos_ref.shape[0]
    ang = pos_ref[...].astype(F32) * invf_ref[...]
    lane = lax.broadcasted_iota(jnp.int32, (r, LANES), 1)
    quarter = lane // half
    sign = jnp.where(quarter % 2 == 0, -1.0, 1.0)
    for table, out_ref, scale in ((jnp.cos(ang), cos_ref, None), (jnp.sin(ang), sin_ref, sign)):
        rolled = [table] + [pltpu.roll(table, half * k, 1) for k in range(1, per_row)]
        for q in range(per_row):
            val = rolled[(0 - q) % per_row]
            for k in range(1, per_row):
                val = jnp.where(quarter == k, rolled[(k - q) % per_row], val)
            if scale is not None:
                val = val * scale
            out_ref[pl.ds(q, r, stride=per_row), :] = val


PROLOGUE_STEPS = 8


def _prologue_kernel(c_ref, w_ref, b_ref, pos_ref, invf_ref, mod_ref, cos_ref, sin_ref):
    _adaln_block(c_ref, w_ref, b_ref, mod_ref)
    _rope_block(pos_ref, invf_ref, cos_ref, sin_ref)


def _prologue(c, w, b, positions):
    bsz, d = c.shape
    n_mod = w.shape[1]
    n = positions.size
    half = HEAD_DIM // 2
    inv_freq = ROPE_BASE ** (-jnp.arange(0, HEAD_DIM, 2, dtype=F32) / HEAD_DIM)
    per_row = LANES // half
    rows = n // per_row
    pos_rep = jnp.broadcast_to(positions.reshape(n, 1), (n, half)).reshape(rows, LANES)
    invf = jnp.tile(inv_freq, per_row).reshape(1, LANES)
    steps = PROLOGUE_STEPS
    tn, tr = n_mod // steps, rows // steps
    assert tn % LANES == 0 and tr % SUBLANES == 0
    return pl.pallas_call(
        _prologue_kernel,
        out_shape=(jax.ShapeDtypeStruct((bsz, n_mod), F32),
                   jax.ShapeDtypeStruct((n, LANES), F32),
                   jax.ShapeDtypeStruct((n, LANES), F32)),
        grid=(steps,),
        in_specs=[pl.BlockSpec((bsz, d), lambda j: (0, 0)),
                  pl.BlockSpec((d, tn), lambda j: (0, j)),
                  pl.BlockSpec((1, tn), lambda j: (0, j)),
                  pl.BlockSpec((tr, LANES), lambda j: (j, 0)),
                  pl.BlockSpec((1, LANES), lambda j: (0, 0))],
        out_specs=(pl.BlockSpec((bsz, tn), lambda j: (0, j)),
                   pl.BlockSpec((tr * per_row, LANES), lambda j: (j, 0)),
                   pl.BlockSpec((tr * per_row, LANES), lambda j: (j, 0))),
        name="prologue",
    )(c, w, b.reshape(1, n_mod), pos_rep, invf)


def _load_token_rows(ref, rows):
    return jnp.concatenate(
        [ref[pl.ds(c, rows, stride=SUBLANES), :] for c in range(SUBLANES)], axis=1)


def _store_token_rows(ref, val, row0=0):
    rows = val.shape[0]
    for c in range(SUBLANES):
        ref[pl.ds(row0 * SUBLANES + c, rows, stride=SUBLANES), :] = val[:, c * LANES:(c + 1) * LANES]


def _token_rows(ref, row):
    return ref.at[pl.ds(pl.multiple_of(row * SUBLANES, SUBLANES), SUBLANES), :]


MIXER_SUBTILES = 2


MIXER_WIN, MIXER_WOUT = 6, 11


def _mixer_kernel(*refs):
    *refs, winb_ref, woutb_ref, dec_ref, qdec_ref, kdec_ref = refs
    state_ref, ubuf_ref = refs[-2:]
    L = SEQ_TILE

    @pl.when((pl.program_id(0) == 0) & (pl.program_id(1) == 0))
    def _():
        for src, dst in ((refs[MIXER_WIN], winb_ref), (refs[MIXER_WOUT], woutb_ref)):
            step = RET_HEADS * HEAD_DIM
            for c0 in range(0, src.shape[1], step):
                dst[:, c0:c0 + step] = src[:, c0:c0 + step].astype(BF16)
        lg_ref, lgl_ref = refs[0], refs[15]
        ii = lax.broadcasted_iota(jnp.int32, (L, L), 0)
        jj = lax.broadcasted_iota(jnp.int32, (L, L), 1)
        dist = jnp.abs(ii - jj).astype(F32)
        allowed = (jj // CHUNK) <= (ii // CHUNK)
        for head in range(RET_HEADS):
            dec_ref[head] = jnp.where(allowed, jnp.exp(lg_ref[head] * dist), 0.0)
        rowf = lax.broadcasted_iota(jnp.int32, qdec_ref.shape, 0).astype(F32)
        qdec_ref[...] = jnp.exp(lgl_ref[...] * (rowf + 1.0))
        kdec_ref[...] = jnp.exp(lgl_ref[...] * (float(L - 1) - rowf))

    @pl.when(pl.program_id(1) == 0)
    def _():
        state_ref[...] = jnp.zeros_like(state_ref)
        ubuf_ref[0:SUBLANES, :] = jnp.zeros((SUBLANES, ubuf_ref.shape[1]), F32)

    refs[MIXER_WIN], refs[MIXER_WOUT] = winb_ref, woutb_ref
    for sub in range(MIXER_SUBTILES):
        _mixer_tile(sub, *refs, dec_ref, qdec_ref, kdec_ref)


def _mixer_tile(sub, lg_ref, x_ref, mod_ref, cos_ref, sin_ref, gmix_ref, win_ref, convw_ref,
                convb_ref, bret_ref, bconv_ref, wout_ref, gffn_ref, wr_ref, br_ref, lgl_ref,
                blk_ref, x1_ref, h2_ref, logit_ref, state_ref, ubuf_ref,
                dec_ref, qdec_ref, kdec_ref):
    L = SEQ_TILE
    W = RET_HEADS * HEAD_DIM
    tile_rows = slice(sub * L, (sub + 1) * L)

    x = x_ref[0, tile_rows, :]
    mod = mod_ref[0]
    shift_m, scale_m, gate_m = mod[0:1], mod[1:2], mod[2:3]
    shift_f, scale_f = mod[3:4], mod[4:5]

    ms = jnp.mean(x * x, axis=-1, keepdims=True)
    h = x * lax.rsqrt(ms + EPS) * gmix_ref[...]
    h = h * (1.0 + scale_m) + shift_m
    hb = h.astype(BF16)

    def proj(i):
        return jnp.dot(hb, win_ref[:, i * W:(i + 1) * W], preferred_element_type=F32)

    cos = jnp.concatenate([cos_ref[tile_rows, :]] * 4, axis=1)
    sin = jnp.concatenate([sin_ref[tile_rows, :]] * 4, axis=1)
    lane_w = lax.broadcasted_iota(jnp.int32, (L, W), 1)
    first_half = (lane_w & (HEAD_DIM - 1)) < (HEAD_DIM // 2)

    def rot(t):
        partner = jnp.where(first_half, pltpu.roll(t, W - HEAD_DIM // 2, 1),
                            pltpu.roll(t, HEAD_DIM // 2, 1))
        return t * cos + partner * sin

    q = rot(proj(0))
    k = rot(proj(1)) * (HEAD_DIM ** -0.5)
    v = proj(2)
    vb = v.astype(BF16)
    kb = k.astype(BF16)

    lgl = lgl_ref[...]
    qd = q * qdec_ref[...]
    kd = k * kdec_ref[...]
    blk = blk_ref[...]
    HW = W // 2
    blk_f = blk.astype(F32)
    qdb = qd.astype(BF16)
    kdb = kd.astype(BF16)
    state_decay = jnp.exp(lgl * float(L))
    inter = []
    for hf in range(2):
        sl = slice(hf * HW, (hf + 1) * HW)
        st = state_ref[hf]
        inter.append(jnp.dot(qdb[:, sl], st.astype(BF16), preferred_element_type=F32))
        kv = lax.dot_general(kdb[:, sl], vb[:, sl], (((0,), (0,)), ((), ())),
                             preferred_element_type=F32)
        state_ref[hf] = st * state_decay[:, sl] + kv * blk_f
    y_inter = jnp.concatenate(inter, axis=1)

    def head_sums(t):
        tb = t.astype(BF16)
        return jnp.concatenate(
            [jnp.dot(tb[:, hf * HW:(hf + 1) * HW], blk, preferred_element_type=F32)
             for hf in range(2)], axis=1)

    lane_p = lax.broadcasted_iota(jnp.int32, (L, LANES), 1)
    lo_head = lane_p < HEAD_DIM
    pairs = []
    for p in range(RET_HEADS // 2):
        sl = slice(p * LANES, (p + 1) * LANES)
        qp, kp, vp = q[:, sl], kb[:, sl], vb[:, sl]
        ys = []
        for hh in range(2):
            head = 2 * p + hh
            keep = lo_head if hh == 0 else jnp.logical_not(lo_head)
            qh = jnp.where(keep, qp, 0.0).astype(BF16)
            sc = lax.dot_general(qh, kp, (((1,), (1,)), ((), ())),
                                 preferred_element_type=F32)
            ys.append(jnp.dot((sc * dec_ref[head]).astype(BF16), vp, preferred_element_type=F32))
        pairs.append(jnp.where(lo_head, ys[0], ys[1]))
    y = jnp.concatenate(pairs, axis=1) + y_inter

    inv_hd = 1.0 / HEAD_DIM
    mu = head_sums(y) * inv_hd
    d = y - mu
    var = head_sums(d * d) * inv_hd
    g = proj(3)
    y_ret = _silu(g) * (d * lax.rsqrt(var + GN_EPS)) * bret_ref[...]

    b_gate = proj(4)
    u = proj(5) * proj(6)
    ubuf_ref[SUBLANES:SUBLANES + L, :] = u
    u1 = ubuf_ref[SUBLANES - 1:SUBLANES - 1 + L, :]
    u2 = ubuf_ref[SUBLANES - 2:SUBLANES - 2 + L, :]
    ubuf_ref[0:SUBLANES, :] = ubuf_ref[L:L + SUBLANES, :]
    cw = convw_ref[...]
    conv = u2 * cw[0:1] + u1 * cw[1:2] + u * cw[2:3] + convb_ref[...]
    yc = b_gate * conv
    msc = head_sums(yc * yc) * (1.0 / CONV_GROUP_DIM)
    y_conv = yc * lax.rsqrt(msc + EPS) * bconv_ref[...]

    mix = (jnp.dot(y_ret.astype(BF16), wout_ref[0:W, :], preferred_element_type=F32)
           + jnp.dot(y_conv.astype(BF16), wout_ref[W:2 * W, :], preferred_element_type=F32))
    x1 = x + gate_m * mix
    x1_ref[0, tile_rows, :] = x1

    ms2 = jnp.mean(x1 * x1, axis=-1, keepdims=True)
    h2 = x1 * lax.rsqrt(ms2 + EPS) * gffn_ref[...]
    h2 = h2 * (1.0 + scale_f) + shift_f
    _store_token_rows(h2_ref, h2, row0=sub * L)

    hi, lo = _split_bf16(h2)
    w2 = wr_ref[...]
    nt_dims = (((1,), (1,)), ((), ()))
    parts = (lax.dot_general(w2, hi, nt_dims, preferred_element_type=F32)
             + lax.dot_general(w2, lo, nt_dims, preferred_element_type=F32))
    logits_t = parts[:ROUTER_LANES] + parts[ROUTER_LANES:] + br_ref[...]
    logit_ref[:, tile_rows] = logits_t[:ROUTER_ROWS]


def _mixer(x, mod, cos128, sin128, gmix, win, convw, convb, bret, bconv, wout, gffn,
           wr2, br, lg, lgl, blk):
    bsz, seq, d = x.shape
    L = SEQ_TILE * MIXER_SUBTILES
    ns = seq // L
    W = RET_HEADS * HEAD_DIM
    n = bsz * seq
    const2 = lambda b, s: (0, 0)
    in_specs = [
        pl.BlockSpec(memory_space=pltpu.SMEM),
        pl.BlockSpec((1, L, d), lambda b, s: (b, s, 0)),
        pl.BlockSpec((1, 6, d), lambda b, s: (b, 0, 0)),
        pl.BlockSpec((L, LANES), lambda b, s: (b * ns + s, 0)),
        pl.BlockSpec((L, LANES), lambda b, s: (b * ns + s, 0)),
        pl.BlockSpec((1, d), const2),
        pl.BlockSpec(win.shape, const2, pipeline_mode=pl.Buffered(1)),
        pl.BlockSpec(convw.shape, const2),
        pl.BlockSpec((1, W), const2),
        pl.BlockSpec((1, W), const2),
        pl.BlockSpec((1, W), const2),
        pl.BlockSpec(wout.shape, const2, pipeline_mode=pl.Buffered(1)),
        pl.BlockSpec((1, d), const2),
        pl.BlockSpec(wr2.shape, const2),
        pl.BlockSpec((ROUTER_LANES, 1), const2),
        pl.BlockSpec((1, W), const2),
        pl.BlockSpec((W // 2, W // 2), const2),
    ]
    assert d == SUBLANES * LANES, "one token must fill exactly one (8, 128) f32 tile"
    out_shape = (jax.ShapeDtypeStruct((bsz, seq, d), F32),
                 jax.ShapeDtypeStruct((n * SUBLANES, LANES), F32),
                 jax.ShapeDtypeStruct((ROUTER_ROWS, n), F32))
    out_specs = (pl.BlockSpec((1, L, d), lambda b, s: (b, s, 0)),
                 pl.BlockSpec((L * SUBLANES, LANES), lambda b, s: (b * ns + s, 0)),
                 pl.BlockSpec((ROUTER_ROWS, L), lambda b, s: (0, b * ns + s)))
    return pl.pallas_call(
        _mixer_kernel,
        out_shape=out_shape,
        grid=(bsz, ns),
        in_specs=in_specs,
        out_specs=out_specs,
        scratch_shapes=[pltpu.VMEM((2, W // 2, W // 2), F32),
                        pltpu.VMEM((SEQ_TILE + 2 * SUBLANES, W), F32),
                        pltpu.VMEM(win.shape, BF16),
                        pltpu.VMEM(wout.shape, BF16),
                        pltpu.VMEM((RET_HEADS, SEQ_TILE, SEQ_TILE), F32),
                        pltpu.VMEM((SEQ_TILE, W), F32),
                        pltpu.VMEM((SEQ_TILE, W), F32)],
        compiler_params=pltpu.CompilerParams(
            dimension_semantics=("arbitrary", "arbitrary"),
            vmem_limit_bytes=VMEM_LIMIT_BYTES),
        name="mixer",
    )(lg, x, mod, cos128, sin128, gmix, win, convw, convb, bret, bconv, wout, gffn,
      wr2, br, lgl, blk)


ROUTER_TILE = 4096
RES_E1, RES_E2, RES_W1, RES_W2, RES_A, RES_B = range(6)
PLAN_FIRST_TILE, PLAN_N_TILE, PLAN_PAD_START, PLAN_PAD_N, PLAN_N_USED = range(5)


def _rows8(vals, width):
    rid = lax.broadcasted_iota(jnp.int32, (SUBLANES, width), 0)
    out = jnp.zeros((SUBLANES, width), F32)
    for r, v in enumerate(vals):
        out = jnp.where(rid == r, v, out)
    return out


def _router_kernel(lt_ref, slab_ref, dest1_ref, dest2_ref, plan_ref, res_ref, cnt_ref):
    T = ROUTER_TILE
    phase = pl.program_id(0)
    j = pl.program_id(1)
    nblk = T // LANES
    big = F32(1e9)
    rid8 =lax.broadcasted_iota(jnp.int32, (SUBLANES, T), 0).astype(F32)
    rid_e = lax.broadcasted_iota(jnp.int32, (N_EXPERTS, LANES), 0).astype(F32)

    def onehot(e_row, k):
        return jnp.where(rid_e == e_row[:, k * LANES:(k + 1) * LANES], 1.0, 0.0)

    @pl.when((phase == 0) & (j == 0))
    def _():
        cnt_ref[...] = jnp.zeros_like(cnt_ref)

    @pl.when(phase == 0)
    def _():
        lt = lt_ref[...]
        g_rows = lt[ROUTER_GROUP_ROW0:ROUTER_GROUP_ROW0 + SUBLANES]
        gvalid = rid8 < float(N_GROUPS)
        gm = jnp.where(gvalid, g_rows, F32(-jnp.inf))
        gexp = jnp.exp(gm - jnp.max(gm, axis=0, keepdims=True))
        gp = gexp / jnp.sum(gexp, axis=0, keepdims=True)
        g_top = jnp.max(gp, axis=0, keepdims=True)
        g_idx = jnp.min(jnp.where(gvalid & (gp == g_top), rid8, big), axis=0, keepdims=True)

        def group_slab(g):
            r0 = ROUTER_EXPERT_ROW0 + g * EXPERTS_PER_GROUP
            return lt[r0:r0 + EXPERTS_PER_GROUP]
        sel = group_slab(N_GROUPS - 1)
        for g in range(N_GROUPS - 2, -1, -1):
            sel = jnp.where(g_idx == float(g), group_slab(g), sel)
        eexp = jnp.exp(sel - jnp.max(sel, axis=0, keepdims=True))
        ep = eexp / jnp.sum(eexp, axis=0, keepdims=True)
        p1 = jnp.max(ep, axis=0, keepdims=True)
        i1 = jnp.min(jnp.where(ep == p1, rid8, big), axis=0, keepdims=True)
        m2 = rid8 != i1
        p2 = jnp.max(jnp.where(m2, ep, -1.0), axis=0, keepdims=True)
        i2 = jnp.min(jnp.where(m2 & (ep == p2), rid8, big), axis=0, keepdims=True)
        den = p1 + p2
        w1 = p1 / den * g_top
        w2 = p2 / den * g_top
        e1 = g_idx * float(EXPERTS_PER_GROUP) + i1
        e2 = g_idx * float(EXPERTS_PER_GROUP) + i2

        ii = lax.broadcasted_iota(jnp.int32, (LANES, LANES), 0)
        jj = lax.broadcasted_iota(jnp.int32, (LANES, LANES), 1)
        upper = jnp.where(ii < jj, 1.0, 0.0).astype(BF16)
        base = cnt_ref[...]
        ranks1, ranks2 = [], []
        for k in range(nblk):
            o1, o2 = onehot(e1, k), onehot(e2, k)
            r1 = jnp.dot(o1.astype(BF16), upper, preferred_element_type=F32)
            r2 = jnp.dot(o2.astype(BF16), upper, preferred_element_type=F32)
            c1 = jnp.sum(o1, axis=1, keepdims=True)
            c2 = jnp.sum(o2, axis=1, keepdims=True)
            ranks1.append(jnp.sum(o1 * (base + r1), axis=0, keepdims=True))
            ranks2.append(jnp.sum(o2 * (base + c1 + r2), axis=0, keepdims=True))
            base = base + c1 + c2
        cnt_ref[...] = base
        rank1 = jnp.concatenate(ranks1, axis=1)
        rank2 = jnp.concatenate(ranks2, axis=1)
        res_ref[j] = _rows8([e1, e2, w1, w2, rank1, rank2], T)

    @pl.when(phase == 1)
    def _():
        tm = float(EXPERT_TILE)
        cnt = cnt_ref[...]
        tiles = jnp.floor((cnt + (tm - 1.0)) * (1.0 / tm))
        ei = lax.broadcasted_iota(jnp.int32, (N_EXPERTS, N_EXPERTS), 0)
        ej = lax.broadcasted_iota(jnp.int32, (N_EXPERTS, N_EXPERTS), 1)
        lower = jnp.where(ej < ei, 1.0, 0.0).astype(BF16)
        first = jnp.dot(lower, tiles.astype(BF16), preferred_element_type=F32)
        starts = first * tm

        res = res_ref[j]
        e1, e2 = res[RES_E1:RES_E1 + 1], res[RES_E2:RES_E2 + 1]
        d1, d2 = [], []
        for k in range(nblk):
            blk_lanes = slice(k * LANES, (k + 1) * LANES)
            d1.append(res[RES_A:RES_A + 1, blk_lanes]
                      + jnp.sum(onehot(e1, k) * starts, axis=0, keepdims=True))
            d2.append(res[RES_B:RES_B + 1, blk_lanes]
                      + jnp.sum(onehot(e2, k) * starts, axis=0, keepdims=True))
        dest1 = jnp.concatenate(d1, axis=1)
        dest2 = jnp.concatenate(d2, axis=1)
        dest1_ref[...] = dest1.astype(jnp.int32)
        dest2_ref[...] = dest2.astype(jnp.int32)

        table = jnp.concatenate(
            [_rows8([e1, e2, res[RES_W1:RES_W1 + 1], res[RES_W2:RES_W2 + 1], dest1, dest2], T),
             jnp.zeros((ROUTER_LANES - SUBLANES, T), F32)], axis=0)
        slab_ref[...] = table.T

        @pl.when(j == 0)
        def _():
            lane_e = lax.broadcasted_iota(jnp.int32, (N_EXPERTS, LANES), 1).astype(F32)

            def as_row(col):
                return jnp.sum(jnp.where(rid_e == lane_e, col, 0.0), axis=0, keepdims=True)
            n_used = jnp.sum(tiles, axis=0, keepdims=True)
            padded = tiles * tm
            plan_ref[...] = _rows8([as_row(first), as_row(tiles), as_row(starts + cnt),
                                    as_row(padded - cnt), n_used], LANES).astype(jnp.int32)


def _router(logits_t):
    n = logits_t.shape[1]
    T = ROUTER_TILE
    nt = n // T
    return pl.pallas_call(
        _router_kernel,
        out_shape=(jax.ShapeDtypeStruct((n, ROUTER_LANES), F32),
                   jax.ShapeDtypeStruct((1, n), jnp.int32),
                   jax.ShapeDtypeStruct((1, n), jnp.int32),
                   jax.ShapeDtypeStruct((SUBLANES, LANES), jnp.int32)),
        grid=(2, nt),
        in_specs=[pl.BlockSpec((ROUTER_ROWS, T), lambda p, j: (0, j * (1 - p) + (nt - 1) * p))],
        out_specs=(pl.BlockSpec((T, ROUTER_LANES), lambda p, j: (j * p, 0)),
                   pl.BlockSpec((1, T), lambda p, j: (0, j * p)),
                   pl.BlockSpec((1, T), lambda p, j: (0, j * p)),
                   pl.BlockSpec((SUBLANES, LANES), lambda p, j: (0, 0))),
        scratch_shapes=[pltpu.VMEM((nt, SUBLANES, T), F32),
                        pltpu.VMEM((N_EXPERTS, LANES), F32)],
        compiler_params=pltpu.CompilerParams(
            dimension_semantics=("arbitrary", "arbitrary"), vmem_limit_bytes=VMEM_LIMIT_BYTES),
        name="router",
    )(logits_t)


DISPATCH_TILE = 512
ROW_DMA_UNROLL = 8
PAD_UNITS = tuple(1 << b for b in reversed(range(EXPERT_TILE.bit_length() - 1)))


DISPATCH_LAG = 2
DISPATCH_SLOTS = DISPATCH_LAG + 2


def _dispatch_kernel(d0_ref, d1_ref, plan_ref, h2_hbm, xs_hbm,
                     stage, zbuf, isem, ssem, zsem):
    dt = DISPATCH_TILE
    ps = SUBLANES
    i = pl.program_id(0)
    nsteps = pl.num_programs(0)
    zrows = PAD_UNITS[0]

    def in_copy(blk, slot):
        src = h2_hbm.at[pl.ds(pl.multiple_of(blk * (dt * ps), dt * ps), dt * ps), :]
        return pltpu.make_async_copy(src, stage.at[slot], isem.at[slot])

    def wait_rows(slot):
        for _ in range(2):
            pltpu.make_async_copy(stage.at[slot], xs_hbm.at[pl.ds(0, dt * ps), :],
                                  ssem.at[slot]).wait()

    def pad_copy(start, unit):
        return pltpu.make_async_copy(zbuf.at[pl.ds(0, unit * ps), :],
                                     xs_hbm.at[pl.ds(pl.multiple_of(start * ps, ps), unit * ps), :],
                                     zsem)

    def pad_pass(do):
        def per_expert(e, carry):
            start = plan_ref[PLAN_PAD_START, e]
            npad = plan_ref[PLAN_PAD_N, e]
            for unit in PAD_UNITS:
                @pl.when((npad & unit) != 0)
                def _():
                    do(pad_copy(start + (npad & ~(2 * unit - 1)), unit))
            return carry
        lax.fori_loop(0, N_EXPERTS, per_expert, 0)

    def tail_pass(do):
        def per_unit(k, carry):
            do(pad_copy(k * zrows, zrows))
            return carry
        per_tile = EXPERT_TILE // zrows
        n_units = xs_hbm.shape[0] // (zrows * ps)
        lax.fori_loop(plan_ref[PLAN_N_USED, 0] * per_tile, n_units, per_unit, 0)

    slot = i % DISPATCH_SLOTS

    @pl.when(i == 0)
    def _():
        in_copy(0, 0).start()

        @pl.when(nsteps > 1)
        def _():
            in_copy(1, 1).start()

        zbuf[...] = jnp.zeros_like(zbuf)
        pad_pass(lambda cp: cp.start())
        tail_pass(lambda cp: cp.start())

    @pl.when(i >= DISPATCH_LAG)
    def _():
        wait_rows((i - DISPATCH_LAG) % DISPATCH_SLOTS)

    @pl.when(i + 2 < nsteps)
    def _():
        in_copy(i + 2, (i + 2) % DISPATCH_SLOTS).start()

    in_copy(i, slot).wait()
    base = i * dt
    src_ref = stage.at[slot]

    def body(r, carry):
        src = _token_rows(src_ref, r)
        for prio, d_ref in enumerate((d0_ref, d1_ref)):
            pltpu.make_async_copy(src, _token_rows(xs_hbm, d_ref[base + r]),
                                  ssem.at[slot]).start(priority=prio)
        return carry

    lax.fori_loop(0, dt, body, 0, unroll=ROW_DMA_UNROLL)

    @pl.when(i == nsteps - 1)
    def _():
        for back in range(DISPATCH_LAG - 1, -1, -1):
            @pl.when(i - back >= 0)
            def _():
                wait_rows((i - back) % DISPATCH_SLOTS)
        pad_pass(lambda cp: cp.wait())
        tail_pass(lambda cp: cp.wait())


def _dispatch(dest0, dest1, plan, h2t, p_rows):
    n = dest0.shape[0]
    dt = DISPATCH_TILE
    grid_spec = pltpu.PrefetchScalarGridSpec(
        num_scalar_prefetch=3,
        grid=(n // dt,),
        in_specs=[pl.BlockSpec(memory_space=pl.ANY)],
        out_specs=pl.BlockSpec(memory_space=pl.ANY),
        scratch_shapes=[pltpu.VMEM((DISPATCH_SLOTS, dt * SUBLANES, LANES), F32),
                        pltpu.VMEM((PAD_UNITS[0] * SUBLANES, LANES), F32),
                        pltpu.SemaphoreType.DMA((DISPATCH_SLOTS,)),
                        pltpu.SemaphoreType.DMA((DISPATCH_SLOTS,)),
                        pltpu.SemaphoreType.DMA(())],
    )
    return pl.pallas_call(
        _dispatch_kernel,
        out_shape=jax.ShapeDtypeStruct((p_rows * SUBLANES, LANES), F32),
        grid_spec=grid_spec,
        compiler_params=pltpu.CompilerParams(dimension_semantics=("arbitrary",)),
        name="dispatch",
    )(dest0, dest1, plan, h2t)


EXPERT_IN_SLOTS = 4
EXPERT_OUT_SLOTS = 3
EXPERT_W_SLOTS = 3
MXU_COLS = 256


def _expert_kernel(plan_ref, xs_hbm, wg_hbm, wu_hbm, wd_hbm, y_hbm,
                   xbuf, ybuf, wgf, wuf, wdf, wgb, wub, wdb, isem, osem, wsem):
    tm = EXPERT_TILE
    rows = tm * SUBLANES
    ni, no, nw = EXPERT_IN_SLOTS, EXPERT_OUT_SLOTS, EXPERT_W_SLOTS
    e = pl.program_id(0)
    n_exp = pl.num_programs(0)
    n_used = plan_ref[PLAN_N_USED, 0]
    n_mine = plan_ref[PLAN_N_TILE, e]

    def w_copies(ex):
        slot = ex % nw
        return [pltpu.make_async_copy(hbm.at[ex], buf.at[slot], wsem.at[slot])
                for hbm, buf in ((wg_hbm, wgf), (wu_hbm, wuf), (wd_hbm, wdf))]

    @pl.when(e == 0)
    def _():
        for e0 in range(nw - 1):
            @pl.when(e0 < n_exp)
            def _():
                for cp in w_copies(e0):
                    cp.start()

    @pl.when(e + nw - 1 < n_exp)
    def _():
        for cp in w_copies(e + nw - 1):
            cp.start()

    for cp in w_copies(e):
        cp.wait()
    wslot = e % nw

    def tile_rows(ref, g):
        return ref.at[pl.ds(pl.multiple_of(g * rows, rows), rows), :]

    def in_copy(g):
        return pltpu.make_async_copy(tile_rows(xs_hbm, g), xbuf.at[g % ni], isem.at[g % ni])

    def out_copy(g):
        return pltpu.make_async_copy(ybuf.at[g % no], tile_rows(y_hbm, g), osem.at[g % no])

    @pl.when(e == 0)
    def _():
        for g0 in range(ni - 1):
            @pl.when(g0 < n_used)
            def _():
                in_copy(g0).start()

    def tile(g, carry, cast_weights=False):
        in_copy(g).wait()

        @pl.when(g >= no)
        def _():
            out_copy(g - no).wait()

        @pl.when(g + ni - 1 < n_used)
        def _():
            in_copy(g + ni - 1).start()

        if cast_weights:
            wgb[...] = wgf[wslot].astype(BF16)
            wub[...] = wuf[wslot].astype(BF16)
            wdb[...] = wdf[wslot].astype(BF16)
        xb = _load_token_rows(xbuf.at[g % ni], tm).astype(BF16)
        a = jnp.dot(xb, wgb[...], preferred_element_type=F32)
        u = jnp.dot(xb, wub[...], preferred_element_type=F32)
        hid = (_silu(a) * u).astype(BF16)
        out = ybuf.at[g % no]
        per_piece = MXU_COLS // LANES
        for p in range(wdb.shape[1] // MXU_COLS):
            y = jnp.dot(hid, wdb[:, p * MXU_COLS:(p + 1) * MXU_COLS], preferred_element_type=F32)
            for q in range(per_piece):
                out[pl.ds(p * per_piece + q, tm, stride=SUBLANES), :] = y[:, q * LANES:(q + 1) * LANES]
        out_copy(g).start()
        return carry

    first = plan_ref[PLAN_FIRST_TILE, e]

    @pl.when(n_mine > 0)
    def _():
        tile(first, 0, cast_weights=True)

    lax.fori_loop(first + 1, first + n_mine, tile, 0)

    @pl.when(e == pl.num_programs(0) - 1)
    def _():
        for back in range(no, 0, -1):
            @pl.when(n_used >= back)
            def _():
                out_copy(n_used - back).wait()


def _experts(plan, xs, wg, wu, wd):
    tm = EXPERT_TILE
    n_exp, d, de = wg.shape
    grid_spec = pltpu.PrefetchScalarGridSpec(
        num_scalar_prefetch=1,
        grid=(n_exp,),
        in_specs=[
            pl.BlockSpec(memory_space=pl.ANY),
            pl.BlockSpec(memory_space=pl.ANY),
            pl.BlockSpec(memory_space=pl.ANY),
            pl.BlockSpec(memory_space=pl.ANY),
        ],
        out_specs=pl.BlockSpec(memory_space=pl.ANY),
        scratch_shapes=[pltpu.VMEM((EXPERT_IN_SLOTS, tm * SUBLANES, LANES), F32),
                        pltpu.VMEM((EXPERT_OUT_SLOTS, tm * SUBLANES, LANES), F32),
                        pltpu.VMEM((EXPERT_W_SLOTS, d, de), F32),
                        pltpu.VMEM((EXPERT_W_SLOTS, d, de), F32),
                        pltpu.VMEM((EXPERT_W_SLOTS, de, d), F32),
                        pltpu.VMEM((d, de), BF16),
                        pltpu.VMEM((d, de), BF16),
                        pltpu.VMEM((de, d), BF16),
                        pltpu.SemaphoreType.DMA((EXPERT_IN_SLOTS,)),
                        pltpu.SemaphoreType.DMA((EXPERT_OUT_SLOTS,)),
                        pltpu.SemaphoreType.DMA((EXPERT_W_SLOTS,))],
    )
    return pl.pallas_call(
        _expert_kernel,
        out_shape=jax.ShapeDtypeStruct(xs.shape, F32),
        input_output_aliases={1: 0},
        grid_spec=grid_spec,
        compiler_params=pltpu.CompilerParams(
            dimension_semantics=("arbitrary",),
            vmem_limit_bytes=VMEM_LIMIT_BYTES),
        name="experts",
    )(plan, xs, wg, wu, wd)


COMBINE_SLOTS = 3


def _combine_kernel(p0_ref, p1_ref, y_hbm, x1_ref, slab_ref, mod_ref, gfin_ref, o_ref,
                    *scratch):
    tm = COMBINE_TILE
    ns = COMBINE_SLOTS
    ahead = ns - 1
    ybufs, sem = scratch[:ns], scratch[ns]
    i = pl.program_id(0)
    nt = pl.num_programs(0)

    def row_copies(tile, slot, r):
        for j, p_ref in enumerate((p0_ref, p1_ref)):
            pltpu.make_async_copy(_token_rows(y_hbm, p_ref[tile * tm + r]),
                                  _token_rows(ybufs[slot].at[j], r),
                                  sem.at[slot]).start(priority=j)

    def wait_tile(slot):
        for j in range(2):
            pltpu.make_async_copy(y_hbm.at[pl.ds(0, tm * SUBLANES), :], ybufs[slot].at[j],
                                  sem.at[slot]).wait()

    @pl.when(i == 0)
    def _():
        for t0 in range(ahead):
            @pl.when(t0 < nt)
            def _():
                def body(r, carry):
                    row_copies(t0, t0, r)
                    return carry
                lax.fori_loop(0, tm, body, 0, unroll=ROW_DMA_UNROLL)

    n_chunks = SUBLANES
    batch = tm // n_chunks

    def step(slot):
        wait_tile(slot)
        nxt = jnp.minimum(i + ahead, nt - 1)
        nslot = (slot + ahead) % ns
        slab = slab_ref[...]
        w0 = slab[:, 2:3]
        w1 = slab[:, 3:4]
        gate_f = mod_ref[0][5:6]
        sq = jnp.zeros((tm, LANES), F32)
        for c in range(n_chunks):
            lanes = slice(c * LANES, (c + 1) * LANES)
            y0 = ybufs[slot].at[0][pl.ds(c, tm, stride=SUBLANES), :]
            y1 = ybufs[slot].at[1][pl.ds(c, tm, stride=SUBLANES), :]
            xo = x1_ref[:, lanes] + gate_f[:, lanes] * (w0 * y0 + w1 * y1)
            sq = sq + xo * xo
            o_ref[:, lanes] = xo
            for r in range(c * batch, (c + 1) * batch):
                row_copies(nxt, nslot, r)
        ms = jnp.sum(sq, axis=-1, keepdims=True) * (1.0 / (n_chunks * LANES))
        scale = lax.rsqrt(ms + EPS)
        for c in range(n_chunks):
            lanes = slice(c * LANES, (c + 1) * LANES)
            o_ref[:, lanes] = o_ref[:, lanes] * scale * gfin_ref[:, lanes]

    for slot in range(ns):
        @pl.when(i % ns == slot)
        def _():
            step(slot)

    @pl.when(i == nt - 1)
    def _():
        for k in range(ahead):
            for slot in range(ns):
                @pl.when((nt - 1 - k >= 0) & ((nt - 1 - k + ahead) % ns == slot))
                def _():
                    wait_tile(slot)


def _combine(p0, p1, y, x1, slab, mod, gfin, seq):
    n, d = x1.shape
    tm = COMBINE_TILE
    tiles_per_seq = seq // tm
    grid_spec = pltpu.PrefetchScalarGridSpec(
        num_scalar_prefetch=2,
        grid=(n // tm,),
        in_specs=[
            pl.BlockSpec(memory_space=pl.ANY),
            pl.BlockSpec((tm, d), lambda i, a, b: (i, 0)),
            pl.BlockSpec((tm, ROUTER_LANES), lambda i, a, b: (i, 0)),
            pl.BlockSpec((1, 6, d), lambda i, a, b: (i // tiles_per_seq, 0, 0)),
            pl.BlockSpec((1, d), lambda i, a, b: (0, 0)),
        ],
        out_specs=pl.BlockSpec((tm, d), lambda i, a, b: (i, 0)),
        scratch_shapes=([pltpu.VMEM((2, tm * SUBLANES, LANES), F32)] * COMBINE_SLOTS
                        + [pltpu.SemaphoreType.DMA((COMBINE_SLOTS,))]),
    )
    return pl.pallas_call(
        _combine_kernel,
        out_shape=jax.ShapeDtypeStruct((n, d), F32),
        grid_spec=grid_spec,
        compiler_params=pltpu.CompilerParams(
            dimension_semantics=("arbitrary",),
            vmem_limit_bytes=VMEM_LIMIT_BYTES),
        name="combine",
    )(p0, p1, y, x1, slab, mod, gfin)


def kernel(x, c, positions, ada_w, ada_b, norm_mix_g, norm_ffn_g, w_in, conv_w, conv_b,
           beta_ret, beta_conv, w_out, router_group_w, router_group_b, router_expert_w,
           router_expert_b, expert_w_gate, expert_w_up, expert_w_down, norm_final_g):
    bsz, seq, d = x.shape
    n = bsz * seq
    depth = ada_w.shape[0]
    assert depth == 1, "the combine kernel fuses the trunk's final RMSNorm (single layer)"
    W = RET_HEADS * HEAD_DIM

    l = 0
    mod, cos128, sin128 = _prologue(c, ada_w[l], ada_b[l], positions)
    mod = mod.reshape(bsz, 6, d)
    heads = jnp.arange(RET_HEADS, dtype=F32)
    lg = jnp.log1p(-jnp.exp2(-5.0 - heads))
    lgl = jnp.repeat(lg, HEAD_DIM).reshape(1, W)
    assert CONV_GROUP_DIM == HEAD_DIM, "conv groups and retention heads share the 64-lane block sums"
    blk_np = np.kron(np.eye(RET_HEADS // 2, dtype=np.float32),
                     np.ones((HEAD_DIM, HEAD_DIM), np.float32))
    blk = jnp.asarray(blk_np, dtype=BF16)

    gap = ROUTER_EXPERT_ROW0 - N_GROUPS
    tail = ROUTER_LANES - ROUTER_ROWS
    wr = jnp.concatenate([router_group_w[l].T, jnp.zeros((gap, d), F32),
                          router_expert_w[l].T, jnp.zeros((tail, d), F32)], axis=0)
    wr2 = jnp.concatenate(_split_bf16(wr), axis=0)
    br = jnp.concatenate([router_group_b[l], jnp.zeros((gap,), F32),
                          router_expert_b[l], jnp.zeros((tail,), F32)]).reshape(ROUTER_LANES, 1)

    x1, h2t, logits_t = _mixer(
        x, mod, cos128, sin128, norm_mix_g[l].reshape(1, d), w_in[l],
        conv_w[l], conv_b[l].reshape(1, W), beta_ret[l].reshape(1, W),
        beta_conv[l].reshape(1, W), w_out[l], norm_ffn_g[l].reshape(1, d),
        wr2, br, lg, lgl, blk)
    slab, p0, p1, plan = _router(logits_t)
    p0, p1 = p0.reshape(n), p1.reshape(n)
    p_rows = 2 * n + N_EXPERTS * EXPERT_TILE

    xs = _dispatch(p0, p1, plan, h2t, p_rows)
    de = expert_w_gate.shape[-1]
    y = _experts(plan, xs,
                 expert_w_gate[l].reshape(N_EXPERTS, d, de),
                 expert_w_up[l].reshape(N_EXPERTS, d, de),
                 expert_w_down[l].reshape(N_EXPERTS, de, d))
    out = _combine(p0, p1, y, x1.reshape(n, d), slab, mod, norm_final_g.reshape(1, d), seq)
    return out.reshape(bsz, seq, d)
```
